```python
import math
import jax, jax.numpy as jnp
from jax import lax
import numpy as np

D_MODEL = 1024
BATCH = 16
SEQ = 2048
DEPTH = 1

HEAD_DIM = 64
N_HEADS_FOX = 8
N_HEADS_SB = 8
D_FOX = N_HEADS_FOX * HEAD_DIM
D_SB = N_HEADS_SB * HEAD_DIM
D_FF = 4 * D_MODEL
D_PLE = 256
Q_BLOCK = 128
EPS = 1e-6
IN_SIZES = (D_FOX, D_FOX, D_FOX, N_HEADS_FOX, D_SB, D_SB, D_SB, D_MODEL, D_MODEL)
D_IN = 3 * D_FOX + N_HEADS_FOX + 3 * D_SB + 2 * D_MODEL

kernel_name = "fox_stickbreaking_gated_hybrid_block"


def _rmsnorm(x, g):
    xf = x.astype(jnp.float32)
    r = lax.rsqrt(jnp.mean(xf * xf, axis=-1, keepdims=True) + EPS)
    return (xf * r * g.astype(jnp.float32)).astype(x.dtype)


def _split_cols(u):
    offs = []
    o = 0
    for s in IN_SIZES[:-1]:
        o += s
        offs.append(o)
    return jnp.split(u, offs, axis=-1)


def _heads(u, n_heads):
    b, s, _ = u.shape
    return u.reshape(b, s, n_heads, HEAD_DIM).transpose(0, 2, 1, 3).astype(jnp.float32)


def _merge_heads(o, dtype):
    b, h, s, d = o.shape
    return o.transpose(0, 2, 1, 3).reshape(b, s, h * d).astype(dtype)


def _forgetting_attention(q, k, v, log_f):
    s_len = q.shape[2]
    scale = HEAD_DIM ** -0.5
    c = jnp.cumsum(log_f, axis=-1)
    outs = []
    for blk in range(s_len // Q_BLOCK):
        q0 = blk * Q_BLOCK
        k_end = q0 + Q_BLOCK
        qb = q[:, :, q0:k_end]
        kb = k[:, :, :k_end]
        vb = v[:, :, :k_end]
        logits = (jnp.einsum('bhqd,bhkd->bhqk', qb, kb) * scale
                  + c[:, :, q0:k_end, None] - c[:, :, None, :k_end])
        q_pos = q0 + jnp.arange(Q_BLOCK)
        k_pos = jnp.arange(k_end)
        causal = k_pos[None, :] <= q_pos[:, None]
        logits = jnp.where(causal, logits, -jnp.inf)
        probs = jax.nn.softmax(logits, axis=-1)
        outs.append(jnp.einsum('bhqk,bhkd->bhqd', probs, vb))
    return jnp.concatenate(outs, axis=2)


def _stick_breaking_attention(q, k, v):
    s_len = q.shape[2]
    scale = HEAD_DIM ** -0.5
    outs = []
    for blk in range(s_len // Q_BLOCK):
        q0 = blk * Q_BLOCK
        k_end = q0 + Q_BLOCK
        qb = q[:, :, q0:k_end]
        kb = k[:, :, :k_end]
        vb = v[:, :, :k_end]
        z = jnp.einsum('bhqd,bhkd->bhqk', qb, kb) * scale
        q_pos = q0 + jnp.arange(Q_BLOCK)
        k_pos = jnp.arange(k_end)
        strict = k_pos[None, :] < q_pos[:, None]
        log_1m_beta = jnp.where(strict, jax.nn.log_sigmoid(-z), 0.0)
        tail = lax.cumsum(log_1m_beta, axis=3, reverse=True) - log_1m_beta
        weights = jnp.where(strict, jnp.exp(jax.nn.log_sigmoid(z) + tail), 0.0)
        outs.append(jnp.einsum('bhqk,bhkd->bhqd', weights, vb))
    return jnp.concatenate(outs, axis=2)


def _fwd_setup_inputs(seed: int = 0) -> dict:
    key = jax.random.key(seed)
    ks = jax.random.split(key, 16)
    f32 = jnp.float32

    def w(k, shape, fan_in, gain=1.0):
        return jax.random.normal(k, shape, f32) * (gain * fan_in ** -0.5)

    def gain(k, shape):
        return 1.0 + 0.02 * jax.random.normal(k, shape, f32)

    return {
        "x": jax.random.normal(ks[0], (BATCH, SEQ, D_MODEL), f32),
        "p": jax.random.normal(ks[1], (DEPTH, BATCH, SEQ, D_PLE), f32),
        "g_mix": gain(ks[2], (DEPTH, D_MODEL)),
        "w_in": w(ks[3], (DEPTH, D_MODEL, D_IN), D_MODEL),
        "b_forget": jax.random.uniform(ks[4], (DEPTH, N_HEADS_FOX), f32, 1.0, 4.0),
        "b_gate": 0.01 * jax.random.normal(ks[5], (DEPTH, 2, D_MODEL), f32),
        "w_branch_fox": w(ks[6], (DEPTH, D_FOX, D_MODEL), D_FOX),
        "w_branch_sb": w(ks[7], (DEPTH, D_SB, D_MODEL), D_SB),
        "w_out": w(ks[8], (DEPTH, D_MODEL, D_MODEL), D_MODEL),
        "g_mlp": gain(ks[9], (DEPTH, D_MODEL)),
        "w_up": w(ks[10], (DEPTH, D_MODEL, D_FF), D_MODEL),
        "w_down": w(ks[11], (DEPTH, D_FF, D_MODEL), D_FF, gain=0.5),
        "g_ple": gain(ks[12], (DEPTH, D_MODEL)),
        "w_ple_gate": w(ks[13], (DEPTH, D_MODEL, D_MODEL), D_MODEL),
        "w_ple": w(ks[14], (DEPTH, D_PLE, D_MODEL), D_PLE),
        "g_final": gain(ks[15], (D_MODEL,)),
    }


def _fwd_reference(x, p, g_mix, w_in, b_forget, b_gate, w_branch_fox, w_branch_sb, w_out,
              g_mlp, w_up, w_down, g_ple, w_ple_gate, w_ple, g_final):
    b, s, _ = x.shape
    for i in range(DEPTH):
        h = _rmsnorm(x, g_mix[i])
        u = h @ w_in[i]
        q_a, k_a, v_a, f_a, q_b, k_b, v_b, gl_a, gl_b = _split_cols(u)
        log_f = jax.nn.log_sigmoid((f_a + b_forget[i]).astype(jnp.float32)).transpose(0, 2, 1)
        o_fox = _forgetting_attention(_heads(q_a, N_HEADS_FOX), _heads(k_a, N_HEADS_FOX),
                                      _heads(v_a, N_HEADS_FOX), log_f)
        o_sb = _stick_breaking_attention(_heads(q_b, N_HEADS_SB), _heads(k_b, N_HEADS_SB),
                                         _heads(v_b, N_HEADS_SB))
        o_fox = _merge_heads(o_fox, x.dtype) @ w_branch_fox[i]
        o_sb = _merge_heads(o_sb, x.dtype) @ w_branch_sb[i]
        merged = (jax.nn.sigmoid(gl_a + b_gate[i, 0]) * o_fox
                  + jax.nn.sigmoid(gl_b + b_gate[i, 1]) * o_sb)
        x = x + merged @ w_out[i]
        h = _rmsnorm(x, g_mlp[i])
        x = x + jnp.square(jax.nn.relu(h @ w_up[i])) @ w_down[i]
        h = _rmsnorm(x, g_ple[i])
        x = x + jax.nn.sigmoid(h @ w_ple_gate[i]) * (p[i] @ w_ple[i])
    return _rmsnorm(x, g_final)


import jax as _jax
import jax.numpy as _jnp

TWIN_FORMAT = 'train_step'
FWD_PARAMS = ['x', 'p', 'g_mix', 'w_in', 'b_forget', 'b_gate', 'w_branch_fox', 'w_branch_sb', 'w_out', 'g_mlp', 'w_up', 'w_down', 'g_ple', 'w_ple_gate', 'w_ple', 'g_final']
TWIN_WEIGHTS = ['g_mix', 'w_in', 'b_forget', 'b_gate', 'w_branch_fox', 'w_branch_sb', 'w_out', 'g_mlp', 'w_up', 'w_down', 'g_ple', 'w_ple_gate', 'w_ple', 'g_final']
TWIN_DIFF_INPUT = 'x'
TWIN_INPUTS = ['x', 'p', 'g_mix', 'w_in', 'b_forget', 'b_gate', 'w_branch_fox', 'w_branch_sb', 'w_out', 'g_mlp', 'w_up', 'w_down', 'g_ple', 'w_ple_gate', 'w_ple', 'g_final', 'loss_target', 'm_g_mix', 'm_w_in', 'm_b_forget', 'm_b_gate', 'm_w_branch_fox', 'm_w_branch_sb', 'm_w_out', 'm_g_mlp', 'm_w_up', 'm_w_down', 'm_g_ple', 'm_w_ple_gate', 'm_w_ple', 'm_g_final', 'v_g_mix', 'v_w_in', 'v_b_forget', 'v_b_gate', 'v_w_branch_fox', 'v_w_branch_sb', 'v_w_out', 'v_g_mlp', 'v_w_up', 'v_w_down', 'v_g_ple', 'v_w_ple_gate', 'v_w_ple', 'v_g_final']
TWIN_OUTPUTS = ['loss', 'grad_x', 'grad_g_mix', 'grad_w_in', 'grad_b_forget', 'grad_b_gate', 'grad_w_branch_fox', 'grad_w_branch_sb', 'grad_w_out', 'grad_g_mlp', 'grad_w_up', 'grad_w_down', 'grad_g_ple', 'grad_w_ple_gate', 'grad_w_ple', 'grad_g_final', 'delta_g_mix', 'delta_w_in', 'delta_b_forget', 'delta_b_gate', 'delta_w_branch_fox', 'delta_w_branch_sb', 'delta_w_out', 'delta_g_mlp', 'delta_w_up', 'delta_w_down', 'delta_g_ple', 'delta_w_ple_gate', 'delta_w_ple', 'delta_g_final', 'new_m_g_mix', 'new_m_w_in', 'new_m_b_forget', 'new_m_b_gate', 'new_m_w_branch_fox', 'new_m_w_branch_sb', 'new_m_w_out', 'new_m_g_mlp', 'new_m_w_up', 'new_m_w_down', 'new_m_g_ple', 'new_m_w_ple_gate', 'new_m_w_ple', 'new_m_g_final', 'new_v_g_mix', 'new_v_w_in', 'new_v_b_forget', 'new_v_b_gate', 'new_v_w_branch_fox', 'new_v_w_branch_sb', 'new_v_w_out', 'new_v_g_mlp', 'new_v_w_up', 'new_v_w_down', 'new_v_g_ple', 'new_v_w_ple_gate', 'new_v_w_ple', 'new_v_g_final']
TWIN_LEAF_KINDS = {'loss': 'loss', 'grad_x': 'grad_x', 'grad_g_mix': 'grad_w', 'grad_w_in': 'grad_w', 'grad_b_forget': 'grad_w', 'grad_b_gate': 'grad_w', 'grad_w_branch_fox': 'grad_w', 'grad_w_branch_sb': 'grad_w', 'grad_w_out': 'grad_w', 'grad_g_mlp': 'grad_w', 'grad_w_up': 'grad_w', 'grad_w_down': 'grad_w', 'grad_g_ple': 'grad_w', 'grad_w_ple_gate': 'grad_w', 'grad_w_ple': 'grad_w', 'grad_g_final': 'grad_w', 'delta_g_mix': 'delta_w', 'delta_w_in': 'delta_w', 'delta_b_forget': 'delta_w', 'delta_b_gate': 'delta_w', 'delta_w_branch_fox': 'delta_w', 'delta_w_branch_sb': 'delta_w', 'delta_w_out': 'delta_w', 'delta_g_mlp': 'delta_w', 'delta_w_up': 'delta_w', 'delta_w_down': 'delta_w', 'delta_g_ple': 'delta_w', 'delta_w_ple_gate': 'delta_w', 'delta_w_ple': 'delta_w', 'delta_g_final': 'delta_w', 'new_m_g_mix': 'new_m', 'new_m_w_in': 'new_m', 'new_m_b_forget': 'new_m', 'new_m_b_gate': 'new_m', 'new_m_w_branch_fox': 'new_m', 'new_m_w_branch_sb': 'new_m', 'new_m_w_out': 'new_m', 'new_m_g_mlp': 'new_m', 'new_m_w_up': 'new_m', 'new_m_w_down': 'new_m', 'new_m_g_ple': 'new_m', 'new_m_w_ple_gate': 'new_m', 'new_m_w_ple': 'new_m', 'new_m_g_final': 'new_m', 'new_v_g_mix': 'new_v', 'new_v_w_in': 'new_v', 'new_v_b_forget': 'new_v', 'new_v_b_gate': 'new_v', 'new_v_w_branch_fox': 'new_v', 'new_v_w_branch_sb': 'new_v', 'new_v_w_out': 'new_v', 'new_v_g_mlp': 'new_v', 'new_v_w_up': 'new_v', 'new_v_w_down': 'new_v', 'new_v_g_ple': 'new_v', 'new_v_w_ple_gate': 'new_v', 'new_v_w_ple': 'new_v', 'new_v_g_final': 'new_v'}


def _forward(args):
    return _fwd_reference(*[args[k] for k in FWD_PARAMS])


def _output_shape():
    out = _jax.eval_shape(lambda: _forward(_fwd_setup_inputs(0)))
    return out.shape, out.dtype

N_MICROBATCH = 1
ADAM_LR = 0.001
ADAM_B1 = 0.9
ADAM_B2 = 0.999
ADAM_EPS = 1e-08
ADAM_WD = 0.01
ADAM_STEP = 10
PER_EXAMPLE_BATCH_AXIS = {'x': 0, 'p': 1, 'loss_target': 0}
SHARED_INPUTS = []
_WEIGHT_DTYPES = {'g_mix': _jnp.float32, 'w_in': _jnp.float32, 'b_forget': _jnp.float32, 'b_gate': _jnp.float32, 'w_branch_fox': _jnp.float32, 'w_branch_sb': _jnp.float32, 'w_out': _jnp.float32, 'g_mlp': _jnp.float32, 'w_up': _jnp.float32, 'w_down': _jnp.float32, 'g_ple': _jnp.float32, 'w_ple_gate': _jnp.float32, 'w_ple': _jnp.float32, 'g_final': _jnp.float32}
MOMENT_SCALE = {'g_mix': 9.144785e-02, 'w_in': 3.873723e-02, 'b_forget': 2.406353e-01, 'b_gate': 1.840463e-02, 'w_branch_fox': 3.464486e-02, 'w_branch_sb': 5.564526e-02, 'w_out': 6.495346e-02, 'g_mlp': 9.789510e-02, 'w_up': 4.692222e-02, 'w_down': 1.664584e-01, 'g_ple': 2.929458e-02, 'w_ple_gate': 2.887043e-02, 'w_ple': 7.106577e-02, 'g_final': 3.197863e+01}


def _to_microbatches(a, axis):
    t = _jnp.moveaxis(a, axis, 0)
    t = t.reshape((N_MICROBATCH, t.shape[0] // N_MICROBATCH) + t.shape[1:])
    return _jnp.moveaxis(t, 1, axis + 1)


def setup_inputs(seed: int = 0) -> dict:
    inp = _fwd_setup_inputs(seed)
    key = _jax.random.fold_in(_jax.random.key(seed), 7919)
    shape, _ = _output_shape()
    out = dict(inp)
    out["loss_target"] = _jax.random.normal(_jax.random.fold_in(key, 0), shape, _jnp.float32)
    for i, name in enumerate(TWIN_WEIGHTS):
        w = inp[name].astype(_jnp.float32)
        if MOMENT_SCALE is None:
            s = _jnp.sqrt(_jnp.mean(_jnp.square(w)) + 1e-30)
        else:
            s = MOMENT_SCALE[name]
        km, kv = _jax.random.split(_jax.random.fold_in(key, i + 1))
        out[name] = w
        out["m_" + name] = s * _jax.random.normal(km, w.shape, _jnp.float32)
        out["v_" + name] = (s * s) * _jax.random.uniform(kv, w.shape, _jnp.float32, 0.5, 1.5)
    if N_MICROBATCH > 1:
        for name, axis in PER_EXAMPLE_BATCH_AXIS.items():
            out[name] = _to_microbatches(out[name], axis)
    return {'x': out['x'], 'p': out['p'], 'g_mix': out['g_mix'], 'w_in': out['w_in'], 'b_forget': out['b_forget'], 'b_gate': out['b_gate'], 'w_branch_fox': out['w_branch_fox'], 'w_branch_sb': out['w_branch_sb'], 'w_out': out['w_out'], 'g_mlp': out['g_mlp'], 'w_up': out['w_up'], 'w_down': out['w_down'], 'g_ple': out['g_ple'], 'w_ple_gate': out['w_ple_gate'], 'w_ple': out['w_ple'], 'g_final': out['g_final'], 'loss_target': out['loss_target'], 'm_g_mix': out['m_g_mix'], 'm_w_in': out['m_w_in'], 'm_b_forget': out['m_b_forget'], 'm_b_gate': out['m_b_gate'], 'm_w_branch_fox': out['m_w_branch_fox'], 'm_w_branch_sb': out['m_w_branch_sb'], 'm_w_out': out['m_w_out'], 'm_g_mlp': out['m_g_mlp'], 'm_w_up': out['m_w_up'], 'm_w_down': out['m_w_down'], 'm_g_ple': out['m_g_ple'], 'm_w_ple_gate': out['m_w_ple_gate'], 'm_w_ple': out['m_w_ple'], 'm_g_final': out['m_g_final'], 'v_g_mix': out['v_g_mix'], 'v_w_in': out['v_w_in'], 'v_b_forget': out['v_b_forget'], 'v_b_gate': out['v_b_gate'], 'v_w_branch_fox': out['v_w_branch_fox'], 'v_w_branch_sb': out['v_w_branch_sb'], 'v_w_out': out['v_w_out'], 'v_g_mlp': out['v_g_mlp'], 'v_w_up': out['v_w_up'], 'v_w_down': out['v_w_down'], 'v_g_ple': out['v_g_ple'], 'v_w_ple_gate': out['v_w_ple_gate'], 'v_w_ple': out['v_w_ple'], 'v_g_final': out['v_g_final']}


def _loss(weights, diff, rest, loss_target):
    with _jax.named_scope("forward"):
        args = {**rest, TWIN_DIFF_INPUT: diff, **{k: w.astype(_WEIGHT_DTYPES[k]) for k, w in weights.items()}}
        y = _forward(args)
    with _jax.named_scope("loss_head"):
        err = _jnp.square(y.astype(_jnp.float32) - loss_target)
        return 0.5 * _jnp.sum(_jnp.mean(err, axis=-1)) if err.ndim else 0.5 * err


def _adamw(w, g, m, v):
    m = ADAM_B1 * m + (1.0 - ADAM_B1) * g
    v = ADAM_B2 * v + (1.0 - ADAM_B2) * _jnp.square(g)
    m_hat = m / (1.0 - ADAM_B1 ** ADAM_STEP)
    v_hat = v / (1.0 - ADAM_B2 ** ADAM_STEP)
    delta = -ADAM_LR * (m_hat / (_jnp.sqrt(v_hat) + ADAM_EPS) + ADAM_WD * w)
    return delta, m, v


def reference(x, p, g_mix, w_in, b_forget, b_gate, w_branch_fox, w_branch_sb, w_out, g_mlp, w_up, w_down, g_ple, w_ple_gate, w_ple, g_final, loss_target, m_g_mix, m_w_in, m_b_forget, m_b_gate, m_w_branch_fox, m_w_branch_sb, m_w_out, m_g_mlp, m_w_up, m_w_down, m_g_ple, m_w_ple_gate, m_w_ple, m_g_final, v_g_mix, v_w_in, v_b_forget, v_b_gate, v_w_branch_fox, v_w_branch_sb, v_w_out, v_g_mlp, v_w_up, v_w_down, v_g_ple, v_w_ple_gate, v_w_ple, v_g_final):
    given = dict(x=x, p=p, g_mix=g_mix, w_in=w_in, b_forget=b_forget, b_gate=b_gate, w_branch_fox=w_branch_fox, w_branch_sb=w_branch_sb, w_out=w_out, g_mlp=g_mlp, w_up=w_up, w_down=w_down, g_ple=g_ple, w_ple_gate=w_ple_gate, w_ple=w_ple, g_final=g_final, loss_target=loss_target, m_g_mix=m_g_mix, m_w_in=m_w_in, m_b_forget=m_b_forget, m_b_gate=m_b_gate, m_w_branch_fox=m_w_branch_fox, m_w_branch_sb=m_w_branch_sb, m_w_out=m_w_out, m_g_mlp=m_g_mlp, m_w_up=m_w_up, m_w_down=m_w_down, m_g_ple=m_g_ple, m_w_ple_gate=m_w_ple_gate, m_w_ple=m_w_ple, m_g_final=m_g_final, v_g_mix=v_g_mix, v_w_in=v_w_in, v_b_forget=v_b_forget, v_b_gate=v_b_gate, v_w_branch_fox=v_w_branch_fox, v_w_branch_sb=v_w_branch_sb, v_w_out=v_w_out, v_g_mlp=v_g_mlp, v_w_up=v_w_up, v_w_down=v_w_down, v_g_ple=v_g_ple, v_w_ple_gate=v_w_ple_gate, v_w_ple=v_w_ple, v_g_final=v_g_final)
    weights = {n: given[n] for n in TWIN_WEIGHTS}
    shared = {n: given[n] for n in SHARED_INPUTS}
    per_example = {n: given[n] for n in ['x', 'p']}
    grad_fn = _jax.value_and_grad(_loss, argnums=(0, 1))

    def one_microbatch(ex, loss_target):
        ex = dict(ex)
        diff = ex.pop(TWIN_DIFF_INPUT)
        return grad_fn(weights, diff, {**shared, **ex}, loss_target)

    if N_MICROBATCH == 1:
        loss, (grad_w, grad_x) = one_microbatch(per_example, given["loss_target"])
    else:
        def body(carry, xs):
            loss_sum, grad_sum = carry
            l_k, (gw_k, gx_k) = one_microbatch(xs[0], xs[1])
            with _jax.named_scope("update"):
                return (loss_sum + l_k, _jax.tree.map(_jnp.add, grad_sum, gw_k)), gx_k

        init = (_jnp.zeros((), _jnp.float32), _jax.tree.map(_jnp.zeros_like, weights))
        (loss, grad_w), grad_x = _jax.lax.scan(body, init, (per_example, given["loss_target"]))
    with _jax.named_scope("update"):
        delta_w, new_m, new_v = {}, {}, {}
        for n in TWIN_WEIGHTS:
            delta_w[n], new_m[n], new_v[n] = _adamw(weights[n], grad_w[n], given["m_" + n], given["v_" + n])
    return (loss, grad_x, *[grad_w[n] for n in TWIN_WEIGHTS], *[delta_w[n] for n in TWIN_WEIGHTS],
            *[new_m[n] for n in TWIN_WEIGHTS], *[new_v[n] for n in TWIN_WEIGHTS])
```

```python
import functools

import jax
import jax.numpy as jnp
from jax import lax
from jax.experimental import pallas as pl
from jax.experimental.pallas import tpu as pltpu

F32 = jnp.float32
BF16 = jnp.bfloat16

D_MODEL = 1024
HEAD_DIM = 64
N_HEADS = 8
D_ATT = N_HEADS * HEAD_DIM
D_FF = 4 * D_MODEL
D_PLE = 256
D_IN = 6 * D_ATT + N_HEADS + 2 * D_MODEL
F_PAD = 128
EPS = 1e-6
SCALE = HEAD_DIM ** -0.5
N_CHIPS = 4
LANES = 128
ATT_BLOCK = 256
NEG = -1e30

ADAM_LR = 0.001
ADAM_B1 = 0.9
ADAM_B2 = 0.999
ADAM_EPS = 1e-08
ADAM_WD = 0.01
ADAM_STEP = 10

VMEM_LIMIT = 56 * 1024 * 1024

PACK_COLS = 1024
PACK_ROWS = 4352
HALF_ROWS = PACK_ROWS // 2
MESH = pl.DeviceIdType.MESH


def _cparams(sem=None):
    return pltpu.CompilerParams(dimension_semantics=sem, vmem_limit_bytes=VMEM_LIMIT)


def _relu2(t):
    t = t.astype(F32)
    return t * t


_DIMS = {"nn": (((1,), (0,)), ((), ())), "nt": (((1,), (1,)), ((), ())), "tn": (((0,), (0,)), ((), ()))}
NT_DIMS = _DIMS["nt"]
TN_DIMS = _DIMS["tn"]


def _mm(a, b, *, mode, name, out_dtype=F32, tm=512, tn=512, tk=512, add=None, a_fn=None, epi=None, extra=None):
    if mode == "nn":
        (m, k), n = a.shape, b.shape[1]
    elif mode == "nt":
        (m, k), n = a.shape, b.shape[0]
    else:
        (k, m), n = a.shape, b.shape[1]
    tm, tn, tk = min(tm, m), min(tn, n), min(tk, k)
    assert m % tm == 0 and n % tn == 0 and k % tk == 0, (name, m, n, k)
    nk = k // tk
    a_spec = {"nn": pl.BlockSpec((tm, tk), lambda i, j, kk: (i, kk)),
              "nt": pl.BlockSpec((tm, tk), lambda i, j, kk: (i, kk)),
              "tn": pl.BlockSpec((tk, tm), lambda i, j, kk: (kk, i))}[mode]
    b_spec = {"nn": pl.BlockSpec((tk, tn), lambda i, j, kk: (kk, j)),
              "nt": pl.BlockSpec((tn, tk), lambda i, j, kk: (j, kk)),
              "tn": pl.BlockSpec((tk, tn), lambda i, j, kk: (kk, j))}[mode]
    o_spec = pl.BlockSpec((tm, tn), lambda i, j, kk: (i, j))
    operands, in_specs = [a, b], [a_spec, b_spec]
    third = add if add is not None else extra
    if third is not None:
        operands.append(third)
        in_specs.append(o_spec)

    def body(*refs):
        a_ref, b_ref = refs[0], refs[1]
        t_ref = refs[2] if third is not None else None
        o_ref = refs[3] if third is not None else refs[2]
        acc_ref = refs[-1] if nk > 1 else None
        at = a_ref[...]
        if a_fn is not None:
            at = a_fn(at)
        part = lax.dot_general(at.astype(BF16), b_ref[...].astype(BF16), _DIMS[mode], preferred_element_type=F32)

        def finish(acc):
            if epi is not None:
                acc = epi(acc, None if t_ref is None else t_ref[...])
            elif add is not None:
                acc = acc + t_ref[...].astype(F32)
            o_ref[...] = acc.astype(o_ref.dtype)

        if nk == 1:
            finish(part)
        else:
            kk = pl.program_id(2)

            @pl.when(kk == 0)
            def _():
                acc_ref[...] = part

            @pl.when(kk > 0)
            def _():
                acc_ref[...] += part

            @pl.when(kk == nk - 1)
            def _():
                finish(acc_ref[...])

    return pl.pallas_call(
        body, name=name, grid=(m // tm, n // tn, nk),
        in_specs=in_specs, out_specs=o_spec,
        out_shape=jax.ShapeDtypeStruct((m, n), out_dtype),
        scratch_shapes=[pltpu.VMEM((tm, tn), F32)] if nk > 1 else [],
        compiler_params=_cparams(("parallel", "parallel", "arbitrary")),
    )(*operands)


ROW_TILE = 512


def _row_spec(width=D_MODEL, rows=ROW_TILE):
    return pl.BlockSpec((rows, width), lambda i: (i, 0))


def _vec_spec(rows=1, width=D_MODEL):
    return pl.BlockSpec((rows, width), lambda i: (0, 0))


def _xhat(x):
    r = lax.rsqrt(jnp.mean(x * x, axis=-1, keepdims=True) + EPS)
    return x * r, r


def _rms_bwd_rows(dh, x, g):
    xh, r = _xhat(x)
    dxh = dh * g
    dx = r * (dxh - xh * jnp.mean(dxh * xh, axis=-1, keepdims=True))
    return dx, jnp.sum(dh * xh, axis=0, keepdims=True)


def _norm_fwd(x, g, name):
    t = x.shape[0]

    def body(x_ref, g_ref, h_ref):
        xh, _ = _xhat(x_ref[...])
        h_ref[...] = (xh * g_ref[...]).astype(BF16)

    return pl.pallas_call(
        body, name=name, grid=(t // ROW_TILE,), in_specs=[_row_spec(), _vec_spec()], out_specs=_row_spec(),
        out_shape=jax.ShapeDtypeStruct((t, D_MODEL), BF16), compiler_params=_cparams(("parallel",)),
    )(x, g)


def _norm_bwd(x, g, dh, dres, name):
    t = x.shape[0]

    def body(x_ref, g_ref, dh_ref, dres_ref, dx_ref, dxb_ref, dg_ref):
        dx, dg = _rms_bwd_rows(dh_ref[...], x_ref[...], g_ref[...])
        dx = dx + dres_ref[...]
        dx_ref[...] = dx
        dxb_ref[...] = dx.astype(BF16)

        @pl.when(pl.program_id(0) == 0)
        def _():
            dg_ref[...] = jnp.zeros_like(dg_ref)

        dg_ref[...] += dg

    return pl.pallas_call(
        body, name=name, grid=(t // ROW_TILE,),
        in_specs=[_row_spec(), _vec_spec(), _row_spec(), _row_spec()],
        out_specs=[_row_spec(), _row_spec(), _vec_spec()],
        out_shape=[jax.ShapeDtypeStruct((t, D_MODEL), F32), jax.ShapeDtypeStruct((t, D_MODEL), BF16),
                   jax.ShapeDtypeStruct((1, D_MODEL), F32)],
        compiler_params=_cparams(("arbitrary",)),
    )(x, g, dh, dres)


def _gate_fwd(gl, b_gate, of, os_):
    t = of.shape[0]

    def body(gla_ref, glb_ref, b_ref, of_ref, os_ref, m_ref):
        ga = jax.nn.sigmoid(gla_ref[...] + b_ref[0:1, :])
        gb = jax.nn.sigmoid(glb_ref[...] + b_ref[1:2, :])
        m_ref[...] = (ga * of_ref[...] + gb * os_ref[...]).astype(BF16)

    return pl.pallas_call(
        body, name="gate_fwd", grid=(t // ROW_TILE,),
        in_specs=[pl.BlockSpec((ROW_TILE, D_MODEL), lambda i: (i, 0)), pl.BlockSpec((ROW_TILE, D_MODEL), lambda i: (i, 1)),
                  _vec_spec(2), _row_spec(), _row_spec()],
        out_specs=_row_spec(), out_shape=jax.ShapeDtypeStruct((t, D_MODEL), BF16),
        compiler_params=_cparams(("parallel",)),
    )(gl, gl, b_gate, of, os_)


def _gate_bwd(gl, b_gate, of, os_, dmerged):
    t = of.shape[0]

    def body(gla_ref, glb_ref, b_ref, of_ref, os_ref, dm_ref, dof_ref, dos_ref, dgla_ref, dglb_ref, db_ref):
        dm = dm_ref[...]
        ga = jax.nn.sigmoid(gla_ref[...] + b_ref[0:1, :])
        gb = jax.nn.sigmoid(glb_ref[...] + b_ref[1:2, :])
        dof_ref[...] = (dm * ga).astype(BF16)
        dos_ref[...] = (dm * gb).astype(BF16)
        dgla = dm * of_ref[...] * ga * (1.0 - ga)
        dglb = dm * os_ref[...] * gb * (1.0 - gb)
        dgla_ref[...] = dgla.astype(BF16)
        dglb_ref[...] = dglb.astype(BF16)

        @pl.when(pl.program_id(0) == 0)
        def _():
            db_ref[...] = jnp.zeros_like(db_ref)

        db_ref[0:1, :] += jnp.sum(dgla, axis=0, keepdims=True)
        db_ref[1:2, :] += jnp.sum(dglb, axis=0, keepdims=True)

    outs = pl.pallas_call(
        body, name="gate_bwd", grid=(t // ROW_TILE,),
        in_specs=[pl.BlockSpec((ROW_TILE, D_MODEL), lambda i: (i, 0)), pl.BlockSpec((ROW_TILE, D_MODEL), lambda i: (i, 1)),
                  _vec_spec(2), _row_spec(), _row_spec(), _row_spec()],
        out_specs=[_row_spec(), _row_spec(), _row_spec(), _row_spec(), _vec_spec(2)],
        out_shape=[jax.ShapeDtypeStruct((t, D_MODEL), BF16)] * 4 + [jax.ShapeDtypeStruct((2, D_MODEL), F32)],
        compiler_params=_cparams(("arbitrary",)),
    )(gl, gl, b_gate, of, os_, dmerged)
    return outs


def _head_and_loss(x2, gpre, pe, g_final, target):
    t = x2.shape[0]

    def body(x2_ref, gpre_ref, pe_ref, g_ref, tgt_ref, dx3_ref, dpre_ref, dpe_ref, dg_ref, loss_ref):
        gp = jax.nn.sigmoid(gpre_ref[...])
        pe_t = pe_ref[...]
        x3 = x2_ref[...] + gp * pe_t
        g = g_ref[...]
        xh, _ = _xhat(x3)
        err = xh * g - tgt_ref[...]
        dy = err * (1.0 / D_MODEL)
        dx3, dg = _rms_bwd_rows(dy, x3, g)
        dx3_ref[...] = dx3
        dpre_ref[...] = (dx3 * pe_t * gp * (1.0 - gp)).astype(BF16)
        dpe_ref[...] = (dx3 * gp).astype(BF16)

        @pl.when(pl.program_id(0) == 0)
        def _():
            dg_ref[...] = jnp.zeros_like(dg_ref)
            loss_ref[...] = jnp.zeros_like(loss_ref)

        dg_ref[...] += dg
        loss_ref[...] += 0.5 * jnp.sum(jnp.mean(err * err, axis=-1, keepdims=True), axis=0, keepdims=True)

    return pl.pallas_call(
        body, name="head_and_loss", grid=(t // ROW_TILE,),
        in_specs=[_row_spec(), _row_spec(), _row_spec(), _vec_spec(), _row_spec()],
        out_specs=[_row_spec(), _row_spec(), _row_spec(), _vec_spec(), _vec_spec(1, LANES)],
        out_shape=[jax.ShapeDtypeStruct((t, D_MODEL), F32), jax.ShapeDtypeStruct((t, D_MODEL), BF16),
                   jax.ShapeDtypeStruct((t, D_MODEL), BF16), jax.ShapeDtypeStruct((1, D_MODEL), F32),
                   jax.ShapeDtypeStruct((1, LANES), F32)],
        compiler_params=_cparams(("arbitrary",)),
    )(x2, gpre, pe, g_final, target)


def _split3(v):
    hi = v.astype(BF16)
    r1 = v - hi.astype(F32)
    mid = r1.astype(BF16)
    lo = (r1 - mid.astype(F32)).astype(BF16)
    return hi, mid, lo


def _split2(v):
    hi = v.astype(BF16)
    return hi, (v - hi.astype(F32)).astype(BF16)


def _dot(a, b, dims=_DIMS["nn"]):
    return lax.dot_general(a, b, dims, preferred_element_type=F32)


def _tri(n, rel):
    row = lax.broadcasted_iota(jnp.int32, (n, n), 0)
    col = lax.broadcasted_iota(jnp.int32, (n, n), 1)
    return rel(row, col).astype(BF16)


def _log_sigmoid(v):
    return -(jnp.maximum(-v, 0.0) + jnp.log(1.0 + jnp.exp(-jnp.abs(v))))


def _fox_prep(fl, b_forget, batch, seq):
    nb = seq // ATT_BLOCK

    def body(fl_ref, b_ref, cw_ref, cr_ref):
        col = lax.broadcasted_iota(jnp.int32, (ATT_BLOCK, F_PAD), 1)
        lower = _tri(ATT_BLOCK, lambda r, c: c <= r)
        upper = _tri(ATT_BLOCK, lambda r, c: r <= c)
        expand = (lax.broadcasted_iota(jnp.int32, (F_PAD, D_ATT), 1) // HEAD_DIM
                  == lax.broadcasted_iota(jnp.int32, (F_PAD, D_ATT), 0)).astype(BF16)
        carry_w = jnp.zeros((1, D_ATT), F32)
        carry_r = jnp.zeros((F_PAD, 1), F32)
        for i in range(nb):
            blk = slice(i * ATT_BLOCK, (i + 1) * ATT_BLOCK)
            logf = jnp.where(col < N_HEADS, _log_sigmoid(fl_ref[blk, :] + b_ref[...]), 0.0)
            cw = jnp.zeros((ATT_BLOCK, D_ATT), F32) + carry_w
            cr = jnp.zeros((F_PAD, ATT_BLOCK), F32) + carry_r
            for part in _split3(logf):
                cw += _dot(lower, _dot(part, expand).astype(BF16))
                cr += _dot(part, upper, TN_DIMS)
            cw_ref[blk, :] = cw
            cr_ref[:, blk] = cr[0:N_HEADS, :]
            carry_w = cw[ATT_BLOCK - 1:ATT_BLOCK, :]
            carry_r = cr[:, ATT_BLOCK - 1:ATT_BLOCK]

    return pl.pallas_call(
        body, name="fox_prep", grid=(batch,),
        in_specs=[pl.BlockSpec((seq, F_PAD), lambda b: (b, 0)), pl.BlockSpec((1, F_PAD), lambda b: (0, 0))],
        out_specs=[pl.BlockSpec((seq, D_ATT), lambda b: (b, 0)), pl.BlockSpec((N_HEADS, seq), lambda b: (b, 0))],
        out_shape=[jax.ShapeDtypeStruct((batch * seq, D_ATT), F32), jax.ShapeDtypeStruct((batch * N_HEADS, seq), F32)],
        compiler_params=_cparams(("parallel",)),
    )(fl, b_forget)


def _fox_post(dcs_wide, drs_wide, fl, b_forget, batch, seq):
    nb = seq // ATT_BLOCK

    def body(dcs_ref, drs_ref, fl_ref, b_ref, dfl_ref, db_ref):
        pick = (lax.broadcasted_iota(jnp.int32, (D_ATT, F_PAD), 0)
                == lax.broadcasted_iota(jnp.int32, (D_ATT, F_PAD), 1) * HEAD_DIM).astype(BF16)
        upper = _tri(ATT_BLOCK, lambda r, c: r <= c)
        col = lax.broadcasted_iota(jnp.int32, (ATT_BLOCK, F_PAD), 1)

        @pl.when(pl.program_id(0) == 0)
        def _():
            db_ref[...] = jnp.zeros_like(db_ref)

        carry = jnp.zeros((1, F_PAD), F32)
        for i in reversed(range(nb)):
            blk = slice(i * ATT_BLOCK, (i + 1) * ATT_BLOCK)
            narrow = jnp.zeros((ATT_BLOCK, F_PAD), F32)
            for part in _split3(drs_ref[blk, :] - dcs_ref[blk, :]):
                narrow += _dot(part, pick)
            after = jnp.zeros((ATT_BLOCK, F_PAD), F32) + carry
            for part in _split3(narrow):
                after += _dot(upper, part)
            carry = after[0:1, :]
            pre = fl_ref[blk, :] + b_ref[...]
            dfl = jnp.where(col < N_HEADS, after * jax.nn.sigmoid(-pre), 0.0)
            dfl_ref[blk, :] = dfl.astype(BF16)
            db_ref[...] += jnp.sum(dfl, axis=0, keepdims=True)

    return pl.pallas_call(
        body, name="fox_post", grid=(batch,),
        in_specs=[pl.BlockSpec((seq, D_ATT), lambda b: (b, 0)), pl.BlockSpec((seq, D_ATT), lambda b: (b, 0)),
                  pl.BlockSpec((seq, F_PAD), lambda b: (b, 0)), pl.BlockSpec((1, F_PAD), lambda b: (0, 0))],
        out_specs=[pl.BlockSpec((seq, F_PAD), lambda b: (b, 0)), pl.BlockSpec((1, F_PAD), lambda b: (0, 0))],
        out_shape=[jax.ShapeDtypeStruct((batch * seq, F_PAD), BF16), jax.ShapeDtypeStruct((1, F_PAD), F32)],
        compiler_params=_cparams(("arbitrary",)),
    )(dcs_wide, drs_wide, fl, b_forget)


N_PAIRS = N_HEADS // 2


def _att_specs(seq, col0):
    nq = seq // ATT_BLOCK
    q = pl.BlockSpec((ATT_BLOCK, LANES), lambda b, hp, qi: (b * nq + qi, col0 + hp))
    k = pl.BlockSpec((seq, LANES), lambda b, hp, qi: (b, col0 + N_PAIRS + hp))
    v = pl.BlockSpec((seq, LANES), lambda b, hp, qi: (b, col0 + 2 * N_PAIRS + hp))
    return q, k, v


def _qblock_spec(seq):
    nq = seq // ATT_BLOCK
    return pl.BlockSpec((ATT_BLOCK, LANES), lambda b, hp, qi: (b * nq + qi, hp))


def _kv_out_spec(seq):
    return pl.BlockSpec((seq, LANES), lambda b, hp, qi: (b, hp))


def _head_masks():
    lane = lax.broadcasted_iota(jnp.int32, (1, LANES), 1)
    return [(lane >= HEAD_DIM * j) & (lane < HEAD_DIM * (j + 1)) for j in range(2)]


def _block_ids():
    row = lax.broadcasted_iota(jnp.int32, (ATT_BLOCK, ATT_BLOCK), 0)
    col = lax.broadcasted_iota(jnp.int32, (ATT_BLOCK, ATT_BLOCK), 1)
    return row, col


def _fox_fwd(qkv, c_wide, c_row, batch, seq):
    nq = seq // ATT_BLOCK
    tb = ATT_BLOCK

    def body(q_ref, k_ref, v_ref, cw_ref, cr_ref, o_ref, lse_ref):
        hp, qi = pl.program_id(1), pl.program_id(2)
        masks = _head_masks()
        row, col = _block_ids()
        q = q_ref[...]
        cw = cw_ref[...]
        res_o, res_l = [], []
        for j in range(2):
            qm = jnp.where(masks[j], q, jnp.zeros_like(q))
            ct = cw[:, HEAD_DIM * j:HEAD_DIM * j + 1]
            h = 2 * hp + j

            def step(kb, carry, diagonal):
                m, l, acc = carry
                k0 = pl.multiple_of(kb * tb, tb)
                kblk = k_ref[pl.ds(k0, tb), :]
                vblk = v_ref[pl.ds(k0, tb), :]
                s = _dot(qm, kblk, NT_DIMS) * SCALE + ct - cr_ref[pl.ds(h, 1), pl.ds(k0, tb)]
                if diagonal:
                    s = jnp.where(col <= row, s, NEG)
                m_new = jnp.maximum(m, jnp.max(s, axis=1, keepdims=True))
                p = jnp.exp(s - m_new)
                alpha = jnp.exp(m - m_new)
                l = alpha * l + jnp.sum(p, axis=1, keepdims=True)
                acc = alpha * acc + _dot(p.astype(BF16), vblk)
                return m_new, l, acc

            init = (jnp.full((tb, 1), NEG, F32), jnp.zeros((tb, 1), F32), jnp.zeros((tb, LANES), F32))
            carry = step(qi, init, True)
            m, l, acc = lax.fori_loop(0, qi, lambda kb, c: step(kb, c, False), carry)
            res_o.append(acc / l)
            res_l.append(m + jnp.log(l))
        o_ref[...] = jnp.where(masks[0], res_o[0], res_o[1]).astype(BF16)
        lse_ref[...] = jnp.where(masks[0], res_l[0], res_l[1])

    q_spec, k_spec, v_spec = _att_specs(seq, 0)
    return pl.pallas_call(
        body, name="fox_fwd", grid=(batch, N_PAIRS, nq),
        in_specs=[q_spec, k_spec, v_spec, _qblock_spec(seq), pl.BlockSpec((N_HEADS, seq), lambda b, hp, qi: (b, 0))],
        out_specs=[_qblock_spec(seq), _qblock_spec(seq)],
        out_shape=[jax.ShapeDtypeStruct((batch * seq, D_ATT), BF16), jax.ShapeDtypeStruct((batch * seq, D_ATT), F32)],
        compiler_params=_cparams(("parallel", "parallel", "arbitrary")),
    )(qkv, qkv, qkv, c_wide, c_row)


def _fox_bwd(qkv, c_wide, c_row, o, do, lse_wide, batch, seq):
    nq = seq // ATT_BLOCK
    tb = ATT_BLOCK

    def body(q_ref, k_ref, v_ref, cw_ref, cr_ref, o_ref, do_ref, lse_ref,
             dq_ref, dk_ref, dv_ref, dcs_ref, drs_ref, dkc_acc, dv_acc):
        hp, qi = pl.program_id(1), pl.program_id(2)

        @pl.when(qi == 0)
        def _():
            dkc_acc[...] = jnp.zeros_like(dkc_acc)
            dv_acc[...] = jnp.zeros_like(dv_acc)

        masks = _head_masks()
        row, col = _block_ids()
        q = q_ref[...]
        cw = cw_ref[...]
        do_t = do_ref[...]
        lse = lse_ref[...]
        prod = do_t.astype(F32) * o_ref[...].astype(F32)
        dq_parts, rs_parts = [], []
        for j in range(2):
            zero = jnp.zeros_like(q)
            qm = jnp.where(masks[j], q, zero)
            dom = jnp.where(masks[j], do_t, zero)
            q_and_ones = jnp.concatenate([qm, jnp.where(masks[j], jnp.ones_like(q), zero)], axis=1)
            ct = cw[:, HEAD_DIM * j:HEAD_DIM * j + 1]
            lse_j = lse[:, HEAD_DIM * j:HEAD_DIM * j + 1]
            delta = jnp.sum(jnp.where(masks[j], prod, 0.0), axis=1, keepdims=True)
            h = 2 * hp + j

            def step(kb, carry, diagonal):
                dq_acc, rs = carry
                k0 = pl.multiple_of(kb * tb, tb)
                kblk = k_ref[pl.ds(k0, tb), :]
                vblk = v_ref[pl.ds(k0, tb), :]
                s = _dot(qm, kblk, NT_DIMS) * SCALE + ct - cr_ref[pl.ds(h, 1), pl.ds(k0, tb)]
                p = jnp.exp(s - lse_j)
                if diagonal:
                    p = jnp.where(col <= row, p, 0.0)
                dp = _dot(dom, vblk, NT_DIMS)
                ds = (p * (dp - delta) * SCALE).astype(BF16)
                dkc_acc[pl.ds(k0, tb), :] += _dot(ds, q_and_ones, TN_DIMS)
                dv_acc[pl.ds(k0, tb), :] += _dot(p.astype(BF16), dom, TN_DIMS)
                return dq_acc + _dot(ds, kblk), rs + jnp.sum(ds.astype(F32), axis=1, keepdims=True)

            carry = step(qi, (jnp.zeros((tb, LANES), F32), jnp.zeros((tb, 1), F32)), True)
            dq_acc, rs = lax.fori_loop(0, qi, lambda kb, c: step(kb, c, False), carry)
            dq_parts.append(dq_acc)
            rs_parts.append(rs)
        dq_ref[...] = jnp.where(masks[0], dq_parts[0], dq_parts[1]).astype(BF16)
        drs_ref[...] = jnp.where(masks[0], rs_parts[0], rs_parts[1]) * (1.0 / SCALE)

        @pl.when(qi == nq - 1)
        def _():
            dk_ref[...] = dkc_acc[:, 0:LANES].astype(BF16)
            dcs_ref[...] = dkc_acc[:, LANES:2 * LANES] * (1.0 / SCALE)
            dv_ref[...] = dv_acc[...].astype(BF16)

    q_spec, k_spec, v_spec = _att_specs(seq, 0)
    qb = _qblock_spec(seq)
    return pl.pallas_call(
        body, name="fox_bwd", grid=(batch, N_PAIRS, nq),
        in_specs=[q_spec, k_spec, v_spec, qb, pl.BlockSpec((N_HEADS, seq), lambda b, hp, qi: (b, 0)), qb, qb, qb],
        out_specs=[qb, _kv_out_spec(seq), _kv_out_spec(seq), _kv_out_spec(seq), qb],
        out_shape=[jax.ShapeDtypeStruct((batch * seq, D_ATT), BF16)] * 3 + [jax.ShapeDtypeStruct((batch * seq, D_ATT), F32)] * 2,
        scratch_shapes=[pltpu.VMEM((seq, 2 * LANES), F32), pltpu.VMEM((seq, LANES), F32)],
        compiler_params=_cparams(("parallel", "parallel", "arbitrary")),
    )(qkv, qkv, qkv, c_wide, c_row, o, do, lse_wide)


def _sb_logits(qm, kblk):
    z = _dot(qm, kblk, NT_DIMS) * SCALE
    lsn = -(jnp.maximum(z, 0.0) + jnp.log(1.0 + jnp.exp(-jnp.abs(z))))
    return lsn + z, lsn


def _sb_fwd(qkv, batch, seq):
    nq = seq // ATT_BLOCK
    tb = ATT_BLOCK

    def body(q_ref, k_ref, v_ref, o_ref, rt_ref):
        qi = pl.program_id(2)
        masks = _head_masks()
        row, col = _block_ids()
        later = _tri(tb, lambda r, c: r > c)
        q = q_ref[...]
        res_o, res_r = [], []
        for j in range(2):
            qm = jnp.where(masks[j], q, jnp.zeros_like(q))

            def step(kb, carry, diagonal):
                run, acc = carry
                k0 = pl.multiple_of(kb * tb, tb)
                ls, lsn = _sb_logits(qm, k_ref[pl.ds(k0, tb), :])
                if diagonal:
                    lsn = jnp.where(col < row, lsn, 0.0)
                hi, lo = _split2(lsn)
                tail = _dot(hi, later) + _dot(lo, later) + run
                w = jnp.exp(ls + tail)
                if diagonal:
                    w = jnp.where(col < row, w, 0.0)
                acc = acc + _dot(w.astype(BF16), v_ref[pl.ds(k0, tb), :])
                return run + jnp.sum(lsn, axis=1, keepdims=True), acc

            carry = step(qi, (jnp.zeros((tb, 1), F32), jnp.zeros((tb, LANES), F32)), True)
            run, acc = lax.fori_loop(0, qi, lambda i, c: step(qi - 1 - i, c, False), carry)
            res_o.append(acc)
            res_r.append(run)
        o_ref[...] = jnp.where(masks[0], res_o[0], res_o[1]).astype(BF16)
        rt_ref[...] = jnp.where(masks[0], res_r[0], res_r[1])

    q_spec, k_spec, v_spec = _att_specs(seq, 3 * N_PAIRS)
    return pl.pallas_call(
        body, name="sb_fwd", grid=(batch, N_PAIRS, nq),
        in_specs=[q_spec, k_spec, v_spec], out_specs=[_qblock_spec(seq), _qblock_spec(seq)],
        out_shape=[jax.ShapeDtypeStruct((batch * seq, D_ATT), BF16), jax.ShapeDtypeStruct((batch * seq, D_ATT), F32)],
        compiler_params=_cparams(("parallel", "parallel", "arbitrary")),
    )(qkv, qkv, qkv)


def _sb_bwd(qkv, do, rt_wide, batch, seq):
    nq = seq // ATT_BLOCK
    tb = ATT_BLOCK

    def body(q_ref, k_ref, v_ref, do_ref, rt_ref, dq_ref, dk_ref, dv_ref, dk_acc, dv_acc):
        qi = pl.program_id(2)

        @pl.when(qi == 0)
        def _():
            dk_acc[...] = jnp.zeros_like(dk_acc)
            dv_acc[...] = jnp.zeros_like(dv_acc)

        masks = _head_masks()
        row, col = _block_ids()
        later = _tri(tb, lambda r, c: r > c)
        earlier = _tri(tb, lambda r, c: r < c)
        q = q_ref[...]
        do_t = do_ref[...]
        rt = rt_ref[...]
        dq_parts = []
        for j in range(2):
            zero = jnp.zeros_like(q)
            qm = jnp.where(masks[j], q, zero)
            dom = jnp.where(masks[j], do_t, zero)
            rt_j = rt[:, HEAD_DIM * j:HEAD_DIM * j + 1]

            def step(kb, carry, diagonal):
                pref, epre, dq_acc = carry
                k0 = pl.multiple_of(kb * tb, tb)
                kblk = k_ref[pl.ds(k0, tb), :]
                vblk = v_ref[pl.ds(k0, tb), :]
                ls, lsn_all = _sb_logits(qm, kblk)
                lsn = jnp.where(col < row, lsn_all, 0.0) if diagonal else lsn_all
                rs = jnp.sum(lsn, axis=1, keepdims=True)
                hi, lo = _split2(lsn)
                tail = _dot(hi, later) + _dot(lo, later) + (rt_j - pref - rs)
                w = jnp.exp(ls + tail)
                if diagonal:
                    w = jnp.where(col < row, w, 0.0)
                e = w * _dot(dom, vblk, NT_DIMS)
                ehi, elo = _split2(e)
                before = _dot(ehi, earlier) + _dot(elo, earlier) + epre
                dz = e * jnp.exp(lsn_all) - jnp.exp(ls) * before
                if diagonal:
                    dz = jnp.where(col < row, dz, 0.0)
                dz = (dz * SCALE).astype(BF16)
                dk_acc[pl.ds(k0, tb), :] += _dot(dz, qm, TN_DIMS)
                dv_acc[pl.ds(k0, tb), :] += _dot(w.astype(BF16), dom, TN_DIMS)
                return pref + rs, epre + jnp.sum(e, axis=1, keepdims=True), dq_acc + _dot(dz, kblk)

            init = (jnp.zeros((tb, 1), F32), jnp.zeros((tb, 1), F32), jnp.zeros((tb, LANES), F32))
            carry = lax.fori_loop(0, qi, lambda kb, c: step(kb, c, False), init)
            dq_parts.append(step(qi, carry, True)[2])
        dq_ref[...] = jnp.where(masks[0], dq_parts[0], dq_parts[1]).astype(BF16)

        @pl.when(qi == nq - 1)
        def _():
            dk_ref[...] = dk_acc[...].astype(BF16)
            dv_ref[...] = dv_acc[...].astype(BF16)

    q_spec, k_spec, v_spec = _att_specs(seq, 3 * N_PAIRS)
    qb = _qblock_spec(seq)
    return pl.pallas_call(
        body, name="sb_bwd", grid=(batch, N_PAIRS, nq),
        in_specs=[q_spec, k_spec, v_spec, qb, qb],
        out_specs=[qb, _kv_out_spec(seq), _kv_out_spec(seq)],
        out_shape=[jax.ShapeDtypeStruct((batch * seq, D_ATT), BF16)] * 3,
        scratch_shapes=[pltpu.VMEM((seq, LANES), F32), pltpu.VMEM((seq, LANES), F32)],
        compiler_params=_cparams(("parallel", "parallel", "arbitrary")),
    )(qkv, qkv, qkv, do, rt_wide)


def _local_step(x, p, target, w, vec):
    batch, seq, _ = x.shape
    t = batch * seq
    x = x.reshape(t, D_MODEL)
    target = target.reshape(t, D_MODEL)
    p = p.reshape(t, D_PLE)
    big = dict(tm=1024, tn=1024, tk=1024)

    h1 = _norm_fwd(x, vec["g_mix"], "norm_mix")
    qkv = _mm(h1, w["qkv"], mode="nn", name="proj_qkv", out_dtype=BF16, **big)
    gl = _mm(h1, w["gate"], mode="nn", name="proj_gate", **big)
    fl = _mm(h1, w["forget"], mode="nn", name="proj_forget", **big)
    c_wide, c_row = _fox_prep(fl, vec["b_forget"], batch, seq)
    o_fox, lse_wide = _fox_fwd(qkv, c_wide, c_row, batch, seq)
    o_sb, rt_wide = _sb_fwd(qkv, batch, seq)
    of = _mm(o_fox, w["branch_fox"], mode="nn", name="branch_fox", **big)
    os_ = _mm(o_sb, w["branch_sb"], mode="nn", name="branch_sb", **big)
    merged = _gate_fwd(gl, w["b_gate"], of, os_)
    x1 = _mm(merged, w["out"], mode="nn", name="proj_out", add=x, **big)
    h2 = _norm_fwd(x1, vec["g_mlp"], "norm_mlp")
    ar = _mm(h2, w["up"], mode="nn", name="mlp_up", out_dtype=BF16, epi=lambda acc, _: jnp.maximum(acc, 0.0), **big)
    x2 = _mm(ar, w["down"], mode="nn", name="mlp_down", a_fn=_relu2, add=x1, **big)
    h3 = _norm_fwd(x2, vec["g_ple"], "norm_ple")
    gpre = _mm(h3, w["ple_gate"], mode="nn", name="ple_gate", **big)
    pe = _mm(p, w["ple"], mode="nn", name="ple_embed", **big)

    dx3, dpre, dpe, dg_final, loss = _head_and_loss(x2, gpre, pe, vec["g_final"], target)
    gw = {}
    gw["ple"] = _mm(p, dpe, mode="tn", name="d_w_ple", **big)
    gw["ple_gate"] = _mm(h3, dpre, mode="tn", name="d_w_ple_gate", **big)
    dh3 = _mm(dpre, w["ple_gate"], mode="nt", name="d_h_ple", **big)
    dx2, dx2b, dg_ple = _norm_bwd(x2, vec["g_ple"], dh3, dx3, "norm_ple_bwd")
    gw["down"] = _mm(ar, dx2b, mode="tn", name="d_w_down", a_fn=_relu2, **big)
    da = _mm(dx2b, w["down"], mode="nt", name="d_act", out_dtype=BF16,
             epi=lambda acc, r: acc * (2.0 * r.astype(F32)), extra=ar, **big)
    gw["up"] = _mm(h2, da, mode="tn", name="d_w_up", **big)
    dh2 = _mm(da, w["up"], mode="nt", name="d_h_mlp", **big)
    dx1, dx1b, dg_mlp = _norm_bwd(x1, vec["g_mlp"], dh2, dx2, "norm_mlp_bwd")
    gw["out"] = _mm(merged, dx1b, mode="tn", name="d_w_out", **big)
    dmerged = _mm(dx1b, w["out"], mode="nt", name="d_merged", **big)
    dof, dos, dgla, dglb, gw["b_gate"] = _gate_bwd(gl, w["b_gate"], of, os_, dmerged)
    gw["branch_fox"] = _mm(o_fox, dof, mode="tn", name="d_w_branch_fox", **big)
    gw["branch_sb"] = _mm(o_sb, dos, mode="tn", name="d_w_branch_sb", **big)
    do_fox = _mm(dof, w["branch_fox"], mode="nt", name="d_o_fox", out_dtype=BF16, **big)
    do_sb = _mm(dos, w["branch_sb"], mode="nt", name="d_o_sb", out_dtype=BF16, **big)
    dq_a, dk_a, dv_a, dcs_wide, drs_wide = _fox_bwd(qkv, c_wide, c_row, o_fox, do_fox, lse_wide, batch, seq)
    dq_b, dk_b, dv_b = _sb_bwd(qkv, do_sb, rt_wide, batch, seq)
    dfl, db_forget = _fox_post(dcs_wide, drs_wide, fl, vec["b_forget"], batch, seq)
    dqkv = jnp.concatenate([dq_a, dk_a, dv_a, dq_b, dk_b, dv_b], axis=1)
    dgl = jnp.concatenate([dgla, dglb], axis=1)
    gw["qkv"] = _mm(h1, dqkv, mode="tn", name="d_w_qkv", **big)
    gw["gate"] = _mm(h1, dgl, mode="tn", name="d_w_gate", **big)
    gw["forget"] = _mm(h1, dfl, mode="tn", name="d_w_forget", **big)
    dh1 = _mm(dqkv, w["qkv"], mode="nt", name="d_h_qkv", **big)
    dh1 = _mm(dgl, w["gate"], mode="nt", name="d_h_gate", add=dh1, **big)
    dh1 = _mm(dfl, w["forget"], mode="nt", name="d_h_forget", add=dh1, **big)
    grad_x, _, dg_mix = _norm_bwd(x, vec["g_mix"], dh1, dx1, "norm_mix_bwd")

    gvec = {"g_mix": dg_mix, "b_forget": db_forget[:, 0:N_HEADS], "g_mlp": dg_mlp, "g_ple": dg_ple,
            "g_final": dg_final}
    return loss[0, 0], grad_x.reshape(batch, seq, D_MODEL), gw, gvec


W_IN_SHARD = D_IN // N_CHIPS
PART_ALIGN = 16
PACK_LAYOUT = (("w_in", 1296), ("b_gate", PART_ALIGN), ("w_branch_fox", 128), ("w_branch_sb", 128), ("w_out", 256),
               ("w_up", 1024), ("w_down", 1024), ("w_ple_gate", 256), ("w_ple", 64),
               ("g_mix", PART_ALIGN), ("b_forget", PART_ALIGN), ("g_mlp", PART_ALIGN), ("g_ple", PART_ALIGN),
               ("g_final", PART_ALIGN))
assert all(r % PART_ALIGN == 0 for _, r in PACK_LAYOUT) and 1296 >= W_IN_SHARD
PACK_USED = sum(r for _, r in PACK_LAYOUT)
assert PACK_USED <= PACK_ROWS


def _as_rows(a, rows):
    flat = a.reshape(-1)
    pad = rows * PACK_COLS - flat.shape[0]
    if pad:
        flat = jnp.concatenate([flat, jnp.zeros((pad,), flat.dtype)])
    return flat.reshape(rows, PACK_COLS)


def _pack(shards, dtype):
    parts = [_as_rows(shards[name].astype(dtype), rows) for name, rows in PACK_LAYOUT]
    parts.append(jnp.zeros((PACK_ROWS - PACK_USED, PACK_COLS), dtype))
    return jnp.concatenate(parts, axis=0)


def _unpack(packed, shapes):
    out, r0 = {}, 0
    for name, rows in PACK_LAYOUT:
        shape = shapes[name]
        n = 1
        for s in shape:
            n *= s
        out[name] = packed[r0:r0 + rows].reshape(-1)[:n].reshape(shape)
        r0 += rows
    return out


def _whole_weights(gathered):
    shapes = {"w_in": (D_MODEL, W_IN_SHARD), "b_gate": (2, 256), "w_branch_fox": (D_ATT, 256), "w_branch_sb": (D_ATT, 256),
              "w_out": (256, D_MODEL), "w_up": (D_MODEL, 1024), "w_down": (1024, D_MODEL), "w_ple_gate": (256, D_MODEL),
              "w_ple": (D_PLE, 256), "g_mix": (1, D_MODEL), "b_forget": (1, N_HEADS), "g_mlp": (1, D_MODEL),
              "g_ple": (1, D_MODEL), "g_final": (1, D_MODEL)}
    per_chip = [_unpack(gathered[j], shapes) for j in range(N_CHIPS)]
    cols = lambda name: jnp.concatenate([c[name] for c in per_chip], axis=1)
    rows = lambda name: jnp.concatenate([c[name] for c in per_chip], axis=0)
    w_in = cols("w_in")
    a0, f0, b0, g0 = 0, 3 * D_ATT, 3 * D_ATT + N_HEADS, 6 * D_ATT + N_HEADS
    forget = jnp.concatenate([w_in[:, f0:b0], jnp.zeros((D_MODEL, F_PAD - N_HEADS), BF16)], axis=1)
    return {"qkv": jnp.concatenate([w_in[:, a0:f0], w_in[:, b0:g0]], axis=1), "gate": w_in[:, g0:], "forget": forget,
            "b_gate": cols("b_gate").astype(F32), "branch_fox": cols("w_branch_fox"), "branch_sb": cols("w_branch_sb"),
            "out": rows("w_out"), "up": cols("w_up"), "down": rows("w_down"), "ple_gate": rows("w_ple_gate"),
            "ple": cols("w_ple")}


def _grad_slots(gw, gvec):
    f0, b0 = 3 * D_ATT, 3 * D_ATT + N_HEADS
    g_in = jnp.concatenate([gw["qkv"][:, :f0], gw["forget"][:, :N_HEADS], gw["qkv"][:, f0:], gw["gate"]], axis=1)
    slots = []
    for j in range(N_CHIPS):
        cs = lambda a, n: a[:, n * j:n * (j + 1)]
        rs = lambda a, n: a[n * j:n * (j + 1)]
        shard = {"w_in": cs(g_in, W_IN_SHARD), "b_gate": cs(gw["b_gate"], 256), "w_branch_fox": cs(gw["branch_fox"], 256),
                 "w_branch_sb": cs(gw["branch_sb"], 256), "w_out": rs(gw["out"], 256), "w_up": cs(gw["up"], 1024),
                 "w_down": rs(gw["down"], 1024), "w_ple_gate": rs(gw["ple_gate"], 256), "w_ple": cs(gw["ple"], 256)}
        shard.update(gvec)
        slots.append(_pack(shard, F32))
    return jnp.stack(slots)


ANY = pl.BlockSpec(memory_space=pl.ANY)


def _place():
    return lax.axis_index("x"), lax.axis_index("y"), lax.axis_index("c")


def _other_chips(x, y):
    return [(1 - x, y), (x, 1 - y), (1 - x, 1 - y)]


def _chip_exchange(src, *, broadcast, name):
    block = src.shape if broadcast else src.shape[1:]

    def body(src_ref, out_ref, send_sems, recv_sems, local_sem):
        x, y, c = _place()
        me = 2 * x + y
        mine = pltpu.make_async_copy(src_ref if broadcast else src_ref.at[me], out_ref.at[me], local_sem)
        mine.start()
        copies = []
        for k, (px, py) in enumerate(_other_chips(x, y)):
            there = 2 * px + py
            copies.append(pltpu.make_async_remote_copy(
                src_ref=src_ref if broadcast else src_ref.at[there], dst_ref=out_ref.at[me],
                send_sem=send_sems.at[k], recv_sem=recv_sems.at[k], device_id=(px, py, c), device_id_type=MESH))
        for cp in copies:
            cp.start()
        for cp in copies:
            cp.wait_recv()
        for cp in copies:
            cp.wait_send()
        mine.wait()

    return pl.pallas_call(
        body, name=name, in_specs=[ANY], out_specs=ANY,
        out_shape=jax.ShapeDtypeStruct((N_CHIPS,) + tuple(block), src.dtype),
        scratch_shapes=[pltpu.SemaphoreType.DMA((3,)), pltpu.SemaphoreType.DMA((3,)), pltpu.SemaphoreType.DMA],
    )(src)


def _sibling_swap(src, name):
    n, _, rows, cols = src.shape

    def body(src_ref, out_ref, send_sems, recv_sems):
        x, y, c = _place()
        copies = [pltpu.make_async_remote_copy(src_ref=src_ref.at[j, 1 - c], dst_ref=out_ref.at[j], send_sem=send_sems.at[j],
                                               recv_sem=recv_sems.at[j], device_id=(x, y, 1 - c), device_id_type=MESH)
                  for j in range(n)]
        for cp in copies:
            cp.start()
        for cp in copies:
            cp.wait_recv()
        for cp in copies:
            cp.wait_send()

    return pl.pallas_call(
        body, name=name, in_specs=[ANY], out_specs=ANY, out_shape=jax.ShapeDtypeStruct((n, rows, cols), src.dtype),
        scratch_shapes=[pltpu.SemaphoreType.DMA((n,)), pltpu.SemaphoreType.DMA((n,))],
    )(src)


def _sibling_share(mine, name):
    n, rows, cols = mine.shape

    def body(src_ref, out_ref, send_sems, recv_sems, local_sems):
        x, y, c = _place()
        local = [pltpu.make_async_copy(src_ref.at[j], out_ref.at[j, c], local_sems.at[j]) for j in range(n)]
        copies = [pltpu.make_async_remote_copy(src_ref=src_ref.at[j], dst_ref=out_ref.at[j, c], send_sem=send_sems.at[j],
                                               recv_sem=recv_sems.at[j], device_id=(x, y, 1 - c), device_id_type=MESH)
                  for j in range(n)]
        for cp in local + copies:
            cp.start()
        for cp in copies:
            cp.wait_recv()
        for cp in copies:
            cp.wait_send()
        for cp in local:
            cp.wait()

    return pl.pallas_call(
        body, name=name, in_specs=[ANY], out_specs=ANY, out_shape=jax.ShapeDtypeStruct((n, 2, rows, cols), mine.dtype),
        scratch_shapes=[pltpu.SemaphoreType.DMA((n,)), pltpu.SemaphoreType.DMA((n,)), pltpu.SemaphoreType.DMA((n,))],
    )(mine)


SUM_TILE = 128


def _sum_pair(a, b):
    n, rows, cols = a.shape

    def body(a_ref, b_ref, o_ref):
        o_ref[...] = a_ref[...] + b_ref[...]

    spec = pl.BlockSpec((None, SUM_TILE, cols), lambda j, i: (j, i, 0))
    return pl.pallas_call(
        body, name="sum_pair", grid=(n, rows // SUM_TILE), in_specs=[spec, spec], out_specs=spec,
        out_shape=jax.ShapeDtypeStruct((n, rows, cols), F32), compiler_params=_cparams(("parallel", "parallel")),
    )(a, b)


def _sum_chips(parts):
    n, rows, cols = parts.shape

    def body(p_ref, o_ref):
        o_ref[0] = ((p_ref[0] + p_ref[1]) + p_ref[2]) + p_ref[3]

    return pl.pallas_call(
        body, name="sum_chips", grid=(rows // SUM_TILE,),
        in_specs=[pl.BlockSpec((n, SUM_TILE, cols), lambda i: (0, i, 0))],
        out_specs=pl.BlockSpec((1, SUM_TILE, cols), lambda i: (0, i, 0)),
        out_shape=jax.ShapeDtypeStruct((1, rows, cols), F32), compiler_params=_cparams(("parallel",)),
    )(parts)


def _gather_weights(packed):
    c = lax.axis_index("c")
    halves = packed.reshape(2, HALF_ROWS, PACK_COLS)
    mine = lax.dynamic_index_in_dim(halves, c, axis=0, keepdims=False)
    got = _chip_exchange(mine, broadcast=True, name="gather_chips")
    return _sibling_share(got, "gather_sibling").reshape(N_CHIPS, PACK_ROWS, PACK_COLS)


def _reduce_scatter(slots):
    c = lax.axis_index("c")
    slots = slots.reshape(N_CHIPS, 2, HALF_ROWS, PACK_COLS)
    from_sibling = _sibling_swap(slots, "reduce_sibling")
    chip_sum = _sum_pair(lax.dynamic_index_in_dim(slots, c, axis=1, keepdims=False), from_sibling)
    parts = _chip_exchange(chip_sum, broadcast=False, name="reduce_chips")
    return _sibling_share(_sum_chips(parts), "reduce_share").reshape(PACK_ROWS, PACK_COLS)


ADAM_TILE = 256


def _adamw(w, g, m, v):
    def body(w_ref, g_ref, m_ref, v_ref, d_ref, nm_ref, nv_ref):
        g_t = g_ref[...]
        m_new = ADAM_B1 * m_ref[...] + (1.0 - ADAM_B1) * g_t
        v_new = ADAM_B2 * v_ref[...] + (1.0 - ADAM_B2) * (g_t * g_t)
        m_hat = m_new / (1.0 - ADAM_B1 ** ADAM_STEP)
        v_hat = v_new / (1.0 - ADAM_B2 ** ADAM_STEP)
        d_ref[...] = -ADAM_LR * (m_hat / (jnp.sqrt(v_hat) + ADAM_EPS) + ADAM_WD * w_ref[...])
        nm_ref[...] = m_new
        nv_ref[...] = v_new

    spec = pl.BlockSpec((ADAM_TILE, PACK_COLS), lambda i: (i, 0))
    return pl.pallas_call(
        body, name="adamw", grid=(PACK_ROWS // ADAM_TILE,), in_specs=[spec] * 4, out_specs=[spec] * 3,
        out_shape=[jax.ShapeDtypeStruct((PACK_ROWS, PACK_COLS), F32)] * 3, compiler_params=_cparams(("parallel",)),
    )(w, g, m, v)


WEIGHT_NAMES = ("g_mix", "w_in", "b_forget", "b_gate", "w_branch_fox", "w_branch_sb", "w_out", "g_mlp", "w_up", "w_down",
                "g_ple", "w_ple_gate", "w_ple", "g_final")


def kernel(x, p, g_mix, w_in, b_forget, b_gate, w_branch_fox, w_branch_sb, w_out, g_mlp, w_up, w_down, g_ple, w_ple_gate, w_ple, g_final, loss_target, m_g_mix, m_w_in, m_b_forget, m_b_gate, m_w_branch_fox, m_w_branch_sb, m_w_out, m_g_mlp, m_w_up, m_w_down, m_g_ple, m_w_ple_gate, m_w_ple, m_g_final, v_g_mix, v_w_in, v_b_forget, v_b_gate, v_w_branch_fox, v_w_branch_sb, v_w_out, v_g_mlp, v_w_up, v_w_down, v_g_ple, v_w_ple_gate, v_w_ple, v_g_final):
    weights = dict(g_mix=g_mix, w_in=w_in, b_forget=b_forget, b_gate=b_gate, w_branch_fox=w_branch_fox,
                   w_branch_sb=w_branch_sb, w_out=w_out, g_mlp=g_mlp, w_up=w_up, w_down=w_down, g_ple=g_ple,
                   w_ple_gate=w_ple_gate, w_ple=w_ple, g_final=g_final)
    first = dict(g_mix=m_g_mix, w_in=m_w_in, b_forget=m_b_forget, b_gate=m_b_gate, w_branch_fox=m_w_branch_fox,
                 w_branch_sb=m_w_branch_sb, w_out=m_w_out, g_mlp=m_g_mlp, w_up=m_w_up, w_down=m_w_down, g_ple=m_g_ple,
                 w_ple_gate=m_w_ple_gate, w_ple=m_w_ple, g_final=m_g_final)
    second = dict(g_mix=v_g_mix, w_in=v_w_in, b_forget=v_b_forget, b_gate=v_b_gate, w_branch_fox=v_w_branch_fox,
                  w_branch_sb=v_w_branch_sb, w_out=v_w_out, g_mlp=v_g_mlp, w_up=v_w_up, w_down=v_w_down, g_ple=v_g_ple,
                  w_ple_gate=v_w_ple_gate, w_ple=v_w_ple, g_final=v_g_final)
    shapes = {n: weights[n].shape for n in WEIGHT_NAMES}

    w_packed = _pack(weights, F32)
    whole = _whole_weights(_gather_weights(w_packed.astype(BF16)))
    vec = {"g_mix": g_mix, "b_forget": jnp.concatenate([b_forget, jnp.zeros((1, F_PAD - N_HEADS), F32)], axis=1),
           "g_mlp": g_mlp, "g_ple": g_ple, "g_final": g_final.reshape(1, D_MODEL)}

    loss, grad_x, gw, gvec = _local_step(x, p[0], loss_target, whole, vec)
    loss = lax.psum(loss, ("x", "y", "c"))

    g_packed = _reduce_scatter(_grad_slots(gw, gvec))
    delta, new_m, new_v = _adamw(w_packed, g_packed, _pack(first, F32), _pack(second, F32))
    grads, deltas, new_ms, new_vs = (_unpack(a, shapes) for a in (g_packed, delta, new_m, new_v))
    return (loss, grad_x, *[grads[n] for n in WEIGHT_NAMES], *[deltas[n] for n in WEIGHT_NAMES],
            *[new_ms[n] for n in WEIGHT_NAMES], *[new_vs[n] for n in WEIGHT_NAMES])
```

```python
import jax
import jax.numpy as jnp
from jax import lax
from jax.experimental import pallas as pl
from jax.experimental.pallas import tpu as pltpu

F32 = jnp.float32
BF16 = jnp.bfloat16

D_MODEL = 1024
HEAD_DIM = 64
N_HEADS = 8
D_ATT = N_HEADS * HEAD_DIM
D_FF = 4 * D_MODEL
D_PLE = 256
D_IN = 6 * D_ATT + N_HEADS + 2 * D_MODEL
F_PAD = 128
EPS = 1e-6
SCALE = HEAD_DIM ** -0.5
N_CHIPS = 4
LANES = 128
ATT_BLOCK = 256
NEG = -1e30

ADAM_LR = 0.001
ADAM_B1 = 0.9
ADAM_B2 = 0.999
ADAM_EPS = 1e-08
ADAM_WD = 0.01
ADAM_STEP = 10

VMEM_LIMIT = 56 * 1024 * 1024

MESH = pl.DeviceIdType.MESH


def _cparams(sem=None):
    return pltpu.CompilerParams(dimension_semantics=sem, vmem_limit_bytes=VMEM_LIMIT)


def _relu2(t):
    t = t.astype(F32)
    return t * t


_DIMS = {"nn": (((1,), (0,)), ((), ())), "nt": (((1,), (1,)), ((), ())), "tn": (((0,), (0,)), ((), ()))}
NT_DIMS = _DIMS["nt"]
TN_DIMS = _DIMS["tn"]


def _mm(a, b, *, mode, name, out_dtype=F32, tm=512, tn=512, tk=512, add=None, a_fn=None, epi=None, extra=None,
        col_shards=False):
    if mode == "nn":
        (m, k), n = a.shape, b.shape[-1]
    elif mode == "nt":
        (m, k), n = a.shape, b.shape[-2]
    else:
        (k, m), n = a.shape, b.shape[1]
    shard = None
    if col_shards:
        if mode == "nn":
            shard, n = n, N_CHIPS * n
            tn = min(tn, shard)
        elif mode == "nt":
            shard = b.shape[-1]
            tk = min(tk, shard)
        else:
            shard = n // N_CHIPS
            tn = min(tn, shard)
    tm, tn, tk = min(tm, m), min(tn, n), min(tk, k)
    assert m % tm == 0 and n % tn == 0 and k % tk == 0, (name, m, n, k)
    nk = k // tk
    a_spec = {"nn": pl.BlockSpec((tm, tk), lambda i, j, kk: (i, kk)),
              "nt": pl.BlockSpec((tm, tk), lambda i, j, kk: (i, kk)),
              "tn": pl.BlockSpec((tk, tm), lambda i, j, kk: (kk, i))}[mode]
    b_spec = {"nn": pl.BlockSpec((tk, tn), lambda i, j, kk: (kk, j)),
              "nt": pl.BlockSpec((tn, tk), lambda i, j, kk: (j, kk)),
              "tn": pl.BlockSpec((tk, tn), lambda i, j, kk: (kk, j))}[mode]
    o_spec = pl.BlockSpec((tm, tn), lambda i, j, kk: (i, j))
    out_shape = (m, n)
    if col_shards and mode == "nn":
        per = shard // tn
        b_spec = pl.BlockSpec((None, tk, tn), lambda i, j, kk: (j // per, kk, j % per))
    elif col_shards and mode == "nt":
        per = shard // tk
        b_spec = pl.BlockSpec((None, tn, tk), lambda i, j, kk: (kk // per, j, kk % per))
    elif col_shards:
        assert add is None and extra is None
        per = shard // tn
        o_spec = pl.BlockSpec((None, tm, tn), lambda i, j, kk: (j // per, i, j % per))
        out_shape = (N_CHIPS, m, shard)
    operands, in_specs = [a, b], [a_spec, b_spec]
    third = add if add is not None else extra
    if third is not None:
        operands.append(third)
        in_specs.append(o_spec)

    def body(*refs):
        a_ref, b_ref = refs[0], refs[1]
        t_ref = refs[2] if third is not None else None
        o_ref = refs[3] if third is not None else refs[2]
        acc_ref = refs[-1] if nk > 1 else None
        at = a_ref[...]
        if a_fn is not None:
            at = a_fn(at)
        part = lax.dot_general(at.astype(BF16), b_ref[...].astype(BF16), _DIMS[mode], preferred_element_type=F32)

        def finish(acc):
            if epi is not None:
                acc = epi(acc, None if t_ref is None else t_ref[...])
            elif add is not None:
                acc = acc + t_ref[...].astype(F32)
            o_ref[...] = acc.astype(o_ref.dtype)

        if nk == 1:
            finish(part)
        else:
            kk = pl.program_id(2)

            @pl.when(kk == 0)
            def _():
                acc_ref[...] = part

            @pl.when(kk > 0)
            def _():
                acc_ref[...] += part

            @pl.when(kk == nk - 1)
            def _():
                finish(acc_ref[...])

    return pl.pallas_call(
        body, name=name, grid=(m // tm, n // tn, nk),
        in_specs=in_specs, out_specs=o_spec,
        out_shape=jax.ShapeDtypeStruct(out_shape, out_dtype),
        scratch_shapes=[pltpu.VMEM((tm, tn), F32)] if nk > 1 else [],
        compiler_params=_cparams(("parallel", "parallel", "arbitrary")),
    )(*operands)


ROW_TILE = 512


def _row_spec(width=D_MODEL, rows=ROW_TILE):
    return pl.BlockSpec((rows, width), lambda i: (i, 0))


def _vec_spec(rows=1, width=D_MODEL):
    return pl.BlockSpec((rows, width), lambda i: (0, 0))


def _xhat(x):
    r = lax.rsqrt(jnp.mean(x * x, axis=-1, keepdims=True) + EPS)
    return x * r, r


def _rms_bwd_rows(dh, x, g):
    xh, r = _xhat(x)
    dxh = dh * g
    dx = r * (dxh - xh * jnp.mean(dxh * xh, axis=-1, keepdims=True))
    return dx, jnp.sum(dh * xh, axis=0, keepdims=True)


def _norm_fwd(x, g, name):
    t = x.shape[0]

    def body(x_ref, g_ref, h_ref):
        xh, _ = _xhat(x_ref[...])
        h_ref[...] = (xh * g_ref[...]).astype(BF16)

    return pl.pallas_call(
        body, name=name, grid=(t // ROW_TILE,), in_specs=[_row_spec(), _vec_spec()], out_specs=_row_spec(),
        out_shape=jax.ShapeDtypeStruct((t, D_MODEL), BF16), compiler_params=_cparams(("parallel",)),
    )(x, g)


def _norm_bwd(x, g, dh, dres, name):
    t = x.shape[0]

    def body(x_ref, g_ref, dh_ref, dres_ref, dx_ref, dxb_ref, dg_ref):
        dx, dg = _rms_bwd_rows(dh_ref[...], x_ref[...], g_ref[...])
        dx = dx + dres_ref[...]
        dx_ref[...] = dx
        dxb_ref[...] = dx.astype(BF16)

        @pl.when(pl.program_id(0) == 0)
        def _():
            dg_ref[...] = jnp.zeros_like(dg_ref)

        dg_ref[...] += dg

    return pl.pallas_call(
        body, name=name, grid=(t // ROW_TILE,),
        in_specs=[_row_spec(), _vec_spec(), _row_spec(), _row_spec()],
        out_specs=[_row_spec(), _row_spec(), _vec_spec()],
        out_shape=[jax.ShapeDtypeStruct((t, D_MODEL), F32), jax.ShapeDtypeStruct((t, D_MODEL), BF16),
                   jax.ShapeDtypeStruct((1, D_MODEL), F32)],
        compiler_params=_cparams(("arbitrary",)),
    )(x, g, dh, dres)


def _gate_fwd(gl, b_gate, of, os_):
    t = of.shape[0]

    def body(gla_ref, glb_ref, b_ref, of_ref, os_ref, m_ref):
        ga = jax.nn.sigmoid(gla_ref[...] + b_ref[0:1, :])
        gb = jax.nn.sigmoid(glb_ref[...] + b_ref[1:2, :])
        m_ref[...] = (ga * of_ref[...] + gb * os_ref[...]).astype(BF16)

    return pl.pallas_call(
        body, name="gate_fwd", grid=(t // ROW_TILE,),
        in_specs=[pl.BlockSpec((ROW_TILE, D_MODEL), lambda i: (i, 0)), pl.BlockSpec((ROW_TILE, D_MODEL), lambda i: (i, 1)),
                  _vec_spec(2), _row_spec(), _row_spec()],
        out_specs=_row_spec(), out_shape=jax.ShapeDtypeStruct((t, D_MODEL), BF16),
        compiler_params=_cparams(("parallel",)),
    )(gl, gl, b_gate, of, os_)


def _gate_bwd(gl, b_gate, of, os_, dmerged):
    t = of.shape[0]

    def body(gla_ref, glb_ref, b_ref, of_ref, os_ref, dm_ref, dof_ref, dos_ref, dgla_ref, dglb_ref, db_ref):
        dm = dm_ref[...]
        ga = jax.nn.sigmoid(gla_ref[...] + b_ref[0:1, :])
        gb = jax.nn.sigmoid(glb_ref[...] + b_ref[1:2, :])
        dof_ref[...] = (dm * ga).astype(BF16)
        dos_ref[...] = (dm * gb).astype(BF16)
        dgla = dm * of_ref[...] * ga * (1.0 - ga)
        dglb = dm * os_ref[...] * gb * (1.0 - gb)
        dgla_ref[...] = dgla.astype(BF16)
        dglb_ref[...] = dglb.astype(BF16)

        @pl.when(pl.program_id(0) == 0)
        def _():
            db_ref[...] = jnp.zeros_like(db_ref)

        db_ref[0:1, :] += jnp.sum(dgla, axis=0, keepdims=True)
        db_ref[1:2, :] += jnp.sum(dglb, axis=0, keepdims=True)

    outs = pl.pallas_call(
        body, name="gate_bwd", grid=(t // ROW_TILE,),
        in_specs=[pl.BlockSpec((ROW_TILE, D_MODEL), lambda i: (i, 0)), pl.BlockSpec((ROW_TILE, D_MODEL), lambda i: (i, 1)),
                  _vec_spec(2), _row_spec(), _row_spec(), _row_spec()],
        out_specs=[_row_spec(), _row_spec(), _row_spec(), _row_spec(), _vec_spec(2)],
        out_shape=[jax.ShapeDtypeStruct((t, D_MODEL), BF16)] * 4 + [jax.ShapeDtypeStruct((2, D_MODEL), F32)],
        compiler_params=_cparams(("arbitrary",)),
    )(gl, gl, b_gate, of, os_, dmerged)
    return outs


def _head_and_loss(x2, gpre, pe, g_final, target):
    t = x2.shape[0]

    def body(x2_ref, gpre_ref, pe_ref, g_ref, tgt_ref, dx3_ref, dpre_ref, dpe_ref, dg_ref, loss_ref):
        gp = jax.nn.sigmoid(gpre_ref[...])
        pe_t = pe_ref[...]
        x3 = x2_ref[...] + gp * pe_t
        g = g_ref[...]
        xh, _ = _xhat(x3)
        err = xh * g - tgt_ref[...]
        dy = err * (1.0 / D_MODEL)
        dx3, dg = _rms_bwd_rows(dy, x3, g)
        dx3_ref[...] = dx3
        dpre_ref[...] = (dx3 * pe_t * gp * (1.0 - gp)).astype(BF16)
        dpe_ref[...] = (dx3 * gp).astype(BF16)

        @pl.when(pl.program_id(0) == 0)
        def _():
            dg_ref[...] = jnp.zeros_like(dg_ref)
            loss_ref[...] = jnp.zeros_like(loss_ref)

        dg_ref[...] += dg
        loss_ref[...] += 0.5 * jnp.sum(jnp.mean(err * err, axis=-1, keepdims=True), axis=0, keepdims=True)

    return pl.pallas_call(
        body, name="head_and_loss", grid=(t // ROW_TILE,),
        in_specs=[_row_spec(), _row_spec(), _row_spec(), _vec_spec(), _row_spec()],
        out_specs=[_row_spec(), _row_spec(), _row_spec(), _vec_spec(), _vec_spec(1, LANES)],
        out_shape=[jax.ShapeDtypeStruct((t, D_MODEL), F32), jax.ShapeDtypeStruct((t, D_MODEL), BF16),
                   jax.ShapeDtypeStruct((t, D_MODEL), BF16), jax.ShapeDtypeStruct((1, D_MODEL), F32),
                   jax.ShapeDtypeStruct((1, LANES), F32)],
        compiler_params=_cparams(("arbitrary",)),
    )(x2, gpre, pe, g_final, target)


def _split3(v):
    hi = v.astype(BF16)
    r1 = v - hi.astype(F32)
    mid = r1.astype(BF16)
    lo = (r1 - mid.astype(F32)).astype(BF16)
    return hi, mid, lo


def _split2(v):
    hi = v.astype(BF16)
    return hi, (v - hi.astype(F32)).astype(BF16)


def _dot(a, b, dims=_DIMS["nn"]):
    return lax.dot_general(a, b, dims, preferred_element_type=F32)


def _tri(n, rel):
    row = lax.broadcasted_iota(jnp.int32, (n, n), 0)
    col = lax.broadcasted_iota(jnp.int32, (n, n), 1)
    return rel(row, col).astype(BF16)


def _log_sigmoid(v):
    return -(jnp.maximum(-v, 0.0) + jnp.log(1.0 + jnp.exp(-jnp.abs(v))))


def _fox_prep(fl, b_forget, batch, seq):
    nb = seq // ATT_BLOCK

    def body(fl_ref, b_ref, cw_ref, cr_ref):
        col = lax.broadcasted_iota(jnp.int32, (ATT_BLOCK, F_PAD), 1)
        lower = _tri(ATT_BLOCK, lambda r, c: c <= r)
        upper = _tri(ATT_BLOCK, lambda r, c: r <= c)
        expand = (lax.broadcasted_iota(jnp.int32, (F_PAD, D_ATT), 1) // HEAD_DIM
                  == lax.broadcasted_iota(jnp.int32, (F_PAD, D_ATT), 0)).astype(BF16)
        carry_w = jnp.zeros((1, D_ATT), F32)
        carry_r = jnp.zeros((F_PAD, 1), F32)
        for i in range(nb):
            blk = slice(i * ATT_BLOCK, (i + 1) * ATT_BLOCK)
            logf = jnp.where(col < N_HEADS, _log_sigmoid(fl_ref[blk, :] + b_ref[...]), 0.0)
            cw = jnp.zeros((ATT_BLOCK, D_ATT), F32) + carry_w
            cr = jnp.zeros((F_PAD, ATT_BLOCK), F32) + carry_r
            for part in _split3(logf):
                cw += _dot(lower, _dot(part, expand).astype(BF16))
                cr += _dot(part, upper, TN_DIMS)
            cw_ref[blk, :] = cw
            cr_ref[:, blk] = cr[0:N_HEADS, :]
            carry_w = cw[ATT_BLOCK - 1:ATT_BLOCK, :]
            carry_r = cr[:, ATT_BLOCK - 1:ATT_BLOCK]

    return pl.pallas_call(
        body, name="fox_prep", grid=(batch,),
        in_specs=[pl.BlockSpec((seq, F_PAD), lambda b: (b, 0)), pl.BlockSpec((1, F_PAD), lambda b: (0, 0))],
        out_specs=[pl.BlockSpec((seq, D_ATT), lambda b: (b, 0)), pl.BlockSpec((N_HEADS, seq), lambda b: (b, 0))],
        out_shape=[jax.ShapeDtypeStruct((batch * seq, D_ATT), F32), jax.ShapeDtypeStruct((batch * N_HEADS, seq), F32)],
        compiler_params=_cparams(("parallel",)),
    )(fl, b_forget)


def _fox_post(dcs_wide, drs_wide, fl, b_forget, batch, seq):
    nb = seq // ATT_BLOCK

    def body(dcs_ref, drs_ref, fl_ref, b_ref, dfl_ref, db_ref):
        pick = (lax.broadcasted_iota(jnp.int32, (D_ATT, F_PAD), 0)
                == lax.broadcasted_iota(jnp.int32, (D_ATT, F_PAD), 1) * HEAD_DIM).astype(BF16)
        upper = _tri(ATT_BLOCK, lambda r, c: r <= c)
        col = lax.broadcasted_iota(jnp.int32, (ATT_BLOCK, F_PAD), 1)

        @pl.when(pl.program_id(0) == 0)
        def _():
            db_ref[...] = jnp.zeros_like(db_ref)

        carry = jnp.zeros((1, F_PAD), F32)
        for i in reversed(range(nb)):
            blk = slice(i * ATT_BLOCK, (i + 1) * ATT_BLOCK)
            narrow = jnp.zeros((ATT_BLOCK, F_PAD), F32)
            for part in _split3(drs_ref[blk, :] - dcs_ref[blk, :]):
                narrow += _dot(part, pick)
            after = jnp.zeros((ATT_BLOCK, F_PAD), F32) + carry
            for part in _split3(narrow):
                after += _dot(upper, part)
            carry = after[0:1, :]
            pre = fl_ref[blk, :] + b_ref[...]
            dfl = jnp.where(col < N_HEADS, after * jax.nn.sigmoid(-pre), 0.0)
            dfl_ref[blk, :] = dfl.astype(BF16)
            db_ref[...] += jnp.sum(dfl, axis=0, keepdims=True)

    return pl.pallas_call(
        body, name="fox_post", grid=(batch,),
        in_specs=[pl.BlockSpec((seq, D_ATT), lambda b: (b, 0)), pl.BlockSpec((seq, D_ATT), lambda b: (b, 0)),
                  pl.BlockSpec((seq, F_PAD), lambda b: (b, 0)), pl.BlockSpec((1, F_PAD), lambda b: (0, 0))],
        out_specs=[pl.BlockSpec((seq, F_PAD), lambda b: (b, 0)), pl.BlockSpec((1, F_PAD), lambda b: (0, 0))],
        out_shape=[jax.ShapeDtypeStruct((batch * seq, F_PAD), BF16), jax.ShapeDtypeStruct((1, F_PAD), F32)],
        compiler_params=_cparams(("arbitrary",)),
    )(dcs_wide, drs_wide, fl, b_forget)


N_PAIRS = N_HEADS // 2


def _att_specs(seq, col0):
    nq = seq // ATT_BLOCK
    q = pl.BlockSpec((ATT_BLOCK, LANES), lambda b, hp, qi: (b * nq + qi, col0 + hp))
    k = pl.BlockSpec((seq, LANES), lambda b, hp, qi: (b, col0 + N_PAIRS + hp))
    v = pl.BlockSpec((seq, LANES), lambda b, hp, qi: (b, col0 + 2 * N_PAIRS + hp))
    return q, k, v


def _qblock_spec(seq):
    nq = seq // ATT_BLOCK
    return pl.BlockSpec((ATT_BLOCK, LANES), lambda b, hp, qi: (b * nq + qi, hp))


def _kv_out_spec(seq):
    return pl.BlockSpec((seq, LANES), lambda b, hp, qi: (b, hp))


def _head_masks():
    lane = lax.broadcasted_iota(jnp.int32, (1, LANES), 1)
    return [(lane >= HEAD_DIM * j) & (lane < HEAD_DIM * (j + 1)) for j in range(2)]


def _block_ids():
    row = lax.broadcasted_iota(jnp.int32, (ATT_BLOCK, ATT_BLOCK), 0)
    col = lax.broadcasted_iota(jnp.int32, (ATT_BLOCK, ATT_BLOCK), 1)
    return row, col


def _fox_fwd(qkv, c_wide, c_row, batch, seq):
    nq = seq // ATT_BLOCK
    tb = ATT_BLOCK

    def body(q_ref, k_ref, v_ref, cw_ref, cr_ref, o_ref, lse_ref):
        hp, qi = pl.program_id(1), pl.program_id(2)
        masks = _head_masks()
        row, col = _block_ids()
        q = q_ref[...]
        cw = cw_ref[...]
        res_o, res_l = [], []
        for j in range(2):
            qm = jnp.where(masks[j], q, jnp.zeros_like(q))
            ct = cw[:, HEAD_DIM * j:HEAD_DIM * j + 1]
            h = 2 * hp + j

            def step(kb, carry, diagonal):
                m, l, acc = carry
                k0 = pl.multiple_of(kb * tb, tb)
                kblk = k_ref[pl.ds(k0, tb), :]
                vblk = v_ref[pl.ds(k0, tb), :]
                s = _dot(qm, kblk, NT_DIMS) * SCALE + ct - cr_ref[pl.ds(h, 1), pl.ds(k0, tb)]
                if diagonal:
                    s = jnp.where(col <= row, s, NEG)
                m_new = jnp.maximum(m, jnp.max(s, axis=1, keepdims=True))
                p = jnp.exp(s - m_new)
                alpha = jnp.exp(m - m_new)
                l = alpha * l + jnp.sum(p, axis=1, keepdims=True)
                acc = alpha * acc + _dot(p.astype(BF16), vblk)
                return m_new, l, acc

            init = (jnp.full((tb, 1), NEG, F32), jnp.zeros((tb, 1), F32), jnp.zeros((tb, LANES), F32))
            carry = step(qi, init, True)
            m, l, acc = lax.fori_loop(0, qi, lambda kb, c: step(kb, c, False), carry)
            res_o.append(acc / l)
            res_l.append(m + jnp.log(l))
        o_ref[...] = jnp.where(masks[0], res_o[0], res_o[1]).astype(BF16)
        lse_ref[...] = jnp.where(masks[0], res_l[0], res_l[1])

    q_spec, k_spec, v_spec = _att_specs(seq, 0)
    return pl.pallas_call(
        body, name="fox_fwd", grid=(batch, N_PAIRS, nq),
        in_specs=[q_spec, k_spec, v_spec, _qblock_spec(seq), pl.BlockSpec((N_HEADS, seq), lambda b, hp, qi: (b, 0))],
        out_specs=[_qblock_spec(seq), _qblock_spec(seq)],
        out_shape=[jax.ShapeDtypeStruct((batch * seq, D_ATT), BF16), jax.ShapeDtypeStruct((batch * seq, D_ATT), F32)],
        compiler_params=_cparams(("parallel", "parallel", "arbitrary")),
    )(qkv, qkv, qkv, c_wide, c_row)


def _fox_bwd(qkv, c_wide, c_row, o, do, lse_wide, batch, seq):
    nq = seq // ATT_BLOCK
    tb = ATT_BLOCK

    def body(q_ref, k_ref, v_ref, cw_ref, cr_ref, o_ref, do_ref, lse_ref,
             dq_ref, dk_ref, dv_ref, dcs_ref, drs_ref, dkc_acc, dv_acc):
        hp, qi = pl.program_id(1), pl.program_id(2)

        @pl.when(qi == 0)
        def _():
            dkc_acc[...] = jnp.zeros_like(dkc_acc)
            dv_acc[...] = jnp.zeros_like(dv_acc)

        masks = _head_masks()
        row, col = _block_ids()
        q = q_ref[...]
        cw = cw_ref[...]
        do_t = do_ref[...]
        lse = lse_ref[...]
        prod = do_t.astype(F32) * o_ref[...].astype(F32)
        dq_parts, rs_parts = [], []
        for j in range(2):
            zero = jnp.zeros_like(q)
            qm = jnp.where(masks[j], q, zero)
            dom = jnp.where(masks[j], do_t, zero)
            q_and_ones = jnp.concatenate([qm, jnp.where(masks[j], jnp.ones_like(q), zero)], axis=1)
            ct = cw[:, HEAD_DIM * j:HEAD_DIM * j + 1]
            lse_j = lse[:, HEAD_DIM * j:HEAD_DIM * j + 1]
            delta = jnp.sum(jnp.where(masks[j], prod, 0.0), axis=1, keepdims=True)
            h = 2 * hp + j

            def step(kb, carry, diagonal):
                dq_acc, rs = carry
                k0 = pl.multiple_of(kb * tb, tb)
                kblk = k_ref[pl.ds(k0, tb), :]
                vblk = v_ref[pl.ds(k0, tb), :]
                s = _dot(qm, kblk, NT_DIMS) * SCALE + ct - cr_ref[pl.ds(h, 1), pl.ds(k0, tb)]
                p = jnp.exp(s - lse_j)
                if diagonal:
                    p = jnp.where(col <= row, p, 0.0)
                dp = _dot(dom, vblk, NT_DIMS)
                ds = (p * (dp - delta) * SCALE).astype(BF16)
                dkc_acc[pl.ds(k0, tb), :] += _dot(ds, q_and_ones, TN_DIMS)
                dv_acc[pl.ds(k0, tb), :] += _dot(p.astype(BF16), dom, TN_DIMS)
                return dq_acc + _dot(ds, kblk), rs + jnp.sum(ds.astype(F32), axis=1, keepdims=True)

            carry = step(qi, (jnp.zeros((tb, LANES), F32), jnp.zeros((tb, 1), F32)), True)
            dq_acc, rs = lax.fori_loop(0, qi, lambda kb, c: step(kb, c, False), carry)
            dq_parts.append(dq_acc)
            rs_parts.append(rs)
        dq_ref[...] = jnp.where(masks[0], dq_parts[0], dq_parts[1]).astype(BF16)
        drs_ref[...] = jnp.where(masks[0], rs_parts[0], rs_parts[1]) * (1.0 / SCALE)

        @pl.when(qi == nq - 1)
        def _():
            dk_ref[...] = dkc_acc[:, 0:LANES].astype(BF16)
            dcs_ref[...] = dkc_acc[:, LANES:2 * LANES] * (1.0 / SCALE)
            dv_ref[...] = dv_acc[...].astype(BF16)

    q_spec, k_spec, v_spec = _att_specs(seq, 0)
    qb = _qblock_spec(seq)
    return pl.pallas_call(
        body, name="fox_bwd", grid=(batch, N_PAIRS, nq),
        in_specs=[q_spec, k_spec, v_spec, qb, pl.BlockSpec((N_HEADS, seq), lambda b, hp, qi: (b, 0)), qb, qb, qb],
        out_specs=[qb, _kv_out_spec(seq), _kv_out_spec(seq), _kv_out_spec(seq), qb],
        out_shape=[jax.ShapeDtypeStruct((batch * seq, D_ATT), BF16)] * 3 + [jax.ShapeDtypeStruct((batch * seq, D_ATT), F32)] * 2,
        scratch_shapes=[pltpu.VMEM((seq, 2 * LANES), F32), pltpu.VMEM((seq, LANES), F32)],
        compiler_params=_cparams(("parallel", "parallel", "arbitrary")),
    )(qkv, qkv, qkv, c_wide, c_row, o, do, lse_wide)


def _sb_logits(qm, kblk):
    z = _dot(qm, kblk, NT_DIMS) * SCALE
    lsn = -(jnp.maximum(z, 0.0) + jnp.log(1.0 + jnp.exp(-jnp.abs(z))))
    return lsn + z, lsn


def _sb_fwd(qkv, batch, seq):
    nq = seq // ATT_BLOCK
    tb = ATT_BLOCK

    def body(q_ref, k_ref, v_ref, o_ref, rt_ref):
        qi = pl.program_id(2)
        masks = _head_masks()
        row, col = _block_ids()
        later = _tri(tb, lambda r, c: r > c)
        q = q_ref[...]
        res_o, res_r = [], []
        for j in range(2):
            qm = jnp.where(masks[j], q, jnp.zeros_like(q))

            def step(kb, carry, diagonal):
                run, acc = carry
                k0 = pl.multiple_of(kb * tb, tb)
                ls, lsn = _sb_logits(qm, k_ref[pl.ds(k0, tb), :])
                if diagonal:
                    lsn = jnp.where(col < row, lsn, 0.0)
                hi, lo = _split2(lsn)
                tail = _dot(hi, later) + _dot(lo, later) + run
                w = jnp.exp(ls + tail)
                if diagonal:
                    w = jnp.where(col < row, w, 0.0)
                acc = acc + _dot(w.astype(BF16), v_ref[pl.ds(k0, tb), :])
                return run + jnp.sum(lsn, axis=1, keepdims=True), acc

            carry = step(qi, (jnp.zeros((tb, 1), F32), jnp.zeros((tb, LANES), F32)), True)
            run, acc = lax.fori_loop(0, qi, lambda i, c: step(qi - 1 - i, c, False), carry)
            res_o.append(acc)
            res_r.append(run)
        o_ref[...] = jnp.where(masks[0], res_o[0], res_o[1]).astype(BF16)
        rt_ref[...] = jnp.where(masks[0], res_r[0], res_r[1])

    q_spec, k_spec, v_spec = _att_specs(seq, 3 * N_PAIRS)
    return pl.pallas_call(
        body, name="sb_fwd", grid=(batch, N_PAIRS, nq),
        in_specs=[q_spec, k_spec, v_spec], out_specs=[_qblock_spec(seq), _qblock_spec(seq)],
        out_shape=[jax.ShapeDtypeStruct((batch * seq, D_ATT), BF16), jax.ShapeDtypeStruct((batch * seq, D_ATT), F32)],
        compiler_params=_cparams(("parallel", "parallel", "arbitrary")),
    )(qkv, qkv, qkv)


def _sb_bwd(qkv, do, rt_wide, batch, seq):
    nq = seq // ATT_BLOCK
    tb = ATT_BLOCK

    def body(q_ref, k_ref, v_ref, do_ref, rt_ref, dq_ref, dk_ref, dv_ref, dk_acc, dv_acc):
        qi = pl.program_id(2)

        @pl.when(qi == 0)
        def _():
            dk_acc[...] = jnp.zeros_like(dk_acc)
            dv_acc[...] = jnp.zeros_like(dv_acc)

        masks = _head_masks()
        row, col = _block_ids()
        later = _tri(tb, lambda r, c: r > c)
        earlier = _tri(tb, lambda r, c: r < c)
        q = q_ref[...]
        do_t = do_ref[...]
        rt = rt_ref[...]
        dq_parts = []
        for j in range(2):
            zero = jnp.zeros_like(q)
            qm = jnp.where(masks[j], q, zero)
            dom = jnp.where(masks[j], do_t, zero)
            rt_j = rt[:, HEAD_DIM * j:HEAD_DIM * j + 1]

            def step(kb, carry, diagonal):
                pref, epre, dq_acc = carry
                k0 = pl.multiple_of(kb * tb, tb)
                kblk = k_ref[pl.ds(k0, tb), :]
                vblk = v_ref[pl.ds(k0, tb), :]
                ls, lsn_all = _sb_logits(qm, kblk)
                lsn = jnp.where(col < row, lsn_all, 0.0) if diagonal else lsn_all
                rs = jnp.sum(lsn, axis=1, keepdims=True)
                hi, lo = _split2(lsn)
                tail = _dot(hi, later) + _dot(lo, later) + (rt_j - pref - rs)
                w = jnp.exp(ls + tail)
                if diagonal:
                    w = jnp.where(col < row, w, 0.0)
                e = w * _dot(dom, vblk, NT_DIMS)
                ehi, elo = _split2(e)
                before = _dot(ehi, earlier) + _dot(elo, earlier) + epre
                dz = e * jnp.exp(lsn_all) - jnp.exp(ls) * before
                if diagonal:
                    dz = jnp.where(col < row, dz, 0.0)
                dz = (dz * SCALE).astype(BF16)
                dk_acc[pl.ds(k0, tb), :] += _dot(dz, qm, TN_DIMS)
                dv_acc[pl.ds(k0, tb), :] += _dot(w.astype(BF16), dom, TN_DIMS)
                return pref + rs, epre + jnp.sum(e, axis=1, keepdims=True), dq_acc + _dot(dz, kblk)

            init = (jnp.zeros((tb, 1), F32), jnp.zeros((tb, 1), F32), jnp.zeros((tb, LANES), F32))
            carry = lax.fori_loop(0, qi, lambda kb, c: step(kb, c, False), init)
            dq_parts.append(step(qi, carry, True)[2])
        dq_ref[...] = jnp.where(masks[0], dq_parts[0], dq_parts[1]).astype(BF16)

        @pl.when(qi == nq - 1)
        def _():
            dk_ref[...] = dk_acc[...].astype(BF16)
            dv_ref[...] = dv_acc[...].astype(BF16)

    q_spec, k_spec, v_spec = _att_specs(seq, 3 * N_PAIRS)
    qb = _qblock_spec(seq)
    return pl.pallas_call(
        body, name="sb_bwd", grid=(batch, N_PAIRS, nq),
        in_specs=[q_spec, k_spec, v_spec, qb, qb],
        out_specs=[qb, _kv_out_spec(seq), _kv_out_spec(seq)],
        out_shape=[jax.ShapeDtypeStruct((batch * seq, D_ATT), BF16)] * 3,
        scratch_shapes=[pltpu.VMEM((seq, LANES), F32), pltpu.VMEM((seq, LANES), F32)],
        compiler_params=_cparams(("parallel", "parallel", "arbitrary")),
    )(qkv, qkv, qkv, do, rt_wide)


def _local_step(x, p, target, w, vec):
    batch, seq, _ = x.shape
    t = batch * seq
    x = x.reshape(t, D_MODEL)
    target = target.reshape(t, D_MODEL)
    p = p.reshape(t, D_PLE)
    big = dict(tm=1024, tn=1024, tk=1024)

    h1 = _norm_fwd(x, vec["g_mix"], "norm_mix")
    qkv = _mm(h1, w["qkv"], mode="nn", name="proj_qkv", out_dtype=BF16, **big)
    gl = _mm(h1, w["gate"], mode="nn", name="proj_gate", **big)
    fl = _mm(h1, w["forget"], mode="nn", name="proj_forget", **big)
    c_wide, c_row = _fox_prep(fl, vec["b_forget"], batch, seq)
    o_fox, lse_wide = _fox_fwd(qkv, c_wide, c_row, batch, seq)
    o_sb, rt_wide = _sb_fwd(qkv, batch, seq)
    of = _mm(o_fox, w["branch_fox"], mode="nn", name="branch_fox", col_shards=True, **big)
    os_ = _mm(o_sb, w["branch_sb"], mode="nn", name="branch_sb", col_shards=True, **big)
    merged = _gate_fwd(gl, w["b_gate"], of, os_)
    x1 = _mm(merged, w["out"], mode="nn", name="proj_out", add=x, **big)
    h2 = _norm_fwd(x1, vec["g_mlp"], "norm_mlp")
    ar = _mm(h2, w["up"], mode="nn", name="mlp_up", out_dtype=BF16, epi=lambda acc, _: jnp.maximum(acc, 0.0),
             col_shards=True, **big)
    x2 = _mm(ar, w["down"], mode="nn", name="mlp_down", a_fn=_relu2, add=x1, **big)
    h3 = _norm_fwd(x2, vec["g_ple"], "norm_ple")
    gpre = _mm(h3, w["ple_gate"], mode="nn", name="ple_gate", **big)
    pe = _mm(p, w["ple"], mode="nn", name="ple_embed", col_shards=True, **big)

    dx3, dpre, dpe, dg_final, loss = _head_and_loss(x2, gpre, pe, vec["g_final"], target)
    gw = {}
    gw["ple"] = _mm(p, dpe, mode="tn", name="d_w_ple", col_shards=True, **big)
    gw["ple_gate"] = _mm(h3, dpre, mode="tn", name="d_w_ple_gate", **big)
    dh3 = _mm(dpre, w["ple_gate"], mode="nt", name="d_h_ple", **big)
    dx2, dx2b, dg_ple = _norm_bwd(x2, vec["g_ple"], dh3, dx3, "norm_ple_bwd")
    gw["down"] = _mm(ar, dx2b, mode="tn", name="d_w_down", a_fn=_relu2, **big)
    da = _mm(dx2b, w["down"], mode="nt", name="d_act", out_dtype=BF16,
             epi=lambda acc, r: acc * (2.0 * r.astype(F32)), extra=ar, **big)
    gw["up"] = _mm(h2, da, mode="tn", name="d_w_up", col_shards=True, **big)
    dh2 = _mm(da, w["up"], mode="nt", name="d_h_mlp", col_shards=True, **big)
    dx1, dx1b, dg_mlp = _norm_bwd(x1, vec["g_mlp"], dh2, dx2, "norm_mlp_bwd")
    gw["out"] = _mm(merged, dx1b, mode="tn", name="d_w_out", **big)
    dmerged = _mm(dx1b, w["out"], mode="nt", name="d_merged", **big)
    dof, dos, dgla, dglb, gw["b_gate"] = _gate_bwd(gl, w["b_gate"], of, os_, dmerged)
    gw["branch_fox"] = _mm(o_fox, dof, mode="tn", name="d_w_branch_fox", col_shards=True, **big)
    gw["branch_sb"] = _mm(o_sb, dos, mode="tn", name="d_w_branch_sb", col_shards=True, **big)
    do_fox = _mm(dof, w["branch_fox"], mode="nt", name="d_o_fox", out_dtype=BF16, col_shards=True, **big)
    do_sb = _mm(dos, w["branch_sb"], mode="nt", name="d_o_sb", out_dtype=BF16, col_shards=True, **big)
    dq_a, dk_a, dv_a, dcs_wide, drs_wide = _fox_bwd(qkv, c_wide, c_row, o_fox, do_fox, lse_wide, batch, seq)
    dq_b, dk_b, dv_b = _sb_bwd(qkv, do_sb, rt_wide, batch, seq)
    dfl, db_forget = _fox_post(dcs_wide, drs_wide, fl, vec["b_forget"], batch, seq)
    dqkv = jnp.concatenate([dq_a, dk_a, dv_a, dq_b, dk_b, dv_b], axis=1)
    dgl = jnp.concatenate([dgla, dglb], axis=1)
    gw["qkv"] = _mm(h1, dqkv, mode="tn", name="d_w_qkv", **big)
    gw["gate"] = _mm(h1, dgl, mode="tn", name="d_w_gate", **big)
    gw["forget"] = _mm(h1, dfl, mode="tn", name="d_w_forget", **big)
    dh1 = _mm(dqkv, w["qkv"], mode="nt", name="d_h_qkv", **big)
    dh1 = _mm(dgl, w["gate"], mode="nt", name="d_h_gate", add=dh1, **big)
    dh1 = _mm(dfl, w["forget"], mode="nt", name="d_h_forget", add=dh1, **big)
    grad_x, _, dg_mix = _norm_bwd(x, vec["g_mix"], dh1, dx1, "norm_mix_bwd")

    gvec = {"g_mix": dg_mix, "b_forget": db_forget[:, 0:N_HEADS], "g_mlp": dg_mlp, "g_ple": dg_ple,
            "g_final": dg_final}
    return loss[0, 0], grad_x.reshape(batch, seq, D_MODEL), gw, gvec


ANY = pl.BlockSpec(memory_space=pl.ANY)
SHARDED = ("w_in", "w_branch_fox", "w_branch_sb", "w_out", "w_up", "w_down", "w_ple_gate", "w_ple")
ROW_ALIGN = 16


def _place():
    return lax.axis_index("x"), lax.axis_index("y"), lax.axis_index("c")


def _other_chips(x, y):
    return [(1 - x, y), (x, 1 - y), (1 - x, 1 - y)]


def _half(ref, h):
    r = ref.shape[0] // 2
    assert r % ROW_ALIGN == 0
    return ref.at[pl.ds(pl.multiple_of(h * r, ROW_ALIGN), r)]


def _remote(src, dst, sems, idx, to):
    send_sems, recv_sems = sems
    return pltpu.make_async_remote_copy(src_ref=src, dst_ref=dst, send_sem=send_sems.at[idx], recv_sem=recv_sems.at[idx],
                                        device_id=to, device_id_type=MESH)


def _gather_weights(shards):
    n = len(shards)

    def body(*refs):
        src, out, sems = refs[:n], refs[n:2 * n], refs[2 * n:]
        x, y, c = _place()
        me = 2 * x + y
        sibling = (x, y, 1 - c)
        chips = _other_chips(x, y)
        sends = []
        for t in range(n):
            for k, (px, py) in enumerate(chips):
                sends.append(_remote(_half(src[t], c), _half(out[t].at[me], c), sems, (t, k), (px, py, c)))
        for t in range(n):
            sends.append(_remote(src[t], out[t].at[me], sems, (t, 3), sibling))
        for cp in sends:
            cp.start()
        for t in range(n):
            for k, (px, py) in enumerate(chips):
                landed = _half(out[t].at[2 * px + py], c)
                _remote(landed, landed, sems, (t, k), (px, py, c)).wait_recv()
                sends.append(_remote(landed, landed, sems, (t, 4 + k), sibling))
                sends[-1].start()
        for t in range(n):
            _remote(src[t], out[t].at[me], sems, (t, 3), sibling).wait_recv()
            for k, (px, py) in enumerate(chips):
                passed = _half(out[t].at[2 * px + py], 1 - c)
                _remote(passed, passed, sems, (t, 4 + k), sibling).wait_recv()
        for cp in sends:
            cp.wait_send()

    return pl.pallas_call(
        body, name="gather_weights", in_specs=[ANY] * n, out_specs=[ANY] * n,
        out_shape=[jax.ShapeDtypeStruct((N_CHIPS,) + s.shape, s.dtype) for s in shards],
        scratch_shapes=[pltpu.SemaphoreType.DMA((n, 7)), pltpu.SemaphoreType.DMA((n, 7))],
    )(*shards)


def _swap_halves(slots):
    n = len(slots)

    def body(*refs):
        src, out, sems = refs[:n], refs[n:2 * n], refs[2 * n:]
        x, y, c = _place()
        copies = []
        for t in range(n):
            r = src[t].shape[1] // 2
            rows = pl.ds(pl.multiple_of((1 - c) * r, ROW_ALIGN), r)
            copies.append(_remote(src[t].at[:, rows], out[t], sems, t, (x, y, 1 - c)))
        for cp in copies:
            cp.start()
        for cp in copies:
            cp.wait_recv()
        for cp in copies:
            cp.wait_send()

    return pl.pallas_call(
        body, name="reduce_swap_halves", in_specs=[ANY] * n, out_specs=[ANY] * n,
        out_shape=[jax.ShapeDtypeStruct((N_CHIPS, s.shape[1] // 2, s.shape[2]), s.dtype) for s in slots],
        scratch_shapes=[pltpu.SemaphoreType.DMA((n,)), pltpu.SemaphoreType.DMA((n,))],
    )(*slots)


def _exchange_chips(sums):
    n = len(sums)

    def body(*refs):
        src, out, sems = refs[:n], refs[n:2 * n], refs[2 * n:]
        x, y, c = _place()
        copies = []
        for t in range(n):
            for k, (px, py) in enumerate(_other_chips(x, y)):
                copies.append(_remote(src[t].at[2 * px + py], out[t].at[k], sems, (t, k), (px, py, c)))
        for cp in copies:
            cp.start()
        for cp in copies:
            cp.wait_recv()
        for cp in copies:
            cp.wait_send()

    return pl.pallas_call(
        body, name="reduce_exchange_chips", in_specs=[ANY] * n, out_specs=[ANY] * n,
        out_shape=[jax.ShapeDtypeStruct((3,) + s.shape[1:], s.dtype) for s in sums],
        scratch_shapes=[pltpu.SemaphoreType.DMA((n, 3)), pltpu.SemaphoreType.DMA((n, 3))],
    )(*sums)


def _share_halves(mine):
    n = len(mine)

    def body(*refs):
        src, out, sems = refs[:n], refs[n:2 * n], refs[2 * n:]
        x, y, c = _place()
        copies = [_remote(src[t], out[t], sems, t, (x, y, 1 - c)) for t in range(n)]
        for cp in copies:
            cp.start()
        for cp in copies:
            cp.wait_recv()
        for cp in copies:
            cp.wait_send()

    return pl.pallas_call(
        body, name="reduce_share_halves", in_specs=[ANY] * n, out_specs=[ANY] * n,
        out_shape=[jax.ShapeDtypeStruct(s.shape, s.dtype) for s in mine],
        scratch_shapes=[pltpu.SemaphoreType.DMA((n,)), pltpu.SemaphoreType.DMA((n,))],
    )(*mine)


def _half_tile(rows):
    return min(rows, 256)


def _sum_sibling(place, slot, received, name):
    n, rows2, cols = slot.shape
    rows = rows2 // 2
    tile = _half_tile(rows)
    nb = rows // tile

    def body(place_ref, a_ref, b_ref, o_ref):
        o_ref[...] = (a_ref[...] + b_ref[...]).astype(BF16)

    return pl.pallas_call(
        body, name=name, out_shape=jax.ShapeDtypeStruct((n, rows, cols), BF16),
        grid_spec=pltpu.PrefetchScalarGridSpec(
            num_scalar_prefetch=1, grid=(n, nb),
            in_specs=[pl.BlockSpec((None, tile, cols), lambda j, i, pr: (j, pr[1] * nb + i, 0)),
                      pl.BlockSpec((None, tile, cols), lambda j, i, pr: (j, i, 0))],
            out_specs=pl.BlockSpec((None, tile, cols), lambda j, i, pr: (j, i, 0))),
        compiler_params=_cparams(("parallel", "parallel")),
    )(place, slot, received)


def _sum_chips(place, slot, received, others, name):
    _, rows2, cols = slot.shape
    rows = rows2 // 2
    tile = _half_tile(rows)
    nb = rows // tile

    def body(place_ref, a_ref, b_ref, p_ref, o_ref):
        own = a_ref[...] + b_ref[...]
        o_ref[...] = ((own + p_ref[0].astype(F32)) + p_ref[1].astype(F32)) + p_ref[2].astype(F32)

    return pl.pallas_call(
        body, name=name, out_shape=jax.ShapeDtypeStruct((rows, cols), F32),
        grid_spec=pltpu.PrefetchScalarGridSpec(
            num_scalar_prefetch=1, grid=(nb,),
            in_specs=[pl.BlockSpec((None, tile, cols), lambda i, pr: (pr[0], pr[1] * nb + i, 0)),
                      pl.BlockSpec((None, tile, cols), lambda i, pr: (pr[0], i, 0)),
                      pl.BlockSpec((3, tile, cols), lambda i, pr: (0, i, 0))],
            out_specs=pl.BlockSpec((tile, cols), lambda i, pr: (i, 0))),
        compiler_params=_cparams(("parallel",)),
    )(place, slot, received, others)


def _reduce_scatter(place, slots, names):
    received = _swap_halves(slots)
    sums = [_sum_sibling(place, s, r, "sum_sibling_" + nm) for s, r, nm in zip(slots, received, names)]
    others = _exchange_chips(sums)
    mine = [_sum_chips(place, s, r, o, "sum_chips_" + nm) for s, r, o, nm in zip(slots, received, others, names)]
    return mine, _share_halves(mine)


N_DEVICES = 8


def _sum_devices(block, name):
    def body(v_ref, o_ref, land_ref, send_sems, recv_sems):
        x, y, c = _place()
        me = 4 * x + 2 * y + c
        copies = []
        for mask in range(1, N_DEVICES):
            peer = (x ^ (mask >> 2), y ^ ((mask >> 1) & 1), c ^ (mask & 1))
            copies.append(pltpu.make_async_remote_copy(src_ref=v_ref, dst_ref=land_ref.at[me], send_sem=send_sems.at[mask - 1],
                                                       recv_sem=recv_sems.at[mask - 1], device_id=peer, device_id_type=MESH))
        for cp in copies:
            cp.start()
        land_ref[me] = v_ref[...]
        for cp in copies:
            cp.wait_recv()
        total = land_ref[0]
        for d in range(1, N_DEVICES):
            total = total + land_ref[d]
        o_ref[...] = total
        for cp in copies:
            cp.wait_send()

    vmem = pl.BlockSpec(memory_space=pltpu.VMEM)
    return pl.pallas_call(
        body, name=name, in_specs=[vmem], out_specs=vmem, out_shape=jax.ShapeDtypeStruct(block.shape, F32),
        scratch_shapes=[pltpu.VMEM((N_DEVICES,) + block.shape, F32), pltpu.SemaphoreType.DMA((N_DEVICES - 1,)),
                        pltpu.SemaphoreType.DMA((N_DEVICES - 1,))],
    )(block)


def _vec_block(g_mix, g_mlp, g_ple, g_final, b_forget, b_gate_rows):
    pad = lambda a: jnp.concatenate([a, jnp.zeros((a.shape[0], D_MODEL - a.shape[1]), F32)], axis=1)
    return jnp.concatenate([g_mix, g_mlp, g_ple, g_final.reshape(1, D_MODEL), pad(b_forget), pad(b_gate_rows),
                            jnp.zeros((1, D_MODEL), F32)], axis=0)


def _adam_math(w, g, m, v):
    m_new = ADAM_B1 * m + (1.0 - ADAM_B1) * g
    v_new = ADAM_B2 * v + (1.0 - ADAM_B2) * (g * g)
    m_hat = m_new / (1.0 - ADAM_B1 ** ADAM_STEP)
    v_hat = v_new / (1.0 - ADAM_B2 ** ADAM_STEP)
    return -ADAM_LR * (m_hat / (jnp.sqrt(v_hat) + ADAM_EPS) + ADAM_WD * w), m_new, v_new


def _adamw_halves(place, w, m, v, g_mine, g_theirs, name):
    rows2, cols = w.shape
    rows = rows2 // 2
    tile = _half_tile(rows)
    nb = rows // tile

    def body(place_ref, w_ref, m_ref, v_ref, gm_ref, gt_ref, g_ref, d_ref, nm_ref, nv_ref):
        g = jnp.where(pl.program_id(0) == 0, gm_ref[...], gt_ref[...])
        g_ref[...] = g
        d_ref[...], nm_ref[...], nv_ref[...] = _adam_math(w_ref[...], g, m_ref[...], v_ref[...])

    whole = pl.BlockSpec((tile, cols), lambda s, i, pr: ((pr[1] + s - 2 * pr[1] * s) * nb + i, 0))
    half = pl.BlockSpec((tile, cols), lambda s, i, pr: (i, 0))
    return pl.pallas_call(
        body, name=name, out_shape=[jax.ShapeDtypeStruct((rows2, cols), F32)] * 4,
        grid_spec=pltpu.PrefetchScalarGridSpec(num_scalar_prefetch=1, grid=(2, nb), in_specs=[whole] * 3 + [half] * 2,
                                               out_specs=[whole] * 4),
        compiler_params=_cparams(("parallel", "parallel")),
    )(place, w, m, v, g_mine, g_theirs)


def _adamw_vec(w, g, m, v):
    def body(w_ref, g_ref, m_ref, v_ref, d_ref, nm_ref, nv_ref):
        d_ref[...], nm_ref[...], nv_ref[...] = _adam_math(w_ref[...], g_ref[...], m_ref[...], v_ref[...])

    return pl.pallas_call(body, name="adamw_vectors", out_shape=[jax.ShapeDtypeStruct(w.shape, F32)] * 3)(w, g, m, v)


WEIGHT_NAMES = ("g_mix", "w_in", "b_forget", "b_gate", "w_branch_fox", "w_branch_sb", "w_out", "g_mlp", "w_up", "w_down",
                "g_ple", "w_ple_gate", "w_ple", "g_final")
W_IN_SHARD = D_IN // N_CHIPS
Q_END, F_END, B_END = 3 * D_ATT, 3 * D_ATT + N_HEADS, 6 * D_ATT + N_HEADS
GATE_SHARD = D_MODEL // N_CHIPS


def _join_cols(slots):
    return jnp.transpose(slots, (1, 0, 2)).reshape(slots.shape[1], N_CHIPS * slots.shape[2])


def _whole_weights(gathered, b_gate):
    w_in = _join_cols(gathered["w_in"])
    forget = jnp.concatenate([w_in[:, Q_END:F_END], jnp.zeros((D_MODEL, F_PAD - N_HEADS), BF16)], axis=1)
    rows = lambda a: a.reshape(N_CHIPS * a.shape[1], a.shape[2])
    return {"qkv": jnp.concatenate([w_in[:, :Q_END], w_in[:, F_END:B_END]], axis=1), "gate": w_in[:, B_END:], "forget": forget,
            "b_gate": b_gate, "branch_fox": gathered["w_branch_fox"], "branch_sb": gathered["w_branch_sb"],
            "out": rows(gathered["w_out"]), "up": gathered["w_up"], "down": rows(gathered["w_down"]),
            "ple_gate": rows(gathered["w_ple_gate"]), "ple": gathered["w_ple"]}


def _grad_slots(gw):
    g_in = jnp.concatenate([gw["qkv"][:, :Q_END], gw["forget"][:, :N_HEADS], gw["qkv"][:, Q_END:], gw["gate"]], axis=1)
    g_in = jnp.transpose(g_in.reshape(D_MODEL, N_CHIPS, W_IN_SHARD), (1, 0, 2))
    rows = lambda a: a.reshape(N_CHIPS, a.shape[0] // N_CHIPS, a.shape[1])
    return {"w_in": g_in, "w_branch_fox": gw["branch_fox"], "w_branch_sb": gw["branch_sb"], "w_out": rows(gw["out"]),
            "w_up": gw["up"], "w_down": rows(gw["down"]), "w_ple_gate": rows(gw["ple_gate"]), "w_ple": gw["ple"]}


def kernel(x, p, g_mix, w_in, b_forget, b_gate, w_branch_fox, w_branch_sb, w_out, g_mlp, w_up, w_down, g_ple, w_ple_gate, w_ple, g_final, loss_target, m_g_mix, m_w_in, m_b_forget, m_b_gate, m_w_branch_fox, m_w_branch_sb, m_w_out, m_g_mlp, m_w_up, m_w_down, m_g_ple, m_w_ple_gate, m_w_ple, m_g_final, v_g_mix, v_w_in, v_b_forget, v_b_gate, v_w_branch_fox, v_w_branch_sb, v_w_out, v_g_mlp, v_w_up, v_w_down, v_g_ple, v_w_ple_gate, v_w_ple, v_g_final):
    weights = dict(g_mix=g_mix, w_in=w_in, b_forget=b_forget, b_gate=b_gate, w_branch_fox=w_branch_fox,
                   w_branch_sb=w_branch_sb, w_out=w_out, g_mlp=g_mlp, w_up=w_up, w_down=w_down, g_ple=g_ple,
                   w_ple_gate=w_ple_gate, w_ple=w_ple, g_final=g_final)
    first = dict(g_mix=m_g_mix, w_in=m_w_in, b_forget=m_b_forget, b_gate=m_b_gate, w_branch_fox=m_w_branch_fox,
                 w_branch_sb=m_w_branch_sb, w_out=m_w_out, g_mlp=m_g_mlp, w_up=m_w_up, w_down=m_w_down, g_ple=m_g_ple,
                 w_ple_gate=m_w_ple_gate, w_ple=m_w_ple, g_final=m_g_final)
    second = dict(g_mix=v_g_mix, w_in=v_w_in, b_forget=v_b_forget, b_gate=v_b_gate, w_branch_fox=v_w_branch_fox,
                  w_branch_sb=v_w_branch_sb, w_out=v_w_out, g_mlp=v_g_mlp, w_up=v_w_up, w_down=v_w_down, g_ple=v_g_ple,
                  w_ple_gate=v_w_ple_gate, w_ple=v_w_ple, g_final=v_g_final)
    cx, cy, cc = _place()
    chip = 2 * cx + cy
    place = jnp.stack([chip, cc]).astype(jnp.int32)
    col0 = chip * GATE_SHARD

    gathered = dict(zip(SHARDED, _gather_weights([weights[n][0].astype(BF16) for n in SHARDED])))
    gate_rows = lax.dynamic_update_slice(jnp.zeros((2, D_MODEL), F32), b_gate[0] * (cc == 0).astype(F32), (0, col0))
    zero_row = jnp.zeros((1, D_MODEL), F32)
    b_gate_whole = _sum_devices(_vec_block(zero_row, zero_row, zero_row, zero_row[0], zero_row[:, :N_HEADS], gate_rows),
                                "gather_b_gate")[5:7]
    vec = {"g_mix": g_mix, "b_forget": jnp.concatenate([b_forget, jnp.zeros((1, F_PAD - N_HEADS), F32)], axis=1),
           "g_mlp": g_mlp, "g_ple": g_ple, "g_final": g_final.reshape(1, D_MODEL)}

    loss, grad_x, gw, gvec = _local_step(x, p[0], loss_target, _whole_weights(gathered, b_gate_whole), vec)
    loss = lax.psum(loss, ("x", "y", "c"))

    slots = _grad_slots(gw)
    mine, theirs = _reduce_scatter(place, [slots[n] for n in SHARDED], SHARDED)
    out = {}
    for n, g_mine, g_theirs in zip(SHARDED, mine, theirs):
        res = _adamw_halves(place, weights[n][0], first[n][0], second[n][0], g_mine, g_theirs, "adamw_" + n)
        out[n] = [r[None] for r in res]

    g_block = _sum_devices(_vec_block(gvec["g_mix"], gvec["g_mlp"], gvec["g_ple"], gvec["g_final"][0], gvec["b_forget"],
                                      gw["b_gate"]), "reduce_vectors")
    g_gate = lax.dynamic_slice(g_block[5:7], (0, col0), (2, GATE_SHARD))
    blocks = [_vec_block(d["g_mix"], d["g_mlp"], d["g_ple"], d["g_final"], d["b_forget"], d["b_gate"][0])
              for d in (weights, first, second)]
    g_rows = jnp.concatenate([g_block[0:5], jnp.concatenate([g_gate, jnp.zeros((2, D_MODEL - GATE_SHARD), F32)], axis=1),
                              jnp.zeros((1, D_MODEL), F32)], axis=0)
    res = (g_rows,) + tuple(_adamw_vec(blocks[0], g_rows, blocks[1], blocks[2]))
    out["g_mix"] = [r[0:1] for r in res]
    out["g_mlp"] = [r[1:2] for r in res]
    out["g_ple"] = [r[2:3] for r in res]
    out["g_final"] = [r[3] for r in res]
    out["b_forget"] = [r[4:5, :N_HEADS] for r in res]
    out["b_gate"] = [r[5:7, :GATE_SHARD][None] for r in res]
    return (loss, grad_x, *[out[n][0] for n in WEIGHT_NAMES], *[out[n][1] for n in WEIGHT_NAMES],
            *[out[n][2] for n in WEIGHT_NAMES], *[out[n][3] for n in WEIGHT_NAMES])
```

```python
import jax
import jax.numpy as jnp
from jax import lax
from jax.experimental import pallas as pl
from jax.experimental.pallas import tpu as pltpu

F32 = jnp.float32
BF16 = jnp.bfloat16

D_MODEL = 1024
HEAD_DIM = 64
N_HEADS = 8
D_ATT = N_HEADS * HEAD_DIM
D_FF = 4 * D_MODEL
D_PLE = 256
D_IN = 6 * D_ATT + N_HEADS + 2 * D_MODEL
F_PAD = 128
EPS = 1e-6
SCALE = HEAD_DIM ** -0.5
N_CHIPS = 4
LANES = 128
ATT_BLOCK = 256
NEG = -1e30

ADAM_LR = 0.001
ADAM_B1 = 0.9
ADAM_B2 = 0.999
ADAM_EPS = 1e-08
ADAM_WD = 0.01
ADAM_STEP = 10

VMEM_LIMIT = 56 * 1024 * 1024

MESH = pl.DeviceIdType.MESH


def _cparams(sem=None):
    return pltpu.CompilerParams(dimension_semantics=sem, vmem_limit_bytes=VMEM_LIMIT)


def _relu2(t):
    t = t.astype(F32)
    return t * t


_DIMS = {"nn": (((1,), (0,)), ((), ())), "nt": (((1,), (1,)), ((), ())), "tn": (((0,), (0,)), ((), ()))}
NT_DIMS = _DIMS["nt"]
TN_DIMS = _DIMS["tn"]


def _mm(a, b, *, mode, name, out_dtype=F32, tm=512, tn=512, tk=512, add=None, a_fn=None, epi=None, extra=None,
        col_shards=False):
    if mode == "nn":
        (m, k), n = a.shape, b.shape[-1]
    elif mode == "nt":
        (m, k), n = a.shape, b.shape[-2]
    else:
        (k, m), n = a.shape, b.shape[1]
    shard = None
    if col_shards:
        if mode == "nn":
            shard, n = n, N_CHIPS * n
            tn = min(tn, shard)
        elif mode == "nt":
            shard = b.shape[-1]
            tk = min(tk, shard)
        else:
            shard = n // N_CHIPS
            tn = min(tn, shard)
    tm, tn, tk = min(tm, m), min(tn, n), min(tk, k)
    assert m % tm == 0 and n % tn == 0 and k % tk == 0, (name, m, n, k)
    nk = k // tk
    a_spec = {"nn": pl.BlockSpec((tm, tk), lambda i, j, kk: (i, kk)),
              "nt": pl.BlockSpec((tm, tk), lambda i, j, kk: (i, kk)),
              "tn": pl.BlockSpec((tk, tm), lambda i, j, kk: (kk, i))}[mode]
    b_spec = {"nn": pl.BlockSpec((tk, tn), lambda i, j, kk: (kk, j)),
              "nt": pl.BlockSpec((tn, tk), lambda i, j, kk: (j, kk)),
              "tn": pl.BlockSpec((tk, tn), lambda i, j, kk: (kk, j))}[mode]
    o_spec = pl.BlockSpec((tm, tn), lambda i, j, kk: (i, j))
    out_shape = (m, n)
    if col_shards and mode == "nn":
        per = shard // tn
        b_spec = pl.BlockSpec((None, tk, tn), lambda i, j, kk: (j // per, kk, j % per))
    elif col_shards and mode == "nt":
        per = shard // tk
        b_spec = pl.BlockSpec((None, tn, tk), lambda i, j, kk: (kk // per, j, kk % per))
    elif col_shards:
        assert add is None and extra is None
        per = shard // tn
        o_spec = pl.BlockSpec((None, tm, tn), lambda i, j, kk: (j // per, i, j % per))
        out_shape = (N_CHIPS, m, shard)
    operands, in_specs = [a, b], [a_spec, b_spec]
    third = add if add is not None else extra
    if third is not None:
        operands.append(third)
        in_specs.append(o_spec)

    def body(*refs):
        a_ref, b_ref = refs[0], refs[1]
        t_ref = refs[2] if third is not None else None
        o_ref = refs[3] if third is not None else refs[2]
        acc_ref = refs[-1] if nk > 1 else None
        at = a_ref[...]
        if a_fn is not None:
            at = a_fn(at)
        part = lax.dot_general(at.astype(BF16), b_ref[...].astype(BF16), _DIMS[mode], preferred_element_type=F32)

        def finish(acc):
            if epi is not None:
                acc = epi(acc, None if t_ref is None else t_ref[...])
            elif add is not None:
                acc = acc + t_ref[...].astype(F32)
            o_ref[...] = acc.astype(o_ref.dtype)

        if nk == 1:
            finish(part)
        else:
            kk = pl.program_id(2)

            @pl.when(kk == 0)
            def _():
                acc_ref[...] = part

            @pl.when(kk > 0)
            def _():
                acc_ref[...] += part

            @pl.when(kk == nk - 1)
            def _():
                finish(acc_ref[...])

    return pl.pallas_call(
        body, name=name, grid=(m // tm, n // tn, nk),
        in_specs=in_specs, out_specs=o_spec,
        out_shape=jax.ShapeDtypeStruct(out_shape, out_dtype),
        scratch_shapes=[pltpu.VMEM((tm, tn), F32)] if nk > 1 else [],
        compiler_params=_cparams(("parallel", "parallel", "arbitrary")),
    )(*operands)


ROW_TILE = 512


def _row_spec(width=D_MODEL, rows=ROW_TILE):
    return pl.BlockSpec((rows, width), lambda i: (i, 0))


def _vec_spec(rows=1, width=D_MODEL):
    return pl.BlockSpec((rows, width), lambda i: (0, 0))


def _xhat(x):
    r = lax.rsqrt(jnp.mean(x * x, axis=-1, keepdims=True) + EPS)
    return x * r, r


def _rms_bwd_rows(dh, x, g):
    xh, r = _xhat(x)
    dxh = dh * g
    dx = r * (dxh - xh * jnp.mean(dxh * xh, axis=-1, keepdims=True))
    return dx, jnp.sum(dh * xh, axis=0, keepdims=True)


def _norm_fwd(x, g, name):
    t = x.shape[0]

    def body(x_ref, g_ref, h_ref):
        xh, _ = _xhat(x_ref[...])
        h_ref[...] = (xh * g_ref[...]).astype(BF16)

    return pl.pallas_call(
        body, name=name, grid=(t // ROW_TILE,), in_specs=[_row_spec(), _vec_spec()], out_specs=_row_spec(),
        out_shape=jax.ShapeDtypeStruct((t, D_MODEL), BF16), compiler_params=_cparams(("parallel",)),
    )(x, g)


def _norm_bwd(x, g, dh, dres, name):
    t = x.shape[0]

    def body(x_ref, g_ref, dh_ref, dres_ref, dx_ref, dxb_ref, dg_ref):
        dx, dg = _rms_bwd_rows(dh_ref[...], x_ref[...], g_ref[...])
        dx = dx + dres_ref[...]
        dx_ref[...] = dx
        dxb_ref[...] = dx.astype(BF16)

        @pl.when(pl.program_id(0) == 0)
        def _():
            dg_ref[...] = jnp.zeros_like(dg_ref)

        dg_ref[...] += dg

    return pl.pallas_call(
        body, name=name, grid=(t // ROW_TILE,),
        in_specs=[_row_spec(), _vec_spec(), _row_spec(), _row_spec()],
        out_specs=[_row_spec(), _row_spec(), _vec_spec()],
        out_shape=[jax.ShapeDtypeStruct((t, D_MODEL), F32), jax.ShapeDtypeStruct((t, D_MODEL), BF16),
                   jax.ShapeDtypeStruct((1, D_MODEL), F32)],
        compiler_params=_cparams(("arbitrary",)),
    )(x, g, dh, dres)


def _gate_fwd(gl, b_gate, of, os_):
    t = of.shape[0]

    def body(gla_ref, glb_ref, b_ref, of_ref, os_ref, m_ref):
        ga = jax.nn.sigmoid(gla_ref[...] + b_ref[0:1, :])
        gb = jax.nn.sigmoid(glb_ref[...] + b_ref[1:2, :])
        m_ref[...] = (ga * of_ref[...] + gb * os_ref[...]).astype(BF16)

    return pl.pallas_call(
        body, name="gate_fwd", grid=(t // ROW_TILE,),
        in_specs=[pl.BlockSpec((ROW_TILE, D_MODEL), lambda i: (i, 0)), pl.BlockSpec((ROW_TILE, D_MODEL), lambda i: (i, 1)),
                  _vec_spec(2), _row_spec(), _row_spec()],
        out_specs=_row_spec(), out_shape=jax.ShapeDtypeStruct((t, D_MODEL), BF16),
        compiler_params=_cparams(("parallel",)),
    )(gl, gl, b_gate, of, os_)


def _gate_bwd(gl, b_gate, of, os_, dmerged):
    t = of.shape[0]

    def body(gla_ref, glb_ref, b_ref, of_ref, os_ref, dm_ref, dof_ref, dos_ref, dgla_ref, dglb_ref, db_ref):
        dm = dm_ref[...]
        ga = jax.nn.sigmoid(gla_ref[...] + b_ref[0:1, :])
        gb = jax.nn.sigmoid(glb_ref[...] + b_ref[1:2, :])
        dof_ref[...] = (dm * ga).astype(BF16)
        dos_ref[...] = (dm * gb).astype(BF16)
        dgla = dm * of_ref[...] * ga * (1.0 - ga)
        dglb = dm * os_ref[...] * gb * (1.0 - gb)
        dgla_ref[...] = dgla.astype(BF16)
        dglb_ref[...] = dglb.astype(BF16)

        @pl.when(pl.program_id(0) == 0)
        def _():
            db_ref[...] = jnp.zeros_like(db_ref)

        db_ref[0:1, :] += jnp.sum(dgla, axis=0, keepdims=True)
        db_ref[1:2, :] += jnp.sum(dglb, axis=0, keepdims=True)

    outs = pl.pallas_call(
        body, name="gate_bwd", grid=(t // ROW_TILE,),
        in_specs=[pl.BlockSpec((ROW_TILE, D_MODEL), lambda i: (i, 0)), pl.BlockSpec((ROW_TILE, D_MODEL), lambda i: (i, 1)),
                  _vec_spec(2), _row_spec(), _row_spec(), _row_spec()],
        out_specs=[_row_spec(), _row_spec(), _row_spec(), _row_spec(), _vec_spec(2)],
        out_shape=[jax.ShapeDtypeStruct((t, D_MODEL), BF16)] * 4 + [jax.ShapeDtypeStruct((2, D_MODEL), F32)],
        compiler_params=_cparams(("arbitrary",)),
    )(gl, gl, b_gate, of, os_, dmerged)
    return outs


def _head_and_loss(x2, gpre, pe, g_final, target):
    t = x2.shape[0]

    def body(x2_ref, gpre_ref, pe_ref, g_ref, tgt_ref, dx3_ref, dpre_ref, dpe_ref, dg_ref, loss_ref):
        gp = jax.nn.sigmoid(gpre_ref[...])
        pe_t = pe_ref[...]
        x3 = x2_ref[...] + gp * pe_t
        g = g_ref[...]
        xh, _ = _xhat(x3)
        err = xh * g - tgt_ref[...]
        dy = err * (1.0 / D_MODEL)
        dx3, dg = _rms_bwd_rows(dy, x3, g)
        dx3_ref[...] = dx3
        dpre_ref[...] = (dx3 * pe_t * gp * (1.0 - gp)).astype(BF16)
        dpe_ref[...] = (dx3 * gp).astype(BF16)

        @pl.when(pl.program_id(0) == 0)
        def _():
            dg_ref[...] = jnp.zeros_like(dg_ref)
            loss_ref[...] = jnp.zeros_like(loss_ref)

        dg_ref[...] += dg
        loss_ref[...] += 0.5 * jnp.sum(jnp.mean(err * err, axis=-1, keepdims=True), axis=0, keepdims=True)

    return pl.pallas_call(
        body, name="head_and_loss", grid=(t // ROW_TILE,),
        in_specs=[_row_spec(), _row_spec(), _row_spec(), _vec_spec(), _row_spec()],
        out_specs=[_row_spec(), _row_spec(), _row_spec(), _vec_spec(), _vec_spec(1, LANES)],
        out_shape=[jax.ShapeDtypeStruct((t, D_MODEL), F32), jax.ShapeDtypeStruct((t, D_MODEL), BF16),
                   jax.ShapeDtypeStruct((t, D_MODEL), BF16), jax.ShapeDtypeStruct((1, D_MODEL), F32),
                   jax.ShapeDtypeStruct((1, LANES), F32)],
        compiler_params=_cparams(("arbitrary",)),
    )(x2, gpre, pe, g_final, target)


def _split3(v):
    hi = v.astype(BF16)
    r1 = v - hi.astype(F32)
    mid = r1.astype(BF16)
    lo = (r1 - mid.astype(F32)).astype(BF16)
    return hi, mid, lo


def _split2(v):
    hi = v.astype(BF16)
    return hi, (v - hi.astype(F32)).astype(BF16)


def _dot(a, b, dims=_DIMS["nn"]):
    return lax.dot_general(a, b, dims, preferred_element_type=F32)


def _tri(n, rel):
    row = lax.broadcasted_iota(jnp.int32, (n, n), 0)
    col = lax.broadcasted_iota(jnp.int32, (n, n), 1)
    return rel(row, col).astype(BF16)


def _log_sigmoid(v):
    return -(jnp.maximum(-v, 0.0) + jnp.log(1.0 + jnp.exp(-jnp.abs(v))))


def _fox_prep(fl, b_forget, batch, seq):
    nb = seq // ATT_BLOCK

    def body(fl_ref, b_ref, cw_ref, cr_ref):
        col = lax.broadcasted_iota(jnp.int32, (ATT_BLOCK, F_PAD), 1)
        lower = _tri(ATT_BLOCK, lambda r, c: c <= r)
        upper = _tri(ATT_BLOCK, lambda r, c: r <= c)
        expand = (lax.broadcasted_iota(jnp.int32, (F_PAD, D_ATT), 1) // HEAD_DIM
                  == lax.broadcasted_iota(jnp.int32, (F_PAD, D_ATT), 0)).astype(BF16)
        carry_w = jnp.zeros((1, D_ATT), F32)
        carry_r = jnp.zeros((F_PAD, 1), F32)
        for i in range(nb):
            blk = slice(i * ATT_BLOCK, (i + 1) * ATT_BLOCK)
            logf = jnp.where(col < N_HEADS, _log_sigmoid(fl_ref[blk, :] + b_ref[...]), 0.0)
            cw = jnp.zeros((ATT_BLOCK, D_ATT), F32) + carry_w
            cr = jnp.zeros((F_PAD, ATT_BLOCK), F32) + carry_r
            for part in _split3(logf):
                cw += _dot(lower, _dot(part, expand).astype(BF16))
                cr += _dot(part, upper, TN_DIMS)
            cw_ref[blk, :] = cw
            cr_ref[:, blk] = cr[0:N_HEADS, :]
            carry_w = cw[ATT_BLOCK - 1:ATT_BLOCK, :]
            carry_r = cr[:, ATT_BLOCK - 1:ATT_BLOCK]

    return pl.pallas_call(
        body, name="fox_prep", grid=(batch,),
        in_specs=[pl.BlockSpec((seq, F_PAD), lambda b: (b, 0)), pl.BlockSpec((1, F_PAD), lambda b: (0, 0))],
        out_specs=[pl.BlockSpec((seq, D_ATT), lambda b: (b, 0)), pl.BlockSpec((N_HEADS, seq), lambda b: (b, 0))],
        out_shape=[jax.ShapeDtypeStruct((batch * seq, D_ATT), F32), jax.ShapeDtypeStruct((batch * N_HEADS, seq), F32)],
        compiler_params=_cparams(("parallel",)),
    )(fl, b_forget)


def _fox_post(dcs_wide, drs_wide, fl, b_forget, batch, seq):
    nb = seq // ATT_BLOCK

    def body(dcs_ref, drs_ref, fl_ref, b_ref, dfl_ref, db_ref):
        pick = (lax.broadcasted_iota(jnp.int32, (D_ATT, F_PAD), 0)
                == lax.broadcasted_iota(jnp.int32, (D_ATT, F_PAD), 1) * HEAD_DIM).astype(BF16)
        upper = _tri(ATT_BLOCK, lambda r, c: r <= c)
        col = lax.broadcasted_iota(jnp.int32, (ATT_BLOCK, F_PAD), 1)

        @pl.when(pl.program_id(0) == 0)
        def _():
            db_ref[...] = jnp.zeros_like(db_ref)

        carry = jnp.zeros((1, F_PAD), F32)
        for i in reversed(range(nb)):
            blk = slice(i * ATT_BLOCK, (i + 1) * ATT_BLOCK)
            narrow = jnp.zeros((ATT_BLOCK, F_PAD), F32)
            for part in _split3(drs_ref[blk, :] - dcs_ref[blk, :]):
                narrow += _dot(part, pick)
            after = jnp.zeros((ATT_BLOCK, F_PAD), F32) + carry
            for part in _split3(narrow):
                after += _dot(upper, part)
            carry = after[0:1, :]
            pre = fl_ref[blk, :] + b_ref[...]
            dfl = jnp.where(col < N_HEADS, after * jax.nn.sigmoid(-pre), 0.0)
            dfl_ref[blk, :] = dfl.astype(BF16)
            db_ref[...] += jnp.sum(dfl, axis=0, keepdims=True)

    return pl.pallas_call(
        body, name="fox_post", grid=(batch,),
        in_specs=[pl.BlockSpec((seq, D_ATT), lambda b: (b, 0)), pl.BlockSpec((seq, D_ATT), lambda b: (b, 0)),
                  pl.BlockSpec((seq, F_PAD), lambda b: (b, 0)), pl.BlockSpec((1, F_PAD), lambda b: (0, 0))],
        out_specs=[pl.BlockSpec((seq, F_PAD), lambda b: (b, 0)), pl.BlockSpec((1, F_PAD), lambda b: (0, 0))],
        out_shape=[jax.ShapeDtypeStruct((batch * seq, F_PAD), BF16), jax.ShapeDtypeStruct((1, F_PAD), F32)],
        compiler_params=_cparams(("arbitrary",)),
    )(dcs_wide, drs_wide, fl, b_forget)


N_PAIRS = N_HEADS // 2


def _att_specs(seq, col0):
    nq = seq // ATT_BLOCK
    q = pl.BlockSpec((ATT_BLOCK, LANES), lambda b, hp, qi: (b * nq + qi, col0 + hp))
    k = pl.BlockSpec((seq, LANES), lambda b, hp, qi: (b, col0 + N_PAIRS + hp))
    v = pl.BlockSpec((seq, LANES), lambda b, hp, qi: (b, col0 + 2 * N_PAIRS + hp))
    return q, k, v


def _qblock_spec(seq):
    nq = seq // ATT_BLOCK
    return pl.BlockSpec((ATT_BLOCK, LANES), lambda b, hp, qi: (b * nq + qi, hp))


def _kv_out_spec(seq):
    return pl.BlockSpec((seq, LANES), lambda b, hp, qi: (b, hp))


def _head_masks():
    lane = lax.broadcasted_iota(jnp.int32, (1, LANES), 1)
    return [(lane >= HEAD_DIM * j) & (lane < HEAD_DIM * (j + 1)) for j in range(2)]


def _stack_heads(t, masks):
    zero = jnp.zeros_like(t)
    return jnp.concatenate([jnp.where(masks[0], t, zero), jnp.where(masks[1], t, zero)], axis=0)


def _stack_cols(t):
    return jnp.concatenate([t[:, 0:1], t[:, HEAD_DIM:HEAD_DIM + 1]], axis=0)


def _unstack(t2, masks):
    tb = t2.shape[0] // 2
    return jnp.where(masks[0], t2[:tb], t2[tb:])


def _stacked_ids():
    row = lax.broadcasted_iota(jnp.int32, (2 * ATT_BLOCK, ATT_BLOCK), 0)
    col = lax.broadcasted_iota(jnp.int32, (2 * ATT_BLOCK, ATT_BLOCK), 1)
    first = lax.broadcasted_iota(jnp.int32, (2 * ATT_BLOCK, 1), 0) < ATT_BLOCK
    return jnp.where(row < ATT_BLOCK, row, row - ATT_BLOCK), col, first


def _fox_fwd(qkv, c_wide, c_row, batch, seq):
    nq = seq // ATT_BLOCK
    tb = ATT_BLOCK

    def body(q_ref, k_ref, v_ref, cw_ref, cr_ref, o_ref, lse_ref):
        hp, qi = pl.program_id(1), pl.program_id(2)
        masks = _head_masks()
        row, col, first = _stacked_ids()
        q2 = _stack_heads(q_ref[...], masks)
        ct = _stack_cols(cw_ref[...])

        def step(kb, carry, diagonal):
            m, l, acc = carry
            k0 = pl.multiple_of(kb * tb, tb)
            cs = jnp.where(first, cr_ref[pl.ds(2 * hp, 1), pl.ds(k0, tb)], cr_ref[pl.ds(2 * hp + 1, 1), pl.ds(k0, tb)])
            s = _dot(q2, k_ref[pl.ds(k0, tb), :], NT_DIMS) * SCALE + ct - cs
            if diagonal:
                s = jnp.where(col <= row, s, NEG)
            m_new = jnp.maximum(m, jnp.max(s, axis=1, keepdims=True))
            p = jnp.exp(s - m_new)
            alpha = jnp.exp(m - m_new)
            l = alpha * l + jnp.sum(p, axis=1, keepdims=True)
            acc = alpha * acc + _dot(p.astype(BF16), v_ref[pl.ds(k0, tb), :])
            return m_new, l, acc

        init = (jnp.full((2 * tb, 1), NEG, F32), jnp.zeros((2 * tb, 1), F32), jnp.zeros((2 * tb, LANES), F32))
        m, l, acc = lax.fori_loop(0, qi, lambda kb, c: step(kb, c, False), step(qi, init, True))
        o_ref[...] = _unstack(acc / l, masks).astype(BF16)
        lse_ref[...] = _unstack(m + jnp.log(l), masks)

    q_spec, k_spec, v_spec = _att_specs(seq, 0)
    return pl.pallas_call(
        body, name="fox_fwd", grid=(batch, N_PAIRS, nq),
        in_specs=[q_spec, k_spec, v_spec, _qblock_spec(seq), pl.BlockSpec((N_HEADS, seq), lambda b, hp, qi: (b, 0))],
        out_specs=[_qblock_spec(seq), _qblock_spec(seq)],
        out_shape=[jax.ShapeDtypeStruct((batch * seq, D_ATT), BF16), jax.ShapeDtypeStruct((batch * seq, D_ATT), F32)],
        compiler_params=_cparams(("parallel", "parallel", "arbitrary")),
    )(qkv, qkv, qkv, c_wide, c_row)


def _fox_bwd(qkv, c_wide, c_row, o, do, lse_wide, batch, seq):
    nq = seq // ATT_BLOCK
    tb = ATT_BLOCK

    def body(q_ref, k_ref, v_ref, cw_ref, cr_ref, o_ref, do_ref, lse_ref,
             dq_ref, dk_ref, dv_ref, dcs_ref, drs_ref, dkc_acc, dv_acc):
        hp, qi = pl.program_id(1), pl.program_id(2)

        @pl.when(qi == 0)
        def _():
            dkc_acc[...] = jnp.zeros_like(dkc_acc)
            dv_acc[...] = jnp.zeros_like(dv_acc)

        masks = _head_masks()
        row, col, first = _stacked_ids()
        q_t, do_t = q_ref[...], do_ref[...]
        q2 = _stack_heads(q_t, masks)
        do2 = _stack_heads(do_t, masks)
        q_and_ones = jnp.concatenate([q2, _stack_heads(jnp.ones_like(q_t), masks)], axis=1)
        ct = _stack_cols(cw_ref[...])
        lse = _stack_cols(lse_ref[...])
        prod = do_t.astype(F32) * o_ref[...].astype(F32)
        delta = jnp.concatenate([jnp.sum(jnp.where(mk, prod, 0.0), axis=1, keepdims=True) for mk in masks], axis=0)

        def step(kb, carry, diagonal):
            dq_acc, rs = carry
            k0 = pl.multiple_of(kb * tb, tb)
            kblk = k_ref[pl.ds(k0, tb), :]
            cs = jnp.where(first, cr_ref[pl.ds(2 * hp, 1), pl.ds(k0, tb)], cr_ref[pl.ds(2 * hp + 1, 1), pl.ds(k0, tb)])
            p = jnp.exp(_dot(q2, kblk, NT_DIMS) * SCALE + ct - cs - lse)
            if diagonal:
                p = jnp.where(col <= row, p, 0.0)
            dp = _dot(do2, v_ref[pl.ds(k0, tb), :], NT_DIMS)
            ds = (p * (dp - delta) * SCALE).astype(BF16)
            dkc_acc[pl.ds(k0, tb), :] += _dot(ds, q_and_ones, TN_DIMS)
            dv_acc[pl.ds(k0, tb), :] += _dot(p.astype(BF16), do2, TN_DIMS)
            return dq_acc + _dot(ds, kblk), rs + jnp.sum(ds.astype(F32), axis=1, keepdims=True)

        init = (jnp.zeros((2 * tb, LANES), F32), jnp.zeros((2 * tb, 1), F32))
        dq_acc, rs = lax.fori_loop(0, qi, lambda kb, c: step(kb, c, False), step(qi, init, True))
        dq_ref[...] = _unstack(dq_acc, masks).astype(BF16)
        drs_ref[...] = _unstack(rs, masks) * (1.0 / SCALE)

        @pl.when(qi == nq - 1)
        def _():
            dk_ref[...] = dkc_acc[:, 0:LANES].astype(BF16)
            dcs_ref[...] = dkc_acc[:, LANES:2 * LANES] * (1.0 / SCALE)
            dv_ref[...] = dv_acc[...].astype(BF16)

    q_spec, k_spec, v_spec = _att_specs(seq, 0)
    qb = _qblock_spec(seq)
    return pl.pallas_call(
        body, name="fox_bwd", grid=(batch, N_PAIRS, nq),
        in_specs=[q_spec, k_spec, v_spec, qb, pl.BlockSpec((N_HEADS, seq), lambda b, hp, qi: (b, 0)), qb, qb, qb],
        out_specs=[qb, _kv_out_spec(seq), _kv_out_spec(seq), _kv_out_spec(seq), qb],
        out_shape=[jax.ShapeDtypeStruct((batch * seq, D_ATT), BF16)] * 3 + [jax.ShapeDtypeStruct((batch * seq, D_ATT), F32)] * 2,
        scratch_shapes=[pltpu.VMEM((seq, 2 * LANES), F32), pltpu.VMEM((seq, LANES), F32)],
        compiler_params=_cparams(("parallel", "parallel", "arbitrary")),
    )(qkv, qkv, qkv, c_wide, c_row, o, do, lse_wide)


def _sb_logits(q2, kblk):
    z = _dot(q2, kblk, NT_DIMS) * SCALE
    lsn = -(jnp.maximum(z, 0.0) + jnp.log(1.0 + jnp.exp(-jnp.abs(z))))
    return lsn + z, lsn


def _sb_fwd(qkv, batch, seq):
    nq = seq // ATT_BLOCK
    tb = ATT_BLOCK

    def body(q_ref, k_ref, v_ref, o_ref, rt_ref):
        qi = pl.program_id(2)
        masks = _head_masks()
        row, col, _ = _stacked_ids()
        later = _tri(tb, lambda r, c: r > c)
        q2 = _stack_heads(q_ref[...], masks)

        def step(kb, carry, diagonal):
            run, acc = carry
            k0 = pl.multiple_of(kb * tb, tb)
            ls, lsn = _sb_logits(q2, k_ref[pl.ds(k0, tb), :])
            if diagonal:
                lsn = jnp.where(col < row, lsn, 0.0)
            hi, lo = _split2(lsn)
            w = jnp.exp(ls + _dot(hi, later) + _dot(lo, later) + run)
            if diagonal:
                w = jnp.where(col < row, w, 0.0)
            return run + jnp.sum(lsn, axis=1, keepdims=True), acc + _dot(w.astype(BF16), v_ref[pl.ds(k0, tb), :])

        init = (jnp.zeros((2 * tb, 1), F32), jnp.zeros((2 * tb, LANES), F32))
        run, acc = lax.fori_loop(0, qi, lambda i, c: step(qi - 1 - i, c, False), step(qi, init, True))
        o_ref[...] = _unstack(acc, masks).astype(BF16)
        rt_ref[...] = _unstack(run, masks)

    q_spec, k_spec, v_spec = _att_specs(seq, 3 * N_PAIRS)
    return pl.pallas_call(
        body, name="sb_fwd", grid=(batch, N_PAIRS, nq),
        in_specs=[q_spec, k_spec, v_spec], out_specs=[_qblock_spec(seq), _qblock_spec(seq)],
        out_shape=[jax.ShapeDtypeStruct((batch * seq, D_ATT), BF16), jax.ShapeDtypeStruct((batch * seq, D_ATT), F32)],
        compiler_params=_cparams(("parallel", "parallel", "arbitrary")),
    )(qkv, qkv, qkv)


def _sb_bwd(qkv, do, rt_wide, batch, seq):
    nq = seq // ATT_BLOCK
    tb = ATT_BLOCK

    def body(q_ref, k_ref, v_ref, do_ref, rt_ref, dq_ref, dk_ref, dv_ref, dk_acc, dv_acc):
        qi = pl.program_id(2)

        @pl.when(qi == 0)
        def _():
            dk_acc[...] = jnp.zeros_like(dk_acc)
            dv_acc[...] = jnp.zeros_like(dv_acc)

        masks = _head_masks()
        row, col, _ = _stacked_ids()
        later = _tri(tb, lambda r, c: r > c)
        earlier = _tri(tb, lambda r, c: r < c)
        q2 = _stack_heads(q_ref[...], masks)
        do2 = _stack_heads(do_ref[...], masks)
        total = _stack_cols(rt_ref[...])

        def step(kb, carry, diagonal):
            pref, epre, dq_acc = carry
            k0 = pl.multiple_of(kb * tb, tb)
            kblk = k_ref[pl.ds(k0, tb), :]
            ls, lsn_all = _sb_logits(q2, kblk)
            lsn = jnp.where(col < row, lsn_all, 0.0) if diagonal else lsn_all
            rs = jnp.sum(lsn, axis=1, keepdims=True)
            hi, lo = _split2(lsn)
            w = jnp.exp(ls + _dot(hi, later) + _dot(lo, later) + (total - pref - rs))
            if diagonal:
                w = jnp.where(col < row, w, 0.0)
            e = w * _dot(do2, v_ref[pl.ds(k0, tb), :], NT_DIMS)
            ehi, elo = _split2(e)
            before = _dot(ehi, earlier) + _dot(elo, earlier) + epre
            dz = e * jnp.exp(lsn_all) - jnp.exp(ls) * before
            if diagonal:
                dz = jnp.where(col < row, dz, 0.0)
            dz = (dz * SCALE).astype(BF16)
            dk_acc[pl.ds(k0, tb), :] += _dot(dz, q2, TN_DIMS)
            dv_acc[pl.ds(k0, tb), :] += _dot(w.astype(BF16), do2, TN_DIMS)
            return pref + rs, epre + jnp.sum(e, axis=1, keepdims=True), dq_acc + _dot(dz, kblk)

        init = (jnp.zeros((2 * tb, 1), F32), jnp.zeros((2 * tb, 1), F32), jnp.zeros((2 * tb, LANES), F32))
        carry = lax.fori_loop(0, qi, lambda kb, c: step(kb, c, False), init)
        dq_ref[...] = _unstack(step(qi, carry, True)[2], masks).astype(BF16)

        @pl.when(qi == nq - 1)
        def _():
            dk_ref[...] = dk_acc[...].astype(BF16)
            dv_ref[...] = dv_acc[...].astype(BF16)

    q_spec, k_spec, v_spec = _att_specs(seq, 3 * N_PAIRS)
    qb = _qblock_spec(seq)
    return pl.pallas_call(
        body, name="sb_bwd", grid=(batch, N_PAIRS, nq),
        in_specs=[q_spec, k_spec, v_spec, qb, qb],
        out_specs=[qb, _kv_out_spec(seq), _kv_out_spec(seq)],
        out_shape=[jax.ShapeDtypeStruct((batch * seq, D_ATT), BF16)] * 3,
        scratch_shapes=[pltpu.VMEM((seq, LANES), F32), pltpu.VMEM((seq, LANES), F32)],
        compiler_params=_cparams(("parallel", "parallel", "arbitrary")),
    )(qkv, qkv, qkv, do, rt_wide)


def _local_step(x, p, target, w, vec):
    batch, seq, _ = x.shape
    t = batch * seq
    x = x.reshape(t, D_MODEL)
    target = target.reshape(t, D_MODEL)
    p = p.reshape(t, D_PLE)
    big = dict(tm=1024, tn=1024, tk=1024)

    h1 = _norm_fwd(x, vec["g_mix"], "norm_mix")
    qkv = _mm(h1, w["qkv"], mode="nn", name="proj_qkv", out_dtype=BF16, **big)
    gl = _mm(h1, w["gate"], mode="nn", name="proj_gate", **big)
    fl = _mm(h1, w["forget"], mode="nn", name="proj_forget", **big)
    c_wide, c_row = _fox_prep(fl, vec["b_forget"], batch, seq)
    o_fox, lse_wide = _fox_fwd(qkv, c_wide, c_row, batch, seq)
    o_sb, rt_wide = _sb_fwd(qkv, batch, seq)
    of = _mm(o_fox, w["branch_fox"], mode="nn", name="branch_fox", col_shards=True, **big)
    os_ = _mm(o_sb, w["branch_sb"], mode="nn", name="branch_sb", col_shards=True, **big)
    merged = _gate_fwd(gl, w["b_gate"], of, os_)
    x1 = _mm(merged, w["out"], mode="nn", name="proj_out", add=x, **big)
    h2 = _norm_fwd(x1, vec["g_mlp"], "norm_mlp")
    ar = _mm(h2, w["up"], mode="nn", name="mlp_up", out_dtype=BF16, epi=lambda acc, _: jnp.maximum(acc, 0.0),
             col_shards=True, **big)
    x2 = _mm(ar, w["down"], mode="nn", name="mlp_down", a_fn=_relu2, add=x1, **big)
    h3 = _norm_fwd(x2, vec["g_ple"], "norm_ple")
    gpre = _mm(h3, w["ple_gate"], mode="nn", name="ple_gate", **big)
    pe = _mm(p, w["ple"], mode="nn", name="ple_embed", col_shards=True, **big)

    dx3, dpre, dpe, dg_final, loss = _head_and_loss(x2, gpre, pe, vec["g_final"], target)
    gw = {}
    gw["ple"] = _mm(p, dpe, mode="tn", name="d_w_ple", col_shards=True, **big)
    gw["ple_gate"] = _mm(h3, dpre, mode="tn", name="d_w_ple_gate", **big)
    dh3 = _mm(dpre, w["ple_gate"], mode="nt", name="d_h_ple", **big)
    dx2, dx2b, dg_ple = _norm_bwd(x2, vec["g_ple"], dh3, dx3, "norm_ple_bwd")
    gw["down"] = _mm(ar, dx2b, mode="tn", name="d_w_down", a_fn=_relu2, **big)
    da = _mm(dx2b, w["down"], mode="nt", name="d_act", out_dtype=BF16,
             epi=lambda acc, r: acc * (2.0 * r.astype(F32)), extra=ar, **big)
    gw["up"] = _mm(h2, da, mode="tn", name="d_w_up", col_shards=True, **big)
    dh2 = _mm(da, w["up"], mode="nt", name="d_h_mlp", col_shards=True, **big)
    dx1, dx1b, dg_mlp = _norm_bwd(x1, vec["g_mlp"], dh2, dx2, "norm_mlp_bwd")
    gw["out"] = _mm(merged, dx1b, mode="tn", name="d_w_out", **big)
    dmerged = _mm(dx1b, w["out"], mode="nt", name="d_merged", **big)
    dof, dos, dgla, dglb, gw["b_gate"] = _gate_bwd(gl, w["b_gate"], of, os_, dmerged)
    gw["branch_fox"] = _mm(o_fox, dof, mode="tn", name="d_w_branch_fox", col_shards=True, **big)
    gw["branch_sb"] = _mm(o_sb, dos, mode="tn", name="d_w_branch_sb", col_shards=True, **big)
    do_fox = _mm(dof, w["branch_fox"], mode="nt", name="d_o_fox", out_dtype=BF16, col_shards=True, **big)
    do_sb = _mm(dos, w["branch_sb"], mode="nt", name="d_o_sb", out_dtype=BF16, col_shards=True, **big)
    dq_a, dk_a, dv_a, dcs_wide, drs_wide = _fox_bwd(qkv, c_wide, c_row, o_fox, do_fox, lse_wide, batch, seq)
    dq_b, dk_b, dv_b = _sb_bwd(qkv, do_sb, rt_wide, batch, seq)
    dfl, db_forget = _fox_post(dcs_wide, drs_wide, fl, vec["b_forget"], batch, seq)
    dqkv = jnp.concatenate([dq_a, dk_a, dv_a, dq_b, dk_b, dv_b], axis=1)
    dgl = jnp.concatenate([dgla, dglb], axis=1)
    gw["qkv"] = _mm(h1, dqkv, mode="tn", name="d_w_qkv", **big)
    gw["gate"] = _mm(h1, dgl, mode="tn", name="d_w_gate", **big)
    gw["forget"] = _mm(h1, dfl, mode="tn", name="d_w_forget", **big)
    dh1 = _mm(dqkv, w["qkv"], mode="nt", name="d_h_qkv", **big)
    dh1 = _mm(dgl, w["gate"], mode="nt", name="d_h_gate", add=dh1, **big)
    dh1 = _mm(dfl, w["forget"], mode="nt", name="d_h_forget", add=dh1, **big)
    grad_x, _, dg_mix = _norm_bwd(x, vec["g_mix"], dh1, dx1, "norm_mix_bwd")

    gvec = {"g_mix": dg_mix, "b_forget": db_forget[:, 0:N_HEADS], "g_mlp": dg_mlp, "g_ple": dg_ple,
            "g_final": dg_final}
    return loss[0, 0], grad_x.reshape(batch, seq, D_MODEL), gw, gvec


ANY = pl.BlockSpec(memory_space=pl.ANY)
SHARDED = ("w_in", "w_branch_fox", "w_branch_sb", "w_out", "w_up", "w_down", "w_ple_gate", "w_ple")
ROW_ALIGN = 16


def _place():
    return lax.axis_index("x"), lax.axis_index("y"), lax.axis_index("c")


def _other_chips(x, y):
    return [(1 - x, y), (x, 1 - y), (1 - x, 1 - y)]


def _half(ref, h):
    r = ref.shape[0] // 2
    assert r % ROW_ALIGN == 0
    return ref.at[pl.ds(pl.multiple_of(h * r, ROW_ALIGN), r)]


def _remote(src, dst, sems, idx, to):
    send_sems, recv_sems = sems
    return pltpu.make_async_remote_copy(src_ref=src, dst_ref=dst, send_sem=send_sems.at[idx], recv_sem=recv_sems.at[idx],
                                        device_id=to, device_id_type=MESH)


def _gather_weights(shards):
    n = len(shards)

    def body(*refs):
        src, out, sems = refs[:n], refs[n:2 * n], refs[2 * n:]
        x, y, c = _place()
        me = 2 * x + y
        sibling = (x, y, 1 - c)
        chips = _other_chips(x, y)
        sends = []
        for t in range(n):
            for k, (px, py) in enumerate(chips):
                sends.append(_remote(_half(src[t], c), _half(out[t].at[me], c), sems, (t, k), (px, py, c)))
        for t in range(n):
            sends.append(_remote(src[t], out[t].at[me], sems, (t, 3), sibling))
        for cp in sends:
            cp.start()
        for t in range(n):
            for k, (px, py) in enumerate(chips):
                landed = _half(out[t].at[2 * px + py], c)
                _remote(landed, landed, sems, (t, k), (px, py, c)).wait_recv()
                sends.append(_remote(landed, landed, sems, (t, 4 + k), sibling))
                sends[-1].start()
        for t in range(n):
            _remote(src[t], out[t].at[me], sems, (t, 3), sibling).wait_recv()
            for k, (px, py) in enumerate(chips):
                passed = _half(out[t].at[2 * px + py], 1 - c)
                _remote(passed, passed, sems, (t, 4 + k), sibling).wait_recv()
        for cp in sends:
            cp.wait_send()

    return pl.pallas_call(
        body, name="gather_weights", in_specs=[ANY] * n, out_specs=[ANY] * n,
        out_shape=[jax.ShapeDtypeStruct((N_CHIPS,) + s.shape, s.dtype) for s in shards],
        scratch_shapes=[pltpu.SemaphoreType.DMA((n, 7)), pltpu.SemaphoreType.DMA((n, 7))],
    )(*shards)


def _swap_halves(slots):
    n = len(slots)

    def body(*refs):
        src, out, sems = refs[:n], refs[n:2 * n], refs[2 * n:]
        x, y, c = _place()
        copies = []
        for t in range(n):
            r = src[t].shape[1] // 2
            rows = pl.ds(pl.multiple_of((1 - c) * r, ROW_ALIGN), r)
            copies.append(_remote(src[t].at[:, rows], out[t], sems, t, (x, y, 1 - c)))
        for cp in copies:
            cp.start()
        for cp in copies:
            cp.wait_recv()
        for cp in copies:
            cp.wait_send()

    return pl.pallas_call(
        body, name="reduce_swap_halves", in_specs=[ANY] * n, out_specs=[ANY] * n,
        out_shape=[jax.ShapeDtypeStruct((N_CHIPS, s.shape[1] // 2, s.shape[2]), s.dtype) for s in slots],
        scratch_shapes=[pltpu.SemaphoreType.DMA((n,)), pltpu.SemaphoreType.DMA((n,))],
    )(*slots)


def _exchange_chips(sums):
    n = len(sums)

    def body(*refs):
        src, out, sems = refs[:n], refs[n:2 * n], refs[2 * n:]
        x, y, c = _place()
        copies = []
        for t in range(n):
            for k, (px, py) in enumerate(_other_chips(x, y)):
                copies.append(_remote(src[t].at[2 * px + py], out[t].at[k], sems, (t, k), (px, py, c)))
        for cp in copies:
            cp.start()
        for cp in copies:
            cp.wait_recv()
        for cp in copies:
            cp.wait_send()

    return pl.pallas_call(
        body, name="reduce_exchange_chips", in_specs=[ANY] * n, out_specs=[ANY] * n,
        out_shape=[jax.ShapeDtypeStruct((3,) + s.shape[1:], s.dtype) for s in sums],
        scratch_shapes=[pltpu.SemaphoreType.DMA((n, 3)), pltpu.SemaphoreType.DMA((n, 3))],
    )(*sums)


def _share_halves(mine):
    n = len(mine)

    def body(*refs):
        src, out, sems = refs[:n], refs[n:2 * n], refs[2 * n:]
        x, y, c = _place()
        copies = [_remote(src[t], out[t], sems, t, (x, y, 1 - c)) for t in range(n)]
        for cp in copies:
            cp.start()
        for cp in copies:
            cp.wait_recv()
        for cp in copies:
            cp.wait_send()

    return pl.pallas_call(
        body, name="reduce_share_halves", in_specs=[ANY] * n, out_specs=[ANY] * n,
        out_shape=[jax.ShapeDtypeStruct(s.shape, s.dtype) for s in mine],
        scratch_shapes=[pltpu.SemaphoreType.DMA((n,)), pltpu.SemaphoreType.DMA((n,))],
    )(*mine)


def _half_tile(rows):
    return min(rows, 256)


def _sum_sibling(place, slot, received, name):
    n, rows2, cols = slot.shape
    rows = rows2 // 2
    tile = _half_tile(rows)
    nb = rows // tile

    def body(place_ref, a_ref, b_ref, o_ref):
        o_ref[...] = (a_ref[...] + b_ref[...]).astype(BF16)

    return pl.pallas_call(
        body, name=name, out_shape=jax.ShapeDtypeStruct((n, rows, cols), BF16),
        grid_spec=pltpu.PrefetchScalarGridSpec(
            num_scalar_prefetch=1, grid=(n, nb),
            in_specs=[pl.BlockSpec((None, tile, cols), lambda j, i, pr: (j, pr[1] * nb + i, 0)),
                      pl.BlockSpec((None, tile, cols), lambda j, i, pr: (j, i, 0))],
            out_specs=pl.BlockSpec((None, tile, cols), lambda j, i, pr: (j, i, 0))),
        compiler_params=_cparams(("parallel", "parallel")),
    )(place, slot, received)


def _sum_chips(place, slot, received, others, name):
    _, rows2, cols = slot.shape
    rows = rows2 // 2
    tile = _half_tile(rows)
    nb = rows // tile

    def body(place_ref, a_ref, b_ref, p_ref, o_ref):
        own = a_ref[...] + b_ref[...]
        o_ref[...] = ((own + p_ref[0].astype(F32)) + p_ref[1].astype(F32)) + p_ref[2].astype(F32)

    return pl.pallas_call(
        body, name=name, out_shape=jax.ShapeDtypeStruct((rows, cols), F32),
        grid_spec=pltpu.PrefetchScalarGridSpec(
            num_scalar_prefetch=1, grid=(nb,),
            in_specs=[pl.BlockSpec((None, tile, cols), lambda i, pr: (pr[0], pr[1] * nb + i, 0)),
                      pl.BlockSpec((None, tile, cols), lambda i, pr: (pr[0], i, 0)),
                      pl.BlockSpec((3, tile, cols), lambda i, pr: (0, i, 0))],
            out_specs=pl.BlockSpec((tile, cols), lambda i, pr: (i, 0))),
        compiler_params=_cparams(("parallel",)),
    )(place, slot, received, others)


def _reduce_scatter(place, slots, names):
    received = _swap_halves(slots)
    sums = [_sum_sibling(place, s, r, "sum_sibling_" + nm) for s, r, nm in zip(slots, received, names)]
    others = _exchange_chips(sums)
    mine = [_sum_chips(place, s, r, o, "sum_chips_" + nm) for s, r, o, nm in zip(slots, received, others, names)]
    return mine, _share_halves(mine)


N_DEVICES = 8


def _sum_devices(block, name):
    def body(v_ref, o_ref, land_ref, send_sems, recv_sems):
        x, y, c = _place()
        me = 4 * x + 2 * y + c
        copies = []
        for mask in range(1, N_DEVICES):
            peer = (x ^ (mask >> 2), y ^ ((mask >> 1) & 1), c ^ (mask & 1))
            copies.append(pltpu.make_async_remote_copy(src_ref=v_ref, dst_ref=land_ref.at[me], send_sem=send_sems.at[mask - 1],
                                                       recv_sem=recv_sems.at[mask - 1], device_id=peer, device_id_type=MESH))
        for cp in copies:
            cp.start()
        land_ref[me] = v_ref[...]
        for cp in copies:
            cp.wait_recv()
        total = land_ref[0]
        for d in range(1, N_DEVICES):
            total = total + land_ref[d]
        o_ref[...] = total
        for cp in copies:
            cp.wait_send()

    vmem = pl.BlockSpec(memory_space=pltpu.VMEM)
    return pl.pallas_call(
        body, name=name, in_specs=[vmem], out_specs=vmem, out_shape=jax.ShapeDtypeStruct(block.shape, F32),
        scratch_shapes=[pltpu.VMEM((N_DEVICES,) + block.shape, F32), pltpu.SemaphoreType.DMA((N_DEVICES - 1,)),
                        pltpu.SemaphoreType.DMA((N_DEVICES - 1,))],
    )(block)


def _vec_block(g_mix, g_mlp, g_ple, g_final, b_forget, b_gate_rows):
    pad = lambda a: jnp.concatenate([a, jnp.zeros((a.shape[0], D_MODEL - a.shape[1]), F32)], axis=1)
    return jnp.concatenate([g_mix, g_mlp, g_ple, g_final.reshape(1, D_MODEL), pad(b_forget), pad(b_gate_rows),
                            jnp.zeros((1, D_MODEL), F32)], axis=0)


def _adam_math(w, g, m, v):
    m_new = ADAM_B1 * m + (1.0 - ADAM_B1) * g
    v_new = ADAM_B2 * v + (1.0 - ADAM_B2) * (g * g)
    m_hat = m_new / (1.0 - ADAM_B1 ** ADAM_STEP)
    v_hat = v_new / (1.0 - ADAM_B2 ** ADAM_STEP)
    return -ADAM_LR * (m_hat / (jnp.sqrt(v_hat) + ADAM_EPS) + ADAM_WD * w), m_new, v_new


def _adamw_halves(place, w, m, v, g_mine, g_theirs, name):
    rows2, cols = w.shape
    rows = rows2 // 2
    tile = _half_tile(rows)
    nb = rows // tile

    def body(place_ref, w_ref, m_ref, v_ref, gm_ref, gt_ref, g_ref, d_ref, nm_ref, nv_ref):
        g = jnp.where(pl.program_id(0) == 0, gm_ref[...], gt_ref[...])
        g_ref[...] = g
        d_ref[...], nm_ref[...], nv_ref[...] = _adam_math(w_ref[...], g, m_ref[...], v_ref[...])

    whole = pl.BlockSpec((tile, cols), lambda s, i, pr: ((pr[1] + s - 2 * pr[1] * s) * nb + i, 0))
    half = pl.BlockSpec((tile, cols), lambda s, i, pr: (i, 0))
    return pl.pallas_call(
        body, name=name, out_shape=[jax.ShapeDtypeStruct((rows2, cols), F32)] * 4,
        grid_spec=pltpu.PrefetchScalarGridSpec(num_scalar_prefetch=1, grid=(2, nb), in_specs=[whole] * 3 + [half] * 2,
                                               out_specs=[whole] * 4),
        compiler_params=_cparams(("parallel", "parallel")),
    )(place, w, m, v, g_mine, g_theirs)


def _adamw_vec(w, g, m, v):
    def body(w_ref, g_ref, m_ref, v_ref, d_ref, nm_ref, nv_ref):
        d_ref[...], nm_ref[...], nv_ref[...] = _adam_math(w_ref[...], g_ref[...], m_ref[...], v_ref[...])

    return pl.pallas_call(body, name="adamw_vectors", out_shape=[jax.ShapeDtypeStruct(w.shape, F32)] * 3)(w, g, m, v)


WEIGHT_NAMES = ("g_mix", "w_in", "b_forget", "b_gate", "w_branch_fox", "w_branch_sb", "w_out", "g_mlp", "w_up", "w_down",
                "g_ple", "w_ple_gate", "w_ple", "g_final")
W_IN_SHARD = D_IN // N_CHIPS
Q_END, F_END, B_END = 3 * D_ATT, 3 * D_ATT + N_HEADS, 6 * D_ATT + N_HEADS
GATE_SHARD = D_MODEL // N_CHIPS


def _join_cols(slots):
    return jnp.transpose(slots, (1, 0, 2)).reshape(slots.shape[1], N_CHIPS * slots.shape[2])


def _whole_weights(gathered, b_gate):
    w_in = _join_cols(gathered["w_in"])
    forget = jnp.concatenate([w_in[:, Q_END:F_END], jnp.zeros((D_MODEL, F_PAD - N_HEADS), BF16)], axis=1)
    rows = lambda a: a.reshape(N_CHIPS * a.shape[1], a.shape[2])
    return {"qkv": jnp.concatenate([w_in[:, :Q_END], w_in[:, F_END:B_END]], axis=1), "gate": w_in[:, B_END:], "forget": forget,
            "b_gate": b_gate, "branch_fox": gathered["w_branch_fox"], "branch_sb": gathered["w_branch_sb"],
            "out": rows(gathered["w_out"]), "up": gathered["w_up"], "down": rows(gathered["w_down"]),
            "ple_gate": rows(gathered["w_ple_gate"]), "ple": gathered["w_ple"]}


def _grad_slots(gw):
    g_in = jnp.concatenate([gw["qkv"][:, :Q_END], gw["forget"][:, :N_HEADS], gw["qkv"][:, Q_END:], gw["gate"]], axis=1)
    g_in = jnp.transpose(g_in.reshape(D_MODEL, N_CHIPS, W_IN_SHARD), (1, 0, 2))
    rows = lambda a: a.reshape(N_CHIPS, a.shape[0] // N_CHIPS, a.shape[1])
    return {"w_in": g_in, "w_branch_fox": gw["branch_fox"], "w_branch_sb": gw["branch_sb"], "w_out": rows(gw["out"]),
            "w_up": gw["up"], "w_down": rows(gw["down"]), "w_ple_gate": rows(gw["ple_gate"]), "w_ple": gw["ple"]}


def kernel(x, p, g_mix, w_in, b_forget, b_gate, w_branch_fox, w_branch_sb, w_out, g_mlp, w_up, w_down, g_ple, w_ple_gate, w_ple, g_final, loss_target, m_g_mix, m_w_in, m_b_forget, m_b_gate, m_w_branch_fox, m_w_branch_sb, m_w_out, m_g_mlp, m_w_up, m_w_down, m_g_ple, m_w_ple_gate, m_w_ple, m_g_final, v_g_mix, v_w_in, v_b_forget, v_b_gate, v_w_branch_fox, v_w_branch_sb, v_w_out, v_g_mlp, v_w_up, v_w_down, v_g_ple, v_w_ple_gate, v_w_ple, v_g_final):
    weights = dict(g_mix=g_mix, w_in=w_in, b_forget=b_forget, b_gate=b_gate, w_branch_fox=w_branch_fox,
                   w_branch_sb=w_branch_sb, w_out=w_out, g_mlp=g_mlp, w_up=w_up, w_down=w_down, g_ple=g_ple,
                   w_ple_gate=w_ple_gate, w_ple=w_ple, g_final=g_final)
    first = dict(g_mix=m_g_mix, w_in=m_w_in, b_forget=m_b_forget, b_gate=m_b_gate, w_branch_fox=m_w_branch_fox,
                 w_branch_sb=m_w_branch_sb, w_out=m_w_out, g_mlp=m_g_mlp, w_up=m_w_up, w_down=m_w_down, g_ple=m_g_ple,
                 w_ple_gate=m_w_ple_gate, w_ple=m_w_ple, g_final=m_g_final)
    second = dict(g_mix=v_g_mix, w_in=v_w_in, b_forget=v_b_forget, b_gate=v_b_gate, w_branch_fox=v_w_branch_fox,
                  w_branch_sb=v_w_branch_sb, w_out=v_w_out, g_mlp=v_g_mlp, w_up=v_w_up, w_down=v_w_down, g_ple=v_g_ple,
                  w_ple_gate=v_w_ple_gate, w_ple=v_w_ple, g_final=v_g_final)
    cx, cy, cc = _place()
    chip = 2 * cx + cy
    place = jnp.stack([chip, cc]).astype(jnp.int32)
    col0 = chip * GATE_SHARD

    gathered = dict(zip(SHARDED, _gather_weights([weights[n][0].astype(BF16) for n in SHARDED])))
    gate_rows = lax.dynamic_update_slice(jnp.zeros((2, D_MODEL), F32), b_gate[0] * (cc == 0).astype(F32), (0, col0))
    zero_row = jnp.zeros((1, D_MODEL), F32)
    b_gate_whole = _sum_devices(_vec_block(zero_row, zero_row, zero_row, zero_row[0], zero_row[:, :N_HEADS], gate_rows),
                                "gather_b_gate")[5:7]
    vec = {"g_mix": g_mix, "b_forget": jnp.concatenate([b_forget, jnp.zeros((1, F_PAD - N_HEADS), F32)], axis=1),
           "g_mlp": g_mlp, "g_ple": g_ple, "g_final": g_final.reshape(1, D_MODEL)}

    loss, grad_x, gw, gvec = _local_step(x, p[0], loss_target, _whole_weights(gathered, b_gate_whole), vec)
    loss = lax.psum(loss, ("x", "y", "c"))

    slots = _grad_slots(gw)
    mine, theirs = _reduce_scatter(place, [slots[n] for n in SHARDED], SHARDED)
    out = {}
    for n, g_mine, g_theirs in zip(SHARDED, mine, theirs):
        res = _adamw_halves(place, weights[n][0], first[n][0], second[n][0], g_mine, g_theirs, "adamw_" + n)
        out[n] = [r[None] for r in res]

    g_block = _sum_devices(_vec_block(gvec["g_mix"], gvec["g_mlp"], gvec["g_ple"], gvec["g_final"][0], gvec["b_forget"],
                                      gw["b_gate"]), "reduce_vectors")
    g_gate = lax.dynamic_slice(g_block[5:7], (0, col0), (2, GATE_SHARD))
    blocks = [_vec_block(d["g_mix"], d["g_mlp"], d["g_ple"], d["g_final"], d["b_forget"], d["b_gate"][0])
              for d in (weights, first, second)]
    g_rows = jnp.concatenate([g_block[0:5], jnp.concatenate([g_gate, jnp.zeros((2, D_MODEL - GATE_SHARD), F32)], axis=1),
                              jnp.zeros((1, D_MODEL), F32)], axis=0)
    res = (g_rows,) + tuple(_adamw_vec(blocks[0], g_rows, blocks[1], blocks[2]))
    out["g_mix"] = [r[0:1] for r in res]
    out["g_mlp"] = [r[1:2] for r in res]
    out["g_ple"] = [r[2:3] for r in res]
    out["g_final"] = [r[3] for r in res]
    out["b_forget"] = [r[4:5, :N_HEADS] for r in res]
    out["b_gate"] = [r[5:7, :GATE_SHARD][None] for r in res]
    return (loss, grad_x, *[out[n][0] for n in WEIGHT_NAMES], *[out[n][1] for n in WEIGHT_NAMES],
            *[out[n][2] for n in WEIGHT_NAMES], *[out[n][3] for n in WEIGHT_NAMES])
```

```python
import jax
import jax.numpy as jnp
from jax import lax
from jax.experimental import pallas as pl
from jax.experimental.pallas import tpu as pltpu

F32 = jnp.float32
BF16 = jnp.bfloat16

D_MODEL = 1024
HEAD_DIM = 64
N_HEADS = 8
D_ATT = N_HEADS * HEAD_DIM
D_FF = 4 * D_MODEL
D_PLE = 256
D_IN = 6 * D_ATT + N_HEADS + 2 * D_MODEL
F_PAD = 128
EPS = 1e-6
SCALE = HEAD_DIM ** -0.5
N_CHIPS = 4
LANES = 128
ATT_BLOCK = 256
FOX_TILES = (512, 512)
SB_TILES = (512, 256)
NEG = -1e30

ADAM_LR = 0.001
ADAM_B1 = 0.9
ADAM_B2 = 0.999
ADAM_EPS = 1e-08
ADAM_WD = 0.01
ADAM_STEP = 10

VMEM_LIMIT = 56 * 1024 * 1024

MESH = pl.DeviceIdType.MESH


def _cparams(sem=None):
    return pltpu.CompilerParams(dimension_semantics=sem, vmem_limit_bytes=VMEM_LIMIT)


def _relu2(t):
    t = t.astype(F32)
    return t * t


_DIMS = {"nn": (((1,), (0,)), ((), ())), "nt": (((1,), (1,)), ((), ())), "tn": (((0,), (0,)), ((), ()))}
NT_DIMS = _DIMS["nt"]
TN_DIMS = _DIMS["tn"]


def _mm(a, b, *, mode, name, out_dtype=F32, tm=512, tn=512, tk=512, add=None, a_fn=None, epi=None, extra=None,
        col_shards=False):
    if mode == "nn":
        (m, k), n = a.shape, b.shape[-1]
    elif mode == "nt":
        (m, k), n = a.shape, b.shape[-2]
    else:
        (k, m), n = a.shape, b.shape[1]
    shard = None
    if col_shards:
        if mode == "nn":
            shard, n = n, N_CHIPS * n
            tn = min(tn, shard)
        elif mode == "nt":
            shard = b.shape[-1]
            tk = min(tk, shard)
        else:
            shard = n // N_CHIPS
            tn = min(tn, shard)
    tm, tn, tk = min(tm, m), min(tn, n), min(tk, k)
    assert m % tm == 0 and n % tn == 0 and k % tk == 0, (name, m, n, k)
    nk = k // tk
    a_spec = {"nn": pl.BlockSpec((tm, tk), lambda i, j, kk: (i, kk)),
              "nt": pl.BlockSpec((tm, tk), lambda i, j, kk: (i, kk)),
              "tn": pl.BlockSpec((tk, tm), lambda i, j, kk: (kk, i))}[mode]
    b_spec = {"nn": pl.BlockSpec((tk, tn), lambda i, j, kk: (kk, j)),
              "nt": pl.BlockSpec((tn, tk), lambda i, j, kk: (j, kk)),
              "tn": pl.BlockSpec((tk, tn), lambda i, j, kk: (kk, j))}[mode]
    o_spec = pl.BlockSpec((tm, tn), lambda i, j, kk: (i, j))
    out_shape = (m, n)
    if col_shards and mode == "nn":
        per = shard // tn
        b_spec = pl.BlockSpec((None, tk, tn), lambda i, j, kk: (j // per, kk, j % per))
    elif col_shards and mode == "nt":
        per = shard // tk
        b_spec = pl.BlockSpec((None, tn, tk), lambda i, j, kk: (kk // per, j, kk % per))
    elif col_shards:
        assert add is None and extra is None
        per = shard // tn
        o_spec = pl.BlockSpec((None, tm, tn), lambda i, j, kk: (j // per, i, j % per))
        out_shape = (N_CHIPS, m, shard)
    operands, in_specs = [a, b], [a_spec, b_spec]
    third = add if add is not None else extra
    if third is not None:
        operands.append(third)
        in_specs.append(o_spec)

    def body(*refs):
        a_ref, b_ref = refs[0], refs[1]
        t_ref = refs[2] if third is not None else None
        o_ref = refs[3] if third is not None else refs[2]
        acc_ref = refs[-1] if nk > 1 else None
        at = a_ref[...]
        if a_fn is not None:
            at = a_fn(at)
        part = lax.dot_general(at.astype(BF16), b_ref[...].astype(BF16), _DIMS[mode], preferred_element_type=F32)

        def finish(acc):
            if epi is not None:
                acc = epi(acc, None if t_ref is None else t_ref[...])
            elif add is not None:
                acc = acc + t_ref[...].astype(F32)
            o_ref[...] = acc.astype(o_ref.dtype)

        if nk == 1:
            finish(part)
        else:
            kk = pl.program_id(2)

            @pl.when(kk == 0)
            def _():
                acc_ref[...] = part

            @pl.when(kk > 0)
            def _():
                acc_ref[...] += part

            @pl.when(kk == nk - 1)
            def _():
                finish(acc_ref[...])

    return pl.pallas_call(
        body, name=name, grid=(m // tm, n // tn, nk),
        in_specs=in_specs, out_specs=o_spec,
        out_shape=jax.ShapeDtypeStruct(out_shape, out_dtype),
        scratch_shapes=[pltpu.VMEM((tm, tn), F32)] if nk > 1 else [],
        compiler_params=_cparams(("parallel", "parallel", "arbitrary")),
    )(*operands)


ROW_TILE = 512


def _row_spec(width=D_MODEL, rows=ROW_TILE):
    return pl.BlockSpec((rows, width), lambda i: (i, 0))


def _vec_spec(rows=1, width=D_MODEL):
    return pl.BlockSpec((rows, width), lambda i: (0, 0))


def _xhat(x):
    r = lax.rsqrt(jnp.mean(x * x, axis=-1, keepdims=True) + EPS)
    return x * r, r


def _rms_bwd_rows(dh, x, g):
    xh, r = _xhat(x)
    dxh = dh * g
    dx = r * (dxh - xh * jnp.mean(dxh * xh, axis=-1, keepdims=True))
    return dx, jnp.sum(dh * xh, axis=0, keepdims=True)


def _norm_fwd(x, g, name):
    t = x.shape[0]

    def body(x_ref, g_ref, h_ref):
        xh, _ = _xhat(x_ref[...])
        h_ref[...] = (xh * g_ref[...]).astype(BF16)

    return pl.pallas_call(
        body, name=name, grid=(t // ROW_TILE,), in_specs=[_row_spec(), _vec_spec()], out_specs=_row_spec(),
        out_shape=jax.ShapeDtypeStruct((t, D_MODEL), BF16), compiler_params=_cparams(("parallel",)),
    )(x, g)


def _norm_bwd(x, g, dh, dres, name):
    t = x.shape[0]

    def body(x_ref, g_ref, dh_ref, dres_ref, dx_ref, dxb_ref, dg_ref):
        dx, dg = _rms_bwd_rows(dh_ref[...], x_ref[...], g_ref[...])
        dx = dx + dres_ref[...]
        dx_ref[...] = dx
        dxb_ref[...] = dx.astype(BF16)

        @pl.when(pl.program_id(0) == 0)
        def _():
            dg_ref[...] = jnp.zeros_like(dg_ref)

        dg_ref[...] += dg

    return pl.pallas_call(
        body, name=name, grid=(t // ROW_TILE,),
        in_specs=[_row_spec(), _vec_spec(), _row_spec(), _row_spec()],
        out_specs=[_row_spec(), _row_spec(), _vec_spec()],
        out_shape=[jax.ShapeDtypeStruct((t, D_MODEL), F32), jax.ShapeDtypeStruct((t, D_MODEL), BF16),
                   jax.ShapeDtypeStruct((1, D_MODEL), F32)],
        compiler_params=_cparams(("arbitrary",)),
    )(x, g, dh, dres)


def _gate_fwd(gl, b_gate, of, os_):
    t = of.shape[0]

    def body(gla_ref, glb_ref, b_ref, of_ref, os_ref, m_ref):
        ga = jax.nn.sigmoid(gla_ref[...] + b_ref[0:1, :])
        gb = jax.nn.sigmoid(glb_ref[...] + b_ref[1:2, :])
        m_ref[...] = (ga * of_ref[...] + gb * os_ref[...]).astype(BF16)

    return pl.pallas_call(
        body, name="gate_fwd", grid=(t // ROW_TILE,),
        in_specs=[pl.BlockSpec((ROW_TILE, D_MODEL), lambda i: (i, 0)), pl.BlockSpec((ROW_TILE, D_MODEL), lambda i: (i, 1)),
                  _vec_spec(2), _row_spec(), _row_spec()],
        out_specs=_row_spec(), out_shape=jax.ShapeDtypeStruct((t, D_MODEL), BF16),
        compiler_params=_cparams(("parallel",)),
    )(gl, gl, b_gate, of, os_)


def _gate_bwd(gl, b_gate, of, os_, dmerged):
    t = of.shape[0]

    def body(gla_ref, glb_ref, b_ref, of_ref, os_ref, dm_ref, dof_ref, dos_ref, dgla_ref, dglb_ref, db_ref):
        dm = dm_ref[...]
        ga = jax.nn.sigmoid(gla_ref[...] + b_ref[0:1, :])
        gb = jax.nn.sigmoid(glb_ref[...] + b_ref[1:2, :])
        dof_ref[...] = (dm * ga).astype(BF16)
        dos_ref[...] = (dm * gb).astype(BF16)
        dgla = dm * of_ref[...] * ga * (1.0 - ga)
        dglb = dm * os_ref[...] * gb * (1.0 - gb)
        dgla_ref[...] = dgla.astype(BF16)
        dglb_ref[...] = dglb.astype(BF16)

        @pl.when(pl.program_id(0) == 0)
        def _():
            db_ref[...] = jnp.zeros_like(db_ref)

        db_ref[0:1, :] += jnp.sum(dgla, axis=0, keepdims=True)
        db_ref[1:2, :] += jnp.sum(dglb, axis=0, keepdims=True)

    outs = pl.pallas_call(
        body, name="gate_bwd", grid=(t // ROW_TILE,),
        in_specs=[pl.BlockSpec((ROW_TILE, D_MODEL), lambda i: (i, 0)), pl.BlockSpec((ROW_TILE, D_MODEL), lambda i: (i, 1)),
                  _vec_spec(2), _row_spec(), _row_spec(), _row_spec()],
        out_specs=[_row_spec(), _row_spec(), _row_spec(), _row_spec(), _vec_spec(2)],
        out_shape=[jax.ShapeDtypeStruct((t, D_MODEL), BF16)] * 4 + [jax.ShapeDtypeStruct((2, D_MODEL), F32)],
        compiler_params=_cparams(("arbitrary",)),
    )(gl, gl, b_gate, of, os_, dmerged)
    return outs


def _head_and_loss(x2, gpre, pe, g_final, target):
    t = x2.shape[0]

    def body(x2_ref, gpre_ref, pe_ref, g_ref, tgt_ref, dx3_ref, dpre_ref, dpe_ref, dg_ref, loss_ref):
        gp = jax.nn.sigmoid(gpre_ref[...])
        pe_t = pe_ref[...]
        x3 = x2_ref[...] + gp * pe_t
        g = g_ref[...]
        xh, _ = _xhat(x3)
        err = xh * g - tgt_ref[...]
        dy = err * (1.0 / D_MODEL)
        dx3, dg = _rms_bwd_rows(dy, x3, g)
        dx3_ref[...] = dx3
        dpre_ref[...] = (dx3 * pe_t * gp * (1.0 - gp)).astype(BF16)
        dpe_ref[...] = (dx3 * gp).astype(BF16)

        @pl.when(pl.program_id(0) == 0)
        def _():
            dg_ref[...] = jnp.zeros_like(dg_ref)
            loss_ref[...] = jnp.zeros_like(loss_ref)

        dg_ref[...] += dg
        loss_ref[...] += 0.5 * jnp.sum(jnp.mean(err * err, axis=-1, keepdims=True), axis=0, keepdims=True)

    return pl.pallas_call(
        body, name="head_and_loss", grid=(t // ROW_TILE,),
        in_specs=[_row_spec(), _row_spec(), _row_spec(), _vec_spec(), _row_spec()],
        out_specs=[_row_spec(), _row_spec(), _row_spec(), _vec_spec(), _vec_spec(1, LANES)],
        out_shape=[jax.ShapeDtypeStruct((t, D_MODEL), F32), jax.ShapeDtypeStruct((t, D_MODEL), BF16),
                   jax.ShapeDtypeStruct((t, D_MODEL), BF16), jax.ShapeDtypeStruct((1, D_MODEL), F32),
                   jax.ShapeDtypeStruct((1, LANES), F32)],
        compiler_params=_cparams(("arbitrary",)),
    )(x2, gpre, pe, g_final, target)


def _split3(v):
    hi = v.astype(BF16)
    r1 = v - hi.astype(F32)
    mid = r1.astype(BF16)
    lo = (r1 - mid.astype(F32)).astype(BF16)
    return hi, mid, lo


def _split2(v):
    hi = v.astype(BF16)
    return hi, (v - hi.astype(F32)).astype(BF16)


def _dot(a, b, dims=_DIMS["nn"]):
    return lax.dot_general(a, b, dims, preferred_element_type=F32)


def _tri(n, rel):
    row = lax.broadcasted_iota(jnp.int32, (n, n), 0)
    col = lax.broadcasted_iota(jnp.int32, (n, n), 1)
    return rel(row, col).astype(BF16)


def _log_sigmoid(v):
    return -(jnp.maximum(-v, 0.0) + jnp.log(1.0 + jnp.exp(-jnp.abs(v))))


def _fox_prep(fl, b_forget, batch, seq):
    nb = seq // ATT_BLOCK

    def body(fl_ref, b_ref, cw_ref, cr_ref):
        col = lax.broadcasted_iota(jnp.int32, (ATT_BLOCK, F_PAD), 1)
        lower = _tri(ATT_BLOCK, lambda r, c: c <= r)
        upper = _tri(ATT_BLOCK, lambda r, c: r <= c)
        expand = (lax.broadcasted_iota(jnp.int32, (F_PAD, D_ATT), 1) // HEAD_DIM
                  == lax.broadcasted_iota(jnp.int32, (F_PAD, D_ATT), 0)).astype(BF16)
        carry_w = jnp.zeros((1, D_ATT), F32)
        carry_r = jnp.zeros((F_PAD, 1), F32)
        for i in range(nb):
            blk = slice(i * ATT_BLOCK, (i + 1) * ATT_BLOCK)
            logf = jnp.where(col < N_HEADS, _log_sigmoid(fl_ref[blk, :] + b_ref[...]), 0.0)
            cw = jnp.zeros((ATT_BLOCK, D_ATT), F32) + carry_w
            cr = jnp.zeros((F_PAD, ATT_BLOCK), F32) + carry_r
            for part in _split3(logf):
                cw += _dot(lower, _dot(part, expand).astype(BF16))
                cr += _dot(part, upper, TN_DIMS)
            cw_ref[blk, :] = cw
            cr_ref[:, blk] = cr[0:N_HEADS, :]
            carry_w = cw[ATT_BLOCK - 1:ATT_BLOCK, :]
            carry_r = cr[:, ATT_BLOCK - 1:ATT_BLOCK]

    return pl.pallas_call(
        body, name="fox_prep", grid=(batch,),
        in_specs=[pl.BlockSpec((seq, F_PAD), lambda b: (b, 0)), pl.BlockSpec((1, F_PAD), lambda b: (0, 0))],
        out_specs=[pl.BlockSpec((seq, D_ATT), lambda b: (b, 0)), pl.BlockSpec((N_HEADS, seq), lambda b: (b, 0))],
        out_shape=[jax.ShapeDtypeStruct((batch * seq, D_ATT), F32), jax.ShapeDtypeStruct((batch * N_HEADS, seq), F32)],
        compiler_params=_cparams(("parallel",)),
    )(fl, b_forget)


def _fox_post(dcs_wide, drs_wide, fl, b_forget, batch, seq):
    nb = seq // ATT_BLOCK

    def body(dcs_ref, drs_ref, fl_ref, b_ref, dfl_ref, db_ref):
        pick = (lax.broadcasted_iota(jnp.int32, (D_ATT, F_PAD), 0)
                == lax.broadcasted_iota(jnp.int32, (D_ATT, F_PAD), 1) * HEAD_DIM).astype(BF16)
        upper = _tri(ATT_BLOCK, lambda r, c: r <= c)
        col = lax.broadcasted_iota(jnp.int32, (ATT_BLOCK, F_PAD), 1)

        @pl.when(pl.program_id(0) == 0)
        def _():
            db_ref[...] = jnp.zeros_like(db_ref)

        carry = jnp.zeros((1, F_PAD), F32)
        for i in reversed(range(nb)):
            blk = slice(i * ATT_BLOCK, (i + 1) * ATT_BLOCK)
            narrow = jnp.zeros((ATT_BLOCK, F_PAD), F32)
            for part in _split3(drs_ref[blk, :] - dcs_ref[blk, :]):
                narrow += _dot(part, pick)
            after = jnp.zeros((ATT_BLOCK, F_PAD), F32) + carry
            for part in _split3(narrow):
                after += _dot(upper, part)
            carry = after[0:1, :]
            pre = fl_ref[blk, :] + b_ref[...]
            dfl = jnp.where(col < N_HEADS, after * jax.nn.sigmoid(-pre), 0.0)
            dfl_ref[blk, :] = dfl.astype(BF16)
            db_ref[...] += jnp.sum(dfl, axis=0, keepdims=True)

    return pl.pallas_call(
        body, name="fox_post", grid=(batch,),
        in_specs=[pl.BlockSpec((seq, D_ATT), lambda b: (b, 0)), pl.BlockSpec((seq, D_ATT), lambda b: (b, 0)),
                  pl.BlockSpec((seq, F_PAD), lambda b: (b, 0)), pl.BlockSpec((1, F_PAD), lambda b: (0, 0))],
        out_specs=[pl.BlockSpec((seq, F_PAD), lambda b: (b, 0)), pl.BlockSpec((1, F_PAD), lambda b: (0, 0))],
        out_shape=[jax.ShapeDtypeStruct((batch * seq, F_PAD), BF16), jax.ShapeDtypeStruct((1, F_PAD), F32)],
        compiler_params=_cparams(("arbitrary",)),
    )(dcs_wide, drs_wide, fl, b_forget)


N_PAIRS = N_HEADS // 2


def _att_specs(seq, col0, tq):
    nq = seq // tq
    q = pl.BlockSpec((tq, LANES), lambda b, hp, qi: (b * nq + qi, col0 + hp))
    k = pl.BlockSpec((seq, LANES), lambda b, hp, qi: (b, col0 + N_PAIRS + hp))
    v = pl.BlockSpec((seq, LANES), lambda b, hp, qi: (b, col0 + 2 * N_PAIRS + hp))
    return q, k, v


def _qblock_spec(seq, tq):
    nq = seq // tq
    return pl.BlockSpec((tq, LANES), lambda b, hp, qi: (b * nq + qi, hp))


def _kv_out_spec(seq):
    return pl.BlockSpec((seq, LANES), lambda b, hp, qi: (b, hp))


def _head_masks():
    lane = lax.broadcasted_iota(jnp.int32, (1, LANES), 1)
    return [(lane >= HEAD_DIM * j) & (lane < HEAD_DIM * (j + 1)) for j in range(2)]


def _stack_heads(t, masks):
    zero = jnp.zeros_like(t)
    return jnp.concatenate([jnp.where(masks[0], t, zero), jnp.where(masks[1], t, zero)], axis=0)


def _stack_cols(t):
    return jnp.concatenate([t[:, 0:1], t[:, HEAD_DIM:HEAD_DIM + 1]], axis=0)


def _unstack(t2, masks):
    tq = t2.shape[0] // 2
    return jnp.where(masks[0], t2[:tq], t2[tq:])


def _stacked_ids(tq, tk):
    row = lax.broadcasted_iota(jnp.int32, (2 * tq, tk), 0)
    col = lax.broadcasted_iota(jnp.int32, (2 * tq, tk), 1)
    first = lax.broadcasted_iota(jnp.int32, (2 * tq, 1), 0) < tq
    return col - jnp.where(row < tq, row, row - tq), first


def _sweep(qi, tq, tk, step, init, leftward):
    per = tq // tk
    whole = lambda carry: lax.fori_loop(0, per * qi, lambda i, c: step(per * qi - 1 - i if leftward else i, c, None), carry)
    crossed = [(per * qi + j, -j * tk) for j in range(per)]
    if leftward:
        carry = init
        for kb, lead in reversed(crossed):
            carry = step(kb, carry, lead)
        return whole(carry)
    carry = whole(init)
    for kb, lead in crossed:
        carry = step(kb, carry, lead)
    return carry


def _fox_fwd(qkv, c_wide, c_row, batch, seq):
    tq, tk = FOX_TILES
    nq = seq // tq

    def body(q_ref, k_ref, v_ref, cw_ref, cr_ref, o_ref, lse_ref):
        hp, qi = pl.program_id(1), pl.program_id(2)
        masks = _head_masks()
        ahead, first = _stacked_ids(tq, tk)
        q2 = _stack_heads(q_ref[...], masks)
        ct = _stack_cols(cw_ref[...])

        def step(kb, carry, lead):
            m, l, acc = carry
            k0 = pl.multiple_of(kb * tk, tk)
            cs = jnp.where(first, cr_ref[pl.ds(2 * hp, 1), pl.ds(k0, tk)], cr_ref[pl.ds(2 * hp + 1, 1), pl.ds(k0, tk)])
            s = _dot(q2, k_ref[pl.ds(k0, tk), :], NT_DIMS) * SCALE + ct - cs
            if lead is not None:
                s = jnp.where(ahead <= lead, s, NEG)
            m_new = jnp.maximum(m, jnp.max(s, axis=1, keepdims=True))
            p = jnp.exp(s - m_new)
            alpha = jnp.exp(m - m_new)
            l = alpha * l + jnp.sum(p, axis=1, keepdims=True)
            acc = alpha * acc + _dot(p.astype(BF16), v_ref[pl.ds(k0, tk), :])
            return m_new, l, acc

        init = (jnp.full((2 * tq, 1), NEG, F32), jnp.zeros((2 * tq, 1), F32), jnp.zeros((2 * tq, LANES), F32))
        m, l, acc = _sweep(qi, tq, tk, step, init, leftward=False)
        o_ref[...] = _unstack(acc / l, masks).astype(BF16)
        lse_ref[...] = _unstack(m + jnp.log(l), masks)

    q_spec, k_spec, v_spec = _att_specs(seq, 0, tq)
    qb = _qblock_spec(seq, tq)
    return pl.pallas_call(
        body, name="fox_fwd", grid=(batch, N_PAIRS, nq),
        in_specs=[q_spec, k_spec, v_spec, qb, pl.BlockSpec((N_HEADS, seq), lambda b, hp, qi: (b, 0))],
        out_specs=[qb, qb],
        out_shape=[jax.ShapeDtypeStruct((batch * seq, D_ATT), BF16), jax.ShapeDtypeStruct((batch * seq, D_ATT), F32)],
        compiler_params=_cparams(("parallel", "parallel", "arbitrary")),
    )(qkv, qkv, qkv, c_wide, c_row)


def _fox_bwd(qkv, c_wide, c_row, o, do, lse_wide, batch, seq):
    tq, tk = FOX_TILES
    nq = seq // tq

    def body(q_ref, k_ref, v_ref, cw_ref, cr_ref, o_ref, do_ref, lse_ref,
             dq_ref, dk_ref, dv_ref, dcs_ref, drs_ref, dkc_acc, dv_acc):
        hp, qi = pl.program_id(1), pl.program_id(2)

        @pl.when(qi == 0)
        def _():
            dkc_acc[...] = jnp.zeros_like(dkc_acc)
            dv_acc[...] = jnp.zeros_like(dv_acc)

        masks = _head_masks()
        ahead, first = _stacked_ids(tq, tk)
        q_t, do_t = q_ref[...], do_ref[...]
        q2 = _stack_heads(q_t, masks)
        do2 = _stack_heads(do_t, masks)
        q_and_ones = jnp.concatenate([q2, _stack_heads(jnp.ones_like(q_t), masks)], axis=1)
        ct = _stack_cols(cw_ref[...])
        lse = _stack_cols(lse_ref[...])
        prod = do_t.astype(F32) * o_ref[...].astype(F32)
        delta = jnp.concatenate([jnp.sum(jnp.where(mk, prod, 0.0), axis=1, keepdims=True) for mk in masks], axis=0)

        def step(kb, carry, lead):
            dq_acc, rs = carry
            k0 = pl.multiple_of(kb * tk, tk)
            kblk = k_ref[pl.ds(k0, tk), :]
            cs = jnp.where(first, cr_ref[pl.ds(2 * hp, 1), pl.ds(k0, tk)], cr_ref[pl.ds(2 * hp + 1, 1), pl.ds(k0, tk)])
            p = jnp.exp(_dot(q2, kblk, NT_DIMS) * SCALE + ct - cs - lse)
            if lead is not None:
                p = jnp.where(ahead <= lead, p, 0.0)
            dp = _dot(do2, v_ref[pl.ds(k0, tk), :], NT_DIMS)
            ds = (p * (dp - delta) * SCALE).astype(BF16)
            dkc_acc[pl.ds(k0, tk), :] += _dot(ds, q_and_ones, TN_DIMS)
            dv_acc[pl.ds(k0, tk), :] += _dot(p.astype(BF16), do2, TN_DIMS)
            return dq_acc + _dot(ds, kblk), rs + jnp.sum(ds.astype(F32), axis=1, keepdims=True)

        init = (jnp.zeros((2 * tq, LANES), F32), jnp.zeros((2 * tq, 1), F32))
        dq_acc, rs = _sweep(qi, tq, tk, step, init, leftward=False)
        dq_ref[...] = _unstack(dq_acc, masks).astype(BF16)
        drs_ref[...] = _unstack(rs, masks) * (1.0 / SCALE)

        @pl.when(qi == nq - 1)
        def _():
            dk_ref[...] = dkc_acc[:, 0:LANES].astype(BF16)
            dcs_ref[...] = dkc_acc[:, LANES:2 * LANES] * (1.0 / SCALE)
            dv_ref[...] = dv_acc[...].astype(BF16)

    q_spec, k_spec, v_spec = _att_specs(seq, 0, tq)
    qb = _qblock_spec(seq, tq)
    return pl.pallas_call(
        body, name="fox_bwd", grid=(batch, N_PAIRS, nq),
        in_specs=[q_spec, k_spec, v_spec, qb, pl.BlockSpec((N_HEADS, seq), lambda b, hp, qi: (b, 0)), qb, qb, qb],
        out_specs=[qb, _kv_out_spec(seq), _kv_out_spec(seq), _kv_out_spec(seq), qb],
        out_shape=[jax.ShapeDtypeStruct((batch * seq, D_ATT), BF16)] * 3 + [jax.ShapeDtypeStruct((batch * seq, D_ATT), F32)] * 2,
        scratch_shapes=[pltpu.VMEM((seq, 2 * LANES), F32), pltpu.VMEM((seq, LANES), F32)],
        compiler_params=_cparams(("parallel", "parallel", "arbitrary")),
    )(qkv, qkv, qkv, c_wide, c_row, o, do, lse_wide)


def _sb_logits(q2, kblk):
    z = _dot(q2, kblk, NT_DIMS) * SCALE
    lsn = -(jnp.maximum(z, 0.0) + jnp.log(1.0 + jnp.exp(-jnp.abs(z))))
    return lsn + z, lsn


def _sb_fwd(qkv, batch, seq):
    tq, tk = SB_TILES
    nq = seq // tq

    def body(q_ref, k_ref, v_ref, o_ref, rt_ref):
        qi = pl.program_id(2)
        masks = _head_masks()
        ahead, _ = _stacked_ids(tq, tk)
        later = _tri(tk, lambda r, c: r > c)
        q2 = _stack_heads(q_ref[...], masks)

        def step(kb, carry, lead):
            run, acc = carry
            k0 = pl.multiple_of(kb * tk, tk)
            ls, lsn = _sb_logits(q2, k_ref[pl.ds(k0, tk), :])
            if lead is not None:
                lsn = jnp.where(ahead < lead, lsn, 0.0)
            hi, lo = _split2(lsn)
            w = jnp.exp(ls + _dot(hi, later) + _dot(lo, later) + run)
            if lead is not None:
                w = jnp.where(ahead < lead, w, 0.0)
            return run + jnp.sum(lsn, axis=1, keepdims=True), acc + _dot(w.astype(BF16), v_ref[pl.ds(k0, tk), :])

        init = (jnp.zeros((2 * tq, 1), F32), jnp.zeros((2 * tq, LANES), F32))
        run, acc = _sweep(qi, tq, tk, step, init, leftward=True)
        o_ref[...] = _unstack(acc, masks).astype(BF16)
        rt_ref[...] = _unstack(run, masks)

    q_spec, k_spec, v_spec = _att_specs(seq, 3 * N_PAIRS, tq)
    qb = _qblock_spec(seq, tq)
    return pl.pallas_call(
        body, name="sb_fwd", grid=(batch, N_PAIRS, nq),
        in_specs=[q_spec, k_spec, v_spec], out_specs=[qb, qb],
        out_shape=[jax.ShapeDtypeStruct((batch * seq, D_ATT), BF16), jax.ShapeDtypeStruct((batch * seq, D_ATT), F32)],
        compiler_params=_cparams(("parallel", "parallel", "arbitrary")),
    )(qkv, qkv, qkv)


def _sb_bwd(qkv, do, rt_wide, batch, seq):
    tq, tk = SB_TILES
    nq = seq // tq

    def body(q_ref, k_ref, v_ref, do_ref, rt_ref, dq_ref, dk_ref, dv_ref, dk_acc, dv_acc):
        qi = pl.program_id(2)

        @pl.when(qi == 0)
        def _():
            dk_acc[...] = jnp.zeros_like(dk_acc)
            dv_acc[...] = jnp.zeros_like(dv_acc)

        masks = _head_masks()
        ahead, _ = _stacked_ids(tq, tk)
        later = _tri(tk, lambda r, c: r > c)
        earlier = _tri(tk, lambda r, c: r < c)
        q2 = _stack_heads(q_ref[...], masks)
        do2 = _stack_heads(do_ref[...], masks)
        total = _stack_cols(rt_ref[...])

        def step(kb, carry, lead):
            pref, epre, dq_acc = carry
            k0 = pl.multiple_of(kb * tk, tk)
            kblk = k_ref[pl.ds(k0, tk), :]
            ls, lsn_all = _sb_logits(q2, kblk)
            lsn = lsn_all if lead is None else jnp.where(ahead < lead, lsn_all, 0.0)
            rs = jnp.sum(lsn, axis=1, keepdims=True)
            hi, lo = _split2(lsn)
            w = jnp.exp(ls + _dot(hi, later) + _dot(lo, later) + (total - pref - rs))
            if lead is not None:
                w = jnp.where(ahead < lead, w, 0.0)
            e = w * _dot(do2, v_ref[pl.ds(k0, tk), :], NT_DIMS)
            ehi, elo = _split2(e)
            before = _dot(ehi, earlier) + _dot(elo, earlier) + epre
            dz = e * jnp.exp(lsn_all) - jnp.exp(ls) * before
            if lead is not None:
                dz = jnp.where(ahead < lead, dz, 0.0)
            dz = (dz * SCALE).astype(BF16)
            dk_acc[pl.ds(k0, tk), :] += _dot(dz, q2, TN_DIMS)
            dv_acc[pl.ds(k0, tk), :] += _dot(w.astype(BF16), do2, TN_DIMS)
            return pref + rs, epre + jnp.sum(e, axis=1, keepdims=True), dq_acc + _dot(dz, kblk)

        init = (jnp.zeros((2 * tq, 1), F32), jnp.zeros((2 * tq, 1), F32), jnp.zeros((2 * tq, LANES), F32))
        dq_acc = _sweep(qi, tq, tk, step, init, leftward=False)[2]
        dq_ref[...] = _unstack(dq_acc, masks).astype(BF16)

        @pl.when(qi == nq - 1)
        def _():
            dk_ref[...] = dk_acc[...].astype(BF16)
            dv_ref[...] = dv_acc[...].astype(BF16)

    q_spec, k_spec, v_spec = _att_specs(seq, 3 * N_PAIRS, tq)
    qb = _qblock_spec(seq, tq)
    return pl.pallas_call(
        body, name="sb_bwd", grid=(batch, N_PAIRS, nq),
        in_specs=[q_spec, k_spec, v_spec, qb, qb],
        out_specs=[qb, _kv_out_spec(seq), _kv_out_spec(seq)],
        out_shape=[jax.ShapeDtypeStruct((batch * seq, D_ATT), BF16)] * 3,
        scratch_shapes=[pltpu.VMEM((seq, LANES), F32), pltpu.VMEM((seq, LANES), F32)],
        compiler_params=_cparams(("parallel", "parallel", "arbitrary")),
    )(qkv, qkv, qkv, do, rt_wide)


def _local_step(x, p, target, w, vec):
    batch, seq, _ = x.shape
    t = batch * seq
    x = x.reshape(t, D_MODEL)
    target = target.reshape(t, D_MODEL)
    p = p.reshape(t, D_PLE)
    big = dict(tm=1024, tn=1024, tk=1024)

    h1 = _norm_fwd(x, vec["g_mix"], "norm_mix")
    qkv = _mm(h1, w["qkv"], mode="nn", name="proj_qkv", out_dtype=BF16, **big)
    gl = _mm(h1, w["gate"], mode="nn", name="proj_gate", **big)
    fl = _mm(h1, w["forget"], mode="nn", name="proj_forget", **big)
    c_wide, c_row = _fox_prep(fl, vec["b_forget"], batch, seq)
    o_fox, lse_wide = _fox_fwd(qkv, c_wide, c_row, batch, seq)
    o_sb, rt_wide = _sb_fwd(qkv, batch, seq)
    of = _mm(o_fox, w["branch_fox"], mode="nn", name="branch_fox", col_shards=True, **big)
    os_ = _mm(o_sb, w["branch_sb"], mode="nn", name="branch_sb", col_shards=True, **big)
    merged = _gate_fwd(gl, w["b_gate"], of, os_)
    x1 = _mm(merged, w["out"], mode="nn", name="proj_out", add=x, **big)
    h2 = _norm_fwd(x1, vec["g_mlp"], "norm_mlp")
    ar = _mm(h2, w["up"], mode="nn", name="mlp_up", out_dtype=BF16, epi=lambda acc, _: jnp.maximum(acc, 0.0),
             col_shards=True, **big)
    x2 = _mm(ar, w["down"], mode="nn", name="mlp_down", a_fn=_relu2, add=x1, **big)
    h3 = _norm_fwd(x2, vec["g_ple"], "norm_ple")
    gpre = _mm(h3, w["ple_gate"], mode="nn", name="ple_gate", **big)
    pe = _mm(p, w["ple"], mode="nn", name="ple_embed", col_shards=True, **big)

    dx3, dpre, dpe, dg_final, loss = _head_and_loss(x2, gpre, pe, vec["g_final"], target)
    gw = {}
    gw["ple"] = _mm(p, dpe, mode="tn", name="d_w_ple", col_shards=True, **big)
    gw["ple_gate"] = _mm(h3, dpre, mode="tn", name="d_w_ple_gate", **big)
    dh3 = _mm(dpre, w["ple_gate"], mode="nt", name="d_h_ple", **big)
    dx2, dx2b, dg_ple = _norm_bwd(x2, vec["g_ple"], dh3, dx3, "norm_ple_bwd")
    gw["down"] = _mm(ar, dx2b, mode="tn", name="d_w_down", a_fn=_relu2, **big)
    da = _mm(dx2b, w["down"], mode="nt", name="d_act", out_dtype=BF16,
             epi=lambda acc, r: acc * (2.0 * r.astype(F32)), extra=ar, **big)
    gw["up"] = _mm(h2, da, mode="tn", name="d_w_up", col_shards=True, **big)
    dh2 = _mm(da, w["up"], mode="nt", name="d_h_mlp", col_shards=True, **big)
    dx1, dx1b, dg_mlp = _norm_bwd(x1, vec["g_mlp"], dh2, dx2, "norm_mlp_bwd")
    gw["out"] = _mm(merged, dx1b, mode="tn", name="d_w_out", **big)
    dmerged = _mm(dx1b, w["out"], mode="nt", name="d_merged", **big)
    dof, dos, dgla, dglb, gw["b_gate"] = _gate_bwd(gl, w["b_gate"], of, os_, dmerged)
    gw["branch_fox"] = _mm(o_fox, dof, mode="tn", name="d_w_branch_fox", col_shards=True, **big)
    gw["branch_sb"] = _mm(o_sb, dos, mode="tn", name="d_w_branch_sb", col_shards=True, **big)
    do_fox = _mm(dof, w["branch_fox"], mode="nt", name="d_o_fox", out_dtype=BF16, col_shards=True, **big)
    do_sb = _mm(dos, w["branch_sb"], mode="nt", name="d_o_sb", out_dtype=BF16, col_shards=True, **big)
    dq_a, dk_a, dv_a, dcs_wide, drs_wide = _fox_bwd(qkv, c_wide, c_row, o_fox, do_fox, lse_wide, batch, seq)
    dq_b, dk_b, dv_b = _sb_bwd(qkv, do_sb, rt_wide, batch, seq)
    dfl, db_forget = _fox_post(dcs_wide, drs_wide, fl, vec["b_forget"], batch, seq)
    dqkv = jnp.concatenate([dq_a, dk_a, dv_a, dq_b, dk_b, dv_b], axis=1)
    dgl = jnp.concatenate([dgla, dglb], axis=1)
    gw["qkv"] = _mm(h1, dqkv, mode="tn", name="d_w_qkv", **big)
    gw["gate"] = _mm(h1, dgl, mode="tn", name="d_w_gate", **big)
    gw["forget"] = _mm(h1, dfl, mode="tn", name="d_w_forget", **big)
    dh1 = _mm(dqkv, w["qkv"], mode="nt", name="d_h_qkv", **big)
    dh1 = _mm(dgl, w["gate"], mode="nt", name="d_h_gate", add=dh1, **big)
    dh1 = _mm(dfl, w["forget"], mode="nt", name="d_h_forget", add=dh1, **big)
    grad_x, _, dg_mix = _norm_bwd(x, vec["g_mix"], dh1, dx1, "norm_mix_bwd")

    gvec = {"g_mix": dg_mix, "b_forget": db_forget[:, 0:N_HEADS], "g_mlp": dg_mlp, "g_ple": dg_ple,
            "g_final": dg_final}
    return loss[0, 0], grad_x.reshape(batch, seq, D_MODEL), gw, gvec


ANY = pl.BlockSpec(memory_space=pl.ANY)
SHARDED = ("w_in", "w_branch_fox", "w_branch_sb", "w_out", "w_up", "w_down", "w_ple_gate", "w_ple")
ROW_ALIGN = 16


def _place():
    return lax.axis_index("x"), lax.axis_index("y"), lax.axis_index("c")


def _other_chips(x, y):
    return [(1 - x, y), (x, 1 - y), (1 - x, 1 - y)]


def _half(ref, h):
    r = ref.shape[0] // 2
    assert r % ROW_ALIGN == 0
    return ref.at[pl.ds(pl.multiple_of(h * r, ROW_ALIGN), r)]


def _remote(src, dst, sems, idx, to):
    send_sems, recv_sems = sems
    return pltpu.make_async_remote_copy(src_ref=src, dst_ref=dst, send_sem=send_sems.at[idx], recv_sem=recv_sems.at[idx],
                                        device_id=to, device_id_type=MESH)


def _gather_weights(shards):
    n = len(shards)

    def body(*refs):
        src, out, sems = refs[:n], refs[n:2 * n], refs[2 * n:]
        x, y, c = _place()
        me = 2 * x + y
        sibling = (x, y, 1 - c)
        chips = _other_chips(x, y)
        sends = []
        for t in range(n):
            for k, (px, py) in enumerate(chips):
                sends.append(_remote(_half(src[t], c), _half(out[t].at[me], c), sems, (t, k), (px, py, c)))
        for t in range(n):
            sends.append(_remote(src[t], out[t].at[me], sems, (t, 3), sibling))
        for cp in sends:
            cp.start()
        for t in range(n):
            for k, (px, py) in enumerate(chips):
                landed = _half(out[t].at[2 * px + py], c)
                _remote(landed, landed, sems, (t, k), (px, py, c)).wait_recv()
                sends.append(_remote(landed, landed, sems, (t, 4 + k), sibling))
                sends[-1].start()
        for t in range(n):
            _remote(src[t], out[t].at[me], sems, (t, 3), sibling).wait_recv()
            for k, (px, py) in enumerate(chips):
                passed = _half(out[t].at[2 * px + py], 1 - c)
                _remote(passed, passed, sems, (t, 4 + k), sibling).wait_recv()
        for cp in sends:
            cp.wait_send()

    return pl.pallas_call(
        body, name="gather_weights", in_specs=[ANY] * n, out_specs=[ANY] * n,
        out_shape=[jax.ShapeDtypeStruct((N_CHIPS,) + s.shape, s.dtype) for s in shards],
        scratch_shapes=[pltpu.SemaphoreType.DMA((n, 7)), pltpu.SemaphoreType.DMA((n, 7))],
    )(*shards)


def _swap_halves(slots):
    n = len(slots)

    def body(*refs):
        src, out, sems = refs[:n], refs[n:2 * n], refs[2 * n:]
        x, y, c = _place()
        copies = []
        for t in range(n):
            r = src[t].shape[1] // 2
            rows = pl.ds(pl.multiple_of((1 - c) * r, ROW_ALIGN), r)
            copies.append(_remote(src[t].at[:, rows], out[t], sems, t, (x, y, 1 - c)))
        for cp in copies:
            cp.start()
        for cp in copies:
            cp.wait_recv()
        for cp in copies:
            cp.wait_send()

    return pl.pallas_call(
        body, name="reduce_swap_halves", in_specs=[ANY] * n, out_specs=[ANY] * n,
        out_shape=[jax.ShapeDtypeStruct((N_CHIPS, s.shape[1] // 2, s.shape[2]), s.dtype) for s in slots],
        scratch_shapes=[pltpu.SemaphoreType.DMA((n,)), pltpu.SemaphoreType.DMA((n,))],
    )(*slots)


def _exchange_chips(sums):
    n = len(sums)

    def body(*refs):
        src, out, sems = refs[:n], refs[n:2 * n], refs[2 * n:]
        x, y, c = _place()
        copies = []
        for t in range(n):
            for k, (px, py) in enumerate(_other_chips(x, y)):
                copies.append(_remote(src[t].at[2 * px + py], out[t].at[k], sems, (t, k), (px, py, c)))
        for cp in copies:
            cp.start()
        for cp in copies:
            cp.wait_recv()
        for cp in copies:
            cp.wait_send()

    return pl.pallas_call(
        body, name="reduce_exchange_chips", in_specs=[ANY] * n, out_specs=[ANY] * n,
        out_shape=[jax.ShapeDtypeStruct((3,) + s.shape[1:], s.dtype) for s in sums],
        scratch_shapes=[pltpu.SemaphoreType.DMA((n, 3)), pltpu.SemaphoreType.DMA((n, 3))],
    )(*sums)


def _share_halves(mine):
    n = len(mine)

    def body(*refs):
        src, out, sems = refs[:n], refs[n:2 * n], refs[2 * n:]
        x, y, c = _place()
        copies = [_remote(src[t], out[t], sems, t, (x, y, 1 - c)) for t in range(n)]
        for cp in copies:
            cp.start()
        for cp in copies:
            cp.wait_recv()
        for cp in copies:
            cp.wait_send()

    return pl.pallas_call(
        body, name="reduce_share_halves", in_specs=[ANY] * n, out_specs=[ANY] * n,
        out_shape=[jax.ShapeDtypeStruct(s.shape, s.dtype) for s in mine],
        scratch_shapes=[pltpu.SemaphoreType.DMA((n,)), pltpu.SemaphoreType.DMA((n,))],
    )(*mine)


def _half_tile(rows):
    return min(rows, 256)


def _sum_sibling(place, slot, received, name):
    n, rows2, cols = slot.shape
    rows = rows2 // 2
    tile = _half_tile(rows)
    nb = rows // tile

    def body(place_ref, a_ref, b_ref, o_ref):
        o_ref[...] = (a_ref[...] + b_ref[...]).astype(BF16)

    return pl.pallas_call(
        body, name=name, out_shape=jax.ShapeDtypeStruct((n, rows, cols), BF16),
        grid_spec=pltpu.PrefetchScalarGridSpec(
            num_scalar_prefetch=1, grid=(n, nb),
            in_specs=[pl.BlockSpec((None, tile, cols), lambda j, i, pr: (j, pr[1] * nb + i, 0)),
                      pl.BlockSpec((None, tile, cols), lambda j, i, pr: (j, i, 0))],
            out_specs=pl.BlockSpec((None, tile, cols), lambda j, i, pr: (j, i, 0))),
        compiler_params=_cparams(("parallel", "parallel")),
    )(place, slot, received)


def _sum_chips(place, slot, received, others, name):
    _, rows2, cols = slot.shape
    rows = rows2 // 2
    tile = _half_tile(rows)
    nb = rows // tile

    def body(place_ref, a_ref, b_ref, p_ref, o_ref):
        own = a_ref[...] + b_ref[...]
        o_ref[...] = ((own + p_ref[0].astype(F32)) + p_ref[1].astype(F32)) + p_ref[2].astype(F32)

    return pl.pallas_call(
        body, name=name, out_shape=jax.ShapeDtypeStruct((rows, cols), F32),
        grid_spec=pltpu.PrefetchScalarGridSpec(
            num_scalar_prefetch=1, grid=(nb,),
            in_specs=[pl.BlockSpec((None, tile, cols), lambda i, pr: (pr[0], pr[1] * nb + i, 0)),
                      pl.BlockSpec((None, tile, cols), lambda i, pr: (pr[0], i, 0)),
                      pl.BlockSpec((3, tile, cols), lambda i, pr: (0, i, 0))],
            out_specs=pl.BlockSpec((tile, cols), lambda i, pr: (i, 0))),
        compiler_params=_cparams(("parallel",)),
    )(place, slot, received, others)


def _reduce_scatter(place, slots, names):
    received = _swap_halves(slots)
    sums = [_sum_sibling(place, s, r, "sum_sibling_" + nm) for s, r, nm in zip(slots, received, names)]
    others = _exchange_chips(sums)
    mine = [_sum_chips(place, s, r, o, "sum_chips_" + nm) for s, r, o, nm in zip(slots, received, others, names)]
    return mine, _share_halves(mine)


N_DEVICES = 8


def _sum_devices(block, name):
    def body(v_ref, o_ref, land_ref, send_sems, recv_sems):
        x, y, c = _place()
        me = 4 * x + 2 * y + c
        copies = []
        for mask in range(1, N_DEVICES):
            peer = (x ^ (mask >> 2), y ^ ((mask >> 1) & 1), c ^ (mask & 1))
            copies.append(pltpu.make_async_remote_copy(src_ref=v_ref, dst_ref=land_ref.at[me], send_sem=send_sems.at[mask - 1],
                                                       recv_sem=recv_sems.at[mask - 1], device_id=peer, device_id_type=MESH))
        for cp in copies:
            cp.start()
        land_ref[me] = v_ref[...]
        for cp in copies:
            cp.wait_recv()
        total = land_ref[0]
        for d in range(1, N_DEVICES):
            total = total + land_ref[d]
        o_ref[...] = total
        for cp in copies:
            cp.wait_send()

    vmem = pl.BlockSpec(memory_space=pltpu.VMEM)
    return pl.pallas_call(
        body, name=name, in_specs=[vmem], out_specs=vmem, out_shape=jax.ShapeDtypeStruct(block.shape, F32),
        scratch_shapes=[pltpu.VMEM((N_DEVICES,) + block.shape, F32), pltpu.SemaphoreType.DMA((N_DEVICES - 1,)),
                        pltpu.SemaphoreType.DMA((N_DEVICES - 1,))],
    )(block)


def _vec_block(g_mix, g_mlp, g_ple, g_final, b_forget, b_gate_rows):
    pad = lambda a: jnp.concatenate([a, jnp.zeros((a.shape[0], D_MODEL - a.shape[1]), F32)], axis=1)
    return jnp.concatenate([g_mix, g_mlp, g_ple, g_final.reshape(1, D_MODEL), pad(b_forget), pad(b_gate_rows),
                            jnp.zeros((1, D_MODEL), F32)], axis=0)


def _adam_math(w, g, m, v):
    m_new = ADAM_B1 * m + (1.0 - ADAM_B1) * g
    v_new = ADAM_B2 * v + (1.0 - ADAM_B2) * (g * g)
    m_hat = m_new / (1.0 - ADAM_B1 ** ADAM_STEP)
    v_hat = v_new / (1.0 - ADAM_B2 ** ADAM_STEP)
    return -ADAM_LR * (m_hat / (jnp.sqrt(v_hat) + ADAM_EPS) + ADAM_WD * w), m_new, v_new


def _adamw_halves(place, w, m, v, g_mine, g_theirs, name):
    rows2, cols = w.shape
    rows = rows2 // 2
    tile = _half_tile(rows)
    nb = rows // tile

    def body(place_ref, w_ref, m_ref, v_ref, gm_ref, gt_ref, g_ref, d_ref, nm_ref, nv_ref):
        g = jnp.where(pl.program_id(0) == 0, gm_ref[...], gt_ref[...])
        g_ref[...] = g
        d_ref[...], nm_ref[...], nv_ref[...] = _adam_math(w_ref[...], g, m_ref[...], v_ref[...])

    whole = pl.BlockSpec((tile, cols), lambda s, i, pr: ((pr[1] + s - 2 * pr[1] * s) * nb + i, 0))
    half = pl.BlockSpec((tile, cols), lambda s, i, pr: (i, 0))
    return pl.pallas_call(
        body, name=name, out_shape=[jax.ShapeDtypeStruct((rows2, cols), F32)] * 4,
        grid_spec=pltpu.PrefetchScalarGridSpec(num_scalar_prefetch=1, grid=(2, nb), in_specs=[whole] * 3 + [half] * 2,
                                               out_specs=[whole] * 4),
        compiler_params=_cparams(("parallel", "parallel")),
    )(place, w, m, v, g_mine, g_theirs)


def _adamw_vec(w, g, m, v):
    def body(w_ref, g_ref, m_ref, v_ref, d_ref, nm_ref, nv_ref):
        d_ref[...], nm_ref[...], nv_ref[...] = _adam_math(w_ref[...], g_ref[...], m_ref[...], v_ref[...])

    return pl.pallas_call(body, name="adamw_vectors", out_shape=[jax.ShapeDtypeStruct(w.shape, F32)] * 3)(w, g, m, v)


WEIGHT_NAMES = ("g_mix", "w_in", "b_forget", "b_gate", "w_branch_fox", "w_branch_sb", "w_out", "g_mlp", "w_up", "w_down",
                "g_ple", "w_ple_gate", "w_ple", "g_final")
W_IN_SHARD = D_IN // N_CHIPS
Q_END, F_END, B_END = 3 * D_ATT, 3 * D_ATT + N_HEADS, 6 * D_ATT + N_HEADS
GATE_SHARD = D_MODEL // N_CHIPS


def _join_cols(slots):
    return jnp.transpose(slots, (1, 0, 2)).reshape(slots.shape[1], N_CHIPS * slots.shape[2])


def _whole_weights(gathered, b_gate):
    w_in = _join_cols(gathered["w_in"])
    forget = jnp.concatenate([w_in[:, Q_END:F_END], jnp.zeros((D_MODEL, F_PAD - N_HEADS), BF16)], axis=1)
    rows = lambda a: a.reshape(N_CHIPS * a.shape[1], a.shape[2])
    return {"qkv": jnp.concatenate([w_in[:, :Q_END], w_in[:, F_END:B_END]], axis=1), "gate": w_in[:, B_END:], "forget": forget,
            "b_gate": b_gate, "branch_fox": gathered["w_branch_fox"], "branch_sb": gathered["w_branch_sb"],
            "out": rows(gathered["w_out"]), "up": gathered["w_up"], "down": rows(gathered["w_down"]),
            "ple_gate": rows(gathered["w_ple_gate"]), "ple": gathered["w_ple"]}


def _grad_slots(gw):
    g_in = jnp.concatenate([gw["qkv"][:, :Q_END], gw["forget"][:, :N_HEADS], gw["qkv"][:, Q_END:], gw["gate"]], axis=1)
    g_in = jnp.transpose(g_in.reshape(D_MODEL, N_CHIPS, W_IN_SHARD), (1, 0, 2))
    rows = lambda a: a.reshape(N_CHIPS, a.shape[0] // N_CHIPS, a.shape[1])
    return {"w_in": g_in, "w_branch_fox": gw["branch_fox"], "w_branch_sb": gw["branch_sb"], "w_out": rows(gw["out"]),
            "w_up": gw["up"], "w_down": rows(gw["down"]), "w_ple_gate": rows(gw["ple_gate"]), "w_ple": gw["ple"]}


def kernel(x, p, g_mix, w_in, b_forget, b_gate, w_branch_fox, w_branch_sb, w_out, g_mlp, w_up, w_down, g_ple, w_ple_gate, w_ple, g_final, loss_target, m_g_mix, m_w_in, m_b_forget, m_b_gate, m_w_branch_fox, m_w_branch_sb, m_w_out, m_g_mlp, m_w_up, m_w_down, m_g_ple, m_w_ple_gate, m_w_ple, m_g_final, v_g_mix, v_w_in, v_b_forget, v_b_gate, v_w_branch_fox, v_w_branch_sb, v_w_out, v_g_mlp, v_w_up, v_w_down, v_g_ple, v_w_ple_gate, v_w_ple, v_g_final):
    weights = dict(g_mix=g_mix, w_in=w_in, b_forget=b_forget, b_gate=b_gate, w_branch_fox=w_branch_fox,
                   w_branch_sb=w_branch_sb, w_out=w_out, g_mlp=g_mlp, w_up=w_up, w_down=w_down, g_ple=g_ple,
                   w_ple_gate=w_ple_gate, w_ple=w_ple, g_final=g_final)
    first = dict(g_mix=m_g_mix, w_in=m_w_in, b_forget=m_b_forget, b_gate=m_b_gate, w_branch_fox=m_w_branch_fox,
                 w_branch_sb=m_w_branch_sb, w_out=m_w_out, g_mlp=m_g_mlp, w_up=m_w_up, w_down=m_w_down, g_ple=m_g_ple,
                 w_ple_gate=m_w_ple_gate, w_ple=m_w_ple, g_final=m_g_final)
    second = dict(g_mix=v_g_mix, w_in=v_w_in, b_forget=v_b_forget, b_gate=v_b_gate, w_branch_fox=v_w_branch_fox,
                  w_branch_sb=v_w_branch_sb, w_out=v_w_out, g_mlp=v_g_mlp, w_up=v_w_up, w_down=v_w_down, g_ple=v_g_ple,
                  w_ple_gate=v_w_ple_gate, w_ple=v_w_ple, g_final=v_g_final)
    cx, cy, cc = _place()
    chip = 2 * cx + cy
    place = jnp.stack([chip, cc]).astype(jnp.int32)
    col0 = chip * GATE_SHARD

    gathered = dict(zip(SHARDED, _gather_weights([weights[n][0].astype(BF16) for n in SHARDED])))
    gate_rows = lax.dynamic_update_slice(jnp.zeros((2, D_MODEL), F32), b_gate[0] * (cc == 0).astype(F32), (0, col0))
    zero_row = jnp.zeros((1, D_MODEL), F32)
    b_gate_whole = _sum_devices(_vec_block(zero_row, zero_row, zero_row, zero_row[0], zero_row[:, :N_HEADS], gate_rows),
                                "gather_b_gate")[5:7]
    vec = {"g_mix": g_mix, "b_forget": jnp.concatenate([b_forget, jnp.zeros((1, F_PAD - N_HEADS), F32)], axis=1),
           "g_mlp": g_mlp, "g_ple": g_ple, "g_final": g_final.reshape(1, D_MODEL)}

    loss, grad_x, gw, gvec = _local_step(x, p[0], loss_target, _whole_weights(gathered, b_gate_whole), vec)
    loss = lax.psum(loss, ("x", "y", "c"))

    slots = _grad_slots(gw)
    mine, theirs = _reduce_scatter(place, [slots[n] for n in SHARDED], SHARDED)
    out = {}
    for n, g_mine, g_theirs in zip(SHARDED, mine, theirs):
        res = _adamw_halves(place, weights[n][0], first[n][0], second[n][0], g_mine, g_theirs, "adamw_" + n)
        out[n] = [r[None] for r in res]

    g_block = _sum_devices(_vec_block(gvec["g_mix"], gvec["g_mlp"], gvec["g_ple"], gvec["g_final"][0], gvec["b_forget"],
                                      gw["b_gate"]), "reduce_vectors")
    g_gate = lax.dynamic_slice(g_block[5:7], (0, col0), (2, GATE_SHARD))
    blocks = [_vec_block(d["g_mix"], d["g_mlp"], d["g_ple"], d["g_final"], d["b_forget"], d["b_gate"][0])
              for d in (weights, first, second)]
    g_rows = jnp.concatenate([g_block[0:5], jnp.concatenate([g_gate, jnp.zeros((2, D_MODEL - GATE_SHARD), F32)], axis=1),
                              jnp.zeros((1, D_MODEL), F32)], axis=0)
    res = (g_rows,) + tuple(_adamw_vec(blocks[0], g_rows, blocks[1], blocks[2]))
    out["g_mix"] = [r[0:1] for r in res]
    out["g_mlp"] = [r[1:2] for r in res]
    out["g_ple"] = [r[2:3] for r in res]
    out["g_final"] = [r[3] for r in res]
    out["b_forget"] = [r[4:5, :N_HEADS] for r in res]
    out["b_gate"] = [r[5:7, :GATE_SHARD][None] for r in res]
    return (loss, grad_x, *[out[n][0] for n in WEIGHT_NAMES], *[out[n][1] for n in WEIGHT_NAMES],
            *[out[n][2] for n in WEIGHT_NAMES], *[out[n][3] for n in WEIGHT_NAMES])
```

```python
import jax
import jax.numpy as jnp
from jax import lax
from jax.experimental import pallas as pl
from jax.experimental.pallas import tpu as pltpu

F32 = jnp.float32
BF16 = jnp.bfloat16

D_MODEL = 1024
HEAD_DIM = 64
N_HEADS = 8
D_ATT = N_HEADS * HEAD_DIM
D_FF = 4 * D_MODEL
D_PLE = 256
D_IN = 6 * D_ATT + N_HEADS + 2 * D_MODEL
F_PAD = 128
EPS = 1e-6
SCALE = HEAD_DIM ** -0.5
N_CHIPS = 4
LANES = 128
ATT_BLOCK = 256
FOX_TILES = (512, 512)
SB_TILES = (512, 256)
NEG = -1e30

ADAM_LR = 0.001
ADAM_B1 = 0.9
ADAM_B2 = 0.999
ADAM_EPS = 1e-08
ADAM_WD = 0.01
ADAM_STEP = 10

VMEM_LIMIT = 56 * 1024 * 1024

MESH = pl.DeviceIdType.MESH


def _cparams(sem=None):
    return pltpu.CompilerParams(dimension_semantics=sem, vmem_limit_bytes=VMEM_LIMIT)


def _relu2(t):
    t = t.astype(F32)
    return t * t


_DIMS = {"nn": (((1,), (0,)), ((), ())), "nt": (((1,), (1,)), ((), ())), "tn": (((0,), (0,)), ((), ()))}
NT_DIMS = _DIMS["nt"]
TN_DIMS = _DIMS["tn"]


def _mm(a, b, *, mode, name, out_dtype=F32, tm=512, tn=512, tk=512, add=None, a_fn=None, epi=None, extra=None,
        col_shards=False):
    if mode == "nn":
        (m, k), n = a.shape, b.shape[-1]
    elif mode == "nt":
        (m, k), n = a.shape, b.shape[-2]
    else:
        (k, m), n = a.shape, b.shape[1]
    shard = None
    if col_shards:
        if mode == "nn":
            shard, n = n, N_CHIPS * n
            tn = min(tn, shard)
        elif mode == "nt":
            shard = b.shape[-1]
            tk = min(tk, shard)
        else:
            shard = n // N_CHIPS
            tn = min(tn, shard)
    tm, tn, tk = min(tm, m), min(tn, n), min(tk, k)
    assert m % tm == 0 and n % tn == 0 and k % tk == 0, (name, m, n, k)
    nk = k // tk
    a_spec = {"nn": pl.BlockSpec((tm, tk), lambda i, j, kk: (i, kk)),
              "nt": pl.BlockSpec((tm, tk), lambda i, j, kk: (i, kk)),
              "tn": pl.BlockSpec((tk, tm), lambda i, j, kk: (kk, i))}[mode]
    b_spec = {"nn": pl.BlockSpec((tk, tn), lambda i, j, kk: (kk, j)),
              "nt": pl.BlockSpec((tn, tk), lambda i, j, kk: (j, kk)),
              "tn": pl.BlockSpec((tk, tn), lambda i, j, kk: (kk, j))}[mode]
    o_spec = pl.BlockSpec((tm, tn), lambda i, j, kk: (i, j))
    out_shape = (m, n)
    if col_shards and mode == "nn":
        per = shard // tn
        b_spec = pl.BlockSpec((None, tk, tn), lambda i, j, kk: (j // per, kk, j % per))
    elif col_shards and mode == "nt":
        per = shard // tk
        b_spec = pl.BlockSpec((None, tn, tk), lambda i, j, kk: (kk // per, j, kk % per))
    elif col_shards:
        assert add is None and extra is None
        per = shard // tn
        o_spec = pl.BlockSpec((None, tm, tn), lambda i, j, kk: (j // per, i, j % per))
        out_shape = (N_CHIPS, m, shard)
    operands, in_specs = [a, b], [a_spec, b_spec]
    third = add if add is not None else extra
    if third is not None:
        operands.append(third)
        in_specs.append(o_spec)

    def body(*refs):
        a_ref, b_ref = refs[0], refs[1]
        t_ref = refs[2] if third is not None else None
        o_ref = refs[3] if third is not None else refs[2]
        acc_ref = refs[-1] if nk > 1 else None
        at = a_ref[...]
        if a_fn is not None:
            at = a_fn(at)
        part = lax.dot_general(at.astype(BF16), b_ref[...].astype(BF16), _DIMS[mode], preferred_element_type=F32)

        def finish(acc):
            if epi is not None:
                acc = epi(acc, None if t_ref is None else t_ref[...])
            elif add is not None:
                acc = acc + t_ref[...].astype(F32)
            o_ref[...] = acc.astype(o_ref.dtype)

        if nk == 1:
            finish(part)
        else:
            kk = pl.program_id(2)

            @pl.when(kk == 0)
            def _():
                acc_ref[...] = part

            @pl.when(kk > 0)
            def _():
                acc_ref[...] += part

            @pl.when(kk == nk - 1)
            def _():
                finish(acc_ref[...])

    return pl.pallas_call(
        body, name=name, grid=(m // tm, n // tn, nk),
        in_specs=in_specs, out_specs=o_spec,
        out_shape=jax.ShapeDtypeStruct(out_shape, out_dtype),
        scratch_shapes=[pltpu.VMEM((tm, tn), F32)] if nk > 1 else [],
        compiler_params=_cparams(("parallel", "parallel", "arbitrary")),
    )(*operands)


ROW_TILE = 512


def _row_spec(width=D_MODEL, rows=ROW_TILE):
    return pl.BlockSpec((rows, width), lambda i: (i, 0))


def _vec_spec(rows=1, width=D_MODEL):
    return pl.BlockSpec((rows, width), lambda i: (0, 0))


def _xhat(x):
    r = lax.rsqrt(jnp.mean(x * x, axis=-1, keepdims=True) + EPS)
    return x * r, r


def _rms_bwd_rows(dh, x, g):
    xh, r = _xhat(x)
    dxh = dh * g
    dx = r * (dxh - xh * jnp.mean(dxh * xh, axis=-1, keepdims=True))
    return dx, jnp.sum(dh * xh, axis=0, keepdims=True)


def _norm_fwd(x, g, name):
    t = x.shape[0]

    def body(x_ref, g_ref, h_ref):
        xh, _ = _xhat(x_ref[...])
        h_ref[...] = (xh * g_ref[...]).astype(BF16)

    return pl.pallas_call(
        body, name=name, grid=(t // ROW_TILE,), in_specs=[_row_spec(), _vec_spec()], out_specs=_row_spec(),
        out_shape=jax.ShapeDtypeStruct((t, D_MODEL), BF16), compiler_params=_cparams(("parallel",)),
    )(x, g)


def _norm_bwd(x, g, dh, dres, name):
    t = x.shape[0]

    def body(x_ref, g_ref, dh_ref, dres_ref, dx_ref, dxb_ref, dg_ref):
        dx, dg = _rms_bwd_rows(dh_ref[...], x_ref[...], g_ref[...])
        dx = dx + dres_ref[...]
        dx_ref[...] = dx
        dxb_ref[...] = dx.astype(BF16)

        @pl.when(pl.program_id(0) == 0)
        def _():
            dg_ref[...] = jnp.zeros_like(dg_ref)

        dg_ref[...] += dg

    return pl.pallas_call(
        body, name=name, grid=(t // ROW_TILE,),
        in_specs=[_row_spec(), _vec_spec(), _row_spec(), _row_spec()],
        out_specs=[_row_spec(), _row_spec(), _vec_spec()],
        out_shape=[jax.ShapeDtypeStruct((t, D_MODEL), F32), jax.ShapeDtypeStruct((t, D_MODEL), BF16),
                   jax.ShapeDtypeStruct((1, D_MODEL), F32)],
        compiler_params=_cparams(("arbitrary",)),
    )(x, g, dh, dres)


def _gate_fwd(gl, b_gate, of, os_):
    t = of.shape[0]

    def body(gla_ref, glb_ref, b_ref, of_ref, os_ref, m_ref):
        ga = jax.nn.sigmoid(gla_ref[...] + b_ref[0:1, :])
        gb = jax.nn.sigmoid(glb_ref[...] + b_ref[1:2, :])
        m_ref[...] = (ga * of_ref[...] + gb * os_ref[...]).astype(BF16)

    return pl.pallas_call(
        body, name="gate_fwd", grid=(t // ROW_TILE,),
        in_specs=[pl.BlockSpec((ROW_TILE, D_MODEL), lambda i: (i, 0)), pl.BlockSpec((ROW_TILE, D_MODEL), lambda i: (i, 1)),
                  _vec_spec(2), _row_spec(), _row_spec()],
        out_specs=_row_spec(), out_shape=jax.ShapeDtypeStruct((t, D_MODEL), BF16),
        compiler_params=_cparams(("parallel",)),
    )(gl, gl, b_gate, of, os_)


def _gate_bwd(gl, b_gate, of, os_, dmerged):
    t = of.shape[0]

    def body(gla_ref, glb_ref, b_ref, of_ref, os_ref, dm_ref, dof_ref, dos_ref, dgla_ref, dglb_ref, db_ref):
        dm = dm_ref[...]
        ga = jax.nn.sigmoid(gla_ref[...] + b_ref[0:1, :])
        gb = jax.nn.sigmoid(glb_ref[...] + b_ref[1:2, :])
        dof_ref[...] = (dm * ga).astype(BF16)
        dos_ref[...] = (dm * gb).astype(BF16)
        dgla = dm * of_ref[...] * ga * (1.0 - ga)
        dglb = dm * os_ref[...] * gb * (1.0 - gb)
        dgla_ref[...] = dgla.astype(BF16)
        dglb_ref[...] = dglb.astype(BF16)

        @pl.when(pl.program_id(0) == 0)
        def _():
            db_ref[...] = jnp.zeros_like(db_ref)

        db_ref[0:1, :] += jnp.sum(dgla, axis=0, keepdims=True)
        db_ref[1:2, :] += jnp.sum(dglb, axis=0, keepdims=True)

    outs = pl.pallas_call(
        body, name="gate_bwd", grid=(t // ROW_TILE,),
        in_specs=[pl.BlockSpec((ROW_TILE, D_MODEL), lambda i: (i, 0)), pl.BlockSpec((ROW_TILE, D_MODEL), lambda i: (i, 1)),
                  _vec_spec(2), _row_spec(), _row_spec(), _row_spec()],
        out_specs=[_row_spec(), _row_spec(), _row_spec(), _row_spec(), _vec_spec(2)],
        out_shape=[jax.ShapeDtypeStruct((t, D_MODEL), BF16)] * 4 + [jax.ShapeDtypeStruct((2, D_MODEL), F32)],
        compiler_params=_cparams(("arbitrary",)),
    )(gl, gl, b_gate, of, os_, dmerged)
    return outs


def _head_and_loss(x2, gpre, pe, g_final, target):
    t = x2.shape[0]

    def body(x2_ref, gpre_ref, pe_ref, g_ref, tgt_ref, dx3_ref, dpre_ref, dpe_ref, dg_ref, loss_ref):
        gp = jax.nn.sigmoid(gpre_ref[...])
        pe_t = pe_ref[...]
        x3 = x2_ref[...] + gp * pe_t
        g = g_ref[...]
        xh, _ = _xhat(x3)
        err = xh * g - tgt_ref[...]
        dy = err * (1.0 / D_MODEL)
        dx3, dg = _rms_bwd_rows(dy, x3, g)
        dx3_ref[...] = dx3
        dpre_ref[...] = (dx3 * pe_t * gp * (1.0 - gp)).astype(BF16)
        dpe_ref[...] = (dx3 * gp).astype(BF16)

        @pl.when(pl.program_id(0) == 0)
        def _():
            dg_ref[...] = jnp.zeros_like(dg_ref)
            loss_ref[...] = jnp.zeros_like(loss_ref)

        dg_ref[...] += dg
        loss_ref[...] += 0.5 * jnp.sum(jnp.mean(err * err, axis=-1, keepdims=True), axis=0, keepdims=True)

    return pl.pallas_call(
        body, name="head_and_loss", grid=(t // ROW_TILE,),
        in_specs=[_row_spec(), _row_spec(), _row_spec(), _vec_spec(), _row_spec()],
        out_specs=[_row_spec(), _row_spec(), _row_spec(), _vec_spec(), _vec_spec(1, LANES)],
        out_shape=[jax.ShapeDtypeStruct((t, D_MODEL), F32), jax.ShapeDtypeStruct((t, D_MODEL), BF16),
                   jax.ShapeDtypeStruct((t, D_MODEL), BF16), jax.ShapeDtypeStruct((1, D_MODEL), F32),
                   jax.ShapeDtypeStruct((1, LANES), F32)],
        compiler_params=_cparams(("arbitrary",)),
    )(x2, gpre, pe, g_final, target)


def _split3(v):
    hi = v.astype(BF16)
    r1 = v - hi.astype(F32)
    mid = r1.astype(BF16)
    lo = (r1 - mid.astype(F32)).astype(BF16)
    return hi, mid, lo


def _split2(v):
    hi = v.astype(BF16)
    return hi, (v - hi.astype(F32)).astype(BF16)


def _dot(a, b, dims=_DIMS["nn"]):
    return lax.dot_general(a, b, dims, preferred_element_type=F32)


def _tri(n, rel):
    row = lax.broadcasted_iota(jnp.int32, (n, n), 0)
    col = lax.broadcasted_iota(jnp.int32, (n, n), 1)
    return rel(row, col).astype(BF16)


def _log_sigmoid(v):
    return -(jnp.maximum(-v, 0.0) + jnp.log(1.0 + jnp.exp(-jnp.abs(v))))


def _fox_prep(fl, b_forget, batch, seq):
    nb = seq // ATT_BLOCK

    def body(fl_ref, b_ref, cw_ref, cr_ref):
        col = lax.broadcasted_iota(jnp.int32, (ATT_BLOCK, F_PAD), 1)
        lower = _tri(ATT_BLOCK, lambda r, c: c <= r)
        upper = _tri(ATT_BLOCK, lambda r, c: r <= c)
        expand = (lax.broadcasted_iota(jnp.int32, (F_PAD, D_ATT), 1) // HEAD_DIM
                  == lax.broadcasted_iota(jnp.int32, (F_PAD, D_ATT), 0)).astype(BF16)
        carry_w = jnp.zeros((1, D_ATT), F32)
        carry_r = jnp.zeros((F_PAD, 1), F32)
        for i in range(nb):
            blk = slice(i * ATT_BLOCK, (i + 1) * ATT_BLOCK)
            logf = jnp.where(col < N_HEADS, _log_sigmoid(fl_ref[blk, :] + b_ref[...]), 0.0)
            cw = jnp.zeros((ATT_BLOCK, D_ATT), F32) + carry_w
            cr = jnp.zeros((F_PAD, ATT_BLOCK), F32) + carry_r
            for part in _split3(logf):
                cw += _dot(lower, _dot(part, expand).astype(BF16))
                cr += _dot(part, upper, TN_DIMS)
            cw_ref[blk, :] = cw
            cr_ref[:, blk] = cr[0:N_HEADS, :]
            carry_w = cw[ATT_BLOCK - 1:ATT_BLOCK, :]
            carry_r = cr[:, ATT_BLOCK - 1:ATT_BLOCK]

    return pl.pallas_call(
        body, name="fox_prep", grid=(batch,),
        in_specs=[pl.BlockSpec((seq, F_PAD), lambda b: (b, 0)), pl.BlockSpec((1, F_PAD), lambda b: (0, 0))],
        out_specs=[pl.BlockSpec((seq, D_ATT), lambda b: (b, 0)), pl.BlockSpec((N_HEADS, seq), lambda b: (b, 0))],
        out_shape=[jax.ShapeDtypeStruct((batch * seq, D_ATT), F32), jax.ShapeDtypeStruct((batch * N_HEADS, seq), F32)],
        compiler_params=_cparams(("parallel",)),
    )(fl, b_forget)


def _fox_post(dcs_wide, drs_wide, fl, b_forget, batch, seq):
    nb = seq // ATT_BLOCK

    def body(dcs_ref, drs_ref, fl_ref, b_ref, dfl_ref, db_ref):
        pick = (lax.broadcasted_iota(jnp.int32, (D_ATT, F_PAD), 0)
                == lax.broadcasted_iota(jnp.int32, (D_ATT, F_PAD), 1) * HEAD_DIM).astype(BF16)
        upper = _tri(ATT_BLOCK, lambda r, c: r <= c)
        col = lax.broadcasted_iota(jnp.int32, (ATT_BLOCK, F_PAD), 1)

        @pl.when(pl.program_id(0) == 0)
        def _():
            db_ref[...] = jnp.zeros_like(db_ref)

        carry = jnp.zeros((1, F_PAD), F32)
        for i in reversed(range(nb)):
            blk = slice(i * ATT_BLOCK, (i + 1) * ATT_BLOCK)
            narrow = jnp.zeros((ATT_BLOCK, F_PAD), F32)
            for part in _split3(drs_ref[blk, :] - dcs_ref[blk, :]):
                narrow += _dot(part, pick)
            after = jnp.zeros((ATT_BLOCK, F_PAD), F32) + carry
            for part in _split3(narrow):
                after += _dot(upper, part)
            carry = after[0:1, :]
            pre = fl_ref[blk, :] + b_ref[...]
            dfl = jnp.where(col < N_HEADS, after * jax.nn.sigmoid(-pre), 0.0)
            dfl_ref[blk, :] = dfl.astype(BF16)
            db_ref[...] += jnp.sum(dfl, axis=0, keepdims=True)

    return pl.pallas_call(
        body, name="fox_post", grid=(batch,),
        in_specs=[pl.BlockSpec((seq, D_ATT), lambda b: (b, 0)), pl.BlockSpec((seq, D_ATT), lambda b: (b, 0)),
                  pl.BlockSpec((seq, F_PAD), lambda b: (b, 0)), pl.BlockSpec((1, F_PAD), lambda b: (0, 0))],
        out_specs=[pl.BlockSpec((seq, F_PAD), lambda b: (b, 0)), pl.BlockSpec((1, F_PAD), lambda b: (0, 0))],
        out_shape=[jax.ShapeDtypeStruct((batch * seq, F_PAD), BF16), jax.ShapeDtypeStruct((1, F_PAD), F32)],
        compiler_params=_cparams(("arbitrary",)),
    )(dcs_wide, drs_wide, fl, b_forget)


N_PAIRS = N_HEADS // 2


def _att_specs(seq, col0, tq):
    nq = seq // tq
    q = pl.BlockSpec((tq, LANES), lambda b, hp, qi: (b * nq + qi, col0 + hp))
    k = pl.BlockSpec((seq, LANES), lambda b, hp, qi: (b, col0 + N_PAIRS + hp))
    v = pl.BlockSpec((seq, LANES), lambda b, hp, qi: (b, col0 + 2 * N_PAIRS + hp))
    return q, k, v


def _qblock_spec(seq, tq):
    nq = seq // tq
    return pl.BlockSpec((tq, LANES), lambda b, hp, qi: (b * nq + qi, hp))


def _kv_out_spec(seq):
    return pl.BlockSpec((seq, LANES), lambda b, hp, qi: (b, hp))


def _head_masks():
    lane = lax.broadcasted_iota(jnp.int32, (1, LANES), 1)
    return [(lane >= HEAD_DIM * j) & (lane < HEAD_DIM * (j + 1)) for j in range(2)]


def _stack_heads(t, masks):
    zero = jnp.zeros_like(t)
    return jnp.concatenate([jnp.where(masks[0], t, zero), jnp.where(masks[1], t, zero)], axis=0)


def _stack_cols(t):
    return jnp.concatenate([t[:, 0:1], t[:, HEAD_DIM:HEAD_DIM + 1]], axis=0)


def _unstack(t2, masks):
    tq = t2.shape[0] // 2
    return jnp.where(masks[0], t2[:tq], t2[tq:])


def _stacked_ids(tq, tk):
    row = lax.broadcasted_iota(jnp.int32, (2 * tq, tk), 0)
    col = lax.broadcasted_iota(jnp.int32, (2 * tq, tk), 1)
    first = lax.broadcasted_iota(jnp.int32, (2 * tq, 1), 0) < tq
    return col - jnp.where(row < tq, row, row - tq), first


def _sweep(qi, tq, tk, step, init, leftward):
    per = tq // tk
    whole = lambda carry: lax.fori_loop(0, per * qi, lambda i, c: step(per * qi - 1 - i if leftward else i, c, None), carry)
    crossed = [(per * qi + j, -j * tk) for j in range(per)]
    if leftward:
        carry = init
        for kb, lead in reversed(crossed):
            carry = step(kb, carry, lead)
        return whole(carry)
    carry = whole(init)
    for kb, lead in crossed:
        carry = step(kb, carry, lead)
    return carry


def _fox_fwd(qkv, c_wide, c_row, batch, seq):
    tq, tk = FOX_TILES
    nq = seq // tq

    def body(q_ref, k_ref, v_ref, cw_ref, cr_ref, o_ref, lse_ref):
        hp, qi = pl.program_id(1), pl.program_id(2)
        masks = _head_masks()
        ahead, first = _stacked_ids(tq, tk)
        q2 = _stack_heads(q_ref[...], masks)
        ct = _stack_cols(cw_ref[...])

        def step(kb, carry, lead):
            m, l, acc = carry
            k0 = pl.multiple_of(kb * tk, tk)
            cs = jnp.where(first, cr_ref[pl.ds(2 * hp, 1), pl.ds(k0, tk)], cr_ref[pl.ds(2 * hp + 1, 1), pl.ds(k0, tk)])
            s = _dot(q2, k_ref[pl.ds(k0, tk), :], NT_DIMS) * SCALE + ct - cs
            if lead is not None:
                s = jnp.where(ahead <= lead, s, NEG)
            m_new = jnp.maximum(m, jnp.max(s, axis=1, keepdims=True))
            p = jnp.exp(s - m_new)
            alpha = jnp.exp(m - m_new)
            l = alpha * l + jnp.sum(p, axis=1, keepdims=True)
            acc = alpha * acc + _dot(p.astype(BF16), v_ref[pl.ds(k0, tk), :])
            return m_new, l, acc

        init = (jnp.full((2 * tq, 1), NEG, F32), jnp.zeros((2 * tq, 1), F32), jnp.zeros((2 * tq, LANES), F32))
        m, l, acc = _sweep(qi, tq, tk, step, init, leftward=False)
        o_ref[...] = _unstack(acc / l, masks).astype(BF16)
        lse_ref[...] = _unstack(m + jnp.log(l), masks)

    q_spec, k_spec, v_spec = _att_specs(seq, 0, tq)
    qb = _qblock_spec(seq, tq)
    return pl.pallas_call(
        body, name="fox_fwd", grid=(batch, N_PAIRS, nq),
        in_specs=[q_spec, k_spec, v_spec, qb, pl.BlockSpec((N_HEADS, seq), lambda b, hp, qi: (b, 0))],
        out_specs=[qb, qb],
        out_shape=[jax.ShapeDtypeStruct((batch * seq, D_ATT), BF16), jax.ShapeDtypeStruct((batch * seq, D_ATT), F32)],
        compiler_params=_cparams(("parallel", "parallel", "arbitrary")),
    )(qkv, qkv, qkv, c_wide, c_row)


def _fox_bwd(qkv, c_wide, c_row, o, do, lse_wide, batch, seq):
    tq, tk = FOX_TILES
    nq = seq // tq

    def body(q_ref, k_ref, v_ref, cw_ref, cr_ref, o_ref, do_ref, lse_ref,
             dq_ref, dk_ref, dv_ref, dcs_ref, drs_ref, dkc_acc, dv_acc):
        hp, qi = pl.program_id(1), pl.program_id(2)

        @pl.when(qi == 0)
        def _():
            dkc_acc[...] = jnp.zeros_like(dkc_acc)
            dv_acc[...] = jnp.zeros_like(dv_acc)

        masks = _head_masks()
        ahead, first = _stacked_ids(tq, tk)
        q_t, do_t = q_ref[...], do_ref[...]
        q2 = _stack_heads(q_t, masks)
        do2 = _stack_heads(do_t, masks)
        q_and_ones = jnp.concatenate([q2, _stack_heads(jnp.ones_like(q_t), masks)], axis=1)
        ct = _stack_cols(cw_ref[...])
        lse = _stack_cols(lse_ref[...])
        prod = do_t.astype(F32) * o_ref[...].astype(F32)
        delta = jnp.concatenate([jnp.sum(jnp.where(mk, prod, 0.0), axis=1, keepdims=True) for mk in masks], axis=0)

        def step(kb, carry, lead):
            dq_acc, rs = carry
            k0 = pl.multiple_of(kb * tk, tk)
            kblk = k_ref[pl.ds(k0, tk), :]
            cs = jnp.where(first, cr_ref[pl.ds(2 * hp, 1), pl.ds(k0, tk)], cr_ref[pl.ds(2 * hp + 1, 1), pl.ds(k0, tk)])
            p = jnp.exp(_dot(q2, kblk, NT_DIMS) * SCALE + ct - cs - lse)
            if lead is not None:
                p = jnp.where(ahead <= lead, p, 0.0)
            dp = _dot(do2, v_ref[pl.ds(k0, tk), :], NT_DIMS)
            ds = (p * (dp - delta) * SCALE).astype(BF16)
            dkc_acc[pl.ds(k0, tk), :] += _dot(ds, q_and_ones, TN_DIMS)
            dv_acc[pl.ds(k0, tk), :] += _dot(p.astype(BF16), do2, TN_DIMS)
            return dq_acc + _dot(ds, kblk), rs + jnp.sum(ds.astype(F32), axis=1, keepdims=True)

        init = (jnp.zeros((2 * tq, LANES), F32), jnp.zeros((2 * tq, 1), F32))
        dq_acc, rs = _sweep(qi, tq, tk, step, init, leftward=False)
        dq_ref[...] = _unstack(dq_acc, masks).astype(BF16)
        drs_ref[...] = _unstack(rs, masks) * (1.0 / SCALE)

        @pl.when(qi == nq - 1)
        def _():
            dk_ref[...] = dkc_acc[:, 0:LANES].astype(BF16)
            dcs_ref[...] = dkc_acc[:, LANES:2 * LANES] * (1.0 / SCALE)
            dv_ref[...] = dv_acc[...].astype(BF16)

    q_spec, k_spec, v_spec = _att_specs(seq, 0, tq)
    qb = _qblock_spec(seq, tq)
    return pl.pallas_call(
        body, name="fox_bwd", grid=(batch, N_PAIRS, nq),
        in_specs=[q_spec, k_spec, v_spec, qb, pl.BlockSpec((N_HEADS, seq), lambda b, hp, qi: (b, 0)), qb, qb, qb],
        out_specs=[qb, _kv_out_spec(seq), _kv_out_spec(seq), _kv_out_spec(seq), qb],
        out_shape=[jax.ShapeDtypeStruct((batch * seq, D_ATT), BF16)] * 3 + [jax.ShapeDtypeStruct((batch * seq, D_ATT), F32)] * 2,
        scratch_shapes=[pltpu.VMEM((seq, 2 * LANES), F32), pltpu.VMEM((seq, LANES), F32)],
        compiler_params=_cparams(("parallel", "parallel", "arbitrary")),
    )(qkv, qkv, qkv, c_wide, c_row, o, do, lse_wide)


def _sb_logits(q2, kblk):
    z = _dot(q2, kblk, NT_DIMS) * SCALE
    lsn = -(jnp.maximum(z, 0.0) + jnp.log(1.0 + jnp.exp(-jnp.abs(z))))
    return lsn + z, lsn


def _sb_fwd(qkv, batch, seq, behind):
    tq, tk = SB_TILES
    nq = seq // tq

    def body(q_ref, k_ref, v_ref, o_ref, rt_ref):
        qi = pl.program_id(2)
        masks = _head_masks()
        ahead, _ = _stacked_ids(tq, tk)
        later = _tri(tk, lambda r, c: r > c)
        q2 = _stack_heads(q_ref[...], masks)

        def step(kb, carry, lead):
            run, acc = carry
            k0 = pl.multiple_of(kb * tk, tk)
            ls, lsn = _sb_logits(q2, k_ref[pl.ds(k0, tk), :])
            if lead is not None:
                lsn = jnp.where(ahead < lead, lsn, 0.0)
            hi, lo = _split2(lsn)
            w = jnp.exp(ls + _dot(hi, later) + _dot(lo, later) + run)
            if lead is not None:
                w = jnp.where(ahead < lead, w, 0.0)
            return run + jnp.sum(lsn, axis=1, keepdims=True), acc + _dot(w.astype(BF16), v_ref[pl.ds(k0, tk), :])

        init = (jnp.zeros((2 * tq, 1), F32), jnp.zeros((2 * tq, LANES), F32))
        run, acc = _sweep(qi, tq, tk, step, init, leftward=True)
        o_ref[...] = _unstack(acc, masks).astype(BF16)
        rt_ref[...] = _unstack(run, masks)

    q_spec, k_spec, v_spec = _att_specs(seq, 3 * N_PAIRS, tq)
    qb = _qblock_spec(seq, tq)
    return _call_behind(
        body, behind, name="sb_fwd", grid=(batch, N_PAIRS, nq), in_specs=[q_spec, k_spec, v_spec], out_specs=[qb, qb],
        out_shape=[jax.ShapeDtypeStruct((batch * seq, D_ATT), BF16), jax.ShapeDtypeStruct((batch * seq, D_ATT), F32)],
        scratch_shapes=[], operands=(qkv, qkv, qkv))


def _sb_bwd(qkv, do, rt_wide, batch, seq, behind):
    tq, tk = SB_TILES
    nq = seq // tq

    def body(q_ref, k_ref, v_ref, do_ref, rt_ref, dq_ref, dk_ref, dv_ref, dk_acc, dv_acc):
        qi = pl.program_id(2)

        @pl.when(qi == 0)
        def _():
            dk_acc[...] = jnp.zeros_like(dk_acc)
            dv_acc[...] = jnp.zeros_like(dv_acc)

        masks = _head_masks()
        ahead, _ = _stacked_ids(tq, tk)
        later = _tri(tk, lambda r, c: r > c)
        earlier = _tri(tk, lambda r, c: r < c)
        q2 = _stack_heads(q_ref[...], masks)
        do2 = _stack_heads(do_ref[...], masks)
        total = _stack_cols(rt_ref[...])

        def step(kb, carry, lead):
            pref, epre, dq_acc = carry
            k0 = pl.multiple_of(kb * tk, tk)
            kblk = k_ref[pl.ds(k0, tk), :]
            ls, lsn_all = _sb_logits(q2, kblk)
            lsn = lsn_all if lead is None else jnp.where(ahead < lead, lsn_all, 0.0)
            rs = jnp.sum(lsn, axis=1, keepdims=True)
            hi, lo = _split2(lsn)
            w = jnp.exp(ls + _dot(hi, later) + _dot(lo, later) + (total - pref - rs))
            if lead is not None:
                w = jnp.where(ahead < lead, w, 0.0)
            e = w * _dot(do2, v_ref[pl.ds(k0, tk), :], NT_DIMS)
            ehi, elo = _split2(e)
            before = _dot(ehi, earlier) + _dot(elo, earlier) + epre
            dz = e * jnp.exp(lsn_all) - jnp.exp(ls) * before
            if lead is not None:
                dz = jnp.where(ahead < lead, dz, 0.0)
            dz = (dz * SCALE).astype(BF16)
            dk_acc[pl.ds(k0, tk), :] += _dot(dz, q2, TN_DIMS)
            dv_acc[pl.ds(k0, tk), :] += _dot(w.astype(BF16), do2, TN_DIMS)
            return pref + rs, epre + jnp.sum(e, axis=1, keepdims=True), dq_acc + _dot(dz, kblk)

        init = (jnp.zeros((2 * tq, 1), F32), jnp.zeros((2 * tq, 1), F32), jnp.zeros((2 * tq, LANES), F32))
        dq_acc = _sweep(qi, tq, tk, step, init, leftward=False)[2]
        dq_ref[...] = _unstack(dq_acc, masks).astype(BF16)

        @pl.when(qi == nq - 1)
        def _():
            dk_ref[...] = dk_acc[...].astype(BF16)
            dv_ref[...] = dv_acc[...].astype(BF16)

    q_spec, k_spec, v_spec = _att_specs(seq, 3 * N_PAIRS, tq)
    qb = _qblock_spec(seq, tq)
    return _call_behind(
        body, behind, name="sb_bwd", grid=(batch, N_PAIRS, nq), in_specs=[q_spec, k_spec, v_spec, qb, qb],
        out_specs=[qb, _kv_out_spec(seq), _kv_out_spec(seq)], out_shape=[jax.ShapeDtypeStruct((batch * seq, D_ATT), BF16)] * 3,
        scratch_shapes=[pltpu.VMEM((seq, LANES), F32), pltpu.VMEM((seq, LANES), F32)], operands=(qkv, qkv, qkv, do, rt_wide))


def _local_step(x, p, target, w, rest, vec, place):
    batch, seq, _ = x.shape
    t = batch * seq
    x = x.reshape(t, D_MODEL)
    target = target.reshape(t, D_MODEL)
    p = p.reshape(t, D_PLE)
    big = dict(tm=1024, tn=1024, tk=1024)

    h1 = _norm_fwd(x, vec["g_mix"], "norm_mix")
    qkv = _mm(h1, w["qkv"], mode="nn", name="proj_qkv", out_dtype=BF16, **big)
    gl = _mm(h1, w["gate"], mode="nn", name="proj_gate", **big)
    fl = _mm(h1, w["forget"], mode="nn", name="proj_forget", **big)
    c_wide, c_row = _fox_prep(fl, vec["b_forget"], batch, seq)
    o_fox, lse_wide = _fox_fwd(qkv, c_wide, c_row, batch, seq)
    (o_sb, rt_wide), gathered = _sb_fwd(qkv, batch, seq, rest)
    w = dict(w, **_rest_weights(dict(zip(EARLY, gathered))))
    of = _mm(o_fox, w["branch_fox"], mode="nn", name="branch_fox", col_shards=True, **big)
    os_ = _mm(o_sb, w["branch_sb"], mode="nn", name="branch_sb", col_shards=True, **big)
    merged = _gate_fwd(gl, w["b_gate"], of, os_)
    x1 = _mm(merged, w["out"], mode="nn", name="proj_out", add=x, **big)
    h2 = _norm_fwd(x1, vec["g_mlp"], "norm_mlp")
    ar = _mm(h2, w["up"], mode="nn", name="mlp_up", out_dtype=BF16, epi=lambda acc, _: jnp.maximum(acc, 0.0),
             col_shards=True, **big)
    x2 = _mm(ar, w["down"], mode="nn", name="mlp_down", a_fn=_relu2, add=x1, **big)
    h3 = _norm_fwd(x2, vec["g_ple"], "norm_ple")
    gpre = _mm(h3, w["ple_gate"], mode="nn", name="ple_gate", **big)
    pe = _mm(p, w["ple"], mode="nn", name="ple_embed", col_shards=True, **big)

    dx3, dpre, dpe, dg_final, loss = _head_and_loss(x2, gpre, pe, vec["g_final"], target)
    gw = {}
    gw["ple"] = _mm(p, dpe, mode="tn", name="d_w_ple", col_shards=True, **big)
    gw["ple_gate"] = _mm(h3, dpre, mode="tn", name="d_w_ple_gate", **big)
    dh3 = _mm(dpre, w["ple_gate"], mode="nt", name="d_h_ple", **big)
    dx2, dx2b, dg_ple = _norm_bwd(x2, vec["g_ple"], dh3, dx3, "norm_ple_bwd")
    gw["down"] = _mm(ar, dx2b, mode="tn", name="d_w_down", a_fn=_relu2, **big)
    da = _mm(dx2b, w["down"], mode="nt", name="d_act", out_dtype=BF16,
             epi=lambda acc, r: acc * (2.0 * r.astype(F32)), extra=ar, **big)
    gw["up"] = _mm(h2, da, mode="tn", name="d_w_up", col_shards=True, **big)
    dh2 = _mm(da, w["up"], mode="nt", name="d_h_mlp", col_shards=True, **big)
    dx1, dx1b, dg_mlp = _norm_bwd(x1, vec["g_mlp"], dh2, dx2, "norm_mlp_bwd")
    gw["out"] = _mm(merged, dx1b, mode="tn", name="d_w_out", **big)
    dmerged = _mm(dx1b, w["out"], mode="nt", name="d_merged", **big)
    dof, dos, dgla, dglb, gw["b_gate"] = _gate_bwd(gl, w["b_gate"], of, os_, dmerged)
    gw["branch_fox"] = _mm(o_fox, dof, mode="tn", name="d_w_branch_fox", col_shards=True, **big)
    gw["branch_sb"] = _mm(o_sb, dos, mode="tn", name="d_w_branch_sb", col_shards=True, **big)
    do_fox = _mm(dof, w["branch_fox"], mode="nt", name="d_o_fox", out_dtype=BF16, col_shards=True, **big)
    do_sb = _mm(dos, w["branch_sb"], mode="nt", name="d_o_sb", out_dtype=BF16, col_shards=True, **big)
    dq_a, dk_a, dv_a, dcs_wide, drs_wide = _fox_bwd(qkv, c_wide, c_row, o_fox, do_fox, lse_wide, batch, seq)
    early = _early_slots(gw)
    received, exchange = _reduce_begin(place, [early[n] for n in EARLY], EARLY, "early")
    (dq_b, dk_b, dv_b), others = _sb_bwd(qkv, do_sb, rt_wide, batch, seq, exchange)
    reduced = dict(zip(EARLY, zip(*_reduce_end(place, [early[n] for n in EARLY], received, others, EARLY, "early"))))
    dfl, db_forget = _fox_post(dcs_wide, drs_wide, fl, vec["b_forget"], batch, seq)
    dqkv = jnp.concatenate([dq_a, dk_a, dv_a, dq_b, dk_b, dv_b], axis=1)
    dgl = jnp.concatenate([dgla, dglb], axis=1)
    gw["qkv"] = _mm(h1, dqkv, mode="tn", name="d_w_qkv", **big)
    gw["gate"] = _mm(h1, dgl, mode="tn", name="d_w_gate", **big)
    gw["forget"] = _mm(h1, dfl, mode="tn", name="d_w_forget", **big)
    dh1 = _mm(dqkv, w["qkv"], mode="nt", name="d_h_qkv", **big)
    dh1 = _mm(dgl, w["gate"], mode="nt", name="d_h_gate", add=dh1, **big)
    dh1 = _mm(dfl, w["forget"], mode="nt", name="d_h_forget", add=dh1, **big)
    grad_x, _, dg_mix = _norm_bwd(x, vec["g_mix"], dh1, dx1, "norm_mix_bwd")

    late = [_w_in_slots(gw)]
    received, exchange = _reduce_begin(place, late, LATE, "late")
    others = _run_exchange(exchange, "reduce_exchange_late")
    reduced.update(zip(LATE, zip(*_reduce_end(place, late, received, others, LATE, "late"))))
    gvec = {"g_mix": dg_mix, "b_forget": db_forget[:, 0:N_HEADS], "g_mlp": dg_mlp, "g_ple": dg_ple,
            "g_final": dg_final, "b_gate": gw["b_gate"]}
    return loss[0, 0], grad_x.reshape(batch, seq, D_MODEL), reduced, gvec


ANY = pl.BlockSpec(memory_space=pl.ANY)
SHARDED = ("w_in", "w_branch_fox", "w_branch_sb", "w_out", "w_up", "w_down", "w_ple_gate", "w_ple")
ROW_ALIGN = 16


def _place():
    return lax.axis_index("x"), lax.axis_index("y"), lax.axis_index("c")


def _other_chips(x, y):
    return [(1 - x, y), (x, 1 - y), (1 - x, 1 - y)]


def _half(ref, h):
    r = ref.shape[0] // 2
    assert r % ROW_ALIGN == 0
    return ref.at[pl.ds(pl.multiple_of(h * r, ROW_ALIGN), r)]


def _remote(src, dst, sems, idx, to):
    send_sems, recv_sems = sems
    return pltpu.make_async_remote_copy(src_ref=src, dst_ref=dst, send_sem=send_sems.at[idx], recv_sem=recv_sems.at[idx],
                                        device_id=to, device_id_type=MESH)


class _Exchange:
    def __init__(self, operands, out_shapes, sem_shape, start, finish):
        self.operands, self.out_shapes, self.sem_shape, self.start, self.finish = operands, out_shapes, sem_shape, start, finish

    def scratch(self):
        return [pltpu.SemaphoreType.DMA(self.sem_shape), pltpu.SemaphoreType.DMA(self.sem_shape)]


def _run_exchange(ex, name):
    n = len(ex.operands)

    def body(*refs):
        ex.start(refs[:n], refs[n:2 * n], refs[2 * n:])
        ex.finish(refs[:n], refs[n:2 * n], refs[2 * n:])

    return pl.pallas_call(body, name=name, in_specs=[ANY] * n, out_specs=[ANY] * n, out_shape=ex.out_shapes,
                          scratch_shapes=ex.scratch())(*ex.operands)


def _call_behind(body, ex, *, name, grid, in_specs, out_specs, out_shape, scratch_shapes, operands):
    n_in, n_out, nx = len(in_specs), len(out_specs), len(ex.operands)

    def wrapped(*refs):
        ins, x_in = refs[:n_in], refs[n_in:n_in + nx]
        outs, x_out = refs[n_in + nx:n_in + nx + n_out], refs[n_in + nx + n_out:n_in + 2 * nx + n_out]
        scratch, sems = refs[n_in + 2 * nx + n_out:-2], refs[-2:]
        first, last = None, None
        for d, steps in enumerate(grid):
            at_start, at_end = pl.program_id(d) == 0, pl.program_id(d) == steps - 1
            first = at_start if first is None else first & at_start
            last = at_end if last is None else last & at_end

        @pl.when(first)
        def _():
            ex.start(x_in, x_out, sems)

        body(*ins, *outs, *scratch)

        @pl.when(last)
        def _():
            ex.finish(x_in, x_out, sems)

    res = pl.pallas_call(
        wrapped, name=name, grid=grid, in_specs=list(in_specs) + [ANY] * nx, out_specs=list(out_specs) + [ANY] * nx,
        out_shape=list(out_shape) + list(ex.out_shapes), scratch_shapes=list(scratch_shapes) + ex.scratch(),
        compiler_params=_cparams(("arbitrary",) * len(grid)),
    )(*operands, *ex.operands)
    return res[:n_out], res[n_out:]


def _gather_weights(shards):
    n = len(shards)

    def first_copies(src, out, sems):
        x, y, c = _place()
        me = 2 * x + y
        copies = [_remote(_half(src[t], c), _half(out[t].at[me], c), sems, (t, k), (px, py, c))
                  for t in range(n) for k, (px, py) in enumerate(_other_chips(x, y))]
        return copies + [_remote(src[t], out[t].at[me], sems, (t, 3), (x, y, 1 - c)) for t in range(n)]

    def start(src, out, sems):
        for cp in first_copies(src, out, sems):
            cp.start()

    def finish(src, out, sems):
        x, y, c = _place()
        me = 2 * x + y
        sibling = (x, y, 1 - c)
        chips = _other_chips(x, y)
        passes = []
        for t in range(n):
            for k, (px, py) in enumerate(chips):
                landed = _half(out[t].at[2 * px + py], c)
                _remote(landed, landed, sems, (t, k), (px, py, c)).wait_recv()
                passes.append(_remote(landed, landed, sems, (t, 4 + k), sibling))
                passes[-1].start()
        for t in range(n):
            _remote(src[t], out[t].at[me], sems, (t, 3), sibling).wait_recv()
            for k, (px, py) in enumerate(chips):
                passed = _half(out[t].at[2 * px + py], 1 - c)
                _remote(passed, passed, sems, (t, 4 + k), sibling).wait_recv()
        for cp in first_copies(src, out, sems) + passes:
            cp.wait_send()

    return _Exchange(shards, [jax.ShapeDtypeStruct((N_CHIPS,) + s.shape, s.dtype) for s in shards], (n, 7), start, finish)


def _swap_halves(slots, name):
    n = len(slots)

    def body(*refs):
        src, out, sems = refs[:n], refs[n:2 * n], refs[2 * n:]
        x, y, c = _place()
        copies = []
        for t in range(n):
            r = src[t].shape[1] // 2
            rows = pl.ds(pl.multiple_of((1 - c) * r, ROW_ALIGN), r)
            copies.append(_remote(src[t].at[:, rows], out[t], sems, t, (x, y, 1 - c)))
        for cp in copies:
            cp.start()
        for cp in copies:
            cp.wait_recv()
        for cp in copies:
            cp.wait_send()

    return pl.pallas_call(
        body, name=name, in_specs=[ANY] * n, out_specs=[ANY] * n,
        out_shape=[jax.ShapeDtypeStruct((N_CHIPS, s.shape[1] // 2, s.shape[2]), s.dtype) for s in slots],
        scratch_shapes=[pltpu.SemaphoreType.DMA((n,)), pltpu.SemaphoreType.DMA((n,))],
    )(*slots)


def _exchange_chips(sums):
    n = len(sums)

    def copies(src, out, sems):
        x, y, c = _place()
        return [_remote(src[t].at[2 * px + py], out[t].at[k], sems, (t, k), (px, py, c))
                for t in range(n) for k, (px, py) in enumerate(_other_chips(x, y))]

    def start(src, out, sems):
        for cp in copies(src, out, sems):
            cp.start()

    def finish(src, out, sems):
        for cp in copies(src, out, sems):
            cp.wait_recv()
        for cp in copies(src, out, sems):
            cp.wait_send()

    return _Exchange(sums, [jax.ShapeDtypeStruct((3,) + s.shape[1:], s.dtype) for s in sums], (n, 3), start, finish)


def _share_halves(mine, name):
    n = len(mine)

    def body(*refs):
        src, out, sems = refs[:n], refs[n:2 * n], refs[2 * n:]
        x, y, c = _place()
        copies = [_remote(src[t], out[t], sems, t, (x, y, 1 - c)) for t in range(n)]
        for cp in copies:
            cp.start()
        for cp in copies:
            cp.wait_recv()
        for cp in copies:
            cp.wait_send()

    return pl.pallas_call(
        body, name=name, in_specs=[ANY] * n, out_specs=[ANY] * n,
        out_shape=[jax.ShapeDtypeStruct(s.shape, s.dtype) for s in mine],
        scratch_shapes=[pltpu.SemaphoreType.DMA((n,)), pltpu.SemaphoreType.DMA((n,))],
    )(*mine)


def _half_tile(rows):
    return min(rows, 256)


def _sum_sibling(place, slot, received, name):
    n, rows2, cols = slot.shape
    rows = rows2 // 2
    tile = _half_tile(rows)
    nb = rows // tile

    def body(place_ref, a_ref, b_ref, o_ref):
        o_ref[...] = (a_ref[...] + b_ref[...]).astype(BF16)

    return pl.pallas_call(
        body, name=name, out_shape=jax.ShapeDtypeStruct((n, rows, cols), BF16),
        grid_spec=pltpu.PrefetchScalarGridSpec(
            num_scalar_prefetch=1, grid=(n, nb),
            in_specs=[pl.BlockSpec((None, tile, cols), lambda j, i, pr: (j, pr[1] * nb + i, 0)),
                      pl.BlockSpec((None, tile, cols), lambda j, i, pr: (j, i, 0))],
            out_specs=pl.BlockSpec((None, tile, cols), lambda j, i, pr: (j, i, 0))),
        compiler_params=_cparams(("parallel", "parallel")),
    )(place, slot, received)


def _sum_chips(place, slot, received, others, name):
    _, rows2, cols = slot.shape
    rows = rows2 // 2
    tile = _half_tile(rows)
    nb = rows // tile

    def body(place_ref, a_ref, b_ref, p_ref, o_ref):
        own = a_ref[...] + b_ref[...]
        o_ref[...] = ((own + p_ref[0].astype(F32)) + p_ref[1].astype(F32)) + p_ref[2].astype(F32)

    return pl.pallas_call(
        body, name=name, out_shape=jax.ShapeDtypeStruct((rows, cols), F32),
        grid_spec=pltpu.PrefetchScalarGridSpec(
            num_scalar_prefetch=1, grid=(nb,),
            in_specs=[pl.BlockSpec((None, tile, cols), lambda i, pr: (pr[0], pr[1] * nb + i, 0)),
                      pl.BlockSpec((None, tile, cols), lambda i, pr: (pr[0], i, 0)),
                      pl.BlockSpec((3, tile, cols), lambda i, pr: (0, i, 0))],
            out_specs=pl.BlockSpec((tile, cols), lambda i, pr: (i, 0))),
        compiler_params=_cparams(("parallel",)),
    )(place, slot, received, others)


def _reduce_begin(place, slots, names, tag):
    received = _swap_halves(slots, "reduce_swap_halves_" + tag)
    sums = [_sum_sibling(place, s, r, "sum_sibling_" + nm) for s, r, nm in zip(slots, received, names)]
    return received, _exchange_chips(sums)


def _reduce_end(place, slots, received, others, names, tag):
    mine = [_sum_chips(place, s, r, o, "sum_chips_" + nm) for s, r, o, nm in zip(slots, received, others, names)]
    return mine, _share_halves(mine, "reduce_share_halves_" + tag)


N_DEVICES = 8


def _sum_devices(block, name):
    def body(v_ref, o_ref, land_ref, send_sems, recv_sems):
        x, y, c = _place()
        me = 4 * x + 2 * y + c
        copies = []
        for mask in range(1, N_DEVICES):
            peer = (x ^ (mask >> 2), y ^ ((mask >> 1) & 1), c ^ (mask & 1))
            copies.append(pltpu.make_async_remote_copy(src_ref=v_ref, dst_ref=land_ref.at[me], send_sem=send_sems.at[mask - 1],
                                                       recv_sem=recv_sems.at[mask - 1], device_id=peer, device_id_type=MESH))
        for cp in copies:
            cp.start()
        land_ref[me] = v_ref[...]
        for cp in copies:
            cp.wait_recv()
        total = land_ref[0]
        for d in range(1, N_DEVICES):
            total = total + land_ref[d]
        o_ref[...] = total
        for cp in copies:
            cp.wait_send()

    vmem = pl.BlockSpec(memory_space=pltpu.VMEM)
    return pl.pallas_call(
        body, name=name, in_specs=[vmem], out_specs=vmem, out_shape=jax.ShapeDtypeStruct(block.shape, F32),
        scratch_shapes=[pltpu.VMEM((N_DEVICES,) + block.shape, F32), pltpu.SemaphoreType.DMA((N_DEVICES - 1,)),
                        pltpu.SemaphoreType.DMA((N_DEVICES - 1,))],
    )(block)


def _vec_block(g_mix, g_mlp, g_ple, g_final, b_forget, b_gate_rows):
    pad = lambda a: jnp.concatenate([a, jnp.zeros((a.shape[0], D_MODEL - a.shape[1]), F32)], axis=1)
    return jnp.concatenate([g_mix, g_mlp, g_ple, g_final.reshape(1, D_MODEL), pad(b_forget), pad(b_gate_rows),
                            jnp.zeros((1, D_MODEL), F32)], axis=0)


def _adam_math(w, g, m, v):
    m_new = ADAM_B1 * m + (1.0 - ADAM_B1) * g
    v_new = ADAM_B2 * v + (1.0 - ADAM_B2) * (g * g)
    m_hat = m_new / (1.0 - ADAM_B1 ** ADAM_STEP)
    v_hat = v_new / (1.0 - ADAM_B2 ** ADAM_STEP)
    return -ADAM_LR * (m_hat / (jnp.sqrt(v_hat) + ADAM_EPS) + ADAM_WD * w), m_new, v_new


def _adamw_halves(place, w, m, v, g_mine, g_theirs, name):
    rows2, cols = w.shape
    rows = rows2 // 2
    tile = _half_tile(rows)
    nb = rows // tile

    def body(place_ref, w_ref, m_ref, v_ref, gm_ref, gt_ref, g_ref, d_ref, nm_ref, nv_ref):
        g = jnp.where(pl.program_id(0) == 0, gm_ref[...], gt_ref[...])
        g_ref[...] = g
        d_ref[...], nm_ref[...], nv_ref[...] = _adam_math(w_ref[...], g, m_ref[...], v_ref[...])

    whole = pl.BlockSpec((tile, cols), lambda s, i, pr: ((pr[1] + s - 2 * pr[1] * s) * nb + i, 0))
    half = pl.BlockSpec((tile, cols), lambda s, i, pr: (i, 0))
    return pl.pallas_call(
        body, name=name, out_shape=[jax.ShapeDtypeStruct((rows2, cols), F32)] * 4,
        grid_spec=pltpu.PrefetchScalarGridSpec(num_scalar_prefetch=1, grid=(2, nb), in_specs=[whole] * 3 + [half] * 2,
                                               out_specs=[whole] * 4),
        compiler_params=_cparams(("parallel", "parallel")),
    )(place, w, m, v, g_mine, g_theirs)


def _adamw_vec(w, g, m, v):
    def body(w_ref, g_ref, m_ref, v_ref, d_ref, nm_ref, nv_ref):
        d_ref[...], nm_ref[...], nv_ref[...] = _adam_math(w_ref[...], g_ref[...], m_ref[...], v_ref[...])

    return pl.pallas_call(body, name="adamw_vectors", out_shape=[jax.ShapeDtypeStruct(w.shape, F32)] * 3)(w, g, m, v)


WEIGHT_NAMES = ("g_mix", "w_in", "b_forget", "b_gate", "w_branch_fox", "w_branch_sb", "w_out", "g_mlp", "w_up", "w_down",
                "g_ple", "w_ple_gate", "w_ple", "g_final")
W_IN_SHARD = D_IN // N_CHIPS
Q_END, F_END, B_END = 3 * D_ATT, 3 * D_ATT + N_HEADS, 6 * D_ATT + N_HEADS
GATE_SHARD = D_MODEL // N_CHIPS


def _join_cols(slots):
    return jnp.transpose(slots, (1, 0, 2)).reshape(slots.shape[1], N_CHIPS * slots.shape[2])


LATE = SHARDED[:1]
EARLY = SHARDED[1:]


def _first_weights(w_in_slots, b_gate):
    w_in = _join_cols(w_in_slots)
    forget = jnp.concatenate([w_in[:, Q_END:F_END], jnp.zeros((D_MODEL, F_PAD - N_HEADS), BF16)], axis=1)
    return {"qkv": jnp.concatenate([w_in[:, :Q_END], w_in[:, F_END:B_END]], axis=1), "gate": w_in[:, B_END:], "forget": forget,
            "b_gate": b_gate}


def _rest_weights(gathered):
    rows = lambda a: a.reshape(N_CHIPS * a.shape[1], a.shape[2])
    return {"branch_fox": gathered["w_branch_fox"], "branch_sb": gathered["w_branch_sb"], "out": rows(gathered["w_out"]),
            "up": gathered["w_up"], "down": rows(gathered["w_down"]), "ple_gate": rows(gathered["w_ple_gate"]),
            "ple": gathered["w_ple"]}


def _early_slots(gw):
    rows = lambda a: a.reshape(N_CHIPS, a.shape[0] // N_CHIPS, a.shape[1])
    return {"w_branch_fox": gw["branch_fox"], "w_branch_sb": gw["branch_sb"], "w_out": rows(gw["out"]), "w_up": gw["up"],
            "w_down": rows(gw["down"]), "w_ple_gate": rows(gw["ple_gate"]), "w_ple": gw["ple"]}


def _w_in_slots(gw):
    g_in = jnp.concatenate([gw["qkv"][:, :Q_END], gw["forget"][:, :N_HEADS], gw["qkv"][:, Q_END:], gw["gate"]], axis=1)
    return jnp.transpose(g_in.reshape(D_MODEL, N_CHIPS, W_IN_SHARD), (1, 0, 2))


def kernel(x, p, g_mix, w_in, b_forget, b_gate, w_branch_fox, w_branch_sb, w_out, g_mlp, w_up, w_down, g_ple, w_ple_gate, w_ple, g_final, loss_target, m_g_mix, m_w_in, m_b_forget, m_b_gate, m_w_branch_fox, m_w_branch_sb, m_w_out, m_g_mlp, m_w_up, m_w_down, m_g_ple, m_w_ple_gate, m_w_ple, m_g_final, v_g_mix, v_w_in, v_b_forget, v_b_gate, v_w_branch_fox, v_w_branch_sb, v_w_out, v_g_mlp, v_w_up, v_w_down, v_g_ple, v_w_ple_gate, v_w_ple, v_g_final):
    weights = dict(g_mix=g_mix, w_in=w_in, b_forget=b_forget, b_gate=b_gate, w_branch_fox=w_branch_fox,
                   w_branch_sb=w_branch_sb, w_out=w_out, g_mlp=g_mlp, w_up=w_up, w_down=w_down, g_ple=g_ple,
                   w_ple_gate=w_ple_gate, w_ple=w_ple, g_final=g_final)
    first = dict(g_mix=m_g_mix, w_in=m_w_in, b_forget=m_b_forget, b_gate=m_b_gate, w_branch_fox=m_w_branch_fox,
                 w_branch_sb=m_w_branch_sb, w_out=m_w_out, g_mlp=m_g_mlp, w_up=m_w_up, w_down=m_w_down, g_ple=m_g_ple,
                 w_ple_gate=m_w_ple_gate, w_ple=m_w_ple, g_final=m_g_final)
    second = dict(g_mix=v_g_mix, w_in=v_w_in, b_forget=v_b_forget, b_gate=v_b_gate, w_branch_fox=v_w_branch_fox,
                  w_branch_sb=v_w_branch_sb, w_out=v_w_out, g_mlp=v_g_mlp, w_up=v_w_up, w_down=v_w_down, g_ple=v_g_ple,
                  w_ple_gate=v_w_ple_gate, w_ple=v_w_ple, g_final=v_g_final)
    cx, cy, cc = _place()
    chip = 2 * cx + cy
    place = jnp.stack([chip, cc]).astype(jnp.int32)
    col0 = chip * GATE_SHARD

    (w_in_slots,) = _run_exchange(_gather_weights([weights[n][0].astype(BF16) for n in LATE]), "gather_w_in")
    rest = _gather_weights([weights[n][0].astype(BF16) for n in EARLY])
    gate_rows = lax.dynamic_update_slice(jnp.zeros((2, D_MODEL), F32), b_gate[0] * (cc == 0).astype(F32), (0, col0))
    zero_row = jnp.zeros((1, D_MODEL), F32)
    b_gate_whole = _sum_devices(_vec_block(zero_row, zero_row, zero_row, zero_row[0], zero_row[:, :N_HEADS], gate_rows),
                                "gather_b_gate")[5:7]
    vec = {"g_mix": g_mix, "b_forget": jnp.concatenate([b_forget, jnp.zeros((1, F_PAD - N_HEADS), F32)], axis=1),
           "g_mlp": g_mlp, "g_ple": g_ple, "g_final": g_final.reshape(1, D_MODEL)}

    loss, grad_x, reduced, gvec = _local_step(x, p[0], loss_target, _first_weights(w_in_slots, b_gate_whole), rest, vec,
                                              place)
    loss = lax.psum(loss, ("x", "y", "c"))

    out = {}
    for n in SHARDED:
        g_mine, g_theirs = reduced[n]
        res = _adamw_halves(place, weights[n][0], first[n][0], second[n][0], g_mine, g_theirs, "adamw_" + n)
        out[n] = [r[None] for r in res]

    g_block = _sum_devices(_vec_block(gvec["g_mix"], gvec["g_mlp"], gvec["g_ple"], gvec["g_final"][0], gvec["b_forget"],
                                      gvec["b_gate"]), "reduce_vectors")
    g_gate = lax.dynamic_slice(g_block[5:7], (0, col0), (2, GATE_SHARD))
    blocks = [_vec_block(d["g_mix"], d["g_mlp"], d["g_ple"], d["g_final"], d["b_forget"], d["b_gate"][0])
              for d in (weights, first, second)]
    g_rows = jnp.concatenate([g_block[0:5], jnp.concatenate([g_gate, jnp.zeros((2, D_MODEL - GATE_SHARD), F32)], axis=1),
                              jnp.zeros((1, D_MODEL), F32)], axis=0)
    res = (g_rows,) + tuple(_adamw_vec(blocks[0], g_rows, blocks[1], blocks[2]))
    out["g_mix"] = [r[0:1] for r in res]
    out["g_mlp"] = [r[1:2] for r in res]
    out["g_ple"] = [r[2:3] for r in res]
    out["g_final"] = [r[3] for r in res]
    out["b_forget"] = [r[4:5, :N_HEADS] for r in res]
    out["b_gate"] = [r[5:7, :GATE_SHARD][None] for r in res]
    return (loss, grad_x, *[out[n][0] for n in WEIGHT_NAMES], *[out[n][1] for n in WEIGHT_NAMES],
            *[out[n][2] for n in WEIGHT_NAMES], *[out[n][3] for n in WEIGHT_NAMES])
```

```python
import jax
import jax.numpy as jnp
from jax import lax
from jax.experimental import pallas as pl
from jax.experimental.pallas import tpu as pltpu

F32 = jnp.float32
BF16 = jnp.bfloat16

D_MODEL = 1024
HEAD_DIM = 64
N_HEADS = 8
D_ATT = N_HEADS * HEAD_DIM
D_FF = 4 * D_MODEL
D_PLE = 256
D_IN = 6 * D_ATT + N_HEADS + 2 * D_MODEL
F_PAD = 128
EPS = 1e-6
SCALE = HEAD_DIM ** -0.5
N_CHIPS = 4
LANES = 128
ATT_BLOCK = 256
FOX_TILES = (512, 512)
SB_TILES = (512, 256)
NEG = -1e30

ADAM_LR = 0.001
ADAM_B1 = 0.9
ADAM_B2 = 0.999
ADAM_EPS = 1e-08
ADAM_WD = 0.01
ADAM_STEP = 10

VMEM_LIMIT = 56 * 1024 * 1024

MESH = pl.DeviceIdType.MESH


def _cparams(sem=None):
    return pltpu.CompilerParams(dimension_semantics=sem, vmem_limit_bytes=VMEM_LIMIT)


def _relu2(t):
    t = t.astype(F32)
    return t * t


_DIMS = {"nn": (((1,), (0,)), ((), ())), "nt": (((1,), (1,)), ((), ())), "tn": (((0,), (0,)), ((), ()))}
NT_DIMS = _DIMS["nt"]
TN_DIMS = _DIMS["tn"]


def _mm(a, b, *, mode, name, out_dtype=F32, tm=512, tn=512, tk=512, add=None, a_fn=None, epi=None, extra=None,
        col_shards=False, behind=None):
    if mode == "nn":
        (m, k), n = a.shape, b.shape[-1]
    elif mode == "nt":
        (m, k), n = a.shape, b.shape[-2]
    else:
        (k, m), n = a.shape, b.shape[1]
    shard = None
    if col_shards:
        if mode == "nn":
            shard, n = n, N_CHIPS * n
            tn = min(tn, shard)
        elif mode == "nt":
            shard = b.shape[-1]
            tk = min(tk, shard)
        else:
            shard = n // N_CHIPS
            tn = min(tn, shard)
    tm, tn, tk = min(tm, m), min(tn, n), min(tk, k)
    assert m % tm == 0 and n % tn == 0 and k % tk == 0, (name, m, n, k)
    nk = k // tk
    a_spec = {"nn": pl.BlockSpec((tm, tk), lambda i, j, kk: (i, kk)),
              "nt": pl.BlockSpec((tm, tk), lambda i, j, kk: (i, kk)),
              "tn": pl.BlockSpec((tk, tm), lambda i, j, kk: (kk, i))}[mode]
    b_spec = {"nn": pl.BlockSpec((tk, tn), lambda i, j, kk: (kk, j)),
              "nt": pl.BlockSpec((tn, tk), lambda i, j, kk: (j, kk)),
              "tn": pl.BlockSpec((tk, tn), lambda i, j, kk: (kk, j))}[mode]
    o_spec = pl.BlockSpec((tm, tn), lambda i, j, kk: (i, j))
    out_shape = (m, n)
    if col_shards and mode == "nn":
        per = shard // tn
        b_spec = pl.BlockSpec((None, tk, tn), lambda i, j, kk: (j // per, kk, j % per))
    elif col_shards and mode == "nt":
        per = shard // tk
        b_spec = pl.BlockSpec((None, tn, tk), lambda i, j, kk: (kk // per, j, kk % per))
    elif col_shards:
        assert add is None and extra is None
        per = shard // tn
        o_spec = pl.BlockSpec((None, tm, tn), lambda i, j, kk: (j // per, i, j % per))
        out_shape = (N_CHIPS, m, shard)
    operands, in_specs = [a, b], [a_spec, b_spec]
    third = add if add is not None else extra
    if third is not None:
        operands.append(third)
        in_specs.append(o_spec)

    def body(*refs):
        a_ref, b_ref = refs[0], refs[1]
        t_ref = refs[2] if third is not None else None
        o_ref = refs[3] if third is not None else refs[2]
        acc_ref = refs[-1] if nk > 1 else None
        at = a_ref[...]
        if a_fn is not None:
            at = a_fn(at)
        part = lax.dot_general(at.astype(BF16), b_ref[...].astype(BF16), _DIMS[mode], preferred_element_type=F32)

        def finish(acc):
            if epi is not None:
                acc = epi(acc, None if t_ref is None else t_ref[...])
            elif add is not None:
                acc = acc + t_ref[...].astype(F32)
            o_ref[...] = acc.astype(o_ref.dtype)

        if nk == 1:
            finish(part)
        else:
            kk = pl.program_id(2)

            @pl.when(kk == 0)
            def _():
                acc_ref[...] = part

            @pl.when(kk > 0)
            def _():
                acc_ref[...] += part

            @pl.when(kk == nk - 1)
            def _():
                finish(acc_ref[...])

    call = dict(name=name, grid=(m // tm, n // tn, nk), in_specs=in_specs,
                scratch_shapes=[pltpu.VMEM((tm, tn), F32)] if nk > 1 else [])
    if behind is not None:
        (res,), exchanged = _call_behind(body, behind, out_specs=[o_spec], out_shape=[jax.ShapeDtypeStruct(out_shape, out_dtype)],
                                         operands=operands, **call)
        return res, exchanged
    return pl.pallas_call(body, out_specs=o_spec, out_shape=jax.ShapeDtypeStruct(out_shape, out_dtype),
                          compiler_params=_cparams(("parallel", "parallel", "arbitrary")), **call)(*operands)


ROW_TILE = 512


def _row_spec(width=D_MODEL, rows=ROW_TILE):
    return pl.BlockSpec((rows, width), lambda i: (i, 0))


def _vec_spec(rows=1, width=D_MODEL):
    return pl.BlockSpec((rows, width), lambda i: (0, 0))


def _xhat(x):
    r = lax.rsqrt(jnp.mean(x * x, axis=-1, keepdims=True) + EPS)
    return x * r, r


def _rms_bwd_rows(dh, x, g):
    xh, r = _xhat(x)
    dxh = dh * g
    dx = r * (dxh - xh * jnp.mean(dxh * xh, axis=-1, keepdims=True))
    return dx, jnp.sum(dh * xh, axis=0, keepdims=True)


def _norm_fwd(x, g, name):
    t = x.shape[0]

    def body(x_ref, g_ref, h_ref):
        xh, _ = _xhat(x_ref[...])
        h_ref[...] = (xh * g_ref[...]).astype(BF16)

    return pl.pallas_call(
        body, name=name, grid=(t // ROW_TILE,), in_specs=[_row_spec(), _vec_spec()], out_specs=_row_spec(),
        out_shape=jax.ShapeDtypeStruct((t, D_MODEL), BF16), compiler_params=_cparams(("parallel",)),
    )(x, g)


def _norm_bwd(x, g, dh, dres, name):
    t = x.shape[0]

    def body(x_ref, g_ref, dh_ref, dres_ref, dx_ref, dxb_ref, dg_ref):
        dx, dg = _rms_bwd_rows(dh_ref[...], x_ref[...], g_ref[...])
        dx = dx + dres_ref[...]
        dx_ref[...] = dx
        dxb_ref[...] = dx.astype(BF16)

        @pl.when(pl.program_id(0) == 0)
        def _():
            dg_ref[...] = jnp.zeros_like(dg_ref)

        dg_ref[...] += dg

    return pl.pallas_call(
        body, name=name, grid=(t // ROW_TILE,),
        in_specs=[_row_spec(), _vec_spec(), _row_spec(), _row_spec()],
        out_specs=[_row_spec(), _row_spec(), _vec_spec()],
        out_shape=[jax.ShapeDtypeStruct((t, D_MODEL), F32), jax.ShapeDtypeStruct((t, D_MODEL), BF16),
                   jax.ShapeDtypeStruct((1, D_MODEL), F32)],
        compiler_params=_cparams(("arbitrary",)),
    )(x, g, dh, dres)


def _gate_fwd(gl, b_gate, of, os_):
    t = of.shape[0]

    def body(gla_ref, glb_ref, b_ref, of_ref, os_ref, m_ref):
        ga = jax.nn.sigmoid(gla_ref[...] + b_ref[0:1, :])
        gb = jax.nn.sigmoid(glb_ref[...] + b_ref[1:2, :])
        m_ref[...] = (ga * of_ref[...] + gb * os_ref[...]).astype(BF16)

    return pl.pallas_call(
        body, name="gate_fwd", grid=(t // ROW_TILE,),
        in_specs=[pl.BlockSpec((ROW_TILE, D_MODEL), lambda i: (i, 0)), pl.BlockSpec((ROW_TILE, D_MODEL), lambda i: (i, 1)),
                  _vec_spec(2), _row_spec(), _row_spec()],
        out_specs=_row_spec(), out_shape=jax.ShapeDtypeStruct((t, D_MODEL), BF16),
        compiler_params=_cparams(("parallel",)),
    )(gl, gl, b_gate, of, os_)


def _gate_bwd(gl, b_gate, of, os_, dmerged):
    t = of.shape[0]

    def body(gla_ref, glb_ref, b_ref, of_ref, os_ref, dm_ref, dof_ref, dos_ref, dgla_ref, dglb_ref, db_ref):
        dm = dm_ref[...]
        ga = jax.nn.sigmoid(gla_ref[...] + b_ref[0:1, :])
        gb = jax.nn.sigmoid(glb_ref[...] + b_ref[1:2, :])
        dof_ref[...] = (dm * ga).astype(BF16)
        dos_ref[...] = (dm * gb).astype(BF16)
        dgla = dm * of_ref[...] * ga * (1.0 - ga)
        dglb = dm * os_ref[...] * gb * (1.0 - gb)
        dgla_ref[...] = dgla.astype(BF16)
        dglb_ref[...] = dglb.astype(BF16)

        @pl.when(pl.program_id(0) == 0)
        def _():
            db_ref[...] = jnp.zeros_like(db_ref)

        db_ref[0:1, :] += jnp.sum(dgla, axis=0, keepdims=True)
        db_ref[1:2, :] += jnp.sum(dglb, axis=0, keepdims=True)

    outs = pl.pallas_call(
        body, name="gate_bwd", grid=(t // ROW_TILE,),
        in_specs=[pl.BlockSpec((ROW_TILE, D_MODEL), lambda i: (i, 0)), pl.BlockSpec((ROW_TILE, D_MODEL), lambda i: (i, 1)),
                  _vec_spec(2), _row_spec(), _row_spec(), _row_spec()],
        out_specs=[_row_spec(), _row_spec(), _row_spec(), _row_spec(), _vec_spec(2)],
        out_shape=[jax.ShapeDtypeStruct((t, D_MODEL), BF16)] * 4 + [jax.ShapeDtypeStruct((2, D_MODEL), F32)],
        compiler_params=_cparams(("arbitrary",)),
    )(gl, gl, b_gate, of, os_, dmerged)
    return outs


def _head_and_loss(x2, gpre, pe, g_final, target):
    t = x2.shape[0]

    def body(x2_ref, gpre_ref, pe_ref, g_ref, tgt_ref, dx3_ref, dpre_ref, dpe_ref, dg_ref, loss_ref):
        gp = jax.nn.sigmoid(gpre_ref[...])
        pe_t = pe_ref[...]
        x3 = x2_ref[...] + gp * pe_t
        g = g_ref[...]
        xh, _ = _xhat(x3)
        err = xh * g - tgt_ref[...]
        dy = err * (1.0 / D_MODEL)
        dx3, dg = _rms_bwd_rows(dy, x3, g)
        dx3_ref[...] = dx3
        dpre_ref[...] = (dx3 * pe_t * gp * (1.0 - gp)).astype(BF16)
        dpe_ref[...] = (dx3 * gp).astype(BF16)

        @pl.when(pl.program_id(0) == 0)
        def _():
            dg_ref[...] = jnp.zeros_like(dg_ref)
            loss_ref[...] = jnp.zeros_like(loss_ref)

        dg_ref[...] += dg
        loss_ref[...] += 0.5 * jnp.sum(jnp.mean(err * err, axis=-1, keepdims=True), axis=0, keepdims=True)

    return pl.pallas_call(
        body, name="head_and_loss", grid=(t // ROW_TILE,),
        in_specs=[_row_spec(), _row_spec(), _row_spec(), _vec_spec(), _row_spec()],
        out_specs=[_row_spec(), _row_spec(), _row_spec(), _vec_spec(), _vec_spec(1, LANES)],
        out_shape=[jax.ShapeDtypeStruct((t, D_MODEL), F32), jax.ShapeDtypeStruct((t, D_MODEL), BF16),
                   jax.ShapeDtypeStruct((t, D_MODEL), BF16), jax.ShapeDtypeStruct((1, D_MODEL), F32),
                   jax.ShapeDtypeStruct((1, LANES), F32)],
        compiler_params=_cparams(("arbitrary",)),
    )(x2, gpre, pe, g_final, target)


def _split3(v):
    hi = v.astype(BF16)
    r1 = v - hi.astype(F32)
    mid = r1.astype(BF16)
    lo = (r1 - mid.astype(F32)).astype(BF16)
    return hi, mid, lo


def _split2(v):
    hi = v.astype(BF16)
    return hi, (v - hi.astype(F32)).astype(BF16)


def _dot(a, b, dims=_DIMS["nn"]):
    return lax.dot_general(a, b, dims, preferred_element_type=F32)


def _tri(n, rel):
    row = lax.broadcasted_iota(jnp.int32, (n, n), 0)
    col = lax.broadcasted_iota(jnp.int32, (n, n), 1)
    return rel(row, col).astype(BF16)


def _log_sigmoid(v):
    return -(jnp.maximum(-v, 0.0) + jnp.log(1.0 + jnp.exp(-jnp.abs(v))))


def _fox_prep(fl, b_forget, batch, seq):
    nb = seq // ATT_BLOCK

    def body(fl_ref, b_ref, cw_ref, cr_ref):
        col = lax.broadcasted_iota(jnp.int32, (ATT_BLOCK, F_PAD), 1)
        lower = _tri(ATT_BLOCK, lambda r, c: c <= r)
        upper = _tri(ATT_BLOCK, lambda r, c: r <= c)
        expand = (lax.broadcasted_iota(jnp.int32, (F_PAD, D_ATT), 1) // HEAD_DIM
                  == lax.broadcasted_iota(jnp.int32, (F_PAD, D_ATT), 0)).astype(BF16)
        carry_w = jnp.zeros((1, D_ATT), F32)
        carry_r = jnp.zeros((F_PAD, 1), F32)
        for i in range(nb):
            blk = slice(i * ATT_BLOCK, (i + 1) * ATT_BLOCK)
            logf = jnp.where(col < N_HEADS, _log_sigmoid(fl_ref[blk, :] + b_ref[...]), 0.0)
            cw = jnp.zeros((ATT_BLOCK, D_ATT), F32) + carry_w
            cr = jnp.zeros((F_PAD, ATT_BLOCK), F32) + carry_r
            for part in _split3(logf):
                cw += _dot(lower, _dot(part, expand).astype(BF16))
                cr += _dot(part, upper, TN_DIMS)
            cw_ref[blk, :] = cw
            cr_ref[:, blk] = cr[0:N_HEADS, :]
            carry_w = cw[ATT_BLOCK - 1:ATT_BLOCK, :]
            carry_r = cr[:, ATT_BLOCK - 1:ATT_BLOCK]

    return pl.pallas_call(
        body, name="fox_prep", grid=(batch,),
        in_specs=[pl.BlockSpec((seq, F_PAD), lambda b: (b, 0)), pl.BlockSpec((1, F_PAD), lambda b: (0, 0))],
        out_specs=[pl.BlockSpec((seq, D_ATT), lambda b: (b, 0)), pl.BlockSpec((N_HEADS, seq), lambda b: (b, 0))],
        out_shape=[jax.ShapeDtypeStruct((batch * seq, D_ATT), F32), jax.ShapeDtypeStruct((batch * N_HEADS, seq), F32)],
        compiler_params=_cparams(("parallel",)),
    )(fl, b_forget)


def _fox_post(dcs_wide, drs_wide, fl, b_forget, batch, seq):
    nb = seq // ATT_BLOCK

    def body(dcs_ref, drs_ref, fl_ref, b_ref, dfl_ref, db_ref):
        pick = (lax.broadcasted_iota(jnp.int32, (D_ATT, F_PAD), 0)
                == lax.broadcasted_iota(jnp.int32, (D_ATT, F_PAD), 1) * HEAD_DIM).astype(BF16)
        upper = _tri(ATT_BLOCK, lambda r, c: r <= c)
        col = lax.broadcasted_iota(jnp.int32, (ATT_BLOCK, F_PAD), 1)

        @pl.when(pl.program_id(0) == 0)
        def _():
            db_ref[...] = jnp.zeros_like(db_ref)

        carry = jnp.zeros((1, F_PAD), F32)
        for i in reversed(range(nb)):
            blk = slice(i * ATT_BLOCK, (i + 1) * ATT_BLOCK)
            narrow = jnp.zeros((ATT_BLOCK, F_PAD), F32)
            for part in _split3(drs_ref[blk, :] - dcs_ref[blk, :]):
                narrow += _dot(part, pick)
            after = jnp.zeros((ATT_BLOCK, F_PAD), F32) + carry
            for part in _split3(narrow):
                after += _dot(upper, part)
            carry = after[0:1, :]
            pre = fl_ref[blk, :] + b_ref[...]
            dfl = jnp.where(col < N_HEADS, after * jax.nn.sigmoid(-pre), 0.0)
            dfl_ref[blk, :] = dfl.astype(BF16)
            db_ref[...] += jnp.sum(dfl, axis=0, keepdims=True)

    return pl.pallas_call(
        body, name="fox_post", grid=(batch,),
        in_specs=[pl.BlockSpec((seq, D_ATT), lambda b: (b, 0)), pl.BlockSpec((seq, D_ATT), lambda b: (b, 0)),
                  pl.BlockSpec((seq, F_PAD), lambda b: (b, 0)), pl.BlockSpec((1, F_PAD), lambda b: (0, 0))],
        out_specs=[pl.BlockSpec((seq, F_PAD), lambda b: (b, 0)), pl.BlockSpec((1, F_PAD), lambda b: (0, 0))],
        out_shape=[jax.ShapeDtypeStruct((batch * seq, F_PAD), BF16), jax.ShapeDtypeStruct((1, F_PAD), F32)],
        compiler_params=_cparams(("arbitrary",)),
    )(dcs_wide, drs_wide, fl, b_forget)


N_PAIRS = N_HEADS // 2


def _att_specs(seq, col0, tq):
    nq = seq // tq
    q = pl.BlockSpec((tq, LANES), lambda b, hp, qi: (b * nq + qi, col0 + hp))
    k = pl.BlockSpec((seq, LANES), lambda b, hp, qi: (b, col0 + N_PAIRS + hp))
    v = pl.BlockSpec((seq, LANES), lambda b, hp, qi: (b, col0 + 2 * N_PAIRS + hp))
    return q, k, v


def _qblock_spec(seq, tq):
    nq = seq // tq
    return pl.BlockSpec((tq, LANES), lambda b, hp, qi: (b * nq + qi, hp))


def _kv_out_spec(seq):
    return pl.BlockSpec((seq, LANES), lambda b, hp, qi: (b, hp))


def _head_masks():
    lane = lax.broadcasted_iota(jnp.int32, (1, LANES), 1)
    return [(lane >= HEAD_DIM * j) & (lane < HEAD_DIM * (j + 1)) for j in range(2)]


def _stack_heads(t, masks):
    zero = jnp.zeros_like(t)
    return jnp.concatenate([jnp.where(masks[0], t, zero), jnp.where(masks[1], t, zero)], axis=0)


def _stack_cols(t):
    return jnp.concatenate([t[:, 0:1], t[:, HEAD_DIM:HEAD_DIM + 1]], axis=0)


def _unstack(t2, masks):
    tq = t2.shape[0] // 2
    return jnp.where(masks[0], t2[:tq], t2[tq:])


def _stacked_ids(tq, tk):
    row = lax.broadcasted_iota(jnp.int32, (2 * tq, tk), 0)
    col = lax.broadcasted_iota(jnp.int32, (2 * tq, tk), 1)
    first = lax.broadcasted_iota(jnp.int32, (2 * tq, 1), 0) < tq
    return col - jnp.where(row < tq, row, row - tq), first


def _sweep(qi, tq, tk, step, init, leftward):
    per = tq // tk
    whole = lambda carry: lax.fori_loop(0, per * qi, lambda i, c: step(per * qi - 1 - i if leftward else i, c, None), carry)
    crossed = [(per * qi + j, -j * tk) for j in range(per)]
    if leftward:
        carry = init
        for kb, lead in reversed(crossed):
            carry = step(kb, carry, lead)
        return whole(carry)
    carry = whole(init)
    for kb, lead in crossed:
        carry = step(kb, carry, lead)
    return carry


def _fox_fwd(qkv, c_wide, c_row, batch, seq):
    tq, tk = FOX_TILES
    nq = seq // tq

    def body(q_ref, k_ref, v_ref, cw_ref, cr_ref, o_ref, lse_ref):
        hp, qi = pl.program_id(1), pl.program_id(2)
        masks = _head_masks()
        ahead, first = _stacked_ids(tq, tk)
        q2 = _stack_heads(q_ref[...], masks)
        ct = _stack_cols(cw_ref[...])

        def step(kb, carry, lead):
            m, l, acc = carry
            k0 = pl.multiple_of(kb * tk, tk)
            cs = jnp.where(first, cr_ref[pl.ds(2 * hp, 1), pl.ds(k0, tk)], cr_ref[pl.ds(2 * hp + 1, 1), pl.ds(k0, tk)])
            s = _dot(q2, k_ref[pl.ds(k0, tk), :], NT_DIMS) * SCALE + ct - cs
            if lead is not None:
                s = jnp.where(ahead <= lead, s, NEG)
            m_new = jnp.maximum(m, jnp.max(s, axis=1, keepdims=True))
            p = jnp.exp(s - m_new)
            alpha = jnp.exp(m - m_new)
            l = alpha * l + jnp.sum(p, axis=1, keepdims=True)
            acc = alpha * acc + _dot(p.astype(BF16), v_ref[pl.ds(k0, tk), :])
            return m_new, l, acc

        init = (jnp.full((2 * tq, 1), NEG, F32), jnp.zeros((2 * tq, 1), F32), jnp.zeros((2 * tq, LANES), F32))
        m, l, acc = _sweep(qi, tq, tk, step, init, leftward=False)
        o_ref[...] = _unstack(acc / l, masks).astype(BF16)
        lse_ref[...] = _unstack(m + jnp.log(l), masks)

    q_spec, k_spec, v_spec = _att_specs(seq, 0, tq)
    qb = _qblock_spec(seq, tq)
    return pl.pallas_call(
        body, name="fox_fwd", grid=(batch, N_PAIRS, nq),
        in_specs=[q_spec, k_spec, v_spec, qb, pl.BlockSpec((N_HEADS, seq), lambda b, hp, qi: (b, 0))],
        out_specs=[qb, qb],
        out_shape=[jax.ShapeDtypeStruct((batch * seq, D_ATT), BF16), jax.ShapeDtypeStruct((batch * seq, D_ATT), F32)],
        compiler_params=_cparams(("parallel", "parallel", "arbitrary")),
    )(qkv, qkv, qkv, c_wide, c_row)


def _fox_bwd(qkv, c_wide, c_row, o, do, lse_wide, batch, seq, behind):
    tq, tk = FOX_TILES
    nq = seq // tq

    def body(q_ref, k_ref, v_ref, cw_ref, cr_ref, o_ref, do_ref, lse_ref,
             dq_ref, dk_ref, dv_ref, dcs_ref, drs_ref, dkc_acc, dv_acc):
        hp, qi = pl.program_id(1), pl.program_id(2)

        @pl.when(qi == 0)
        def _():
            dkc_acc[...] = jnp.zeros_like(dkc_acc)
            dv_acc[...] = jnp.zeros_like(dv_acc)

        masks = _head_masks()
        ahead, first = _stacked_ids(tq, tk)
        q_t, do_t = q_ref[...], do_ref[...]
        q2 = _stack_heads(q_t, masks)
        do2 = _stack_heads(do_t, masks)
        q_and_ones = jnp.concatenate([q2, _stack_heads(jnp.ones_like(q_t), masks)], axis=1)
        ct = _stack_cols(cw_ref[...])
        lse = _stack_cols(lse_ref[...])
        prod = do_t.astype(F32) * o_ref[...].astype(F32)
        delta = jnp.concatenate([jnp.sum(jnp.where(mk, prod, 0.0), axis=1, keepdims=True) for mk in masks], axis=0)

        def step(kb, carry, lead):
            dq_acc, rs = carry
            k0 = pl.multiple_of(kb * tk, tk)
            kblk = k_ref[pl.ds(k0, tk), :]
            cs = jnp.where(first, cr_ref[pl.ds(2 * hp, 1), pl.ds(k0, tk)], cr_ref[pl.ds(2 * hp + 1, 1), pl.ds(k0, tk)])
            p = jnp.exp(_dot(q2, kblk, NT_DIMS) * SCALE + ct - cs - lse)
            if lead is not None:
                p = jnp.where(ahead <= lead, p, 0.0)
            dp = _dot(do2, v_ref[pl.ds(k0, tk), :], NT_DIMS)
            ds = (p * (dp - delta) * SCALE).astype(BF16)
            dkc_acc[pl.ds(k0, tk), :] += _dot(ds, q_and_ones, TN_DIMS)
            dv_acc[pl.ds(k0, tk), :] += _dot(p.astype(BF16), do2, TN_DIMS)
            return dq_acc + _dot(ds, kblk), rs + jnp.sum(ds.astype(F32), axis=1, keepdims=True)

        init = (jnp.zeros((2 * tq, LANES), F32), jnp.zeros((2 * tq, 1), F32))
        dq_acc, rs = _sweep(qi, tq, tk, step, init, leftward=False)
        dq_ref[...] = _unstack(dq_acc, masks).astype(BF16)
        drs_ref[...] = _unstack(rs, masks) * (1.0 / SCALE)

        @pl.when(qi == nq - 1)
        def _():
            dk_ref[...] = dkc_acc[:, 0:LANES].astype(BF16)
            dcs_ref[...] = dkc_acc[:, LANES:2 * LANES] * (1.0 / SCALE)
            dv_ref[...] = dv_acc[...].astype(BF16)

    q_spec, k_spec, v_spec = _att_specs(seq, 0, tq)
    qb = _qblock_spec(seq, tq)
    return _call_behind(
        body, behind, name="fox_bwd", grid=(batch, N_PAIRS, nq),
        in_specs=[q_spec, k_spec, v_spec, qb, pl.BlockSpec((N_HEADS, seq), lambda b, hp, qi: (b, 0)), qb, qb, qb],
        out_specs=[qb, _kv_out_spec(seq), _kv_out_spec(seq), _kv_out_spec(seq), qb],
        out_shape=[jax.ShapeDtypeStruct((batch * seq, D_ATT), BF16)] * 3 + [jax.ShapeDtypeStruct((batch * seq, D_ATT), F32)] * 2,
        scratch_shapes=[pltpu.VMEM((seq, 2 * LANES), F32), pltpu.VMEM((seq, LANES), F32)],
        operands=(qkv, qkv, qkv, c_wide, c_row, o, do, lse_wide))


def _sb_logits(q2, kblk):
    z = _dot(q2, kblk, NT_DIMS) * SCALE
    lsn = -(jnp.maximum(z, 0.0) + jnp.log(1.0 + jnp.exp(-jnp.abs(z))))
    return lsn + z, lsn


def _sb_fwd(qkv, batch, seq, behind):
    tq, tk = SB_TILES
    nq = seq // tq

    def body(q_ref, k_ref, v_ref, o_ref, rt_ref):
        qi = pl.program_id(2)
        masks = _head_masks()
        ahead, _ = _stacked_ids(tq, tk)
        later = _tri(tk, lambda r, c: r > c)
        q2 = _stack_heads(q_ref[...], masks)

        def step(kb, carry, lead):
            run, acc = carry
            k0 = pl.multiple_of(kb * tk, tk)
            ls, lsn = _sb_logits(q2, k_ref[pl.ds(k0, tk), :])
            if lead is not None:
                lsn = jnp.where(ahead < lead, lsn, 0.0)
            hi, lo = _split2(lsn)
            w = jnp.exp(ls + _dot(hi, later) + _dot(lo, later) + run)
            if lead is not None:
                w = jnp.where(ahead < lead, w, 0.0)
            return run + jnp.sum(lsn, axis=1, keepdims=True), acc + _dot(w.astype(BF16), v_ref[pl.ds(k0, tk), :])

        init = (jnp.zeros((2 * tq, 1), F32), jnp.zeros((2 * tq, LANES), F32))
        run, acc = _sweep(qi, tq, tk, step, init, leftward=True)
        o_ref[...] = _unstack(acc, masks).astype(BF16)
        rt_ref[...] = _unstack(run, masks)

    q_spec, k_spec, v_spec = _att_specs(seq, 3 * N_PAIRS, tq)
    qb = _qblock_spec(seq, tq)
    return _call_behind(
        body, behind, name="sb_fwd", grid=(batch, N_PAIRS, nq), in_specs=[q_spec, k_spec, v_spec], out_specs=[qb, qb],
        out_shape=[jax.ShapeDtypeStruct((batch * seq, D_ATT), BF16), jax.ShapeDtypeStruct((batch * seq, D_ATT), F32)],
        scratch_shapes=[], operands=(qkv, qkv, qkv))


def _sb_bwd(qkv, do, rt_wide, batch, seq, behind):
    tq, tk = SB_TILES
    nq = seq // tq

    def body(q_ref, k_ref, v_ref, do_ref, rt_ref, dq_ref, dk_ref, dv_ref, dk_acc, dv_acc):
        qi = pl.program_id(2)

        @pl.when(qi == 0)
        def _():
            dk_acc[...] = jnp.zeros_like(dk_acc)
            dv_acc[...] = jnp.zeros_like(dv_acc)

        masks = _head_masks()
        ahead, _ = _stacked_ids(tq, tk)
        later = _tri(tk, lambda r, c: r > c)
        earlier = _tri(tk, lambda r, c: r < c)
        q2 = _stack_heads(q_ref[...], masks)
        do2 = _stack_heads(do_ref[...], masks)
        total = _stack_cols(rt_ref[...])

        def step(kb, carry, lead):
            pref, epre, dq_acc = carry
            k0 = pl.multiple_of(kb * tk, tk)
            kblk = k_ref[pl.ds(k0, tk), :]
            ls, lsn_all = _sb_logits(q2, kblk)
            lsn = lsn_all if lead is None else jnp.where(ahead < lead, lsn_all, 0.0)
            rs = jnp.sum(lsn, axis=1, keepdims=True)
            hi, lo = _split2(lsn)
            w = jnp.exp(ls + _dot(hi, later) + _dot(lo, later) + (total - pref - rs))
            if lead is not None:
                w = jnp.where(ahead < lead, w, 0.0)
            e = w * _dot(do2, v_ref[pl.ds(k0, tk), :], NT_DIMS)
            ehi, elo = _split2(e)
            before = _dot(ehi, earlier) + _dot(elo, earlier) + epre
            dz = e * jnp.exp(lsn_all) - jnp.exp(ls) * before
            if lead is not None:
                dz = jnp.where(ahead < lead, dz, 0.0)
            dz = (dz * SCALE).astype(BF16)
            dk_acc[pl.ds(k0, tk), :] += _dot(dz, q2, TN_DIMS)
            dv_acc[pl.ds(k0, tk), :] += _dot(w.astype(BF16), do2, TN_DIMS)
            return pref + rs, epre + jnp.sum(e, axis=1, keepdims=True), dq_acc + _dot(dz, kblk)

        init = (jnp.zeros((2 * tq, 1), F32), jnp.zeros((2 * tq, 1), F32), jnp.zeros((2 * tq, LANES), F32))
        dq_acc = _sweep(qi, tq, tk, step, init, leftward=False)[2]
        dq_ref[...] = _unstack(dq_acc, masks).astype(BF16)

        @pl.when(qi == nq - 1)
        def _():
            dk_ref[...] = dk_acc[...].astype(BF16)
            dv_ref[...] = dv_acc[...].astype(BF16)

    q_spec, k_spec, v_spec = _att_specs(seq, 3 * N_PAIRS, tq)
    qb = _qblock_spec(seq, tq)
    return _call_behind(
        body, behind, name="sb_bwd", grid=(batch, N_PAIRS, nq), in_specs=[q_spec, k_spec, v_spec, qb, qb],
        out_specs=[qb, _kv_out_spec(seq), _kv_out_spec(seq)], out_shape=[jax.ShapeDtypeStruct((batch * seq, D_ATT), BF16)] * 3,
        scratch_shapes=[pltpu.VMEM((seq, LANES), F32), pltpu.VMEM((seq, LANES), F32)], operands=(qkv, qkv, qkv, do, rt_wide))


def _local_step(x, p, target, w, rest, vec, place):
    batch, seq, _ = x.shape
    t = batch * seq
    x = x.reshape(t, D_MODEL)
    target = target.reshape(t, D_MODEL)
    p = p.reshape(t, D_PLE)
    big = dict(tm=1024, tn=1024, tk=1024)

    h1 = _norm_fwd(x, vec["g_mix"], "norm_mix")
    qkv = _mm(h1, w["qkv"], mode="nn", name="proj_qkv", out_dtype=BF16, **big)
    gl = _mm(h1, w["gate"], mode="nn", name="proj_gate", **big)
    fl = _mm(h1, w["forget"], mode="nn", name="proj_forget", **big)
    c_wide, c_row = _fox_prep(fl, vec["b_forget"], batch, seq)
    o_fox, lse_wide = _fox_fwd(qkv, c_wide, c_row, batch, seq)
    (o_sb, rt_wide), gathered = _sb_fwd(qkv, batch, seq, rest)
    w = dict(w, **_rest_weights(dict(zip(EARLY, gathered))))
    of = _mm(o_fox, w["branch_fox"], mode="nn", name="branch_fox", col_shards=True, **big)
    os_ = _mm(o_sb, w["branch_sb"], mode="nn", name="branch_sb", col_shards=True, **big)
    merged = _gate_fwd(gl, w["b_gate"], of, os_)
    x1 = _mm(merged, w["out"], mode="nn", name="proj_out", add=x, **big)
    h2 = _norm_fwd(x1, vec["g_mlp"], "norm_mlp")
    ar = _mm(h2, w["up"], mode="nn", name="mlp_up", out_dtype=BF16, epi=lambda acc, _: jnp.maximum(acc, 0.0),
             col_shards=True, **big)
    x2 = _mm(ar, w["down"], mode="nn", name="mlp_down", a_fn=_relu2, add=x1, **big)
    h3 = _norm_fwd(x2, vec["g_ple"], "norm_ple")
    gpre = _mm(h3, w["ple_gate"], mode="nn", name="ple_gate", **big)
    pe = _mm(p, w["ple"], mode="nn", name="ple_embed", col_shards=True, **big)

    dx3, dpre, dpe, dg_final, loss = _head_and_loss(x2, gpre, pe, vec["g_final"], target)
    gw = {}
    gw["ple"] = _mm(p, dpe, mode="tn", name="d_w_ple", col_shards=True, **big)
    gw["ple_gate"] = _mm(h3, dpre, mode="tn", name="d_w_ple_gate", **big)
    dh3 = _mm(dpre, w["ple_gate"], mode="nt", name="d_h_ple", **big)
    dx2, dx2b, dg_ple = _norm_bwd(x2, vec["g_ple"], dh3, dx3, "norm_ple_bwd")
    gw["down"] = _mm(ar, dx2b, mode="tn", name="d_w_down", a_fn=_relu2, **big)
    da = _mm(dx2b, w["down"], mode="nt", name="d_act", out_dtype=BF16,
             epi=lambda acc, r: acc * (2.0 * r.astype(F32)), extra=ar, **big)
    gw["up"] = _mm(h2, da, mode="tn", name="d_w_up", col_shards=True, **big)
    dh2 = _mm(da, w["up"], mode="nt", name="d_h_mlp", col_shards=True, **big)
    dx1, dx1b, dg_mlp = _norm_bwd(x1, vec["g_mlp"], dh2, dx2, "norm_mlp_bwd")
    gw["out"] = _mm(merged, dx1b, mode="tn", name="d_w_out", **big)
    dmerged = _mm(dx1b, w["out"], mode="nt", name="d_merged", **big)
    dof, dos, dgla, dglb, gw["b_gate"] = _gate_bwd(gl, w["b_gate"], of, os_, dmerged)
    gw["branch_fox"] = _mm(o_fox, dof, mode="tn", name="d_w_branch_fox", col_shards=True, **big)
    gw["branch_sb"] = _mm(o_sb, dos, mode="tn", name="d_w_branch_sb", col_shards=True, **big)
    do_fox = _mm(dof, w["branch_fox"], mode="nt", name="d_o_fox", out_dtype=BF16, col_shards=True, **big)
    do_sb = _mm(dos, w["branch_sb"], mode="nt", name="d_o_sb", out_dtype=BF16, col_shards=True, **big)
    early = _early_slots(gw)
    early = [early[n] for n in EARLY]
    (dq_a, dk_a, dv_a, dcs_wide, drs_wide), received = _fox_bwd(qkv, c_wide, c_row, o_fox, do_fox, lse_wide, batch, seq,
                                                                _swap_halves(early))
    sums = [_sum_sibling(place, s, r, "sum_sibling_" + n) for s, r, n in zip(early, received, EARLY)]
    (dq_b, dk_b, dv_b), others = _sb_bwd(qkv, do_sb, rt_wide, batch, seq, _exchange_chips(sums))
    mine = [_sum_chips(place, s, r, o, "sum_chips_" + n) for s, r, o, n in zip(early, received, others, EARLY)]
    dfl, db_forget = _fox_post(dcs_wide, drs_wide, fl, vec["b_forget"], batch, seq)
    dqkv = jnp.concatenate([dq_a, dk_a, dv_a, dq_b, dk_b, dv_b], axis=1)
    dgl = jnp.concatenate([dgla, dglb], axis=1)
    gw["qkv"], theirs = _mm(h1, dqkv, mode="tn", name="d_w_qkv", behind=_share_halves(mine), **big)
    reduced = dict(zip(EARLY, zip(mine, theirs)))
    gw["gate"] = _mm(h1, dgl, mode="tn", name="d_w_gate", **big)
    gw["forget"] = _mm(h1, dfl, mode="tn", name="d_w_forget", **big)
    late = [_w_in_slots(gw)]
    dh1, received = _mm(dqkv, w["qkv"], mode="nt", name="d_h_qkv", behind=_swap_halves(late), **big)
    sums = [_sum_sibling(place, late[0], received[0], "sum_sibling_w_in")]
    dh1, others = _mm(dgl, w["gate"], mode="nt", name="d_h_gate", add=dh1, behind=_exchange_chips(sums), **big)
    dh1 = _mm(dfl, w["forget"], mode="nt", name="d_h_forget", add=dh1, **big)
    grad_x, _, dg_mix = _norm_bwd(x, vec["g_mix"], dh1, dx1, "norm_mix_bwd")
    mine = [_sum_chips(place, late[0], received[0], others[0], "sum_chips_w_in")]
    reduced["w_in"] = (mine[0], _run_exchange(_share_halves(mine), "reduce_share_w_in")[0])
    gvec = {"g_mix": dg_mix, "b_forget": db_forget[:, 0:N_HEADS], "g_mlp": dg_mlp, "g_ple": dg_ple,
            "g_final": dg_final, "b_gate": gw["b_gate"]}
    return loss, grad_x.reshape(batch, seq, D_MODEL), reduced, gvec


ANY = pl.BlockSpec(memory_space=pl.ANY)
SHARDED = ("w_in", "w_branch_fox", "w_branch_sb", "w_out", "w_up", "w_down", "w_ple_gate", "w_ple")
ROW_ALIGN = 16


def _place():
    return lax.axis_index("x"), lax.axis_index("y"), lax.axis_index("c")


def _other_chips(x, y):
    return [(1 - x, y), (x, 1 - y), (1 - x, 1 - y)]


def _half(ref, h):
    r = ref.shape[0] // 2
    assert r % ROW_ALIGN == 0
    return ref.at[pl.ds(pl.multiple_of(h * r, ROW_ALIGN), r)]


def _remote(src, dst, sems, idx, to):
    send_sems, recv_sems = sems
    return pltpu.make_async_remote_copy(src_ref=src, dst_ref=dst, send_sem=send_sems.at[idx], recv_sem=recv_sems.at[idx],
                                        device_id=to, device_id_type=MESH)


class _Exchange:
    def __init__(self, operands, out_shapes, sem_shape, start, finish):
        self.operands, self.out_shapes, self.sem_shape, self.start, self.finish = operands, out_shapes, sem_shape, start, finish

    def scratch(self):
        return [pltpu.SemaphoreType.DMA(self.sem_shape), pltpu.SemaphoreType.DMA(self.sem_shape)]


def _run_exchange(ex, name):
    n = len(ex.operands)

    def body(*refs):
        ex.start(refs[:n], refs[n:2 * n], refs[2 * n:])
        ex.finish(refs[:n], refs[n:2 * n], refs[2 * n:])

    return pl.pallas_call(body, name=name, in_specs=[ANY] * n, out_specs=[ANY] * n, out_shape=ex.out_shapes,
                          scratch_shapes=ex.scratch())(*ex.operands)


def _call_behind(body, ex, *, name, grid, in_specs, out_specs, out_shape, scratch_shapes, operands):
    n_in, n_out, nx = len(in_specs), len(out_specs), len(ex.operands)

    def wrapped(*refs):
        ins, x_in = refs[:n_in], refs[n_in:n_in + nx]
        outs, x_out = refs[n_in + nx:n_in + nx + n_out], refs[n_in + nx + n_out:n_in + 2 * nx + n_out]
        scratch, sems = refs[n_in + 2 * nx + n_out:-2], refs[-2:]
        first, last = None, None
        for d, steps in enumerate(grid):
            at_start, at_end = pl.program_id(d) == 0, pl.program_id(d) == steps - 1
            first = at_start if first is None else first & at_start
            last = at_end if last is None else last & at_end

        @pl.when(first)
        def _():
            ex.start(x_in, x_out, sems)

        body(*ins, *outs, *scratch)

        @pl.when(last)
        def _():
            ex.finish(x_in, x_out, sems)

    res = pl.pallas_call(
        wrapped, name=name, grid=grid, in_specs=list(in_specs) + [ANY] * nx, out_specs=list(out_specs) + [ANY] * nx,
        out_shape=list(out_shape) + list(ex.out_shapes), scratch_shapes=list(scratch_shapes) + ex.scratch(),
        compiler_params=_cparams(("arbitrary",) * len(grid)),
    )(*operands, *ex.operands)
    return res[:n_out], res[n_out:]


def _gather_weights(shards):
    n = len(shards)

    def first_copies(src, out, sems):
        x, y, c = _place()
        me = 2 * x + y
        copies = [_remote(_half(src[t], c), _half(out[t].at[me], c), sems, (t, k), (px, py, c))
                  for t in range(n) for k, (px, py) in enumerate(_other_chips(x, y))]
        return copies + [_remote(src[t], out[t].at[me], sems, (t, 3), (x, y, 1 - c)) for t in range(n)]

    def start(src, out, sems):
        for cp in first_copies(src, out, sems):
            cp.start()

    def finish(src, out, sems):
        x, y, c = _place()
        me = 2 * x + y
        sibling = (x, y, 1 - c)
        chips = _other_chips(x, y)
        passes = []
        for t in range(n):
            for k, (px, py) in enumerate(chips):
                landed = _half(out[t].at[2 * px + py], c)
                _remote(landed, landed, sems, (t, k), (px, py, c)).wait_recv()
                passes.append(_remote(landed, landed, sems, (t, 4 + k), sibling))
                passes[-1].start()
        for t in range(n):
            _remote(src[t], out[t].at[me], sems, (t, 3), sibling).wait_recv()
            for k, (px, py) in enumerate(chips):
                passed = _half(out[t].at[2 * px + py], 1 - c)
                _remote(passed, passed, sems, (t, 4 + k), sibling).wait_recv()
        for cp in first_copies(src, out, sems) + passes:
            cp.wait_send()

    return _Exchange(shards, [jax.ShapeDtypeStruct((N_CHIPS,) + s.shape, s.dtype) for s in shards], (n, 7), start, finish)


def _simple_exchange(operands, out_shapes, copies):
    def start(src, out, sems):
        for cp in copies(src, out, sems):
            cp.start()

    def finish(src, out, sems):
        for cp in copies(src, out, sems):
            cp.wait_recv()
        for cp in copies(src, out, sems):
            cp.wait_send()

    return _Exchange(operands, out_shapes, (len(operands),), start, finish)


def _swap_halves(slots):
    def copies(src, out, sems):
        x, y, c = _place()
        res = []
        for t in range(len(slots)):
            r = src[t].shape[1] // 2
            rows = pl.ds(pl.multiple_of((1 - c) * r, ROW_ALIGN), r)
            res.append(_remote(src[t].at[:, rows], out[t], sems, t, (x, y, 1 - c)))
        return res

    return _simple_exchange(slots, [jax.ShapeDtypeStruct((N_CHIPS, s.shape[1] // 2, s.shape[2]), s.dtype) for s in slots], copies)


def _exchange_chips(sums):
    n = len(sums)

    def copies(src, out, sems):
        x, y, c = _place()
        return [_remote(src[t].at[2 * px + py], out[t].at[k], sems, (t, k), (px, py, c))
                for t in range(n) for k, (px, py) in enumerate(_other_chips(x, y))]

    def start(src, out, sems):
        for cp in copies(src, out, sems):
            cp.start()

    def finish(src, out, sems):
        for cp in copies(src, out, sems):
            cp.wait_recv()
        for cp in copies(src, out, sems):
            cp.wait_send()

    return _Exchange(sums, [jax.ShapeDtypeStruct((3,) + s.shape[1:], s.dtype) for s in sums], (n, 3), start, finish)


def _share_halves(mine):
    def copies(src, out, sems):
        x, y, c = _place()
        return [_remote(src[t], out[t], sems, t, (x, y, 1 - c)) for t in range(len(mine))]

    return _simple_exchange(mine, [jax.ShapeDtypeStruct(s.shape, s.dtype) for s in mine], copies)


def _half_tile(rows):
    return min(rows, 256)


def _sum_sibling(place, slot, received, name):
    n, rows2, cols = slot.shape
    rows = rows2 // 2
    tile = _half_tile(rows)
    nb = rows // tile

    def body(place_ref, a_ref, b_ref, o_ref):
        o_ref[...] = (a_ref[...] + b_ref[...]).astype(BF16)

    return pl.pallas_call(
        body, name=name, out_shape=jax.ShapeDtypeStruct((n, rows, cols), BF16),
        grid_spec=pltpu.PrefetchScalarGridSpec(
            num_scalar_prefetch=1, grid=(n, nb),
            in_specs=[pl.BlockSpec((None, tile, cols), lambda j, i, pr: (j, pr[1] * nb + i, 0)),
                      pl.BlockSpec((None, tile, cols), lambda j, i, pr: (j, i, 0))],
            out_specs=pl.BlockSpec((None, tile, cols), lambda j, i, pr: (j, i, 0))),
        compiler_params=_cparams(("parallel", "parallel")),
    )(place, slot, received)


def _sum_chips(place, slot, received, others, name):
    _, rows2, cols = slot.shape
    rows = rows2 // 2
    tile = _half_tile(rows)
    nb = rows // tile

    def body(place_ref, a_ref, b_ref, p_ref, o_ref):
        own = a_ref[...] + b_ref[...]
        o_ref[...] = ((own + p_ref[0].astype(F32)) + p_ref[1].astype(F32)) + p_ref[2].astype(F32)

    return pl.pallas_call(
        body, name=name, out_shape=jax.ShapeDtypeStruct((rows, cols), F32),
        grid_spec=pltpu.PrefetchScalarGridSpec(
            num_scalar_prefetch=1, grid=(nb,),
            in_specs=[pl.BlockSpec((None, tile, cols), lambda i, pr: (pr[0], pr[1] * nb + i, 0)),
                      pl.BlockSpec((None, tile, cols), lambda i, pr: (pr[0], i, 0)),
                      pl.BlockSpec((3, tile, cols), lambda i, pr: (0, i, 0))],
            out_specs=pl.BlockSpec((tile, cols), lambda i, pr: (i, 0))),
        compiler_params=_cparams(("parallel",)),
    )(place, slot, received, others)


N_DEVICES = 8


def _sum_devices(block, name):
    def body(v_ref, o_ref, land_ref, send_sems, recv_sems):
        x, y, c = _place()
        me = 4 * x + 2 * y + c
        copies = []
        for mask in range(1, N_DEVICES):
            peer = (x ^ (mask >> 2), y ^ ((mask >> 1) & 1), c ^ (mask & 1))
            copies.append(pltpu.make_async_remote_copy(src_ref=v_ref, dst_ref=land_ref.at[me], send_sem=send_sems.at[mask - 1],
                                                       recv_sem=recv_sems.at[mask - 1], device_id=peer, device_id_type=MESH))
        for cp in copies:
            cp.start()
        land_ref[me] = v_ref[...]
        for cp in copies:
            cp.wait_recv()
        total = land_ref[0]
        for d in range(1, N_DEVICES):
            total = total + land_ref[d]
        o_ref[...] = total
        for cp in copies:
            cp.wait_send()

    vmem = pl.BlockSpec(memory_space=pltpu.VMEM)
    return pl.pallas_call(
        body, name=name, in_specs=[vmem], out_specs=vmem, out_shape=jax.ShapeDtypeStruct(block.shape, F32),
        scratch_shapes=[pltpu.VMEM((N_DEVICES,) + block.shape, F32), pltpu.SemaphoreType.DMA((N_DEVICES - 1,)),
                        pltpu.SemaphoreType.DMA((N_DEVICES - 1,))],
    )(block)


def _vec_block(g_mix, g_mlp, g_ple, g_final, b_forget, b_gate_rows, last=None):
    pad = lambda a: jnp.concatenate([a, jnp.zeros((a.shape[0], D_MODEL - a.shape[1]), F32)], axis=1)
    last = jnp.zeros((1, 0), F32) if last is None else last
    return jnp.concatenate([g_mix, g_mlp, g_ple, g_final.reshape(1, D_MODEL), pad(b_forget), pad(b_gate_rows), pad(last)],
                           axis=0)


def _adam_math(w, g, m, v):
    m_new = ADAM_B1 * m + (1.0 - ADAM_B1) * g
    v_new = ADAM_B2 * v + (1.0 - ADAM_B2) * (g * g)
    m_hat = m_new / (1.0 - ADAM_B1 ** ADAM_STEP)
    v_hat = v_new / (1.0 - ADAM_B2 ** ADAM_STEP)
    return -ADAM_LR * (m_hat / (jnp.sqrt(v_hat) + ADAM_EPS) + ADAM_WD * w), m_new, v_new


def _adamw_halves(place, w, m, v, g_mine, g_theirs, name):
    rows2, cols = w.shape
    rows = rows2 // 2
    tile = _half_tile(rows)
    nb = rows // tile

    def body(place_ref, w_ref, m_ref, v_ref, gm_ref, gt_ref, g_ref, d_ref, nm_ref, nv_ref):
        g = jnp.where(pl.program_id(0) == 0, gm_ref[...], gt_ref[...])
        g_ref[...] = g
        d_ref[...], nm_ref[...], nv_ref[...] = _adam_math(w_ref[...], g, m_ref[...], v_ref[...])

    whole = pl.BlockSpec((tile, cols), lambda s, i, pr: ((pr[1] + s - 2 * pr[1] * s) * nb + i, 0))
    half = pl.BlockSpec((tile, cols), lambda s, i, pr: (i, 0))
    return pl.pallas_call(
        body, name=name, out_shape=[jax.ShapeDtypeStruct((rows2, cols), F32)] * 4,
        grid_spec=pltpu.PrefetchScalarGridSpec(num_scalar_prefetch=1, grid=(2, nb), in_specs=[whole] * 3 + [half] * 2,
                                               out_specs=[whole] * 4),
        compiler_params=_cparams(("parallel", "parallel")),
    )(place, w, m, v, g_mine, g_theirs)


def _adamw_vec(w, g, m, v):
    def body(w_ref, g_ref, m_ref, v_ref, d_ref, nm_ref, nv_ref):
        d_ref[...], nm_ref[...], nv_ref[...] = _adam_math(w_ref[...], g_ref[...], m_ref[...], v_ref[...])

    return pl.pallas_call(body, name="adamw_vectors", out_shape=[jax.ShapeDtypeStruct(w.shape, F32)] * 3)(w, g, m, v)


WEIGHT_NAMES = ("g_mix", "w_in", "b_forget", "b_gate", "w_branch_fox", "w_branch_sb", "w_out", "g_mlp", "w_up", "w_down",
                "g_ple", "w_ple_gate", "w_ple", "g_final")
W_IN_SHARD = D_IN // N_CHIPS
Q_END, F_END, B_END = 3 * D_ATT, 3 * D_ATT + N_HEADS, 6 * D_ATT + N_HEADS
GATE_SHARD = D_MODEL // N_CHIPS


def _join_cols(slots):
    return jnp.transpose(slots, (1, 0, 2)).reshape(slots.shape[1], N_CHIPS * slots.shape[2])


LATE = SHARDED[:1]
EARLY = SHARDED[1:]


def _first_weights(w_in_slots, b_gate):
    w_in = _join_cols(w_in_slots)
    forget = jnp.concatenate([w_in[:, Q_END:F_END], jnp.zeros((D_MODEL, F_PAD - N_HEADS), BF16)], axis=1)
    return {"qkv": jnp.concatenate([w_in[:, :Q_END], w_in[:, F_END:B_END]], axis=1), "gate": w_in[:, B_END:], "forget": forget,
            "b_gate": b_gate}


def _rest_weights(gathered):
    rows = lambda a: a.reshape(N_CHIPS * a.shape[1], a.shape[2])
    return {"branch_fox": gathered["w_branch_fox"], "branch_sb": gathered["w_branch_sb"], "out": rows(gathered["w_out"]),
            "up": gathered["w_up"], "down": rows(gathered["w_down"]), "ple_gate": rows(gathered["w_ple_gate"]),
            "ple": gathered["w_ple"]}


def _early_slots(gw):
    rows = lambda a: a.reshape(N_CHIPS, a.shape[0] // N_CHIPS, a.shape[1])
    return {"w_branch_fox": gw["branch_fox"], "w_branch_sb": gw["branch_sb"], "w_out": rows(gw["out"]), "w_up": gw["up"],
            "w_down": rows(gw["down"]), "w_ple_gate": rows(gw["ple_gate"]), "w_ple": gw["ple"]}


def _w_in_slots(gw):
    g_in = jnp.concatenate([gw["qkv"][:, :Q_END], gw["forget"][:, :N_HEADS], gw["qkv"][:, Q_END:], gw["gate"]], axis=1)
    return jnp.transpose(g_in.reshape(D_MODEL, N_CHIPS, W_IN_SHARD), (1, 0, 2))


def kernel(x, p, g_mix, w_in, b_forget, b_gate, w_branch_fox, w_branch_sb, w_out, g_mlp, w_up, w_down, g_ple, w_ple_gate, w_ple, g_final, loss_target, m_g_mix, m_w_in, m_b_forget, m_b_gate, m_w_branch_fox, m_w_branch_sb, m_w_out, m_g_mlp, m_w_up, m_w_down, m_g_ple, m_w_ple_gate, m_w_ple, m_g_final, v_g_mix, v_w_in, v_b_forget, v_b_gate, v_w_branch_fox, v_w_branch_sb, v_w_out, v_g_mlp, v_w_up, v_w_down, v_g_ple, v_w_ple_gate, v_w_ple, v_g_final):
    weights = dict(g_mix=g_mix, w_in=w_in, b_forget=b_forget, b_gate=b_gate, w_branch_fox=w_branch_fox,
                   w_branch_sb=w_branch_sb, w_out=w_out, g_mlp=g_mlp, w_up=w_up, w_down=w_down, g_ple=g_ple,
                   w_ple_gate=w_ple_gate, w_ple=w_ple, g_final=g_final)
    first = dict(g_mix=m_g_mix, w_in=m_w_in, b_forget=m_b_forget, b_gate=m_b_gate, w_branch_fox=m_w_branch_fox,
                 w_branch_sb=m_w_branch_sb, w_out=m_w_out, g_mlp=m_g_mlp, w_up=m_w_up, w_down=m_w_down, g_ple=m_g_ple,
                 w_ple_gate=m_w_ple_gate, w_ple=m_w_ple, g_final=m_g_final)
    second = dict(g_mix=v_g_mix, w_in=v_w_in, b_forget=v_b_forget, b_gate=v_b_gate, w_branch_fox=v_w_branch_fox,
                  w_branch_sb=v_w_branch_sb, w_out=v_w_out, g_mlp=v_g_mlp, w_up=v_w_up, w_down=v_w_down, g_ple=v_g_ple,
                  w_ple_gate=v_w_ple_gate, w_ple=v_w_ple, g_final=v_g_final)
    cx, cy, cc = _place()
    chip = 2 * cx + cy
    place = jnp.stack([chip, cc]).astype(jnp.int32)
    col0 = chip * GATE_SHARD

    (w_in_slots,) = _run_exchange(_gather_weights([weights[n][0].astype(BF16) for n in LATE]), "gather_w_in")
    rest = _gather_weights([weights[n][0].astype(BF16) for n in EARLY])
    gate_rows = lax.dynamic_update_slice(jnp.zeros((2, D_MODEL), F32), b_gate[0] * (cc == 0).astype(F32), (0, col0))
    zero_row = jnp.zeros((1, D_MODEL), F32)
    b_gate_whole = _sum_devices(_vec_block(zero_row, zero_row, zero_row, zero_row[0], zero_row[:, :N_HEADS], gate_rows),
                                "gather_b_gate")[5:7]
    vec = {"g_mix": g_mix, "b_forget": jnp.concatenate([b_forget, jnp.zeros((1, F_PAD - N_HEADS), F32)], axis=1),
           "g_mlp": g_mlp, "g_ple": g_ple, "g_final": g_final.reshape(1, D_MODEL)}

    loss, grad_x, reduced, gvec = _local_step(x, p[0], loss_target, _first_weights(w_in_slots, b_gate_whole), rest, vec,
                                              place)

    out = {}
    for n in SHARDED:
        g_mine, g_theirs = reduced[n]
        res = _adamw_halves(place, weights[n][0], first[n][0], second[n][0], g_mine, g_theirs, "adamw_" + n)
        out[n] = [r[None] for r in res]

    g_block = _sum_devices(_vec_block(gvec["g_mix"], gvec["g_mlp"], gvec["g_ple"], gvec["g_final"][0], gvec["b_forget"],
                                      gvec["b_gate"], loss), "reduce_vectors")
    loss = g_block[7, 0]
    g_gate = lax.dynamic_slice(g_block[5:7], (0, col0), (2, GATE_SHARD))
    blocks = [_vec_block(d["g_mix"], d["g_mlp"], d["g_ple"], d["g_final"], d["b_forget"], d["b_gate"][0])
              for d in (weights, first, second)]
    g_rows = jnp.concatenate([g_block[0:5], jnp.concatenate([g_gate, jnp.zeros((2, D_MODEL - GATE_SHARD), F32)], axis=1),
                              jnp.zeros((1, D_MODEL), F32)], axis=0)
    res = (g_rows,) + tuple(_adamw_vec(blocks[0], g_rows, blocks[1], blocks[2]))
    out["g_mix"] = [r[0:1] for r in res]
    out["g_mlp"] = [r[1:2] for r in res]
    out["g_ple"] = [r[2:3] for r in res]
    out["g_final"] = [r[3] for r in res]
    out["b_forget"] = [r[4:5, :N_HEADS] for r in res]
    out["b_gate"] = [r[5:7, :GATE_SHARD][None] for r in res]
    return (loss, grad_x, *[out[n][0] for n in WEIGHT_NAMES], *[out[n][1] for n in WEIGHT_NAMES],
            *[out[n][2] for n in WEIGHT_NAMES], *[out[n][3] for n in WEIGHT_NAMES])
```

```python
import jax
import jax.numpy as jnp
from jax import lax
from jax.experimental import pallas as pl
from jax.experimental.pallas import tpu as pltpu

F32 = jnp.float32
BF16 = jnp.bfloat16

D_MODEL = 1024
HEAD_DIM = 64
N_HEADS = 8
D_ATT = N_HEADS * HEAD_DIM
D_FF = 4 * D_MODEL
D_PLE = 256
D_IN = 6 * D_ATT + N_HEADS + 2 * D_MODEL
F_PAD = 128
EPS = 1e-6
SCALE = HEAD_DIM ** -0.5
N_CHIPS = 4
LANES = 128
ATT_BLOCK = 256
FOX_TILES = (512, 512)
SB_TILES = (512, 256)
NEG = -1e30

ADAM_LR = 0.001
ADAM_B1 = 0.9
ADAM_B2 = 0.999
ADAM_EPS = 1e-08
ADAM_WD = 0.01
ADAM_STEP = 10

VMEM_LIMIT = 56 * 1024 * 1024

MESH = pl.DeviceIdType.MESH


def _cparams(sem=None):
    return pltpu.CompilerParams(dimension_semantics=sem, vmem_limit_bytes=VMEM_LIMIT)


def _relu2(t):
    t = t.astype(F32)
    return t * t


_DIMS = {"nn": (((1,), (0,)), ((), ())), "nt": (((1,), (1,)), ((), ())), "tn": (((0,), (0,)), ((), ()))}
NT_DIMS = _DIMS["nt"]
TN_DIMS = _DIMS["tn"]


def _mm(a, b, *, mode, name, out_dtype=F32, tm=512, tn=512, tk=512, add=None, a_fn=None, epi=None, extra=None,
        col_shards=False, behind=None):
    if mode == "nn":
        (m, k), n = a.shape, b.shape[-1]
    elif mode == "nt":
        (m, k), n = a.shape, b.shape[-2]
    else:
        (k, m), n = a.shape, b.shape[1]
    shard = None
    if col_shards:
        if mode == "nn":
            shard, n = n, N_CHIPS * n
            tn = min(tn, shard)
        elif mode == "nt":
            shard = b.shape[-1]
            tk = min(tk, shard)
        else:
            shard = n // N_CHIPS
            tn = min(tn, shard)
    tm, tn, tk = min(tm, m), min(tn, n), min(tk, k)
    assert m % tm == 0 and n % tn == 0 and k % tk == 0, (name, m, n, k)
    nk = k // tk
    a_spec = {"nn": pl.BlockSpec((tm, tk), lambda i, j, kk: (i, kk)),
              "nt": pl.BlockSpec((tm, tk), lambda i, j, kk: (i, kk)),
              "tn": pl.BlockSpec((tk, tm), lambda i, j, kk: (kk, i))}[mode]
    b_spec = {"nn": pl.BlockSpec((tk, tn), lambda i, j, kk: (kk, j)),
              "nt": pl.BlockSpec((tn, tk), lambda i, j, kk: (j, kk)),
              "tn": pl.BlockSpec((tk, tn), lambda i, j, kk: (kk, j))}[mode]
    o_spec = pl.BlockSpec((tm, tn), lambda i, j, kk: (i, j))
    out_shape = (m, n)
    if col_shards and mode == "nn":
        per = shard // tn
        b_spec = pl.BlockSpec((None, tk, tn), lambda i, j, kk: (j // per, kk, j % per))
    elif col_shards and mode == "nt":
        per = shard // tk
        b_spec = pl.BlockSpec((None, tn, tk), lambda i, j, kk: (kk // per, j, kk % per))
    elif col_shards:
        assert add is None and extra is None
        per = shard // tn
        o_spec = pl.BlockSpec((None, tm, tn), lambda i, j, kk: (j // per, i, j % per))
        out_shape = (N_CHIPS, m, shard)
    operands, in_specs = [a, b], [a_spec, b_spec]
    third = add if add is not None else extra
    if third is not None:
        operands.append(third)
        in_specs.append(o_spec)

    def body(*refs):
        a_ref, b_ref = refs[0], refs[1]
        t_ref = refs[2] if third is not None else None
        o_ref = refs[3] if third is not None else refs[2]
        acc_ref = refs[-1] if nk > 1 else None
        at = a_ref[...]
        if a_fn is not None:
            at = a_fn(at)
        part = lax.dot_general(at.astype(BF16), b_ref[...].astype(BF16), _DIMS[mode], preferred_element_type=F32)

        def finish(acc):
            if epi is not None:
                acc = epi(acc, None if t_ref is None else t_ref[...])
            elif add is not None:
                acc = acc + t_ref[...].astype(F32)
            o_ref[...] = acc.astype(o_ref.dtype)

        if nk == 1:
            finish(part)
        else:
            kk = pl.program_id(2)

            @pl.when(kk == 0)
            def _():
                acc_ref[...] = part

            @pl.when(kk > 0)
            def _():
                acc_ref[...] += part

            @pl.when(kk == nk - 1)
            def _():
                finish(acc_ref[...])

    call = dict(name=name, grid=(m // tm, n // tn, nk), in_specs=in_specs,
                scratch_shapes=[pltpu.VMEM((tm, tn), F32)] if nk > 1 else [])
    if behind is not None:
        (res,), exchanged = _call_behind(body, behind, out_specs=[o_spec], out_shape=[jax.ShapeDtypeStruct(out_shape, out_dtype)],
                                         operands=operands, **call)
        return res, exchanged
    return pl.pallas_call(body, out_specs=o_spec, out_shape=jax.ShapeDtypeStruct(out_shape, out_dtype),
                          compiler_params=_cparams(("parallel", "parallel", "arbitrary")), **call)(*operands)


ROW_TILE = 512


def _row_spec(width=D_MODEL, rows=ROW_TILE):
    return pl.BlockSpec((rows, width), lambda i: (i, 0))


def _vec_spec(rows=1, width=D_MODEL):
    return pl.BlockSpec((rows, width), lambda i: (0, 0))


def _xhat(x):
    r = lax.rsqrt(jnp.mean(x * x, axis=-1, keepdims=True) + EPS)
    return x * r, r


def _rms_bwd_rows(dh, x, g):
    xh, r = _xhat(x)
    dxh = dh * g
    dx = r * (dxh - xh * jnp.mean(dxh * xh, axis=-1, keepdims=True))
    return dx, jnp.sum(dh * xh, axis=0, keepdims=True)


def _norm_fwd(x, g, name):
    t = x.shape[0]

    def body(x_ref, g_ref, h_ref):
        xh, _ = _xhat(x_ref[...])
        h_ref[...] = (xh * g_ref[...]).astype(BF16)

    return pl.pallas_call(
        body, name=name, grid=(t // ROW_TILE,), in_specs=[_row_spec(), _vec_spec()], out_specs=_row_spec(),
        out_shape=jax.ShapeDtypeStruct((t, D_MODEL), BF16), compiler_params=_cparams(("parallel",)),
    )(x, g)


def _norm_bwd(x, g, dh, dres, name):
    t = x.shape[0]

    def body(x_ref, g_ref, dh_ref, dres_ref, dx_ref, dxb_ref, dg_ref):
        dx, dg = _rms_bwd_rows(dh_ref[...], x_ref[...], g_ref[...])
        dx = dx + dres_ref[...]
        dx_ref[...] = dx
        dxb_ref[...] = dx.astype(BF16)

        @pl.when(pl.program_id(0) == 0)
        def _():
            dg_ref[...] = jnp.zeros_like(dg_ref)

        dg_ref[...] += dg

    return pl.pallas_call(
        body, name=name, grid=(t // ROW_TILE,),
        in_specs=[_row_spec(), _vec_spec(), _row_spec(), _row_spec()],
        out_specs=[_row_spec(), _row_spec(), _vec_spec()],
        out_shape=[jax.ShapeDtypeStruct((t, D_MODEL), F32), jax.ShapeDtypeStruct((t, D_MODEL), BF16),
                   jax.ShapeDtypeStruct((1, D_MODEL), F32)],
        compiler_params=_cparams(("arbitrary",)),
    )(x, g, dh, dres)


def _gate_fwd(gl, b_gate, of, os_):
    t = of.shape[0]

    def body(gla_ref, glb_ref, b_ref, of_ref, os_ref, m_ref):
        ga = jax.nn.sigmoid(gla_ref[...] + b_ref[0:1, :])
        gb = jax.nn.sigmoid(glb_ref[...] + b_ref[1:2, :])
        m_ref[...] = (ga * of_ref[...] + gb * os_ref[...]).astype(BF16)

    return pl.pallas_call(
        body, name="gate_fwd", grid=(t // ROW_TILE,),
        in_specs=[pl.BlockSpec((ROW_TILE, D_MODEL), lambda i: (i, 0)), pl.BlockSpec((ROW_TILE, D_MODEL), lambda i: (i, 1)),
                  _vec_spec(2), _row_spec(), _row_spec()],
        out_specs=_row_spec(), out_shape=jax.ShapeDtypeStruct((t, D_MODEL), BF16),
        compiler_params=_cparams(("parallel",)),
    )(gl, gl, b_gate, of, os_)


def _gate_bwd(gl, b_gate, of, os_, dmerged):
    t = of.shape[0]

    def body(gla_ref, glb_ref, b_ref, of_ref, os_ref, dm_ref, dof_ref, dos_ref, dgla_ref, dglb_ref, db_ref):
        dm = dm_ref[...]
        ga = jax.nn.sigmoid(gla_ref[...] + b_ref[0:1, :])
        gb = jax.nn.sigmoid(glb_ref[...] + b_ref[1:2, :])
        dof_ref[...] = (dm * ga).astype(BF16)
        dos_ref[...] = (dm * gb).astype(BF16)
        dgla = dm * of_ref[...] * ga * (1.0 - ga)
        dglb = dm * os_ref[...] * gb * (1.0 - gb)
        dgla_ref[...] = dgla.astype(BF16)
        dglb_ref[...] = dglb.astype(BF16)

        @pl.when(pl.program_id(0) == 0)
        def _():
            db_ref[...] = jnp.zeros_like(db_ref)

        db_ref[0:1, :] += jnp.sum(dgla, axis=0, keepdims=True)
        db_ref[1:2, :] += jnp.sum(dglb, axis=0, keepdims=True)

    outs = pl.pallas_call(
        body, name="gate_bwd", grid=(t // ROW_TILE,),
        in_specs=[pl.BlockSpec((ROW_TILE, D_MODEL), lambda i: (i, 0)), pl.BlockSpec((ROW_TILE, D_MODEL), lambda i: (i, 1)),
                  _vec_spec(2), _row_spec(), _row_spec(), _row_spec()],
        out_specs=[_row_spec(), _row_spec(), _row_spec(), _row_spec(), _vec_spec(2)],
        out_shape=[jax.ShapeDtypeStruct((t, D_MODEL), BF16)] * 4 + [jax.ShapeDtypeStruct((2, D_MODEL), F32)],
        compiler_params=_cparams(("arbitrary",)),
    )(gl, gl, b_gate, of, os_, dmerged)
    return outs


def _head_and_loss(x2, gpre, pe, g_final, target):
    t = x2.shape[0]

    def body(x2_ref, gpre_ref, pe_ref, g_ref, tgt_ref, dx3_ref, dpre_ref, dpe_ref, dg_ref, loss_ref):
        gp = jax.nn.sigmoid(gpre_ref[...])
        pe_t = pe_ref[...]
        x3 = x2_ref[...] + gp * pe_t
        g = g_ref[...]
        xh, _ = _xhat(x3)
        err = xh * g - tgt_ref[...]
        dy = err * (1.0 / D_MODEL)
        dx3, dg = _rms_bwd_rows(dy, x3, g)
        dx3_ref[...] = dx3
        dpre_ref[...] = (dx3 * pe_t * gp * (1.0 - gp)).astype(BF16)
        dpe_ref[...] = (dx3 * gp).astype(BF16)

        @pl.when(pl.program_id(0) == 0)
        def _():
            dg_ref[...] = jnp.zeros_like(dg_ref)
            loss_ref[...] = jnp.zeros_like(loss_ref)

        dg_ref[...] += dg
        loss_ref[...] += 0.5 * jnp.sum(jnp.mean(err * err, axis=-1, keepdims=True), axis=0, keepdims=True)

    return pl.pallas_call(
        body, name="head_and_loss", grid=(t // ROW_TILE,),
        in_specs=[_row_spec(), _row_spec(), _row_spec(), _vec_spec(), _row_spec()],
        out_specs=[_row_spec(), _row_spec(), _row_spec(), _vec_spec(), _vec_spec(1, LANES)],
        out_shape=[jax.ShapeDtypeStruct((t, D_MODEL), F32), jax.ShapeDtypeStruct((t, D_MODEL), BF16),
                   jax.ShapeDtypeStruct((t, D_MODEL), BF16), jax.ShapeDtypeStruct((1, D_MODEL), F32),
                   jax.ShapeDtypeStruct((1, LANES), F32)],
        compiler_params=_cparams(("arbitrary",)),
    )(x2, gpre, pe, g_final, target)


def _split3(v):
    hi = v.astype(BF16)
    r1 = v - hi.astype(F32)
    mid = r1.astype(BF16)
    lo = (r1 - mid.astype(F32)).astype(BF16)
    return hi, mid, lo


def _split2(v):
    hi = v.astype(BF16)
    return jnp.concatenate([hi, (v - hi.astype(F32)).astype(BF16)], axis=1)


def _dot(a, b, dims=_DIMS["nn"]):
    return lax.dot_general(a, b, dims, preferred_element_type=F32)


def _tri(n, rel):
    row = lax.broadcasted_iota(jnp.int32, (n, n), 0)
    col = lax.broadcasted_iota(jnp.int32, (n, n), 1)
    return rel(row, col).astype(BF16)


def _tri2(n, rel):
    t = _tri(n, rel)
    return jnp.concatenate([t, t], axis=0)


def _log_sigmoid(v):
    return -(jnp.maximum(-v, 0.0) + jnp.log(1.0 + jnp.exp(-jnp.abs(v))))


def _fox_prep(fl, b_forget, batch, seq):
    nb = seq // ATT_BLOCK

    def body(fl_ref, b_ref, cw_ref, cr_ref):
        col = lax.broadcasted_iota(jnp.int32, (ATT_BLOCK, F_PAD), 1)
        lower = _tri(ATT_BLOCK, lambda r, c: c <= r)
        upper = _tri(ATT_BLOCK, lambda r, c: r <= c)
        expand = (lax.broadcasted_iota(jnp.int32, (F_PAD, D_ATT), 1) // HEAD_DIM
                  == lax.broadcasted_iota(jnp.int32, (F_PAD, D_ATT), 0)).astype(BF16)
        carry_w = jnp.zeros((1, D_ATT), F32)
        carry_r = jnp.zeros((F_PAD, 1), F32)
        for i in range(nb):
            blk = slice(i * ATT_BLOCK, (i + 1) * ATT_BLOCK)
            logf = jnp.where(col < N_HEADS, _log_sigmoid(fl_ref[blk, :] + b_ref[...]), 0.0)
            cw = jnp.zeros((ATT_BLOCK, D_ATT), F32) + carry_w
            cr = jnp.zeros((F_PAD, ATT_BLOCK), F32) + carry_r
            for part in _split3(logf):
                cw += _dot(lower, _dot(part, expand).astype(BF16))
                cr += _dot(part, upper, TN_DIMS)
            cw_ref[blk, :] = cw
            cr_ref[:, blk] = cr[0:N_HEADS, :]
            carry_w = cw[ATT_BLOCK - 1:ATT_BLOCK, :]
            carry_r = cr[:, ATT_BLOCK - 1:ATT_BLOCK]

    return pl.pallas_call(
        body, name="fox_prep", grid=(batch,),
        in_specs=[pl.BlockSpec((seq, F_PAD), lambda b: (b, 0)), pl.BlockSpec((1, F_PAD), lambda b: (0, 0))],
        out_specs=[pl.BlockSpec((seq, D_ATT), lambda b: (b, 0)), pl.BlockSpec((N_HEADS, seq), lambda b: (b, 0))],
        out_shape=[jax.ShapeDtypeStruct((batch * seq, D_ATT), F32), jax.ShapeDtypeStruct((batch * N_HEADS, seq), F32)],
        compiler_params=_cparams(("parallel",)),
    )(fl, b_forget)


def _fox_post(dcs_wide, drs_wide, fl, b_forget, batch, seq):
    nb = seq // ATT_BLOCK

    def body(dcs_ref, drs_ref, fl_ref, b_ref, dfl_ref, db_ref):
        pick = (lax.broadcasted_iota(jnp.int32, (D_ATT, F_PAD), 0)
                == lax.broadcasted_iota(jnp.int32, (D_ATT, F_PAD), 1) * HEAD_DIM).astype(BF16)
        upper = _tri(ATT_BLOCK, lambda r, c: r <= c)
        col = lax.broadcasted_iota(jnp.int32, (ATT_BLOCK, F_PAD), 1)

        @pl.when(pl.program_id(0) == 0)
        def _():
            db_ref[...] = jnp.zeros_like(db_ref)

        carry = jnp.zeros((1, F_PAD), F32)
        for i in reversed(range(nb)):
            blk = slice(i * ATT_BLOCK, (i + 1) * ATT_BLOCK)
            narrow = jnp.zeros((ATT_BLOCK, F_PAD), F32)
            for part in _split3(drs_ref[blk, :] - dcs_ref[blk, :]):
                narrow += _dot(part, pick)
            after = jnp.zeros((ATT_BLOCK, F_PAD), F32) + carry
            for part in _split3(narrow):
                after += _dot(upper, part)
            carry = after[0:1, :]
            pre = fl_ref[blk, :] + b_ref[...]
            dfl = jnp.where(col < N_HEADS, after * jax.nn.sigmoid(-pre), 0.0)
            dfl_ref[blk, :] = dfl.astype(BF16)
            db_ref[...] += jnp.sum(dfl, axis=0, keepdims=True)

    return pl.pallas_call(
        body, name="fox_post", grid=(batch,),
        in_specs=[pl.BlockSpec((seq, D_ATT), lambda b: (b, 0)), pl.BlockSpec((seq, D_ATT), lambda b: (b, 0)),
                  pl.BlockSpec((seq, F_PAD), lambda b: (b, 0)), pl.BlockSpec((1, F_PAD), lambda b: (0, 0))],
        out_specs=[pl.BlockSpec((seq, F_PAD), lambda b: (b, 0)), pl.BlockSpec((1, F_PAD), lambda b: (0, 0))],
        out_shape=[jax.ShapeDtypeStruct((batch * seq, F_PAD), BF16), jax.ShapeDtypeStruct((1, F_PAD), F32)],
        compiler_params=_cparams(("arbitrary",)),
    )(dcs_wide, drs_wide, fl, b_forget)


N_PAIRS = N_HEADS // 2


def _att_specs(seq, col0, tq):
    nq = seq // tq
    q = pl.BlockSpec((tq, LANES), lambda b, hp, qi: (b * nq + qi, col0 + hp))
    k = pl.BlockSpec((seq, LANES), lambda b, hp, qi: (b, col0 + N_PAIRS + hp))
    v = pl.BlockSpec((seq, LANES), lambda b, hp, qi: (b, col0 + 2 * N_PAIRS + hp))
    return q, k, v


def _qblock_spec(seq, tq):
    nq = seq // tq
    return pl.BlockSpec((tq, LANES), lambda b, hp, qi: (b * nq + qi, hp))


def _kv_out_spec(seq):
    return pl.BlockSpec((seq, LANES), lambda b, hp, qi: (b, hp))


def _head_masks():
    lane = lax.broadcasted_iota(jnp.int32, (1, LANES), 1)
    return [(lane >= HEAD_DIM * j) & (lane < HEAD_DIM * (j + 1)) for j in range(2)]


def _stack_heads(t, masks):
    zero = jnp.zeros_like(t)
    return jnp.concatenate([jnp.where(masks[0], t, zero), jnp.where(masks[1], t, zero)], axis=0)


def _stack_cols(t):
    return jnp.concatenate([t[:, 0:1], t[:, HEAD_DIM:HEAD_DIM + 1]], axis=0)


def _unstack(t2, masks):
    tq = t2.shape[0] // 2
    return jnp.where(masks[0], t2[:tq], t2[tq:])


def _stacked_ids(tq, tk):
    row = lax.broadcasted_iota(jnp.int32, (2 * tq, tk), 0)
    col = lax.broadcasted_iota(jnp.int32, (2 * tq, tk), 1)
    first = lax.broadcasted_iota(jnp.int32, (2 * tq, 1), 0) < tq
    return col - jnp.where(row < tq, row, row - tq), first


def _sweep(qi, tq, tk, step, init, leftward):
    per = tq // tk
    whole = lambda carry: lax.fori_loop(0, per * qi, lambda i, c: step(per * qi - 1 - i if leftward else i, c, None), carry)
    crossed = [(per * qi + j, -j * tk) for j in range(per)]
    if leftward:
        carry = init
        for kb, lead in reversed(crossed):
            carry = step(kb, carry, lead)
        return whole(carry)
    carry = whole(init)
    for kb, lead in crossed:
        carry = step(kb, carry, lead)
    return carry


def _fox_fwd(qkv, c_wide, c_row, batch, seq):
    tq, tk = FOX_TILES
    nq = seq // tq

    def body(q_ref, k_ref, v_ref, cw_ref, cr_ref, o_ref, lse_ref):
        hp, qi = pl.program_id(1), pl.program_id(2)
        masks = _head_masks()
        ahead, first = _stacked_ids(tq, tk)
        q2 = _stack_heads(q_ref[...], masks) * SCALE
        ct = _stack_cols(cw_ref[...])

        def step(kb, carry, lead):
            m, l, acc = carry
            k0 = pl.multiple_of(kb * tk, tk)
            cs = jnp.where(first, cr_ref[pl.ds(2 * hp, 1), pl.ds(k0, tk)], cr_ref[pl.ds(2 * hp + 1, 1), pl.ds(k0, tk)])
            s = _dot(q2, k_ref[pl.ds(k0, tk), :], NT_DIMS) + ct - cs
            if lead is not None:
                s = jnp.where(ahead <= lead, s, NEG)
            m_new = jnp.maximum(m, jnp.max(s, axis=1, keepdims=True))
            p = jnp.exp(s - m_new)
            alpha = jnp.exp(m - m_new)
            l = alpha * l + jnp.sum(p, axis=1, keepdims=True)
            acc = alpha * acc + _dot(p.astype(BF16), v_ref[pl.ds(k0, tk), :])
            return m_new, l, acc

        init = (jnp.full((2 * tq, 1), NEG, F32), jnp.zeros((2 * tq, 1), F32), jnp.zeros((2 * tq, LANES), F32))
        m, l, acc = _sweep(qi, tq, tk, step, init, leftward=False)
        o_ref[...] = _unstack(acc / l, masks).astype(BF16)
        lse_ref[...] = _unstack(m + jnp.log(l), masks)

    q_spec, k_spec, v_spec = _att_specs(seq, 0, tq)
    qb = _qblock_spec(seq, tq)
    return pl.pallas_call(
        body, name="fox_fwd", grid=(batch, N_PAIRS, nq),
        in_specs=[q_spec, k_spec, v_spec, qb, pl.BlockSpec((N_HEADS, seq), lambda b, hp, qi: (b, 0))],
        out_specs=[qb, qb],
        out_shape=[jax.ShapeDtypeStruct((batch * seq, D_ATT), BF16), jax.ShapeDtypeStruct((batch * seq, D_ATT), F32)],
        compiler_params=_cparams(("parallel", "parallel", "arbitrary")),
    )(qkv, qkv, qkv, c_wide, c_row)


def _fox_bwd(qkv, c_wide, c_row, o, do, lse_wide, batch, seq, behind):
    tq, tk = FOX_TILES
    nq = seq // tq

    def body(q_ref, k_ref, v_ref, cw_ref, cr_ref, o_ref, do_ref, lse_ref,
             dq_ref, dk_ref, dv_ref, dcs_ref, drs_ref, dkc_acc, dv_acc):
        hp, qi = pl.program_id(1), pl.program_id(2)

        @pl.when(qi == 0)
        def _():
            dkc_acc[...] = jnp.zeros_like(dkc_acc)
            dv_acc[...] = jnp.zeros_like(dv_acc)

        masks = _head_masks()
        ahead, first = _stacked_ids(tq, tk)
        q_t, do_t = q_ref[...], do_ref[...]
        q2 = _stack_heads(q_t, masks) * SCALE
        do2 = _stack_heads(do_t, masks)
        q_and_ones = jnp.concatenate([q2, _stack_heads(jnp.ones_like(q_t), masks)], axis=1)
        ct = _stack_cols(cw_ref[...])
        lse = _stack_cols(lse_ref[...])
        prod = do_t.astype(F32) * o_ref[...].astype(F32)
        delta = jnp.concatenate([jnp.sum(jnp.where(mk, prod, 0.0), axis=1, keepdims=True) for mk in masks], axis=0)

        def step(kb, carry, lead):
            dq_acc, rs = carry
            k0 = pl.multiple_of(kb * tk, tk)
            kblk = k_ref[pl.ds(k0, tk), :]
            cs = jnp.where(first, cr_ref[pl.ds(2 * hp, 1), pl.ds(k0, tk)], cr_ref[pl.ds(2 * hp + 1, 1), pl.ds(k0, tk)])
            p = jnp.exp(_dot(q2, kblk, NT_DIMS) + ct - cs - lse)
            if lead is not None:
                p = jnp.where(ahead <= lead, p, 0.0)
            dp = _dot(do2, v_ref[pl.ds(k0, tk), :], NT_DIMS)
            ds = (p * (dp - delta)).astype(BF16)
            dkc_acc[pl.ds(k0, tk), :] += _dot(ds, q_and_ones, TN_DIMS)
            dv_acc[pl.ds(k0, tk), :] += _dot(p.astype(BF16), do2, TN_DIMS)
            return dq_acc + _dot(ds, kblk), rs + jnp.sum(ds.astype(F32), axis=1, keepdims=True)

        init = (jnp.zeros((2 * tq, LANES), F32), jnp.zeros((2 * tq, 1), F32))
        dq_acc, rs = _sweep(qi, tq, tk, step, init, leftward=False)
        dq_ref[...] = (_unstack(dq_acc, masks) * SCALE).astype(BF16)
        drs_ref[...] = _unstack(rs, masks)

        @pl.when(qi == nq - 1)
        def _():
            dk_ref[...] = dkc_acc[:, 0:LANES].astype(BF16)
            dcs_ref[...] = dkc_acc[:, LANES:2 * LANES]
            dv_ref[...] = dv_acc[...].astype(BF16)

    q_spec, k_spec, v_spec = _att_specs(seq, 0, tq)
    qb = _qblock_spec(seq, tq)
    return _call_behind(
        body, behind, name="fox_bwd", grid=(batch, N_PAIRS, nq),
        in_specs=[q_spec, k_spec, v_spec, qb, pl.BlockSpec((N_HEADS, seq), lambda b, hp, qi: (b, 0)), qb, qb, qb],
        out_specs=[qb, _kv_out_spec(seq), _kv_out_spec(seq), _kv_out_spec(seq), qb],
        out_shape=[jax.ShapeDtypeStruct((batch * seq, D_ATT), BF16)] * 3 + [jax.ShapeDtypeStruct((batch * seq, D_ATT), F32)] * 2,
        scratch_shapes=[pltpu.VMEM((seq, 2 * LANES), F32), pltpu.VMEM((seq, LANES), F32)],
        operands=(qkv, qkv, qkv, c_wide, c_row, o, do, lse_wide))


def _sb_logits(q2, kblk):
    z = _dot(q2, kblk, NT_DIMS)
    lsn = jnp.minimum(-z, 0.0) - jnp.log(1.0 + jnp.exp(-jnp.abs(z)))
    return lsn + z, lsn


def _sb_fwd(qkv, batch, seq, behind):
    tq, tk = SB_TILES
    nq = seq // tq

    def body(q_ref, k_ref, v_ref, o_ref, rt_ref):
        qi = pl.program_id(2)
        masks = _head_masks()
        ahead, _ = _stacked_ids(tq, tk)
        later = _tri2(tk, lambda r, c: r > c)
        q2 = _stack_heads(q_ref[...], masks) * SCALE

        def step(kb, carry, lead):
            run, acc = carry
            k0 = pl.multiple_of(kb * tk, tk)
            ls, lsn = _sb_logits(q2, k_ref[pl.ds(k0, tk), :])
            if lead is not None:
                lsn = jnp.where(ahead < lead, lsn, 0.0)
            w = jnp.exp(ls + _dot(_split2(lsn), later) + run)
            if lead is not None:
                w = jnp.where(ahead < lead, w, 0.0)
            return run + jnp.sum(lsn, axis=1, keepdims=True), acc + _dot(w.astype(BF16), v_ref[pl.ds(k0, tk), :])

        init = (jnp.zeros((2 * tq, 1), F32), jnp.zeros((2 * tq, LANES), F32))
        run, acc = _sweep(qi, tq, tk, step, init, leftward=True)
        o_ref[...] = _unstack(acc, masks).astype(BF16)
        rt_ref[...] = _unstack(run, masks)

    q_spec, k_spec, v_spec = _att_specs(seq, 3 * N_PAIRS, tq)
    qb = _qblock_spec(seq, tq)
    return _call_behind(
        body, behind, name="sb_fwd", grid=(batch, N_PAIRS, nq), in_specs=[q_spec, k_spec, v_spec], out_specs=[qb, qb],
        out_shape=[jax.ShapeDtypeStruct((batch * seq, D_ATT), BF16), jax.ShapeDtypeStruct((batch * seq, D_ATT), F32)],
        scratch_shapes=[], operands=(qkv, qkv, qkv))


def _sb_bwd(qkv, do, rt_wide, batch, seq, behind):
    tq, tk = SB_TILES
    nq = seq // tq

    def body(q_ref, k_ref, v_ref, do_ref, rt_ref, dq_ref, dk_ref, dv_ref, dk_acc, dv_acc):
        qi = pl.program_id(2)

        @pl.when(qi == 0)
        def _():
            dk_acc[...] = jnp.zeros_like(dk_acc)
            dv_acc[...] = jnp.zeros_like(dv_acc)

        masks = _head_masks()
        ahead, _ = _stacked_ids(tq, tk)
        later = _tri2(tk, lambda r, c: r > c)
        earlier = _tri2(tk, lambda r, c: r < c)
        q2 = _stack_heads(q_ref[...], masks) * SCALE
        do2 = _stack_heads(do_ref[...], masks)
        total = _stack_cols(rt_ref[...])

        def step(kb, carry, lead):
            pref, epre, dq_acc = carry
            k0 = pl.multiple_of(kb * tk, tk)
            kblk = k_ref[pl.ds(k0, tk), :]
            ls, lsn_all = _sb_logits(q2, kblk)
            lsn = lsn_all if lead is None else jnp.where(ahead < lead, lsn_all, 0.0)
            rs = jnp.sum(lsn, axis=1, keepdims=True)
            w = jnp.exp(ls + _dot(_split2(lsn), later) + (total - pref - rs))
            if lead is not None:
                w = jnp.where(ahead < lead, w, 0.0)
            e = w * _dot(do2, v_ref[pl.ds(k0, tk), :], NT_DIMS)
            before = _dot(_split2(e), earlier) + epre
            dz = e * jnp.exp(lsn_all) - jnp.exp(ls) * before
            if lead is not None:
                dz = jnp.where(ahead < lead, dz, 0.0)
            dz = dz.astype(BF16)
            dk_acc[pl.ds(k0, tk), :] += _dot(dz, q2, TN_DIMS)
            dv_acc[pl.ds(k0, tk), :] += _dot(w.astype(BF16), do2, TN_DIMS)
            return pref + rs, epre + jnp.sum(e, axis=1, keepdims=True), dq_acc + _dot(dz, kblk)

        init = (jnp.zeros((2 * tq, 1), F32), jnp.zeros((2 * tq, 1), F32), jnp.zeros((2 * tq, LANES), F32))
        dq_acc = _sweep(qi, tq, tk, step, init, leftward=False)[2]
        dq_ref[...] = (_unstack(dq_acc, masks) * SCALE).astype(BF16)

        @pl.when(qi == nq - 1)
        def _():
            dk_ref[...] = dk_acc[...].astype(BF16)
            dv_ref[...] = dv_acc[...].astype(BF16)

    q_spec, k_spec, v_spec = _att_specs(seq, 3 * N_PAIRS, tq)
    qb = _qblock_spec(seq, tq)
    return _call_behind(
        body, behind, name="sb_bwd", grid=(batch, N_PAIRS, nq), in_specs=[q_spec, k_spec, v_spec, qb, qb],
        out_specs=[qb, _kv_out_spec(seq), _kv_out_spec(seq)], out_shape=[jax.ShapeDtypeStruct((batch * seq, D_ATT), BF16)] * 3,
        scratch_shapes=[pltpu.VMEM((seq, LANES), F32), pltpu.VMEM((seq, LANES), F32)], operands=(qkv, qkv, qkv, do, rt_wide))


def _local_step(x, p, target, w, rest, vec, place):
    batch, seq, _ = x.shape
    t = batch * seq
    x = x.reshape(t, D_MODEL)
    target = target.reshape(t, D_MODEL)
    p = p.reshape(t, D_PLE)
    big = dict(tm=1024, tn=1024, tk=1024)

    h1 = _norm_fwd(x, vec["g_mix"], "norm_mix")
    qkv = _mm(h1, w["qkv"], mode="nn", name="proj_qkv", out_dtype=BF16, **big)
    gl = _mm(h1, w["gate"], mode="nn", name="proj_gate", **big)
    fl = _mm(h1, w["forget"], mode="nn", name="proj_forget", **big)
    c_wide, c_row = _fox_prep(fl, vec["b_forget"], batch, seq)
    o_fox, lse_wide = _fox_fwd(qkv, c_wide, c_row, batch, seq)
    (o_sb, rt_wide), gathered = _sb_fwd(qkv, batch, seq, rest)
    w = dict(w, **_rest_weights(dict(zip(EARLY, gathered))))
    of = _mm(o_fox, w["branch_fox"], mode="nn", name="branch_fox", col_shards=True, **big)
    os_ = _mm(o_sb, w["branch_sb"], mode="nn", name="branch_sb", col_shards=True, **big)
    merged = _gate_fwd(gl, w["b_gate"], of, os_)
    x1 = _mm(merged, w["out"], mode="nn", name="proj_out", add=x, **big)
    h2 = _norm_fwd(x1, vec["g_mlp"], "norm_mlp")
    ar = _mm(h2, w["up"], mode="nn", name="mlp_up", out_dtype=BF16, epi=lambda acc, _: jnp.maximum(acc, 0.0),
             col_shards=True, **big)
    x2 = _mm(ar, w["down"], mode="nn", name="mlp_down", a_fn=_relu2, add=x1, **big)
    h3 = _norm_fwd(x2, vec["g_ple"], "norm_ple")
    gpre = _mm(h3, w["ple_gate"], mode="nn", name="ple_gate", **big)
    pe = _mm(p, w["ple"], mode="nn", name="ple_embed", col_shards=True, **big)

    dx3, dpre, dpe, dg_final, loss = _head_and_loss(x2, gpre, pe, vec["g_final"], target)
    gw = {}
    gw["ple"] = _mm(p, dpe, mode="tn", name="d_w_ple", col_shards=True, **big)
    gw["ple_gate"] = _mm(h3, dpre, mode="tn", name="d_w_ple_gate", **big)
    dh3 = _mm(dpre, w["ple_gate"], mode="nt", name="d_h_ple", **big)
    dx2, dx2b, dg_ple = _norm_bwd(x2, vec["g_ple"], dh3, dx3, "norm_ple_bwd")
    gw["down"] = _mm(ar, dx2b, mode="tn", name="d_w_down", a_fn=_relu2, **big)
    da = _mm(dx2b, w["down"], mode="nt", name="d_act", out_dtype=BF16,
             epi=lambda acc, r: acc * (2.0 * r.astype(F32)), extra=ar, **big)
    gw["up"] = _mm(h2, da, mode="tn", name="d_w_up", col_shards=True, **big)
    dh2 = _mm(da, w["up"], mode="nt", name="d_h_mlp", col_shards=True, **big)
    dx1, dx1b, dg_mlp = _norm_bwd(x1, vec["g_mlp"], dh2, dx2, "norm_mlp_bwd")
    gw["out"] = _mm(merged, dx1b, mode="tn", name="d_w_out", **big)
    dmerged = _mm(dx1b, w["out"], mode="nt", name="d_merged", **big)
    dof, dos, dgla, dglb, gw["b_gate"] = _gate_bwd(gl, w["b_gate"], of, os_, dmerged)
    gw["branch_fox"] = _mm(o_fox, dof, mode="tn", name="d_w_branch_fox", col_shards=True, **big)
    gw["branch_sb"] = _mm(o_sb, dos, mode="tn", name="d_w_branch_sb", col_shards=True, **big)
    do_fox = _mm(dof, w["branch_fox"], mode="nt", name="d_o_fox", out_dtype=BF16, col_shards=True, **big)
    do_sb = _mm(dos, w["branch_sb"], mode="nt", name="d_o_sb", out_dtype=BF16, col_shards=True, **big)
    early = _early_slots(gw)
    early = [early[n] for n in EARLY]
    (dq_a, dk_a, dv_a, dcs_wide, drs_wide), received = _fox_bwd(qkv, c_wide, c_row, o_fox, do_fox, lse_wide, batch, seq,
                                                                _swap_halves(early))
    sums = [_sum_sibling(place, s, r, "sum_sibling_" + n) for s, r, n in zip(early, received, EARLY)]
    (dq_b, dk_b, dv_b), others = _sb_bwd(qkv, do_sb, rt_wide, batch, seq, _exchange_chips(sums))
    mine = [_sum_chips(place, s, r, o, "sum_chips_" + n) for s, r, o, n in zip(early, received, others, EARLY)]
    dfl, db_forget = _fox_post(dcs_wide, drs_wide, fl, vec["b_forget"], batch, seq)
    dqkv = jnp.concatenate([dq_a, dk_a, dv_a, dq_b, dk_b, dv_b], axis=1)
    dgl = jnp.concatenate([dgla, dglb], axis=1)
    gw["qkv"], theirs = _mm(dqkv, h1, mode="tn", name="d_w_qkv", behind=_share_halves(mine), **big)
    reduced = dict(zip(EARLY, zip(mine, theirs)))
    gw["gate"] = _mm(dgl, h1, mode="tn", name="d_w_gate", **big)
    gw["forget"] = _mm(dfl, h1, mode="tn", name="d_w_forget", **big)
    late = [_w_in_slots(gw)]
    dh1, received = _mm(dqkv, w["qkv"], mode="nt", name="d_h_qkv", behind=_swap_halves(late), **big)
    sums = [_sum_sibling(place, late[0], received[0], "sum_sibling_w_in")]
    dh1, others = _mm(dgl, w["gate"], mode="nt", name="d_h_gate", add=dh1, behind=_exchange_chips(sums), **big)
    dh1 = _mm(dfl, w["forget"], mode="nt", name="d_h_forget", add=dh1, **big)
    grad_x, _, dg_mix = _norm_bwd(x, vec["g_mix"], dh1, dx1, "norm_mix_bwd")
    mine = [_sum_chips(place, late[0], received[0], others[0], "sum_chips_w_in")]
    reduced["w_in"] = (mine[0], _run_exchange(_share_halves(mine), "reduce_share_w_in")[0])
    gvec = {"g_mix": dg_mix, "b_forget": db_forget[:, 0:N_HEADS], "g_mlp": dg_mlp, "g_ple": dg_ple,
            "g_final": dg_final, "b_gate": gw["b_gate"]}
    return loss, grad_x.reshape(batch, seq, D_MODEL), reduced, gvec


ANY = pl.BlockSpec(memory_space=pl.ANY)
SHARDED = ("w_in", "w_branch_fox", "w_branch_sb", "w_out", "w_up", "w_down", "w_ple_gate", "w_ple")
ROW_ALIGN = 16
F32_ROWS = 8


def _place():
    return lax.axis_index("x"), lax.axis_index("y"), lax.axis_index("c")


def _other_chips(x, y):
    return [(1 - x, y), (x, 1 - y), (1 - x, 1 - y)]


def _half(ref, h):
    r = ref.shape[0] // 2
    assert r % ROW_ALIGN == 0
    return ref.at[pl.ds(pl.multiple_of(h * r, ROW_ALIGN), r)]


def _remote(src, dst, sems, idx, to):
    send_sems, recv_sems = sems
    return pltpu.make_async_remote_copy(src_ref=src, dst_ref=dst, send_sem=send_sems.at[idx], recv_sem=recv_sems.at[idx],
                                        device_id=to, device_id_type=MESH)


class _Exchange:
    def __init__(self, operands, out_shapes, sem_shape, start, finish):
        self.operands, self.out_shapes, self.sem_shape, self.start, self.finish = operands, out_shapes, sem_shape, start, finish

    def scratch(self):
        return [pltpu.SemaphoreType.DMA(self.sem_shape), pltpu.SemaphoreType.DMA(self.sem_shape)]


def _run_exchange(ex, name):
    n = len(ex.operands)

    def body(*refs):
        ex.start(refs[:n], refs[n:2 * n], refs[2 * n:])
        ex.finish(refs[:n], refs[n:2 * n], refs[2 * n:])

    return pl.pallas_call(body, name=name, in_specs=[ANY] * n, out_specs=[ANY] * n, out_shape=ex.out_shapes,
                          scratch_shapes=ex.scratch())(*ex.operands)


def _call_behind(body, ex, *, name, grid, in_specs, out_specs, out_shape, scratch_shapes, operands):
    n_in, n_out, nx = len(in_specs), len(out_specs), len(ex.operands)

    def wrapped(*refs):
        ins, x_in = refs[:n_in], refs[n_in:n_in + nx]
        outs, x_out = refs[n_in + nx:n_in + nx + n_out], refs[n_in + nx + n_out:n_in + 2 * nx + n_out]
        scratch, sems = refs[n_in + 2 * nx + n_out:-2], refs[-2:]
        first, last = None, None
        for d, steps in enumerate(grid):
            at_start, at_end = pl.program_id(d) == 0, pl.program_id(d) == steps - 1
            first = at_start if first is None else first & at_start
            last = at_end if last is None else last & at_end

        @pl.when(first)
        def _():
            ex.start(x_in, x_out, sems)

        body(*ins, *outs, *scratch)

        @pl.when(last)
        def _():
            ex.finish(x_in, x_out, sems)

    res = pl.pallas_call(
        wrapped, name=name, grid=grid, in_specs=list(in_specs) + [ANY] * nx, out_specs=list(out_specs) + [ANY] * nx,
        out_shape=list(out_shape) + list(ex.out_shapes), scratch_shapes=list(scratch_shapes) + ex.scratch(),
        compiler_params=_cparams(("arbitrary",) * len(grid)),
    )(*operands, *ex.operands)
    return res[:n_out], res[n_out:]


def _gather_weights(shards):
    n = len(shards)

    def first_copies(src, out, sems):
        x, y, c = _place()
        me = 2 * x + y
        copies = [_remote(_half(src[t], c), _half(out[t].at[me], c), sems, (t, k), (px, py, c))
                  for t in range(n) for k, (px, py) in enumerate(_other_chips(x, y))]
        return copies + [_remote(src[t], out[t].at[me], sems, (t, 3), (x, y, 1 - c)) for t in range(n)]

    def start(src, out, sems):
        for cp in first_copies(src, out, sems):
            cp.start()

    def finish(src, out, sems):
        x, y, c = _place()
        me = 2 * x + y
        sibling = (x, y, 1 - c)
        chips = _other_chips(x, y)
        passes = []
        for t in range(n):
            for k, (px, py) in enumerate(chips):
                landed = _half(out[t].at[2 * px + py], c)
                _remote(landed, landed, sems, (t, k), (px, py, c)).wait_recv()
                passes.append(_remote(landed, landed, sems, (t, 4 + k), sibling))
                passes[-1].start()
        for t in range(n):
            _remote(src[t], out[t].at[me], sems, (t, 3), sibling).wait_recv()
            for k, (px, py) in enumerate(chips):
                passed = _half(out[t].at[2 * px + py], 1 - c)
                _remote(passed, passed, sems, (t, 4 + k), sibling).wait_recv()
        for cp in first_copies(src, out, sems) + passes:
            cp.wait_send()

    return _Exchange(shards, [jax.ShapeDtypeStruct((N_CHIPS,) + s.shape, s.dtype) for s in shards], (n, 7), start, finish)


def _simple_exchange(operands, out_shapes, copies):
    def start(src, out, sems):
        for cp in copies(src, out, sems):
            cp.start()

    def finish(src, out, sems):
        for cp in copies(src, out, sems):
            cp.wait_recv()
        for cp in copies(src, out, sems):
            cp.wait_send()

    return _Exchange(operands, out_shapes, (len(operands),), start, finish)


def _swap_halves(slots):
    def copies(src, out, sems):
        x, y, c = _place()
        res = []
        for t in range(len(slots)):
            r = src[t].shape[1] // 2
            rows = pl.ds(pl.multiple_of((1 - c) * r, F32_ROWS), r)
            res.append(_remote(src[t].at[:, rows], out[t], sems, t, (x, y, 1 - c)))
        return res

    return _simple_exchange(slots, [jax.ShapeDtypeStruct((N_CHIPS, s.shape[1] // 2, s.shape[2]), s.dtype) for s in slots], copies)


def _exchange_chips(sums):
    n = len(sums)

    def copies(src, out, sems):
        x, y, c = _place()
        return [_remote(src[t].at[2 * px + py], out[t].at[k], sems, (t, k), (px, py, c))
                for t in range(n) for k, (px, py) in enumerate(_other_chips(x, y))]

    def start(src, out, sems):
        for cp in copies(src, out, sems):
            cp.start()

    def finish(src, out, sems):
        for cp in copies(src, out, sems):
            cp.wait_recv()
        for cp in copies(src, out, sems):
            cp.wait_send()

    return _Exchange(sums, [jax.ShapeDtypeStruct((3,) + s.shape[1:], s.dtype) for s in sums], (n, 3), start, finish)


def _share_halves(mine):
    def copies(src, out, sems):
        x, y, c = _place()
        return [_remote(src[t], out[t], sems, t, (x, y, 1 - c)) for t in range(len(mine))]

    return _simple_exchange(mine, [jax.ShapeDtypeStruct(s.shape, s.dtype) for s in mine], copies)


def _half_tile(rows):
    return 256 if rows % 256 == 0 else rows


def _sum_sibling(place, slot, received, name):
    n, rows2, cols = slot.shape
    rows = rows2 // 2
    tile = _half_tile(rows)
    nb = rows // tile

    def body(place_ref, a_ref, b_ref, o_ref):
        o_ref[...] = (a_ref[...] + b_ref[...]).astype(BF16)

    return pl.pallas_call(
        body, name=name, out_shape=jax.ShapeDtypeStruct((n, rows, cols), BF16),
        grid_spec=pltpu.PrefetchScalarGridSpec(
            num_scalar_prefetch=1, grid=(n, nb),
            in_specs=[pl.BlockSpec((None, tile, cols), lambda j, i, pr: (j, pr[1] * nb + i, 0)),
                      pl.BlockSpec((None, tile, cols), lambda j, i, pr: (j, i, 0))],
            out_specs=pl.BlockSpec((None, tile, cols), lambda j, i, pr: (j, i, 0))),
        compiler_params=_cparams(("parallel", "parallel")),
    )(place, slot, received)


def _sum_chips(place, slot, received, others, name):
    _, rows2, cols = slot.shape
    rows = rows2 // 2
    tile = _half_tile(rows)
    nb = rows // tile

    def body(place_ref, a_ref, b_ref, p_ref, o_ref):
        own = a_ref[...] + b_ref[...]
        o_ref[...] = ((own + p_ref[0].astype(F32)) + p_ref[1].astype(F32)) + p_ref[2].astype(F32)

    return pl.pallas_call(
        body, name=name, out_shape=jax.ShapeDtypeStruct((rows, cols), F32),
        grid_spec=pltpu.PrefetchScalarGridSpec(
            num_scalar_prefetch=1, grid=(nb,),
            in_specs=[pl.BlockSpec((None, tile, cols), lambda i, pr: (pr[0], pr[1] * nb + i, 0)),
                      pl.BlockSpec((None, tile, cols), lambda i, pr: (pr[0], i, 0)),
                      pl.BlockSpec((3, tile, cols), lambda i, pr: (0, i, 0))],
            out_specs=pl.BlockSpec((tile, cols), lambda i, pr: (i, 0))),
        compiler_params=_cparams(("parallel",)),
    )(place, slot, received, others)


N_DEVICES = 8


def _sum_devices(block, name):
    def body(v_ref, o_ref, land_ref, send_sems, recv_sems):
        x, y, c = _place()
        me = 4 * x + 2 * y + c
        copies = []
        for mask in range(1, N_DEVICES):
            peer = (x ^ (mask >> 2), y ^ ((mask >> 1) & 1), c ^ (mask & 1))
            copies.append(pltpu.make_async_remote_copy(src_ref=v_ref, dst_ref=land_ref.at[me], send_sem=send_sems.at[mask - 1],
                                                       recv_sem=recv_sems.at[mask - 1], device_id=peer, device_id_type=MESH))
        for cp in copies:
            cp.start()
        land_ref[me] = v_ref[...]
        for cp in copies:
            cp.wait_recv()
        total = land_ref[0]
        for d in range(1, N_DEVICES):
            total = total + land_ref[d]
        o_ref[...] = total
        for cp in copies:
            cp.wait_send()

    vmem = pl.BlockSpec(memory_space=pltpu.VMEM)
    return pl.pallas_call(
        body, name=name, in_specs=[vmem], out_specs=vmem, out_shape=jax.ShapeDtypeStruct(block.shape, F32),
        scratch_shapes=[pltpu.VMEM((N_DEVICES,) + block.shape, F32), pltpu.SemaphoreType.DMA((N_DEVICES - 1,)),
                        pltpu.SemaphoreType.DMA((N_DEVICES - 1,))],
    )(block)


def _vec_block(g_mix, g_mlp, g_ple, g_final, b_forget, b_gate_rows, last=None):
    pad = lambda a: jnp.concatenate([a, jnp.zeros((a.shape[0], D_MODEL - a.shape[1]), F32)], axis=1)
    last = jnp.zeros((1, 0), F32) if last is None else last
    return jnp.concatenate([g_mix, g_mlp, g_ple, g_final.reshape(1, D_MODEL), pad(b_forget), pad(b_gate_rows), pad(last)],
                           axis=0)


def _adam_math(w, g, m, v):
    m_new = ADAM_B1 * m + (1.0 - ADAM_B1) * g
    v_new = ADAM_B2 * v + (1.0 - ADAM_B2) * (g * g)
    m_hat = m_new / (1.0 - ADAM_B1 ** ADAM_STEP)
    v_hat = v_new / (1.0 - ADAM_B2 ** ADAM_STEP)
    return -ADAM_LR * (m_hat / (jnp.sqrt(v_hat) + ADAM_EPS) + ADAM_WD * w), m_new, v_new


def _adamw_halves(place, w, m, v, g_mine, g_theirs, name):
    rows2, cols = w.shape
    rows = rows2 // 2
    tile = _half_tile(rows)
    nb = rows // tile

    def body(place_ref, w_ref, m_ref, v_ref, gm_ref, gt_ref, g_ref, d_ref, nm_ref, nv_ref):
        g = jnp.where(pl.program_id(0) == 0, gm_ref[...], gt_ref[...])
        g_ref[...] = g
        d_ref[...], nm_ref[...], nv_ref[...] = _adam_math(w_ref[...], g, m_ref[...], v_ref[...])

    whole = pl.BlockSpec((tile, cols), lambda s, i, pr: ((pr[1] + s - 2 * pr[1] * s) * nb + i, 0))
    half = pl.BlockSpec((tile, cols), lambda s, i, pr: (i, 0))
    return pl.pallas_call(
        body, name=name, out_shape=[jax.ShapeDtypeStruct((rows2, cols), F32)] * 4,
        grid_spec=pltpu.PrefetchScalarGridSpec(num_scalar_prefetch=1, grid=(2, nb), in_specs=[whole] * 3 + [half] * 2,
                                               out_specs=[whole] * 4),
        compiler_params=_cparams(("parallel", "parallel")),
    )(place, w, m, v, g_mine, g_theirs)


def _adamw_vec(w, g, m, v):
    def body(w_ref, g_ref, m_ref, v_ref, d_ref, nm_ref, nv_ref):
        d_ref[...], nm_ref[...], nv_ref[...] = _adam_math(w_ref[...], g_ref[...], m_ref[...], v_ref[...])

    return pl.pallas_call(body, name="adamw_vectors", out_shape=[jax.ShapeDtypeStruct(w.shape, F32)] * 3)(w, g, m, v)


WEIGHT_NAMES = ("g_mix", "w_in", "b_forget", "b_gate", "w_branch_fox", "w_branch_sb", "w_out", "g_mlp", "w_up", "w_down",
                "g_ple", "w_ple_gate", "w_ple", "g_final")
W_IN_SHARD = D_IN // N_CHIPS
Q_END, F_END, B_END = 3 * D_ATT, 3 * D_ATT + N_HEADS, 6 * D_ATT + N_HEADS
GATE_SHARD = D_MODEL // N_CHIPS


def _join_cols(slots):
    return jnp.transpose(slots, (1, 0, 2)).reshape(slots.shape[1], N_CHIPS * slots.shape[2])


LATE = SHARDED[:1]
EARLY = SHARDED[1:]


def _first_weights(w_in_slots, b_gate):
    w_in = _join_cols(w_in_slots)
    forget = jnp.concatenate([w_in[:, Q_END:F_END], jnp.zeros((D_MODEL, F_PAD - N_HEADS), BF16)], axis=1)
    return {"qkv": jnp.concatenate([w_in[:, :Q_END], w_in[:, F_END:B_END]], axis=1), "gate": w_in[:, B_END:], "forget": forget,
            "b_gate": b_gate}


def _rest_weights(gathered):
    rows = lambda a: a.reshape(N_CHIPS * a.shape[1], a.shape[2])
    return {"branch_fox": gathered["w_branch_fox"], "branch_sb": gathered["w_branch_sb"], "out": rows(gathered["w_out"]),
            "up": gathered["w_up"], "down": rows(gathered["w_down"]), "ple_gate": rows(gathered["w_ple_gate"]),
            "ple": gathered["w_ple"]}


def _early_slots(gw):
    rows = lambda a: a.reshape(N_CHIPS, a.shape[0] // N_CHIPS, a.shape[1])
    return {"w_branch_fox": gw["branch_fox"], "w_branch_sb": gw["branch_sb"], "w_out": rows(gw["out"]), "w_up": gw["up"],
            "w_down": rows(gw["down"]), "w_ple_gate": rows(gw["ple_gate"]), "w_ple": gw["ple"]}


W_IN_FLAT = (W_IN_SHARD * D_MODEL // LANES, LANES)


def _w_in_slots(gw):
    g_t = jnp.concatenate([gw["qkv"][:Q_END], gw["forget"][:N_HEADS], gw["qkv"][Q_END:], gw["gate"]], axis=0)
    return g_t.reshape((N_CHIPS,) + W_IN_FLAT)


def _flat(a):
    return jnp.transpose(a, (2, 0, 1)).reshape(W_IN_FLAT)


def _unflat(a):
    return jnp.transpose(a.reshape(W_IN_SHARD, 1, D_MODEL), (1, 2, 0))


def kernel(x, p, g_mix, w_in, b_forget, b_gate, w_branch_fox, w_branch_sb, w_out, g_mlp, w_up, w_down, g_ple, w_ple_gate, w_ple, g_final, loss_target, m_g_mix, m_w_in, m_b_forget, m_b_gate, m_w_branch_fox, m_w_branch_sb, m_w_out, m_g_mlp, m_w_up, m_w_down, m_g_ple, m_w_ple_gate, m_w_ple, m_g_final, v_g_mix, v_w_in, v_b_forget, v_b_gate, v_w_branch_fox, v_w_branch_sb, v_w_out, v_g_mlp, v_w_up, v_w_down, v_g_ple, v_w_ple_gate, v_w_ple, v_g_final):
    weights = dict(g_mix=g_mix, w_in=w_in, b_forget=b_forget, b_gate=b_gate, w_branch_fox=w_branch_fox,
                   w_branch_sb=w_branch_sb, w_out=w_out, g_mlp=g_mlp, w_up=w_up, w_down=w_down, g_ple=g_ple,
                   w_ple_gate=w_ple_gate, w_ple=w_ple, g_final=g_final)
    first = dict(g_mix=m_g_mix, w_in=m_w_in, b_forget=m_b_forget, b_gate=m_b_gate, w_branch_fox=m_w_branch_fox,
                 w_branch_sb=m_w_branch_sb, w_out=m_w_out, g_mlp=m_g_mlp, w_up=m_w_up, w_down=m_w_down, g_ple=m_g_ple,
                 w_ple_gate=m_w_ple_gate, w_ple=m_w_ple, g_final=m_g_final)
    second = dict(g_mix=v_g_mix, w_in=v_w_in, b_forget=v_b_forget, b_gate=v_b_gate, w_branch_fox=v_w_branch_fox,
                  w_branch_sb=v_w_branch_sb, w_out=v_w_out, g_mlp=v_g_mlp, w_up=v_w_up, w_down=v_w_down, g_ple=v_g_ple,
                  w_ple_gate=v_w_ple_gate, w_ple=v_w_ple, g_final=v_g_final)
    cx, cy, cc = _place()
    chip = 2 * cx + cy
    place = jnp.stack([chip, cc]).astype(jnp.int32)
    col0 = chip * GATE_SHARD

    (w_in_slots,) = _run_exchange(_gather_weights([weights[n][0].astype(BF16) for n in LATE]), "gather_w_in")
    rest = _gather_weights([weights[n][0].astype(BF16) for n in EARLY])
    gate_rows = lax.dynamic_update_slice(jnp.zeros((2, D_MODEL), F32), b_gate[0] * (cc == 0).astype(F32), (0, col0))
    zero_row = jnp.zeros((1, D_MODEL), F32)
    b_gate_whole = _sum_devices(_vec_block(zero_row, zero_row, zero_row, zero_row[0], zero_row[:, :N_HEADS], gate_rows),
                                "gather_b_gate")[5:7]
    vec = {"g_mix": g_mix, "b_forget": jnp.concatenate([b_forget, jnp.zeros((1, F_PAD - N_HEADS), F32)], axis=1),
           "g_mlp": g_mlp, "g_ple": g_ple, "g_final": g_final.reshape(1, D_MODEL)}

    loss, grad_x, reduced, gvec = _local_step(x, p[0], loss_target, _first_weights(w_in_slots, b_gate_whole), rest, vec,
                                              place)

    out = {}
    for n in EARLY:
        g_mine, g_theirs = reduced[n]
        res = _adamw_halves(place, weights[n][0], first[n][0], second[n][0], g_mine, g_theirs, "adamw_" + n)
        out[n] = [r[None] for r in res]
    g_mine, g_theirs = reduced["w_in"]
    out["w_in"] = [_unflat(r) for r in _adamw_halves(place, _flat(w_in), _flat(m_w_in), _flat(v_w_in), g_mine, g_theirs,
                                                      "adamw_w_in")]

    g_block = _sum_devices(_vec_block(gvec["g_mix"], gvec["g_mlp"], gvec["g_ple"], gvec["g_final"][0], gvec["b_forget"],
                                      gvec["b_gate"], loss), "reduce_vectors")
    loss = g_block[7, 0]
    g_gate = lax.dynamic_slice(g_block[5:7], (0, col0), (2, GATE_SHARD))
    blocks = [_vec_block(d["g_mix"], d["g_mlp"], d["g_ple"], d["g_final"], d["b_forget"], d["b_gate"][0])
              for d in (weights, first, second)]
    g_rows = jnp.concatenate([g_block[0:5], jnp.concatenate([g_gate, jnp.zeros((2, D_MODEL - GATE_SHARD), F32)], axis=1),
                              jnp.zeros((1, D_MODEL), F32)], axis=0)
    res = (g_rows,) + tuple(_adamw_vec(blocks[0], g_rows, blocks[1], blocks[2]))
    out["g_mix"] = [r[0:1] for r in res]
    out["g_mlp"] = [r[1:2] for r in res]
    out["g_ple"] = [r[2:3] for r in res]
    out["g_final"] = [r[3] for r in res]
    out["b_forget"] = [r[4:5, :N_HEADS] for r in res]
    out["b_gate"] = [r[5:7, :GATE_SHARD][None] for r in res]
    return (loss, grad_x, *[out[n][0] for n in WEIGHT_NAMES], *[out[n][1] for n in WEIGHT_NAMES],
            *[out[n][2] for n in WEIGHT_NAMES], *[out[n][3] for n in WEIGHT_NAMES])
```

```python
import jax
import jax.numpy as jnp
from jax import lax
from jax.experimental import pallas as pl
from jax.experimental.pallas import tpu as pltpu

F32 = jnp.float32
BF16 = jnp.bfloat16

D_MODEL = 1024
HEAD_DIM = 64
N_HEADS = 8
D_ATT = N_HEADS * HEAD_DIM
D_FF = 4 * D_MODEL
D_PLE = 256
D_IN = 6 * D_ATT + N_HEADS + 2 * D_MODEL
F_PAD = 128
EPS = 1e-6
SCALE = HEAD_DIM ** -0.5
N_CHIPS = 4
LANES = 128
ATT_BLOCK = 256
FOX_TILES = (512, 512)
SB_TILES = (512, 256)
NEG = -1e30

ADAM_LR = 0.001
ADAM_B1 = 0.9
ADAM_B2 = 0.999
ADAM_EPS = 1e-08
ADAM_WD = 0.01
ADAM_STEP = 10

VMEM_LIMIT = 56 * 1024 * 1024

MESH = pl.DeviceIdType.MESH


def _cparams(sem=None):
    return pltpu.CompilerParams(dimension_semantics=sem, vmem_limit_bytes=VMEM_LIMIT)


def _relu2(t):
    t = t.astype(F32)
    return t * t


_DIMS = {"nn": (((1,), (0,)), ((), ())), "nt": (((1,), (1,)), ((), ())), "tn": (((0,), (0,)), ((), ()))}
NT_DIMS = _DIMS["nt"]
TN_DIMS = _DIMS["tn"]


def _mm(a, b, *, mode, name, out_dtype=F32, tm=512, tn=512, tk=512, add=None, a_fn=None, epi=None, extra=None,
        col_shards=False, behind=None):
    if mode == "nn":
        (m, k), n = a.shape, b.shape[-1]
    elif mode == "nt":
        (m, k), n = a.shape, b.shape[-2]
    else:
        (k, m), n = a.shape, b.shape[1]
    shard = None
    if col_shards:
        if mode == "nn":
            shard, n = n, N_CHIPS * n
            tn = min(tn, shard)
        elif mode == "nt":
            shard = b.shape[-1]
            tk = min(tk, shard)
        else:
            shard = n // N_CHIPS
            tn = min(tn, shard)
    tm, tn, tk = min(tm, m), min(tn, n), min(tk, k)
    assert m % tm == 0 and n % tn == 0 and k % tk == 0, (name, m, n, k)
    nk = k // tk
    a_spec = {"nn": pl.BlockSpec((tm, tk), lambda i, j, kk: (i, kk)),
              "nt": pl.BlockSpec((tm, tk), lambda i, j, kk: (i, kk)),
              "tn": pl.BlockSpec((tk, tm), lambda i, j, kk: (kk, i))}[mode]
    b_spec = {"nn": pl.BlockSpec((tk, tn), lambda i, j, kk: (kk, j)),
              "nt": pl.BlockSpec((tn, tk), lambda i, j, kk: (j, kk)),
              "tn": pl.BlockSpec((tk, tn), lambda i, j, kk: (kk, j))}[mode]
    o_spec = pl.BlockSpec((tm, tn), lambda i, j, kk: (i, j))
    out_shape = (m, n)
    if col_shards and mode == "nn":
        per = shard // tn
        b_spec = pl.BlockSpec((None, tk, tn), lambda i, j, kk: (j // per, kk, j % per))
    elif col_shards and mode == "nt":
        per = shard // tk
        b_spec = pl.BlockSpec((None, tn, tk), lambda i, j, kk: (kk // per, j, kk % per))
    elif col_shards:
        assert add is None and extra is None
        per = shard // tn
        o_spec = pl.BlockSpec((None, tm, tn), lambda i, j, kk: (j // per, i, j % per))
        out_shape = (N_CHIPS, m, shard)
    operands, in_specs = [a, b], [a_spec, b_spec]
    third = add if add is not None else extra
    if third is not None:
        operands.append(third)
        in_specs.append(o_spec)

    def body(*refs):
        a_ref, b_ref = refs[0], refs[1]
        t_ref = refs[2] if third is not None else None
        o_ref = refs[3] if third is not None else refs[2]
        acc_ref = refs[-1] if nk > 1 else None
        at = a_ref[...]
        if a_fn is not None:
            at = a_fn(at)
        part = lax.dot_general(at.astype(BF16), b_ref[...].astype(BF16), _DIMS[mode], preferred_element_type=F32)

        def finish(acc):
            if epi is not None:
                acc = epi(acc, None if t_ref is None else t_ref[...])
            elif add is not None:
                acc = acc + t_ref[...].astype(F32)
            o_ref[...] = acc.astype(o_ref.dtype)

        if nk == 1:
            finish(part)
        else:
            kk = pl.program_id(2)

            @pl.when(kk == 0)
            def _():
                acc_ref[...] = part

            @pl.when(kk > 0)
            def _():
                acc_ref[...] += part

            @pl.when(kk == nk - 1)
            def _():
                finish(acc_ref[...])

    call = dict(name=name, grid=(m // tm, n // tn, nk), in_specs=in_specs,
                scratch_shapes=[pltpu.VMEM((tm, tn), F32)] if nk > 1 else [])
    if behind is not None:
        (res,), exchanged = _call_behind(body, behind, out_specs=[o_spec], out_shape=[jax.ShapeDtypeStruct(out_shape, out_dtype)],
                                         operands=operands, **call)
        return res, exchanged
    return pl.pallas_call(body, out_specs=o_spec, out_shape=jax.ShapeDtypeStruct(out_shape, out_dtype),
                          compiler_params=_cparams(("parallel", "parallel", "arbitrary")), **call)(*operands)


ROW_TILE = 512


def _row_spec(width=D_MODEL, rows=ROW_TILE):
    return pl.BlockSpec((rows, width), lambda i: (i, 0))


def _vec_spec(rows=1, width=D_MODEL):
    return pl.BlockSpec((rows, width), lambda i: (0, 0))


def _xhat(x):
    r = lax.rsqrt(jnp.mean(x * x, axis=-1, keepdims=True) + EPS)
    return x * r, r


def _rms_bwd_rows(dh, x, g):
    xh, r = _xhat(x)
    dxh = dh * g
    dx = r * (dxh - xh * jnp.mean(dxh * xh, axis=-1, keepdims=True))
    return dx, jnp.sum(dh * xh, axis=0, keepdims=True)


def _norm_fwd(x, g, name):
    t = x.shape[0]

    def body(x_ref, g_ref, h_ref):
        xh, _ = _xhat(x_ref[...])
        h_ref[...] = (xh * g_ref[...]).astype(BF16)

    return pl.pallas_call(
        body, name=name, grid=(t // ROW_TILE,), in_specs=[_row_spec(), _vec_spec()], out_specs=_row_spec(),
        out_shape=jax.ShapeDtypeStruct((t, D_MODEL), BF16), compiler_params=_cparams(("parallel",)),
    )(x, g)


def _norm_bwd(x, g, dh, dres, name):
    t = x.shape[0]

    def body(x_ref, g_ref, dh_ref, dres_ref, dx_ref, dxb_ref, dg_ref):
        dx, dg = _rms_bwd_rows(dh_ref[...], x_ref[...], g_ref[...])
        dx = dx + dres_ref[...]
        dx_ref[...] = dx
        dxb_ref[...] = dx.astype(BF16)

        @pl.when(pl.program_id(0) == 0)
        def _():
            dg_ref[...] = jnp.zeros_like(dg_ref)

        dg_ref[...] += dg

    return pl.pallas_call(
        body, name=name, grid=(t // ROW_TILE,),
        in_specs=[_row_spec(), _vec_spec(), _row_spec(), _row_spec()],
        out_specs=[_row_spec(), _row_spec(), _vec_spec()],
        out_shape=[jax.ShapeDtypeStruct((t, D_MODEL), F32), jax.ShapeDtypeStruct((t, D_MODEL), BF16),
                   jax.ShapeDtypeStruct((1, D_MODEL), F32)],
        compiler_params=_cparams(("arbitrary",)),
    )(x, g, dh, dres)


def _mm_res_norm(a, b, res, g, name, a_fn=None):
    t, k = a.shape

    def body(a_ref, b_ref, res_ref, g_ref, x_ref, h_ref):
        at = a_ref[...] if a_fn is None else a_fn(a_ref[...])
        x_new = res_ref[...] + _dot(at.astype(BF16), b_ref[...])
        x_ref[...] = x_new
        h_ref[...] = (_xhat(x_new)[0] * g_ref[...]).astype(BF16)

    return pl.pallas_call(
        body, name=name, grid=(t // ROW_TILE,),
        in_specs=[pl.BlockSpec((ROW_TILE, k), lambda i: (i, 0)), pl.BlockSpec(b.shape, lambda i: (0, 0)), _row_spec(), _vec_spec()],
        out_specs=[_row_spec(), _row_spec()],
        out_shape=[jax.ShapeDtypeStruct((t, D_MODEL), F32), jax.ShapeDtypeStruct((t, D_MODEL), BF16)],
        compiler_params=_cparams(("parallel",)),
    )(a, b, res, g)


def _mm_norm_bwd(pairs, dh_first, x, g, dres, name, behind=None):
    t = x.shape[0]
    operands, in_specs = [], []
    for a, b in pairs:
        if b.ndim == 3:
            for j in range(b.shape[0]):
                operands += [a, b]
                in_specs += [pl.BlockSpec((ROW_TILE, b.shape[2]), lambda i, j=j: (i, j)),
                             pl.BlockSpec((None, D_MODEL, b.shape[2]), lambda i, j=j: (j, 0, 0))]
        else:
            operands += [a, b]
            in_specs += [pl.BlockSpec((ROW_TILE, a.shape[1]), lambda i: (i, 0)), pl.BlockSpec(b.shape, lambda i: (0, 0))]
    n_mm = len(operands)
    operands += [x, g, dres] + ([] if dh_first is None else [dh_first])
    in_specs += [_row_spec(), _vec_spec(), _row_spec()] + ([] if dh_first is None else [_row_spec()])

    def body(*refs):
        x_ref, g_ref, dres_ref = refs[n_mm:n_mm + 3]
        dx_ref, dxb_ref, dg_ref = refs[-3:]
        dh = 0.0 if dh_first is None else refs[n_mm + 3][...]
        for k in range(0, n_mm, 2):
            dh = dh + lax.dot_general(refs[k][...].astype(BF16), refs[k + 1][...].astype(BF16), NT_DIMS,
                                      preferred_element_type=F32)
        dx, dg = _rms_bwd_rows(dh, x_ref[...], g_ref[...])
        dx = dx + dres_ref[...]
        dx_ref[...] = dx
        dxb_ref[...] = dx.astype(BF16)

        @pl.when(pl.program_id(0) == 0)
        def _():
            dg_ref[...] = jnp.zeros_like(dg_ref)

        dg_ref[...] += dg

    call = dict(name=name, grid=(t // ROW_TILE,), in_specs=in_specs, out_specs=[_row_spec(), _row_spec(), _vec_spec()],
                out_shape=[jax.ShapeDtypeStruct((t, D_MODEL), F32), jax.ShapeDtypeStruct((t, D_MODEL), BF16),
                           jax.ShapeDtypeStruct((1, D_MODEL), F32)])
    if behind is not None:
        return _call_behind(body, behind, scratch_shapes=[], operands=operands, **call)
    return pl.pallas_call(body, compiler_params=_cparams(("arbitrary",)), **call)(*operands)


def _gate_fwd(gl, b_gate, of, os_):
    t = of.shape[0]

    def body(gla_ref, glb_ref, b_ref, of_ref, os_ref, m_ref):
        ga = jax.nn.sigmoid(gla_ref[...] + b_ref[0:1, :])
        gb = jax.nn.sigmoid(glb_ref[...] + b_ref[1:2, :])
        m_ref[...] = (ga * of_ref[...] + gb * os_ref[...]).astype(BF16)

    return pl.pallas_call(
        body, name="gate_fwd", grid=(t // ROW_TILE,),
        in_specs=[pl.BlockSpec((ROW_TILE, D_MODEL), lambda i: (i, 0)), pl.BlockSpec((ROW_TILE, D_MODEL), lambda i: (i, 1)),
                  _vec_spec(2), _row_spec(), _row_spec()],
        out_specs=_row_spec(), out_shape=jax.ShapeDtypeStruct((t, D_MODEL), BF16),
        compiler_params=_cparams(("parallel",)),
    )(gl, gl, b_gate, of, os_)


def _gate_bwd(gl, b_gate, of, os_, dmerged):
    t = of.shape[0]

    def body(gla_ref, glb_ref, b_ref, of_ref, os_ref, dm_ref, dof_ref, dos_ref, dgla_ref, dglb_ref, db_ref):
        dm = dm_ref[...]
        ga = jax.nn.sigmoid(gla_ref[...] + b_ref[0:1, :])
        gb = jax.nn.sigmoid(glb_ref[...] + b_ref[1:2, :])
        dof_ref[...] = (dm * ga).astype(BF16)
        dos_ref[...] = (dm * gb).astype(BF16)
        dgla = dm * of_ref[...] * ga * (1.0 - ga)
        dglb = dm * os_ref[...] * gb * (1.0 - gb)
        dgla_ref[...] = dgla.astype(BF16)
        dglb_ref[...] = dglb.astype(BF16)

        @pl.when(pl.program_id(0) == 0)
        def _():
            db_ref[...] = jnp.zeros_like(db_ref)

        db_ref[0:1, :] += jnp.sum(dgla, axis=0, keepdims=True)
        db_ref[1:2, :] += jnp.sum(dglb, axis=0, keepdims=True)

    outs = pl.pallas_call(
        body, name="gate_bwd", grid=(t // ROW_TILE,),
        in_specs=[pl.BlockSpec((ROW_TILE, D_MODEL), lambda i: (i, 0)), pl.BlockSpec((ROW_TILE, D_MODEL), lambda i: (i, 1)),
                  _vec_spec(2), _row_spec(), _row_spec(), _row_spec()],
        out_specs=[_row_spec(), _row_spec(), _row_spec(), _row_spec(), _vec_spec(2)],
        out_shape=[jax.ShapeDtypeStruct((t, D_MODEL), BF16)] * 4 + [jax.ShapeDtypeStruct((2, D_MODEL), F32)],
        compiler_params=_cparams(("arbitrary",)),
    )(gl, gl, b_gate, of, os_, dmerged)
    return outs


def _head_and_loss(x2, gpre, pe, g_final, target):
    t = x2.shape[0]

    def body(x2_ref, gpre_ref, pe_ref, g_ref, tgt_ref, dx3_ref, dpre_ref, dpe_ref, dg_ref, loss_ref):
        gp = jax.nn.sigmoid(gpre_ref[...])
        pe_t = pe_ref[...]
        x3 = x2_ref[...] + gp * pe_t
        g = g_ref[...]
        xh, _ = _xhat(x3)
        err = xh * g - tgt_ref[...]
        dy = err * (1.0 / D_MODEL)
        dx3, dg = _rms_bwd_rows(dy, x3, g)
        dx3_ref[...] = dx3
        dpre_ref[...] = (dx3 * pe_t * gp * (1.0 - gp)).astype(BF16)
        dpe_ref[...] = (dx3 * gp).astype(BF16)

        @pl.when(pl.program_id(0) == 0)
        def _():
            dg_ref[...] = jnp.zeros_like(dg_ref)
            loss_ref[...] = jnp.zeros_like(loss_ref)

        dg_ref[...] += dg
        loss_ref[...] += 0.5 * jnp.sum(jnp.mean(err * err, axis=-1, keepdims=True), axis=0, keepdims=True)

    return pl.pallas_call(
        body, name="head_and_loss", grid=(t // ROW_TILE,),
        in_specs=[_row_spec(), _row_spec(), _row_spec(), _vec_spec(), _row_spec()],
        out_specs=[_row_spec(), _row_spec(), _row_spec(), _vec_spec(), _vec_spec(1, LANES)],
        out_shape=[jax.ShapeDtypeStruct((t, D_MODEL), F32), jax.ShapeDtypeStruct((t, D_MODEL), BF16),
                   jax.ShapeDtypeStruct((t, D_MODEL), BF16), jax.ShapeDtypeStruct((1, D_MODEL), F32),
                   jax.ShapeDtypeStruct((1, LANES), F32)],
        compiler_params=_cparams(("arbitrary",)),
    )(x2, gpre, pe, g_final, target)


def _split3(v):
    hi = v.astype(BF16)
    r1 = v - hi.astype(F32)
    mid = r1.astype(BF16)
    lo = (r1 - mid.astype(F32)).astype(BF16)
    return hi, mid, lo


def _split2(v):
    hi = v.astype(BF16)
    return jnp.concatenate([hi, (v - hi.astype(F32)).astype(BF16)], axis=1)


def _dot(a, b, dims=_DIMS["nn"]):
    return lax.dot_general(a, b, dims, preferred_element_type=F32)


def _tri(n, rel):
    row = lax.broadcasted_iota(jnp.int32, (n, n), 0)
    col = lax.broadcasted_iota(jnp.int32, (n, n), 1)
    return rel(row, col).astype(BF16)


def _tri2(n, rel):
    t = _tri(n, rel)
    return jnp.concatenate([t, t], axis=0)


def _log_sigmoid(v):
    return -(jnp.maximum(-v, 0.0) + jnp.log(1.0 + jnp.exp(-jnp.abs(v))))


def _fox_prep(fl, b_forget, batch, seq):
    nb = seq // ATT_BLOCK

    def body(fl_ref, b_ref, cw_ref, cr_ref):
        col = lax.broadcasted_iota(jnp.int32, (ATT_BLOCK, F_PAD), 1)
        lower = _tri(ATT_BLOCK, lambda r, c: c <= r)
        upper = _tri(ATT_BLOCK, lambda r, c: r <= c)
        expand = (lax.broadcasted_iota(jnp.int32, (F_PAD, D_ATT), 1) // HEAD_DIM
                  == lax.broadcasted_iota(jnp.int32, (F_PAD, D_ATT), 0)).astype(BF16)
        carry_w = jnp.zeros((1, D_ATT), F32)
        carry_r = jnp.zeros((F_PAD, 1), F32)
        for i in range(nb):
            blk = slice(i * ATT_BLOCK, (i + 1) * ATT_BLOCK)
            logf = jnp.where(col < N_HEADS, _log_sigmoid(fl_ref[blk, :] + b_ref[...]), 0.0)
            cw = jnp.zeros((ATT_BLOCK, D_ATT), F32) + carry_w
            cr = jnp.zeros((F_PAD, ATT_BLOCK), F32) + carry_r
            for part in _split3(logf):
                cw += _dot(lower, _dot(part, expand).astype(BF16))
                cr += _dot(part, upper, TN_DIMS)
            cw_ref[blk, :] = cw
            cr_ref[:, blk] = cr[0:N_HEADS, :]
            carry_w = cw[ATT_BLOCK - 1:ATT_BLOCK, :]
            carry_r = cr[:, ATT_BLOCK - 1:ATT_BLOCK]

    return pl.pallas_call(
        body, name="fox_prep", grid=(batch,),
        in_specs=[pl.BlockSpec((seq, F_PAD), lambda b: (b, 0)), pl.BlockSpec((1, F_PAD), lambda b: (0, 0))],
        out_specs=[pl.BlockSpec((seq, D_ATT), lambda b: (b, 0)), pl.BlockSpec((N_HEADS, seq), lambda b: (b, 0))],
        out_shape=[jax.ShapeDtypeStruct((batch * seq, D_ATT), F32), jax.ShapeDtypeStruct((batch * N_HEADS, seq), F32)],
        compiler_params=_cparams(("parallel",)),
    )(fl, b_forget)


def _fox_post(dcs_wide, drs_wide, fl, b_forget, batch, seq):
    nb = seq // ATT_BLOCK

    def body(dcs_ref, drs_ref, fl_ref, b_ref, dfl_ref, db_ref):
        pick = (lax.broadcasted_iota(jnp.int32, (D_ATT, F_PAD), 0)
                == lax.broadcasted_iota(jnp.int32, (D_ATT, F_PAD), 1) * HEAD_DIM).astype(BF16)
        upper = _tri(ATT_BLOCK, lambda r, c: r <= c)
        col = lax.broadcasted_iota(jnp.int32, (ATT_BLOCK, F_PAD), 1)

        @pl.when(pl.program_id(0) == 0)
        def _():
            db_ref[...] = jnp.zeros_like(db_ref)

        carry = jnp.zeros((1, F_PAD), F32)
        for i in reversed(range(nb)):
            blk = slice(i * ATT_BLOCK, (i + 1) * ATT_BLOCK)
            narrow = jnp.zeros((ATT_BLOCK, F_PAD), F32)
            for part in _split3(drs_ref[blk, :] - dcs_ref[blk, :]):
                narrow += _dot(part, pick)
            after = jnp.zeros((ATT_BLOCK, F_PAD), F32) + carry
            for part in _split3(narrow):
                after += _dot(upper, part)
            carry = after[0:1, :]
            pre = fl_ref[blk, :] + b_ref[...]
            dfl = jnp.where(col < N_HEADS, after * jax.nn.sigmoid(-pre), 0.0)
            dfl_ref[blk, :] = dfl.astype(BF16)
            db_ref[...] += jnp.sum(dfl, axis=0, keepdims=True)

    return pl.pallas_call(
        body, name="fox_post", grid=(batch,),
        in_specs=[pl.BlockSpec((seq, D_ATT), lambda b: (b, 0)), pl.BlockSpec((seq, D_ATT), lambda b: (b, 0)),
                  pl.BlockSpec((seq, F_PAD), lambda b: (b, 0)), pl.BlockSpec((1, F_PAD), lambda b: (0, 0))],
        out_specs=[pl.BlockSpec((seq, F_PAD), lambda b: (b, 0)), pl.BlockSpec((1, F_PAD), lambda b: (0, 0))],
        out_shape=[jax.ShapeDtypeStruct((batch * seq, F_PAD), BF16), jax.ShapeDtypeStruct((1, F_PAD), F32)],
        compiler_params=_cparams(("arbitrary",)),
    )(dcs_wide, drs_wide, fl, b_forget)


N_PAIRS = N_HEADS // 2


def _att_specs(seq, col0, tq):
    nq = seq // tq
    q = pl.BlockSpec((tq, LANES), lambda b, hp, qi: (b * nq + qi, col0 + hp))
    k = pl.BlockSpec((seq, LANES), lambda b, hp, qi: (b, col0 + N_PAIRS + hp))
    v = pl.BlockSpec((seq, LANES), lambda b, hp, qi: (b, col0 + 2 * N_PAIRS + hp))
    return q, k, v


def _qblock_spec(seq, tq):
    nq = seq // tq
    return pl.BlockSpec((tq, LANES), lambda b, hp, qi: (b * nq + qi, hp))


def _kv_out_spec(seq):
    return pl.BlockSpec((seq, LANES), lambda b, hp, qi: (b, hp))


def _head_masks():
    lane = lax.broadcasted_iota(jnp.int32, (1, LANES), 1)
    return [(lane >= HEAD_DIM * j) & (lane < HEAD_DIM * (j + 1)) for j in range(2)]


def _stack_heads(t, masks):
    zero = jnp.zeros_like(t)
    return jnp.concatenate([jnp.where(masks[0], t, zero), jnp.where(masks[1], t, zero)], axis=0)


def _stack_cols(t):
    return jnp.concatenate([t[:, 0:1], t[:, HEAD_DIM:HEAD_DIM + 1]], axis=0)


def _unstack(t2, masks):
    tq = t2.shape[0] // 2
    return jnp.where(masks[0], t2[:tq], t2[tq:])


def _stacked_ids(tq, tk):
    row = lax.broadcasted_iota(jnp.int32, (2 * tq, tk), 0)
    col = lax.broadcasted_iota(jnp.int32, (2 * tq, tk), 1)
    first = lax.broadcasted_iota(jnp.int32, (2 * tq, 1), 0) < tq
    return col - jnp.where(row < tq, row, row - tq), first


def _sweep(qi, tq, tk, step, init, leftward):
    per = tq // tk
    whole = lambda carry: lax.fori_loop(0, per * qi, lambda i, c: step(per * qi - 1 - i if leftward else i, c, None), carry)
    crossed = [(per * qi + j, -j * tk) for j in range(per)]
    if leftward:
        carry = init
        for kb, lead in reversed(crossed):
            carry = step(kb, carry, lead)
        return whole(carry)
    carry = whole(init)
    for kb, lead in crossed:
        carry = step(kb, carry, lead)
    return carry


def _fox_fwd(qkv, c_wide, c_row, batch, seq):
    tq, tk = FOX_TILES
    nq = seq // tq

    def body(q_ref, k_ref, v_ref, cw_ref, cr_ref, o_ref, lse_ref):
        hp, qi = pl.program_id(1), pl.program_id(2)
        masks = _head_masks()
        ahead, first = _stacked_ids(tq, tk)
        q2 = _stack_heads(q_ref[...], masks) * SCALE
        ct = _stack_cols(cw_ref[...])

        def step(kb, carry, lead):
            m, l, acc = carry
            k0 = pl.multiple_of(kb * tk, tk)
            cs = jnp.where(first, cr_ref[pl.ds(2 * hp, 1), pl.ds(k0, tk)], cr_ref[pl.ds(2 * hp + 1, 1), pl.ds(k0, tk)])
            s = _dot(q2, k_ref[pl.ds(k0, tk), :], NT_DIMS) + ct - cs
            if lead is not None:
                s = jnp.where(ahead <= lead, s, NEG)
            m_new = jnp.maximum(m, jnp.max(s, axis=1, keepdims=True))
            p = jnp.exp(s - m_new)
            alpha = jnp.exp(m - m_new)
            l = alpha * l + jnp.sum(p, axis=1, keepdims=True)
            acc = alpha * acc + _dot(p.astype(BF16), v_ref[pl.ds(k0, tk), :])
            return m_new, l, acc

        init = (jnp.full((2 * tq, 1), NEG, F32), jnp.zeros((2 * tq, 1), F32), jnp.zeros((2 * tq, LANES), F32))
        m, l, acc = _sweep(qi, tq, tk, step, init, leftward=False)
        o_ref[...] = _unstack(acc / l, masks).astype(BF16)
        lse_ref[...] = _unstack(m + jnp.log(l), masks)

    q_spec, k_spec, v_spec = _att_specs(seq, 0, tq)
    qb = _qblock_spec(seq, tq)
    return pl.pallas_call(
        body, name="fox_fwd", grid=(batch, N_PAIRS, nq),
        in_specs=[q_spec, k_spec, v_spec, qb, pl.BlockSpec((N_HEADS, seq), lambda b, hp, qi: (b, 0))],
        out_specs=[qb, qb],
        out_shape=[jax.ShapeDtypeStruct((batch * seq, D_ATT), BF16), jax.ShapeDtypeStruct((batch * seq, D_ATT), F32)],
        compiler_params=_cparams(("parallel", "parallel", "arbitrary")),
    )(qkv, qkv, qkv, c_wide, c_row)


def _fox_bwd(qkv, c_wide, c_row, o, do, lse_wide, batch, seq, behind):
    tq, tk = FOX_TILES
    nq = seq // tq

    def body(q_ref, k_ref, v_ref, cw_ref, cr_ref, o_ref, do_ref, lse_ref,
             dq_ref, dk_ref, dv_ref, dcs_ref, drs_ref, dkc_acc, dv_acc):
        hp, qi = pl.program_id(1), pl.program_id(2)

        @pl.when(qi == 0)
        def _():
            dkc_acc[...] = jnp.zeros_like(dkc_acc)
            dv_acc[...] = jnp.zeros_like(dv_acc)

        masks = _head_masks()
        ahead, first = _stacked_ids(tq, tk)
        q_t, do_t = q_ref[...], do_ref[...]
        q2 = _stack_heads(q_t, masks) * SCALE
        do2 = _stack_heads(do_t, masks)
        q_and_ones = jnp.concatenate([q2, _stack_heads(jnp.ones_like(q_t), masks)], axis=1)
        ct = _stack_cols(cw_ref[...])
        lse = _stack_cols(lse_ref[...])
        prod = do_t.astype(F32) * o_ref[...].astype(F32)
        delta = jnp.concatenate([jnp.sum(jnp.where(mk, prod, 0.0), axis=1, keepdims=True) for mk in masks], axis=0)

        def step(kb, carry, lead):
            dq_acc, rs = carry
            k0 = pl.multiple_of(kb * tk, tk)
            kblk = k_ref[pl.ds(k0, tk), :]
            cs = jnp.where(first, cr_ref[pl.ds(2 * hp, 1), pl.ds(k0, tk)], cr_ref[pl.ds(2 * hp + 1, 1), pl.ds(k0, tk)])
            p = jnp.exp(_dot(q2, kblk, NT_DIMS) + ct - cs - lse)
            if lead is not None:
                p = jnp.where(ahead <= lead, p, 0.0)
            dp = _dot(do2, v_ref[pl.ds(k0, tk), :], NT_DIMS)
            ds = (p * (dp - delta)).astype(BF16)
            dkc_acc[pl.ds(k0, tk), :] += _dot(ds, q_and_ones, TN_DIMS)
            dv_acc[pl.ds(k0, tk), :] += _dot(p.astype(BF16), do2, TN_DIMS)
            return dq_acc + _dot(ds, kblk), rs + jnp.sum(ds.astype(F32), axis=1, keepdims=True)

        init = (jnp.zeros((2 * tq, LANES), F32), jnp.zeros((2 * tq, 1), F32))
        dq_acc, rs = _sweep(qi, tq, tk, step, init, leftward=False)
        dq_ref[...] = (_unstack(dq_acc, masks) * SCALE).astype(BF16)
        drs_ref[...] = _unstack(rs, masks)

        @pl.when(qi == nq - 1)
        def _():
            dk_ref[...] = dkc_acc[:, 0:LANES].astype(BF16)
            dcs_ref[...] = dkc_acc[:, LANES:2 * LANES]
            dv_ref[...] = dv_acc[...].astype(BF16)

    q_spec, k_spec, v_spec = _att_specs(seq, 0, tq)
    qb = _qblock_spec(seq, tq)
    return _call_behind(
        body, behind, name="fox_bwd", grid=(batch, N_PAIRS, nq),
        in_specs=[q_spec, k_spec, v_spec, qb, pl.BlockSpec((N_HEADS, seq), lambda b, hp, qi: (b, 0)), qb, qb, qb],
        out_specs=[qb, _kv_out_spec(seq), _kv_out_spec(seq), _kv_out_spec(seq), qb],
        out_shape=[jax.ShapeDtypeStruct((batch * seq, D_ATT), BF16)] * 3 + [jax.ShapeDtypeStruct((batch * seq, D_ATT), F32)] * 2,
        scratch_shapes=[pltpu.VMEM((seq, 2 * LANES), F32), pltpu.VMEM((seq, LANES), F32)],
        operands=(qkv, qkv, qkv, c_wide, c_row, o, do, lse_wide))


def _sb_logits(q2, kblk):
    z = _dot(q2, kblk, NT_DIMS)
    lsn = jnp.minimum(-z, 0.0) - jnp.log(1.0 + jnp.exp(-jnp.abs(z)))
    return lsn + z, lsn


def _sb_fwd(qkv, batch, seq, behind):
    tq, tk = SB_TILES
    nq = seq // tq

    def body(q_ref, k_ref, v_ref, o_ref, rt_ref):
        qi = pl.program_id(2)
        masks = _head_masks()
        ahead, _ = _stacked_ids(tq, tk)
        later = _tri2(tk, lambda r, c: r > c)
        q2 = _stack_heads(q_ref[...], masks) * SCALE

        def step(kb, carry, lead):
            run, acc = carry
            k0 = pl.multiple_of(kb * tk, tk)
            ls, lsn = _sb_logits(q2, k_ref[pl.ds(k0, tk), :])
            if lead is not None:
                lsn = jnp.where(ahead < lead, lsn, 0.0)
            w = jnp.exp(ls + _dot(_split2(lsn), later) + run)
            if lead is not None:
                w = jnp.where(ahead < lead, w, 0.0)
            return run + jnp.sum(lsn, axis=1, keepdims=True), acc + _dot(w.astype(BF16), v_ref[pl.ds(k0, tk), :])

        init = (jnp.zeros((2 * tq, 1), F32), jnp.zeros((2 * tq, LANES), F32))
        run, acc = _sweep(qi, tq, tk, step, init, leftward=True)
        o_ref[...] = _unstack(acc, masks).astype(BF16)
        rt_ref[...] = _unstack(run, masks)

    q_spec, k_spec, v_spec = _att_specs(seq, 3 * N_PAIRS, tq)
    qb = _qblock_spec(seq, tq)
    return _call_behind(
        body, behind, name="sb_fwd", grid=(batch, N_PAIRS, nq), in_specs=[q_spec, k_spec, v_spec], out_specs=[qb, qb],
        out_shape=[jax.ShapeDtypeStruct((batch * seq, D_ATT), BF16), jax.ShapeDtypeStruct((batch * seq, D_ATT), F32)],
        scratch_shapes=[], operands=(qkv, qkv, qkv))


def _sb_bwd(qkv, do, rt_wide, batch, seq, behind):
    tq, tk = SB_TILES
    nq = seq // tq

    def body(q_ref, k_ref, v_ref, do_ref, rt_ref, dq_ref, dk_ref, dv_ref, dk_acc, dv_acc):
        qi = pl.program_id(2)

        @pl.when(qi == 0)
        def _():
            dk_acc[...] = jnp.zeros_like(dk_acc)
            dv_acc[...] = jnp.zeros_like(dv_acc)

        masks = _head_masks()
        ahead, _ = _stacked_ids(tq, tk)
        later = _tri2(tk, lambda r, c: r > c)
        earlier = _tri2(tk, lambda r, c: r < c)
        q2 = _stack_heads(q_ref[...], masks) * SCALE
        do2 = _stack_heads(do_ref[...], masks)
        total = _stack_cols(rt_ref[...])

        def step(kb, carry, lead):
            pref, epre, dq_acc = carry
            k0 = pl.multiple_of(kb * tk, tk)
            kblk = k_ref[pl.ds(k0, tk), :]
            ls, lsn_all = _sb_logits(q2, kblk)
            lsn = lsn_all if lead is None else jnp.where(ahead < lead, lsn_all, 0.0)
            rs = jnp.sum(lsn, axis=1, keepdims=True)
            w = jnp.exp(ls + _dot(_split2(lsn), later) + (total - pref - rs))
            if lead is not None:
                w = jnp.where(ahead < lead, w, 0.0)
            e = w * _dot(do2, v_ref[pl.ds(k0, tk), :], NT_DIMS)
            before = _dot(_split2(e), earlier) + epre
            dz = e * jnp.exp(lsn_all) - jnp.exp(ls) * before
            if lead is not None:
                dz = jnp.where(ahead < lead, dz, 0.0)
            dz = dz.astype(BF16)
            dk_acc[pl.ds(k0, tk), :] += _dot(dz, q2, TN_DIMS)
            dv_acc[pl.ds(k0, tk), :] += _dot(w.astype(BF16), do2, TN_DIMS)
            return pref + rs, epre + jnp.sum(e, axis=1, keepdims=True), dq_acc + _dot(dz, kblk)

        init = (jnp.zeros((2 * tq, 1), F32), jnp.zeros((2 * tq, 1), F32), jnp.zeros((2 * tq, LANES), F32))
        dq_acc = _sweep(qi, tq, tk, step, init, leftward=False)[2]
        dq_ref[...] = (_unstack(dq_acc, masks) * SCALE).astype(BF16)

        @pl.when(qi == nq - 1)
        def _():
            dk_ref[...] = dk_acc[...].astype(BF16)
            dv_ref[...] = dv_acc[...].astype(BF16)

    q_spec, k_spec, v_spec = _att_specs(seq, 3 * N_PAIRS, tq)
    qb = _qblock_spec(seq, tq)
    return _call_behind(
        body, behind, name="sb_bwd", grid=(batch, N_PAIRS, nq), in_specs=[q_spec, k_spec, v_spec, qb, qb],
        out_specs=[qb, _kv_out_spec(seq), _kv_out_spec(seq)], out_shape=[jax.ShapeDtypeStruct((batch * seq, D_ATT), BF16)] * 3,
        scratch_shapes=[pltpu.VMEM((seq, LANES), F32), pltpu.VMEM((seq, LANES), F32)], operands=(qkv, qkv, qkv, do, rt_wide))


def _local_step(x, p, target, w, rest, vec, place):
    batch, seq, _ = x.shape
    t = batch * seq
    x = x.reshape(t, D_MODEL)
    target = target.reshape(t, D_MODEL)
    p = p.reshape(t, D_PLE)
    big = dict(tm=1024, tn=1024, tk=1024)

    h1 = _norm_fwd(x, vec["g_mix"], "norm_mix")
    qkv = _mm(h1, w["qkv"], mode="nn", name="proj_qkv", out_dtype=BF16, **big)
    gl = _mm(h1, w["gate"], mode="nn", name="proj_gate", **big)
    fl = _mm(h1, w["forget"], mode="nn", name="proj_forget", **big)
    c_wide, c_row = _fox_prep(fl, vec["b_forget"], batch, seq)
    o_fox, lse_wide = _fox_fwd(qkv, c_wide, c_row, batch, seq)
    (o_sb, rt_wide), gathered = _sb_fwd(qkv, batch, seq, rest)
    w = dict(w, **_rest_weights(dict(zip(EARLY, gathered))))
    of = _mm(o_fox, w["branch_fox"], mode="nn", name="branch_fox", col_shards=True, **big)
    os_ = _mm(o_sb, w["branch_sb"], mode="nn", name="branch_sb", col_shards=True, **big)
    merged = _gate_fwd(gl, w["b_gate"], of, os_)
    x1, h2 = _mm_res_norm(merged, w["out"], x, vec["g_mlp"], "proj_out_norm")
    ar = _mm(h2, w["up"], mode="nn", name="mlp_up", out_dtype=BF16, epi=lambda acc, _: jnp.maximum(acc, 0.0),
             col_shards=True, **big)
    x2, h3 = _mm_res_norm(ar, w["down"], x1, vec["g_ple"], "mlp_down_norm", a_fn=_relu2)
    gpre = _mm(h3, w["ple_gate"], mode="nn", name="ple_gate", **big)
    pe = _mm(p, w["ple"], mode="nn", name="ple_embed", col_shards=True, **big)

    dx3, dpre, dpe, dg_final, loss = _head_and_loss(x2, gpre, pe, vec["g_final"], target)
    gw = {}
    gw["ple"] = _mm(p, dpe, mode="tn", name="d_w_ple", col_shards=True, **big)
    gw["ple_gate"] = _mm(h3, dpre, mode="tn", name="d_w_ple_gate", **big)
    dx2, dx2b, dg_ple = _mm_norm_bwd([(dpre, w["ple_gate"])], None, x2, vec["g_ple"], dx3, "d_h_ple_norm_bwd")
    gw["down"] = _mm(ar, dx2b, mode="tn", name="d_w_down", a_fn=_relu2, **big)
    da = _mm(dx2b, w["down"], mode="nt", name="d_act", out_dtype=BF16,
             epi=lambda acc, r: acc * (2.0 * r.astype(F32)), extra=ar, **big)
    gw["up"] = _mm(h2, da, mode="tn", name="d_w_up", col_shards=True, **big)
    dx1, dx1b, dg_mlp = _mm_norm_bwd([(da, w["up"])], None, x1, vec["g_mlp"], dx2, "d_h_mlp_norm_bwd")
    gw["out"] = _mm(merged, dx1b, mode="tn", name="d_w_out", **big)
    dmerged = _mm(dx1b, w["out"], mode="nt", name="d_merged", **big)
    dof, dos, dgla, dglb, gw["b_gate"] = _gate_bwd(gl, w["b_gate"], of, os_, dmerged)
    gw["branch_fox"] = _mm(o_fox, dof, mode="tn", name="d_w_branch_fox", col_shards=True, **big)
    gw["branch_sb"] = _mm(o_sb, dos, mode="tn", name="d_w_branch_sb", col_shards=True, **big)
    do_fox = _mm(dof, w["branch_fox"], mode="nt", name="d_o_fox", out_dtype=BF16, col_shards=True, **big)
    do_sb = _mm(dos, w["branch_sb"], mode="nt", name="d_o_sb", out_dtype=BF16, col_shards=True, **big)
    early = _early_slots(gw)
    early = [early[n] for n in EARLY]
    (dq_a, dk_a, dv_a, dcs_wide, drs_wide), received = _fox_bwd(qkv, c_wide, c_row, o_fox, do_fox, lse_wide, batch, seq,
                                                                _swap_halves(early))
    sums = [_sum_sibling(place, s, r, "sum_sibling_" + n) for s, r, n in zip(early, received, EARLY)]
    (dq_b, dk_b, dv_b), others = _sb_bwd(qkv, do_sb, rt_wide, batch, seq, _exchange_chips(sums))
    mine = [_sum_chips(place, s, r, o, "sum_chips_" + n) for s, r, o, n in zip(early, received, others, EARLY)]
    dfl, db_forget = _fox_post(dcs_wide, drs_wide, fl, vec["b_forget"], batch, seq)
    dqkv = jnp.concatenate([dq_a, dk_a, dv_a, dq_b, dk_b, dv_b], axis=1)
    dgl = jnp.concatenate([dgla, dglb], axis=1)
    gw["qkv"], theirs = _mm(dqkv, h1, mode="tn", name="d_w_qkv", behind=_share_halves(mine), **big)
    reduced = dict(zip(EARLY, zip(mine, theirs)))
    gw["gate"] = _mm(dgl, h1, mode="tn", name="d_w_gate", **big)
    gw["forget"] = _mm(dfl, h1, mode="tn", name="d_w_forget", **big)
    late = [_w_in_slots(gw)]
    dh1, received = _mm(dqkv, w["qkv"], mode="nt", name="d_h_qkv", behind=_swap_halves(late), **big)
    sums = [_sum_sibling(place, late[0], received[0], "sum_sibling_w_in")]
    (grad_x, _, dg_mix), others = _mm_norm_bwd([(dgl, w["gate"]), (dfl, w["forget"])], dh1, x, vec["g_mix"], dx1,
                                               "d_h_gate_norm_bwd", behind=_exchange_chips(sums))
    mine = [_sum_chips(place, late[0], received[0], others[0], "sum_chips_w_in")]
    reduced["w_in"] = (mine[0], _run_exchange(_share_halves(mine), "reduce_share_w_in")[0])
    gvec = {"g_mix": dg_mix, "b_forget": db_forget[:, 0:N_HEADS], "g_mlp": dg_mlp, "g_ple": dg_ple,
            "g_final": dg_final, "b_gate": gw["b_gate"]}
    return loss, grad_x.reshape(batch, seq, D_MODEL), reduced, gvec


ANY = pl.BlockSpec(memory_space=pl.ANY)
SHARDED = ("w_in", "w_branch_fox", "w_branch_sb", "w_out", "w_up", "w_down", "w_ple_gate", "w_ple")
ROW_ALIGN = 16
F32_ROWS = 8


def _place():
    return lax.axis_index("x"), lax.axis_index("y"), lax.axis_index("c")


def _other_chips(x, y):
    return [(1 - x, y), (x, 1 - y), (1 - x, 1 - y)]


def _half(ref, h):
    r = ref.shape[0] // 2
    assert r % ROW_ALIGN == 0
    return ref.at[pl.ds(pl.multiple_of(h * r, ROW_ALIGN), r)]


def _remote(src, dst, sems, idx, to):
    send_sems, recv_sems = sems
    return pltpu.make_async_remote_copy(src_ref=src, dst_ref=dst, send_sem=send_sems.at[idx], recv_sem=recv_sems.at[idx],
                                        device_id=to, device_id_type=MESH)


class _Exchange:
    def __init__(self, operands, out_shapes, sem_shape, start, finish):
        self.operands, self.out_shapes, self.sem_shape, self.start, self.finish = operands, out_shapes, sem_shape, start, finish

    def scratch(self):
        return [pltpu.SemaphoreType.DMA(self.sem_shape), pltpu.SemaphoreType.DMA(self.sem_shape)]


def _run_exchange(ex, name):
    n = len(ex.operands)

    def body(*refs):
        ex.start(refs[:n], refs[n:2 * n], refs[2 * n:])
        ex.finish(refs[:n], refs[n:2 * n], refs[2 * n:])

    return pl.pallas_call(body, name=name, in_specs=[ANY] * n, out_specs=[ANY] * n, out_shape=ex.out_shapes,
                          scratch_shapes=ex.scratch())(*ex.operands)


def _call_behind(body, ex, *, name, grid, in_specs, out_specs, out_shape, scratch_shapes, operands):
    n_in, n_out, nx = len(in_specs), len(out_specs), len(ex.operands)

    def wrapped(*refs):
        ins, x_in = refs[:n_in], refs[n_in:n_in + nx]
        outs, x_out = refs[n_in + nx:n_in + nx + n_out], refs[n_in + nx + n_out:n_in + 2 * nx + n_out]
        scratch, sems = refs[n_in + 2 * nx + n_out:-2], refs[-2:]
        first, last = None, None
        for d, steps in enumerate(grid):
            at_start, at_end = pl.program_id(d) == 0, pl.program_id(d) == steps - 1
            first = at_start if first is None else first & at_start
            last = at_end if last is None else last & at_end

        @pl.when(first)
        def _():
            ex.start(x_in, x_out, sems)

        body(*ins, *outs, *scratch)

        @pl.when(last)
        def _():
            ex.finish(x_in, x_out, sems)

    res = pl.pallas_call(
        wrapped, name=name, grid=grid, in_specs=list(in_specs) + [ANY] * nx, out_specs=list(out_specs) + [ANY] * nx,
        out_shape=list(out_shape) + list(ex.out_shapes), scratch_shapes=list(scratch_shapes) + ex.scratch(),
        compiler_params=_cparams(("arbitrary",) * len(grid)),
    )(*operands, *ex.operands)
    return res[:n_out], res[n_out:]


def _gather_weights(shards):
    n = len(shards)

    def first_copies(src, out, sems):
        x, y, c = _place()
        me = 2 * x + y
        copies = [_remote(_half(src[t], c), _half(out[t].at[me], c), sems, (t, k), (px, py, c))
                  for t in range(n) for k, (px, py) in enumerate(_other_chips(x, y))]
        return copies + [_remote(src[t], out[t].at[me], sems, (t, 3), (x, y, 1 - c)) for t in range(n)]

    def start(src, out, sems):
        for cp in first_copies(src, out, sems):
            cp.start()

    def finish(src, out, sems):
        x, y, c = _place()
        me = 2 * x + y
        sibling = (x, y, 1 - c)
        chips = _other_chips(x, y)
        passes = []
        for t in range(n):
            for k, (px, py) in enumerate(chips):
                landed = _half(out[t].at[2 * px + py], c)
                _remote(landed, landed, sems, (t, k), (px, py, c)).wait_recv()
                passes.append(_remote(landed, landed, sems, (t, 4 + k), sibling))
                passes[-1].start()
        for t in range(n):
            _remote(src[t], out[t].at[me], sems, (t, 3), sibling).wait_recv()
            for k, (px, py) in enumerate(chips):
                passed = _half(out[t].at[2 * px + py], 1 - c)
                _remote(passed, passed, sems, (t, 4 + k), sibling).wait_recv()
        for cp in first_copies(src, out, sems) + passes:
            cp.wait_send()

    return _Exchange(shards, [jax.ShapeDtypeStruct((N_CHIPS,) + s.shape, s.dtype) for s in shards], (n, 7), start, finish)


def _simple_exchange(operands, out_shapes, copies):
    def start(src, out, sems):
        for cp in copies(src, out, sems):
            cp.start()

    def finish(src, out, sems):
        for cp in copies(src, out, sems):
            cp.wait_recv()
        for cp in copies(src, out, sems):
            cp.wait_send()

    return _Exchange(operands, out_shapes, (len(operands),), start, finish)


def _swap_halves(slots):
    def copies(src, out, sems):
        x, y, c = _place()
        res = []
        for t in range(len(slots)):
            r = src[t].shape[1] // 2
            rows = pl.ds(pl.multiple_of((1 - c) * r, F32_ROWS), r)
            res.append(_remote(src[t].at[:, rows], out[t], sems, t, (x, y, 1 - c)))
        return res

    return _simple_exchange(slots, [jax.ShapeDtypeStruct((N_CHIPS, s.shape[1] // 2, s.shape[2]), s.dtype) for s in slots], copies)


def _exchange_chips(sums):
    n = len(sums)

    def copies(src, out, sems):
        x, y, c = _place()
        return [_remote(src[t].at[2 * px + py], out[t].at[k], sems, (t, k), (px, py, c))
                for t in range(n) for k, (px, py) in enumerate(_other_chips(x, y))]

    def start(src, out, sems):
        for cp in copies(src, out, sems):
            cp.start()

    def finish(src, out, sems):
        for cp in copies(src, out, sems):
            cp.wait_recv()
        for cp in copies(src, out, sems):
            cp.wait_send()

    return _Exchange(sums, [jax.ShapeDtypeStruct((3,) + s.shape[1:], s.dtype) for s in sums], (n, 3), start, finish)


def _share_halves(mine):
    def copies(src, out, sems):
        x, y, c = _place()
        return [_remote(src[t], out[t], sems, t, (x, y, 1 - c)) for t in range(len(mine))]

    return _simple_exchange(mine, [jax.ShapeDtypeStruct(s.shape, s.dtype) for s in mine], copies)


def _half_tile(rows):
    return 256 if rows % 256 == 0 else rows


def _sum_sibling(place, slot, received, name):
    n, rows2, cols = slot.shape
    rows = rows2 // 2
    tile = _half_tile(rows)
    nb = rows // tile

    def body(place_ref, a_ref, b_ref, o_ref):
        o_ref[...] = (a_ref[...] + b_ref[...]).astype(BF16)

    return pl.pallas_call(
        body, name=name, out_shape=jax.ShapeDtypeStruct((n, rows, cols), BF16),
        grid_spec=pltpu.PrefetchScalarGridSpec(
            num_scalar_prefetch=1, grid=(n, nb),
            in_specs=[pl.BlockSpec((None, tile, cols), lambda j, i, pr: (j, pr[1] * nb + i, 0)),
                      pl.BlockSpec((None, tile, cols), lambda j, i, pr: (j, i, 0))],
            out_specs=pl.BlockSpec((None, tile, cols), lambda j, i, pr: (j, i, 0))),
        compiler_params=_cparams(("parallel", "parallel")),
    )(place, slot, received)


def _sum_chips(place, slot, received, others, name):
    _, rows2, cols = slot.shape
    rows = rows2 // 2
    tile = _half_tile(rows)
    nb = rows // tile

    def body(place_ref, a_ref, b_ref, p_ref, o_ref):
        own = a_ref[...] + b_ref[...]
        o_ref[...] = ((own + p_ref[0].astype(F32)) + p_ref[1].astype(F32)) + p_ref[2].astype(F32)

    return pl.pallas_call(
        body, name=name, out_shape=jax.ShapeDtypeStruct((rows, cols), F32),
        grid_spec=pltpu.PrefetchScalarGridSpec(
            num_scalar_prefetch=1, grid=(nb,),
            in_specs=[pl.BlockSpec((None, tile, cols), lambda i, pr: (pr[0], pr[1] * nb + i, 0)),
                      pl.BlockSpec((None, tile, cols), lambda i, pr: (pr[0], i, 0)),
                      pl.BlockSpec((3, tile, cols), lambda i, pr: (0, i, 0))],
            out_specs=pl.BlockSpec((tile, cols), lambda i, pr: (i, 0))),
        compiler_params=_cparams(("parallel",)),
    )(place, slot, received, others)


N_DEVICES = 8


def _sum_devices(block, name):
    def body(v_ref, o_ref, land_ref, send_sems, recv_sems):
        x, y, c = _place()
        me = 4 * x + 2 * y + c
        copies = []
        for mask in range(1, N_DEVICES):
            peer = (x ^ (mask >> 2), y ^ ((mask >> 1) & 1), c ^ (mask & 1))
            copies.append(pltpu.make_async_remote_copy(src_ref=v_ref, dst_ref=land_ref.at[me], send_sem=send_sems.at[mask - 1],
                                                       recv_sem=recv_sems.at[mask - 1], device_id=peer, device_id_type=MESH))
        for cp in copies:
            cp.start()
        land_ref[me] = v_ref[...]
        for cp in copies:
            cp.wait_recv()
        total = land_ref[0]
        for d in range(1, N_DEVICES):
            total = total + land_ref[d]
        o_ref[...] = total
        for cp in copies:
            cp.wait_send()

    vmem = pl.BlockSpec(memory_space=pltpu.VMEM)
    return pl.pallas_call(
        body, name=name, in_specs=[vmem], out_specs=vmem, out_shape=jax.ShapeDtypeStruct(block.shape, F32),
        scratch_shapes=[pltpu.VMEM((N_DEVICES,) + block.shape, F32), pltpu.SemaphoreType.DMA((N_DEVICES - 1,)),
                        pltpu.SemaphoreType.DMA((N_DEVICES - 1,))],
    )(block)


def _vec_block(g_mix, g_mlp, g_ple, g_final, b_forget, b_gate_rows, last=None):
    pad = lambda a: jnp.concatenate([a, jnp.zeros((a.shape[0], D_MODEL - a.shape[1]), F32)], axis=1)
    last = jnp.zeros((1, 0), F32) if last is None else last
    return jnp.concatenate([g_mix, g_mlp, g_ple, g_final.reshape(1, D_MODEL), pad(b_forget), pad(b_gate_rows), pad(last)],
                           axis=0)


def _adam_math(w, g, m, v):
    m_new = ADAM_B1 * m + (1.0 - ADAM_B1) * g
    v_new = ADAM_B2 * v + (1.0 - ADAM_B2) * (g * g)
    m_hat = m_new / (1.0 - ADAM_B1 ** ADAM_STEP)
    v_hat = v_new / (1.0 - ADAM_B2 ** ADAM_STEP)
    return -ADAM_LR * (m_hat / (jnp.sqrt(v_hat) + ADAM_EPS) + ADAM_WD * w), m_new, v_new


def _adamw_halves(place, w, m, v, g_mine, g_theirs, name):
    rows2, cols = w.shape
    rows = rows2 // 2
    tile = _half_tile(rows)
    nb = rows // tile

    def body(place_ref, w_ref, m_ref, v_ref, gm_ref, gt_ref, g_ref, d_ref, nm_ref, nv_ref):
        g = jnp.where(pl.program_id(0) == 0, gm_ref[...], gt_ref[...])
        g_ref[...] = g
        d_ref[...], nm_ref[...], nv_ref[...] = _adam_math(w_ref[...], g, m_ref[...], v_ref[...])

    whole = pl.BlockSpec((tile, cols), lambda s, i, pr: ((pr[1] + s - 2 * pr[1] * s) * nb + i, 0))
    half = pl.BlockSpec((tile, cols), lambda s, i, pr: (i, 0))
    return pl.pallas_call(
        body, name=name, out_shape=[jax.ShapeDtypeStruct((rows2, cols), F32)] * 4,
        grid_spec=pltpu.PrefetchScalarGridSpec(num_scalar_prefetch=1, grid=(2, nb), in_specs=[whole] * 3 + [half] * 2,
                                               out_specs=[whole] * 4),
        compiler_params=_cparams(("parallel", "parallel")),
    )(place, w, m, v, g_mine, g_theirs)


def _adamw_vec(w, g, m, v):
    def body(w_ref, g_ref, m_ref, v_ref, d_ref, nm_ref, nv_ref):
        d_ref[...], nm_ref[...], nv_ref[...] = _adam_math(w_ref[...], g_ref[...], m_ref[...], v_ref[...])

    return pl.pallas_call(body, name="adamw_vectors", out_shape=[jax.ShapeDtypeStruct(w.shape, F32)] * 3)(w, g, m, v)


WEIGHT_NAMES = ("g_mix", "w_in", "b_forget", "b_gate", "w_branch_fox", "w_branch_sb", "w_out", "g_mlp", "w_up", "w_down",
                "g_ple", "w_ple_gate", "w_ple", "g_final")
W_IN_SHARD = D_IN // N_CHIPS
Q_END, F_END, B_END = 3 * D_ATT, 3 * D_ATT + N_HEADS, 6 * D_ATT + N_HEADS
GATE_SHARD = D_MODEL // N_CHIPS


def _join_cols(slots):
    return jnp.transpose(slots, (1, 0, 2)).reshape(slots.shape[1], N_CHIPS * slots.shape[2])


LATE = SHARDED[:1]
EARLY = SHARDED[1:]


def _first_weights(w_in_slots, b_gate):
    w_in = _join_cols(w_in_slots)
    forget = jnp.concatenate([w_in[:, Q_END:F_END], jnp.zeros((D_MODEL, F_PAD - N_HEADS), BF16)], axis=1)
    return {"qkv": jnp.concatenate([w_in[:, :Q_END], w_in[:, F_END:B_END]], axis=1), "gate": w_in[:, B_END:], "forget": forget,
            "b_gate": b_gate}


def _rest_weights(gathered):
    rows = lambda a: a.reshape(N_CHIPS * a.shape[1], a.shape[2])
    return {"branch_fox": gathered["w_branch_fox"], "branch_sb": gathered["w_branch_sb"], "out": rows(gathered["w_out"]),
            "up": gathered["w_up"], "down": rows(gathered["w_down"]), "ple_gate": rows(gathered["w_ple_gate"]),
            "ple": gathered["w_ple"]}


def _early_slots(gw):
    rows = lambda a: a.reshape(N_CHIPS, a.shape[0] // N_CHIPS, a.shape[1])
    return {"w_branch_fox": gw["branch_fox"], "w_branch_sb": gw["branch_sb"], "w_out": rows(gw["out"]), "w_up": gw["up"],
            "w_down": rows(gw["down"]), "w_ple_gate": rows(gw["ple_gate"]), "w_ple": gw["ple"]}


W_IN_FLAT = (W_IN_SHARD * D_MODEL // LANES, LANES)


def _w_in_slots(gw):
    g_t = jnp.concatenate([gw["qkv"][:Q_END], gw["forget"][:N_HEADS], gw["qkv"][Q_END:], gw["gate"]], axis=0)
    return g_t.reshape((N_CHIPS,) + W_IN_FLAT)


def _flat(a):
    return jnp.transpose(a, (2, 0, 1)).reshape(W_IN_FLAT)


def _unflat(a):
    return jnp.transpose(a.reshape(W_IN_SHARD, 1, D_MODEL), (1, 2, 0))


def kernel(x, p, g_mix, w_in, b_forget, b_gate, w_branch_fox, w_branch_sb, w_out, g_mlp, w_up, w_down, g_ple, w_ple_gate, w_ple, g_final, loss_target, m_g_mix, m_w_in, m_b_forget, m_b_gate, m_w_branch_fox, m_w_branch_sb, m_w_out, m_g_mlp, m_w_up, m_w_down, m_g_ple, m_w_ple_gate, m_w_ple, m_g_final, v_g_mix, v_w_in, v_b_forget, v_b_gate, v_w_branch_fox, v_w_branch_sb, v_w_out, v_g_mlp, v_w_up, v_w_down, v_g_ple, v_w_ple_gate, v_w_ple, v_g_final):
    weights = dict(g_mix=g_mix, w_in=w_in, b_forget=b_forget, b_gate=b_gate, w_branch_fox=w_branch_fox,
                   w_branch_sb=w_branch_sb, w_out=w_out, g_mlp=g_mlp, w_up=w_up, w_down=w_down, g_ple=g_ple,
                   w_ple_gate=w_ple_gate, w_ple=w_ple, g_final=g_final)
    first = dict(g_mix=m_g_mix, w_in=m_w_in, b_forget=m_b_forget, b_gate=m_b_gate, w_branch_fox=m_w_branch_fox,
                 w_branch_sb=m_w_branch_sb, w_out=m_w_out, g_mlp=m_g_mlp, w_up=m_w_up, w_down=m_w_down, g_ple=m_g_ple,
                 w_ple_gate=m_w_ple_gate, w_ple=m_w_ple, g_final=m_g_final)
    second = dict(g_mix=v_g_mix, w_in=v_w_in, b_forget=v_b_forget, b_gate=v_b_gate, w_branch_fox=v_w_branch_fox,
                  w_branch_sb=v_w_branch_sb, w_out=v_w_out, g_mlp=v_g_mlp, w_up=v_w_up, w_down=v_w_down, g_ple=v_g_ple,
                  w_ple_gate=v_w_ple_gate, w_ple=v_w_ple, g_final=v_g_final)
    cx, cy, cc = _place()
    chip = 2 * cx + cy
    place = jnp.stack([chip, cc]).astype(jnp.int32)
    col0 = chip * GATE_SHARD

    (w_in_slots,) = _run_exchange(_gather_weights([weights[n][0].astype(BF16) for n in LATE]), "gather_w_in")
    rest = _gather_weights([weights[n][0].astype(BF16) for n in EARLY])
    gate_rows = lax.dynamic_update_slice(jnp.zeros((2, D_MODEL), F32), b_gate[0] * (cc == 0).astype(F32), (0, col0))
    zero_row = jnp.zeros((1, D_MODEL), F32)
    b_gate_whole = _sum_devices(_vec_block(zero_row, zero_row, zero_row, zero_row[0], zero_row[:, :N_HEADS], gate_rows),
                                "gather_b_gate")[5:7]
    vec = {"g_mix": g_mix, "b_forget": jnp.concatenate([b_forget, jnp.zeros((1, F_PAD - N_HEADS), F32)], axis=1),
           "g_mlp": g_mlp, "g_ple": g_ple, "g_final": g_final.reshape(1, D_MODEL)}

    loss, grad_x, reduced, gvec = _local_step(x, p[0], loss_target, _first_weights(w_in_slots, b_gate_whole), rest, vec,
                                              place)

    out = {}
    for n in EARLY:
        g_mine, g_theirs = reduced[n]
        res = _adamw_halves(place, weights[n][0], first[n][0], second[n][0], g_mine, g_theirs, "adamw_" + n)
        out[n] = [r[None] for r in res]
    g_mine, g_theirs = reduced["w_in"]
    out["w_in"] = [_unflat(r) for r in _adamw_halves(place, _flat(w_in), _flat(m_w_in), _flat(v_w_in), g_mine, g_theirs,
                                                      "adamw_w_in")]

    g_block = _sum_devices(_vec_block(gvec["g_mix"], gvec["g_mlp"], gvec["g_ple"], gvec["g_final"][0], gvec["b_forget"],
                                      gvec["b_gate"], loss), "reduce_vectors")
    loss = g_block[7, 0]
    g_gate = lax.dynamic_slice(g_block[5:7], (0, col0), (2, GATE_SHARD))
    blocks = [_vec_block(d["g_mix"], d["g_mlp"], d["g_ple"], d["g_final"], d["b_forget"], d["b_gate"][0])
              for d in (weights, first, second)]
    g_rows = jnp.concatenate([g_block[0:5], jnp.concatenate([g_gate, jnp.zeros((2, D_MODEL - GATE_SHARD), F32)], axis=1),
                              jnp.zeros((1, D_MODEL), F32)], axis=0)
    res = (g_rows,) + tuple(_adamw_vec(blocks[0], g_rows, blocks[1], blocks[2]))
    out["g_mix"] = [r[0:1] for r in res]
    out["g_mlp"] = [r[1:2] for r in res]
    out["g_ple"] = [r[2:3] for r in res]
    out["g_final"] = [r[3] for r in res]
    out["b_forget"] = [r[4:5, :N_HEADS] for r in res]
    out["b_gate"] = [r[5:7, :GATE_SHARD][None] for r in res]
    return (loss, grad_x, *[out[n][0] for n in WEIGHT_NAMES], *[out[n][1] for n in WEIGHT_NAMES],
            *[out[n][2] for n in WEIGHT_NAMES], *[out[n][3] for n in WEIGHT_NAMES])
```

```python
import jax
import jax.numpy as jnp
from jax import lax
from jax.experimental import pallas as pl
from jax.experimental.pallas import tpu as pltpu

F32 = jnp.float32
BF16 = jnp.bfloat16

D_MODEL = 1024
HEAD_DIM = 64
N_HEADS = 8
D_ATT = N_HEADS * HEAD_DIM
D_FF = 4 * D_MODEL
D_PLE = 256
D_IN = 6 * D_ATT + N_HEADS + 2 * D_MODEL
F_PAD = 128
EPS = 1e-6
SCALE = HEAD_DIM ** -0.5
N_CHIPS = 4
LANES = 128
ATT_BLOCK = 256
FOX_TILES = (512, 512)
SB_TILES = (512, 256)
NEG = -1e30

ADAM_LR = 0.001
ADAM_B1 = 0.9
ADAM_B2 = 0.999
ADAM_EPS = 1e-08
ADAM_WD = 0.01
ADAM_STEP = 10

VMEM_LIMIT = 56 * 1024 * 1024

MESH = pl.DeviceIdType.MESH


def _cparams(sem=None):
    return pltpu.CompilerParams(dimension_semantics=sem, vmem_limit_bytes=VMEM_LIMIT)


def _relu2(t):
    t = t.astype(F32)
    return t * t


_DIMS = {"nn": (((1,), (0,)), ((), ())), "nt": (((1,), (1,)), ((), ())), "tn": (((0,), (0,)), ((), ()))}
NT_DIMS = _DIMS["nt"]
TN_DIMS = _DIMS["tn"]


def _mm(a, b, *, mode, name, out_dtype=F32, tm=512, tn=512, tk=512, add=None, a_fn=None, epi=None, extra=None,
        col_shards=False, behind=None):
    if mode == "nn":
        (m, k), n = a.shape, b.shape[-1]
    elif mode == "nt":
        (m, k), n = a.shape, b.shape[-2]
    else:
        (k, m), n = a.shape, b.shape[1]
    shard = None
    if col_shards:
        if mode == "nn":
            shard, n = n, N_CHIPS * n
            tn = min(tn, shard)
        elif mode == "nt":
            shard = b.shape[-1]
            tk = min(tk, shard)
        else:
            shard = n // N_CHIPS
            tn = min(tn, shard)
    tm, tn, tk = min(tm, m), min(tn, n), min(tk, k)
    assert m % tm == 0 and n % tn == 0 and k % tk == 0, (name, m, n, k)
    nk = k // tk
    a_spec = {"nn": pl.BlockSpec((tm, tk), lambda i, j, kk: (i, kk)),
              "nt": pl.BlockSpec((tm, tk), lambda i, j, kk: (i, kk)),
              "tn": pl.BlockSpec((tk, tm), lambda i, j, kk: (kk, i))}[mode]
    b_spec = {"nn": pl.BlockSpec((tk, tn), lambda i, j, kk: (kk, j)),
              "nt": pl.BlockSpec((tn, tk), lambda i, j, kk: (j, kk)),
              "tn": pl.BlockSpec((tk, tn), lambda i, j, kk: (kk, j))}[mode]
    o_spec = pl.BlockSpec((tm, tn), lambda i, j, kk: (i, j))
    out_shape = (m, n)
    if col_shards and mode == "nn":
        per = shard // tn
        b_spec = pl.BlockSpec((None, tk, tn), lambda i, j, kk: (j // per, kk, j % per))
    elif col_shards and mode == "nt":
        per = shard // tk
        b_spec = pl.BlockSpec((None, tn, tk), lambda i, j, kk: (kk // per, j, kk % per))
    elif col_shards:
        assert add is None and extra is None
        per = shard // tn
        o_spec = pl.BlockSpec((None, tm, tn), lambda i, j, kk: (j // per, i, j % per))
        out_shape = (N_CHIPS, m, shard)
    operands, in_specs = [a, b], [a_spec, b_spec]
    third = add if add is not None else extra
    if third is not None:
        operands.append(third)
        in_specs.append(o_spec)

    def body(*refs):
        a_ref, b_ref = refs[0], refs[1]
        t_ref = refs[2] if third is not None else None
        o_ref = refs[3] if third is not None else refs[2]
        acc_ref = refs[-1] if nk > 1 else None
        at = a_ref[...]
        if a_fn is not None:
            at = a_fn(at)
        part = lax.dot_general(at.astype(BF16), b_ref[...].astype(BF16), _DIMS[mode], preferred_element_type=F32)

        def finish(acc):
            if epi is not None:
                acc = epi(acc, None if t_ref is None else t_ref[...])
            elif add is not None:
                acc = acc + t_ref[...].astype(F32)
            o_ref[...] = acc.astype(o_ref.dtype)

        if nk == 1:
            finish(part)
        else:
            kk = pl.program_id(2)

            @pl.when(kk == 0)
            def _():
                acc_ref[...] = part

            @pl.when(kk > 0)
            def _():
                acc_ref[...] += part

            @pl.when(kk == nk - 1)
            def _():
                finish(acc_ref[...])

    call = dict(name=name, grid=(m // tm, n // tn, nk), in_specs=in_specs,
                scratch_shapes=[pltpu.VMEM((tm, tn), F32)] if nk > 1 else [])
    if behind is not None:
        (res,), exchanged = _call_behind(body, behind, out_specs=[o_spec], out_shape=[jax.ShapeDtypeStruct(out_shape, out_dtype)],
                                         operands=operands, **call)
        return res, exchanged
    return pl.pallas_call(body, out_specs=o_spec, out_shape=jax.ShapeDtypeStruct(out_shape, out_dtype),
                          compiler_params=_cparams(("parallel", "parallel", "arbitrary")), **call)(*operands)


ROW_TILE = 512


def _row_spec(width=D_MODEL, rows=ROW_TILE):
    return pl.BlockSpec((rows, width), lambda i: (i, 0))


def _vec_spec(rows=1, width=D_MODEL):
    return pl.BlockSpec((rows, width), lambda i: (0, 0))


def _xhat(x):
    r = lax.rsqrt(jnp.mean(x * x, axis=-1, keepdims=True) + EPS)
    return x * r, r


def _rms_bwd_rows(dh, x, g):
    xh, r = _xhat(x)
    dxh = dh * g
    dx = r * (dxh - xh * jnp.mean(dxh * xh, axis=-1, keepdims=True))
    return dx, jnp.sum(dh * xh, axis=0, keepdims=True)


def _norm_fwd(x, g, name):
    t = x.shape[0]

    def body(x_ref, g_ref, h_ref):
        xh, _ = _xhat(x_ref[...])
        h_ref[...] = (xh * g_ref[...]).astype(BF16)

    return pl.pallas_call(
        body, name=name, grid=(t // ROW_TILE,), in_specs=[_row_spec(), _vec_spec()], out_specs=_row_spec(),
        out_shape=jax.ShapeDtypeStruct((t, D_MODEL), BF16), compiler_params=_cparams(("parallel",)),
    )(x, g)


def _norm_bwd(x, g, dh, dres, name):
    t = x.shape[0]

    def body(x_ref, g_ref, dh_ref, dres_ref, dx_ref, dxb_ref, dg_ref):
        dx, dg = _rms_bwd_rows(dh_ref[...], x_ref[...], g_ref[...])
        dx = dx + dres_ref[...]
        dx_ref[...] = dx
        dxb_ref[...] = dx.astype(BF16)

        @pl.when(pl.program_id(0) == 0)
        def _():
            dg_ref[...] = jnp.zeros_like(dg_ref)

        dg_ref[...] += dg

    return pl.pallas_call(
        body, name=name, grid=(t // ROW_TILE,),
        in_specs=[_row_spec(), _vec_spec(), _row_spec(), _row_spec()],
        out_specs=[_row_spec(), _row_spec(), _vec_spec()],
        out_shape=[jax.ShapeDtypeStruct((t, D_MODEL), F32), jax.ShapeDtypeStruct((t, D_MODEL), BF16),
                   jax.ShapeDtypeStruct((1, D_MODEL), F32)],
        compiler_params=_cparams(("arbitrary",)),
    )(x, g, dh, dres)


def _mm_res_norm(a, b, res, g, name, a_fn=None):
    t, k = a.shape

    def body(a_ref, b_ref, res_ref, g_ref, x_ref, h_ref):
        at = a_ref[...] if a_fn is None else a_fn(a_ref[...])
        x_new = res_ref[...] + _dot(at.astype(BF16), b_ref[...])
        x_ref[...] = x_new
        h_ref[...] = (_xhat(x_new)[0] * g_ref[...]).astype(BF16)

    return pl.pallas_call(
        body, name=name, grid=(t // ROW_TILE,),
        in_specs=[pl.BlockSpec((ROW_TILE, k), lambda i: (i, 0)), pl.BlockSpec(b.shape, lambda i: (0, 0)), _row_spec(), _vec_spec()],
        out_specs=[_row_spec(), _row_spec()],
        out_shape=[jax.ShapeDtypeStruct((t, D_MODEL), F32), jax.ShapeDtypeStruct((t, D_MODEL), BF16)],
        compiler_params=_cparams(("parallel",)),
    )(a, b, res, g)


def _mm_norm_bwd(pairs, dh_first, x, g, dres, name, behind=None):
    t = x.shape[0]
    operands, in_specs = [], []
    for a, b in pairs:
        if b.ndim == 3:
            for j in range(b.shape[0]):
                operands += [a, b]
                in_specs += [pl.BlockSpec((ROW_TILE, b.shape[2]), lambda i, j=j: (i, j)),
                             pl.BlockSpec((None, D_MODEL, b.shape[2]), lambda i, j=j: (j, 0, 0))]
        else:
            operands += [a, b]
            in_specs += [pl.BlockSpec((ROW_TILE, a.shape[1]), lambda i: (i, 0)), pl.BlockSpec(b.shape, lambda i: (0, 0))]
    n_mm = len(operands)
    operands += [x, g, dres] + ([] if dh_first is None else [dh_first])
    in_specs += [_row_spec(), _vec_spec(), _row_spec()] + ([] if dh_first is None else [_row_spec()])

    def body(*refs):
        x_ref, g_ref, dres_ref = refs[n_mm:n_mm + 3]
        dx_ref, dxb_ref, dg_ref = refs[-3:]
        dh = 0.0 if dh_first is None else refs[n_mm + 3][...]
        for k in range(0, n_mm, 2):
            dh = dh + lax.dot_general(refs[k][...].astype(BF16), refs[k + 1][...].astype(BF16), NT_DIMS,
                                      preferred_element_type=F32)
        dx, dg = _rms_bwd_rows(dh, x_ref[...], g_ref[...])
        dx = dx + dres_ref[...]
        dx_ref[...] = dx
        dxb_ref[...] = dx.astype(BF16)

        @pl.when(pl.program_id(0) == 0)
        def _():
            dg_ref[...] = jnp.zeros_like(dg_ref)

        dg_ref[...] += dg

    call = dict(name=name, grid=(t // ROW_TILE,), in_specs=in_specs, out_specs=[_row_spec(), _row_spec(), _vec_spec()],
                out_shape=[jax.ShapeDtypeStruct((t, D_MODEL), F32), jax.ShapeDtypeStruct((t, D_MODEL), BF16),
                           jax.ShapeDtypeStruct((1, D_MODEL), F32)])
    if behind is not None:
        return _call_behind(body, behind, scratch_shapes=[], operands=operands, **call)
    return pl.pallas_call(body, compiler_params=_cparams(("arbitrary",)), **call)(*operands)


def _gate_fwd(gl, b_gate, of, os_):
    t = of.shape[0]

    def body(gla_ref, glb_ref, b_ref, of_ref, os_ref, m_ref):
        ga = jax.nn.sigmoid(gla_ref[...] + b_ref[0:1, :])
        gb = jax.nn.sigmoid(glb_ref[...] + b_ref[1:2, :])
        m_ref[...] = (ga * of_ref[...] + gb * os_ref[...]).astype(BF16)

    return pl.pallas_call(
        body, name="gate_fwd", grid=(t // ROW_TILE,),
        in_specs=[pl.BlockSpec((ROW_TILE, D_MODEL), lambda i: (i, 0)), pl.BlockSpec((ROW_TILE, D_MODEL), lambda i: (i, 1)),
                  _vec_spec(2), _row_spec(), _row_spec()],
        out_specs=_row_spec(), out_shape=jax.ShapeDtypeStruct((t, D_MODEL), BF16),
        compiler_params=_cparams(("parallel",)),
    )(gl, gl, b_gate, of, os_)


def _gate_bwd(gl, b_gate, of, os_, dx, w_out):
    t = of.shape[0]

    def body(gla_ref, glb_ref, b_ref, of_ref, os_ref, dx_ref, w_ref, dof_ref, dos_ref, dgla_ref, dglb_ref, db_ref):
        dm = _dot(dx_ref[...], w_ref[...], NT_DIMS)
        ga = jax.nn.sigmoid(gla_ref[...] + b_ref[0:1, :])
        gb = jax.nn.sigmoid(glb_ref[...] + b_ref[1:2, :])
        dof_ref[...] = (dm * ga).astype(BF16)
        dos_ref[...] = (dm * gb).astype(BF16)
        dgla = dm * of_ref[...] * ga * (1.0 - ga)
        dglb = dm * os_ref[...] * gb * (1.0 - gb)
        dgla_ref[...] = dgla.astype(BF16)
        dglb_ref[...] = dglb.astype(BF16)

        @pl.when(pl.program_id(0) == 0)
        def _():
            db_ref[...] = jnp.zeros_like(db_ref)

        db_ref[0:1, :] += jnp.sum(dgla, axis=0, keepdims=True)
        db_ref[1:2, :] += jnp.sum(dglb, axis=0, keepdims=True)

    outs = pl.pallas_call(
        body, name="gate_bwd", grid=(t // ROW_TILE,),
        in_specs=[pl.BlockSpec((ROW_TILE, D_MODEL), lambda i: (i, 0)), pl.BlockSpec((ROW_TILE, D_MODEL), lambda i: (i, 1)),
                  _vec_spec(2), _row_spec(), _row_spec(), _row_spec(), pl.BlockSpec(w_out.shape, lambda i: (0, 0))],
        out_specs=[_row_spec(), _row_spec(), _row_spec(), _row_spec(), _vec_spec(2)],
        out_shape=[jax.ShapeDtypeStruct((t, D_MODEL), BF16)] * 4 + [jax.ShapeDtypeStruct((2, D_MODEL), F32)],
        compiler_params=_cparams(("arbitrary",)),
    )(gl, gl, b_gate, of, os_, dx, w_out)
    return outs


def _head_and_loss(x2, h3, p, w_gate, w_ple, g_final, target):
    t = x2.shape[0]

    def body(x2_ref, h3_ref, p_ref, wg_ref, wp_ref, g_ref, tgt_ref, dx3_ref, dpre_ref, dpe_ref, dg_ref, loss_ref):
        gp = jax.nn.sigmoid(_dot(h3_ref[...], wg_ref[...]))
        p_t = p_ref[...].astype(BF16)
        pe_t = jnp.concatenate([_dot(p_t, wp_ref[j]) for j in range(N_CHIPS)], axis=1)
        x3 = x2_ref[...] + gp * pe_t
        g = g_ref[...]
        xh, _ = _xhat(x3)
        err = xh * g - tgt_ref[...]
        dy = err * (1.0 / D_MODEL)
        dx3, dg = _rms_bwd_rows(dy, x3, g)
        dx3_ref[...] = dx3
        dpre_ref[...] = (dx3 * pe_t * gp * (1.0 - gp)).astype(BF16)
        dpe_ref[...] = (dx3 * gp).astype(BF16)

        @pl.when(pl.program_id(0) == 0)
        def _():
            dg_ref[...] = jnp.zeros_like(dg_ref)
            loss_ref[...] = jnp.zeros_like(loss_ref)

        dg_ref[...] += dg
        loss_ref[...] += 0.5 * jnp.sum(jnp.mean(err * err, axis=-1, keepdims=True), axis=0, keepdims=True)

    return pl.pallas_call(
        body, name="head_and_loss", grid=(t // ROW_TILE,),
        in_specs=[_row_spec(), _row_spec(), _row_spec(D_PLE), pl.BlockSpec(w_gate.shape, lambda i: (0, 0)),
                  pl.BlockSpec(w_ple.shape, lambda i: (0, 0, 0)), _vec_spec(), _row_spec()],
        out_specs=[_row_spec(), _row_spec(), _row_spec(), _vec_spec(), _vec_spec(1, LANES)],
        out_shape=[jax.ShapeDtypeStruct((t, D_MODEL), F32), jax.ShapeDtypeStruct((t, D_MODEL), BF16),
                   jax.ShapeDtypeStruct((t, D_MODEL), BF16), jax.ShapeDtypeStruct((1, D_MODEL), F32),
                   jax.ShapeDtypeStruct((1, LANES), F32)],
        compiler_params=_cparams(("arbitrary",)),
    )(x2, h3, p, w_gate, w_ple, g_final, target)


def _split3(v):
    hi = v.astype(BF16)
    r1 = v - hi.astype(F32)
    mid = r1.astype(BF16)
    lo = (r1 - mid.astype(F32)).astype(BF16)
    return hi, mid, lo


def _split2(v):
    hi = v.astype(BF16)
    return jnp.concatenate([hi, (v - hi.astype(F32)).astype(BF16)], axis=1)


def _dot(a, b, dims=_DIMS["nn"]):
    return lax.dot_general(a, b, dims, preferred_element_type=F32)


def _tri(n, rel):
    row = lax.broadcasted_iota(jnp.int32, (n, n), 0)
    col = lax.broadcasted_iota(jnp.int32, (n, n), 1)
    return rel(row, col).astype(BF16)


def _tri2(n, rel):
    t = _tri(n, rel)
    return jnp.concatenate([t, t], axis=0)


def _log_sigmoid(v):
    return -(jnp.maximum(-v, 0.0) + jnp.log(1.0 + jnp.exp(-jnp.abs(v))))


def _fox_prep(fl, b_forget, batch, seq):
    nb = seq // ATT_BLOCK

    def body(fl_ref, b_ref, cw_ref, cr_ref):
        col = lax.broadcasted_iota(jnp.int32, (ATT_BLOCK, F_PAD), 1)
        lower = _tri(ATT_BLOCK, lambda r, c: c <= r)
        upper = _tri(ATT_BLOCK, lambda r, c: r <= c)
        expand = (lax.broadcasted_iota(jnp.int32, (F_PAD, D_ATT), 1) // HEAD_DIM
                  == lax.broadcasted_iota(jnp.int32, (F_PAD, D_ATT), 0)).astype(BF16)
        carry_w = jnp.zeros((1, D_ATT), F32)
        carry_r = jnp.zeros((F_PAD, 1), F32)
        for i in range(nb):
            blk = slice(i * ATT_BLOCK, (i + 1) * ATT_BLOCK)
            logf = jnp.where(col < N_HEADS, _log_sigmoid(fl_ref[blk, :] + b_ref[...]), 0.0)
            cw = jnp.zeros((ATT_BLOCK, D_ATT), F32) + carry_w
            cr = jnp.zeros((F_PAD, ATT_BLOCK), F32) + carry_r
            for part in _split3(logf):
                cw += _dot(lower, _dot(part, expand).astype(BF16))
                cr += _dot(part, upper, TN_DIMS)
            cw_ref[blk, :] = cw
            cr_ref[:, blk] = cr[0:N_HEADS, :]
            carry_w = cw[ATT_BLOCK - 1:ATT_BLOCK, :]
            carry_r = cr[:, ATT_BLOCK - 1:ATT_BLOCK]

    return pl.pallas_call(
        body, name="fox_prep", grid=(batch,),
        in_specs=[pl.BlockSpec((seq, F_PAD), lambda b: (b, 0)), pl.BlockSpec((1, F_PAD), lambda b: (0, 0))],
        out_specs=[pl.BlockSpec((seq, D_ATT), lambda b: (b, 0)), pl.BlockSpec((N_HEADS, seq), lambda b: (b, 0))],
        out_shape=[jax.ShapeDtypeStruct((batch * seq, D_ATT), F32), jax.ShapeDtypeStruct((batch * N_HEADS, seq), F32)],
        compiler_params=_cparams(("parallel",)),
    )(fl, b_forget)


def _fox_post(dcs_wide, drs_wide, fl, b_forget, batch, seq):
    nb = seq // ATT_BLOCK

    def body(dcs_ref, drs_ref, fl_ref, b_ref, dfl_ref, db_ref):
        pick = (lax.broadcasted_iota(jnp.int32, (D_ATT, F_PAD), 0)
                == lax.broadcasted_iota(jnp.int32, (D_ATT, F_PAD), 1) * HEAD_DIM).astype(BF16)
        upper = _tri(ATT_BLOCK, lambda r, c: r <= c)
        col = lax.broadcasted_iota(jnp.int32, (ATT_BLOCK, F_PAD), 1)

        @pl.when(pl.program_id(0) == 0)
        def _():
            db_ref[...] = jnp.zeros_like(db_ref)

        carry = jnp.zeros((1, F_PAD), F32)
        for i in reversed(range(nb)):
            blk = slice(i * ATT_BLOCK, (i + 1) * ATT_BLOCK)
            narrow = jnp.zeros((ATT_BLOCK, F_PAD), F32)
            for part in _split3(drs_ref[blk, :] - dcs_ref[blk, :]):
                narrow += _dot(part, pick)
            after = jnp.zeros((ATT_BLOCK, F_PAD), F32) + carry
            for part in _split3(narrow):
                after += _dot(upper, part)
            carry = after[0:1, :]
            pre = fl_ref[blk, :] + b_ref[...]
            dfl = jnp.where(col < N_HEADS, after * jax.nn.sigmoid(-pre), 0.0)
            dfl_ref[blk, :] = dfl.astype(BF16)
            db_ref[...] += jnp.sum(dfl, axis=0, keepdims=True)

    return pl.pallas_call(
        body, name="fox_post", grid=(batch,),
        in_specs=[pl.BlockSpec((seq, D_ATT), lambda b: (b, 0)), pl.BlockSpec((seq, D_ATT), lambda b: (b, 0)),
                  pl.BlockSpec((seq, F_PAD), lambda b: (b, 0)), pl.BlockSpec((1, F_PAD), lambda b: (0, 0))],
        out_specs=[pl.BlockSpec((seq, F_PAD), lambda b: (b, 0)), pl.BlockSpec((1, F_PAD), lambda b: (0, 0))],
        out_shape=[jax.ShapeDtypeStruct((batch * seq, F_PAD), BF16), jax.ShapeDtypeStruct((1, F_PAD), F32)],
        compiler_params=_cparams(("arbitrary",)),
    )(dcs_wide, drs_wide, fl, b_forget)


N_PAIRS = N_HEADS // 2


def _att_specs(seq, col0, tq):
    nq = seq // tq
    q = pl.BlockSpec((tq, LANES), lambda b, hp, qi: (b * nq + qi, col0 + hp))
    k = pl.BlockSpec((seq, LANES), lambda b, hp, qi: (b, col0 + N_PAIRS + hp))
    v = pl.BlockSpec((seq, LANES), lambda b, hp, qi: (b, col0 + 2 * N_PAIRS + hp))
    return q, k, v


def _qblock_spec(seq, tq):
    nq = seq // tq
    return pl.BlockSpec((tq, LANES), lambda b, hp, qi: (b * nq + qi, hp))


def _kv_out_spec(seq):
    return pl.BlockSpec((seq, LANES), lambda b, hp, qi: (b, hp))


def _head_masks():
    lane = lax.broadcasted_iota(jnp.int32, (1, LANES), 1)
    return [(lane >= HEAD_DIM * j) & (lane < HEAD_DIM * (j + 1)) for j in range(2)]


def _stack_heads(t, masks):
    zero = jnp.zeros_like(t)
    return jnp.concatenate([jnp.where(masks[0], t, zero), jnp.where(masks[1], t, zero)], axis=0)


def _stack_cols(t):
    return jnp.concatenate([t[:, 0:1], t[:, HEAD_DIM:HEAD_DIM + 1]], axis=0)


def _unstack(t2, masks):
    tq = t2.shape[0] // 2
    return jnp.where(masks[0], t2[:tq], t2[tq:])


def _stacked_ids(tq, tk):
    row = lax.broadcasted_iota(jnp.int32, (2 * tq, tk), 0)
    col = lax.broadcasted_iota(jnp.int32, (2 * tq, tk), 1)
    first = lax.broadcasted_iota(jnp.int32, (2 * tq, 1), 0) < tq
    return col - jnp.where(row < tq, row, row - tq), first


def _sweep(qi, tq, tk, step, init, leftward):
    per = tq // tk
    whole = lambda carry: lax.fori_loop(0, per * qi, lambda i, c: step(per * qi - 1 - i if leftward else i, c, None), carry)
    crossed = [(per * qi + j, -j * tk) for j in range(per)]
    if leftward:
        carry = init
        for kb, lead in reversed(crossed):
            carry = step(kb, carry, lead)
        return whole(carry)
    carry = whole(init)
    for kb, lead in crossed:
        carry = step(kb, carry, lead)
    return carry


def _fox_fwd(qkv, c_wide, c_row, batch, seq):
    tq, tk = FOX_TILES
    nq = seq // tq

    def body(q_ref, k_ref, v_ref, cw_ref, cr_ref, o_ref, lse_ref):
        hp, qi = pl.program_id(1), pl.program_id(2)
        masks = _head_masks()
        ahead, first = _stacked_ids(tq, tk)
        q2 = _stack_heads(q_ref[...], masks) * SCALE
        ct = _stack_cols(cw_ref[...])

        def step(kb, carry, lead):
            m, l, acc = carry
            k0 = pl.multiple_of(kb * tk, tk)
            cs = jnp.where(first, cr_ref[pl.ds(2 * hp, 1), pl.ds(k0, tk)], cr_ref[pl.ds(2 * hp + 1, 1), pl.ds(k0, tk)])
            s = _dot(q2, k_ref[pl.ds(k0, tk), :], NT_DIMS) + ct - cs
            if lead is not None:
                s = jnp.where(ahead <= lead, s, NEG)
            m_new = jnp.maximum(m, jnp.max(s, axis=1, keepdims=True))
            p = jnp.exp(s - m_new)
            alpha = jnp.exp(m - m_new)
            l = alpha * l + jnp.sum(p, axis=1, keepdims=True)
            acc = alpha * acc + _dot(p.astype(BF16), v_ref[pl.ds(k0, tk), :])
            return m_new, l, acc

        init = (jnp.full((2 * tq, 1), NEG, F32), jnp.zeros((2 * tq, 1), F32), jnp.zeros((2 * tq, LANES), F32))
        m, l, acc = _sweep(qi, tq, tk, step, init, leftward=False)
        o_ref[...] = _unstack(acc / l, masks).astype(BF16)
        lse_ref[...] = _unstack(m + jnp.log(l), masks)

    q_spec, k_spec, v_spec = _att_specs(seq, 0, tq)
    qb = _qblock_spec(seq, tq)
    return pl.pallas_call(
        body, name="fox_fwd", grid=(batch, N_PAIRS, nq),
        in_specs=[q_spec, k_spec, v_spec, qb, pl.BlockSpec((N_HEADS, seq), lambda b, hp, qi: (b, 0))],
        out_specs=[qb, qb],
        out_shape=[jax.ShapeDtypeStruct((batch * seq, D_ATT), BF16), jax.ShapeDtypeStruct((batch * seq, D_ATT), F32)],
        compiler_params=_cparams(("parallel", "parallel", "arbitrary")),
    )(qkv, qkv, qkv, c_wide, c_row)


def _fox_bwd(qkv, c_wide, c_row, o, do, lse_wide, batch, seq, behind):
    tq, tk = FOX_TILES
    nq = seq // tq

    def body(q_ref, k_ref, v_ref, cw_ref, cr_ref, o_ref, do_ref, lse_ref,
             dq_ref, dk_ref, dv_ref, dcs_ref, drs_ref, dkc_acc, dv_acc):
        hp, qi = pl.program_id(1), pl.program_id(2)

        @pl.when(qi == 0)
        def _():
            dkc_acc[...] = jnp.zeros_like(dkc_acc)
            dv_acc[...] = jnp.zeros_like(dv_acc)

        masks = _head_masks()
        ahead, first = _stacked_ids(tq, tk)
        q_t, do_t = q_ref[...], do_ref[...]
        q2 = _stack_heads(q_t, masks) * SCALE
        do2 = _stack_heads(do_t, masks)
        q_and_ones = jnp.concatenate([q2, _stack_heads(jnp.ones_like(q_t), masks)], axis=1)
        ct = _stack_cols(cw_ref[...])
        lse = _stack_cols(lse_ref[...])
        prod = do_t.astype(F32) * o_ref[...].astype(F32)
        delta = jnp.concatenate([jnp.sum(jnp.where(mk, prod, 0.0), axis=1, keepdims=True) for mk in masks], axis=0)

        def step(kb, carry, lead):
            dq_acc, rs = carry
            k0 = pl.multiple_of(kb * tk, tk)
            kblk = k_ref[pl.ds(k0, tk), :]
            cs = jnp.where(first, cr_ref[pl.ds(2 * hp, 1), pl.ds(k0, tk)], cr_ref[pl.ds(2 * hp + 1, 1), pl.ds(k0, tk)])
            p = jnp.exp(_dot(q2, kblk, NT_DIMS) + ct - cs - lse)
            if lead is not None:
                p = jnp.where(ahead <= lead, p, 0.0)
            dp = _dot(do2, v_ref[pl.ds(k0, tk), :], NT_DIMS)
            ds = (p * (dp - delta)).astype(BF16)
            dkc_acc[pl.ds(k0, tk), :] += _dot(ds, q_and_ones, TN_DIMS)
            dv_acc[pl.ds(k0, tk), :] += _dot(p.astype(BF16), do2, TN_DIMS)
            return dq_acc + _dot(ds, kblk), rs + jnp.sum(ds.astype(F32), axis=1, keepdims=True)

        init = (jnp.zeros((2 * tq, LANES), F32), jnp.zeros((2 * tq, 1), F32))
        dq_acc, rs = _sweep(qi, tq, tk, step, init, leftward=False)
        dq_ref[...] = (_unstack(dq_acc, masks) * SCALE).astype(BF16)
        drs_ref[...] = _unstack(rs, masks)

        @pl.when(qi == nq - 1)
        def _():
            dk_ref[...] = dkc_acc[:, 0:LANES].astype(BF16)
            dcs_ref[...] = dkc_acc[:, LANES:2 * LANES]
            dv_ref[...] = dv_acc[...].astype(BF16)

    q_spec, k_spec, v_spec = _att_specs(seq, 0, tq)
    qb = _qblock_spec(seq, tq)
    return _call_behind(
        body, behind, name="fox_bwd", grid=(batch, N_PAIRS, nq),
        in_specs=[q_spec, k_spec, v_spec, qb, pl.BlockSpec((N_HEADS, seq), lambda b, hp, qi: (b, 0)), qb, qb, qb],
        out_specs=[qb, _kv_out_spec(seq), _kv_out_spec(seq), _kv_out_spec(seq), qb],
        out_shape=[jax.ShapeDtypeStruct((batch * seq, D_ATT), BF16)] * 3 + [jax.ShapeDtypeStruct((batch * seq, D_ATT), F32)] * 2,
        scratch_shapes=[pltpu.VMEM((seq, 2 * LANES), F32), pltpu.VMEM((seq, LANES), F32)],
        operands=(qkv, qkv, qkv, c_wide, c_row, o, do, lse_wide))


def _sb_logits(q2, kblk):
    z = _dot(q2, kblk, NT_DIMS)
    lsn = jnp.minimum(-z, 0.0) - jnp.log(1.0 + jnp.exp(-jnp.abs(z)))
    return lsn + z, lsn


def _sb_fwd(qkv, batch, seq, behind):
    tq, tk = SB_TILES
    nq = seq // tq

    def body(q_ref, k_ref, v_ref, o_ref, rt_ref):
        qi = pl.program_id(2)
        masks = _head_masks()
        ahead, _ = _stacked_ids(tq, tk)
        later = _tri2(tk, lambda r, c: r > c)
        q2 = _stack_heads(q_ref[...], masks) * SCALE

        def step(kb, carry, lead):
            run, acc = carry
            k0 = pl.multiple_of(kb * tk, tk)
            ls, lsn = _sb_logits(q2, k_ref[pl.ds(k0, tk), :])
            if lead is not None:
                lsn = jnp.where(ahead < lead, lsn, 0.0)
            w = jnp.exp(ls + _dot(_split2(lsn), later) + run)
            if lead is not None:
                w = jnp.where(ahead < lead, w, 0.0)
            return run + jnp.sum(lsn, axis=1, keepdims=True), acc + _dot(w.astype(BF16), v_ref[pl.ds(k0, tk), :])

        init = (jnp.zeros((2 * tq, 1), F32), jnp.zeros((2 * tq, LANES), F32))
        run, acc = _sweep(qi, tq, tk, step, init, leftward=True)
        o_ref[...] = _unstack(acc, masks).astype(BF16)
        rt_ref[...] = _unstack(run, masks)

    q_spec, k_spec, v_spec = _att_specs(seq, 3 * N_PAIRS, tq)
    qb = _qblock_spec(seq, tq)
    return _call_behind(
        body, behind, name="sb_fwd", grid=(batch, N_PAIRS, nq), in_specs=[q_spec, k_spec, v_spec], out_specs=[qb, qb],
        out_shape=[jax.ShapeDtypeStruct((batch * seq, D_ATT), BF16), jax.ShapeDtypeStruct((batch * seq, D_ATT), F32)],
        scratch_shapes=[], operands=(qkv, qkv, qkv))


def _sb_bwd(qkv, do, rt_wide, batch, seq, behind):
    tq, tk = SB_TILES
    nq = seq // tq

    def body(q_ref, k_ref, v_ref, do_ref, rt_ref, dq_ref, dk_ref, dv_ref, dk_acc, dv_acc):
        qi = pl.program_id(2)

        @pl.when(qi == 0)
        def _():
            dk_acc[...] = jnp.zeros_like(dk_acc)
            dv_acc[...] = jnp.zeros_like(dv_acc)

        masks = _head_masks()
        ahead, _ = _stacked_ids(tq, tk)
        later = _tri2(tk, lambda r, c: r > c)
        earlier = _tri2(tk, lambda r, c: r < c)
        q2 = _stack_heads(q_ref[...], masks) * SCALE
        do2 = _stack_heads(do_ref[...], masks)
        total = _stack_cols(rt_ref[...])

        def step(kb, carry, lead):
            pref, epre, dq_acc = carry
            k0 = pl.multiple_of(kb * tk, tk)
            kblk = k_ref[pl.ds(k0, tk), :]
            ls, lsn_all = _sb_logits(q2, kblk)
            lsn = lsn_all if lead is None else jnp.where(ahead < lead, lsn_all, 0.0)
            rs = jnp.sum(lsn, axis=1, keepdims=True)
            w = jnp.exp(ls + _dot(_split2(lsn), later) + (total - pref - rs))
            if lead is not None:
                w = jnp.where(ahead < lead, w, 0.0)
            e = w * _dot(do2, v_ref[pl.ds(k0, tk), :], NT_DIMS)
            before = _dot(_split2(e), earlier) + epre
            dz = e * jnp.exp(lsn_all) - jnp.exp(ls) * before
            if lead is not None:
                dz = jnp.where(ahead < lead, dz, 0.0)
            dz = dz.astype(BF16)
            dk_acc[pl.ds(k0, tk), :] += _dot(dz, q2, TN_DIMS)
            dv_acc[pl.ds(k0, tk), :] += _dot(w.astype(BF16), do2, TN_DIMS)
            return pref + rs, epre + jnp.sum(e, axis=1, keepdims=True), dq_acc + _dot(dz, kblk)

        init = (jnp.zeros((2 * tq, 1), F32), jnp.zeros((2 * tq, 1), F32), jnp.zeros((2 * tq, LANES), F32))
        dq_acc = _sweep(qi, tq, tk, step, init, leftward=False)[2]
        dq_ref[...] = (_unstack(dq_acc, masks) * SCALE).astype(BF16)

        @pl.when(qi == nq - 1)
        def _():
            dk_ref[...] = dk_acc[...].astype(BF16)
            dv_ref[...] = dv_acc[...].astype(BF16)

    q_spec, k_spec, v_spec = _att_specs(seq, 3 * N_PAIRS, tq)
    qb = _qblock_spec(seq, tq)
    return _call_behind(
        body, behind, name="sb_bwd", grid=(batch, N_PAIRS, nq), in_specs=[q_spec, k_spec, v_spec, qb, qb],
        out_specs=[qb, _kv_out_spec(seq), _kv_out_spec(seq)], out_shape=[jax.ShapeDtypeStruct((batch * seq, D_ATT), BF16)] * 3,
        scratch_shapes=[pltpu.VMEM((seq, LANES), F32), pltpu.VMEM((seq, LANES), F32)], operands=(qkv, qkv, qkv, do, rt_wide))


def _local_step(x, p, target, w, rest, vec, place):
    batch, seq, _ = x.shape
    t = batch * seq
    x = x.reshape(t, D_MODEL)
    target = target.reshape(t, D_MODEL)
    p = p.reshape(t, D_PLE)
    big = dict(tm=1024, tn=1024, tk=1024)

    h1 = _norm_fwd(x, vec["g_mix"], "norm_mix")
    qkv = _mm(h1, w["qkv"], mode="nn", name="proj_qkv", out_dtype=BF16, **big)
    gl = _mm(h1, w["gate"], mode="nn", name="proj_gate", **big)
    fl = _mm(h1, w["forget"], mode="nn", name="proj_forget", **big)
    c_wide, c_row = _fox_prep(fl, vec["b_forget"], batch, seq)
    o_fox, lse_wide = _fox_fwd(qkv, c_wide, c_row, batch, seq)
    (o_sb, rt_wide), gathered = _sb_fwd(qkv, batch, seq, rest)
    w = dict(w, **_rest_weights(dict(zip(EARLY, gathered))))
    of = _mm(o_fox, w["branch_fox"], mode="nn", name="branch_fox", col_shards=True, **big)
    os_ = _mm(o_sb, w["branch_sb"], mode="nn", name="branch_sb", col_shards=True, **big)
    merged = _gate_fwd(gl, w["b_gate"], of, os_)
    x1, h2 = _mm_res_norm(merged, w["out"], x, vec["g_mlp"], "proj_out_norm")
    ar = _mm(h2, w["up"], mode="nn", name="mlp_up", out_dtype=BF16, epi=lambda acc, _: jnp.maximum(acc, 0.0),
             col_shards=True, **big)
    x2, h3 = _mm_res_norm(ar, w["down"], x1, vec["g_ple"], "mlp_down_norm", a_fn=_relu2)

    dx3, dpre, dpe, dg_final, loss = _head_and_loss(x2, h3, p, w["ple_gate"], w["ple"], vec["g_final"], target)
    gw = {}
    gw["ple"] = _mm(p, dpe, mode="tn", name="d_w_ple", col_shards=True, **big)
    gw["ple_gate"] = _mm(h3, dpre, mode="tn", name="d_w_ple_gate", **big)
    dx2, dx2b, dg_ple = _mm_norm_bwd([(dpre, w["ple_gate"])], None, x2, vec["g_ple"], dx3, "d_h_ple_norm_bwd")
    gw["down"] = _mm(ar, dx2b, mode="tn", name="d_w_down", a_fn=_relu2, **big)
    da = _mm(dx2b, w["down"], mode="nt", name="d_act", out_dtype=BF16,
             epi=lambda acc, r: acc * (2.0 * r.astype(F32)), extra=ar, **big)
    gw["up"] = _mm(h2, da, mode="tn", name="d_w_up", col_shards=True, **big)
    dx1, dx1b, dg_mlp = _mm_norm_bwd([(da, w["up"])], None, x1, vec["g_mlp"], dx2, "d_h_mlp_norm_bwd")
    gw["out"] = _mm(merged, dx1b, mode="tn", name="d_w_out", **big)
    dof, dos, dgla, dglb, gw["b_gate"] = _gate_bwd(gl, w["b_gate"], of, os_, dx1b, w["out"])
    gw["branch_fox"] = _mm(o_fox, dof, mode="tn", name="d_w_branch_fox", col_shards=True, **big)
    gw["branch_sb"] = _mm(o_sb, dos, mode="tn", name="d_w_branch_sb", col_shards=True, **big)
    do_fox = _mm(dof, w["branch_fox"], mode="nt", name="d_o_fox", out_dtype=BF16, col_shards=True, **big)
    do_sb = _mm(dos, w["branch_sb"], mode="nt", name="d_o_sb", out_dtype=BF16, col_shards=True, **big)
    early = _early_slots(gw)
    early = [early[n] for n in EARLY]
    (dq_a, dk_a, dv_a, dcs_wide, drs_wide), received = _fox_bwd(qkv, c_wide, c_row, o_fox, do_fox, lse_wide, batch, seq,
                                                                _swap_halves(early))
    sums = [_sum_sibling(place, s, r, "sum_sibling_" + n) for s, r, n in zip(early, received, EARLY)]
    (dq_b, dk_b, dv_b), others = _sb_bwd(qkv, do_sb, rt_wide, batch, seq, _exchange_chips(sums))
    mine = [_sum_chips(place, s, r, o, "sum_chips_" + n) for s, r, o, n in zip(early, received, others, EARLY)]
    dfl, db_forget = _fox_post(dcs_wide, drs_wide, fl, vec["b_forget"], batch, seq)
    dqkv = jnp.concatenate([dq_a, dk_a, dv_a, dq_b, dk_b, dv_b], axis=1)
    dgl = jnp.concatenate([dgla, dglb], axis=1)
    gw["qkv"], theirs = _mm(dqkv, h1, mode="tn", name="d_w_qkv", behind=_share_halves(mine), **big)
    reduced = dict(zip(EARLY, zip(mine, theirs)))
    gw["gate"] = _mm(dgl, h1, mode="tn", name="d_w_gate", **big)
    gw["forget"] = _mm(dfl, h1, mode="tn", name="d_w_forget", **big)
    late = [_w_in_slots(gw)]
    dh1, received = _mm(dqkv, w["qkv"], mode="nt", name="d_h_qkv", behind=_swap_halves(late), **big)
    sums = [_sum_sibling(place, late[0], received[0], "sum_sibling_w_in")]
    (grad_x, _, dg_mix), others = _mm_norm_bwd([(dgl, w["gate"]), (dfl, w["forget"])], dh1, x, vec["g_mix"], dx1,
                                               "d_h_gate_norm_bwd", behind=_exchange_chips(sums))
    mine = [_sum_chips(place, late[0], received[0], others[0], "sum_chips_w_in")]
    reduced["w_in"] = (mine[0], _run_exchange(_share_halves(mine), "reduce_share_w_in")[0])
    gvec = {"g_mix": dg_mix, "b_forget": db_forget[:, 0:N_HEADS], "g_mlp": dg_mlp, "g_ple": dg_ple,
            "g_final": dg_final, "b_gate": gw["b_gate"]}
    return loss, grad_x.reshape(batch, seq, D_MODEL), reduced, gvec


ANY = pl.BlockSpec(memory_space=pl.ANY)
SHARDED = ("w_in", "w_branch_fox", "w_branch_sb", "w_out", "w_up", "w_down", "w_ple_gate", "w_ple")
ROW_ALIGN = 16
F32_ROWS = 8


def _place():
    return lax.axis_index("x"), lax.axis_index("y"), lax.axis_index("c")


def _other_chips(x, y):
    return [(1 - x, y), (x, 1 - y), (1 - x, 1 - y)]


def _half(ref, h):
    r = ref.shape[0] // 2
    assert r % ROW_ALIGN == 0
    return ref.at[pl.ds(pl.multiple_of(h * r, ROW_ALIGN), r)]


def _remote(src, dst, sems, idx, to):
    send_sems, recv_sems = sems
    return pltpu.make_async_remote_copy(src_ref=src, dst_ref=dst, send_sem=send_sems.at[idx], recv_sem=recv_sems.at[idx],
                                        device_id=to, device_id_type=MESH)


class _Exchange:
    def __init__(self, operands, out_shapes, sem_shape, start, finish):
        self.operands, self.out_shapes, self.sem_shape, self.start, self.finish = operands, out_shapes, sem_shape, start, finish

    def scratch(self):
        return [pltpu.SemaphoreType.DMA(self.sem_shape), pltpu.SemaphoreType.DMA(self.sem_shape)]


def _run_exchange(ex, name):
    n = len(ex.operands)

    def body(*refs):
        ex.start(refs[:n], refs[n:2 * n], refs[2 * n:])
        ex.finish(refs[:n], refs[n:2 * n], refs[2 * n:])

    return pl.pallas_call(body, name=name, in_specs=[ANY] * n, out_specs=[ANY] * n, out_shape=ex.out_shapes,
                          scratch_shapes=ex.scratch())(*ex.operands)


def _call_behind(body, ex, *, name, grid, in_specs, out_specs, out_shape, scratch_shapes, operands):
    n_in, n_out, nx = len(in_specs), len(out_specs), len(ex.operands)

    def wrapped(*refs):
        ins, x_in = refs[:n_in], refs[n_in:n_in + nx]
        outs, x_out = refs[n_in + nx:n_in + nx + n_out], refs[n_in + nx + n_out:n_in + 2 * nx + n_out]
        scratch, sems = refs[n_in + 2 * nx + n_out:-2], refs[-2:]
        first, last = None, None
        for d, steps in enumerate(grid):
            at_start, at_end = pl.program_id(d) == 0, pl.program_id(d) == steps - 1
            first = at_start if first is None else first & at_start
            last = at_end if last is None else last & at_end

        @pl.when(first)
        def _():
            ex.start(x_in, x_out, sems)

        body(*ins, *outs, *scratch)

        @pl.when(last)
        def _():
            ex.finish(x_in, x_out, sems)

    res = pl.pallas_call(
        wrapped, name=name, grid=grid, in_specs=list(in_specs) + [ANY] * nx, out_specs=list(out_specs) + [ANY] * nx,
        out_shape=list(out_shape) + list(ex.out_shapes), scratch_shapes=list(scratch_shapes) + ex.scratch(),
        compiler_params=_cparams(("arbitrary",) * len(grid)),
    )(*operands, *ex.operands)
    return res[:n_out], res[n_out:]


def _gather_weights(shards):
    n = len(shards)

    def first_copies(src, out, sems):
        x, y, c = _place()
        me = 2 * x + y
        copies = [_remote(_half(src[t], c), _half(out[t].at[me], c), sems, (t, k), (px, py, c))
                  for t in range(n) for k, (px, py) in enumerate(_other_chips(x, y))]
        return copies + [_remote(src[t], out[t].at[me], sems, (t, 3), (x, y, 1 - c)) for t in range(n)]

    def start(src, out, sems):
        for cp in first_copies(src, out, sems):
            cp.start()

    def finish(src, out, sems):
        x, y, c = _place()
        me = 2 * x + y
        sibling = (x, y, 1 - c)
        chips = _other_chips(x, y)
        passes = []
        for t in range(n):
            for k, (px, py) in enumerate(chips):
                landed = _half(out[t].at[2 * px + py], c)
                _remote(landed, landed, sems, (t, k), (px, py, c)).wait_recv()
                passes.append(_remote(landed, landed, sems, (t, 4 + k), sibling))
                passes[-1].start()
        for t in range(n):
            _remote(src[t], out[t].at[me], sems, (t, 3), sibling).wait_recv()
            for k, (px, py) in enumerate(chips):
                passed = _half(out[t].at[2 * px + py], 1 - c)
                _remote(passed, passed, sems, (t, 4 + k), sibling).wait_recv()
        for cp in first_copies(src, out, sems) + passes:
            cp.wait_send()

    return _Exchange(shards, [jax.ShapeDtypeStruct((N_CHIPS,) + s.shape, s.dtype) for s in shards], (n, 7), start, finish)


def _simple_exchange(operands, out_shapes, copies):
    def start(src, out, sems):
        for cp in copies(src, out, sems):
            cp.start()

    def finish(src, out, sems):
        for cp in copies(src, out, sems):
            cp.wait_recv()
        for cp in copies(src, out, sems):
            cp.wait_send()

    return _Exchange(operands, out_shapes, (len(operands),), start, finish)


def _swap_halves(slots):
    def copies(src, out, sems):
        x, y, c = _place()
        res = []
        for t in range(len(slots)):
            r = src[t].shape[1] // 2
            rows = pl.ds(pl.multiple_of((1 - c) * r, F32_ROWS), r)
            res.append(_remote(src[t].at[:, rows], out[t], sems, t, (x, y, 1 - c)))
        return res

    return _simple_exchange(slots, [jax.ShapeDtypeStruct((N_CHIPS, s.shape[1] // 2, s.shape[2]), s.dtype) for s in slots], copies)


def _exchange_chips(sums):
    n = len(sums)

    def copies(src, out, sems):
        x, y, c = _place()
        return [_remote(src[t].at[2 * px + py], out[t].at[k], sems, (t, k), (px, py, c))
                for t in range(n) for k, (px, py) in enumerate(_other_chips(x, y))]

    def start(src, out, sems):
        for cp in copies(src, out, sems):
            cp.start()

    def finish(src, out, sems):
        for cp in copies(src, out, sems):
            cp.wait_recv()
        for cp in copies(src, out, sems):
            cp.wait_send()

    return _Exchange(sums, [jax.ShapeDtypeStruct((3,) + s.shape[1:], s.dtype) for s in sums], (n, 3), start, finish)


def _share_halves(mine):
    def copies(src, out, sems):
        x, y, c = _place()
        return [_remote(src[t], out[t], sems, t, (x, y, 1 - c)) for t in range(len(mine))]

    return _simple_exchange(mine, [jax.ShapeDtypeStruct(s.shape, s.dtype) for s in mine], copies)


def _half_tile(rows):
    return 256 if rows % 256 == 0 else rows


def _sum_sibling(place, slot, received, name):
    n, rows2, cols = slot.shape
    rows = rows2 // 2
    tile = _half_tile(rows)
    nb = rows // tile

    def body(place_ref, a_ref, b_ref, o_ref):
        o_ref[...] = (a_ref[...] + b_ref[...]).astype(BF16)

    return pl.pallas_call(
        body, name=name, out_shape=jax.ShapeDtypeStruct((n, rows, cols), BF16),
        grid_spec=pltpu.PrefetchScalarGridSpec(
            num_scalar_prefetch=1, grid=(n, nb),
            in_specs=[pl.BlockSpec((None, tile, cols), lambda j, i, pr: (j, pr[1] * nb + i, 0)),
                      pl.BlockSpec((None, tile, cols), lambda j, i, pr: (j, i, 0))],
            out_specs=pl.BlockSpec((None, tile, cols), lambda j, i, pr: (j, i, 0))),
        compiler_params=_cparams(("parallel", "parallel")),
    )(place, slot, received)


def _sum_chips(place, slot, received, others, name):
    _, rows2, cols = slot.shape
    rows = rows2 // 2
    tile = _half_tile(rows)
    nb = rows // tile

    def body(place_ref, a_ref, b_ref, p_ref, o_ref):
        own = a_ref[...] + b_ref[...]
        o_ref[...] = ((own + p_ref[0].astype(F32)) + p_ref[1].astype(F32)) + p_ref[2].astype(F32)

    return pl.pallas_call(
        body, name=name, out_shape=jax.ShapeDtypeStruct((rows, cols), F32),
        grid_spec=pltpu.PrefetchScalarGridSpec(
            num_scalar_prefetch=1, grid=(nb,),
            in_specs=[pl.BlockSpec((None, tile, cols), lambda i, pr: (pr[0], pr[1] * nb + i, 0)),
                      pl.BlockSpec((None, tile, cols), lambda i, pr: (pr[0], i, 0)),
                      pl.BlockSpec((3, tile, cols), lambda i, pr: (0, i, 0))],
            out_specs=pl.BlockSpec((tile, cols), lambda i, pr: (i, 0))),
        compiler_params=_cparams(("parallel",)),
    )(place, slot, received, others)


N_DEVICES = 8


def _sum_devices(block, name):
    def body(v_ref, o_ref, land_ref, send_sems, recv_sems):
        x, y, c = _place()
        me = 4 * x + 2 * y + c
        copies = []
        for mask in range(1, N_DEVICES):
            peer = (x ^ (mask >> 2), y ^ ((mask >> 1) & 1), c ^ (mask & 1))
            copies.append(pltpu.make_async_remote_copy(src_ref=v_ref, dst_ref=land_ref.at[me], send_sem=send_sems.at[mask - 1],
                                                       recv_sem=recv_sems.at[mask - 1], device_id=peer, device_id_type=MESH))
        for cp in copies:
            cp.start()
        land_ref[me] = v_ref[...]
        for cp in copies:
            cp.wait_recv()
        total = land_ref[0]
        for d in range(1, N_DEVICES):
            total = total + land_ref[d]
        o_ref[...] = total
        for cp in copies:
            cp.wait_send()

    vmem = pl.BlockSpec(memory_space=pltpu.VMEM)
    return pl.pallas_call(
        body, name=name, in_specs=[vmem], out_specs=vmem, out_shape=jax.ShapeDtypeStruct(block.shape, F32),
        scratch_shapes=[pltpu.VMEM((N_DEVICES,) + block.shape, F32), pltpu.SemaphoreType.DMA((N_DEVICES - 1,)),
                        pltpu.SemaphoreType.DMA((N_DEVICES - 1,))],
    )(block)


def _vec_block(g_mix, g_mlp, g_ple, g_final, b_forget, b_gate_rows, last=None):
    pad = lambda a: jnp.concatenate([a, jnp.zeros((a.shape[0], D_MODEL - a.shape[1]), F32)], axis=1)
    last = jnp.zeros((1, 0), F32) if last is None else last
    return jnp.concatenate([g_mix, g_mlp, g_ple, g_final.reshape(1, D_MODEL), pad(b_forget), pad(b_gate_rows), pad(last)],
                           axis=0)


def _adam_math(w, g, m, v):
    m_new = ADAM_B1 * m + (1.0 - ADAM_B1) * g
    v_new = ADAM_B2 * v + (1.0 - ADAM_B2) * (g * g)
    m_hat = m_new / (1.0 - ADAM_B1 ** ADAM_STEP)
    v_hat = v_new / (1.0 - ADAM_B2 ** ADAM_STEP)
    return -ADAM_LR * (m_hat / (jnp.sqrt(v_hat) + ADAM_EPS) + ADAM_WD * w), m_new, v_new


def _adamw_halves(place, w, m, v, g_mine, g_theirs, name):
    rows2, cols = w.shape
    rows = rows2 // 2
    tile = _half_tile(rows)
    nb = rows // tile

    def body(place_ref, w_ref, m_ref, v_ref, gm_ref, gt_ref, g_ref, d_ref, nm_ref, nv_ref):
        g = jnp.where(pl.program_id(0) == 0, gm_ref[...], gt_ref[...])
        g_ref[...] = g
        d_ref[...], nm_ref[...], nv_ref[...] = _adam_math(w_ref[...], g, m_ref[...], v_ref[...])

    whole = pl.BlockSpec((tile, cols), lambda s, i, pr: ((pr[1] + s - 2 * pr[1] * s) * nb + i, 0))
    half = pl.BlockSpec((tile, cols), lambda s, i, pr: (i, 0))
    return pl.pallas_call(
        body, name=name, out_shape=[jax.ShapeDtypeStruct((rows2, cols), F32)] * 4,
        grid_spec=pltpu.PrefetchScalarGridSpec(num_scalar_prefetch=1, grid=(2, nb), in_specs=[whole] * 3 + [half] * 2,
                                               out_specs=[whole] * 4),
        compiler_params=_cparams(("parallel", "parallel")),
    )(place, w, m, v, g_mine, g_theirs)


def _adamw_vec(w, g, m, v):
    def body(w_ref, g_ref, m_ref, v_ref, d_ref, nm_ref, nv_ref):
        d_ref[...], nm_ref[...], nv_ref[...] = _adam_math(w_ref[...], g_ref[...], m_ref[...], v_ref[...])

    return pl.pallas_call(body, name="adamw_vectors", out_shape=[jax.ShapeDtypeStruct(w.shape, F32)] * 3)(w, g, m, v)


WEIGHT_NAMES = ("g_mix", "w_in", "b_forget", "b_gate", "w_branch_fox", "w_branch_sb", "w_out", "g_mlp", "w_up", "w_down",
                "g_ple", "w_ple_gate", "w_ple", "g_final")
W_IN_SHARD = D_IN // N_CHIPS
Q_END, F_END, B_END = 3 * D_ATT, 3 * D_ATT + N_HEADS, 6 * D_ATT + N_HEADS
GATE_SHARD = D_MODEL // N_CHIPS


def _join_cols(slots):
    return jnp.transpose(slots, (1, 0, 2)).reshape(slots.shape[1], N_CHIPS * slots.shape[2])


LATE = SHARDED[:1]
EARLY = SHARDED[1:]


def _first_weights(w_in_slots, b_gate):
    w_in = _join_cols(w_in_slots)
    forget = jnp.concatenate([w_in[:, Q_END:F_END], jnp.zeros((D_MODEL, F_PAD - N_HEADS), BF16)], axis=1)
    return {"qkv": jnp.concatenate([w_in[:, :Q_END], w_in[:, F_END:B_END]], axis=1), "gate": w_in[:, B_END:], "forget": forget,
            "b_gate": b_gate}


def _rest_weights(gathered):
    rows = lambda a: a.reshape(N_CHIPS * a.shape[1], a.shape[2])
    return {"branch_fox": gathered["w_branch_fox"], "branch_sb": gathered["w_branch_sb"], "out": rows(gathered["w_out"]),
            "up": gathered["w_up"], "down": rows(gathered["w_down"]), "ple_gate": rows(gathered["w_ple_gate"]),
            "ple": gathered["w_ple"]}


def _early_slots(gw):
    rows = lambda a: a.reshape(N_CHIPS, a.shape[0] // N_CHIPS, a.shape[1])
    return {"w_branch_fox": gw["branch_fox"], "w_branch_sb": gw["branch_sb"], "w_out": rows(gw["out"]), "w_up": gw["up"],
            "w_down": rows(gw["down"]), "w_ple_gate": rows(gw["ple_gate"]), "w_ple": gw["ple"]}


W_IN_FLAT = (W_IN_SHARD * D_MODEL // LANES, LANES)


def _w_in_slots(gw):
    g_t = jnp.concatenate([gw["qkv"][:Q_END], gw["forget"][:N_HEADS], gw["qkv"][Q_END:], gw["gate"]], axis=0)
    return g_t.reshape((N_CHIPS,) + W_IN_FLAT)


def _flat(a):
    return jnp.transpose(a, (2, 0, 1)).reshape(W_IN_FLAT)


def _unflat(a):
    return jnp.transpose(a.reshape(W_IN_SHARD, D_MODEL // LANES, LANES), (1, 2, 0)).reshape(1, D_MODEL, W_IN_SHARD)


def kernel(x, p, g_mix, w_in, b_forget, b_gate, w_branch_fox, w_branch_sb, w_out, g_mlp, w_up, w_down, g_ple, w_ple_gate, w_ple, g_final, loss_target, m_g_mix, m_w_in, m_b_forget, m_b_gate, m_w_branch_fox, m_w_branch_sb, m_w_out, m_g_mlp, m_w_up, m_w_down, m_g_ple, m_w_ple_gate, m_w_ple, m_g_final, v_g_mix, v_w_in, v_b_forget, v_b_gate, v_w_branch_fox, v_w_branch_sb, v_w_out, v_g_mlp, v_w_up, v_w_down, v_g_ple, v_w_ple_gate, v_w_ple, v_g_final):
    weights = dict(g_mix=g_mix, w_in=w_in, b_forget=b_forget, b_gate=b_gate, w_branch_fox=w_branch_fox,
                   w_branch_sb=w_branch_sb, w_out=w_out, g_mlp=g_mlp, w_up=w_up, w_down=w_down, g_ple=g_ple,
                   w_ple_gate=w_ple_gate, w_ple=w_ple, g_final=g_final)
    first = dict(g_mix=m_g_mix, w_in=m_w_in, b_forget=m_b_forget, b_gate=m_b_gate, w_branch_fox=m_w_branch_fox,
                 w_branch_sb=m_w_branch_sb, w_out=m_w_out, g_mlp=m_g_mlp, w_up=m_w_up, w_down=m_w_down, g_ple=m_g_ple,
                 w_ple_gate=m_w_ple_gate, w_ple=m_w_ple, g_final=m_g_final)
    second = dict(g_mix=v_g_mix, w_in=v_w_in, b_forget=v_b_forget, b_gate=v_b_gate, w_branch_fox=v_w_branch_fox,
                  w_branch_sb=v_w_branch_sb, w_out=v_w_out, g_mlp=v_g_mlp, w_up=v_w_up, w_down=v_w_down, g_ple=v_g_ple,
                  w_ple_gate=v_w_ple_gate, w_ple=v_w_ple, g_final=v_g_final)
    cx, cy, cc = _place()
    chip = 2 * cx + cy
    place = jnp.stack([chip, cc]).astype(jnp.int32)
    col0 = chip * GATE_SHARD

    (w_in_slots,) = _run_exchange(_gather_weights([weights[n][0].astype(BF16) for n in LATE]), "gather_w_in")
    rest = _gather_weights([weights[n][0].astype(BF16) for n in EARLY])
    gate_rows = lax.dynamic_update_slice(jnp.zeros((2, D_MODEL), F32), b_gate[0] * (cc == 0).astype(F32), (0, col0))
    zero_row = jnp.zeros((1, D_MODEL), F32)
    b_gate_whole = _sum_devices(_vec_block(zero_row, zero_row, zero_row, zero_row[0], zero_row[:, :N_HEADS], gate_rows),
                                "gather_b_gate")[5:7]
    vec = {"g_mix": g_mix, "b_forget": jnp.concatenate([b_forget, jnp.zeros((1, F_PAD - N_HEADS), F32)], axis=1),
           "g_mlp": g_mlp, "g_ple": g_ple, "g_final": g_final.reshape(1, D_MODEL)}

    loss, grad_x, reduced, gvec = _local_step(x, p[0], loss_target, _first_weights(w_in_slots, b_gate_whole), rest, vec,
                                              place)

    out = {}
    for n in EARLY:
        g_mine, g_theirs = reduced[n]
        res = _adamw_halves(place, weights[n][0], first[n][0], second[n][0], g_mine, g_theirs, "adamw_" + n)
        out[n] = [r[None] for r in res]
    g_mine, g_theirs = reduced["w_in"]
    out["w_in"] = [_unflat(r) for r in _adamw_halves(place, _flat(w_in), _flat(m_w_in), _flat(v_w_in), g_mine, g_theirs,
                                                      "adamw_w_in")]

    g_block = _sum_devices(_vec_block(gvec["g_mix"], gvec["g_mlp"], gvec["g_ple"], gvec["g_final"][0], gvec["b_forget"],
                                      gvec["b_gate"], loss), "reduce_vectors")
    loss = g_block[7, 0]
    g_gate = lax.dynamic_slice(g_block[5:7], (0, col0), (2, GATE_SHARD))
    blocks = [_vec_block(d["g_mix"], d["g_mlp"], d["g_ple"], d["g_final"], d["b_forget"], d["b_gate"][0])
              for d in (weights, first, second)]
    g_rows = jnp.concatenate([g_block[0:5], jnp.concatenate([g_gate, jnp.zeros((2, D_MODEL - GATE_SHARD), F32)], axis=1),
                              jnp.zeros((1, D_MODEL), F32)], axis=0)
    res = (g_rows,) + tuple(_adamw_vec(blocks[0], g_rows, blocks[1], blocks[2]))
    out["g_mix"] = [r[0:1] for r in res]
    out["g_mlp"] = [r[1:2] for r in res]
    out["g_ple"] = [r[2:3] for r in res]
    out["g_final"] = [r[3] for r in res]
    out["b_forget"] = [r[4:5, :N_HEADS] for r in res]
    out["b_gate"] = [r[5:7, :GATE_SHARD][None] for r in res]
    return (loss, grad_x, *[out[n][0] for n in WEIGHT_NAMES], *[out[n][1] for n in WEIGHT_NAMES],
            *[out[n][2] for n in WEIGHT_NAMES], *[out[n][3] for n in WEIGHT_NAMES])
```

```python
import jax
import jax.numpy as jnp
from jax import lax
from jax.experimental import pallas as pl
from jax.experimental.pallas import tpu as pltpu

F32 = jnp.float32
BF16 = jnp.bfloat16

D_MODEL = 1024
HEAD_DIM = 64
N_HEADS = 8
D_ATT = N_HEADS * HEAD_DIM
D_FF = 4 * D_MODEL
D_PLE = 256
D_IN = 6 * D_ATT + N_HEADS + 2 * D_MODEL
F_PAD = 128
EPS = 1e-6
SCALE = HEAD_DIM ** -0.5
N_CHIPS = 4
LANES = 128
ATT_BLOCK = 256
FOX_TILES = (512, 512)
SB_TILES = (512, 256)
NEG = -1e30

ADAM_LR = 0.001
ADAM_B1 = 0.9
ADAM_B2 = 0.999
ADAM_EPS = 1e-08
ADAM_WD = 0.01
ADAM_STEP = 10

VMEM_LIMIT = 56 * 1024 * 1024

MESH = pl.DeviceIdType.MESH


def _cparams(sem=None):
    return pltpu.CompilerParams(dimension_semantics=sem, vmem_limit_bytes=VMEM_LIMIT)


def _relu2(t):
    t = t.astype(F32)
    return t * t


_DIMS = {"nn": (((1,), (0,)), ((), ())), "nt": (((1,), (1,)), ((), ())), "tn": (((0,), (0,)), ((), ()))}
NT_DIMS = _DIMS["nt"]
TN_DIMS = _DIMS["tn"]


def _mm(a, b, *, mode, name, out_dtype=F32, tm=512, tn=512, tk=512, add=None, a_fn=None, epi=None, extra=None,
        col_shards=False, behind=None):
    if mode == "nn":
        (m, k), n = a.shape, b.shape[-1]
    elif mode == "nt":
        (m, k), n = a.shape, b.shape[-2]
    else:
        (k, m), n = a.shape, b.shape[1]
    shard = None
    if col_shards:
        if mode == "nn":
            shard, n = n, N_CHIPS * n
            tn = min(tn, shard)
        elif mode == "nt":
            shard = b.shape[-1]
            tk = min(tk, shard)
        else:
            shard = n // N_CHIPS
            tn = min(tn, shard)
    tm, tn, tk = min(tm, m), min(tn, n), min(tk, k)
    assert m % tm == 0 and n % tn == 0 and k % tk == 0, (name, m, n, k)
    nk = k // tk
    a_spec = {"nn": pl.BlockSpec((tm, tk), lambda i, j, kk: (i, kk)),
              "nt": pl.BlockSpec((tm, tk), lambda i, j, kk: (i, kk)),
              "tn": pl.BlockSpec((tk, tm), lambda i, j, kk: (kk, i))}[mode]
    b_spec = {"nn": pl.BlockSpec((tk, tn), lambda i, j, kk: (kk, j)),
              "nt": pl.BlockSpec((tn, tk), lambda i, j, kk: (j, kk)),
              "tn": pl.BlockSpec((tk, tn), lambda i, j, kk: (kk, j))}[mode]
    o_spec = pl.BlockSpec((tm, tn), lambda i, j, kk: (i, j))
    out_shape = (m, n)
    if col_shards and mode == "nn":
        per = shard // tn
        b_spec = pl.BlockSpec((None, tk, tn), lambda i, j, kk: (j // per, kk, j % per))
    elif col_shards and mode == "nt":
        per = shard // tk
        b_spec = pl.BlockSpec((None, tn, tk), lambda i, j, kk: (kk // per, j, kk % per))
    elif col_shards:
        assert add is None and extra is None
        per = shard // tn
        o_spec = pl.BlockSpec((None, tm, tn), lambda i, j, kk: (j // per, i, j % per))
        out_shape = (N_CHIPS, m, shard)
    operands, in_specs = [a, b], [a_spec, b_spec]
    third = add if add is not None else extra
    if third is not None:
        operands.append(third)
        in_specs.append(o_spec)

    def body(*refs):
        a_ref, b_ref = refs[0], refs[1]
        t_ref = refs[2] if third is not None else None
        o_ref = refs[3] if third is not None else refs[2]
        acc_ref = refs[-1] if nk > 1 else None
        at = a_ref[...]
        if a_fn is not None:
            at = a_fn(at)
        part = lax.dot_general(at.astype(BF16), b_ref[...].astype(BF16), _DIMS[mode], preferred_element_type=F32)

        def finish(acc):
            if epi is not None:
                acc = epi(acc, None if t_ref is None else t_ref[...])
            elif add is not None:
                acc = acc + t_ref[...].astype(F32)
            o_ref[...] = acc.astype(o_ref.dtype)

        if nk == 1:
            finish(part)
        else:
            kk = pl.program_id(2)

            @pl.when(kk == 0)
            def _():
                acc_ref[...] = part

            @pl.when(kk > 0)
            def _():
                acc_ref[...] += part

            @pl.when(kk == nk - 1)
            def _():
                finish(acc_ref[...])

    call = dict(name=name, grid=(m // tm, n // tn, nk), in_specs=in_specs,
                scratch_shapes=[pltpu.VMEM((tm, tn), F32)] if nk > 1 else [])
    if behind is not None:
        (res,), exchanged = _call_behind(body, behind, out_specs=[o_spec], out_shape=[jax.ShapeDtypeStruct(out_shape, out_dtype)],
                                         operands=operands, **call)
        return res, exchanged
    return pl.pallas_call(body, out_specs=o_spec, out_shape=jax.ShapeDtypeStruct(out_shape, out_dtype),
                          compiler_params=_cparams(("parallel", "parallel", "arbitrary")), **call)(*operands)


ROW_TILE = 512


def _row_spec(width=D_MODEL, rows=ROW_TILE):
    return pl.BlockSpec((rows, width), lambda i: (i, 0))


def _vec_spec(rows=1, width=D_MODEL):
    return pl.BlockSpec((rows, width), lambda i: (0, 0))


def _xhat(x):
    r = lax.rsqrt(jnp.mean(x * x, axis=-1, keepdims=True) + EPS)
    return x * r, r


def _rms_bwd_rows(dh, x, g):
    xh, r = _xhat(x)
    dxh = dh * g
    dx = r * (dxh - xh * jnp.mean(dxh * xh, axis=-1, keepdims=True))
    return dx, jnp.sum(dh * xh, axis=0, keepdims=True)


def _norm_fwd(x, g, name):
    t = x.shape[0]

    def body(x_ref, g_ref, h_ref):
        xh, _ = _xhat(x_ref[...])
        h_ref[...] = (xh * g_ref[...]).astype(BF16)

    return pl.pallas_call(
        body, name=name, grid=(t // ROW_TILE,), in_specs=[_row_spec(), _vec_spec()], out_specs=_row_spec(),
        out_shape=jax.ShapeDtypeStruct((t, D_MODEL), BF16), compiler_params=_cparams(("parallel",)),
    )(x, g)


def _norm_bwd(x, g, dh, dres, name):
    t = x.shape[0]

    def body(x_ref, g_ref, dh_ref, dres_ref, dx_ref, dxb_ref, dg_ref):
        dx, dg = _rms_bwd_rows(dh_ref[...], x_ref[...], g_ref[...])
        dx = dx + dres_ref[...]
        dx_ref[...] = dx
        dxb_ref[...] = dx.astype(BF16)

        @pl.when(pl.program_id(0) == 0)
        def _():
            dg_ref[...] = jnp.zeros_like(dg_ref)

        dg_ref[...] += dg

    return pl.pallas_call(
        body, name=name, grid=(t // ROW_TILE,),
        in_specs=[_row_spec(), _vec_spec(), _row_spec(), _row_spec()],
        out_specs=[_row_spec(), _row_spec(), _vec_spec()],
        out_shape=[jax.ShapeDtypeStruct((t, D_MODEL), F32), jax.ShapeDtypeStruct((t, D_MODEL), BF16),
                   jax.ShapeDtypeStruct((1, D_MODEL), F32)],
        compiler_params=_cparams(("arbitrary",)),
    )(x, g, dh, dres)


def _mm_res_norm(a, b, res, g, name, a_fn=None):
    t, k = a.shape

    def body(a_ref, b_ref, res_ref, g_ref, x_ref, h_ref):
        at = a_ref[...] if a_fn is None else a_fn(a_ref[...])
        x_new = res_ref[...] + _dot(at.astype(BF16), b_ref[...])
        x_ref[...] = x_new
        h_ref[...] = (_xhat(x_new)[0] * g_ref[...]).astype(BF16)

    return pl.pallas_call(
        body, name=name, grid=(t // ROW_TILE,),
        in_specs=[pl.BlockSpec((ROW_TILE, k), lambda i: (i, 0)), pl.BlockSpec(b.shape, lambda i: (0, 0)), _row_spec(), _vec_spec()],
        out_specs=[_row_spec(), _row_spec()],
        out_shape=[jax.ShapeDtypeStruct((t, D_MODEL), F32), jax.ShapeDtypeStruct((t, D_MODEL), BF16)],
        compiler_params=_cparams(("parallel",)),
    )(a, b, res, g)


def _mm_norm_bwd(pairs, dh_first, x, g, dres, name, behind=None):
    t = x.shape[0]
    operands, in_specs = [], []
    for a, b in pairs:
        if b.ndim == 3:
            for j in range(b.shape[0]):
                operands += [a, b]
                in_specs += [pl.BlockSpec((ROW_TILE, b.shape[2]), lambda i, j=j: (i, j)),
                             pl.BlockSpec((None, D_MODEL, b.shape[2]), lambda i, j=j: (j, 0, 0))]
        else:
            operands += [a, b]
            in_specs += [pl.BlockSpec((ROW_TILE, a.shape[1]), lambda i: (i, 0)), pl.BlockSpec(b.shape, lambda i: (0, 0))]
    n_mm = len(operands)
    operands += [x, g, dres] + ([] if dh_first is None else [dh_first])
    in_specs += [_row_spec(), _vec_spec(), _row_spec()] + ([] if dh_first is None else [_row_spec()])

    def body(*refs):
        x_ref, g_ref, dres_ref = refs[n_mm:n_mm + 3]
        dx_ref, dxb_ref, dg_ref = refs[-3:]
        dh = 0.0 if dh_first is None else refs[n_mm + 3][...]
        for k in range(0, n_mm, 2):
            dh = dh + lax.dot_general(refs[k][...].astype(BF16), refs[k + 1][...].astype(BF16), NT_DIMS,
                                      preferred_element_type=F32)
        dx, dg = _rms_bwd_rows(dh, x_ref[...], g_ref[...])
        dx = dx + dres_ref[...]
        dx_ref[...] = dx
        dxb_ref[...] = dx.astype(BF16)

        @pl.when(pl.program_id(0) == 0)
        def _():
            dg_ref[...] = jnp.zeros_like(dg_ref)

        dg_ref[...] += dg

    call = dict(name=name, grid=(t // ROW_TILE,), in_specs=in_specs, out_specs=[_row_spec(), _row_spec(), _vec_spec()],
                out_shape=[jax.ShapeDtypeStruct((t, D_MODEL), F32), jax.ShapeDtypeStruct((t, D_MODEL), BF16),
                           jax.ShapeDtypeStruct((1, D_MODEL), F32)])
    if behind is not None:
        return _call_behind(body, behind, scratch_shapes=[], operands=operands, **call)
    return pl.pallas_call(body, compiler_params=_cparams(("arbitrary",)), **call)(*operands)


def _shards_spec(w):
    return pl.BlockSpec(w.shape, lambda i: (0, 0, 0))


def _gate_fwd(gl, b_gate, o_fox, o_sb, w_fox, w_sb):
    t = o_fox.shape[0]

    def body(gla_ref, glb_ref, b_ref, ofox_ref, osb_ref, wf_ref, ws_ref, m_ref, of_ref, os_ref):
        of = jnp.concatenate([_dot(ofox_ref[...], wf_ref[j]) for j in range(N_CHIPS)], axis=1)
        os_ = jnp.concatenate([_dot(osb_ref[...], ws_ref[j]) for j in range(N_CHIPS)], axis=1)
        ga = jax.nn.sigmoid(gla_ref[...] + b_ref[0:1, :])
        gb = jax.nn.sigmoid(glb_ref[...] + b_ref[1:2, :])
        of_ref[...] = of
        os_ref[...] = os_
        m_ref[...] = (ga * of + gb * os_).astype(BF16)

    return pl.pallas_call(
        body, name="gate_fwd", grid=(t // ROW_TILE,),
        in_specs=[pl.BlockSpec((ROW_TILE, D_MODEL), lambda i: (i, 0)), pl.BlockSpec((ROW_TILE, D_MODEL), lambda i: (i, 1)),
                  _vec_spec(2), _row_spec(D_ATT), _row_spec(D_ATT), _shards_spec(w_fox), _shards_spec(w_sb)],
        out_specs=[_row_spec(), _row_spec(), _row_spec()],
        out_shape=[jax.ShapeDtypeStruct((t, D_MODEL), BF16)] + [jax.ShapeDtypeStruct((t, D_MODEL), F32)] * 2,
        compiler_params=_cparams(("parallel",)),
    )(gl, gl, b_gate, o_fox, o_sb, w_fox, w_sb)


def _gate_bwd(gl, b_gate, of, os_, dx, w_out, w_fox, w_sb):
    t = of.shape[0]
    shard = D_MODEL // N_CHIPS

    def back(d, w_ref):
        return sum(_dot(d[:, j * shard:(j + 1) * shard], w_ref[j], NT_DIMS) for j in range(N_CHIPS)).astype(BF16)

    def body(gla_ref, glb_ref, b_ref, of_ref, os_ref, dx_ref, w_ref, wf_ref, ws_ref,
             dof_ref, dos_ref, dgla_ref, dglb_ref, db_ref, dofox_ref, dosb_ref):
        dm = _dot(dx_ref[...], w_ref[...], NT_DIMS)
        ga = jax.nn.sigmoid(gla_ref[...] + b_ref[0:1, :])
        gb = jax.nn.sigmoid(glb_ref[...] + b_ref[1:2, :])
        dof = (dm * ga).astype(BF16)
        dos = (dm * gb).astype(BF16)
        dof_ref[...] = dof
        dos_ref[...] = dos
        dofox_ref[...] = back(dof, wf_ref)
        dosb_ref[...] = back(dos, ws_ref)
        dgla = dm * of_ref[...] * ga * (1.0 - ga)
        dglb = dm * os_ref[...] * gb * (1.0 - gb)
        dgla_ref[...] = dgla.astype(BF16)
        dglb_ref[...] = dglb.astype(BF16)

        @pl.when(pl.program_id(0) == 0)
        def _():
            db_ref[...] = jnp.zeros_like(db_ref)

        db_ref[0:1, :] += jnp.sum(dgla, axis=0, keepdims=True)
        db_ref[1:2, :] += jnp.sum(dglb, axis=0, keepdims=True)

    outs = pl.pallas_call(
        body, name="gate_bwd", grid=(t // ROW_TILE,),
        in_specs=[pl.BlockSpec((ROW_TILE, D_MODEL), lambda i: (i, 0)), pl.BlockSpec((ROW_TILE, D_MODEL), lambda i: (i, 1)),
                  _vec_spec(2), _row_spec(), _row_spec(), _row_spec(), pl.BlockSpec(w_out.shape, lambda i: (0, 0)),
                  _shards_spec(w_fox), _shards_spec(w_sb)],
        out_specs=[_row_spec(), _row_spec(), _row_spec(), _row_spec(), _vec_spec(2), _row_spec(D_ATT), _row_spec(D_ATT)],
        out_shape=[jax.ShapeDtypeStruct((t, D_MODEL), BF16)] * 4 + [jax.ShapeDtypeStruct((2, D_MODEL), F32)]
        + [jax.ShapeDtypeStruct((t, D_ATT), BF16)] * 2,
        compiler_params=_cparams(("arbitrary",)),
    )(gl, gl, b_gate, of, os_, dx, w_out, w_fox, w_sb)
    return outs


def _head_and_loss(x2, h3, p, w_gate, w_ple, g_final, target):
    t = x2.shape[0]

    def body(x2_ref, h3_ref, p_ref, wg_ref, wp_ref, g_ref, tgt_ref, dx3_ref, dpre_ref, dpe_ref, dg_ref, loss_ref):
        gp = jax.nn.sigmoid(_dot(h3_ref[...], wg_ref[...]))
        p_t = p_ref[...].astype(BF16)
        pe_t = jnp.concatenate([_dot(p_t, wp_ref[j]) for j in range(N_CHIPS)], axis=1)
        x3 = x2_ref[...] + gp * pe_t
        g = g_ref[...]
        xh, _ = _xhat(x3)
        err = xh * g - tgt_ref[...]
        dy = err * (1.0 / D_MODEL)
        dx3, dg = _rms_bwd_rows(dy, x3, g)
        dx3_ref[...] = dx3
        dpre_ref[...] = (dx3 * pe_t * gp * (1.0 - gp)).astype(BF16)
        dpe_ref[...] = (dx3 * gp).astype(BF16)

        @pl.when(pl.program_id(0) == 0)
        def _():
            dg_ref[...] = jnp.zeros_like(dg_ref)
            loss_ref[...] = jnp.zeros_like(loss_ref)

        dg_ref[...] += dg
        loss_ref[...] += 0.5 * jnp.sum(jnp.mean(err * err, axis=-1, keepdims=True), axis=0, keepdims=True)

    return pl.pallas_call(
        body, name="head_and_loss", grid=(t // ROW_TILE,),
        in_specs=[_row_spec(), _row_spec(), _row_spec(D_PLE), pl.BlockSpec(w_gate.shape, lambda i: (0, 0)),
                  pl.BlockSpec(w_ple.shape, lambda i: (0, 0, 0)), _vec_spec(), _row_spec()],
        out_specs=[_row_spec(), _row_spec(), _row_spec(), _vec_spec(), _vec_spec(1, LANES)],
        out_shape=[jax.ShapeDtypeStruct((t, D_MODEL), F32), jax.ShapeDtypeStruct((t, D_MODEL), BF16),
                   jax.ShapeDtypeStruct((t, D_MODEL), BF16), jax.ShapeDtypeStruct((1, D_MODEL), F32),
                   jax.ShapeDtypeStruct((1, LANES), F32)],
        compiler_params=_cparams(("arbitrary",)),
    )(x2, h3, p, w_gate, w_ple, g_final, target)


def _split3(v):
    hi = v.astype(BF16)
    r1 = v - hi.astype(F32)
    mid = r1.astype(BF16)
    lo = (r1 - mid.astype(F32)).astype(BF16)
    return hi, mid, lo


def _split2(v):
    hi = v.astype(BF16)
    return jnp.concatenate([hi, (v - hi.astype(F32)).astype(BF16)], axis=1)


def _dot(a, b, dims=_DIMS["nn"]):
    return lax.dot_general(a, b, dims, preferred_element_type=F32)


def _tri(n, rel):
    row = lax.broadcasted_iota(jnp.int32, (n, n), 0)
    col = lax.broadcasted_iota(jnp.int32, (n, n), 1)
    return rel(row, col).astype(BF16)


def _tri2(n, rel):
    t = _tri(n, rel)
    return jnp.concatenate([t, t], axis=0)


def _log_sigmoid(v):
    return -(jnp.maximum(-v, 0.0) + jnp.log(1.0 + jnp.exp(-jnp.abs(v))))


def _fox_prep(fl, b_forget, batch, seq):
    nb = seq // ATT_BLOCK

    def body(fl_ref, b_ref, cw_ref, cr_ref):
        col = lax.broadcasted_iota(jnp.int32, (ATT_BLOCK, F_PAD), 1)
        lower = _tri(ATT_BLOCK, lambda r, c: c <= r)
        upper = _tri(ATT_BLOCK, lambda r, c: r <= c)
        expand = (lax.broadcasted_iota(jnp.int32, (F_PAD, D_ATT), 1) // HEAD_DIM
                  == lax.broadcasted_iota(jnp.int32, (F_PAD, D_ATT), 0)).astype(BF16)
        carry_w = jnp.zeros((1, D_ATT), F32)
        carry_r = jnp.zeros((F_PAD, 1), F32)
        for i in range(nb):
            blk = slice(i * ATT_BLOCK, (i + 1) * ATT_BLOCK)
            logf = jnp.where(col < N_HEADS, _log_sigmoid(fl_ref[blk, :] + b_ref[...]), 0.0)
            cw = jnp.zeros((ATT_BLOCK, D_ATT), F32) + carry_w
            cr = jnp.zeros((F_PAD, ATT_BLOCK), F32) + carry_r
            for part in _split3(logf):
                cw += _dot(lower, _dot(part, expand).astype(BF16))
                cr += _dot(part, upper, TN_DIMS)
            cw_ref[blk, :] = cw
            cr_ref[:, blk] = cr[0:N_HEADS, :]
            carry_w = cw[ATT_BLOCK - 1:ATT_BLOCK, :]
            carry_r = cr[:, ATT_BLOCK - 1:ATT_BLOCK]

    return pl.pallas_call(
        body, name="fox_prep", grid=(batch,),
        in_specs=[pl.BlockSpec((seq, F_PAD), lambda b: (b, 0)), pl.BlockSpec((1, F_PAD), lambda b: (0, 0))],
        out_specs=[pl.BlockSpec((seq, D_ATT), lambda b: (b, 0)), pl.BlockSpec((N_HEADS, seq), lambda b: (b, 0))],
        out_shape=[jax.ShapeDtypeStruct((batch * seq, D_ATT), F32), jax.ShapeDtypeStruct((batch * N_HEADS, seq), F32)],
        compiler_params=_cparams(("parallel",)),
    )(fl, b_forget)


def _fox_post(dcs_wide, drs_wide, fl, b_forget, batch, seq):
    nb = seq // ATT_BLOCK

    def body(dcs_ref, drs_ref, fl_ref, b_ref, dfl_ref, db_ref):
        pick = (lax.broadcasted_iota(jnp.int32, (D_ATT, F_PAD), 0)
                == lax.broadcasted_iota(jnp.int32, (D_ATT, F_PAD), 1) * HEAD_DIM).astype(BF16)
        upper = _tri(ATT_BLOCK, lambda r, c: r <= c)
        col = lax.broadcasted_iota(jnp.int32, (ATT_BLOCK, F_PAD), 1)

        @pl.when(pl.program_id(0) == 0)
        def _():
            db_ref[...] = jnp.zeros_like(db_ref)

        carry = jnp.zeros((1, F_PAD), F32)
        for i in reversed(range(nb)):
            blk = slice(i * ATT_BLOCK, (i + 1) * ATT_BLOCK)
            narrow = jnp.zeros((ATT_BLOCK, F_PAD), F32)
            for part in _split3(drs_ref[blk, :] - dcs_ref[blk, :]):
                narrow += _dot(part, pick)
            after = jnp.zeros((ATT_BLOCK, F_PAD), F32) + carry
            for part in _split3(narrow):
                after += _dot(upper, part)
            carry = after[0:1, :]
            pre = fl_ref[blk, :] + b_ref[...]
            dfl = jnp.where(col < N_HEADS, after * jax.nn.sigmoid(-pre), 0.0)
            dfl_ref[blk, :] = dfl.astype(BF16)
            db_ref[...] += jnp.sum(dfl, axis=0, keepdims=True)

    return pl.pallas_call(
        body, name="fox_post", grid=(batch,),
        in_specs=[pl.BlockSpec((seq, D_ATT), lambda b: (b, 0)), pl.BlockSpec((seq, D_ATT), lambda b: (b, 0)),
                  pl.BlockSpec((seq, F_PAD), lambda b: (b, 0)), pl.BlockSpec((1, F_PAD), lambda b: (0, 0))],
        out_specs=[pl.BlockSpec((seq, F_PAD), lambda b: (b, 0)), pl.BlockSpec((1, F_PAD), lambda b: (0, 0))],
        out_shape=[jax.ShapeDtypeStruct((batch * seq, F_PAD), BF16), jax.ShapeDtypeStruct((1, F_PAD), F32)],
        compiler_params=_cparams(("arbitrary",)),
    )(dcs_wide, drs_wide, fl, b_forget)


N_PAIRS = N_HEADS // 2


def _att_specs(seq, col0, tq):
    nq = seq // tq
    q = pl.BlockSpec((tq, LANES), lambda b, hp, qi: (b * nq + qi, col0 + hp))
    k = pl.BlockSpec((seq, LANES), lambda b, hp, qi: (b, col0 + N_PAIRS + hp))
    v = pl.BlockSpec((seq, LANES), lambda b, hp, qi: (b, col0 + 2 * N_PAIRS + hp))
    return q, k, v


def _qblock_spec(seq, tq):
    nq = seq // tq
    return pl.BlockSpec((tq, LANES), lambda b, hp, qi: (b * nq + qi, hp))


def _kv_out_spec(seq):
    return pl.BlockSpec((seq, LANES), lambda b, hp, qi: (b, hp))


def _head_masks():
    lane = lax.broadcasted_iota(jnp.int32, (1, LANES), 1)
    return [(lane >= HEAD_DIM * j) & (lane < HEAD_DIM * (j + 1)) for j in range(2)]


def _stack_heads(t, masks):
    zero = jnp.zeros_like(t)
    return jnp.concatenate([jnp.where(masks[0], t, zero), jnp.where(masks[1], t, zero)], axis=0)


def _stack_cols(t):
    return jnp.concatenate([t[:, 0:1], t[:, HEAD_DIM:HEAD_DIM + 1]], axis=0)


def _unstack(t2, masks):
    tq = t2.shape[0] // 2
    return jnp.where(masks[0], t2[:tq], t2[tq:])


def _stacked_ids(tq, tk):
    row = lax.broadcasted_iota(jnp.int32, (2 * tq, tk), 0)
    col = lax.broadcasted_iota(jnp.int32, (2 * tq, tk), 1)
    first = lax.broadcasted_iota(jnp.int32, (2 * tq, 1), 0) < tq
    return col - jnp.where(row < tq, row, row - tq), first


def _sweep(qi, tq, tk, step, init, leftward):
    per = tq // tk
    whole = lambda carry: lax.fori_loop(0, per * qi, lambda i, c: step(per * qi - 1 - i if leftward else i, c, None), carry)
    crossed = [(per * qi + j, -j * tk) for j in range(per)]
    if leftward:
        carry = init
        for kb, lead in reversed(crossed):
            carry = step(kb, carry, lead)
        return whole(carry)
    carry = whole(init)
    for kb, lead in crossed:
        carry = step(kb, carry, lead)
    return carry


def _fox_fwd(qkv, c_wide, c_row, batch, seq):
    tq, tk = FOX_TILES
    nq = seq // tq

    def body(q_ref, k_ref, v_ref, cw_ref, cr_ref, o_ref, lse_ref):
        hp, qi = pl.program_id(1), pl.program_id(2)
        masks = _head_masks()
        ahead, first = _stacked_ids(tq, tk)
        q2 = _stack_heads(q_ref[...], masks) * SCALE
        ct = _stack_cols(cw_ref[...])

        def step(kb, carry, lead):
            m, l, acc = carry
            k0 = pl.multiple_of(kb * tk, tk)
            cs = jnp.where(first, cr_ref[pl.ds(2 * hp, 1), pl.ds(k0, tk)], cr_ref[pl.ds(2 * hp + 1, 1), pl.ds(k0, tk)])
            s = _dot(q2, k_ref[pl.ds(k0, tk), :], NT_DIMS) + ct - cs
            if lead is not None:
                s = jnp.where(ahead <= lead, s, NEG)
            m_new = jnp.maximum(m, jnp.max(s, axis=1, keepdims=True))
            p = jnp.exp(s - m_new)
            alpha = jnp.exp(m - m_new)
            l = alpha * l + jnp.sum(p, axis=1, keepdims=True)
            acc = alpha * acc + _dot(p.astype(BF16), v_ref[pl.ds(k0, tk), :])
            return m_new, l, acc

        init = (jnp.full((2 * tq, 1), NEG, F32), jnp.zeros((2 * tq, 1), F32), jnp.zeros((2 * tq, LANES), F32))
        m, l, acc = _sweep(qi, tq, tk, step, init, leftward=False)
        o_ref[...] = _unstack(acc / l, masks).astype(BF16)
        lse_ref[...] = _unstack(m + jnp.log(l), masks)

    q_spec, k_spec, v_spec = _att_specs(seq, 0, tq)
    qb = _qblock_spec(seq, tq)
    return pl.pallas_call(
        body, name="fox_fwd", grid=(batch, N_PAIRS, nq),
        in_specs=[q_spec, k_spec, v_spec, qb, pl.BlockSpec((N_HEADS, seq), lambda b, hp, qi: (b, 0))],
        out_specs=[qb, qb],
        out_shape=[jax.ShapeDtypeStruct((batch * seq, D_ATT), BF16), jax.ShapeDtypeStruct((batch * seq, D_ATT), F32)],
        compiler_params=_cparams(("parallel", "parallel", "arbitrary")),
    )(qkv, qkv, qkv, c_wide, c_row)


def _fox_bwd(qkv, c_wide, c_row, o, do, lse_wide, batch, seq, behind):
    tq, tk = FOX_TILES
    nq = seq // tq

    def body(q_ref, k_ref, v_ref, cw_ref, cr_ref, o_ref, do_ref, lse_ref,
             dq_ref, dk_ref, dv_ref, dcs_ref, drs_ref, dkc_acc, dv_acc):
        hp, qi = pl.program_id(1), pl.program_id(2)

        @pl.when(qi == 0)
        def _():
            dkc_acc[...] = jnp.zeros_like(dkc_acc)
            dv_acc[...] = jnp.zeros_like(dv_acc)

        masks = _head_masks()
        ahead, first = _stacked_ids(tq, tk)
        q_t, do_t = q_ref[...], do_ref[...]
        q2 = _stack_heads(q_t, masks) * SCALE
        do2 = _stack_heads(do_t, masks)
        q_and_ones = jnp.concatenate([q2, _stack_heads(jnp.ones_like(q_t), masks)], axis=1)
        ct = _stack_cols(cw_ref[...])
        lse = _stack_cols(lse_ref[...])
        prod = do_t.astype(F32) * o_ref[...].astype(F32)
        delta = jnp.concatenate([jnp.sum(jnp.where(mk, prod, 0.0), axis=1, keepdims=True) for mk in masks], axis=0)

        def step(kb, carry, lead):
            dq_acc, rs = carry
            k0 = pl.multiple_of(kb * tk, tk)
            kblk = k_ref[pl.ds(k0, tk), :]
            cs = jnp.where(first, cr_ref[pl.ds(2 * hp, 1), pl.ds(k0, tk)], cr_ref[pl.ds(2 * hp + 1, 1), pl.ds(k0, tk)])
            p = jnp.exp(_dot(q2, kblk, NT_DIMS) + ct - cs - lse)
            if lead is not None:
                p = jnp.where(ahead <= lead, p, 0.0)
            dp = _dot(do2, v_ref[pl.ds(k0, tk), :], NT_DIMS)
            ds = (p * (dp - delta)).astype(BF16)
            dkc_acc[pl.ds(k0, tk), :] += _dot(ds, q_and_ones, TN_DIMS)
            dv_acc[pl.ds(k0, tk), :] += _dot(p.astype(BF16), do2, TN_DIMS)
            return dq_acc + _dot(ds, kblk), rs + jnp.sum(ds.astype(F32), axis=1, keepdims=True)

        init = (jnp.zeros((2 * tq, LANES), F32), jnp.zeros((2 * tq, 1), F32))
        dq_acc, rs = _sweep(qi, tq, tk, step, init, leftward=False)
        dq_ref[...] = (_unstack(dq_acc, masks) * SCALE).astype(BF16)
        drs_ref[...] = _unstack(rs, masks)

        @pl.when(qi == nq - 1)
        def _():
            dk_ref[...] = dkc_acc[:, 0:LANES].astype(BF16)
            dcs_ref[...] = dkc_acc[:, LANES:2 * LANES]
            dv_ref[...] = dv_acc[...].astype(BF16)

    q_spec, k_spec, v_spec = _att_specs(seq, 0, tq)
    qb = _qblock_spec(seq, tq)
    return _call_behind(
        body, behind, name="fox_bwd", grid=(batch, N_PAIRS, nq),
        in_specs=[q_spec, k_spec, v_spec, qb, pl.BlockSpec((N_HEADS, seq), lambda b, hp, qi: (b, 0)), qb, qb, qb],
        out_specs=[qb, _kv_out_spec(seq), _kv_out_spec(seq), _kv_out_spec(seq), qb],
        out_shape=[jax.ShapeDtypeStruct((batch * seq, D_ATT), BF16)] * 3 + [jax.ShapeDtypeStruct((batch * seq, D_ATT), F32)] * 2,
        scratch_shapes=[pltpu.VMEM((seq, 2 * LANES), F32), pltpu.VMEM((seq, LANES), F32)],
        operands=(qkv, qkv, qkv, c_wide, c_row, o, do, lse_wide))


def _sb_logits(q2, kblk):
    z = _dot(q2, kblk, NT_DIMS)
    lsn = jnp.minimum(-z, 0.0) - jnp.log(1.0 + jnp.exp(-jnp.abs(z)))
    return lsn + z, lsn


def _sb_fwd(qkv, batch, seq, behind):
    tq, tk = SB_TILES
    nq = seq // tq

    def body(q_ref, k_ref, v_ref, o_ref, rt_ref):
        qi = pl.program_id(2)
        masks = _head_masks()
        ahead, _ = _stacked_ids(tq, tk)
        later = _tri2(tk, lambda r, c: r > c)
        q2 = _stack_heads(q_ref[...], masks) * SCALE

        def step(kb, carry, lead):
            run, acc = carry
            k0 = pl.multiple_of(kb * tk, tk)
            ls, lsn = _sb_logits(q2, k_ref[pl.ds(k0, tk), :])
            if lead is not None:
                lsn = jnp.where(ahead < lead, lsn, 0.0)
            w = jnp.exp(ls + _dot(_split2(lsn), later) + run)
            if lead is not None:
                w = jnp.where(ahead < lead, w, 0.0)
            return run + jnp.sum(lsn, axis=1, keepdims=True), acc + _dot(w.astype(BF16), v_ref[pl.ds(k0, tk), :])

        init = (jnp.zeros((2 * tq, 1), F32), jnp.zeros((2 * tq, LANES), F32))
        run, acc = _sweep(qi, tq, tk, step, init, leftward=True)
        o_ref[...] = _unstack(acc, masks).astype(BF16)
        rt_ref[...] = _unstack(run, masks)

    q_spec, k_spec, v_spec = _att_specs(seq, 3 * N_PAIRS, tq)
    qb = _qblock_spec(seq, tq)
    return _call_behind(
        body, behind, name="sb_fwd", grid=(batch, N_PAIRS, nq), in_specs=[q_spec, k_spec, v_spec], out_specs=[qb, qb],
        out_shape=[jax.ShapeDtypeStruct((batch * seq, D_ATT), BF16), jax.ShapeDtypeStruct((batch * seq, D_ATT), F32)],
        scratch_shapes=[], operands=(qkv, qkv, qkv))


def _sb_bwd(qkv, do, rt_wide, batch, seq, behind):
    tq, tk = SB_TILES
    nq = seq // tq

    def body(q_ref, k_ref, v_ref, do_ref, rt_ref, dq_ref, dk_ref, dv_ref, dk_acc, dv_acc):
        qi = pl.program_id(2)

        @pl.when(qi == 0)
        def _():
            dk_acc[...] = jnp.zeros_like(dk_acc)
            dv_acc[...] = jnp.zeros_like(dv_acc)

        masks = _head_masks()
        ahead, _ = _stacked_ids(tq, tk)
        later = _tri2(tk, lambda r, c: r > c)
        earlier = _tri2(tk, lambda r, c: r < c)
        q2 = _stack_heads(q_ref[...], masks) * SCALE
        do2 = _stack_heads(do_ref[...], masks)
        total = _stack_cols(rt_ref[...])

        def step(kb, carry, lead):
            pref, epre, dq_acc = carry
            k0 = pl.multiple_of(kb * tk, tk)
            kblk = k_ref[pl.ds(k0, tk), :]
            ls, lsn_all = _sb_logits(q2, kblk)
            lsn = lsn_all if lead is None else jnp.where(ahead < lead, lsn_all, 0.0)
            rs = jnp.sum(lsn, axis=1, keepdims=True)
            w = jnp.exp(ls + _dot(_split2(lsn), later) + (total - pref - rs))
            if lead is not None:
                w = jnp.where(ahead < lead, w, 0.0)
            e = w * _dot(do2, v_ref[pl.ds(k0, tk), :], NT_DIMS)
            before = _dot(_split2(e), earlier) + epre
            dz = e * jnp.exp(lsn_all) - jnp.exp(ls) * before
            if lead is not None:
                dz = jnp.where(ahead < lead, dz, 0.0)
            dz = dz.astype(BF16)
            dk_acc[pl.ds(k0, tk), :] += _dot(dz, q2, TN_DIMS)
            dv_acc[pl.ds(k0, tk), :] += _dot(w.astype(BF16), do2, TN_DIMS)
            return pref + rs, epre + jnp.sum(e, axis=1, keepdims=True), dq_acc + _dot(dz, kblk)

        init = (jnp.zeros((2 * tq, 1), F32), jnp.zeros((2 * tq, 1), F32), jnp.zeros((2 * tq, LANES), F32))
        dq_acc = _sweep(qi, tq, tk, step, init, leftward=False)[2]
        dq_ref[...] = (_unstack(dq_acc, masks) * SCALE).astype(BF16)

        @pl.when(qi == nq - 1)
        def _():
            dk_ref[...] = dk_acc[...].astype(BF16)
            dv_ref[...] = dv_acc[...].astype(BF16)

    q_spec, k_spec, v_spec = _att_specs(seq, 3 * N_PAIRS, tq)
    qb = _qblock_spec(seq, tq)
    return _call_behind(
        body, behind, name="sb_bwd", grid=(batch, N_PAIRS, nq), in_specs=[q_spec, k_spec, v_spec, qb, qb],
        out_specs=[qb, _kv_out_spec(seq), _kv_out_spec(seq)], out_shape=[jax.ShapeDtypeStruct((batch * seq, D_ATT), BF16)] * 3,
        scratch_shapes=[pltpu.VMEM((seq, LANES), F32), pltpu.VMEM((seq, LANES), F32)], operands=(qkv, qkv, qkv, do, rt_wide))


def _local_step(x, p, target, w, rest, vec, place):
    batch, seq, _ = x.shape
    t = batch * seq
    x = x.reshape(t, D_MODEL)
    target = target.reshape(t, D_MODEL)
    p = p.reshape(t, D_PLE)
    big = dict(tm=1024, tn=1024, tk=1024)

    h1 = _norm_fwd(x, vec["g_mix"], "norm_mix")
    qkv = _mm(h1, w["qkv"], mode="nn", name="proj_qkv", out_dtype=BF16, **big)
    gl = _mm(h1, w["gate"], mode="nn", name="proj_gate", **big)
    fl = _mm(h1, w["forget"], mode="nn", name="proj_forget", **big)
    c_wide, c_row = _fox_prep(fl, vec["b_forget"], batch, seq)
    o_fox, lse_wide = _fox_fwd(qkv, c_wide, c_row, batch, seq)
    (o_sb, rt_wide), gathered = _sb_fwd(qkv, batch, seq, rest)
    w = dict(w, **_rest_weights(dict(zip(EARLY + ("b_gate",), gathered))))
    merged, of, os_ = _gate_fwd(gl, w["b_gate"], o_fox, o_sb, w["branch_fox"], w["branch_sb"])
    x1, h2 = _mm_res_norm(merged, w["out"], x, vec["g_mlp"], "proj_out_norm")
    ar = _mm(h2, w["up"], mode="nn", name="mlp_up", out_dtype=BF16, epi=lambda acc, _: jnp.maximum(acc, 0.0),
             col_shards=True, **big)
    x2, h3 = _mm_res_norm(ar, w["down"], x1, vec["g_ple"], "mlp_down_norm", a_fn=_relu2)

    dx3, dpre, dpe, dg_final, loss = _head_and_loss(x2, h3, p, w["ple_gate"], w["ple"], vec["g_final"], target)
    gw = {}
    gw["ple"] = _mm(p, dpe, mode="tn", name="d_w_ple", col_shards=True, **big)
    gw["ple_gate"] = _mm(h3, dpre, mode="tn", name="d_w_ple_gate", **big)
    dx2, dx2b, dg_ple = _mm_norm_bwd([(dpre, w["ple_gate"])], None, x2, vec["g_ple"], dx3, "d_h_ple_norm_bwd")
    gw["down"] = _mm(ar, dx2b, mode="tn", name="d_w_down", a_fn=_relu2, **big)
    da = _mm(dx2b, w["down"], mode="nt", name="d_act", out_dtype=BF16,
             epi=lambda acc, r: acc * (2.0 * r.astype(F32)), extra=ar, **big)
    gw["up"] = _mm(h2, da, mode="tn", name="d_w_up", col_shards=True, **big)
    dx1, dx1b, dg_mlp = _mm_norm_bwd([(da, w["up"])], None, x1, vec["g_mlp"], dx2, "d_h_mlp_norm_bwd")
    gw["out"] = _mm(merged, dx1b, mode="tn", name="d_w_out", **big)
    dof, dos, dgla, dglb, gw["b_gate"], do_fox, do_sb = _gate_bwd(gl, w["b_gate"], of, os_, dx1b, w["out"], w["branch_fox"],
                                                                  w["branch_sb"])
    gw["branch_fox"] = _mm(o_fox, dof, mode="tn", name="d_w_branch_fox", col_shards=True, **big)
    gw["branch_sb"] = _mm(o_sb, dos, mode="tn", name="d_w_branch_sb", col_shards=True, **big)
    early = _early_slots(gw)
    early = [early[n] for n in EARLY]
    (dq_a, dk_a, dv_a, dcs_wide, drs_wide), received = _fox_bwd(qkv, c_wide, c_row, o_fox, do_fox, lse_wide, batch, seq,
                                                                _swap_halves(early))
    sums = [_sum_sibling(place, s, r, "sum_sibling_" + n) for s, r, n in zip(early, received, EARLY)]
    (dq_b, dk_b, dv_b), others = _sb_bwd(qkv, do_sb, rt_wide, batch, seq, _exchange_chips(sums))
    mine = [_sum_chips(place, s, r, o, "sum_chips_" + n) for s, r, o, n in zip(early, received, others, EARLY)]
    dfl, db_forget = _fox_post(dcs_wide, drs_wide, fl, vec["b_forget"], batch, seq)
    dqkv = jnp.concatenate([dq_a, dk_a, dv_a, dq_b, dk_b, dv_b], axis=1)
    dgl = jnp.concatenate([dgla, dglb], axis=1)
    gw["qkv"], theirs = _mm(dqkv, h1, mode="tn", name="d_w_qkv", behind=_share_halves(mine), **big)
    reduced = dict(zip(EARLY, zip(mine, theirs)))
    gw["gate"] = _mm(dgl, h1, mode="tn", name="d_w_gate", **big)
    gw["forget"] = _mm(dfl, h1, mode="tn", name="d_w_forget", **big)
    late = [_w_in_slots(gw)]
    dh1, received = _mm(dqkv, w["qkv"], mode="nt", name="d_h_qkv", behind=_swap_halves(late), **big)
    sums = [_sum_sibling(place, late[0], received[0], "sum_sibling_w_in")]
    (grad_x, _, dg_mix), others = _mm_norm_bwd([(dgl, w["gate"]), (dfl, w["forget"])], dh1, x, vec["g_mix"], dx1,
                                               "d_h_gate_norm_bwd", behind=_exchange_chips(sums))
    mine = [_sum_chips(place, late[0], received[0], others[0], "sum_chips_w_in")]
    reduced["w_in"] = (mine[0], _run_exchange(_share_halves(mine), "reduce_share_w_in")[0])
    gvec = {"g_mix": dg_mix, "b_forget": db_forget[:, 0:N_HEADS], "g_mlp": dg_mlp, "g_ple": dg_ple,
            "g_final": dg_final, "b_gate": gw["b_gate"]}
    return loss, grad_x.reshape(batch, seq, D_MODEL), reduced, gvec


ANY = pl.BlockSpec(memory_space=pl.ANY)
SHARDED = ("w_in", "w_branch_fox", "w_branch_sb", "w_out", "w_up", "w_down", "w_ple_gate", "w_ple")
ROW_ALIGN = 16
F32_ROWS = 8


def _place():
    return lax.axis_index("x"), lax.axis_index("y"), lax.axis_index("c")


def _other_chips(x, y):
    return [(1 - x, y), (x, 1 - y), (1 - x, 1 - y)]


def _half(ref, h):
    r = ref.shape[0] // 2
    assert r % ROW_ALIGN == 0
    return ref.at[pl.ds(pl.multiple_of(h * r, ROW_ALIGN), r)]


def _remote(src, dst, sems, idx, to):
    send_sems, recv_sems = sems
    return pltpu.make_async_remote_copy(src_ref=src, dst_ref=dst, send_sem=send_sems.at[idx], recv_sem=recv_sems.at[idx],
                                        device_id=to, device_id_type=MESH)


class _Exchange:
    def __init__(self, operands, out_shapes, sem_shape, start, finish):
        self.operands, self.out_shapes, self.sem_shape, self.start, self.finish = operands, out_shapes, sem_shape, start, finish

    def scratch(self):
        return [pltpu.SemaphoreType.DMA(self.sem_shape), pltpu.SemaphoreType.DMA(self.sem_shape)]


def _run_exchange(ex, name):
    n = len(ex.operands)

    def body(*refs):
        ex.start(refs[:n], refs[n:2 * n], refs[2 * n:])
        ex.finish(refs[:n], refs[n:2 * n], refs[2 * n:])

    return pl.pallas_call(body, name=name, in_specs=[ANY] * n, out_specs=[ANY] * n, out_shape=ex.out_shapes,
                          scratch_shapes=ex.scratch())(*ex.operands)


def _call_behind(body, ex, *, name, grid, in_specs, out_specs, out_shape, scratch_shapes, operands):
    n_in, n_out, nx = len(in_specs), len(out_specs), len(ex.operands)

    def wrapped(*refs):
        ins, x_in = refs[:n_in], refs[n_in:n_in + nx]
        outs, x_out = refs[n_in + nx:n_in + nx + n_out], refs[n_in + nx + n_out:n_in + 2 * nx + n_out]
        scratch, sems = refs[n_in + 2 * nx + n_out:-2], refs[-2:]
        first, last = None, None
        for d, steps in enumerate(grid):
            at_start, at_end = pl.program_id(d) == 0, pl.program_id(d) == steps - 1
            first = at_start if first is None else first & at_start
            last = at_end if last is None else last & at_end

        @pl.when(first)
        def _():
            ex.start(x_in, x_out, sems)

        body(*ins, *outs, *scratch)

        @pl.when(last)
        def _():
            ex.finish(x_in, x_out, sems)

    res = pl.pallas_call(
        wrapped, name=name, grid=grid, in_specs=list(in_specs) + [ANY] * nx, out_specs=list(out_specs) + [ANY] * nx,
        out_shape=list(out_shape) + list(ex.out_shapes), scratch_shapes=list(scratch_shapes) + ex.scratch(),
        compiler_params=_cparams(("arbitrary",) * len(grid)),
    )(*operands, *ex.operands)
    return res[:n_out], res[n_out:]


def _gather_weights(shards):
    n = len(shards)

    def first_copies(src, out, sems):
        x, y, c = _place()
        me = 2 * x + y
        copies = [_remote(_half(src[t], c), _half(out[t].at[me], c), sems, (t, k), (px, py, c))
                  for t in range(n) for k, (px, py) in enumerate(_other_chips(x, y))]
        return copies + [_remote(src[t], out[t].at[me], sems, (t, 3), (x, y, 1 - c)) for t in range(n)]

    def start(src, out, sems):
        for cp in first_copies(src, out, sems):
            cp.start()

    def finish(src, out, sems):
        x, y, c = _place()
        me = 2 * x + y
        sibling = (x, y, 1 - c)
        chips = _other_chips(x, y)
        passes = []
        for t in range(n):
            for k, (px, py) in enumerate(chips):
                landed = _half(out[t].at[2 * px + py], c)
                _remote(landed, landed, sems, (t, k), (px, py, c)).wait_recv()
                passes.append(_remote(landed, landed, sems, (t, 4 + k), sibling))
                passes[-1].start()
        for t in range(n):
            _remote(src[t], out[t].at[me], sems, (t, 3), sibling).wait_recv()
            for k, (px, py) in enumerate(chips):
                passed = _half(out[t].at[2 * px + py], 1 - c)
                _remote(passed, passed, sems, (t, 4 + k), sibling).wait_recv()
        for cp in first_copies(src, out, sems) + passes:
            cp.wait_send()

    return _Exchange(shards, [jax.ShapeDtypeStruct((N_CHIPS,) + s.shape, s.dtype) for s in shards], (n, 7), start, finish)


def _simple_exchange(operands, out_shapes, copies):
    def start(src, out, sems):
        for cp in copies(src, out, sems):
            cp.start()

    def finish(src, out, sems):
        for cp in copies(src, out, sems):
            cp.wait_recv()
        for cp in copies(src, out, sems):
            cp.wait_send()

    return _Exchange(operands, out_shapes, (len(operands),), start, finish)


def _swap_halves(slots):
    def copies(src, out, sems):
        x, y, c = _place()
        res = []
        for t in range(len(slots)):
            r = src[t].shape[1] // 2
            rows = pl.ds(pl.multiple_of((1 - c) * r, F32_ROWS), r)
            res.append(_remote(src[t].at[:, rows], out[t], sems, t, (x, y, 1 - c)))
        return res

    return _simple_exchange(slots, [jax.ShapeDtypeStruct((N_CHIPS, s.shape[1] // 2, s.shape[2]), s.dtype) for s in slots], copies)


def _exchange_chips(sums):
    n = len(sums)

    def copies(src, out, sems):
        x, y, c = _place()
        return [_remote(src[t].at[2 * px + py], out[t].at[k], sems, (t, k), (px, py, c))
                for t in range(n) for k, (px, py) in enumerate(_other_chips(x, y))]

    def start(src, out, sems):
        for cp in copies(src, out, sems):
            cp.start()

    def finish(src, out, sems):
        for cp in copies(src, out, sems):
            cp.wait_recv()
        for cp in copies(src, out, sems):
            cp.wait_send()

    return _Exchange(sums, [jax.ShapeDtypeStruct((3,) + s.shape[1:], s.dtype) for s in sums], (n, 3), start, finish)


def _share_halves(mine):
    def copies(src, out, sems):
        x, y, c = _place()
        return [_remote(src[t], out[t], sems, t, (x, y, 1 - c)) for t in range(len(mine))]

    return _simple_exchange(mine, [jax.ShapeDtypeStruct(s.shape, s.dtype) for s in mine], copies)


def _half_tile(rows):
    return 256 if rows % 256 == 0 else rows


def _sum_sibling(place, slot, received, name):
    n, rows2, cols = slot.shape
    rows = rows2 // 2
    tile = _half_tile(rows)
    nb = rows // tile

    def body(place_ref, a_ref, b_ref, o_ref):
        o_ref[...] = (a_ref[...] + b_ref[...]).astype(BF16)

    return pl.pallas_call(
        body, name=name, out_shape=jax.ShapeDtypeStruct((n, rows, cols), BF16),
        grid_spec=pltpu.PrefetchScalarGridSpec(
            num_scalar_prefetch=1, grid=(n, nb),
            in_specs=[pl.BlockSpec((None, tile, cols), lambda j, i, pr: (j, pr[1] * nb + i, 0)),
                      pl.BlockSpec((None, tile, cols), lambda j, i, pr: (j, i, 0))],
            out_specs=pl.BlockSpec((None, tile, cols), lambda j, i, pr: (j, i, 0))),
        compiler_params=_cparams(("parallel", "parallel")),
    )(place, slot, received)


def _sum_chips(place, slot, received, others, name):
    _, rows2, cols = slot.shape
    rows = rows2 // 2
    tile = _half_tile(rows)
    nb = rows // tile

    def body(place_ref, a_ref, b_ref, p_ref, o_ref):
        own = a_ref[...] + b_ref[...]
        o_ref[...] = ((own + p_ref[0].astype(F32)) + p_ref[1].astype(F32)) + p_ref[2].astype(F32)

    return pl.pallas_call(
        body, name=name, out_shape=jax.ShapeDtypeStruct((rows, cols), F32),
        grid_spec=pltpu.PrefetchScalarGridSpec(
            num_scalar_prefetch=1, grid=(nb,),
            in_specs=[pl.BlockSpec((None, tile, cols), lambda i, pr: (pr[0], pr[1] * nb + i, 0)),
                      pl.BlockSpec((None, tile, cols), lambda i, pr: (pr[0], i, 0)),
                      pl.BlockSpec((3, tile, cols), lambda i, pr: (0, i, 0))],
            out_specs=pl.BlockSpec((tile, cols), lambda i, pr: (i, 0))),
        compiler_params=_cparams(("parallel",)),
    )(place, slot, received, others)


N_DEVICES = 8


def _sum_devices(block, name):
    def body(v_ref, o_ref, land_ref, send_sems, recv_sems):
        x, y, c = _place()
        me = 4 * x + 2 * y + c
        copies = []
        for mask in range(1, N_DEVICES):
            peer = (x ^ (mask >> 2), y ^ ((mask >> 1) & 1), c ^ (mask & 1))
            copies.append(pltpu.make_async_remote_copy(src_ref=v_ref, dst_ref=land_ref.at[me], send_sem=send_sems.at[mask - 1],
                                                       recv_sem=recv_sems.at[mask - 1], device_id=peer, device_id_type=MESH))
        for cp in copies:
            cp.start()
        land_ref[me] = v_ref[...]
        for cp in copies:
            cp.wait_recv()
        total = land_ref[0]
        for d in range(1, N_DEVICES):
            total = total + land_ref[d]
        o_ref[...] = total
        for cp in copies:
            cp.wait_send()

    vmem = pl.BlockSpec(memory_space=pltpu.VMEM)
    return pl.pallas_call(
        body, name=name, in_specs=[vmem], out_specs=vmem, out_shape=jax.ShapeDtypeStruct(block.shape, F32),
        scratch_shapes=[pltpu.VMEM((N_DEVICES,) + block.shape, F32), pltpu.SemaphoreType.DMA((N_DEVICES - 1,)),
                        pltpu.SemaphoreType.DMA((N_DEVICES - 1,))],
    )(block)


def _vec_block(g_mix, g_mlp, g_ple, g_final, b_forget, b_gate_rows, last=None):
    pad = lambda a: jnp.concatenate([a, jnp.zeros((a.shape[0], D_MODEL - a.shape[1]), F32)], axis=1)
    last = jnp.zeros((1, 0), F32) if last is None else last
    return jnp.concatenate([g_mix, g_mlp, g_ple, g_final.reshape(1, D_MODEL), pad(b_forget), pad(b_gate_rows), pad(last)],
                           axis=0)


def _adam_math(w, g, m, v):
    m_new = ADAM_B1 * m + (1.0 - ADAM_B1) * g
    v_new = ADAM_B2 * v + (1.0 - ADAM_B2) * (g * g)
    m_hat = m_new / (1.0 - ADAM_B1 ** ADAM_STEP)
    v_hat = v_new / (1.0 - ADAM_B2 ** ADAM_STEP)
    return -ADAM_LR * (m_hat / (jnp.sqrt(v_hat) + ADAM_EPS) + ADAM_WD * w), m_new, v_new


def _adamw_halves(place, w, m, v, g_mine, g_theirs, name):
    rows2, cols = w.shape
    rows = rows2 // 2
    tile = _half_tile(rows)
    nb = rows // tile

    def body(place_ref, w_ref, m_ref, v_ref, gm_ref, gt_ref, g_ref, d_ref, nm_ref, nv_ref):
        g = jnp.where(pl.program_id(0) == 0, gm_ref[...], gt_ref[...])
        g_ref[...] = g
        d_ref[...], nm_ref[...], nv_ref[...] = _adam_math(w_ref[...], g, m_ref[...], v_ref[...])

    whole = pl.BlockSpec((tile, cols), lambda s, i, pr: ((pr[1] + s - 2 * pr[1] * s) * nb + i, 0))
    half = pl.BlockSpec((tile, cols), lambda s, i, pr: (i, 0))
    return pl.pallas_call(
        body, name=name, out_shape=[jax.ShapeDtypeStruct((rows2, cols), F32)] * 4,
        grid_spec=pltpu.PrefetchScalarGridSpec(num_scalar_prefetch=1, grid=(2, nb), in_specs=[whole] * 3 + [half] * 2,
                                               out_specs=[whole] * 4),
        compiler_params=_cparams(("parallel", "parallel")),
    )(place, w, m, v, g_mine, g_theirs)


def _adamw_vec(w, g, m, v):
    def body(w_ref, g_ref, m_ref, v_ref, d_ref, nm_ref, nv_ref):
        d_ref[...], nm_ref[...], nv_ref[...] = _adam_math(w_ref[...], g_ref[...], m_ref[...], v_ref[...])

    return pl.pallas_call(body, name="adamw_vectors", out_shape=[jax.ShapeDtypeStruct(w.shape, F32)] * 3)(w, g, m, v)


WEIGHT_NAMES = ("g_mix", "w_in", "b_forget", "b_gate", "w_branch_fox", "w_branch_sb", "w_out", "g_mlp", "w_up", "w_down",
                "g_ple", "w_ple_gate", "w_ple", "g_final")
W_IN_SHARD = D_IN // N_CHIPS
Q_END, F_END, B_END = 3 * D_ATT, 3 * D_ATT + N_HEADS, 6 * D_ATT + N_HEADS
GATE_SHARD = D_MODEL // N_CHIPS


def _join_cols(slots):
    return jnp.transpose(slots, (1, 0, 2)).reshape(slots.shape[1], N_CHIPS * slots.shape[2])


LATE = SHARDED[:1]
EARLY = SHARDED[1:]


def _first_weights(w_in_slots):
    def cols(*ranges):
        parts = []
        for lo, hi in ranges:
            for j in range(N_CHIPS):
                a, b = max(lo, j * W_IN_SHARD), min(hi, (j + 1) * W_IN_SHARD)
                if a < b:
                    parts.append(w_in_slots[j, :, a - j * W_IN_SHARD:b - j * W_IN_SHARD])
        return parts

    forget = jnp.concatenate(cols((Q_END, F_END)) + [jnp.zeros((D_MODEL, F_PAD - N_HEADS), BF16)], axis=1)
    return {"qkv": jnp.concatenate(cols((0, Q_END), (F_END, B_END)), axis=1), "gate": jnp.concatenate(cols((B_END, D_IN)), axis=1),
            "forget": forget}


GATE_ROWS = 2 * ROW_ALIGN


def _gate_bits(b_gate):
    bits = lax.bitcast_convert_type(b_gate, BF16).reshape(2, 2 * GATE_SHARD)
    return jnp.concatenate([bits, jnp.zeros((GATE_ROWS - 2, 2 * GATE_SHARD), BF16)], axis=0)


def _rest_weights(gathered):
    rows = lambda a: a.reshape(N_CHIPS * a.shape[1], a.shape[2])
    bits = gathered["b_gate"][:, :2].reshape(N_CHIPS, 2, GATE_SHARD, 2)
    b_gate = jnp.transpose(lax.bitcast_convert_type(bits, F32), (1, 0, 2)).reshape(2, D_MODEL)
    return {"branch_fox": gathered["w_branch_fox"], "branch_sb": gathered["w_branch_sb"], "out": rows(gathered["w_out"]),
            "up": gathered["w_up"], "down": rows(gathered["w_down"]), "ple_gate": rows(gathered["w_ple_gate"]),
            "ple": gathered["w_ple"], "b_gate": b_gate}


def _early_slots(gw):
    rows = lambda a: a.reshape(N_CHIPS, a.shape[0] // N_CHIPS, a.shape[1])
    return {"w_branch_fox": gw["branch_fox"], "w_branch_sb": gw["branch_sb"], "w_out": rows(gw["out"]), "w_up": gw["up"],
            "w_down": rows(gw["down"]), "w_ple_gate": rows(gw["ple_gate"]), "w_ple": gw["ple"]}


W_IN_FLAT = (W_IN_SHARD * D_MODEL // LANES, LANES)


def _w_in_slots(gw):
    g_t = jnp.concatenate([gw["qkv"][:Q_END], gw["forget"][:N_HEADS], gw["qkv"][Q_END:], gw["gate"]], axis=0)
    return g_t.reshape((N_CHIPS,) + W_IN_FLAT)


def _flat(a):
    return jnp.transpose(a, (2, 0, 1)).reshape(W_IN_FLAT)


def _unflat(a):
    return jnp.transpose(a.reshape(W_IN_SHARD, D_MODEL // LANES, LANES), (1, 2, 0)).reshape(1, D_MODEL, W_IN_SHARD)


def kernel(x, p, g_mix, w_in, b_forget, b_gate, w_branch_fox, w_branch_sb, w_out, g_mlp, w_up, w_down, g_ple, w_ple_gate, w_ple, g_final, loss_target, m_g_mix, m_w_in, m_b_forget, m_b_gate, m_w_branch_fox, m_w_branch_sb, m_w_out, m_g_mlp, m_w_up, m_w_down, m_g_ple, m_w_ple_gate, m_w_ple, m_g_final, v_g_mix, v_w_in, v_b_forget, v_b_gate, v_w_branch_fox, v_w_branch_sb, v_w_out, v_g_mlp, v_w_up, v_w_down, v_g_ple, v_w_ple_gate, v_w_ple, v_g_final):
    weights = dict(g_mix=g_mix, w_in=w_in, b_forget=b_forget, b_gate=b_gate, w_branch_fox=w_branch_fox,
                   w_branch_sb=w_branch_sb, w_out=w_out, g_mlp=g_mlp, w_up=w_up, w_down=w_down, g_ple=g_ple,
                   w_ple_gate=w_ple_gate, w_ple=w_ple, g_final=g_final)
    first = dict(g_mix=m_g_mix, w_in=m_w_in, b_forget=m_b_forget, b_gate=m_b_gate, w_branch_fox=m_w_branch_fox,
                 w_branch_sb=m_w_branch_sb, w_out=m_w_out, g_mlp=m_g_mlp, w_up=m_w_up, w_down=m_w_down, g_ple=m_g_ple,
                 w_ple_gate=m_w_ple_gate, w_ple=m_w_ple, g_final=m_g_final)
    second = dict(g_mix=v_g_mix, w_in=v_w_in, b_forget=v_b_forget, b_gate=v_b_gate, w_branch_fox=v_w_branch_fox,
                  w_branch_sb=v_w_branch_sb, w_out=v_w_out, g_mlp=v_g_mlp, w_up=v_w_up, w_down=v_w_down, g_ple=v_g_ple,
                  w_ple_gate=v_w_ple_gate, w_ple=v_w_ple, g_final=v_g_final)
    cx, cy, cc = _place()
    chip = 2 * cx + cy
    place = jnp.stack([chip, cc]).astype(jnp.int32)
    col0 = chip * GATE_SHARD

    (w_in_slots,) = _run_exchange(_gather_weights([weights[n][0].astype(BF16) for n in LATE]), "gather_w_in")
    rest = _gather_weights([weights[n][0].astype(BF16) for n in EARLY] + [_gate_bits(b_gate[0])])
    vec = {"g_mix": g_mix, "b_forget": jnp.concatenate([b_forget, jnp.zeros((1, F_PAD - N_HEADS), F32)], axis=1),
           "g_mlp": g_mlp, "g_ple": g_ple, "g_final": g_final.reshape(1, D_MODEL)}

    loss, grad_x, reduced, gvec = _local_step(x, p[0], loss_target, _first_weights(w_in_slots), rest, vec, place)

    out = {}
    for n in EARLY:
        g_mine, g_theirs = reduced[n]
        res = _adamw_halves(place, weights[n][0], first[n][0], second[n][0], g_mine, g_theirs, "adamw_" + n)
        out[n] = [r[None] for r in res]
    g_mine, g_theirs = reduced["w_in"]
    out["w_in"] = [_unflat(r) for r in _adamw_halves(place, _flat(w_in), _flat(m_w_in), _flat(v_w_in), g_mine, g_theirs,
                                                      "adamw_w_in")]

    g_block = _sum_devices(_vec_block(gvec["g_mix"], gvec["g_mlp"], gvec["g_ple"], gvec["g_final"][0], gvec["b_forget"],
                                      gvec["b_gate"], loss), "reduce_vectors")
    loss = g_block[7, 0]
    g_gate = lax.dynamic_slice(g_block[5:7], (0, col0), (2, GATE_SHARD))
    blocks = [_vec_block(d["g_mix"], d["g_mlp"], d["g_ple"], d["g_final"], d["b_forget"], d["b_gate"][0])
              for d in (weights, first, second)]
    g_rows = jnp.concatenate([g_block[0:5], jnp.concatenate([g_gate, jnp.zeros((2, D_MODEL - GATE_SHARD), F32)], axis=1),
                              jnp.zeros((1, D_MODEL), F32)], axis=0)
    res = (g_rows,) + tuple(_adamw_vec(blocks[0], g_rows, blocks[1], blocks[2]))
    out["g_mix"] = [r[0:1] for r in res]
    out["g_mlp"] = [r[1:2] for r in res]
    out["g_ple"] = [r[2:3] for r in res]
    out["g_final"] = [r[3] for r in res]
    out["b_forget"] = [r[4:5, :N_HEADS] for r in res]
    out["b_gate"] = [r[5:7, :GATE_SHARD][None] for r in res]
    return (loss, grad_x, *[out[n][0] for n in WEIGHT_NAMES], *[out[n][1] for n in WEIGHT_NAMES],
            *[out[n][2] for n in WEIGHT_NAMES], *[out[n][3] for n in WEIGHT_NAMES])
```

```python
import jax
import jax.numpy as jnp
from jax import lax
from jax.experimental import pallas as pl
from jax.experimental.pallas import tpu as pltpu

F32 = jnp.float32
BF16 = jnp.bfloat16

D_MODEL = 1024
HEAD_DIM = 64
N_HEADS = 8
D_ATT = N_HEADS * HEAD_DIM
D_PLE = 256
D_IN = 6 * D_ATT + N_HEADS + 2 * D_MODEL
F_PAD = 128
EPS = 1e-6
SCALE = HEAD_DIM ** -0.5
N_CHIPS = 4
LANES = 128
ATT_BLOCK = 256
FOX_TILES = (512, 512)
SB_TILES = (512, 256)
NEG = -1e30

ADAM_LR = 0.001
ADAM_B1 = 0.9
ADAM_B2 = 0.999
ADAM_EPS = 1e-08
ADAM_WD = 0.01
ADAM_STEP = 10

VMEM_LIMIT = 56 * 1024 * 1024

MESH = pl.DeviceIdType.MESH


def _cparams(sem=None):
    return pltpu.CompilerParams(dimension_semantics=sem, vmem_limit_bytes=VMEM_LIMIT)


def _relu2(t):
    t = t.astype(F32)
    return t * t


_DIMS = {"nn": (((1,), (0,)), ((), ())), "nt": (((1,), (1,)), ((), ())), "tn": (((0,), (0,)), ((), ()))}
NT_DIMS = _DIMS["nt"]
TN_DIMS = _DIMS["tn"]


def _mm(a, b, *, mode, name, out_dtype=F32, tm=512, tn=512, tk=512, add=None, a_fn=None, epi=None, extra=None,
        col_shards=False, behind=None, flat_out=False):
    if mode == "nn":
        (m, k), n = a.shape, b.shape[-1]
    elif mode == "nt":
        (m, k), n = a.shape, b.shape[-2]
    else:
        (k, m), n = a.shape, b.shape[1]
    shard = None
    if col_shards:
        if mode == "nn":
            shard, n = n, N_CHIPS * n
            tn = min(tn, shard)
        elif mode == "nt":
            shard = b.shape[-1]
            tk = min(tk, shard)
        else:
            shard = n // N_CHIPS
            tn = min(tn, shard)
    tm, tn, tk = min(tm, m), min(tn, n), min(tk, k)
    assert m % tm == 0 and n % tn == 0 and k % tk == 0, (name, m, n, k)
    nk = k // tk
    a_spec = {"nn": pl.BlockSpec((tm, tk), lambda i, j, kk: (i, kk)),
              "nt": pl.BlockSpec((tm, tk), lambda i, j, kk: (i, kk)),
              "tn": pl.BlockSpec((tk, tm), lambda i, j, kk: (kk, i))}[mode]
    b_spec = {"nn": pl.BlockSpec((tk, tn), lambda i, j, kk: (kk, j)),
              "nt": pl.BlockSpec((tn, tk), lambda i, j, kk: (j, kk)),
              "tn": pl.BlockSpec((tk, tn), lambda i, j, kk: (kk, j))}[mode]
    o_spec = pl.BlockSpec((tm, tn), lambda i, j, kk: (i, j))
    out_shape = (m, n)
    if col_shards and mode == "nn":
        per = shard // tn
        b_spec = pl.BlockSpec((None, tk, tn), lambda i, j, kk: (j // per, kk, j % per))
    elif col_shards and mode == "nt":
        per = shard // tk
        b_spec = pl.BlockSpec((None, tn, tk), lambda i, j, kk: (kk // per, j, kk % per))
    elif col_shards:
        assert add is None and extra is None
        per = shard // tn
        o_spec = pl.BlockSpec((None, tm, tn), lambda i, j, kk: (j // per, i, j % per))
        out_shape = (N_CHIPS, m, shard)
    if flat_out:
        assert mode == "tn" and tn == n == D_MODEL and not col_shards and add is None and extra is None
        chunks = D_MODEL // LANES
        o_spec = pl.BlockSpec((tm * chunks, LANES), lambda i, j, kk: (i, 0))
        out_shape = (m * chunks, LANES)
    operands, in_specs = [a, b], [a_spec, b_spec]
    third = add if add is not None else extra
    if third is not None:
        operands.append(third)
        in_specs.append(o_spec)

    def body(*refs):
        a_ref, b_ref = refs[0], refs[1]
        t_ref = refs[2] if third is not None else None
        o_ref = refs[3] if third is not None else refs[2]
        acc_ref = refs[-1] if nk > 1 else None
        at = a_ref[...]
        if a_fn is not None:
            at = a_fn(at)
        part = lax.dot_general(at.astype(BF16), b_ref[...].astype(BF16), _DIMS[mode], preferred_element_type=F32)

        def finish(acc):
            if epi is not None:
                acc = epi(acc, None if t_ref is None else t_ref[...])
            elif add is not None:
                acc = acc + t_ref[...].astype(F32)
            if flat_out:
                for q in range(D_MODEL // LANES):
                    o_ref[pl.ds(q, tm, stride=D_MODEL // LANES), :] = acc[:, q * LANES:(q + 1) * LANES].astype(o_ref.dtype)
                return
            o_ref[...] = acc.astype(o_ref.dtype)

        if nk == 1:
            finish(part)
        else:
            kk = pl.program_id(2)

            @pl.when(kk == 0)
            def _():
                acc_ref[...] = part

            @pl.when(kk > 0)
            def _():
                acc_ref[...] += part

            @pl.when(kk == nk - 1)
            def _():
                finish(acc_ref[...])

    call = dict(name=name, grid=(m // tm, n // tn, nk), in_specs=in_specs,
                scratch_shapes=[pltpu.VMEM((tm, tn), F32)] if nk > 1 else [])
    if behind is not None:
        (res,), exchanged = _call_behind(body, behind, out_specs=[o_spec], out_shape=[jax.ShapeDtypeStruct(out_shape, out_dtype)],
                                         operands=operands, **call)
        return res, exchanged
    return pl.pallas_call(body, out_specs=o_spec, out_shape=jax.ShapeDtypeStruct(out_shape, out_dtype),
                          compiler_params=_cparams(("parallel", "parallel", "arbitrary")), **call)(*operands)


ROW_TILE = 512


def _row_spec(width=D_MODEL, rows=ROW_TILE):
    return pl.BlockSpec((rows, width), lambda i: (i, 0))


def _vec_spec(rows=1, width=D_MODEL):
    return pl.BlockSpec((rows, width), lambda i: (0, 0))


def _xhat(x):
    r = lax.rsqrt(jnp.mean(x * x, axis=-1, keepdims=True) + EPS)
    return x * r, r


def _rms_bwd_rows(dh, x, g):
    xh, r = _xhat(x)
    dxh = dh * g
    dx = r * (dxh - xh * jnp.mean(dxh * xh, axis=-1, keepdims=True))
    return dx, jnp.sum(dh * xh, axis=0, keepdims=True)


def _norm_fwd(x, g, name):
    t = x.shape[0]

    def body(x_ref, g_ref, h_ref):
        xh, _ = _xhat(x_ref[...])
        h_ref[...] = (xh * g_ref[...]).astype(BF16)

    return pl.pallas_call(
        body, name=name, grid=(t // ROW_TILE,), in_specs=[_row_spec(), _vec_spec()], out_specs=_row_spec(),
        out_shape=jax.ShapeDtypeStruct((t, D_MODEL), BF16), compiler_params=_cparams(("parallel",)),
    )(x, g)


def _mm_res_norm(a, b, res, g, name, a_fn=None):
    t, k = a.shape

    def body(a_ref, b_ref, res_ref, g_ref, x_ref, h_ref):
        at = a_ref[...] if a_fn is None else a_fn(a_ref[...])
        x_new = res_ref[...] + _dot(at.astype(BF16), b_ref[...])
        x_ref[...] = x_new
        h_ref[...] = (_xhat(x_new)[0] * g_ref[...]).astype(BF16)

    return pl.pallas_call(
        body, name=name, grid=(t // ROW_TILE,),
        in_specs=[pl.BlockSpec((ROW_TILE, k), lambda i: (i, 0)), pl.BlockSpec(b.shape, lambda i: (0, 0)), _row_spec(), _vec_spec()],
        out_specs=[_row_spec(), _row_spec()],
        out_shape=[jax.ShapeDtypeStruct((t, D_MODEL), F32), jax.ShapeDtypeStruct((t, D_MODEL), BF16)],
        compiler_params=_cparams(("parallel",)),
    )(a, b, res, g)


def _mm_norm_bwd(pairs, dh_first, x, g, dres, name, behind=None):
    t = x.shape[0]
    operands, in_specs = [], []
    for a, b in pairs:
        if b.ndim == 3:
            for j in range(b.shape[0]):
                operands += [a, b]
                in_specs += [pl.BlockSpec((ROW_TILE, b.shape[2]), lambda i, j=j: (i, j)),
                             pl.BlockSpec((None, D_MODEL, b.shape[2]), lambda i, j=j: (j, 0, 0))]
        else:
            operands += [a, b]
            in_specs += [pl.BlockSpec((ROW_TILE, a.shape[1]), lambda i: (i, 0)), pl.BlockSpec(b.shape, lambda i: (0, 0))]
    n_mm = len(operands)
    operands += [x, g, dres] + ([] if dh_first is None else [dh_first])
    in_specs += [_row_spec(), _vec_spec(), _row_spec()] + ([] if dh_first is None else [_row_spec()])

    def body(*refs):
        x_ref, g_ref, dres_ref = refs[n_mm:n_mm + 3]
        dx_ref, dxb_ref, dg_ref = refs[-3:]
        dh = 0.0 if dh_first is None else refs[n_mm + 3][...]
        for k in range(0, n_mm, 2):
            dh = dh + lax.dot_general(refs[k][...].astype(BF16), refs[k + 1][...].astype(BF16), NT_DIMS,
                                      preferred_element_type=F32)
        dx, dg = _rms_bwd_rows(dh, x_ref[...], g_ref[...])
        dx = dx + dres_ref[...]
        dx_ref[...] = dx
        dxb_ref[...] = dx.astype(BF16)

        @pl.when(pl.program_id(0) == 0)
        def _():
            dg_ref[...] = jnp.zeros_like(dg_ref)

        dg_ref[...] += dg

    call = dict(name=name, grid=(t // ROW_TILE,), in_specs=in_specs, out_specs=[_row_spec(), _row_spec(), _vec_spec()],
                out_shape=[jax.ShapeDtypeStruct((t, D_MODEL), F32), jax.ShapeDtypeStruct((t, D_MODEL), BF16),
                           jax.ShapeDtypeStruct((1, D_MODEL), F32)])
    if behind is not None:
        return _call_behind(body, behind, scratch_shapes=[], operands=operands, **call)
    return pl.pallas_call(body, compiler_params=_cparams(("arbitrary",)), **call)(*operands)


def _shards_spec(w):
    return pl.BlockSpec(w.shape, lambda i: (0, 0, 0))


def _gate_fwd(gl, b_gate, o_fox, o_sb, w_fox, w_sb):
    t = o_fox.shape[0]

    def body(gla_ref, glb_ref, b_ref, ofox_ref, osb_ref, wf_ref, ws_ref, m_ref, of_ref, os_ref):
        of = jnp.concatenate([_dot(ofox_ref[...], wf_ref[j]) for j in range(N_CHIPS)], axis=1)
        os_ = jnp.concatenate([_dot(osb_ref[...], ws_ref[j]) for j in range(N_CHIPS)], axis=1)
        ga = jax.nn.sigmoid(gla_ref[...] + b_ref[0:1, :])
        gb = jax.nn.sigmoid(glb_ref[...] + b_ref[1:2, :])
        of_ref[...] = of
        os_ref[...] = os_
        m_ref[...] = (ga * of + gb * os_).astype(BF16)

    return pl.pallas_call(
        body, name="gate_fwd", grid=(t // ROW_TILE,),
        in_specs=[pl.BlockSpec((ROW_TILE, D_MODEL), lambda i: (i, 0)), pl.BlockSpec((ROW_TILE, D_MODEL), lambda i: (i, 1)),
                  _vec_spec(2), _row_spec(D_ATT), _row_spec(D_ATT), _shards_spec(w_fox), _shards_spec(w_sb)],
        out_specs=[_row_spec(), _row_spec(), _row_spec()],
        out_shape=[jax.ShapeDtypeStruct((t, D_MODEL), BF16)] + [jax.ShapeDtypeStruct((t, D_MODEL), F32)] * 2,
        compiler_params=_cparams(("parallel",)),
    )(gl, gl, b_gate, o_fox, o_sb, w_fox, w_sb)


def _gate_bwd(gl, b_gate, of, os_, dx, w_out, w_fox, w_sb):
    t = of.shape[0]
    shard = D_MODEL // N_CHIPS

    def back(d, w_ref):
        return sum(_dot(d[:, j * shard:(j + 1) * shard], w_ref[j], NT_DIMS) for j in range(N_CHIPS)).astype(BF16)

    def body(gla_ref, glb_ref, b_ref, of_ref, os_ref, dx_ref, w_ref, wf_ref, ws_ref,
             dof_ref, dos_ref, dgl_ref, db_ref, dofox_ref, dosb_ref):
        dm = _dot(dx_ref[...], w_ref[...], NT_DIMS)
        ga = jax.nn.sigmoid(gla_ref[...] + b_ref[0:1, :])
        gb = jax.nn.sigmoid(glb_ref[...] + b_ref[1:2, :])
        dof = (dm * ga).astype(BF16)
        dos = (dm * gb).astype(BF16)
        dof_ref[...] = dof
        dos_ref[...] = dos
        dofox_ref[...] = back(dof, wf_ref)
        dosb_ref[...] = back(dos, ws_ref)
        dgla = dm * of_ref[...] * ga * (1.0 - ga)
        dglb = dm * os_ref[...] * gb * (1.0 - gb)
        dgl_ref[:, 0:D_MODEL] = dgla.astype(BF16)
        dgl_ref[:, D_MODEL:2 * D_MODEL] = dglb.astype(BF16)

        @pl.when(pl.program_id(0) == 0)
        def _():
            db_ref[...] = jnp.zeros_like(db_ref)

        db_ref[0:1, :] += jnp.sum(dgla, axis=0, keepdims=True)
        db_ref[1:2, :] += jnp.sum(dglb, axis=0, keepdims=True)

    outs = pl.pallas_call(
        body, name="gate_bwd", grid=(t // ROW_TILE,),
        in_specs=[pl.BlockSpec((ROW_TILE, D_MODEL), lambda i: (i, 0)), pl.BlockSpec((ROW_TILE, D_MODEL), lambda i: (i, 1)),
                  _vec_spec(2), _row_spec(), _row_spec(), _row_spec(), pl.BlockSpec(w_out.shape, lambda i: (0, 0)),
                  _shards_spec(w_fox), _shards_spec(w_sb)],
        out_specs=[_row_spec(), _row_spec(), _row_spec(2 * D_MODEL), _vec_spec(2), _row_spec(D_ATT), _row_spec(D_ATT)],
        out_shape=[jax.ShapeDtypeStruct((t, D_MODEL), BF16)] * 2 + [jax.ShapeDtypeStruct((t, 2 * D_MODEL), BF16),
                                                                      jax.ShapeDtypeStruct((2, D_MODEL), F32)]
        + [jax.ShapeDtypeStruct((t, D_ATT), BF16)] * 2,
        compiler_params=_cparams(("arbitrary",)),
    )(gl, gl, b_gate, of, os_, dx, w_out, w_fox, w_sb)
    return outs


def _head_and_loss(x2, h3, p, w_gate, w_ple, g_final, target):
    t = x2.shape[0]

    def body(x2_ref, h3_ref, p_ref, wg_ref, wp_ref, g_ref, tgt_ref, dx3_ref, dpre_ref, dpe_ref, dg_ref, loss_ref):
        gp = jax.nn.sigmoid(_dot(h3_ref[...], wg_ref[...]))
        p_t = p_ref[...].astype(BF16)
        pe_t = jnp.concatenate([_dot(p_t, wp_ref[j]) for j in range(N_CHIPS)], axis=1)
        x3 = x2_ref[...] + gp * pe_t
        g = g_ref[...]
        xh, _ = _xhat(x3)
        err = xh * g - tgt_ref[...]
        dy = err * (1.0 / D_MODEL)
        dx3, dg = _rms_bwd_rows(dy, x3, g)
        dx3_ref[...] = dx3
        dpre_ref[...] = (dx3 * pe_t * gp * (1.0 - gp)).astype(BF16)
        dpe_ref[...] = (dx3 * gp).astype(BF16)

        @pl.when(pl.program_id(0) == 0)
        def _():
            dg_ref[...] = jnp.zeros_like(dg_ref)
            loss_ref[...] = jnp.zeros_like(loss_ref)

        dg_ref[...] += dg
        loss_ref[...] += 0.5 * jnp.sum(jnp.mean(err * err, axis=-1, keepdims=True), axis=0, keepdims=True)

    return pl.pallas_call(
        body, name="head_and_loss", grid=(t // ROW_TILE,),
        in_specs=[_row_spec(), _row_spec(), _row_spec(D_PLE), pl.BlockSpec(w_gate.shape, lambda i: (0, 0)),
                  pl.BlockSpec(w_ple.shape, lambda i: (0, 0, 0)), _vec_spec(), _row_spec()],
        out_specs=[_row_spec(), _row_spec(), _row_spec(), _vec_spec(), _vec_spec(1, LANES)],
        out_shape=[jax.ShapeDtypeStruct((t, D_MODEL), F32), jax.ShapeDtypeStruct((t, D_MODEL), BF16),
                   jax.ShapeDtypeStruct((t, D_MODEL), BF16), jax.ShapeDtypeStruct((1, D_MODEL), F32),
                   jax.ShapeDtypeStruct((1, LANES), F32)],
        compiler_params=_cparams(("arbitrary",)),
    )(x2, h3, p, w_gate, w_ple, g_final, target)


def _split3(v):
    hi = v.astype(BF16)
    r1 = v - hi.astype(F32)
    mid = r1.astype(BF16)
    lo = (r1 - mid.astype(F32)).astype(BF16)
    return hi, mid, lo


def _split2(v):
    hi = v.astype(BF16)
    return jnp.concatenate([hi, (v - hi.astype(F32)).astype(BF16)], axis=1)


def _dot(a, b, dims=_DIMS["nn"]):
    return lax.dot_general(a, b, dims, preferred_element_type=F32)


def _tri(n, rel):
    row = lax.broadcasted_iota(jnp.int32, (n, n), 0)
    col = lax.broadcasted_iota(jnp.int32, (n, n), 1)
    return rel(row, col).astype(BF16)


def _tri2(n, rel):
    t = _tri(n, rel)
    return jnp.concatenate([t, t], axis=0)


def _log_sigmoid(v):
    return -(jnp.maximum(-v, 0.0) + jnp.log(1.0 + jnp.exp(-jnp.abs(v))))


def _fox_prep(fl, b_forget, batch, seq):
    nb = seq // ATT_BLOCK

    def body(fl_ref, b_ref, cw_ref, cr_ref):
        col = lax.broadcasted_iota(jnp.int32, (ATT_BLOCK, F_PAD), 1)
        lower = _tri(ATT_BLOCK, lambda r, c: c <= r)
        upper = _tri(ATT_BLOCK, lambda r, c: r <= c)
        expand = (lax.broadcasted_iota(jnp.int32, (F_PAD, D_ATT), 1) // HEAD_DIM
                  == lax.broadcasted_iota(jnp.int32, (F_PAD, D_ATT), 0)).astype(BF16)
        carry_w = jnp.zeros((1, D_ATT), F32)
        carry_r = jnp.zeros((F_PAD, 1), F32)
        for i in range(nb):
            blk = slice(i * ATT_BLOCK, (i + 1) * ATT_BLOCK)
            logf = jnp.where(col < N_HEADS, _log_sigmoid(fl_ref[blk, :] + b_ref[...]), 0.0)
            cw = jnp.zeros((ATT_BLOCK, D_ATT), F32) + carry_w
            cr = jnp.zeros((F_PAD, ATT_BLOCK), F32) + carry_r
            for part in _split3(logf):
                cw += _dot(lower, _dot(part, expand).astype(BF16))
                cr += _dot(part, upper, TN_DIMS)
            cw_ref[blk, :] = cw
            cr_ref[:, blk] = cr[0:N_HEADS, :]
            carry_w = cw[ATT_BLOCK - 1:ATT_BLOCK, :]
            carry_r = cr[:, ATT_BLOCK - 1:ATT_BLOCK]

    return pl.pallas_call(
        body, name="fox_prep", grid=(batch,),
        in_specs=[pl.BlockSpec((seq, F_PAD), lambda b: (b, 0)), pl.BlockSpec((1, F_PAD), lambda b: (0, 0))],
        out_specs=[pl.BlockSpec((seq, D_ATT), lambda b: (b, 0)), pl.BlockSpec((N_HEADS, seq), lambda b: (b, 0))],
        out_shape=[jax.ShapeDtypeStruct((batch * seq, D_ATT), F32), jax.ShapeDtypeStruct((batch * N_HEADS, seq), F32)],
        compiler_params=_cparams(("parallel",)),
    )(fl, b_forget)


def _fox_post(dcs_wide, drs_wide, fl, b_forget, batch, seq):
    nb = seq // ATT_BLOCK

    def body(dcs_ref, drs_ref, fl_ref, b_ref, dfl_ref, db_ref):
        pick = (lax.broadcasted_iota(jnp.int32, (D_ATT, F_PAD), 0)
                == lax.broadcasted_iota(jnp.int32, (D_ATT, F_PAD), 1) * HEAD_DIM).astype(BF16)
        upper = _tri(ATT_BLOCK, lambda r, c: r <= c)
        col = lax.broadcasted_iota(jnp.int32, (ATT_BLOCK, F_PAD), 1)

        @pl.when(pl.program_id(0) == 0)
        def _():
            db_ref[...] = jnp.zeros_like(db_ref)

        carry = jnp.zeros((1, F_PAD), F32)
        for i in reversed(range(nb)):
            blk = slice(i * ATT_BLOCK, (i + 1) * ATT_BLOCK)
            narrow = jnp.zeros((ATT_BLOCK, F_PAD), F32)
            for part in _split3(drs_ref[blk, :] - dcs_ref[blk, :]):
                narrow += _dot(part, pick)
            after = jnp.zeros((ATT_BLOCK, F_PAD), F32) + carry
            for part in _split3(narrow):
                after += _dot(upper, part)
            carry = after[0:1, :]
            pre = fl_ref[blk, :] + b_ref[...]
            dfl = jnp.where(col < N_HEADS, after * jax.nn.sigmoid(-pre), 0.0)
            dfl_ref[blk, :] = dfl.astype(BF16)
            db_ref[...] += jnp.sum(dfl, axis=0, keepdims=True)

    return pl.pallas_call(
        body, name="fox_post", grid=(batch,),
        in_specs=[pl.BlockSpec((seq, D_ATT), lambda b: (b, 0)), pl.BlockSpec((seq, D_ATT), lambda b: (b, 0)),
                  pl.BlockSpec((seq, F_PAD), lambda b: (b, 0)), pl.BlockSpec((1, F_PAD), lambda b: (0, 0))],
        out_specs=[pl.BlockSpec((seq, F_PAD), lambda b: (b, 0)), pl.BlockSpec((1, F_PAD), lambda b: (0, 0))],
        out_shape=[jax.ShapeDtypeStruct((batch * seq, F_PAD), BF16), jax.ShapeDtypeStruct((1, F_PAD), F32)],
        compiler_params=_cparams(("arbitrary",)),
    )(dcs_wide, drs_wide, fl, b_forget)


N_PAIRS = N_HEADS // 2


def _att_specs(seq, col0, tq):
    nq = seq // tq
    q = pl.BlockSpec((tq, LANES), lambda b, hp, qi: (b * nq + qi, col0 + hp))
    k = pl.BlockSpec((seq, LANES), lambda b, hp, qi: (b, col0 + N_PAIRS + hp))
    v = pl.BlockSpec((seq, LANES), lambda b, hp, qi: (b, col0 + 2 * N_PAIRS + hp))
    return q, k, v


def _qblock_spec(seq, tq):
    nq = seq // tq
    return pl.BlockSpec((tq, LANES), lambda b, hp, qi: (b * nq + qi, hp))


def _kv_out_spec(seq):
    return pl.BlockSpec((seq, LANES), lambda b, hp, qi: (b, hp))


def _head_masks():
    lane = lax.broadcasted_iota(jnp.int32, (1, LANES), 1)
    return [(lane >= HEAD_DIM * j) & (lane < HEAD_DIM * (j + 1)) for j in range(2)]


def _stack_heads(t, masks):
    zero = jnp.zeros_like(t)
    return jnp.concatenate([jnp.where(masks[0], t, zero), jnp.where(masks[1], t, zero)], axis=0)


def _stack_cols(t):
    return jnp.concatenate([t[:, 0:1], t[:, HEAD_DIM:HEAD_DIM + 1]], axis=0)


def _unstack(t2, masks):
    tq = t2.shape[0] // 2
    return jnp.where(masks[0], t2[:tq], t2[tq:])


def _stacked_ids(tq, tk):
    row = lax.broadcasted_iota(jnp.int32, (2 * tq, tk), 0)
    col = lax.broadcasted_iota(jnp.int32, (2 * tq, tk), 1)
    first = lax.broadcasted_iota(jnp.int32, (2 * tq, 1), 0) < tq
    return col - jnp.where(row < tq, row, row - tq), first


def _sweep(qi, tq, tk, step, init, leftward):
    per = tq // tk
    whole = lambda carry: lax.fori_loop(0, per * qi, lambda i, c: step(per * qi - 1 - i if leftward else i, c, None), carry)
    crossed = [(per * qi + j, -j * tk) for j in range(per)]
    if leftward:
        carry = init
        for kb, lead in reversed(crossed):
            carry = step(kb, carry, lead)
        return whole(carry)
    carry = whole(init)
    for kb, lead in crossed:
        carry = step(kb, carry, lead)
    return carry


def _fox_fwd(qkv, c_wide, c_row, batch, seq):
    tq, tk = FOX_TILES
    nq = seq // tq

    def body(q_ref, k_ref, v_ref, cw_ref, cr_ref, o_ref, lse_ref):
        hp, qi = pl.program_id(1), pl.program_id(2)
        masks = _head_masks()
        ahead, first = _stacked_ids(tq, tk)
        q2 = _stack_heads(q_ref[...], masks) * SCALE
        ct = _stack_cols(cw_ref[...])

        def step(kb, carry, lead):
            m, l, acc = carry
            k0 = pl.multiple_of(kb * tk, tk)
            cs = jnp.where(first, cr_ref[pl.ds(2 * hp, 1), pl.ds(k0, tk)], cr_ref[pl.ds(2 * hp + 1, 1), pl.ds(k0, tk)])
            s = _dot(q2, k_ref[pl.ds(k0, tk), :], NT_DIMS) + ct - cs
            if lead is not None:
                s = jnp.where(ahead <= lead, s, NEG)
            m_new = jnp.maximum(m, jnp.max(s, axis=1, keepdims=True))
            p = jnp.exp(s - m_new)
            alpha = jnp.exp(m - m_new)
            l = alpha * l + jnp.sum(p, axis=1, keepdims=True)
            acc = alpha * acc + _dot(p.astype(BF16), v_ref[pl.ds(k0, tk), :])
            return m_new, l, acc

        init = (jnp.full((2 * tq, 1), NEG, F32), jnp.zeros((2 * tq, 1), F32), jnp.zeros((2 * tq, LANES), F32))
        m, l, acc = _sweep(qi, tq, tk, step, init, leftward=False)
        o_ref[...] = _unstack(acc / l, masks).astype(BF16)
        lse_ref[...] = _unstack(m + jnp.log(l), masks)

    q_spec, k_spec, v_spec = _att_specs(seq, 0, tq)
    qb = _qblock_spec(seq, tq)
    return pl.pallas_call(
        body, name="fox_fwd", grid=(batch, N_PAIRS, nq),
        in_specs=[q_spec, k_spec, v_spec, qb, pl.BlockSpec((N_HEADS, seq), lambda b, hp, qi: (b, 0))],
        out_specs=[qb, qb],
        out_shape=[jax.ShapeDtypeStruct((batch * seq, D_ATT), BF16), jax.ShapeDtypeStruct((batch * seq, D_ATT), F32)],
        compiler_params=_cparams(("parallel", "parallel", "arbitrary")),
    )(qkv, qkv, qkv, c_wide, c_row)


def _fox_bwd(qkv, c_wide, c_row, o, do, lse_wide, batch, seq, behind):
    tq, tk = FOX_TILES
    nq = seq // tq

    def body(q_ref, k_ref, v_ref, cw_ref, cr_ref, o_ref, do_ref, lse_ref,
             dq_ref, dk_ref, dv_ref, dcs_ref, drs_ref, dkc_acc, dv_acc):
        hp, qi = pl.program_id(1), pl.program_id(2)

        @pl.when(qi == 0)
        def _():
            dkc_acc[...] = jnp.zeros_like(dkc_acc)
            dv_acc[...] = jnp.zeros_like(dv_acc)

        masks = _head_masks()
        ahead, first = _stacked_ids(tq, tk)
        q_t, do_t = q_ref[...], do_ref[...]
        q2 = _stack_heads(q_t, masks) * SCALE
        do2 = _stack_heads(do_t, masks)
        q_and_ones = jnp.concatenate([q2, _stack_heads(jnp.ones_like(q_t), masks)], axis=1)
        ct = _stack_cols(cw_ref[...])
        lse = _stack_cols(lse_ref[...])
        prod = do_t.astype(F32) * o_ref[...].astype(F32)
        delta = jnp.concatenate([jnp.sum(jnp.where(mk, prod, 0.0), axis=1, keepdims=True) for mk in masks], axis=0)

        def step(kb, carry, lead):
            dq_acc, rs = carry
            k0 = pl.multiple_of(kb * tk, tk)
            kblk = k_ref[pl.ds(k0, tk), :]
            cs = jnp.where(first, cr_ref[pl.ds(2 * hp, 1), pl.ds(k0, tk)], cr_ref[pl.ds(2 * hp + 1, 1), pl.ds(k0, tk)])
            p = jnp.exp(_dot(q2, kblk, NT_DIMS) + ct - cs - lse)
            if lead is not None:
                p = jnp.where(ahead <= lead, p, 0.0)
            dp = _dot(do2, v_ref[pl.ds(k0, tk), :], NT_DIMS)
            ds = (p * (dp - delta)).astype(BF16)
            dkc_acc[pl.ds(k0, tk), :] += _dot(ds, q_and_ones, TN_DIMS)
            dv_acc[pl.ds(k0, tk), :] += _dot(p.astype(BF16), do2, TN_DIMS)
            return dq_acc + _dot(ds, kblk), rs + jnp.sum(ds.astype(F32), axis=1, keepdims=True)

        init = (jnp.zeros((2 * tq, LANES), F32), jnp.zeros((2 * tq, 1), F32))
        dq_acc, rs = _sweep(qi, tq, tk, step, init, leftward=False)
        dq_ref[...] = (_unstack(dq_acc, masks) * SCALE).astype(BF16)
        drs_ref[...] = _unstack(rs, masks)

        @pl.when(qi == nq - 1)
        def _():
            dk_ref[...] = dkc_acc[:, 0:LANES].astype(BF16)
            dcs_ref[...] = dkc_acc[:, LANES:2 * LANES]
            dv_ref[...] = dv_acc[...].astype(BF16)

    q_spec, k_spec, v_spec = _att_specs(seq, 0, tq)
    qb = _qblock_spec(seq, tq)
    return _call_behind(
        body, behind, name="fox_bwd", grid=(batch, N_PAIRS, nq),
        in_specs=[q_spec, k_spec, v_spec, qb, pl.BlockSpec((N_HEADS, seq), lambda b, hp, qi: (b, 0)), qb, qb, qb],
        out_specs=[qb, _kv_out_spec(seq), _kv_out_spec(seq), _kv_out_spec(seq), qb],
        out_shape=[jax.ShapeDtypeStruct((batch * seq, D_ATT), BF16)] * 3 + [jax.ShapeDtypeStruct((batch * seq, D_ATT), F32)] * 2,
        scratch_shapes=[pltpu.VMEM((seq, 2 * LANES), F32), pltpu.VMEM((seq, LANES), F32)],
        operands=(qkv, qkv, qkv, c_wide, c_row, o, do, lse_wide))


def _sb_logits(q2, kblk):
    z = _dot(q2, kblk, NT_DIMS)
    lsn = jnp.minimum(-z, 0.0) - jnp.log(1.0 + jnp.exp(-jnp.abs(z)))
    return lsn + z, lsn


def _sb_fwd(qkv, batch, seq, behind):
    tq, tk = SB_TILES
    nq = seq // tq

    def body(q_ref, k_ref, v_ref, o_ref, rt_ref):
        qi = pl.program_id(2)
        masks = _head_masks()
        ahead, _ = _stacked_ids(tq, tk)
        later = _tri2(tk, lambda r, c: r > c)
        q2 = _stack_heads(q_ref[...], masks) * SCALE

        def step(kb, carry, lead):
            run, acc = carry
            k0 = pl.multiple_of(kb * tk, tk)
            ls, lsn = _sb_logits(q2, k_ref[pl.ds(k0, tk), :])
            if lead is not None:
                lsn = jnp.where(ahead < lead, lsn, 0.0)
            w = jnp.exp(ls + _dot(_split2(lsn), later) + run)
            if lead is not None:
                w = jnp.where(ahead < lead, w, 0.0)
            return run + jnp.sum(lsn, axis=1, keepdims=True), acc + _dot(w.astype(BF16), v_ref[pl.ds(k0, tk), :])

        init = (jnp.zeros((2 * tq, 1), F32), jnp.zeros((2 * tq, LANES), F32))
        run, acc = _sweep(qi, tq, tk, step, init, leftward=True)
        o_ref[...] = _unstack(acc, masks).astype(BF16)
        rt_ref[...] = _unstack(run, masks)

    q_spec, k_spec, v_spec = _att_specs(seq, 3 * N_PAIRS, tq)
    qb = _qblock_spec(seq, tq)
    return _call_behind(
        body, behind, name="sb_fwd", grid=(batch, N_PAIRS, nq), in_specs=[q_spec, k_spec, v_spec], out_specs=[qb, qb],
        out_shape=[jax.ShapeDtypeStruct((batch * seq, D_ATT), BF16), jax.ShapeDtypeStruct((batch * seq, D_ATT), F32)],
        scratch_shapes=[], operands=(qkv, qkv, qkv))


def _sb_bwd(qkv, do, rt_wide, batch, seq, behind):
    tq, tk = SB_TILES
    nq = seq // tq

    def body(q_ref, k_ref, v_ref, do_ref, rt_ref, dq_ref, dk_ref, dv_ref, dk_acc, dv_acc):
        qi = pl.program_id(2)

        @pl.when(qi == 0)
        def _():
            dk_acc[...] = jnp.zeros_like(dk_acc)
            dv_acc[...] = jnp.zeros_like(dv_acc)

        masks = _head_masks()
        ahead, _ = _stacked_ids(tq, tk)
        later = _tri2(tk, lambda r, c: r > c)
        earlier = _tri(tk, lambda r, c: r < c)
        q2 = _stack_heads(q_ref[...], masks) * SCALE
        do2 = _stack_heads(do_ref[...], masks)
        total = _stack_cols(rt_ref[...])

        def step(kb, carry, lead):
            pref, epre, dq_acc = carry
            k0 = pl.multiple_of(kb * tk, tk)
            kblk = k_ref[pl.ds(k0, tk), :]
            ls, lsn_all = _sb_logits(q2, kblk)
            lsn = lsn_all if lead is None else jnp.where(ahead < lead, lsn_all, 0.0)
            rs = jnp.sum(lsn, axis=1, keepdims=True)
            w = jnp.exp(ls + _dot(_split2(lsn), later) + (total - pref - rs))
            if lead is not None:
                w = jnp.where(ahead < lead, w, 0.0)
            e = w * _dot(do2, v_ref[pl.ds(k0, tk), :], NT_DIMS)
            before = _dot(e.astype(BF16), earlier) + epre
            dz = e * jnp.exp(lsn_all) - jnp.exp(ls) * before
            if lead is not None:
                dz = jnp.where(ahead < lead, dz, 0.0)
            dz = dz.astype(BF16)
            dk_acc[pl.ds(k0, tk), :] += _dot(dz, q2, TN_DIMS)
            dv_acc[pl.ds(k0, tk), :] += _dot(w.astype(BF16), do2, TN_DIMS)
            return pref + rs, epre + jnp.sum(e, axis=1, keepdims=True), dq_acc + _dot(dz, kblk)

        init = (jnp.zeros((2 * tq, 1), F32), jnp.zeros((2 * tq, 1), F32), jnp.zeros((2 * tq, LANES), F32))
        dq_acc = _sweep(qi, tq, tk, step, init, leftward=False)[2]
        dq_ref[...] = (_unstack(dq_acc, masks) * SCALE).astype(BF16)

        @pl.when(qi == nq - 1)
        def _():
            dk_ref[...] = dk_acc[...].astype(BF16)
            dv_ref[...] = dv_acc[...].astype(BF16)

    q_spec, k_spec, v_spec = _att_specs(seq, 3 * N_PAIRS, tq)
    qb = _qblock_spec(seq, tq)
    return _call_behind(
        body, behind, name="sb_bwd", grid=(batch, N_PAIRS, nq), in_specs=[q_spec, k_spec, v_spec, qb, qb],
        out_specs=[qb, _kv_out_spec(seq), _kv_out_spec(seq)], out_shape=[jax.ShapeDtypeStruct((batch * seq, D_ATT), BF16)] * 3,
        scratch_shapes=[pltpu.VMEM((seq, LANES), F32), pltpu.VMEM((seq, LANES), F32)], operands=(qkv, qkv, qkv, do, rt_wide))


def _local_step(x, p, target, w, rest, vec, place):
    batch, seq, _ = x.shape
    t = batch * seq
    x = x.reshape(t, D_MODEL)
    target = target.reshape(t, D_MODEL)
    p = p.reshape(t, D_PLE)
    big = dict(tm=1024, tn=1024, tk=1024)

    h1 = _norm_fwd(x, vec["g_mix"], "norm_mix")
    qkv = _mm(h1, w["qkv"], mode="nn", name="proj_qkv", out_dtype=BF16, **big)
    gl = _mm(h1, w["gate"], mode="nn", name="proj_gate", **big)
    fl = _mm(h1, w["forget"], mode="nn", name="proj_forget", **big)
    c_wide, c_row = _fox_prep(fl, vec["b_forget"], batch, seq)
    o_fox, lse_wide = _fox_fwd(qkv, c_wide, c_row, batch, seq)
    (o_sb, rt_wide), gathered = _sb_fwd(qkv, batch, seq, rest)
    w = dict(w, **_rest_weights(dict(zip(EARLY + ("b_gate",), gathered))))
    merged, of, os_ = _gate_fwd(gl, w["b_gate"], o_fox, o_sb, w["branch_fox"], w["branch_sb"])
    x1, h2 = _mm_res_norm(merged, w["out"], x, vec["g_mlp"], "proj_out_norm")
    ar = _mm(h2, w["up"], mode="nn", name="mlp_up", out_dtype=BF16, epi=lambda acc, _: jnp.maximum(acc, 0.0),
             col_shards=True, **big)
    x2, h3 = _mm_res_norm(ar, w["down"], x1, vec["g_ple"], "mlp_down_norm", a_fn=_relu2)

    dx3, dpre, dpe, dg_final, loss = _head_and_loss(x2, h3, p, w["ple_gate"], w["ple"], vec["g_final"], target)
    gw = {}
    gw["ple"] = _mm(p, dpe, mode="tn", name="d_w_ple", col_shards=True, **big)
    gw["ple_gate"] = _mm(h3, dpre, mode="tn", name="d_w_ple_gate", **big)
    dx2, dx2b, dg_ple = _mm_norm_bwd([(dpre, w["ple_gate"])], None, x2, vec["g_ple"], dx3, "d_h_ple_norm_bwd")
    gw["down"] = _mm(ar, dx2b, mode="tn", name="d_w_down", a_fn=_relu2, **big)
    da = _mm(dx2b, w["down"], mode="nt", name="d_act", out_dtype=BF16,
             epi=lambda acc, r: acc * (2.0 * r.astype(F32)), extra=ar, **big)
    gw["up"] = _mm(h2, da, mode="tn", name="d_w_up", col_shards=True, **big)
    dx1, dx1b, dg_mlp = _mm_norm_bwd([(da, w["up"])], None, x1, vec["g_mlp"], dx2, "d_h_mlp_norm_bwd")
    gw["out"] = _mm(merged, dx1b, mode="tn", name="d_w_out", **big)
    dof, dos, dgl, gw["b_gate"], do_fox, do_sb = _gate_bwd(gl, w["b_gate"], of, os_, dx1b, w["out"], w["branch_fox"],
                                                                  w["branch_sb"])
    gw["branch_fox"] = _mm(o_fox, dof, mode="tn", name="d_w_branch_fox", col_shards=True, **big)
    gw["branch_sb"] = _mm(o_sb, dos, mode="tn", name="d_w_branch_sb", col_shards=True, **big)
    early = _early_slots(gw)
    early = [early[n] for n in EARLY]
    (dq_a, dk_a, dv_a, dcs_wide, drs_wide), received = _fox_bwd(qkv, c_wide, c_row, o_fox, do_fox, lse_wide, batch, seq,
                                                                _swap_halves(early))
    sums = [_sum_sibling(place, s, r, "sum_sibling_" + n) for s, r, n in zip(early, received, EARLY)]
    (dq_b, dk_b, dv_b), others = _sb_bwd(qkv, do_sb, rt_wide, batch, seq, _exchange_chips(sums))
    mine = [_sum_chips(place, s, r, o, "sum_chips_" + n) for s, r, o, n in zip(early, received, others, EARLY)]
    dfl, db_forget = _fox_post(dcs_wide, drs_wide, fl, vec["b_forget"], batch, seq)
    dqkv = jnp.concatenate([dq_a, dk_a, dv_a, dq_b, dk_b, dv_b], axis=1)
    gw["qkv"], theirs = _mm(dqkv, h1, mode="tn", name="d_w_qkv", behind=_share_halves(mine), flat_out=True, **big)
    reduced = dict(zip(EARLY, zip(mine, theirs)))
    gw["gate"] = _mm(dgl, h1, mode="tn", name="d_w_gate", flat_out=True, **big)
    gw["forget"] = _mm(dfl, h1, mode="tn", name="d_w_forget", flat_out=True, **big)
    late = [_w_in_slots(gw)]
    dh1, received = _mm(dqkv, w["qkv"], mode="nt", name="d_h_qkv", behind=_swap_halves(late), **big)
    sums = [_sum_sibling(place, late[0], received[0], "sum_sibling_w_in")]
    (grad_x, _, dg_mix), others = _mm_norm_bwd([(dgl, w["gate"]), (dfl, w["forget"])], dh1, x, vec["g_mix"], dx1,
                                               "d_h_gate_norm_bwd", behind=_exchange_chips(sums))
    mine = [_sum_chips(place, late[0], received[0], others[0], "sum_chips_w_in")]
    reduced["w_in"] = (mine[0], _run_exchange(_share_halves(mine), "reduce_share_w_in")[0])
    gvec = {"g_mix": dg_mix, "b_forget": db_forget[:, 0:N_HEADS], "g_mlp": dg_mlp, "g_ple": dg_ple,
            "g_final": dg_final, "b_gate": gw["b_gate"]}
    return loss, grad_x.reshape(batch, seq, D_MODEL), reduced, gvec


ANY = pl.BlockSpec(memory_space=pl.ANY)
SHARDED = ("w_in", "w_branch_fox", "w_branch_sb", "w_out", "w_up", "w_down", "w_ple_gate", "w_ple")
ROW_ALIGN = 16
F32_ROWS = 8


def _place():
    return lax.axis_index("x"), lax.axis_index("y"), lax.axis_index("c")


def _other_chips(x, y):
    return [(1 - x, y), (x, 1 - y), (1 - x, 1 - y)]


def _half(ref, h):
    r = ref.shape[0] // 2
    assert r % ROW_ALIGN == 0
    return ref.at[pl.ds(pl.multiple_of(h * r, ROW_ALIGN), r)]


def _remote(src, dst, sems, idx, to):
    send_sems, recv_sems = sems
    return pltpu.make_async_remote_copy(src_ref=src, dst_ref=dst, send_sem=send_sems.at[idx], recv_sem=recv_sems.at[idx],
                                        device_id=to, device_id_type=MESH)


class _Exchange:
    def __init__(self, operands, out_shapes, sem_shape, start, finish):
        self.operands, self.out_shapes, self.sem_shape, self.start, self.finish = operands, out_shapes, sem_shape, start, finish

    def scratch(self):
        return [pltpu.SemaphoreType.DMA(self.sem_shape), pltpu.SemaphoreType.DMA(self.sem_shape)]


def _run_exchange(ex, name):
    n = len(ex.operands)

    def body(*refs):
        ex.start(refs[:n], refs[n:2 * n], refs[2 * n:])
        ex.finish(refs[:n], refs[n:2 * n], refs[2 * n:])

    return pl.pallas_call(body, name=name, in_specs=[ANY] * n, out_specs=[ANY] * n, out_shape=ex.out_shapes,
                          scratch_shapes=ex.scratch())(*ex.operands)


def _call_behind(body, ex, *, name, grid, in_specs, out_specs, out_shape, scratch_shapes, operands):
    n_in, n_out, nx = len(in_specs), len(out_specs), len(ex.operands)

    def wrapped(*refs):
        ins, x_in = refs[:n_in], refs[n_in:n_in + nx]
        outs, x_out = refs[n_in + nx:n_in + nx + n_out], refs[n_in + nx + n_out:n_in + 2 * nx + n_out]
        scratch, sems = refs[n_in + 2 * nx + n_out:-2], refs[-2:]
        first, last = None, None
        for d, steps in enumerate(grid):
            at_start, at_end = pl.program_id(d) == 0, pl.program_id(d) == steps - 1
            first = at_start if first is None else first & at_start
            last = at_end if last is None else last & at_end

        @pl.when(first)
        def _():
            ex.start(x_in, x_out, sems)

        body(*ins, *outs, *scratch)

        @pl.when(last)
        def _():
            ex.finish(x_in, x_out, sems)

    res = pl.pallas_call(
        wrapped, name=name, grid=grid, in_specs=list(in_specs) + [ANY] * nx, out_specs=list(out_specs) + [ANY] * nx,
        out_shape=list(out_shape) + list(ex.out_shapes), scratch_shapes=list(scratch_shapes) + ex.scratch(),
        compiler_params=_cparams(("arbitrary",) * len(grid)),
    )(*operands, *ex.operands)
    return res[:n_out], res[n_out:]


def _gather_weights(shards):
    n = len(shards)

    def first_copies(src, out, sems):
        x, y, c = _place()
        me = 2 * x + y
        copies = [_remote(_half(src[t], c), _half(out[t].at[me], c), sems, (t, k), (px, py, c))
                  for t in range(n) for k, (px, py) in enumerate(_other_chips(x, y))]
        return copies + [_remote(src[t], out[t].at[me], sems, (t, 3), (x, y, 1 - c)) for t in range(n)]

    def start(src, out, sems):
        for cp in first_copies(src, out, sems):
            cp.start()

    def finish(src, out, sems):
        x, y, c = _place()
        me = 2 * x + y
        sibling = (x, y, 1 - c)
        chips = _other_chips(x, y)
        passes = []
        for t in range(n):
            for k, (px, py) in enumerate(chips):
                landed = _half(out[t].at[2 * px + py], c)
                _remote(landed, landed, sems, (t, k), (px, py, c)).wait_recv()
                passes.append(_remote(landed, landed, sems, (t, 4 + k), sibling))
                passes[-1].start()
        for t in range(n):
            _remote(src[t], out[t].at[me], sems, (t, 3), sibling).wait_recv()
            for k, (px, py) in enumerate(chips):
                passed = _half(out[t].at[2 * px + py], 1 - c)
                _remote(passed, passed, sems, (t, 4 + k), sibling).wait_recv()
        for cp in first_copies(src, out, sems) + passes:
            cp.wait_send()

    return _Exchange(shards, [jax.ShapeDtypeStruct((N_CHIPS,) + s.shape, s.dtype) for s in shards], (n, 7), start, finish)


def _simple_exchange(operands, out_shapes, copies):
    def start(src, out, sems):
        for cp in copies(src, out, sems):
            cp.start()

    def finish(src, out, sems):
        for cp in copies(src, out, sems):
            cp.wait_recv()
        for cp in copies(src, out, sems):
            cp.wait_send()

    return _Exchange(operands, out_shapes, (len(operands),), start, finish)


def _swap_halves(slots):
    def copies(src, out, sems):
        x, y, c = _place()
        res = []
        for t in range(len(slots)):
            r = src[t].shape[1] // 2
            rows = pl.ds(pl.multiple_of((1 - c) * r, F32_ROWS), r)
            res.append(_remote(src[t].at[:, rows], out[t], sems, t, (x, y, 1 - c)))
        return res

    return _simple_exchange(slots, [jax.ShapeDtypeStruct((N_CHIPS, s.shape[1] // 2, s.shape[2]), s.dtype) for s in slots], copies)


def _exchange_chips(sums):
    n = len(sums)

    def copies(src, out, sems):
        x, y, c = _place()
        return [_remote(src[t].at[2 * px + py], out[t].at[k], sems, (t, k), (px, py, c))
                for t in range(n) for k, (px, py) in enumerate(_other_chips(x, y))]

    def start(src, out, sems):
        for cp in copies(src, out, sems):
            cp.start()

    def finish(src, out, sems):
        for cp in copies(src, out, sems):
            cp.wait_recv()
        for cp in copies(src, out, sems):
            cp.wait_send()

    return _Exchange(sums, [jax.ShapeDtypeStruct((3,) + s.shape[1:], s.dtype) for s in sums], (n, 3), start, finish)


def _share_halves(mine):
    def copies(src, out, sems):
        x, y, c = _place()
        return [_remote(src[t], out[t], sems, t, (x, y, 1 - c)) for t in range(len(mine))]

    return _simple_exchange(mine, [jax.ShapeDtypeStruct(s.shape, s.dtype) for s in mine], copies)


def _half_tile(rows):
    return 256 if rows % 256 == 0 else rows


def _sum_sibling(place, slot, received, name):
    n, rows2, cols = slot.shape
    rows = rows2 // 2
    tile = _half_tile(rows)
    nb = rows // tile

    def body(place_ref, a_ref, b_ref, o_ref):
        o_ref[...] = (a_ref[...] + b_ref[...]).astype(BF16)

    return pl.pallas_call(
        body, name=name, out_shape=jax.ShapeDtypeStruct((n, rows, cols), BF16),
        grid_spec=pltpu.PrefetchScalarGridSpec(
            num_scalar_prefetch=1, grid=(n, nb),
            in_specs=[pl.BlockSpec((None, tile, cols), lambda j, i, pr: (j, pr[1] * nb + i, 0)),
                      pl.BlockSpec((None, tile, cols), lambda j, i, pr: (j, i, 0))],
            out_specs=pl.BlockSpec((None, tile, cols), lambda j, i, pr: (j, i, 0))),
        compiler_params=_cparams(("parallel", "parallel")),
    )(place, slot, received)


def _sum_chips(place, slot, received, others, name):
    _, rows2, cols = slot.shape
    rows = rows2 // 2
    tile = _half_tile(rows)
    nb = rows // tile

    def body(place_ref, a_ref, b_ref, p_ref, o_ref):
        own = a_ref[...] + b_ref[...]
        o_ref[...] = ((own + p_ref[0].astype(F32)) + p_ref[1].astype(F32)) + p_ref[2].astype(F32)

    return pl.pallas_call(
        body, name=name, out_shape=jax.ShapeDtypeStruct((rows, cols), F32),
        grid_spec=pltpu.PrefetchScalarGridSpec(
            num_scalar_prefetch=1, grid=(nb,),
            in_specs=[pl.BlockSpec((None, tile, cols), lambda i, pr: (pr[0], pr[1] * nb + i, 0)),
                      pl.BlockSpec((None, tile, cols), lambda i, pr: (pr[0], i, 0)),
                      pl.BlockSpec((3, tile, cols), lambda i, pr: (0, i, 0))],
            out_specs=pl.BlockSpec((tile, cols), lambda i, pr: (i, 0))),
        compiler_params=_cparams(("parallel",)),
    )(place, slot, received, others)


N_DEVICES = 8


def _sum_devices(block, name):
    def body(v_ref, o_ref, land_ref, send_sems, recv_sems):
        x, y, c = _place()
        me = 4 * x + 2 * y + c
        copies = []
        for mask in range(1, N_DEVICES):
            peer = (x ^ (mask >> 2), y ^ ((mask >> 1) & 1), c ^ (mask & 1))
            copies.append(pltpu.make_async_remote_copy(src_ref=v_ref, dst_ref=land_ref.at[me], send_sem=send_sems.at[mask - 1],
                                                       recv_sem=recv_sems.at[mask - 1], device_id=peer, device_id_type=MESH))
        for cp in copies:
            cp.start()
        land_ref[me] = v_ref[...]
        for cp in copies:
            cp.wait_recv()
        total = land_ref[0]
        for d in range(1, N_DEVICES):
            total = total + land_ref[d]
        o_ref[...] = total
        for cp in copies:
            cp.wait_send()

    vmem = pl.BlockSpec(memory_space=pltpu.VMEM)
    return pl.pallas_call(
        body, name=name, in_specs=[vmem], out_specs=vmem, out_shape=jax.ShapeDtypeStruct(block.shape, F32),
        scratch_shapes=[pltpu.VMEM((N_DEVICES,) + block.shape, F32), pltpu.SemaphoreType.DMA((N_DEVICES - 1,)),
                        pltpu.SemaphoreType.DMA((N_DEVICES - 1,))],
    )(block)


def _vec_block(g_mix, g_mlp, g_ple, g_final, b_forget, b_gate_rows, last=None):
    pad = lambda a: jnp.concatenate([a, jnp.zeros((a.shape[0], D_MODEL - a.shape[1]), F32)], axis=1)
    last = jnp.zeros((1, 0), F32) if last is None else last
    return jnp.concatenate([g_mix, g_mlp, g_ple, g_final.reshape(1, D_MODEL), pad(b_forget), pad(b_gate_rows), pad(last)],
                           axis=0)


def _adam_math(w, g, m, v):
    m_new = ADAM_B1 * m + (1.0 - ADAM_B1) * g
    v_new = ADAM_B2 * v + (1.0 - ADAM_B2) * (g * g)
    m_hat = m_new / (1.0 - ADAM_B1 ** ADAM_STEP)
    v_hat = v_new / (1.0 - ADAM_B2 ** ADAM_STEP)
    return -ADAM_LR * (m_hat / (jnp.sqrt(v_hat) + ADAM_EPS) + ADAM_WD * w), m_new, v_new


def _adamw_halves(place, w, m, v, g_mine, g_theirs, name):
    rows2, cols = w.shape
    rows = rows2 // 2
    tile = _half_tile(rows)
    nb = rows // tile

    def body(place_ref, w_ref, m_ref, v_ref, gm_ref, gt_ref, g_ref, d_ref, nm_ref, nv_ref):
        g = jnp.where(pl.program_id(0) == 0, gm_ref[...], gt_ref[...])
        g_ref[...] = g
        d_ref[...], nm_ref[...], nv_ref[...] = _adam_math(w_ref[...], g, m_ref[...], v_ref[...])

    whole = pl.BlockSpec((tile, cols), lambda s, i, pr: ((pr[1] + s - 2 * pr[1] * s) * nb + i, 0))
    half = pl.BlockSpec((tile, cols), lambda s, i, pr: (i, 0))
    return pl.pallas_call(
        body, name=name, out_shape=[jax.ShapeDtypeStruct((rows2, cols), F32)] * 4,
        grid_spec=pltpu.PrefetchScalarGridSpec(num_scalar_prefetch=1, grid=(2, nb), in_specs=[whole] * 3 + [half] * 2,
                                               out_specs=[whole] * 4),
        compiler_params=_cparams(("parallel", "parallel")),
    )(place, w, m, v, g_mine, g_theirs)


def _adamw_vec(w, g, m, v):
    def body(w_ref, g_ref, m_ref, v_ref, d_ref, nm_ref, nv_ref):
        d_ref[...], nm_ref[...], nv_ref[...] = _adam_math(w_ref[...], g_ref[...], m_ref[...], v_ref[...])

    return pl.pallas_call(body, name="adamw_vectors", out_shape=[jax.ShapeDtypeStruct(w.shape, F32)] * 3)(w, g, m, v)


WEIGHT_NAMES = ("g_mix", "w_in", "b_forget", "b_gate", "w_branch_fox", "w_branch_sb", "w_out", "g_mlp", "w_up", "w_down",
                "g_ple", "w_ple_gate", "w_ple", "g_final")
W_IN_SHARD = D_IN // N_CHIPS
Q_END, F_END, B_END = 3 * D_ATT, 3 * D_ATT + N_HEADS, 6 * D_ATT + N_HEADS
GATE_SHARD = D_MODEL // N_CHIPS


LATE = SHARDED[:1]
EARLY = SHARDED[1:]


def _first_weights(w_in_slots):
    def cols(*ranges):
        parts = []
        for lo, hi in ranges:
            for j in range(N_CHIPS):
                a, b = max(lo, j * W_IN_SHARD), min(hi, (j + 1) * W_IN_SHARD)
                if a < b:
                    parts.append(w_in_slots[j, :, a - j * W_IN_SHARD:b - j * W_IN_SHARD])
        return parts

    forget = jnp.concatenate(cols((Q_END, F_END)) + [jnp.zeros((D_MODEL, F_PAD - N_HEADS), BF16)], axis=1)
    return {"qkv": jnp.concatenate(cols((0, Q_END), (F_END, B_END)), axis=1), "gate": jnp.concatenate(cols((B_END, D_IN)), axis=1),
            "forget": forget}


GATE_ROWS = 2 * ROW_ALIGN


def _gate_bits(b_gate):
    bits = lax.bitcast_convert_type(b_gate, BF16).reshape(2, 2 * GATE_SHARD)
    return jnp.concatenate([bits, jnp.zeros((GATE_ROWS - 2, 2 * GATE_SHARD), BF16)], axis=0)


def _rest_weights(gathered):
    rows = lambda a: a.reshape(N_CHIPS * a.shape[1], a.shape[2])
    bits = gathered["b_gate"][:, :2].reshape(N_CHIPS, 2, GATE_SHARD, 2)
    b_gate = jnp.transpose(lax.bitcast_convert_type(bits, F32), (1, 0, 2)).reshape(2, D_MODEL)
    return {"branch_fox": gathered["w_branch_fox"], "branch_sb": gathered["w_branch_sb"], "out": rows(gathered["w_out"]),
            "up": gathered["w_up"], "down": rows(gathered["w_down"]), "ple_gate": rows(gathered["w_ple_gate"]),
            "ple": gathered["w_ple"], "b_gate": b_gate}


def _early_slots(gw):
    rows = lambda a: a.reshape(N_CHIPS, a.shape[0] // N_CHIPS, a.shape[1])
    return {"w_branch_fox": gw["branch_fox"], "w_branch_sb": gw["branch_sb"], "w_out": rows(gw["out"]), "w_up": gw["up"],
            "w_down": rows(gw["down"]), "w_ple_gate": rows(gw["ple_gate"]), "w_ple": gw["ple"]}


W_IN_FLAT = (W_IN_SHARD * D_MODEL // LANES, LANES)


def _w_in_slots(gw):
    c = D_MODEL // LANES
    g_t = jnp.concatenate([gw["qkv"][:Q_END * c], gw["forget"][:N_HEADS * c], gw["qkv"][Q_END * c:], gw["gate"]], axis=0)
    return g_t.reshape((N_CHIPS,) + W_IN_FLAT)


def _flat(a):
    return jnp.transpose(a, (2, 0, 1)).reshape(W_IN_FLAT)


def _unflat(a):
    return jnp.transpose(a.reshape(W_IN_SHARD, D_MODEL // LANES, LANES), (1, 2, 0)).reshape(1, D_MODEL, W_IN_SHARD)


def kernel(x, p, g_mix, w_in, b_forget, b_gate, w_branch_fox, w_branch_sb, w_out, g_mlp, w_up, w_down, g_ple, w_ple_gate, w_ple, g_final, loss_target, m_g_mix, m_w_in, m_b_forget, m_b_gate, m_w_branch_fox, m_w_branch_sb, m_w_out, m_g_mlp, m_w_up, m_w_down, m_g_ple, m_w_ple_gate, m_w_ple, m_g_final, v_g_mix, v_w_in, v_b_forget, v_b_gate, v_w_branch_fox, v_w_branch_sb, v_w_out, v_g_mlp, v_w_up, v_w_down, v_g_ple, v_w_ple_gate, v_w_ple, v_g_final):
    weights = dict(g_mix=g_mix, w_in=w_in, b_forget=b_forget, b_gate=b_gate, w_branch_fox=w_branch_fox,
                   w_branch_sb=w_branch_sb, w_out=w_out, g_mlp=g_mlp, w_up=w_up, w_down=w_down, g_ple=g_ple,
                   w_ple_gate=w_ple_gate, w_ple=w_ple, g_final=g_final)
    first = dict(g_mix=m_g_mix, w_in=m_w_in, b_forget=m_b_forget, b_gate=m_b_gate, w_branch_fox=m_w_branch_fox,
                 w_branch_sb=m_w_branch_sb, w_out=m_w_out, g_mlp=m_g_mlp, w_up=m_w_up, w_down=m_w_down, g_ple=m_g_ple,
                 w_ple_gate=m_w_ple_gate, w_ple=m_w_ple, g_final=m_g_final)
    second = dict(g_mix=v_g_mix, w_in=v_w_in, b_forget=v_b_forget, b_gate=v_b_gate, w_branch_fox=v_w_branch_fox,
                  w_branch_sb=v_w_branch_sb, w_out=v_w_out, g_mlp=v_g_mlp, w_up=v_w_up, w_down=v_w_down, g_ple=v_g_ple,
                  w_ple_gate=v_w_ple_gate, w_ple=v_w_ple, g_final=v_g_final)
    cx, cy, cc = _place()
    chip = 2 * cx + cy
    place = jnp.stack([chip, cc]).astype(jnp.int32)
    col0 = chip * GATE_SHARD

    (w_in_slots,) = _run_exchange(_gather_weights([weights[n][0].astype(BF16) for n in LATE]), "gather_w_in")
    rest = _gather_weights([weights[n][0].astype(BF16) for n in EARLY] + [_gate_bits(b_gate[0])])
    vec = {"g_mix": g_mix, "b_forget": jnp.concatenate([b_forget, jnp.zeros((1, F_PAD - N_HEADS), F32)], axis=1),
           "g_mlp": g_mlp, "g_ple": g_ple, "g_final": g_final.reshape(1, D_MODEL)}

    loss, grad_x, reduced, gvec = _local_step(x, p[0], loss_target, _first_weights(w_in_slots), rest, vec, place)

    out = {}
    for n in EARLY:
        g_mine, g_theirs = reduced[n]
        res = _adamw_halves(place, weights[n][0], first[n][0], second[n][0], g_mine, g_theirs, "adamw_" + n)
        out[n] = [r[None] for r in res]
    g_mine, g_theirs = reduced["w_in"]
    out["w_in"] = [_unflat(r) for r in _adamw_halves(place, _flat(w_in), _flat(m_w_in), _flat(v_w_in), g_mine, g_theirs,
                                                      "adamw_w_in")]

    g_block = _sum_devices(_vec_block(gvec["g_mix"], gvec["g_mlp"], gvec["g_ple"], gvec["g_final"][0], gvec["b_forget"],
                                      gvec["b_gate"], loss), "reduce_vectors")
    loss = g_block[7, 0]
    g_gate = lax.dynamic_slice(g_block[5:7], (0, col0), (2, GATE_SHARD))
    blocks = [_vec_block(d["g_mix"], d["g_mlp"], d["g_ple"], d["g_final"], d["b_forget"], d["b_gate"][0])
              for d in (weights, first, second)]
    g_rows = jnp.concatenate([g_block[0:5], jnp.concatenate([g_gate, jnp.zeros((2, D_MODEL - GATE_SHARD), F32)], axis=1),
                              jnp.zeros((1, D_MODEL), F32)], axis=0)
    res = (g_rows,) + tuple(_adamw_vec(blocks[0], g_rows, blocks[1], blocks[2]))
    out["g_mix"] = [r[0:1] for r in res]
    out["g_mlp"] = [r[1:2] for r in res]
    out["g_ple"] = [r[2:3] for r in res]
    out["g_final"] = [r[3] for r in res]
    out["b_forget"] = [r[4:5, :N_HEADS] for r in res]
    out["b_gate"] = [r[5:7, :GATE_SHARD][None] for r in res]
    return (loss, grad_x, *[out[n][0] for n in WEIGHT_NAMES], *[out[n][1] for n in WEIGHT_NAMES],
            *[out[n][2] for n in WEIGHT_NAMES], *[out[n][3] for n in WEIGHT_NAMES])
```

```python
import jax
import jax.numpy as jnp
from jax import lax
from jax.experimental import pallas as pl
from jax.experimental.pallas import tpu as pltpu

F32 = jnp.float32
BF16 = jnp.bfloat16

D_MODEL = 1024
HEAD_DIM = 64
N_HEADS = 8
D_ATT = N_HEADS * HEAD_DIM
D_PLE = 256
D_IN = 6 * D_ATT + N_HEADS + 2 * D_MODEL
F_PAD = 128
EPS = 1e-6
SCALE = HEAD_DIM ** -0.5
N_CHIPS = 4
LANES = 128
ATT_BLOCK = 256
FOX_TILES = (512, 512)
SB_TILES = (512, 256)
NEG = -1e30

ADAM_LR = 0.001
ADAM_B1 = 0.9
ADAM_B2 = 0.999
ADAM_EPS = 1e-08
ADAM_WD = 0.01
ADAM_STEP = 10

VMEM_LIMIT = 56 * 1024 * 1024

MESH = pl.DeviceIdType.MESH


def _cparams(sem=None):
    return pltpu.CompilerParams(dimension_semantics=sem, vmem_limit_bytes=VMEM_LIMIT)


def _relu2(t):
    t = t.astype(F32)
    return t * t


_DIMS = {"nn": (((1,), (0,)), ((), ())), "nt": (((1,), (1,)), ((), ())), "tn": (((0,), (0,)), ((), ()))}
NT_DIMS = _DIMS["nt"]
TN_DIMS = _DIMS["tn"]


def _mm(a, b, *, mode, name, out_dtype=F32, tm=512, tn=512, tk=512, add=None, a_fn=None, epi=None, extra=None,
        col_shards=False, behind=None, flat_out=False):
    if mode == "nn":
        (m, k), n = a.shape, b.shape[-1]
    elif mode == "nt":
        (m, k), n = a.shape, b.shape[-2]
    else:
        (k, m), n = a.shape, b.shape[1]
    shard = None
    if col_shards:
        if mode == "nn":
            shard, n = n, N_CHIPS * n
            tn = min(tn, shard)
        elif mode == "nt":
            shard = b.shape[-1]
            tk = min(tk, shard)
        else:
            shard = n // N_CHIPS
            tn = min(tn, shard)
    tm, tn, tk = min(tm, m), min(tn, n), min(tk, k)
    assert m % tm == 0 and n % tn == 0 and k % tk == 0, (name, m, n, k)
    nk = k // tk
    a_spec = {"nn": pl.BlockSpec((tm, tk), lambda i, j, kk: (i, kk)),
              "nt": pl.BlockSpec((tm, tk), lambda i, j, kk: (i, kk)),
              "tn": pl.BlockSpec((tk, tm), lambda i, j, kk: (kk, i))}[mode]
    b_spec = {"nn": pl.BlockSpec((tk, tn), lambda i, j, kk: (kk, j)),
              "nt": pl.BlockSpec((tn, tk), lambda i, j, kk: (j, kk)),
              "tn": pl.BlockSpec((tk, tn), lambda i, j, kk: (kk, j))}[mode]
    o_spec = pl.BlockSpec((tm, tn), lambda i, j, kk: (i, j))
    out_shape = (m, n)
    if col_shards and mode == "nn":
        per = shard // tn
        b_spec = pl.BlockSpec((None, tk, tn), lambda i, j, kk: (j // per, kk, j % per))
    elif col_shards and mode == "nt":
        per = shard // tk
        b_spec = pl.BlockSpec((None, tn, tk), lambda i, j, kk: (kk // per, j, kk % per))
    elif col_shards:
        assert add is None and extra is None
        per = shard // tn
        o_spec = pl.BlockSpec((None, tm, tn), lambda i, j, kk: (j // per, i, j % per))
        out_shape = (N_CHIPS, m, shard)
    if flat_out:
        assert mode == "tn" and tn == n == D_MODEL and not col_shards and add is None and extra is None
        chunks = D_MODEL // LANES
        o_spec = pl.BlockSpec((tm * chunks, LANES), lambda i, j, kk: (i, 0))
        out_shape = (m * chunks, LANES)
    operands, in_specs = [a, b], [a_spec, b_spec]
    third = add if add is not None else extra
    if third is not None:
        operands.append(third)
        in_specs.append(o_spec)

    def body(*refs):
        a_ref, b_ref = refs[0], refs[1]
        t_ref = refs[2] if third is not None else None
        o_ref = refs[3] if third is not None else refs[2]
        acc_ref = refs[-1] if nk > 1 else None
        at = a_ref[...]
        if a_fn is not None:
            at = a_fn(at)
        part = lax.dot_general(at.astype(BF16), b_ref[...].astype(BF16), _DIMS[mode], preferred_element_type=F32)

        def finish(acc):
            if epi is not None:
                acc = epi(acc, None if t_ref is None else t_ref[...])
            elif add is not None:
                acc = acc + t_ref[...].astype(F32)
            if flat_out:
                for q in range(D_MODEL // LANES):
                    o_ref[pl.ds(q, tm, stride=D_MODEL // LANES), :] = acc[:, q * LANES:(q + 1) * LANES].astype(o_ref.dtype)
                return
            o_ref[...] = acc.astype(o_ref.dtype)

        if nk == 1:
            finish(part)
        else:
            kk = pl.program_id(2)

            @pl.when(kk == 0)
            def _():
                acc_ref[...] = part

            @pl.when(kk > 0)
            def _():
                acc_ref[...] += part

            @pl.when(kk == nk - 1)
            def _():
                finish(acc_ref[...])

    call = dict(name=name, grid=(m // tm, n // tn, nk), in_specs=in_specs,
                scratch_shapes=[pltpu.VMEM((tm, tn), F32)] if nk > 1 else [])
    if behind is not None:
        (res,), exchanged = _call_behind(body, behind, out_specs=[o_spec], out_shape=[jax.ShapeDtypeStruct(out_shape, out_dtype)],
                                         operands=operands, **call)
        return res, exchanged
    return pl.pallas_call(body, out_specs=o_spec, out_shape=jax.ShapeDtypeStruct(out_shape, out_dtype),
                          compiler_params=_cparams(("parallel", "parallel", "arbitrary")), **call)(*operands)


ROW_TILE = 512


def _row_spec(width=D_MODEL, rows=ROW_TILE):
    return pl.BlockSpec((rows, width), lambda i: (i, 0))


def _vec_spec(rows=1, width=D_MODEL):
    return pl.BlockSpec((rows, width), lambda i: (0, 0))


def _xhat(x):
    r = lax.rsqrt(jnp.mean(x * x, axis=-1, keepdims=True) + EPS)
    return x * r, r


def _rms_bwd_rows(dh, x, g):
    xh, r = _xhat(x)
    dxh = dh * g
    dx = r * (dxh - xh * jnp.mean(dxh * xh, axis=-1, keepdims=True))
    return dx, jnp.sum(dh * xh, axis=0, keepdims=True)


def _norm_fwd(x, g, name):
    t = x.shape[0]

    def body(x_ref, g_ref, h_ref):
        xh, _ = _xhat(x_ref[...])
        h_ref[...] = (xh * g_ref[...]).astype(BF16)

    return pl.pallas_call(
        body, name=name, grid=(t // ROW_TILE,), in_specs=[_row_spec(), _vec_spec()], out_specs=_row_spec(),
        out_shape=jax.ShapeDtypeStruct((t, D_MODEL), BF16), compiler_params=_cparams(("parallel",)),
    )(x, g)


def _mm_res_norm(a, b, res, g, name, a_fn=None):
    t, k = a.shape

    def body(a_ref, b_ref, res_ref, g_ref, x_ref, h_ref):
        at = a_ref[...] if a_fn is None else a_fn(a_ref[...])
        x_new = res_ref[...] + _dot(at.astype(BF16), b_ref[...])
        x_ref[...] = x_new
        h_ref[...] = (_xhat(x_new)[0] * g_ref[...]).astype(BF16)

    return pl.pallas_call(
        body, name=name, grid=(t // ROW_TILE,),
        in_specs=[pl.BlockSpec((ROW_TILE, k), lambda i: (i, 0)), pl.BlockSpec(b.shape, lambda i: (0, 0)), _row_spec(), _vec_spec()],
        out_specs=[_row_spec(), _row_spec()],
        out_shape=[jax.ShapeDtypeStruct((t, D_MODEL), F32), jax.ShapeDtypeStruct((t, D_MODEL), BF16)],
        compiler_params=_cparams(("parallel",)),
    )(a, b, res, g)


def _mm_norm_bwd(pairs, dh_first, x, g, dres, name, behind=None):
    t = x.shape[0]
    operands, in_specs = [], []
    for a, b in pairs:
        if b.ndim == 3:
            for j in range(b.shape[0]):
                operands += [a, b]
                in_specs += [pl.BlockSpec((ROW_TILE, b.shape[2]), lambda i, j=j: (i, j)),
                             pl.BlockSpec((None, D_MODEL, b.shape[2]), lambda i, j=j: (j, 0, 0))]
        else:
            operands += [a, b]
            in_specs += [pl.BlockSpec((ROW_TILE, a.shape[1]), lambda i: (i, 0)), pl.BlockSpec(b.shape, lambda i: (0, 0))]
    n_mm = len(operands)
    operands += [x, g, dres] + ([] if dh_first is None else [dh_first])
    in_specs += [_row_spec(), _vec_spec(), _row_spec()] + ([] if dh_first is None else [_row_spec()])

    def body(*refs):
        x_ref, g_ref, dres_ref = refs[n_mm:n_mm + 3]
        dx_ref, dxb_ref, dg_ref = refs[-3:]
        dh = 0.0 if dh_first is None else refs[n_mm + 3][...]
        for k in range(0, n_mm, 2):
            dh = dh + lax.dot_general(refs[k][...].astype(BF16), refs[k + 1][...].astype(BF16), NT_DIMS,
                                      preferred_element_type=F32)
        dx, dg = _rms_bwd_rows(dh, x_ref[...], g_ref[...])
        dx = dx + dres_ref[...]
        dx_ref[...] = dx
        dxb_ref[...] = dx.astype(BF16)

        @pl.when(pl.program_id(0) == 0)
        def _():
            dg_ref[...] = jnp.zeros_like(dg_ref)

        dg_ref[...] += dg

    call = dict(name=name, grid=(t // ROW_TILE,), in_specs=in_specs, out_specs=[_row_spec(), _row_spec(), _vec_spec()],
                out_shape=[jax.ShapeDtypeStruct((t, D_MODEL), F32), jax.ShapeDtypeStruct((t, D_MODEL), BF16),
                           jax.ShapeDtypeStruct((1, D_MODEL), F32)])
    if behind is not None:
        return _call_behind(body, behind, scratch_shapes=[], operands=operands, **call)
    return pl.pallas_call(body, compiler_params=_cparams(("arbitrary",)), **call)(*operands)


def _shards_spec(w):
    return pl.BlockSpec(w.shape, lambda i: (0, 0, 0))


def _gate_fwd(gl, b_gate, o_fox, o_sb, w_fox, w_sb):
    t = o_fox.shape[0]

    def body(gla_ref, glb_ref, b_ref, ofox_ref, osb_ref, wf_ref, ws_ref, m_ref, of_ref, os_ref):
        of = jnp.concatenate([_dot(ofox_ref[...], wf_ref[j]) for j in range(N_CHIPS)], axis=1)
        os_ = jnp.concatenate([_dot(osb_ref[...], ws_ref[j]) for j in range(N_CHIPS)], axis=1)
        ga = jax.nn.sigmoid(gla_ref[...] + b_ref[0:1, :])
        gb = jax.nn.sigmoid(glb_ref[...] + b_ref[1:2, :])
        of_ref[...] = of
        os_ref[...] = os_
        m_ref[...] = (ga * of + gb * os_).astype(BF16)

    return pl.pallas_call(
        body, name="gate_fwd", grid=(t // ROW_TILE,),
        in_specs=[pl.BlockSpec((ROW_TILE, D_MODEL), lambda i: (i, 0)), pl.BlockSpec((ROW_TILE, D_MODEL), lambda i: (i, 1)),
                  _vec_spec(2), _row_spec(D_ATT), _row_spec(D_ATT), _shards_spec(w_fox), _shards_spec(w_sb)],
        out_specs=[_row_spec(), _row_spec(), _row_spec()],
        out_shape=[jax.ShapeDtypeStruct((t, D_MODEL), BF16)] + [jax.ShapeDtypeStruct((t, D_MODEL), F32)] * 2,
        compiler_params=_cparams(("parallel",)),
    )(gl, gl, b_gate, o_fox, o_sb, w_fox, w_sb)


def _gate_bwd(gl, b_gate, of, os_, dx, w_out, w_fox, w_sb):
    t = of.shape[0]
    shard = D_MODEL // N_CHIPS

    def back(d, w_ref):
        return sum(_dot(d[:, j * shard:(j + 1) * shard], w_ref[j], NT_DIMS) for j in range(N_CHIPS)).astype(BF16)

    def body(gla_ref, glb_ref, b_ref, of_ref, os_ref, dx_ref, w_ref, wf_ref, ws_ref,
             dof_ref, dos_ref, dgl_ref, db_ref, dofox_ref, dosb_ref):
        dm = _dot(dx_ref[...], w_ref[...], NT_DIMS)
        ga = jax.nn.sigmoid(gla_ref[...] + b_ref[0:1, :])
        gb = jax.nn.sigmoid(glb_ref[...] + b_ref[1:2, :])
        dof = (dm * ga).astype(BF16)
        dos = (dm * gb).astype(BF16)
        dof_ref[...] = dof
        dos_ref[...] = dos
        dofox_ref[...] = back(dof, wf_ref)
        dosb_ref[...] = back(dos, ws_ref)
        dgla = dm * of_ref[...] * ga * (1.0 - ga)
        dglb = dm * os_ref[...] * gb * (1.0 - gb)
        dgl_ref[:, 0:D_MODEL] = dgla.astype(BF16)
        dgl_ref[:, D_MODEL:2 * D_MODEL] = dglb.astype(BF16)

        @pl.when(pl.program_id(0) == 0)
        def _():
            db_ref[...] = jnp.zeros_like(db_ref)

        db_ref[0:1, :] += jnp.sum(dgla, axis=0, keepdims=True)
        db_ref[1:2, :] += jnp.sum(dglb, axis=0, keepdims=True)

    outs = pl.pallas_call(
        body, name="gate_bwd", grid=(t // ROW_TILE,),
        in_specs=[pl.BlockSpec((ROW_TILE, D_MODEL), lambda i: (i, 0)), pl.BlockSpec((ROW_TILE, D_MODEL), lambda i: (i, 1)),
                  _vec_spec(2), _row_spec(), _row_spec(), _row_spec(), pl.BlockSpec(w_out.shape, lambda i: (0, 0)),
                  _shards_spec(w_fox), _shards_spec(w_sb)],
        out_specs=[_row_spec(), _row_spec(), _row_spec(2 * D_MODEL), _vec_spec(2), _row_spec(D_ATT), _row_spec(D_ATT)],
        out_shape=[jax.ShapeDtypeStruct((t, D_MODEL), BF16)] * 2 + [jax.ShapeDtypeStruct((t, 2 * D_MODEL), BF16),
                                                                      jax.ShapeDtypeStruct((2, D_MODEL), F32)]
        + [jax.ShapeDtypeStruct((t, D_ATT), BF16)] * 2,
        compiler_params=_cparams(("arbitrary",)),
    )(gl, gl, b_gate, of, os_, dx, w_out, w_fox, w_sb)
    return outs


def _head_and_loss(x2, h3, p, w_gate, w_ple, g_final, target):
    t = x2.shape[0]

    def body(x2_ref, h3_ref, p_ref, wg_ref, wp_ref, g_ref, tgt_ref, dx3_ref, dpre_ref, dpe_ref, dg_ref, loss_ref):
        gp = jax.nn.sigmoid(_dot(h3_ref[...], wg_ref[...]))
        p_t = p_ref[...].astype(BF16)
        pe_t = jnp.concatenate([_dot(p_t, wp_ref[j]) for j in range(N_CHIPS)], axis=1)
        x3 = x2_ref[...] + gp * pe_t
        g = g_ref[...]
        xh, _ = _xhat(x3)
        err = xh * g - tgt_ref[...]
        dy = err * (1.0 / D_MODEL)
        dx3, dg = _rms_bwd_rows(dy, x3, g)
        dx3_ref[...] = dx3
        dpre_ref[...] = (dx3 * pe_t * gp * (1.0 - gp)).astype(BF16)
        dpe_ref[...] = (dx3 * gp).astype(BF16)

        @pl.when(pl.program_id(0) == 0)
        def _():
            dg_ref[...] = jnp.zeros_like(dg_ref)
            loss_ref[...] = jnp.zeros_like(loss_ref)

        dg_ref[...] += dg
        loss_ref[...] += 0.5 * jnp.sum(jnp.mean(err * err, axis=-1, keepdims=True), axis=0, keepdims=True)

    return pl.pallas_call(
        body, name="head_and_loss", grid=(t // ROW_TILE,),
        in_specs=[_row_spec(), _row_spec(), _row_spec(D_PLE), pl.BlockSpec(w_gate.shape, lambda i: (0, 0)),
                  pl.BlockSpec(w_ple.shape, lambda i: (0, 0, 0)), _vec_spec(), _row_spec()],
        out_specs=[_row_spec(), _row_spec(), _row_spec(), _vec_spec(), _vec_spec(1, LANES)],
        out_shape=[jax.ShapeDtypeStruct((t, D_MODEL), F32), jax.ShapeDtypeStruct((t, D_MODEL), BF16),
                   jax.ShapeDtypeStruct((t, D_MODEL), BF16), jax.ShapeDtypeStruct((1, D_MODEL), F32),
                   jax.ShapeDtypeStruct((1, LANES), F32)],
        compiler_params=_cparams(("arbitrary",)),
    )(x2, h3, p, w_gate, w_ple, g_final, target)


def _split3(v):
    hi = v.astype(BF16)
    r1 = v - hi.astype(F32)
    mid = r1.astype(BF16)
    lo = (r1 - mid.astype(F32)).astype(BF16)
    return hi, mid, lo


def _split2(v):
    hi = v.astype(BF16)
    return jnp.concatenate([hi, (v - hi.astype(F32)).astype(BF16)], axis=1)


def _dot(a, b, dims=_DIMS["nn"]):
    return lax.dot_general(a, b, dims, preferred_element_type=F32)


def _tri(n, rel):
    row = lax.broadcasted_iota(jnp.int32, (n, n), 0)
    col = lax.broadcasted_iota(jnp.int32, (n, n), 1)
    return rel(row, col).astype(BF16)


def _tri2(n, rel):
    t = _tri(n, rel)
    return jnp.concatenate([t, t], axis=0)


def _log_sigmoid(v):
    return -(jnp.maximum(-v, 0.0) + jnp.log(1.0 + jnp.exp(-jnp.abs(v))))


def _fox_prep(fl, b_forget, batch, seq):
    nb = seq // ATT_BLOCK

    def body(fl_ref, b_ref, cw_ref, cr_ref):
        col = lax.broadcasted_iota(jnp.int32, (ATT_BLOCK, F_PAD), 1)
        lower = _tri(ATT_BLOCK, lambda r, c: c <= r)
        upper = _tri(ATT_BLOCK, lambda r, c: r <= c)
        expand = (lax.broadcasted_iota(jnp.int32, (F_PAD, D_ATT), 1) // HEAD_DIM
                  == lax.broadcasted_iota(jnp.int32, (F_PAD, D_ATT), 0)).astype(BF16)
        carry_w = jnp.zeros((1, D_ATT), F32)
        carry_r = jnp.zeros((F_PAD, 1), F32)
        for i in range(nb):
            blk = slice(i * ATT_BLOCK, (i + 1) * ATT_BLOCK)
            logf = jnp.where(col < N_HEADS, _log_sigmoid(fl_ref[blk, :] + b_ref[...]), 0.0)
            cw = jnp.zeros((ATT_BLOCK, D_ATT), F32) + carry_w
            cr = jnp.zeros((F_PAD, ATT_BLOCK), F32) + carry_r
            for part in _split3(logf):
                cw += _dot(lower, _dot(part, expand).astype(BF16))
                cr += _dot(part, upper, TN_DIMS)
            cw_ref[blk, :] = cw
            cr_ref[:, blk] = cr[0:N_HEADS, :]
            carry_w = cw[ATT_BLOCK - 1:ATT_BLOCK, :]
            carry_r = cr[:, ATT_BLOCK - 1:ATT_BLOCK]

    return pl.pallas_call(
        body, name="fox_prep", grid=(batch,),
        in_specs=[pl.BlockSpec((seq, F_PAD), lambda b: (b, 0)), pl.BlockSpec((1, F_PAD), lambda b: (0, 0))],
        out_specs=[pl.BlockSpec((seq, D_ATT), lambda b: (b, 0)), pl.BlockSpec((N_HEADS, seq), lambda b: (b, 0))],
        out_shape=[jax.ShapeDtypeStruct((batch * seq, D_ATT), F32), jax.ShapeDtypeStruct((batch * N_HEADS, seq), F32)],
        compiler_params=_cparams(("parallel",)),
    )(fl, b_forget)


def _fox_post(dcs_wide, drs_wide, fl, b_forget, batch, seq):
    nb = seq // ATT_BLOCK

    def body(dcs_ref, drs_ref, fl_ref, b_ref, dfl_ref, db_ref):
        pick = (lax.broadcasted_iota(jnp.int32, (D_ATT, F_PAD), 0)
                == lax.broadcasted_iota(jnp.int32, (D_ATT, F_PAD), 1) * HEAD_DIM).astype(BF16)
        upper = _tri(ATT_BLOCK, lambda r, c: r <= c)
        col = lax.broadcasted_iota(jnp.int32, (ATT_BLOCK, F_PAD), 1)

        @pl.when(pl.program_id(0) == 0)
        def _():
            db_ref[...] = jnp.zeros_like(db_ref)

        carry = jnp.zeros((1, F_PAD), F32)
        for i in reversed(range(nb)):
            blk = slice(i * ATT_BLOCK, (i + 1) * ATT_BLOCK)
            narrow = jnp.zeros((ATT_BLOCK, F_PAD), F32)
            for part in _split3(drs_ref[blk, :] - dcs_ref[blk, :]):
                narrow += _dot(part, pick)
            after = jnp.zeros((ATT_BLOCK, F_PAD), F32) + carry
            for part in _split3(narrow):
                after += _dot(upper, part)
            carry = after[0:1, :]
            pre = fl_ref[blk, :] + b_ref[...]
            dfl = jnp.where(col < N_HEADS, after * jax.nn.sigmoid(-pre), 0.0)
            dfl_ref[blk, :] = dfl.astype(BF16)
            db_ref[...] += jnp.sum(dfl, axis=0, keepdims=True)

    return pl.pallas_call(
        body, name="fox_post", grid=(batch,),
        in_specs=[pl.BlockSpec((seq, D_ATT), lambda b: (b, 0)), pl.BlockSpec((seq, D_ATT), lambda b: (b, 0)),
                  pl.BlockSpec((seq, F_PAD), lambda b: (b, 0)), pl.BlockSpec((1, F_PAD), lambda b: (0, 0))],
        out_specs=[pl.BlockSpec((seq, F_PAD), lambda b: (b, 0)), pl.BlockSpec((1, F_PAD), lambda b: (0, 0))],
        out_shape=[jax.ShapeDtypeStruct((batch * seq, F_PAD), BF16), jax.ShapeDtypeStruct((1, F_PAD), F32)],
        compiler_params=_cparams(("arbitrary",)),
    )(dcs_wide, drs_wide, fl, b_forget)


N_PAIRS = N_HEADS // 2


def _att_specs(seq, col0, tq):
    nq = seq // tq
    q = pl.BlockSpec((tq, LANES), lambda b, hp, qi: (b * nq + qi, col0 + hp))
    k = pl.BlockSpec((seq, LANES), lambda b, hp, qi: (b, col0 + N_PAIRS + hp))
    v = pl.BlockSpec((seq, LANES), lambda b, hp, qi: (b, col0 + 2 * N_PAIRS + hp))
    return q, k, v


def _qblock_spec(seq, tq):
    nq = seq // tq
    return pl.BlockSpec((tq, LANES), lambda b, hp, qi: (b * nq + qi, hp))


def _kv_out_spec(seq):
    return pl.BlockSpec((seq, LANES), lambda b, hp, qi: (b, hp))


def _head_masks():
    lane = lax.broadcasted_iota(jnp.int32, (1, LANES), 1)
    return [(lane >= HEAD_DIM * j) & (lane < HEAD_DIM * (j + 1)) for j in range(2)]


def _stack_heads(t, masks):
    zero = jnp.zeros_like(t)
    return jnp.concatenate([jnp.where(masks[0], t, zero), jnp.where(masks[1], t, zero)], axis=0)


def _stack_cols(t):
    return jnp.concatenate([t[:, 0:1], t[:, HEAD_DIM:HEAD_DIM + 1]], axis=0)


def _unstack(t2, masks):
    tq = t2.shape[0] // 2
    return jnp.where(masks[0], t2[:tq], t2[tq:])


def _stacked_ids(tq, tk):
    row = lax.broadcasted_iota(jnp.int32, (2 * tq, tk), 0)
    col = lax.broadcasted_iota(jnp.int32, (2 * tq, tk), 1)
    first = lax.broadcasted_iota(jnp.int32, (2 * tq, 1), 0) < tq
    return col - jnp.where(row < tq, row, row - tq), first


def _sweep(qi, tq, tk, step, init, leftward):
    per = tq // tk
    whole = lambda carry: lax.fori_loop(0, per * qi, lambda i, c: step(per * qi - 1 - i if leftward else i, c, None), carry)
    crossed = [(per * qi + j, -j * tk) for j in range(per)]
    if leftward:
        carry = init
        for kb, lead in reversed(crossed):
            carry = step(kb, carry, lead)
        return whole(carry)
    carry = whole(init)
    for kb, lead in crossed:
        carry = step(kb, carry, lead)
    return carry


def _fox_fwd(qkv, c_wide, c_row, batch, seq):
    tq, tk = FOX_TILES
    nq = seq // tq

    def body(q_ref, k_ref, v_ref, cw_ref, cr_ref, o_ref, lse_ref):
        hp, qi = pl.program_id(1), pl.program_id(2)
        masks = _head_masks()
        ahead, first = _stacked_ids(tq, tk)
        q2 = _stack_heads(q_ref[...], masks) * SCALE
        ct = _stack_cols(cw_ref[...])

        def step(kb, carry, lead):
            m, l, acc = carry
            k0 = pl.multiple_of(kb * tk, tk)
            cs = jnp.where(first, cr_ref[pl.ds(2 * hp, 1), pl.ds(k0, tk)], cr_ref[pl.ds(2 * hp + 1, 1), pl.ds(k0, tk)])
            s = _dot(q2, k_ref[pl.ds(k0, tk), :], NT_DIMS) + ct - cs
            if lead is not None:
                s = jnp.where(ahead <= lead, s, NEG)
            m_new = jnp.maximum(m, jnp.max(s, axis=1, keepdims=True))
            p = jnp.exp(s - m_new)
            alpha = jnp.exp(m - m_new)
            l = alpha * l + jnp.sum(p, axis=1, keepdims=True)
            acc = alpha * acc + _dot(p.astype(BF16), v_ref[pl.ds(k0, tk), :])
            return m_new, l, acc

        init = (jnp.full((2 * tq, 1), NEG, F32), jnp.zeros((2 * tq, 1), F32), jnp.zeros((2 * tq, LANES), F32))
        m, l, acc = _sweep(qi, tq, tk, step, init, leftward=False)
        o_ref[...] = _unstack(acc / l, masks).astype(BF16)
        lse_ref[...] = _unstack(m + jnp.log(l), masks)

    q_spec, k_spec, v_spec = _att_specs(seq, 0, tq)
    qb = _qblock_spec(seq, tq)
    return pl.pallas_call(
        body, name="fox_fwd", grid=(batch, N_PAIRS, nq),
        in_specs=[q_spec, k_spec, v_spec, qb, pl.BlockSpec((N_HEADS, seq), lambda b, hp, qi: (b, 0))],
        out_specs=[qb, qb],
        out_shape=[jax.ShapeDtypeStruct((batch * seq, D_ATT), BF16), jax.ShapeDtypeStruct((batch * seq, D_ATT), F32)],
        compiler_params=_cparams(("parallel", "parallel", "arbitrary")),
    )(qkv, qkv, qkv, c_wide, c_row)


def _fox_bwd(qkv, c_wide, c_row, o, do, lse_wide, batch, seq, behind):
    tq, tk = FOX_TILES
    nq = seq // tq

    def body(q_ref, k_ref, v_ref, cw_ref, cr_ref, o_ref, do_ref, lse_ref,
             dq_ref, dk_ref, dv_ref, dcs_ref, drs_ref, dkc_acc, dv_acc):
        hp, qi = pl.program_id(1), pl.program_id(2)

        @pl.when(qi == 0)
        def _():
            dkc_acc[...] = jnp.zeros_like(dkc_acc)
            dv_acc[...] = jnp.zeros_like(dv_acc)

        masks = _head_masks()
        ahead, first = _stacked_ids(tq, tk)
        q_t, do_t = q_ref[...], do_ref[...]
        q2 = _stack_heads(q_t, masks) * SCALE
        do2 = _stack_heads(do_t, masks)
        q_and_ones = jnp.concatenate([q2, _stack_heads(jnp.ones_like(q_t), masks)], axis=1)
        ct = _stack_cols(cw_ref[...])
        lse = _stack_cols(lse_ref[...])
        prod = do_t.astype(F32) * o_ref[...].astype(F32)
        delta = jnp.concatenate([jnp.sum(jnp.where(mk, prod, 0.0), axis=1, keepdims=True) for mk in masks], axis=0)

        def step(kb, carry, lead):
            dq_acc, rs = carry
            k0 = pl.multiple_of(kb * tk, tk)
            kblk = k_ref[pl.ds(k0, tk), :]
            cs = jnp.where(first, cr_ref[pl.ds(2 * hp, 1), pl.ds(k0, tk)], cr_ref[pl.ds(2 * hp + 1, 1), pl.ds(k0, tk)])
            p = jnp.exp(_dot(q2, kblk, NT_DIMS) + ct - cs - lse)
            if lead is not None:
                p = jnp.where(ahead <= lead, p, 0.0)
            dp = _dot(do2, v_ref[pl.ds(k0, tk), :], NT_DIMS)
            ds = (p * (dp - delta)).astype(BF16)
            dkc_acc[pl.ds(k0, tk), :] += _dot(ds, q_and_ones, TN_DIMS)
            dv_acc[pl.ds(k0, tk), :] += _dot(p.astype(BF16), do2, TN_DIMS)
            return dq_acc + _dot(ds, kblk), rs + jnp.sum(ds.astype(F32), axis=1, keepdims=True)

        init = (jnp.zeros((2 * tq, LANES), F32), jnp.zeros((2 * tq, 1), F32))
        dq_acc, rs = _sweep(qi, tq, tk, step, init, leftward=False)
        dq_ref[...] = (_unstack(dq_acc, masks) * SCALE).astype(BF16)
        drs_ref[...] = _unstack(rs, masks)

        @pl.when(qi == nq - 1)
        def _():
            dk_ref[...] = dkc_acc[:, 0:LANES].astype(BF16)
            dcs_ref[...] = dkc_acc[:, LANES:2 * LANES]
            dv_ref[...] = dv_acc[...].astype(BF16)

    q_spec, k_spec, v_spec = _att_specs(seq, 0, tq)
    qb = _qblock_spec(seq, tq)
    return _call_behind(
        body, behind, name="fox_bwd", grid=(batch, N_PAIRS, nq),
        in_specs=[q_spec, k_spec, v_spec, qb, pl.BlockSpec((N_HEADS, seq), lambda b, hp, qi: (b, 0)), qb, qb, qb],
        out_specs=[qb, _kv_out_spec(seq), _kv_out_spec(seq), _kv_out_spec(seq), qb],
        out_shape=[jax.ShapeDtypeStruct((batch * seq, D_ATT), BF16)] * 3 + [jax.ShapeDtypeStruct((batch * seq, D_ATT), F32)] * 2,
        scratch_shapes=[pltpu.VMEM((seq, 2 * LANES), F32), pltpu.VMEM((seq, LANES), F32)],
        operands=(qkv, qkv, qkv, c_wide, c_row, o, do, lse_wide))


def _sb_logits(q2, kblk):
    z = _dot(q2, kblk, NT_DIMS)
    lsn = jnp.minimum(-z, 0.0) - jnp.log(1.0 + jnp.exp(-jnp.abs(z)))
    return lsn + z, lsn


def _sb_fwd(qkv, batch, seq, behind):
    tq, tk = SB_TILES
    nq = seq // tq

    def body(q_ref, k_ref, v_ref, o_ref, rt_ref):
        qi = pl.program_id(2)
        masks = _head_masks()
        ahead, _ = _stacked_ids(tq, tk)
        later = _tri2(tk, lambda r, c: r > c)
        q2 = _stack_heads(q_ref[...], masks) * SCALE

        def step(kb, carry, lead):
            run, acc = carry
            k0 = pl.multiple_of(kb * tk, tk)
            ls, lsn = _sb_logits(q2, k_ref[pl.ds(k0, tk), :])
            if lead is not None:
                lsn = jnp.where(ahead < lead, lsn, 0.0)
            w = jnp.exp(ls + _dot(_split2(lsn), later) + run)
            if lead is not None:
                w = jnp.where(ahead < lead, w, 0.0)
            return run + jnp.sum(lsn, axis=1, keepdims=True), acc + _dot(w.astype(BF16), v_ref[pl.ds(k0, tk), :])

        init = (jnp.zeros((2 * tq, 1), F32), jnp.zeros((2 * tq, LANES), F32))
        run, acc = _sweep(qi, tq, tk, step, init, leftward=True)
        o_ref[...] = _unstack(acc, masks).astype(BF16)
        rt_ref[...] = _unstack(run, masks)

    q_spec, k_spec, v_spec = _att_specs(seq, 3 * N_PAIRS, tq)
    qb = _qblock_spec(seq, tq)
    return _call_behind(
        body, behind, name="sb_fwd", grid=(batch, N_PAIRS, nq), in_specs=[q_spec, k_spec, v_spec], out_specs=[qb, qb],
        out_shape=[jax.ShapeDtypeStruct((batch * seq, D_ATT), BF16), jax.ShapeDtypeStruct((batch * seq, D_ATT), F32)],
        scratch_shapes=[], operands=(qkv, qkv, qkv))


def _sb_bwd(qkv, do, rt_wide, batch, seq, behind):
    tq, tk = SB_TILES
    nq = seq // tq

    def body(q_ref, k_ref, v_ref, do_ref, rt_ref, dq_ref, dk_ref, dv_ref, dk_acc, dv_acc):
        qi = pl.program_id(2)

        @pl.when(qi == 0)
        def _():
            dk_acc[...] = jnp.zeros_like(dk_acc)
            dv_acc[...] = jnp.zeros_like(dv_acc)

        masks = _head_masks()
        ahead, _ = _stacked_ids(tq, tk)
        later = _tri2(tk, lambda r, c: r > c)
        earlier = _tri(tk, lambda r, c: r < c)
        q2 = _stack_heads(q_ref[...], masks) * SCALE
        do2 = _stack_heads(do_ref[...], masks)
        total = _stack_cols(rt_ref[...])

        def step(kb, carry, lead):
            pref, epre, dq_acc = carry
            k0 = pl.multiple_of(kb * tk, tk)
            kblk = k_ref[pl.ds(k0, tk), :]
            ls, lsn_all = _sb_logits(q2, kblk)
            lsn = lsn_all if lead is None else jnp.where(ahead < lead, lsn_all, 0.0)
            rs = jnp.sum(lsn, axis=1, keepdims=True)
            w = jnp.exp(ls + _dot(_split2(lsn), later) + (total - pref - rs))
            if lead is not None:
                w = jnp.where(ahead < lead, w, 0.0)
            e = w * _dot(do2, v_ref[pl.ds(k0, tk), :], NT_DIMS)
            before = _dot(e.astype(BF16), earlier) + epre
            dz = e * jnp.exp(lsn_all) - jnp.exp(ls) * before
            if lead is not None:
                dz = jnp.where(ahead < lead, dz, 0.0)
            dz = dz.astype(BF16)
            dk_acc[pl.ds(k0, tk), :] += _dot(dz, q2, TN_DIMS)
            dv_acc[pl.ds(k0, tk), :] += _dot(w.astype(BF16), do2, TN_DIMS)
            return pref + rs, epre + jnp.sum(e, axis=1, keepdims=True), dq_acc + _dot(dz, kblk)

        init = (jnp.zeros((2 * tq, 1), F32), jnp.zeros((2 * tq, 1), F32), jnp.zeros((2 * tq, LANES), F32))
        dq_acc = _sweep(qi, tq, tk, step, init, leftward=False)[2]
        dq_ref[...] = (_unstack(dq_acc, masks) * SCALE).astype(BF16)

        @pl.when(qi == nq - 1)
        def _():
            dk_ref[...] = dk_acc[...].astype(BF16)
            dv_ref[...] = dv_acc[...].astype(BF16)

    q_spec, k_spec, v_spec = _att_specs(seq, 3 * N_PAIRS, tq)
    qb = _qblock_spec(seq, tq)
    return _call_behind(
        body, behind, name="sb_bwd", grid=(batch, N_PAIRS, nq), in_specs=[q_spec, k_spec, v_spec, qb, qb],
        out_specs=[qb, _kv_out_spec(seq), _kv_out_spec(seq)], out_shape=[jax.ShapeDtypeStruct((batch * seq, D_ATT), BF16)] * 3,
        scratch_shapes=[pltpu.VMEM((seq, LANES), F32), pltpu.VMEM((seq, LANES), F32)], operands=(qkv, qkv, qkv, do, rt_wide))


def _local_step(x, p, target, w, rest, vec, place):
    batch, seq, _ = x.shape
    t = batch * seq
    x = x.reshape(t, D_MODEL)
    target = target.reshape(t, D_MODEL)
    p = p.reshape(t, D_PLE)
    big = dict(tm=1024, tn=1024, tk=1024)

    h1 = _norm_fwd(x, vec["g_mix"], "norm_mix")
    qkv = _mm(h1, w["qkv"], mode="nn", name="proj_qkv", out_dtype=BF16, **big)
    gl = _mm(h1, w["gate"], mode="nn", name="proj_gate", **big)
    fl = _mm(h1, w["forget"], mode="nn", name="proj_forget", **big)
    c_wide, c_row = _fox_prep(fl, vec["b_forget"], batch, seq)
    o_fox, lse_wide = _fox_fwd(qkv, c_wide, c_row, batch, seq)
    (o_sb, rt_wide), gathered = _sb_fwd(qkv, batch, seq, rest)
    w = dict(w, **_rest_weights(dict(zip(EARLY + ("b_gate",), gathered))))
    merged, of, os_ = _gate_fwd(gl, w["b_gate"], o_fox, o_sb, w["branch_fox"], w["branch_sb"])
    x1, h2 = _mm_res_norm(merged, w["out"], x, vec["g_mlp"], "proj_out_norm")
    ar = _mm(h2, w["up"], mode="nn", name="mlp_up", out_dtype=BF16, epi=lambda acc, _: jnp.maximum(acc, 0.0),
             col_shards=True, **big)
    x2, h3 = _mm_res_norm(ar, w["down"], x1, vec["g_ple"], "mlp_down_norm", a_fn=_relu2)

    dx3, dpre, dpe, dg_final, loss = _head_and_loss(x2, h3, p, w["ple_gate"], w["ple"], vec["g_final"], target)
    gw = {}
    gw["ple"] = _mm(p, dpe, mode="tn", name="d_w_ple", col_shards=True, **big)
    gw["ple_gate"] = _mm(h3, dpre, mode="tn", name="d_w_ple_gate", **big)
    dx2, dx2b, dg_ple = _mm_norm_bwd([(dpre, w["ple_gate"])], None, x2, vec["g_ple"], dx3, "d_h_ple_norm_bwd")
    gw["down"] = _mm(ar, dx2b, mode="tn", name="d_w_down", a_fn=_relu2, **big)
    da = _mm(dx2b, w["down"], mode="nt", name="d_act", out_dtype=BF16,
             epi=lambda acc, r: acc * (2.0 * r.astype(F32)), extra=ar, **big)
    gw["up"] = _mm(h2, da, mode="tn", name="d_w_up", col_shards=True, **big)
    dx1, dx1b, dg_mlp = _mm_norm_bwd([(da, w["up"])], None, x1, vec["g_mlp"], dx2, "d_h_mlp_norm_bwd")
    gw["out"] = _mm(merged, dx1b, mode="tn", name="d_w_out", **big)
    dof, dos, dgl, gw["b_gate"], do_fox, do_sb = _gate_bwd(gl, w["b_gate"], of, os_, dx1b, w["out"], w["branch_fox"],
                                                                  w["branch_sb"])
    gw["branch_fox"] = _mm(o_fox, dof, mode="tn", name="d_w_branch_fox", col_shards=True, **big)
    gw["branch_sb"] = _mm(o_sb, dos, mode="tn", name="d_w_branch_sb", col_shards=True, **big)
    early = _early_slots(gw)
    early = [early[n] for n in EARLY]
    (dq_a, dk_a, dv_a, dcs_wide, drs_wide), received = _fox_bwd(qkv, c_wide, c_row, o_fox, do_fox, lse_wide, batch, seq,
                                                                _swap_halves(early))
    sums = _sum_sibling(place, early, received, "sum_sibling_early")
    (dq_b, dk_b, dv_b), others = _sb_bwd(qkv, do_sb, rt_wide, batch, seq, _exchange_chips(sums))
    mine = [_sum_chips(place, s, r, o, "sum_chips_" + n) for s, r, o, n in zip(early, received, others, EARLY)]
    dfl, db_forget = _fox_post(dcs_wide, drs_wide, fl, vec["b_forget"], batch, seq)
    dqkv = jnp.concatenate([dq_a, dk_a, dv_a, dq_b, dk_b, dv_b], axis=1)
    gw["qkv"], theirs = _mm(dqkv, h1, mode="tn", name="d_w_qkv", behind=_share_halves(mine), flat_out=True, **big)
    reduced = dict(zip(EARLY, zip(mine, theirs)))
    gw["gate"] = _mm(dgl, h1, mode="tn", name="d_w_gate", flat_out=True, **big)
    gw["forget"] = _mm(dfl, h1, mode="tn", name="d_w_forget", flat_out=True, **big)
    late = [_w_in_slots(gw)]
    dh1, received = _mm(dqkv, w["qkv"], mode="nt", name="d_h_qkv", behind=_swap_halves(late), **big)
    sums = _sum_sibling(place, late, received, "sum_sibling_w_in")
    (grad_x, _, dg_mix), others = _mm_norm_bwd([(dgl, w["gate"]), (dfl, w["forget"])], dh1, x, vec["g_mix"], dx1,
                                               "d_h_gate_norm_bwd", behind=_exchange_chips(sums))
    mine = [_sum_chips(place, late[0], received[0], others[0], "sum_chips_w_in")]
    reduced["w_in"] = (mine[0], _run_exchange(_share_halves(mine), "reduce_share_w_in")[0])
    gvec = {"g_mix": dg_mix, "b_forget": db_forget[:, 0:N_HEADS], "g_mlp": dg_mlp, "g_ple": dg_ple,
            "g_final": dg_final, "b_gate": gw["b_gate"]}
    return loss, grad_x.reshape(batch, seq, D_MODEL), reduced, gvec


ANY = pl.BlockSpec(memory_space=pl.ANY)
SHARDED = ("w_in", "w_branch_fox", "w_branch_sb", "w_out", "w_up", "w_down", "w_ple_gate", "w_ple")
ROW_ALIGN = 16
F32_ROWS = 8


def _place():
    return lax.axis_index("x"), lax.axis_index("y"), lax.axis_index("c")


def _other_chips(x, y):
    return [(1 - x, y), (x, 1 - y), (1 - x, 1 - y)]


def _half(ref, h):
    r = ref.shape[0] // 2
    assert r % ROW_ALIGN == 0
    return ref.at[pl.ds(pl.multiple_of(h * r, ROW_ALIGN), r)]


def _remote(src, dst, sems, idx, to):
    send_sems, recv_sems = sems
    return pltpu.make_async_remote_copy(src_ref=src, dst_ref=dst, send_sem=send_sems.at[idx], recv_sem=recv_sems.at[idx],
                                        device_id=to, device_id_type=MESH)


class _Exchange:
    def __init__(self, operands, out_shapes, sem_shape, start, finish):
        self.operands, self.out_shapes, self.sem_shape, self.start, self.finish = operands, out_shapes, sem_shape, start, finish

    def scratch(self):
        return [pltpu.SemaphoreType.DMA(self.sem_shape), pltpu.SemaphoreType.DMA(self.sem_shape)]


def _run_exchange(ex, name):
    n = len(ex.operands)

    def body(*refs):
        ex.start(refs[:n], refs[n:2 * n], refs[2 * n:])
        ex.finish(refs[:n], refs[n:2 * n], refs[2 * n:])

    return pl.pallas_call(body, name=name, in_specs=[ANY] * n, out_specs=[ANY] * n, out_shape=ex.out_shapes,
                          scratch_shapes=ex.scratch())(*ex.operands)


def _call_behind(body, ex, *, name, grid, in_specs, out_specs, out_shape, scratch_shapes, operands):
    n_in, n_out, nx = len(in_specs), len(out_specs), len(ex.operands)

    def wrapped(*refs):
        ins, x_in = refs[:n_in], refs[n_in:n_in + nx]
        outs, x_out = refs[n_in + nx:n_in + nx + n_out], refs[n_in + nx + n_out:n_in + 2 * nx + n_out]
        scratch, sems = refs[n_in + 2 * nx + n_out:-2], refs[-2:]
        first, last = None, None
        for d, steps in enumerate(grid):
            at_start, at_end = pl.program_id(d) == 0, pl.program_id(d) == steps - 1
            first = at_start if first is None else first & at_start
            last = at_end if last is None else last & at_end

        @pl.when(first)
        def _():
            ex.start(x_in, x_out, sems)

        body(*ins, *outs, *scratch)

        @pl.when(last)
        def _():
            ex.finish(x_in, x_out, sems)

    res = pl.pallas_call(
        wrapped, name=name, grid=grid, in_specs=list(in_specs) + [ANY] * nx, out_specs=list(out_specs) + [ANY] * nx,
        out_shape=list(out_shape) + list(ex.out_shapes), scratch_shapes=list(scratch_shapes) + ex.scratch(),
        compiler_params=_cparams(("arbitrary",) * len(grid)),
    )(*operands, *ex.operands)
    return res[:n_out], res[n_out:]


def _gather_weights(shards):
    n = len(shards)

    def first_copies(src, out, sems):
        x, y, c = _place()
        me = 2 * x + y
        copies = [_remote(_half(src[t], c), _half(out[t].at[me], c), sems, (t, k), (px, py, c))
                  for t in range(n) for k, (px, py) in enumerate(_other_chips(x, y))]
        return copies + [_remote(src[t], out[t].at[me], sems, (t, 3), (x, y, 1 - c)) for t in range(n)]

    def start(src, out, sems):
        for cp in first_copies(src, out, sems):
            cp.start()

    def finish(src, out, sems):
        x, y, c = _place()
        me = 2 * x + y
        sibling = (x, y, 1 - c)
        chips = _other_chips(x, y)
        passes = []
        for t in range(n):
            for k, (px, py) in enumerate(chips):
                landed = _half(out[t].at[2 * px + py], c)
                _remote(landed, landed, sems, (t, k), (px, py, c)).wait_recv()
                passes.append(_remote(landed, landed, sems, (t, 4 + k), sibling))
                passes[-1].start()
        for t in range(n):
            _remote(src[t], out[t].at[me], sems, (t, 3), sibling).wait_recv()
            for k, (px, py) in enumerate(chips):
                passed = _half(out[t].at[2 * px + py], 1 - c)
                _remote(passed, passed, sems, (t, 4 + k), sibling).wait_recv()
        for cp in first_copies(src, out, sems) + passes:
            cp.wait_send()

    return _Exchange(shards, [jax.ShapeDtypeStruct((N_CHIPS,) + s.shape, s.dtype) for s in shards], (n, 7), start, finish)


def _simple_exchange(operands, out_shapes, copies):
    def start(src, out, sems):
        for cp in copies(src, out, sems):
            cp.start()

    def finish(src, out, sems):
        for cp in copies(src, out, sems):
            cp.wait_recv()
        for cp in copies(src, out, sems):
            cp.wait_send()

    return _Exchange(operands, out_shapes, (len(operands),), start, finish)


def _swap_halves(slots):
    def copies(src, out, sems):
        x, y, c = _place()
        res = []
        for t in range(len(slots)):
            r = src[t].shape[1] // 2
            rows = pl.ds(pl.multiple_of((1 - c) * r, F32_ROWS), r)
            res.append(_remote(src[t].at[:, rows], out[t], sems, t, (x, y, 1 - c)))
        return res

    return _simple_exchange(slots, [jax.ShapeDtypeStruct((N_CHIPS, s.shape[1] // 2, s.shape[2]), s.dtype) for s in slots], copies)


def _exchange_chips(sums):
    n = len(sums)

    def copies(src, out, sems):
        x, y, c = _place()
        return [_remote(src[t].at[2 * px + py], out[t].at[k], sems, (t, k), (px, py, c))
                for t in range(n) for k, (px, py) in enumerate(_other_chips(x, y))]

    def start(src, out, sems):
        for cp in copies(src, out, sems):
            cp.start()

    def finish(src, out, sems):
        for cp in copies(src, out, sems):
            cp.wait_recv()
        for cp in copies(src, out, sems):
            cp.wait_send()

    return _Exchange(sums, [jax.ShapeDtypeStruct((3,) + s.shape[1:], s.dtype) for s in sums], (n, 3), start, finish)


def _share_halves(mine):
    def copies(src, out, sems):
        x, y, c = _place()
        return [_remote(src[t], out[t], sems, t, (x, y, 1 - c)) for t in range(len(mine))]

    return _simple_exchange(mine, [jax.ShapeDtypeStruct(s.shape, s.dtype) for s in mine], copies)


def _half_tile(rows):
    return rows


def _sum_sibling(place, slots, received, name):
    shapes = [(s.shape[0], s.shape[1] // 2, s.shape[2]) for s in slots]
    starts = [sum(n for n, _, _ in shapes[:t]) for t in range(len(shapes))]
    in_specs, out_specs, operands = [], [], []
    for (n, rows, cols), start, slot, got in zip(shapes, starts, slots, received):
        local = lambda s, start=start, n=n: jnp.clip(s - start, 0, n - 1)
        in_specs += [pl.BlockSpec((None, rows, cols), lambda s, pr, local=local: (local(s), pr[1], 0)),
                     pl.BlockSpec((None, rows, cols), lambda s, pr, local=local: (local(s), 0, 0))]
        out_specs.append(pl.BlockSpec((None, rows, cols), lambda s, pr, local=local: (local(s), 0, 0)))
        operands += [slot, got]

    def body(place_ref, *refs):
        s = pl.program_id(0)
        for t, ((n, _, _), start) in enumerate(zip(shapes, starts)):
            @pl.when((s >= start) & (s < start + n))
            def _(t=t):
                refs[2 * len(shapes) + t][...] = (refs[2 * t][...] + refs[2 * t + 1][...]).astype(BF16)

    return pl.pallas_call(
        body, name=name, out_shape=[jax.ShapeDtypeStruct(sh, BF16) for sh in shapes],
        grid_spec=pltpu.PrefetchScalarGridSpec(num_scalar_prefetch=1, grid=(sum(n for n, _, _ in shapes),), in_specs=in_specs,
                                               out_specs=out_specs),
        compiler_params=_cparams(("arbitrary",)),
    )(place, *operands)


def _sum_chips(place, slot, received, others, name):
    _, rows2, cols = slot.shape
    rows = rows2 // 2
    tile = _half_tile(rows)
    nb = rows // tile

    def body(place_ref, a_ref, b_ref, p_ref, o_ref):
        own = a_ref[...] + b_ref[...]
        o_ref[...] = ((own + p_ref[0].astype(F32)) + p_ref[1].astype(F32)) + p_ref[2].astype(F32)

    return pl.pallas_call(
        body, name=name, out_shape=jax.ShapeDtypeStruct((rows, cols), F32),
        grid_spec=pltpu.PrefetchScalarGridSpec(
            num_scalar_prefetch=1, grid=(nb,),
            in_specs=[pl.BlockSpec((None, tile, cols), lambda i, pr: (pr[0], pr[1] * nb + i, 0)),
                      pl.BlockSpec((None, tile, cols), lambda i, pr: (pr[0], i, 0)),
                      pl.BlockSpec((3, tile, cols), lambda i, pr: (0, i, 0))],
            out_specs=pl.BlockSpec((tile, cols), lambda i, pr: (i, 0))),
        compiler_params=_cparams(("parallel",)),
    )(place, slot, received, others)


N_DEVICES = 8


def _sum_devices(block, name):
    def body(v_ref, o_ref, land_ref, send_sems, recv_sems):
        x, y, c = _place()
        me = 4 * x + 2 * y + c
        copies = []
        for mask in range(1, N_DEVICES):
            peer = (x ^ (mask >> 2), y ^ ((mask >> 1) & 1), c ^ (mask & 1))
            copies.append(pltpu.make_async_remote_copy(src_ref=v_ref, dst_ref=land_ref.at[me], send_sem=send_sems.at[mask - 1],
                                                       recv_sem=recv_sems.at[mask - 1], device_id=peer, device_id_type=MESH))
        for cp in copies:
            cp.start()
        land_ref[me] = v_ref[...]
        for cp in copies:
            cp.wait_recv()
        total = land_ref[0]
        for d in range(1, N_DEVICES):
            total = total + land_ref[d]
        o_ref[...] = total
        for cp in copies:
            cp.wait_send()

    vmem = pl.BlockSpec(memory_space=pltpu.VMEM)
    return pl.pallas_call(
        body, name=name, in_specs=[vmem], out_specs=vmem, out_shape=jax.ShapeDtypeStruct(block.shape, F32),
        scratch_shapes=[pltpu.VMEM((N_DEVICES,) + block.shape, F32), pltpu.SemaphoreType.DMA((N_DEVICES - 1,)),
                        pltpu.SemaphoreType.DMA((N_DEVICES - 1,))],
    )(block)


def _vec_block(g_mix, g_mlp, g_ple, g_final, b_forget, b_gate_rows, last=None):
    pad = lambda a: jnp.concatenate([a, jnp.zeros((a.shape[0], D_MODEL - a.shape[1]), F32)], axis=1)
    last = jnp.zeros((1, 0), F32) if last is None else last
    return jnp.concatenate([g_mix, g_mlp, g_ple, g_final.reshape(1, D_MODEL), pad(b_forget), pad(b_gate_rows), pad(last)],
                           axis=0)


def _adam_math(w, g, m, v):
    m_new = ADAM_B1 * m + (1.0 - ADAM_B1) * g
    v_new = ADAM_B2 * v + (1.0 - ADAM_B2) * (g * g)
    m_hat = m_new / (1.0 - ADAM_B1 ** ADAM_STEP)
    v_hat = v_new / (1.0 - ADAM_B2 ** ADAM_STEP)
    return -ADAM_LR * (m_hat / (jnp.sqrt(v_hat) + ADAM_EPS) + ADAM_WD * w), m_new, v_new


def _adamw_halves(place, w, m, v, g_mine, g_theirs, name):
    rows2, cols = w.shape
    rows = rows2 // 2
    tile = _half_tile(rows)
    nb = rows // tile

    def body(place_ref, w_ref, m_ref, v_ref, gm_ref, gt_ref, g_ref, d_ref, nm_ref, nv_ref):
        g = jnp.where(pl.program_id(0) == 0, gm_ref[...], gt_ref[...])
        g_ref[...] = g
        d_ref[...], nm_ref[...], nv_ref[...] = _adam_math(w_ref[...], g, m_ref[...], v_ref[...])

    whole = pl.BlockSpec((tile, cols), lambda s, i, pr: ((pr[1] + s - 2 * pr[1] * s) * nb + i, 0))
    half = pl.BlockSpec((tile, cols), lambda s, i, pr: (i, 0))
    return pl.pallas_call(
        body, name=name, out_shape=[jax.ShapeDtypeStruct((rows2, cols), F32)] * 4,
        grid_spec=pltpu.PrefetchScalarGridSpec(num_scalar_prefetch=1, grid=(2, nb), in_specs=[whole] * 3 + [half] * 2,
                                               out_specs=[whole] * 4),
        compiler_params=_cparams(("parallel", "parallel")),
    )(place, w, m, v, g_mine, g_theirs)


def _adamw_vec(w, g, m, v):
    def body(w_ref, g_ref, m_ref, v_ref, d_ref, nm_ref, nv_ref):
        d_ref[...], nm_ref[...], nv_ref[...] = _adam_math(w_ref[...], g_ref[...], m_ref[...], v_ref[...])

    return pl.pallas_call(body, name="adamw_vectors", out_shape=[jax.ShapeDtypeStruct(w.shape, F32)] * 3)(w, g, m, v)


WEIGHT_NAMES = ("g_mix", "w_in", "b_forget", "b_gate", "w_branch_fox", "w_branch_sb", "w_out", "g_mlp", "w_up", "w_down",
                "g_ple", "w_ple_gate", "w_ple", "g_final")
W_IN_SHARD = D_IN // N_CHIPS
Q_END, F_END, B_END = 3 * D_ATT, 3 * D_ATT + N_HEADS, 6 * D_ATT + N_HEADS
GATE_SHARD = D_MODEL // N_CHIPS


LATE = SHARDED[:1]
EARLY = SHARDED[1:]


def _first_weights(w_in_slots):
    def cols(*ranges):
        parts = []
        for lo, hi in ranges:
            for j in range(N_CHIPS):
                a, b = max(lo, j * W_IN_SHARD), min(hi, (j + 1) * W_IN_SHARD)
                if a < b:
                    parts.append(w_in_slots[j, :, a - j * W_IN_SHARD:b - j * W_IN_SHARD])
        return parts

    forget = jnp.concatenate(cols((Q_END, F_END)) + [jnp.zeros((D_MODEL, F_PAD - N_HEADS), BF16)], axis=1)
    return {"qkv": jnp.concatenate(cols((0, Q_END), (F_END, B_END)), axis=1), "gate": jnp.concatenate(cols((B_END, D_IN)), axis=1),
            "forget": forget}


GATE_ROWS = 2 * ROW_ALIGN


def _gate_bits(b_gate):
    bits = lax.bitcast_convert_type(b_gate, BF16).reshape(2, 2 * GATE_SHARD)
    return jnp.concatenate([bits, jnp.zeros((GATE_ROWS - 2, 2 * GATE_SHARD), BF16)], axis=0)


def _rest_weights(gathered):
    rows = lambda a: a.reshape(N_CHIPS * a.shape[1], a.shape[2])
    bits = gathered["b_gate"][:, :2].reshape(N_CHIPS, 2, GATE_SHARD, 2)
    b_gate = jnp.transpose(lax.bitcast_convert_type(bits, F32), (1, 0, 2)).reshape(2, D_MODEL)
    return {"branch_fox": gathered["w_branch_fox"], "branch_sb": gathered["w_branch_sb"], "out": rows(gathered["w_out"]),
            "up": gathered["w_up"], "down": rows(gathered["w_down"]), "ple_gate": rows(gathered["w_ple_gate"]),
            "ple": gathered["w_ple"], "b_gate": b_gate}


def _early_slots(gw):
    rows = lambda a: a.reshape(N_CHIPS, a.shape[0] // N_CHIPS, a.shape[1])
    return {"w_branch_fox": gw["branch_fox"], "w_branch_sb": gw["branch_sb"], "w_out": rows(gw["out"]), "w_up": gw["up"],
            "w_down": rows(gw["down"]), "w_ple_gate": rows(gw["ple_gate"]), "w_ple": gw["ple"]}


W_IN_FLAT = (W_IN_SHARD * D_MODEL // LANES, LANES)


def _w_in_slots(gw):
    c = D_MODEL // LANES
    g_t = jnp.concatenate([gw["qkv"][:Q_END * c], gw["forget"][:N_HEADS * c], gw["qkv"][Q_END * c:], gw["gate"]], axis=0)
    return g_t.reshape((N_CHIPS,) + W_IN_FLAT)


def _flat(a):
    return jnp.transpose(a, (2, 0, 1)).reshape(W_IN_FLAT)


def _unflat(a):
    return jnp.transpose(a.reshape(W_IN_SHARD, D_MODEL // LANES, LANES), (1, 2, 0)).reshape(1, D_MODEL, W_IN_SHARD)


def kernel(x, p, g_mix, w_in, b_forget, b_gate, w_branch_fox, w_branch_sb, w_out, g_mlp, w_up, w_down, g_ple, w_ple_gate, w_ple, g_final, loss_target, m_g_mix, m_w_in, m_b_forget, m_b_gate, m_w_branch_fox, m_w_branch_sb, m_w_out, m_g_mlp, m_w_up, m_w_down, m_g_ple, m_w_ple_gate, m_w_ple, m_g_final, v_g_mix, v_w_in, v_b_forget, v_b_gate, v_w_branch_fox, v_w_branch_sb, v_w_out, v_g_mlp, v_w_up, v_w_down, v_g_ple, v_w_ple_gate, v_w_ple, v_g_final):
    weights = dict(g_mix=g_mix, w_in=w_in, b_forget=b_forget, b_gate=b_gate, w_branch_fox=w_branch_fox,
                   w_branch_sb=w_branch_sb, w_out=w_out, g_mlp=g_mlp, w_up=w_up, w_down=w_down, g_ple=g_ple,
                   w_ple_gate=w_ple_gate, w_ple=w_ple, g_final=g_final)
    first = dict(g_mix=m_g_mix, w_in=m_w_in, b_forget=m_b_forget, b_gate=m_b_gate, w_branch_fox=m_w_branch_fox,
                 w_branch_sb=m_w_branch_sb, w_out=m_w_out, g_mlp=m_g_mlp, w_up=m_w_up, w_down=m_w_down, g_ple=m_g_ple,
                 w_ple_gate=m_w_ple_gate, w_ple=m_w_ple, g_final=m_g_final)
    second = dict(g_mix=v_g_mix, w_in=v_w_in, b_forget=v_b_forget, b_gate=v_b_gate, w_branch_fox=v_w_branch_fox,
                  w_branch_sb=v_w_branch_sb, w_out=v_w_out, g_mlp=v_g_mlp, w_up=v_w_up, w_down=v_w_down, g_ple=v_g_ple,
                  w_ple_gate=v_w_ple_gate, w_ple=v_w_ple, g_final=v_g_final)
    cx, cy, cc = _place()
    chip = 2 * cx + cy
    place = jnp.stack([chip, cc]).astype(jnp.int32)
    col0 = chip * GATE_SHARD

    (w_in_slots,) = _run_exchange(_gather_weights([weights[n][0].astype(BF16) for n in LATE]), "gather_w_in")
    rest = _gather_weights([weights[n][0].astype(BF16) for n in EARLY] + [_gate_bits(b_gate[0])])
    vec = {"g_mix": g_mix, "b_forget": jnp.concatenate([b_forget, jnp.zeros((1, F_PAD - N_HEADS), F32)], axis=1),
           "g_mlp": g_mlp, "g_ple": g_ple, "g_final": g_final.reshape(1, D_MODEL)}

    loss, grad_x, reduced, gvec = _local_step(x, p[0], loss_target, _first_weights(w_in_slots), rest, vec, place)

    out = {}
    for n in EARLY:
        g_mine, g_theirs = reduced[n]
        res = _adamw_halves(place, weights[n][0], first[n][0], second[n][0], g_mine, g_theirs, "adamw_" + n)
        out[n] = [r[None] for r in res]
    g_mine, g_theirs = reduced["w_in"]
    out["w_in"] = [_unflat(r) for r in _adamw_halves(place, _flat(w_in), _flat(m_w_in), _flat(v_w_in), g_mine, g_theirs,
                                                      "adamw_w_in")]

    g_block = _sum_devices(_vec_block(gvec["g_mix"], gvec["g_mlp"], gvec["g_ple"], gvec["g_final"][0], gvec["b_forget"],
                                      gvec["b_gate"], loss), "reduce_vectors")
    loss = g_block[7, 0]
    g_gate = lax.dynamic_slice(g_block[5:7], (0, col0), (2, GATE_SHARD))
    blocks = [_vec_block(d["g_mix"], d["g_mlp"], d["g_ple"], d["g_final"], d["b_forget"], d["b_gate"][0])
              for d in (weights, first, second)]
    g_rows = jnp.concatenate([g_block[0:5], jnp.concatenate([g_gate, jnp.zeros((2, D_MODEL - GATE_SHARD), F32)], axis=1),
                              jnp.zeros((1, D_MODEL), F32)], axis=0)
    res = (g_rows,) + tuple(_adamw_vec(blocks[0], g_rows, blocks[1], blocks[2]))
    out["g_mix"] = [r[0:1] for r in res]
    out["g_mlp"] = [r[1:2] for r in res]
    out["g_ple"] = [r[2:3] for r in res]
    out["g_final"] = [r[3] for r in res]
    out["b_forget"] = [r[4:5, :N_HEADS] for r in res]
    out["b_gate"] = [r[5:7, :GATE_SHARD][None] for r in res]
    return (loss, grad_x, *[out[n][0] for n in WEIGHT_NAMES], *[out[n][1] for n in WEIGHT_NAMES],
            *[out[n][2] for n in WEIGHT_NAMES], *[out[n][3] for n in WEIGHT_NAMES])
```

```python
import jax
import jax.numpy as jnp
from jax import lax
from jax.experimental import pallas as pl
from jax.experimental.pallas import tpu as pltpu

F32 = jnp.float32
BF16 = jnp.bfloat16

D_MODEL = 1024
HEAD_DIM = 64
N_HEADS = 8
D_ATT = N_HEADS * HEAD_DIM
D_PLE = 256
D_IN = 6 * D_ATT + N_HEADS + 2 * D_MODEL
F_PAD = 128
EPS = 1e-6
SCALE = HEAD_DIM ** -0.5
N_CHIPS = 4
LANES = 128
ATT_BLOCK = 256
FOX_TILES = (512, 512)
SB_TILES = (512, 256)
NEG = -1e30

ADAM_LR = 0.001
ADAM_B1 = 0.9
ADAM_B2 = 0.999
ADAM_EPS = 1e-08
ADAM_WD = 0.01
ADAM_STEP = 10

VMEM_LIMIT = 56 * 1024 * 1024

MESH = pl.DeviceIdType.MESH


def _cparams(sem=None):
    return pltpu.CompilerParams(dimension_semantics=sem, vmem_limit_bytes=VMEM_LIMIT)


def _relu2(t):
    t = t.astype(F32)
    return t * t


_DIMS = {"nn": (((1,), (0,)), ((), ())), "nt": (((1,), (1,)), ((), ())), "tn": (((0,), (0,)), ((), ()))}
NT_DIMS = _DIMS["nt"]
TN_DIMS = _DIMS["tn"]


def _mm(a, b, *, mode, name, out_dtype=F32, tm=512, tn=512, tk=512, add=None, a_fn=None, epi=None, extra=None,
        col_shards=False, behind=None, flat_out=False):
    if mode == "nn":
        (m, k), n = a.shape, b.shape[-1]
    elif mode == "nt":
        (m, k), n = a.shape, b.shape[-2]
    else:
        (k, m), n = a.shape, b.shape[1]
    shard = None
    if col_shards:
        if mode == "nn":
            shard, n = n, N_CHIPS * n
            tn = min(tn, shard)
        elif mode == "nt":
            shard = b.shape[-1]
            tk = min(tk, shard)
        else:
            shard = n // N_CHIPS
            tn = min(tn, shard)
    tm, tn, tk = min(tm, m), min(tn, n), min(tk, k)
    assert m % tm == 0 and n % tn == 0 and k % tk == 0, (name, m, n, k)
    nk = k // tk
    a_spec = {"nn": pl.BlockSpec((tm, tk), lambda i, j, kk: (i, kk)),
              "nt": pl.BlockSpec((tm, tk), lambda i, j, kk: (i, kk)),
              "tn": pl.BlockSpec((tk, tm), lambda i, j, kk: (kk, i))}[mode]
    b_spec = {"nn": pl.BlockSpec((tk, tn), lambda i, j, kk: (kk, j)),
              "nt": pl.BlockSpec((tn, tk), lambda i, j, kk: (j, kk)),
              "tn": pl.BlockSpec((tk, tn), lambda i, j, kk: (kk, j))}[mode]
    o_spec = pl.BlockSpec((tm, tn), lambda i, j, kk: (i, j))
    out_shape = (m, n)
    if col_shards and mode == "nn":
        per = shard // tn
        b_spec = pl.BlockSpec((None, tk, tn), lambda i, j, kk: (j // per, kk, j % per))
    elif col_shards and mode == "nt":
        per = shard // tk
        b_spec = pl.BlockSpec((None, tn, tk), lambda i, j, kk: (kk // per, j, kk % per))
    elif col_shards:
        assert add is None and extra is None
        per = shard // tn
        o_spec = pl.BlockSpec((None, tm, tn), lambda i, j, kk: (j // per, i, j % per))
        out_shape = (N_CHIPS, m, shard)
    if flat_out:
        assert mode == "tn" and tn == n == D_MODEL and not col_shards and add is None and extra is None
        chunks = D_MODEL // LANES
        o_spec = pl.BlockSpec((tm * chunks, LANES), lambda i, j, kk: (i, 0))
        out_shape = (m * chunks, LANES)
    operands, in_specs = [a, b], [a_spec, b_spec]
    third = add if add is not None else extra
    if third is not None:
        operands.append(third)
        in_specs.append(o_spec)

    def body(*refs):
        a_ref, b_ref = refs[0], refs[1]
        t_ref = refs[2] if third is not None else None
        o_ref = refs[3] if third is not None else refs[2]
        acc_ref = refs[-1] if nk > 1 else None
        at = a_ref[...]
        if a_fn is not None:
            at = a_fn(at)
        part = lax.dot_general(at.astype(BF16), b_ref[...].astype(BF16), _DIMS[mode], preferred_element_type=F32)

        def finish(acc):
            if epi is not None:
                acc = epi(acc, None if t_ref is None else t_ref[...])
            elif add is not None:
                acc = acc + t_ref[...].astype(F32)
            if flat_out:
                for q in range(D_MODEL // LANES):
                    o_ref[pl.ds(q, tm, stride=D_MODEL // LANES), :] = acc[:, q * LANES:(q + 1) * LANES].astype(o_ref.dtype)
                return
            o_ref[...] = acc.astype(o_ref.dtype)

        if nk == 1:
            finish(part)
        else:
            kk = pl.program_id(2)

            @pl.when(kk == 0)
            def _():
                acc_ref[...] = part

            @pl.when(kk > 0)
            def _():
                acc_ref[...] += part

            @pl.when(kk == nk - 1)
            def _():
                finish(acc_ref[...])

    call = dict(name=name, grid=(m // tm, n // tn, nk), in_specs=in_specs,
                scratch_shapes=[pltpu.VMEM((tm, tn), F32)] if nk > 1 else [])
    if behind is not None:
        (res,), exchanged = _call_behind(body, behind, out_specs=[o_spec], out_shape=[jax.ShapeDtypeStruct(out_shape, out_dtype)],
                                         operands=operands, **call)
        return res, exchanged
    return pl.pallas_call(body, out_specs=o_spec, out_shape=jax.ShapeDtypeStruct(out_shape, out_dtype),
                          compiler_params=_cparams(("parallel", "parallel", "arbitrary")), **call)(*operands)


ROW_TILE = 512


def _row_spec(width=D_MODEL, rows=ROW_TILE):
    return pl.BlockSpec((rows, width), lambda i: (i, 0))


def _vec_spec(rows=1, width=D_MODEL):
    return pl.BlockSpec((rows, width), lambda i: (0, 0))


def _xhat(x):
    r = lax.rsqrt(jnp.mean(x * x, axis=-1, keepdims=True) + EPS)
    return x * r, r


def _rms_bwd_rows(dh, x, g):
    xh, r = _xhat(x)
    dxh = dh * g
    dx = r * (dxh - xh * jnp.mean(dxh * xh, axis=-1, keepdims=True))
    return dx, jnp.sum(dh * xh, axis=0, keepdims=True)


def _norm_fwd(x, g, name):
    t = x.shape[0]

    def body(x_ref, g_ref, h_ref):
        xh, _ = _xhat(x_ref[...])
        h_ref[...] = (xh * g_ref[...]).astype(BF16)

    return pl.pallas_call(
        body, name=name, grid=(t // ROW_TILE,), in_specs=[_row_spec(), _vec_spec()], out_specs=_row_spec(),
        out_shape=jax.ShapeDtypeStruct((t, D_MODEL), BF16), compiler_params=_cparams(("parallel",)),
    )(x, g)


def _mm_res_norm(a, b, res, g, name, a_fn=None):
    t, k = a.shape

    def body(a_ref, b_ref, res_ref, g_ref, x_ref, h_ref):
        at = a_ref[...] if a_fn is None else a_fn(a_ref[...])
        x_new = res_ref[...] + _dot(at.astype(BF16), b_ref[...])
        x_ref[...] = x_new
        h_ref[...] = (_xhat(x_new)[0] * g_ref[...]).astype(BF16)

    return pl.pallas_call(
        body, name=name, grid=(t // ROW_TILE,),
        in_specs=[pl.BlockSpec((ROW_TILE, k), lambda i: (i, 0)), pl.BlockSpec(b.shape, lambda i: (0, 0)), _row_spec(), _vec_spec()],
        out_specs=[_row_spec(), _row_spec()],
        out_shape=[jax.ShapeDtypeStruct((t, D_MODEL), F32), jax.ShapeDtypeStruct((t, D_MODEL), BF16)],
        compiler_params=_cparams(("parallel",)),
    )(a, b, res, g)


def _mm_norm_bwd(pairs, dh_first, x, g, dres, name, behind=None):
    t = x.shape[0]
    operands, in_specs = [], []
    for a, b in pairs:
        if b.ndim == 3:
            for j in range(b.shape[0]):
                operands += [a, b]
                in_specs += [pl.BlockSpec((ROW_TILE, b.shape[2]), lambda i, j=j: (i, j)),
                             pl.BlockSpec((None, D_MODEL, b.shape[2]), lambda i, j=j: (j, 0, 0))]
        else:
            operands += [a, b]
            in_specs += [pl.BlockSpec((ROW_TILE, a.shape[1]), lambda i: (i, 0)), pl.BlockSpec(b.shape, lambda i: (0, 0))]
    n_mm = len(operands)
    operands += [x, g, dres] + ([] if dh_first is None else [dh_first])
    in_specs += [_row_spec(), _vec_spec(), _row_spec()] + ([] if dh_first is None else [_row_spec()])

    def body(*refs):
        x_ref, g_ref, dres_ref = refs[n_mm:n_mm + 3]
        dx_ref, dxb_ref, dg_ref = refs[-3:]
        dh = 0.0 if dh_first is None else refs[n_mm + 3][...]
        for k in range(0, n_mm, 2):
            dh = dh + lax.dot_general(refs[k][...].astype(BF16), refs[k + 1][...].astype(BF16), NT_DIMS,
                                      preferred_element_type=F32)
        dx, dg = _rms_bwd_rows(dh, x_ref[...], g_ref[...])
        dx = dx + dres_ref[...]
        dx_ref[...] = dx
        dxb_ref[...] = dx.astype(BF16)

        @pl.when(pl.program_id(0) == 0)
        def _():
            dg_ref[...] = jnp.zeros_like(dg_ref)

        dg_ref[...] += dg

    call = dict(name=name, grid=(t // ROW_TILE,), in_specs=in_specs, out_specs=[_row_spec(), _row_spec(), _vec_spec()],
                out_shape=[jax.ShapeDtypeStruct((t, D_MODEL), F32), jax.ShapeDtypeStruct((t, D_MODEL), BF16),
                           jax.ShapeDtypeStruct((1, D_MODEL), F32)])
    if behind is not None:
        return _call_behind(body, behind, scratch_shapes=[], operands=operands, **call)
    return pl.pallas_call(body, compiler_params=_cparams(("arbitrary",)), **call)(*operands)


def _shards_spec(w):
    return pl.BlockSpec(w.shape, lambda i: (0, 0, 0))


def _gate_fwd(gl, b_gate, o_fox, o_sb, w_fox, w_sb):
    t = o_fox.shape[0]

    def body(gla_ref, glb_ref, b_ref, ofox_ref, osb_ref, wf_ref, ws_ref, m_ref, of_ref, os_ref):
        of = jnp.concatenate([_dot(ofox_ref[...], wf_ref[j]) for j in range(N_CHIPS)], axis=1)
        os_ = jnp.concatenate([_dot(osb_ref[...], ws_ref[j]) for j in range(N_CHIPS)], axis=1)
        ga = jax.nn.sigmoid(gla_ref[...] + b_ref[0:1, :])
        gb = jax.nn.sigmoid(glb_ref[...] + b_ref[1:2, :])
        of_ref[...] = of
        os_ref[...] = os_
        m_ref[...] = (ga * of + gb * os_).astype(BF16)

    return pl.pallas_call(
        body, name="gate_fwd", grid=(t // ROW_TILE,),
        in_specs=[pl.BlockSpec((ROW_TILE, D_MODEL), lambda i: (i, 0)), pl.BlockSpec((ROW_TILE, D_MODEL), lambda i: (i, 1)),
                  _vec_spec(2), _row_spec(D_ATT), _row_spec(D_ATT), _shards_spec(w_fox), _shards_spec(w_sb)],
        out_specs=[_row_spec(), _row_spec(), _row_spec()],
        out_shape=[jax.ShapeDtypeStruct((t, D_MODEL), BF16)] + [jax.ShapeDtypeStruct((t, D_MODEL), F32)] * 2,
        compiler_params=_cparams(("parallel",)),
    )(gl, gl, b_gate, o_fox, o_sb, w_fox, w_sb)


def _gate_bwd(gl, b_gate, of, os_, dx, w_out, w_fox, w_sb):
    t = of.shape[0]
    shard = D_MODEL // N_CHIPS

    def back(d, w_ref):
        return sum(_dot(d[:, j * shard:(j + 1) * shard], w_ref[j], NT_DIMS) for j in range(N_CHIPS)).astype(BF16)

    def body(gla_ref, glb_ref, b_ref, of_ref, os_ref, dx_ref, w_ref, wf_ref, ws_ref,
             dof_ref, dos_ref, dgl_ref, db_ref, dofox_ref, dosb_ref):
        dm = _dot(dx_ref[...], w_ref[...], NT_DIMS)
        ga = jax.nn.sigmoid(gla_ref[...] + b_ref[0:1, :])
        gb = jax.nn.sigmoid(glb_ref[...] + b_ref[1:2, :])
        dof = (dm * ga).astype(BF16)
        dos = (dm * gb).astype(BF16)
        dof_ref[...] = dof
        dos_ref[...] = dos
        dofox_ref[...] = back(dof, wf_ref)
        dosb_ref[...] = back(dos, ws_ref)
        dgla = dm * of_ref[...] * ga * (1.0 - ga)
        dglb = dm * os_ref[...] * gb * (1.0 - gb)
        dgl_ref[:, 0:D_MODEL] = dgla.astype(BF16)
        dgl_ref[:, D_MODEL:2 * D_MODEL] = dglb.astype(BF16)

        @pl.when(pl.program_id(0) == 0)
        def _():
            db_ref[...] = jnp.zeros_like(db_ref)

        db_ref[0:1, :] += jnp.sum(dgla, axis=0, keepdims=True)
        db_ref[1:2, :] += jnp.sum(dglb, axis=0, keepdims=True)

    outs = pl.pallas_call(
        body, name="gate_bwd", grid=(t // ROW_TILE,),
        in_specs=[pl.BlockSpec((ROW_TILE, D_MODEL), lambda i: (i, 0)), pl.BlockSpec((ROW_TILE, D_MODEL), lambda i: (i, 1)),
                  _vec_spec(2), _row_spec(), _row_spec(), _row_spec(), pl.BlockSpec(w_out.shape, lambda i: (0, 0)),
                  _shards_spec(w_fox), _shards_spec(w_sb)],
        out_specs=[_row_spec(), _row_spec(), _row_spec(2 * D_MODEL), _vec_spec(2), _row_spec(D_ATT), _row_spec(D_ATT)],
        out_shape=[jax.ShapeDtypeStruct((t, D_MODEL), BF16)] * 2 + [jax.ShapeDtypeStruct((t, 2 * D_MODEL), BF16),
                                                                      jax.ShapeDtypeStruct((2, D_MODEL), F32)]
        + [jax.ShapeDtypeStruct((t, D_ATT), BF16)] * 2,
        compiler_params=_cparams(("arbitrary",)),
    )(gl, gl, b_gate, of, os_, dx, w_out, w_fox, w_sb)
    return outs


def _head_and_loss(x2, h3, p, w_gate, w_ple, g_final, target):
    t = x2.shape[0]

    def body(x2_ref, h3_ref, p_ref, wg_ref, wp_ref, g_ref, tgt_ref, dx3_ref, dpre_ref, dpe_ref, dg_ref, loss_ref):
        gp = jax.nn.sigmoid(_dot(h3_ref[...], wg_ref[...]))
        p_t = p_ref[...].astype(BF16)
        pe_t = jnp.concatenate([_dot(p_t, wp_ref[j]) for j in range(N_CHIPS)], axis=1)
        x3 = x2_ref[...] + gp * pe_t
        g = g_ref[...]
        xh, _ = _xhat(x3)
        err = xh * g - tgt_ref[...]
        dy = err * (1.0 / D_MODEL)
        dx3, dg = _rms_bwd_rows(dy, x3, g)
        dx3_ref[...] = dx3
        dpre_ref[...] = (dx3 * pe_t * gp * (1.0 - gp)).astype(BF16)
        dpe_ref[...] = (dx3 * gp).astype(BF16)

        @pl.when(pl.program_id(0) == 0)
        def _():
            dg_ref[...] = jnp.zeros_like(dg_ref)
            loss_ref[...] = jnp.zeros_like(loss_ref)

        dg_ref[...] += dg
        loss_ref[...] += 0.5 * jnp.sum(jnp.mean(err * err, axis=-1, keepdims=True), axis=0, keepdims=True)

    return pl.pallas_call(
        body, name="head_and_loss", grid=(t // ROW_TILE,),
        in_specs=[_row_spec(), _row_spec(), _row_spec(D_PLE), pl.BlockSpec(w_gate.shape, lambda i: (0, 0)),
                  pl.BlockSpec(w_ple.shape, lambda i: (0, 0, 0)), _vec_spec(), _row_spec()],
        out_specs=[_row_spec(), _row_spec(), _row_spec(), _vec_spec(), _vec_spec(1, LANES)],
        out_shape=[jax.ShapeDtypeStruct((t, D_MODEL), F32), jax.ShapeDtypeStruct((t, D_MODEL), BF16),
                   jax.ShapeDtypeStruct((t, D_MODEL), BF16), jax.ShapeDtypeStruct((1, D_MODEL), F32),
                   jax.ShapeDtypeStruct((1, LANES), F32)],
        compiler_params=_cparams(("arbitrary",)),
    )(x2, h3, p, w_gate, w_ple, g_final, target)


def _split3(v):
    hi = v.astype(BF16)
    r1 = v - hi.astype(F32)
    mid = r1.astype(BF16)
    lo = (r1 - mid.astype(F32)).astype(BF16)
    return hi, mid, lo


def _split2(v):
    hi = v.astype(BF16)
    return jnp.concatenate([hi, (v - hi.astype(F32)).astype(BF16)], axis=1)


def _dot(a, b, dims=_DIMS["nn"]):
    return lax.dot_general(a, b, dims, preferred_element_type=F32)


def _tri(n, rel):
    row = lax.broadcasted_iota(jnp.int32, (n, n), 0)
    col = lax.broadcasted_iota(jnp.int32, (n, n), 1)
    return rel(row, col).astype(BF16)


def _tri2(n, rel):
    t = _tri(n, rel)
    return jnp.concatenate([t, t], axis=0)


def _log_sigmoid(v):
    return -(jnp.maximum(-v, 0.0) + jnp.log(1.0 + jnp.exp(-jnp.abs(v))))


def _fox_prep(fl, b_forget, batch, seq):
    nb = seq // ATT_BLOCK

    def body(fl_ref, b_ref, cw_ref, cr_ref):
        col = lax.broadcasted_iota(jnp.int32, (ATT_BLOCK, F_PAD), 1)
        lower = _tri(ATT_BLOCK, lambda r, c: c <= r)
        upper = _tri(ATT_BLOCK, lambda r, c: r <= c)
        expand = (lax.broadcasted_iota(jnp.int32, (F_PAD, D_ATT), 1) // HEAD_DIM
                  == lax.broadcasted_iota(jnp.int32, (F_PAD, D_ATT), 0)).astype(BF16)
        carry_w = jnp.zeros((1, D_ATT), F32)
        carry_r = jnp.zeros((F_PAD, 1), F32)
        for i in range(nb):
            blk = slice(i * ATT_BLOCK, (i + 1) * ATT_BLOCK)
            logf = jnp.where(col < N_HEADS, _log_sigmoid(fl_ref[blk, :] + b_ref[...]), 0.0)
            cw = jnp.zeros((ATT_BLOCK, D_ATT), F32) + carry_w
            cr = jnp.zeros((F_PAD, ATT_BLOCK), F32) + carry_r
            for part in _split3(logf):
                cw += _dot(lower, _dot(part, expand).astype(BF16))
                cr += _dot(part, upper, TN_DIMS)
            cw_ref[blk, :] = cw
            cr_ref[:, blk] = cr[0:N_HEADS, :]
            carry_w = cw[ATT_BLOCK - 1:ATT_BLOCK, :]
            carry_r = cr[:, ATT_BLOCK - 1:ATT_BLOCK]

    return pl.pallas_call(
        body, name="fox_prep", grid=(batch,),
        in_specs=[pl.BlockSpec((seq, F_PAD), lambda b: (b, 0)), pl.BlockSpec((1, F_PAD), lambda b: (0, 0))],
        out_specs=[pl.BlockSpec((seq, D_ATT), lambda b: (b, 0)), pl.BlockSpec((N_HEADS, seq), lambda b: (b, 0))],
        out_shape=[jax.ShapeDtypeStruct((batch * seq, D_ATT), F32), jax.ShapeDtypeStruct((batch * N_HEADS, seq), F32)],
        compiler_params=_cparams(("parallel",)),
    )(fl, b_forget)


def _fox_post(dcs_wide, drs_wide, fl, b_forget, batch, seq):
    nb = seq // ATT_BLOCK

    def body(dcs_ref, drs_ref, fl_ref, b_ref, dfl_ref, db_ref):
        pick = (lax.broadcasted_iota(jnp.int32, (D_ATT, F_PAD), 0)
                == lax.broadcasted_iota(jnp.int32, (D_ATT, F_PAD), 1) * HEAD_DIM).astype(BF16)
        upper = _tri(ATT_BLOCK, lambda r, c: r <= c)
        col = lax.broadcasted_iota(jnp.int32, (ATT_BLOCK, F_PAD), 1)

        @pl.when(pl.program_id(0) == 0)
        def _():
            db_ref[...] = jnp.zeros_like(db_ref)

        carry = jnp.zeros((1, F_PAD), F32)
        for i in reversed(range(nb)):
            blk = slice(i * ATT_BLOCK, (i + 1) * ATT_BLOCK)
            narrow = jnp.zeros((ATT_BLOCK, F_PAD), F32)
            for part in _split3(drs_ref[blk, :] - dcs_ref[blk, :]):
                narrow += _dot(part, pick)
            after = jnp.zeros((ATT_BLOCK, F_PAD), F32) + carry
            for part in _split3(narrow):
                after += _dot(upper, part)
            carry = after[0:1, :]
            pre = fl_ref[blk, :] + b_ref[...]
            dfl = jnp.where(col < N_HEADS, after * jax.nn.sigmoid(-pre), 0.0)
            dfl_ref[blk, :] = dfl.astype(BF16)
            db_ref[...] += jnp.sum(dfl, axis=0, keepdims=True)

    return pl.pallas_call(
        body, name="fox_post", grid=(batch,),
        in_specs=[pl.BlockSpec((seq, D_ATT), lambda b: (b, 0)), pl.BlockSpec((seq, D_ATT), lambda b: (b, 0)),
                  pl.BlockSpec((seq, F_PAD), lambda b: (b, 0)), pl.BlockSpec((1, F_PAD), lambda b: (0, 0))],
        out_specs=[pl.BlockSpec((seq, F_PAD), lambda b: (b, 0)), pl.BlockSpec((1, F_PAD), lambda b: (0, 0))],
        out_shape=[jax.ShapeDtypeStruct((batch * seq, F_PAD), BF16), jax.ShapeDtypeStruct((1, F_PAD), F32)],
        compiler_params=_cparams(("arbitrary",)),
    )(dcs_wide, drs_wide, fl, b_forget)


N_PAIRS = N_HEADS // 2


def _att_specs(seq, col0, tq):
    nq = seq // tq
    q = pl.BlockSpec((tq, LANES), lambda b, hp, qi: (b * nq + qi, col0 + hp))
    k = pl.BlockSpec((seq, LANES), lambda b, hp, qi: (b, col0 + N_PAIRS + hp))
    v = pl.BlockSpec((seq, LANES), lambda b, hp, qi: (b, col0 + 2 * N_PAIRS + hp))
    return q, k, v


def _qblock_spec(seq, tq):
    nq = seq // tq
    return pl.BlockSpec((tq, LANES), lambda b, hp, qi: (b * nq + qi, hp))


def _kv_out_spec(seq):
    return pl.BlockSpec((seq, LANES), lambda b, hp, qi: (b, hp))


def _head_masks():
    lane = lax.broadcasted_iota(jnp.int32, (1, LANES), 1)
    return [(lane >= HEAD_DIM * j) & (lane < HEAD_DIM * (j + 1)) for j in range(2)]


def _stack_heads(t, masks):
    zero = jnp.zeros_like(t)
    return jnp.concatenate([jnp.where(masks[0], t, zero), jnp.where(masks[1], t, zero)], axis=0)


def _stack_cols(t):
    return jnp.concatenate([t[:, 0:1], t[:, HEAD_DIM:HEAD_DIM + 1]], axis=0)


def _unstack(t2, masks):
    tq = t2.shape[0] // 2
    return jnp.where(masks[0], t2[:tq], t2[tq:])


def _stacked_ids(tq, tk):
    row = lax.broadcasted_iota(jnp.int32, (2 * tq, tk), 0)
    col = lax.broadcasted_iota(jnp.int32, (2 * tq, tk), 1)
    first = lax.broadcasted_iota(jnp.int32, (2 * tq, 1), 0) < tq
    return col - jnp.where(row < tq, row, row - tq), first


def _sweep(qi, tq, tk, step, init, leftward):
    per = tq // tk
    whole = lambda carry: lax.fori_loop(0, per * qi, lambda i, c: step(per * qi - 1 - i if leftward else i, c, None), carry)
    crossed = [(per * qi + j, -j * tk) for j in range(per)]
    if leftward:
        carry = init
        for kb, lead in reversed(crossed):
            carry = step(kb, carry, lead)
        return whole(carry)
    carry = whole(init)
    for kb, lead in crossed:
        carry = step(kb, carry, lead)
    return carry


def _fox_fwd(qkv, c_wide, c_row, batch, seq):
    tq, tk = FOX_TILES
    nq = seq // tq

    def body(q_ref, k_ref, v_ref, cw_ref, cr_ref, o_ref, lse_ref):
        hp, qi = pl.program_id(1), pl.program_id(2)
        masks = _head_masks()
        ahead, first = _stacked_ids(tq, tk)
        q2 = _stack_heads(q_ref[...], masks) * SCALE
        ct = _stack_cols(cw_ref[...])

        def step(kb, carry, lead):
            m, l, acc = carry
            k0 = pl.multiple_of(kb * tk, tk)
            cs = jnp.where(first, cr_ref[pl.ds(2 * hp, 1), pl.ds(k0, tk)], cr_ref[pl.ds(2 * hp + 1, 1), pl.ds(k0, tk)])
            s = _dot(q2, k_ref[pl.ds(k0, tk), :], NT_DIMS) + ct - cs
            if lead is not None:
                s = jnp.where(ahead <= lead, s, NEG)
            m_new = jnp.maximum(m, jnp.max(s, axis=1, keepdims=True))
            p = jnp.exp(s - m_new)
            alpha = jnp.exp(m - m_new)
            l = alpha * l + jnp.sum(p, axis=1, keepdims=True)
            acc = alpha * acc + _dot(p.astype(BF16), v_ref[pl.ds(k0, tk), :])
            return m_new, l, acc

        init = (jnp.full((2 * tq, 1), NEG, F32), jnp.zeros((2 * tq, 1), F32), jnp.zeros((2 * tq, LANES), F32))
        m, l, acc = _sweep(qi, tq, tk, step, init, leftward=False)
        o_ref[...] = _unstack(acc / l, masks).astype(BF16)
        lse_ref[...] = _unstack(m + jnp.log(l), masks)

    q_spec, k_spec, v_spec = _att_specs(seq, 0, tq)
    qb = _qblock_spec(seq, tq)
    return pl.pallas_call(
        body, name="fox_fwd", grid=(batch, N_PAIRS, nq),
        in_specs=[q_spec, k_spec, v_spec, qb, pl.BlockSpec((N_HEADS, seq), lambda b, hp, qi: (b, 0))],
        out_specs=[qb, qb],
        out_shape=[jax.ShapeDtypeStruct((batch * seq, D_ATT), BF16), jax.ShapeDtypeStruct((batch * seq, D_ATT), F32)],
        compiler_params=_cparams(("parallel", "parallel", "arbitrary")),
    )(qkv, qkv, qkv, c_wide, c_row)


def _fox_bwd(qkv, c_wide, c_row, o, do, lse_wide, batch, seq, behind):
    tq, tk = FOX_TILES
    nq = seq // tq

    def body(q_ref, k_ref, v_ref, cw_ref, cr_ref, o_ref, do_ref, lse_ref,
             dq_ref, dk_ref, dv_ref, dcs_ref, drs_ref, dkc_acc, dv_acc):
        hp, qi = pl.program_id(1), pl.program_id(2)

        @pl.when(qi == 0)
        def _():
            dkc_acc[...] = jnp.zeros_like(dkc_acc)
            dv_acc[...] = jnp.zeros_like(dv_acc)

        masks = _head_masks()
        ahead, first = _stacked_ids(tq, tk)
        q_t, do_t = q_ref[...], do_ref[...]
        q2 = _stack_heads(q_t, masks) * SCALE
        do2 = _stack_heads(do_t, masks)
        q_and_ones = jnp.concatenate([q2, _stack_heads(jnp.ones_like(q_t), masks)], axis=1)
        ct = _stack_cols(cw_ref[...])
        lse = _stack_cols(lse_ref[...])
        prod = do_t.astype(F32) * o_ref[...].astype(F32)
        delta = jnp.concatenate([jnp.sum(jnp.where(mk, prod, 0.0), axis=1, keepdims=True) for mk in masks], axis=0)

        def step(kb, carry, lead):
            dq_acc, rs = carry
            k0 = pl.multiple_of(kb * tk, tk)
            kblk = k_ref[pl.ds(k0, tk), :]
            cs = jnp.where(first, cr_ref[pl.ds(2 * hp, 1), pl.ds(k0, tk)], cr_ref[pl.ds(2 * hp + 1, 1), pl.ds(k0, tk)])
            p = jnp.exp(_dot(q2, kblk, NT_DIMS) + ct - cs - lse)
            if lead is not None:
                p = jnp.where(ahead <= lead, p, 0.0)
            dp = _dot(do2, v_ref[pl.ds(k0, tk), :], NT_DIMS)
            ds = (p * (dp - delta)).astype(BF16)
            dkc_acc[pl.ds(k0, tk), :] += _dot(ds, q_and_ones, TN_DIMS)
            dv_acc[pl.ds(k0, tk), :] += _dot(p.astype(BF16), do2, TN_DIMS)
            return dq_acc + _dot(ds, kblk), rs + jnp.sum(ds.astype(F32), axis=1, keepdims=True)

        init = (jnp.zeros((2 * tq, LANES), F32), jnp.zeros((2 * tq, 1), F32))
        dq_acc, rs = _sweep(qi, tq, tk, step, init, leftward=False)
        dq_ref[...] = (_unstack(dq_acc, masks) * SCALE).astype(BF16)
        drs_ref[...] = _unstack(rs, masks)

        @pl.when(qi == nq - 1)
        def _():
            dk_ref[...] = dkc_acc[:, 0:LANES].astype(BF16)
            dcs_ref[...] = dkc_acc[:, LANES:2 * LANES]
            dv_ref[...] = dv_acc[...].astype(BF16)

    q_spec, k_spec, v_spec = _att_specs(seq, 0, tq)
    qb = _qblock_spec(seq, tq)
    return _call_behind(
        body, behind, name="fox_bwd", grid=(batch, N_PAIRS, nq),
        in_specs=[q_spec, k_spec, v_spec, qb, pl.BlockSpec((N_HEADS, seq), lambda b, hp, qi: (b, 0)), qb, qb, qb],
        out_specs=[qb, _kv_out_spec(seq), _kv_out_spec(seq), _kv_out_spec(seq), qb],
        out_shape=[jax.ShapeDtypeStruct((batch * seq, D_ATT), BF16)] * 3 + [jax.ShapeDtypeStruct((batch * seq, D_ATT), F32)] * 2,
        scratch_shapes=[pltpu.VMEM((seq, 2 * LANES), F32), pltpu.VMEM((seq, LANES), F32)],
        operands=(qkv, qkv, qkv, c_wide, c_row, o, do, lse_wide))


def _sb_logits(q2, kblk):
    z = _dot(q2, kblk, NT_DIMS)
    lsn = jnp.minimum(-z, 0.0) - jnp.log(1.0 + jnp.exp(-jnp.abs(z)))
    return lsn + z, lsn


def _sb_fwd(qkv, batch, seq, behind):
    tq, tk = SB_TILES
    nq = seq // tq

    def body(q_ref, k_ref, v_ref, o_ref, rt_ref):
        qi = pl.program_id(2)
        masks = _head_masks()
        ahead, _ = _stacked_ids(tq, tk)
        later = _tri2(tk, lambda r, c: r > c)
        q2 = _stack_heads(q_ref[...], masks) * SCALE

        def step(kb, carry, lead):
            run, acc = carry
            k0 = pl.multiple_of(kb * tk, tk)
            ls, lsn = _sb_logits(q2, k_ref[pl.ds(k0, tk), :])
            if lead is not None:
                lsn = jnp.where(ahead < lead, lsn, 0.0)
            w = jnp.exp(ls + _dot(_split2(lsn), later) + run)
            if lead is not None:
                w = jnp.where(ahead < lead, w, 0.0)
            return run + jnp.sum(lsn, axis=1, keepdims=True), acc + _dot(w.astype(BF16), v_ref[pl.ds(k0, tk), :])

        init = (jnp.zeros((2 * tq, 1), F32), jnp.zeros((2 * tq, LANES), F32))
        run, acc = _sweep(qi, tq, tk, step, init, leftward=True)
        o_ref[...] = _unstack(acc, masks).astype(BF16)
        rt_ref[...] = _unstack(run, masks)

    q_spec, k_spec, v_spec = _att_specs(seq, 3 * N_PAIRS, tq)
    qb = _qblock_spec(seq, tq)
    return _call_behind(
        body, behind, name="sb_fwd", grid=(batch, N_PAIRS, nq), in_specs=[q_spec, k_spec, v_spec], out_specs=[qb, qb],
        out_shape=[jax.ShapeDtypeStruct((batch * seq, D_ATT), BF16), jax.ShapeDtypeStruct((batch * seq, D_ATT), F32)],
        scratch_shapes=[], operands=(qkv, qkv, qkv))


def _sb_bwd(qkv, do, rt_wide, batch, seq, behind):
    tq, tk = SB_TILES
    nq = seq // tq

    def body(q_ref, k_ref, v_ref, do_ref, rt_ref, dq_ref, dk_ref, dv_ref, dk_acc, dv_acc):
        qi = pl.program_id(2)

        @pl.when(qi == 0)
        def _():
            dk_acc[...] = jnp.zeros_like(dk_acc)
            dv_acc[...] = jnp.zeros_like(dv_acc)

        masks = _head_masks()
        ahead, _ = _stacked_ids(tq, tk)
        later = _tri2(tk, lambda r, c: r > c)
        earlier = _tri(tk, lambda r, c: r < c)
        q2 = _stack_heads(q_ref[...], masks) * SCALE
        do2 = _stack_heads(do_ref[...], masks)
        total = _stack_cols(rt_ref[...])

        def step(kb, carry, lead):
            pref, epre, dq_acc = carry
            k0 = pl.multiple_of(kb * tk, tk)
            kblk = k_ref[pl.ds(k0, tk), :]
            ls, lsn_all = _sb_logits(q2, kblk)
            lsn = lsn_all if lead is None else jnp.where(ahead < lead, lsn_all, 0.0)
            rs = jnp.sum(lsn, axis=1, keepdims=True)
            w = jnp.exp(ls + _dot(_split2(lsn), later) + (total - pref - rs))
            if lead is not None:
                w = jnp.where(ahead < lead, w, 0.0)
            e = w * _dot(do2, v_ref[pl.ds(k0, tk), :], NT_DIMS)
            before = _dot(e.astype(BF16), earlier) + epre
            dz = e * jnp.exp(lsn_all) - jnp.exp(ls) * before
            if lead is not None:
                dz = jnp.where(ahead < lead, dz, 0.0)
            dz = dz.astype(BF16)
            dk_acc[pl.ds(k0, tk), :] += _dot(dz, q2, TN_DIMS)
            dv_acc[pl.ds(k0, tk), :] += _dot(w.astype(BF16), do2, TN_DIMS)
            return pref + rs, epre + jnp.sum(e, axis=1, keepdims=True), dq_acc + _dot(dz, kblk)

        init = (jnp.zeros((2 * tq, 1), F32), jnp.zeros((2 * tq, 1), F32), jnp.zeros((2 * tq, LANES), F32))
        dq_acc = _sweep(qi, tq, tk, step, init, leftward=False)[2]
        dq_ref[...] = (_unstack(dq_acc, masks) * SCALE).astype(BF16)

        @pl.when(qi == nq - 1)
        def _():
            dk_ref[...] = dk_acc[...].astype(BF16)
            dv_ref[...] = dv_acc[...].astype(BF16)

    q_spec, k_spec, v_spec = _att_specs(seq, 3 * N_PAIRS, tq)
    qb = _qblock_spec(seq, tq)
    return _call_behind(
        body, behind, name="sb_bwd", grid=(batch, N_PAIRS, nq), in_specs=[q_spec, k_spec, v_spec, qb, qb],
        out_specs=[qb, _kv_out_spec(seq), _kv_out_spec(seq)], out_shape=[jax.ShapeDtypeStruct((batch * seq, D_ATT), BF16)] * 3,
        scratch_shapes=[pltpu.VMEM((seq, LANES), F32), pltpu.VMEM((seq, LANES), F32)], operands=(qkv, qkv, qkv, do, rt_wide))


def _local_step(x, p, target, w, rest, vec, place):
    batch, seq, _ = x.shape
    t = batch * seq
    x = x.reshape(t, D_MODEL)
    target = target.reshape(t, D_MODEL)
    p = p.reshape(t, D_PLE)
    big = dict(tm=1024, tn=1024, tk=1024)

    h1 = _norm_fwd(x, vec["g_mix"], "norm_mix")
    qkv = _mm(h1, w["qkv"], mode="nn", name="proj_qkv", out_dtype=BF16, **big)
    gl = _mm(h1, w["gate"], mode="nn", name="proj_gate", **big)
    fl = _mm(h1, w["forget"], mode="nn", name="proj_forget", **big)
    c_wide, c_row = _fox_prep(fl, vec["b_forget"], batch, seq)
    o_fox, lse_wide = _fox_fwd(qkv, c_wide, c_row, batch, seq)
    (o_sb, rt_wide), gathered = _sb_fwd(qkv, batch, seq, rest)
    w = dict(w, **_rest_weights(dict(zip(EARLY + ("b_gate",), gathered))))
    merged, of, os_ = _gate_fwd(gl, w["b_gate"], o_fox, o_sb, w["branch_fox"], w["branch_sb"])
    x1, h2 = _mm_res_norm(merged, w["out"], x, vec["g_mlp"], "proj_out_norm")
    ar = _mm(h2, w["up"], mode="nn", name="mlp_up", out_dtype=BF16, epi=lambda acc, _: jnp.maximum(acc, 0.0),
             col_shards=True, **big)
    x2, h3 = _mm_res_norm(ar, w["down"], x1, vec["g_ple"], "mlp_down_norm", a_fn=_relu2)

    dx3, dpre, dpe, dg_final, loss = _head_and_loss(x2, h3, p, w["ple_gate"], w["ple"], vec["g_final"], target)
    gw = {}
    gw["ple"] = _mm(p, dpe, mode="tn", name="d_w_ple", col_shards=True, **big)
    gw["ple_gate"] = _mm(h3, dpre, mode="tn", name="d_w_ple_gate", **big)
    dx2, dx2b, dg_ple = _mm_norm_bwd([(dpre, w["ple_gate"])], None, x2, vec["g_ple"], dx3, "d_h_ple_norm_bwd")
    gw["down"] = _mm(ar, dx2b, mode="tn", name="d_w_down", a_fn=_relu2, **big)
    da = _mm(dx2b, w["down"], mode="nt", name="d_act", out_dtype=BF16,
             epi=lambda acc, r: acc * (2.0 * r.astype(F32)), extra=ar, **big)
    gw["up"] = _mm(h2, da, mode="tn", name="d_w_up", col_shards=True, **big)
    dx1, dx1b, dg_mlp = _mm_norm_bwd([(da, w["up"])], None, x1, vec["g_mlp"], dx2, "d_h_mlp_norm_bwd")
    gw["out"] = _mm(merged, dx1b, mode="tn", name="d_w_out", **big)
    dof, dos, dgl, gw["b_gate"], do_fox, do_sb = _gate_bwd(gl, w["b_gate"], of, os_, dx1b, w["out"], w["branch_fox"],
                                                                  w["branch_sb"])
    gw["branch_fox"] = _mm(o_fox, dof, mode="tn", name="d_w_branch_fox", col_shards=True, **big)
    gw["branch_sb"] = _mm(o_sb, dos, mode="tn", name="d_w_branch_sb", col_shards=True, **big)
    early = _early_slots(gw)
    early = [early[n] for n in EARLY]
    (dq_a, dk_a, dv_a, dcs_wide, drs_wide), received = _fox_bwd(qkv, c_wide, c_row, o_fox, do_fox, lse_wide, batch, seq,
                                                                _swap_halves(early))
    sums = _sum_sibling(place, early, received, "sum_sibling_early")
    (dq_b, dk_b, dv_b), others = _sb_bwd(qkv, do_sb, rt_wide, batch, seq, _exchange_chips(sums))
    mine = [None] * len(EARLY)
    for group, tag in ((BIG, "big"), (SMALL, "small")):
        for t, res in zip(group, _sum_chips(place, *[[a[t] for t in group] for a in (early, received, others)], "sum_chips_" + tag)):
            mine[t] = res
    dfl, db_forget = _fox_post(dcs_wide, drs_wide, fl, vec["b_forget"], batch, seq)
    dqkv = jnp.concatenate([dq_a, dk_a, dv_a, dq_b, dk_b, dv_b], axis=1)
    gw["qkv"], theirs = _mm(dqkv, h1, mode="tn", name="d_w_qkv", behind=_share_halves(mine), flat_out=True, **big)
    reduced = dict(zip(EARLY, zip(mine, theirs)))
    gw["gate"] = _mm(dgl, h1, mode="tn", name="d_w_gate", flat_out=True, **big)
    gw["forget"] = _mm(dfl, h1, mode="tn", name="d_w_forget", flat_out=True, **big)
    late = [_w_in_slots(gw)]
    dh1, received = _mm(dqkv, w["qkv"], mode="nt", name="d_h_qkv", behind=_swap_halves(late), **big)
    sums = _sum_sibling(place, late, received, "sum_sibling_w_in")
    (grad_x, _, dg_mix), others = _mm_norm_bwd([(dgl, w["gate"]), (dfl, w["forget"])], dh1, x, vec["g_mix"], dx1,
                                               "d_h_gate_norm_bwd", behind=_exchange_chips(sums))
    mine = _sum_chips(place, late, received, others, "sum_chips_w_in")
    reduced["w_in"] = (mine[0], _run_exchange(_share_halves(mine), "reduce_share_w_in")[0])
    gvec = {"g_mix": dg_mix, "b_forget": db_forget[:, 0:N_HEADS], "g_mlp": dg_mlp, "g_ple": dg_ple,
            "g_final": dg_final, "b_gate": gw["b_gate"]}
    return loss, grad_x.reshape(batch, seq, D_MODEL), reduced, gvec


ANY = pl.BlockSpec(memory_space=pl.ANY)
SHARDED = ("w_in", "w_branch_fox", "w_branch_sb", "w_out", "w_up", "w_down", "w_ple_gate", "w_ple")
ROW_ALIGN = 16
F32_ROWS = 8


def _place():
    return lax.axis_index("x"), lax.axis_index("y"), lax.axis_index("c")


def _other_chips(x, y):
    return [(1 - x, y), (x, 1 - y), (1 - x, 1 - y)]


def _half(ref, h):
    r = ref.shape[0] // 2
    assert r % ROW_ALIGN == 0
    return ref.at[pl.ds(pl.multiple_of(h * r, ROW_ALIGN), r)]


def _remote(src, dst, sems, idx, to):
    send_sems, recv_sems = sems
    return pltpu.make_async_remote_copy(src_ref=src, dst_ref=dst, send_sem=send_sems.at[idx], recv_sem=recv_sems.at[idx],
                                        device_id=to, device_id_type=MESH)


class _Exchange:
    def __init__(self, operands, out_shapes, sem_shape, start, finish):
        self.operands, self.out_shapes, self.sem_shape, self.start, self.finish = operands, out_shapes, sem_shape, start, finish

    def scratch(self):
        return [pltpu.SemaphoreType.DMA(self.sem_shape), pltpu.SemaphoreType.DMA(self.sem_shape)]


def _run_exchange(ex, name):
    n = len(ex.operands)

    def body(*refs):
        ex.start(refs[:n], refs[n:2 * n], refs[2 * n:])
        ex.finish(refs[:n], refs[n:2 * n], refs[2 * n:])

    return pl.pallas_call(body, name=name, in_specs=[ANY] * n, out_specs=[ANY] * n, out_shape=ex.out_shapes,
                          scratch_shapes=ex.scratch())(*ex.operands)


def _call_behind(body, ex, *, name, grid, in_specs, out_specs, out_shape, scratch_shapes, operands):
    n_in, n_out, nx = len(in_specs), len(out_specs), len(ex.operands)

    def wrapped(*refs):
        ins, x_in = refs[:n_in], refs[n_in:n_in + nx]
        outs, x_out = refs[n_in + nx:n_in + nx + n_out], refs[n_in + nx + n_out:n_in + 2 * nx + n_out]
        scratch, sems = refs[n_in + 2 * nx + n_out:-2], refs[-2:]
        first, last = None, None
        for d, steps in enumerate(grid):
            at_start, at_end = pl.program_id(d) == 0, pl.program_id(d) == steps - 1
            first = at_start if first is None else first & at_start
            last = at_end if last is None else last & at_end

        @pl.when(first)
        def _():
            ex.start(x_in, x_out, sems)

        body(*ins, *outs, *scratch)

        @pl.when(last)
        def _():
            ex.finish(x_in, x_out, sems)

    res = pl.pallas_call(
        wrapped, name=name, grid=grid, in_specs=list(in_specs) + [ANY] * nx, out_specs=list(out_specs) + [ANY] * nx,
        out_shape=list(out_shape) + list(ex.out_shapes), scratch_shapes=list(scratch_shapes) + ex.scratch(),
        compiler_params=_cparams(("arbitrary",) * len(grid)),
    )(*operands, *ex.operands)
    return res[:n_out], res[n_out:]


def _gather_weights(shards):
    n = len(shards)

    def first_copies(src, out, sems):
        x, y, c = _place()
        me = 2 * x + y
        copies = [_remote(_half(src[t], c), _half(out[t].at[me], c), sems, (t, k), (px, py, c))
                  for t in range(n) for k, (px, py) in enumerate(_other_chips(x, y))]
        return copies + [_remote(src[t], out[t].at[me], sems, (t, 3), (x, y, 1 - c)) for t in range(n)]

    def start(src, out, sems):
        for cp in first_copies(src, out, sems):
            cp.start()

    def finish(src, out, sems):
        x, y, c = _place()
        me = 2 * x + y
        sibling = (x, y, 1 - c)
        chips = _other_chips(x, y)
        passes = []
        for t in range(n):
            for k, (px, py) in enumerate(chips):
                landed = _half(out[t].at[2 * px + py], c)
                _remote(landed, landed, sems, (t, k), (px, py, c)).wait_recv()
                passes.append(_remote(landed, landed, sems, (t, 4 + k), sibling))
                passes[-1].start()
        for t in range(n):
            _remote(src[t], out[t].at[me], sems, (t, 3), sibling).wait_recv()
            for k, (px, py) in enumerate(chips):
                passed = _half(out[t].at[2 * px + py], 1 - c)
                _remote(passed, passed, sems, (t, 4 + k), sibling).wait_recv()
        for cp in first_copies(src, out, sems) + passes:
            cp.wait_send()

    return _Exchange(shards, [jax.ShapeDtypeStruct((N_CHIPS,) + s.shape, s.dtype) for s in shards], (n, 7), start, finish)


def _simple_exchange(operands, out_shapes, copies):
    def start(src, out, sems):
        for cp in copies(src, out, sems):
            cp.start()

    def finish(src, out, sems):
        for cp in copies(src, out, sems):
            cp.wait_recv()
        for cp in copies(src, out, sems):
            cp.wait_send()

    return _Exchange(operands, out_shapes, (len(operands),), start, finish)


def _swap_halves(slots):
    def copies(src, out, sems):
        x, y, c = _place()
        res = []
        for t in range(len(slots)):
            r = src[t].shape[1] // 2
            rows = pl.ds(pl.multiple_of((1 - c) * r, F32_ROWS), r)
            res.append(_remote(src[t].at[:, rows], out[t], sems, t, (x, y, 1 - c)))
        return res

    return _simple_exchange(slots, [jax.ShapeDtypeStruct((N_CHIPS, s.shape[1] // 2, s.shape[2]), s.dtype) for s in slots], copies)


def _exchange_chips(sums):
    n = len(sums)

    def copies(src, out, sems):
        x, y, c = _place()
        return [_remote(src[t].at[2 * px + py], out[t].at[k], sems, (t, k), (px, py, c))
                for t in range(n) for k, (px, py) in enumerate(_other_chips(x, y))]

    def start(src, out, sems):
        for cp in copies(src, out, sems):
            cp.start()

    def finish(src, out, sems):
        for cp in copies(src, out, sems):
            cp.wait_recv()
        for cp in copies(src, out, sems):
            cp.wait_send()

    return _Exchange(sums, [jax.ShapeDtypeStruct((3,) + s.shape[1:], s.dtype) for s in sums], (n, 3), start, finish)


def _share_halves(mine):
    def copies(src, out, sems):
        x, y, c = _place()
        return [_remote(src[t], out[t], sems, t, (x, y, 1 - c)) for t in range(len(mine))]

    return _simple_exchange(mine, [jax.ShapeDtypeStruct(s.shape, s.dtype) for s in mine], copies)


def _walk(name, place, parts):
    starts = [sum(p[0] for p in parts[:t]) for t in range(len(parts))]
    held = lambda index, start, steps: (lambda s, pr: index(jnp.clip(s - start, 0, steps - 1), pr))
    in_specs, out_specs, out_shapes, operands = [], [], [], []
    for (steps, ins, outs, shapes, ops, _), start in zip(parts, starts):
        in_specs += [pl.BlockSpec(blk, held(index, start, steps)) for blk, index in ins]
        out_specs += [pl.BlockSpec(blk, held(index, start, steps)) for blk, index in outs]
        out_shapes += list(shapes)
        operands += list(ops)

    def body(place_ref, *refs):
        s = pl.program_id(0)
        i, o = 0, len(in_specs)
        for (steps, ins, outs, _, _, fn), start in zip(parts, starts):
            mine_in, mine_out = refs[i:i + len(ins)], refs[o:o + len(outs)]
            i, o = i + len(ins), o + len(outs)

            @pl.when((s >= start) & (s < start + steps))
            def _(mine_in=mine_in, mine_out=mine_out, start=start, fn=fn):
                fn(s - start, mine_in, mine_out)

    res = pl.pallas_call(
        body, name=name, out_shape=out_shapes,
        grid_spec=pltpu.PrefetchScalarGridSpec(num_scalar_prefetch=1, grid=(sum(p[0] for p in parts),), in_specs=in_specs,
                                               out_specs=out_specs),
        compiler_params=_cparams(("arbitrary",)),
    )(place, *operands)
    counts = [len(p[2]) for p in parts]
    return [res[sum(counts[:t]):sum(counts[:t + 1])] for t in range(len(parts))]


def _sum_sibling(place, slots, received, name):
    def part(slot, got):
        n, rows, cols = got.shape
        block = (None, rows, cols)

        def fn(j, ins, outs):
            outs[0][...] = (ins[0][...] + ins[1][...]).astype(BF16)

        return (n, [(block, lambda j, pr: (j, pr[1], 0)), (block, lambda j, pr: (j, 0, 0))], [(block, lambda j, pr: (j, 0, 0))],
                [jax.ShapeDtypeStruct(got.shape, BF16)], [slot, got], fn)

    return [r[0] for r in _walk(name, place, [part(s, g) for s, g in zip(slots, received)])]


def _sum_chips(place, slots, received, others, name):
    def part(slot, got, other):
        _, rows, cols = got.shape
        block = (None, rows, cols)

        def fn(_, ins, outs):
            own = ins[0][...] + ins[1][...]
            outs[0][...] = ((own + ins[2][0].astype(F32)) + ins[2][1].astype(F32)) + ins[2][2].astype(F32)

        return (1, [(block, lambda _, pr: (pr[0], pr[1], 0)), (block, lambda _, pr: (pr[0], 0, 0)),
                    ((3, rows, cols), lambda _, pr: (0, 0, 0))], [((rows, cols), lambda _, pr: (0, 0))],
                [jax.ShapeDtypeStruct((rows, cols), F32)], [slot, got, other], fn)

    return [r[0] for r in _walk(name, place, [part(*t) for t in zip(slots, received, others)])]


N_DEVICES = 8


def _sum_devices(block, name):
    def body(v_ref, o_ref, land_ref, send_sems, recv_sems):
        x, y, c = _place()
        me = 4 * x + 2 * y + c
        copies = []
        for mask in range(1, N_DEVICES):
            peer = (x ^ (mask >> 2), y ^ ((mask >> 1) & 1), c ^ (mask & 1))
            copies.append(pltpu.make_async_remote_copy(src_ref=v_ref, dst_ref=land_ref.at[me], send_sem=send_sems.at[mask - 1],
                                                       recv_sem=recv_sems.at[mask - 1], device_id=peer, device_id_type=MESH))
        for cp in copies:
            cp.start()
        land_ref[me] = v_ref[...]
        for cp in copies:
            cp.wait_recv()
        total = land_ref[0]
        for d in range(1, N_DEVICES):
            total = total + land_ref[d]
        o_ref[...] = total
        for cp in copies:
            cp.wait_send()

    vmem = pl.BlockSpec(memory_space=pltpu.VMEM)
    return pl.pallas_call(
        body, name=name, in_specs=[vmem], out_specs=vmem, out_shape=jax.ShapeDtypeStruct(block.shape, F32),
        scratch_shapes=[pltpu.VMEM((N_DEVICES,) + block.shape, F32), pltpu.SemaphoreType.DMA((N_DEVICES - 1,)),
                        pltpu.SemaphoreType.DMA((N_DEVICES - 1,))],
    )(block)


def _vec_block(g_mix, g_mlp, g_ple, g_final, b_forget, b_gate_rows, last=None):
    pad = lambda a: jnp.concatenate([a, jnp.zeros((a.shape[0], D_MODEL - a.shape[1]), F32)], axis=1)
    last = jnp.zeros((1, 0), F32) if last is None else last
    return jnp.concatenate([g_mix, g_mlp, g_ple, g_final.reshape(1, D_MODEL), pad(b_forget), pad(b_gate_rows), pad(last)],
                           axis=0)


def _adam_math(w, g, m, v):
    m_new = ADAM_B1 * m + (1.0 - ADAM_B1) * g
    v_new = ADAM_B2 * v + (1.0 - ADAM_B2) * (g * g)
    m_hat = m_new / (1.0 - ADAM_B1 ** ADAM_STEP)
    v_hat = v_new / (1.0 - ADAM_B2 ** ADAM_STEP)
    return -ADAM_LR * (m_hat / (jnp.sqrt(v_hat) + ADAM_EPS) + ADAM_WD * w), m_new, v_new


def _adamw_halves(place, weights, name):
    def part(w, m, v, g_mine, g_theirs):
        rows, cols = g_mine.shape
        whole = ((rows, cols), lambda s, pr: (pr[1] + s - 2 * pr[1] * s, 0))
        half = ((rows, cols), lambda s, pr: (0, 0))

        def fn(s, ins, outs):
            g = jnp.where(s == 0, ins[3][...], ins[4][...])
            outs[0][...] = g
            outs[1][...], outs[2][...], outs[3][...] = _adam_math(ins[0][...], g, ins[1][...], ins[2][...])

        return (2, [whole] * 3 + [half] * 2, [whole] * 4, [jax.ShapeDtypeStruct(w.shape, F32)] * 4, [w, m, v, g_mine, g_theirs], fn)

    return _walk(name, place, [part(*t) for t in weights])


def _adamw_vec(w, g, m, v):
    def body(w_ref, g_ref, m_ref, v_ref, d_ref, nm_ref, nv_ref):
        d_ref[...], nm_ref[...], nv_ref[...] = _adam_math(w_ref[...], g_ref[...], m_ref[...], v_ref[...])

    return pl.pallas_call(body, name="adamw_vectors", out_shape=[jax.ShapeDtypeStruct(w.shape, F32)] * 3)(w, g, m, v)


WEIGHT_NAMES = ("g_mix", "w_in", "b_forget", "b_gate", "w_branch_fox", "w_branch_sb", "w_out", "g_mlp", "w_up", "w_down",
                "g_ple", "w_ple_gate", "w_ple", "g_final")
W_IN_SHARD = D_IN // N_CHIPS
Q_END, F_END, B_END = 3 * D_ATT, 3 * D_ATT + N_HEADS, 6 * D_ATT + N_HEADS
GATE_SHARD = D_MODEL // N_CHIPS


LATE = SHARDED[:1]
EARLY = SHARDED[1:]
BIG = tuple(t for t, n in enumerate(EARLY) if n in ("w_up", "w_down"))
SMALL = tuple(t for t in range(len(EARLY)) if t not in BIG)


def _first_weights(w_in_slots):
    def cols(*ranges):
        parts = []
        for lo, hi in ranges:
            for j in range(N_CHIPS):
                a, b = max(lo, j * W_IN_SHARD), min(hi, (j + 1) * W_IN_SHARD)
                if a < b:
                    parts.append(w_in_slots[j, :, a - j * W_IN_SHARD:b - j * W_IN_SHARD])
        return parts

    forget = jnp.concatenate(cols((Q_END, F_END)) + [jnp.zeros((D_MODEL, F_PAD - N_HEADS), BF16)], axis=1)
    return {"qkv": jnp.concatenate(cols((0, Q_END), (F_END, B_END)), axis=1), "gate": jnp.concatenate(cols((B_END, D_IN)), axis=1),
            "forget": forget}


GATE_ROWS = 2 * ROW_ALIGN


def _gate_bits(b_gate):
    bits = lax.bitcast_convert_type(b_gate, BF16).reshape(2, 2 * GATE_SHARD)
    return jnp.concatenate([bits, jnp.zeros((GATE_ROWS - 2, 2 * GATE_SHARD), BF16)], axis=0)


def _rest_weights(gathered):
    rows = lambda a: a.reshape(N_CHIPS * a.shape[1], a.shape[2])
    bits = gathered["b_gate"][:, :2].reshape(N_CHIPS, 2, GATE_SHARD, 2)
    b_gate = jnp.transpose(lax.bitcast_convert_type(bits, F32), (1, 0, 2)).reshape(2, D_MODEL)
    return {"branch_fox": gathered["w_branch_fox"], "branch_sb": gathered["w_branch_sb"], "out": rows(gathered["w_out"]),
            "up": gathered["w_up"], "down": rows(gathered["w_down"]), "ple_gate": rows(gathered["w_ple_gate"]),
            "ple": gathered["w_ple"], "b_gate": b_gate}


def _early_slots(gw):
    rows = lambda a: a.reshape(N_CHIPS, a.shape[0] // N_CHIPS, a.shape[1])
    return {"w_branch_fox": gw["branch_fox"], "w_branch_sb": gw["branch_sb"], "w_out": rows(gw["out"]), "w_up": gw["up"],
            "w_down": rows(gw["down"]), "w_ple_gate": rows(gw["ple_gate"]), "w_ple": gw["ple"]}


W_IN_FLAT = (W_IN_SHARD * D_MODEL // LANES, LANES)


def _w_in_slots(gw):
    c = D_MODEL // LANES
    g_t = jnp.concatenate([gw["qkv"][:Q_END * c], gw["forget"][:N_HEADS * c], gw["qkv"][Q_END * c:], gw["gate"]], axis=0)
    return g_t.reshape((N_CHIPS,) + W_IN_FLAT)


def _flat(a):
    return jnp.transpose(a, (2, 0, 1)).reshape(W_IN_FLAT)


def _unflat(a):
    return jnp.transpose(a.reshape(W_IN_SHARD, D_MODEL // LANES, LANES), (1, 2, 0)).reshape(1, D_MODEL, W_IN_SHARD)


def kernel(x, p, g_mix, w_in, b_forget, b_gate, w_branch_fox, w_branch_sb, w_out, g_mlp, w_up, w_down, g_ple, w_ple_gate, w_ple, g_final, loss_target, m_g_mix, m_w_in, m_b_forget, m_b_gate, m_w_branch_fox, m_w_branch_sb, m_w_out, m_g_mlp, m_w_up, m_w_down, m_g_ple, m_w_ple_gate, m_w_ple, m_g_final, v_g_mix, v_w_in, v_b_forget, v_b_gate, v_w_branch_fox, v_w_branch_sb, v_w_out, v_g_mlp, v_w_up, v_w_down, v_g_ple, v_w_ple_gate, v_w_ple, v_g_final):
    weights = dict(g_mix=g_mix, w_in=w_in, b_forget=b_forget, b_gate=b_gate, w_branch_fox=w_branch_fox,
                   w_branch_sb=w_branch_sb, w_out=w_out, g_mlp=g_mlp, w_up=w_up, w_down=w_down, g_ple=g_ple,
                   w_ple_gate=w_ple_gate, w_ple=w_ple, g_final=g_final)
    first = dict(g_mix=m_g_mix, w_in=m_w_in, b_forget=m_b_forget, b_gate=m_b_gate, w_branch_fox=m_w_branch_fox,
                 w_branch_sb=m_w_branch_sb, w_out=m_w_out, g_mlp=m_g_mlp, w_up=m_w_up, w_down=m_w_down, g_ple=m_g_ple,
                 w_ple_gate=m_w_ple_gate, w_ple=m_w_ple, g_final=m_g_final)
    second = dict(g_mix=v_g_mix, w_in=v_w_in, b_forget=v_b_forget, b_gate=v_b_gate, w_branch_fox=v_w_branch_fox,
                  w_branch_sb=v_w_branch_sb, w_out=v_w_out, g_mlp=v_g_mlp, w_up=v_w_up, w_down=v_w_down, g_ple=v_g_ple,
                  w_ple_gate=v_w_ple_gate, w_ple=v_w_ple, g_final=v_g_final)
    cx, cy, cc = _place()
    chip = 2 * cx + cy
    place = jnp.stack([chip, cc]).astype(jnp.int32)
    col0 = chip * GATE_SHARD

    (w_in_slots,) = _run_exchange(_gather_weights([weights[n][0].astype(BF16) for n in LATE]), "gather_w_in")
    rest = _gather_weights([weights[n][0].astype(BF16) for n in EARLY] + [_gate_bits(b_gate[0])])
    vec = {"g_mix": g_mix, "b_forget": jnp.concatenate([b_forget, jnp.zeros((1, F_PAD - N_HEADS), F32)], axis=1),
           "g_mlp": g_mlp, "g_ple": g_ple, "g_final": g_final.reshape(1, D_MODEL)}

    loss, grad_x, reduced, gvec = _local_step(x, p[0], loss_target, _first_weights(w_in_slots), rest, vec, place)

    out = {}
    args = lambda n: (weights[n][0], first[n][0], second[n][0]) + tuple(reduced[n])
    for names, tag in [([EARLY[t] for t in SMALL], "small")] + [([EARLY[t]], EARLY[t]) for t in BIG]:
        for n, res in zip(names, _adamw_halves(place, [args(n) for n in names], "adamw_" + tag)):
            out[n] = [r[None] for r in res]
    (res,) = _adamw_halves(place, [(_flat(w_in), _flat(m_w_in), _flat(v_w_in)) + tuple(reduced["w_in"])], "adamw_w_in")
    out["w_in"] = [_unflat(r) for r in res]

    g_block = _sum_devices(_vec_block(gvec["g_mix"], gvec["g_mlp"], gvec["g_ple"], gvec["g_final"][0], gvec["b_forget"],
                                      gvec["b_gate"], loss), "reduce_vectors")
    loss = g_block[7, 0]
    g_gate = lax.dynamic_slice(g_block[5:7], (0, col0), (2, GATE_SHARD))
    blocks = [_vec_block(d["g_mix"], d["g_mlp"], d["g_ple"], d["g_final"], d["b_forget"], d["b_gate"][0])
              for d in (weights, first, second)]
    g_rows = jnp.concatenate([g_block[0:5], jnp.concatenate([g_gate, jnp.zeros((2, D_MODEL - GATE_SHARD), F32)], axis=1),
                              jnp.zeros((1, D_MODEL), F32)], axis=0)
    res = (g_rows,) + tuple(_adamw_vec(blocks[0], g_rows, blocks[1], blocks[2]))
    out["g_mix"] = [r[0:1] for r in res]
    out["g_mlp"] = [r[1:2] for r in res]
    out["g_ple"] = [r[2:3] for r in res]
    out["g_final"] = [r[3] for r in res]
    out["b_forget"] = [r[4:5, :N_HEADS] for r in res]
    out["b_gate"] = [r[5:7, :GATE_SHARD][None] for r in res]
    return (loss, grad_x, *[out[n][0] for n in WEIGHT_NAMES], *[out[n][1] for n in WEIGHT_NAMES],
            *[out[n][2] for n in WEIGHT_NAMES], *[out[n][3] for n in WEIGHT_NAMES])
```

```python
import jax
import jax.numpy as jnp
from jax import lax
from jax.experimental import pallas as pl
from jax.experimental.pallas import tpu as pltpu

F32 = jnp.float32
BF16 = jnp.bfloat16

D_MODEL = 1024
HEAD_DIM = 64
N_HEADS = 8
D_ATT = N_HEADS * HEAD_DIM
D_PLE = 256
D_IN = 6 * D_ATT + N_HEADS + 2 * D_MODEL
F_PAD = 128
EPS = 1e-6
SCALE = HEAD_DIM ** -0.5
N_CHIPS = 4
LANES = 128
ATT_BLOCK = 256
FOX_TILES = (512, 512)
SB_TILES = (512, 256)
NEG = -1e30

ADAM_LR = 0.001
ADAM_B1 = 0.9
ADAM_B2 = 0.999
ADAM_EPS = 1e-08
ADAM_WD = 0.01
ADAM_STEP = 10

VMEM_LIMIT = 56 * 1024 * 1024

MESH = pl.DeviceIdType.MESH


def _cparams(sem=None):
    return pltpu.CompilerParams(dimension_semantics=sem, vmem_limit_bytes=VMEM_LIMIT)


def _relu2(t):
    t = t.astype(F32)
    return t * t


_DIMS = {"nn": (((1,), (0,)), ((), ())), "nt": (((1,), (1,)), ((), ())), "tn": (((0,), (0,)), ((), ()))}
NT_DIMS = _DIMS["nt"]
TN_DIMS = _DIMS["tn"]


def _mm(a, b, *, mode, name, out_dtype=F32, tm=512, tn=512, tk=512, add=None, a_fn=None, epi=None, extra=None,
        col_shards=False, behind=None, flat_out=False):
    if mode == "nn":
        (m, k), n = a.shape, b.shape[-1]
    elif mode == "nt":
        (m, k), n = a.shape, b.shape[-2]
    else:
        (k, m), n = a.shape, b.shape[1]
    shard = None
    if col_shards:
        if mode == "nn":
            shard, n = n, N_CHIPS * n
            tn = min(tn, shard)
        elif mode == "nt":
            shard = b.shape[-1]
            tk = min(tk, shard)
        else:
            shard = n // N_CHIPS
            tn = min(tn, shard)
    tm, tn, tk = min(tm, m), min(tn, n), min(tk, k)
    assert m % tm == 0 and n % tn == 0 and k % tk == 0, (name, m, n, k)
    nk = k // tk
    a_spec = {"nn": pl.BlockSpec((tm, tk), lambda i, j, kk: (i, kk)),
              "nt": pl.BlockSpec((tm, tk), lambda i, j, kk: (i, kk)),
              "tn": pl.BlockSpec((tk, tm), lambda i, j, kk: (kk, i))}[mode]
    b_spec = {"nn": pl.BlockSpec((tk, tn), lambda i, j, kk: (kk, j)),
              "nt": pl.BlockSpec((tn, tk), lambda i, j, kk: (j, kk)),
              "tn": pl.BlockSpec((tk, tn), lambda i, j, kk: (kk, j))}[mode]
    o_spec = pl.BlockSpec((tm, tn), lambda i, j, kk: (i, j))
    out_shape = (m, n)
    if col_shards and mode == "nn":
        per = shard // tn
        b_spec = pl.BlockSpec((None, tk, tn), lambda i, j, kk: (j // per, kk, j % per))
    elif col_shards and mode == "nt":
        per = shard // tk
        b_spec = pl.BlockSpec((None, tn, tk), lambda i, j, kk: (kk // per, j, kk % per))
    elif col_shards:
        assert add is None and extra is None
        per = shard // tn
        o_spec = pl.BlockSpec((None, tm, tn), lambda i, j, kk: (j // per, i, j % per))
        out_shape = (N_CHIPS, m, shard)
    if flat_out:
        assert mode == "tn" and tn == n == D_MODEL and not col_shards and add is None and extra is None
        chunks = D_MODEL // LANES
        o_spec = pl.BlockSpec((tm * chunks, LANES), lambda i, j, kk: (i, 0))
        out_shape = (m * chunks, LANES)
    operands, in_specs = [a, b], [a_spec, b_spec]
    third = add if add is not None else extra
    if third is not None:
        operands.append(third)
        in_specs.append(o_spec)

    def body(*refs):
        a_ref, b_ref = refs[0], refs[1]
        t_ref = refs[2] if third is not None else None
        o_ref = refs[3] if third is not None else refs[2]
        acc_ref = refs[-1] if nk > 1 else None
        at = a_ref[...]
        if a_fn is not None:
            at = a_fn(at)
        part = lax.dot_general(at.astype(BF16), b_ref[...].astype(BF16), _DIMS[mode], preferred_element_type=F32)

        def finish(acc):
            if epi is not None:
                acc = epi(acc, None if t_ref is None else t_ref[...])
            elif add is not None:
                acc = acc + t_ref[...].astype(F32)
            if flat_out:
                for q in range(D_MODEL // LANES):
                    o_ref[pl.ds(q, tm, stride=D_MODEL // LANES), :] = acc[:, q * LANES:(q + 1) * LANES].astype(o_ref.dtype)
                return
            o_ref[...] = acc.astype(o_ref.dtype)

        if nk == 1:
            finish(part)
        else:
            kk = pl.program_id(2)

            @pl.when(kk == 0)
            def _():
                acc_ref[...] = part

            @pl.when(kk > 0)
            def _():
                acc_ref[...] += part

            @pl.when(kk == nk - 1)
            def _():
                finish(acc_ref[...])

    call = dict(name=name, grid=(m // tm, n // tn, nk), in_specs=in_specs,
                scratch_shapes=[pltpu.VMEM((tm, tn), F32)] if nk > 1 else [])
    if behind is not None:
        (res,), exchanged = _call_behind(body, behind, out_specs=[o_spec], out_shape=[jax.ShapeDtypeStruct(out_shape, out_dtype)],
                                         operands=operands, **call)
        return res, exchanged
    return pl.pallas_call(body, out_specs=o_spec, out_shape=jax.ShapeDtypeStruct(out_shape, out_dtype),
                          compiler_params=_cparams(("parallel", "parallel", "arbitrary")), **call)(*operands)


ROW_TILE = 512


def _row_spec(width=D_MODEL, rows=ROW_TILE):
    return pl.BlockSpec((rows, width), lambda i: (i, 0))


def _vec_spec(rows=1, width=D_MODEL):
    return pl.BlockSpec((rows, width), lambda i: (0, 0))


def _xhat(x):
    r = lax.rsqrt(jnp.mean(x * x, axis=-1, keepdims=True) + EPS)
    return x * r, r


def _rms_bwd_rows(dh, x, g):
    xh, r = _xhat(x)
    dxh = dh * g
    dx = r * (dxh - xh * jnp.mean(dxh * xh, axis=-1, keepdims=True))
    return dx, jnp.sum(dh * xh, axis=0, keepdims=True)


def _norm_fwd(x, g, name):
    t = x.shape[0]

    def body(x_ref, g_ref, h_ref):
        xh, _ = _xhat(x_ref[...])
        h_ref[...] = (xh * g_ref[...]).astype(BF16)

    return pl.pallas_call(
        body, name=name, grid=(t // ROW_TILE,), in_specs=[_row_spec(), _vec_spec()], out_specs=_row_spec(),
        out_shape=jax.ShapeDtypeStruct((t, D_MODEL), BF16), compiler_params=_cparams(("parallel",)),
    )(x, g)


def _mm_res_norm(a, b, res, g, name, a_fn=None):
    t, k = a.shape

    def body(a_ref, b_ref, res_ref, g_ref, x_ref, h_ref):
        at = a_ref[...] if a_fn is None else a_fn(a_ref[...])
        x_new = res_ref[...] + _dot(at.astype(BF16), b_ref[...])
        x_ref[...] = x_new
        h_ref[...] = (_xhat(x_new)[0] * g_ref[...]).astype(BF16)

    return pl.pallas_call(
        body, name=name, grid=(t // ROW_TILE,),
        in_specs=[pl.BlockSpec((ROW_TILE, k), lambda i: (i, 0)), pl.BlockSpec(b.shape, lambda i: (0, 0)), _row_spec(), _vec_spec()],
        out_specs=[_row_spec(), _row_spec()],
        out_shape=[jax.ShapeDtypeStruct((t, D_MODEL), F32), jax.ShapeDtypeStruct((t, D_MODEL), BF16)],
        compiler_params=_cparams(("parallel",)),
    )(a, b, res, g)


def _mm_norm_bwd(pairs, dh_first, x, g, dres, name, behind=None):
    t = x.shape[0]
    operands, in_specs = [], []
    for a, b in pairs:
        if b.ndim == 3:
            for j in range(b.shape[0]):
                operands += [a, b]
                in_specs += [pl.BlockSpec((ROW_TILE, b.shape[2]), lambda i, j=j: (i, j)),
                             pl.BlockSpec((None, D_MODEL, b.shape[2]), lambda i, j=j: (j, 0, 0))]
        else:
            operands += [a, b]
            in_specs += [pl.BlockSpec((ROW_TILE, a.shape[1]), lambda i: (i, 0)), pl.BlockSpec(b.shape, lambda i: (0, 0))]
    n_mm = len(operands)
    operands += [x, g, dres] + ([] if dh_first is None else [dh_first])
    in_specs += [_row_spec(), _vec_spec(), _row_spec()] + ([] if dh_first is None else [_row_spec()])

    def body(*refs):
        x_ref, g_ref, dres_ref = refs[n_mm:n_mm + 3]
        dx_ref, dxb_ref, dg_ref = refs[-3:]
        dh = 0.0 if dh_first is None else refs[n_mm + 3][...]
        for k in range(0, n_mm, 2):
            dh = dh + lax.dot_general(refs[k][...].astype(BF16), refs[k + 1][...].astype(BF16), NT_DIMS,
                                      preferred_element_type=F32)
        dx, dg = _rms_bwd_rows(dh, x_ref[...], g_ref[...])
        dx = dx + dres_ref[...]
        dx_ref[...] = dx
        dxb_ref[...] = dx.astype(BF16)

        @pl.when(pl.program_id(0) == 0)
        def _():
            dg_ref[...] = jnp.zeros_like(dg_ref)

        dg_ref[...] += dg

    call = dict(name=name, grid=(t // ROW_TILE,), in_specs=in_specs, out_specs=[_row_spec(), _row_spec(), _vec_spec()],
                out_shape=[jax.ShapeDtypeStruct((t, D_MODEL), F32), jax.ShapeDtypeStruct((t, D_MODEL), BF16),
                           jax.ShapeDtypeStruct((1, D_MODEL), F32)])
    if behind is not None:
        return _call_behind(body, behind, scratch_shapes=[], operands=operands, **call)
    return pl.pallas_call(body, compiler_params=_cparams(("arbitrary",)), **call)(*operands)


def _shards_spec(w):
    return pl.BlockSpec(w.shape, lambda i: (0, 0, 0))


def _gate_fwd(gl, b_gate, o_fox, o_sb, w_fox, w_sb):
    t = o_fox.shape[0]

    def body(gla_ref, glb_ref, b_ref, ofox_ref, osb_ref, wf_ref, ws_ref, m_ref, of_ref, os_ref):
        of = jnp.concatenate([_dot(ofox_ref[...], wf_ref[j]) for j in range(N_CHIPS)], axis=1)
        os_ = jnp.concatenate([_dot(osb_ref[...], ws_ref[j]) for j in range(N_CHIPS)], axis=1)
        ga = jax.nn.sigmoid(gla_ref[...] + b_ref[0:1, :])
        gb = jax.nn.sigmoid(glb_ref[...] + b_ref[1:2, :])
        of_ref[...] = of
        os_ref[...] = os_
        m_ref[...] = (ga * of + gb * os_).astype(BF16)

    return pl.pallas_call(
        body, name="gate_fwd", grid=(t // ROW_TILE,),
        in_specs=[pl.BlockSpec((ROW_TILE, D_MODEL), lambda i: (i, 0)), pl.BlockSpec((ROW_TILE, D_MODEL), lambda i: (i, 1)),
                  _vec_spec(2), _row_spec(D_ATT), _row_spec(D_ATT), _shards_spec(w_fox), _shards_spec(w_sb)],
        out_specs=[_row_spec(), _row_spec(), _row_spec()],
        out_shape=[jax.ShapeDtypeStruct((t, D_MODEL), BF16)] + [jax.ShapeDtypeStruct((t, D_MODEL), F32)] * 2,
        compiler_params=_cparams(("parallel",)),
    )(gl, gl, b_gate, o_fox, o_sb, w_fox, w_sb)


def _gate_bwd(gl, b_gate, of, os_, dx, w_out, w_fox, w_sb):
    t = of.shape[0]
    shard = D_MODEL // N_CHIPS

    def back(d, w_ref):
        return sum(_dot(d[:, j * shard:(j + 1) * shard], w_ref[j], NT_DIMS) for j in range(N_CHIPS)).astype(BF16)

    def body(gla_ref, glb_ref, b_ref, of_ref, os_ref, dx_ref, w_ref, wf_ref, ws_ref,
             dof_ref, dos_ref, dgl_ref, db_ref, dofox_ref, dosb_ref):
        dm = _dot(dx_ref[...], w_ref[...], NT_DIMS)
        ga = jax.nn.sigmoid(gla_ref[...] + b_ref[0:1, :])
        gb = jax.nn.sigmoid(glb_ref[...] + b_ref[1:2, :])
        dof = (dm * ga).astype(BF16)
        dos = (dm * gb).astype(BF16)
        dof_ref[...] = dof
        dos_ref[...] = dos
        dofox_ref[...] = back(dof, wf_ref)
        dosb_ref[...] = back(dos, ws_ref)
        dgla = dm * of_ref[...] * ga * (1.0 - ga)
        dglb = dm * os_ref[...] * gb * (1.0 - gb)
        dgl_ref[:, 0:D_MODEL] = dgla.astype(BF16)
        dgl_ref[:, D_MODEL:2 * D_MODEL] = dglb.astype(BF16)

        @pl.when(pl.program_id(0) == 0)
        def _():
            db_ref[...] = jnp.zeros_like(db_ref)

        db_ref[0:1, :] += jnp.sum(dgla, axis=0, keepdims=True)
        db_ref[1:2, :] += jnp.sum(dglb, axis=0, keepdims=True)

    outs = pl.pallas_call(
        body, name="gate_bwd", grid=(t // ROW_TILE,),
        in_specs=[pl.BlockSpec((ROW_TILE, D_MODEL), lambda i: (i, 0)), pl.BlockSpec((ROW_TILE, D_MODEL), lambda i: (i, 1)),
                  _vec_spec(2), _row_spec(), _row_spec(), _row_spec(), pl.BlockSpec(w_out.shape, lambda i: (0, 0)),
                  _shards_spec(w_fox), _shards_spec(w_sb)],
        out_specs=[_row_spec(), _row_spec(), _row_spec(2 * D_MODEL), _vec_spec(2), _row_spec(D_ATT), _row_spec(D_ATT)],
        out_shape=[jax.ShapeDtypeStruct((t, D_MODEL), BF16)] * 2 + [jax.ShapeDtypeStruct((t, 2 * D_MODEL), BF16),
                                                                      jax.ShapeDtypeStruct((2, D_MODEL), F32)]
        + [jax.ShapeDtypeStruct((t, D_ATT), BF16)] * 2,
        compiler_params=_cparams(("arbitrary",)),
    )(gl, gl, b_gate, of, os_, dx, w_out, w_fox, w_sb)
    return outs


def _head_and_loss(x2, h3, p, w_gate, w_ple, g_final, target):
    t = x2.shape[0]

    def body(x2_ref, h3_ref, p_ref, wg_ref, wp_ref, g_ref, tgt_ref, dx3_ref, dpre_ref, dpe_ref, dg_ref, loss_ref):
        gp = jax.nn.sigmoid(_dot(h3_ref[...], wg_ref[...]))
        p_t = p_ref[...].astype(BF16)
        pe_t = jnp.concatenate([_dot(p_t, wp_ref[j]) for j in range(N_CHIPS)], axis=1)
        x3 = x2_ref[...] + gp * pe_t
        g = g_ref[...]
        xh, _ = _xhat(x3)
        err = xh * g - tgt_ref[...]
        dy = err * (1.0 / D_MODEL)
        dx3, dg = _rms_bwd_rows(dy, x3, g)
        dx3_ref[...] = dx3
        dpre_ref[...] = (dx3 * pe_t * gp * (1.0 - gp)).astype(BF16)
        dpe_ref[...] = (dx3 * gp).astype(BF16)

        @pl.when(pl.program_id(0) == 0)
        def _():
            dg_ref[...] = jnp.zeros_like(dg_ref)
            loss_ref[...] = jnp.zeros_like(loss_ref)

        dg_ref[...] += dg
        loss_ref[...] += 0.5 * jnp.sum(jnp.mean(err * err, axis=-1, keepdims=True), axis=0, keepdims=True)

    return pl.pallas_call(
        body, name="head_and_loss", grid=(t // ROW_TILE,),
        in_specs=[_row_spec(), _row_spec(), _row_spec(D_PLE), pl.BlockSpec(w_gate.shape, lambda i: (0, 0)),
                  pl.BlockSpec(w_ple.shape, lambda i: (0, 0, 0)), _vec_spec(), _row_spec()],
        out_specs=[_row_spec(), _row_spec(), _row_spec(), _vec_spec(), _vec_spec(1, LANES)],
        out_shape=[jax.ShapeDtypeStruct((t, D_MODEL), F32), jax.ShapeDtypeStruct((t, D_MODEL), BF16),
                   jax.ShapeDtypeStruct((t, D_MODEL), BF16), jax.ShapeDtypeStruct((1, D_MODEL), F32),
                   jax.ShapeDtypeStruct((1, LANES), F32)],
        compiler_params=_cparams(("arbitrary",)),
    )(x2, h3, p, w_gate, w_ple, g_final, target)


def _split3(v):
    hi = v.astype(BF16)
    r1 = v - hi.astype(F32)
    mid = r1.astype(BF16)
    lo = (r1 - mid.astype(F32)).astype(BF16)
    return hi, mid, lo


def _split2(v):
    hi = v.astype(BF16)
    return jnp.concatenate([hi, (v - hi.astype(F32)).astype(BF16)], axis=1)


def _dot(a, b, dims=_DIMS["nn"]):
    return lax.dot_general(a, b, dims, preferred_element_type=F32)


def _tri(n, rel):
    row = lax.broadcasted_iota(jnp.int32, (n, n), 0)
    col = lax.broadcasted_iota(jnp.int32, (n, n), 1)
    return rel(row, col).astype(BF16)


def _tri2(n, rel):
    t = _tri(n, rel)
    return jnp.concatenate([t, t], axis=0)


def _log_sigmoid(v):
    return -(jnp.maximum(-v, 0.0) + jnp.log(1.0 + jnp.exp(-jnp.abs(v))))


def _fox_prep(fl, b_forget, batch, seq):
    nb = seq // ATT_BLOCK

    def body(fl_ref, b_ref, cw_ref, cr_ref):
        col = lax.broadcasted_iota(jnp.int32, (ATT_BLOCK, F_PAD), 1)
        lower = _tri(ATT_BLOCK, lambda r, c: c <= r)
        upper = _tri(ATT_BLOCK, lambda r, c: r <= c)
        expand = (lax.broadcasted_iota(jnp.int32, (F_PAD, D_ATT), 1) // HEAD_DIM
                  == lax.broadcasted_iota(jnp.int32, (F_PAD, D_ATT), 0)).astype(BF16)
        carry_w = jnp.zeros((1, D_ATT), F32)
        carry_r = jnp.zeros((F_PAD, 1), F32)
        for i in range(nb):
            blk = slice(i * ATT_BLOCK, (i + 1) * ATT_BLOCK)
            logf = jnp.where(col < N_HEADS, _log_sigmoid(fl_ref[blk, :] + b_ref[...]), 0.0)
            cw = jnp.zeros((ATT_BLOCK, D_ATT), F32) + carry_w
            cr = jnp.zeros((F_PAD, ATT_BLOCK), F32) + carry_r
            for part in _split3(logf):
                cw += _dot(lower, _dot(part, expand).astype(BF16))
                cr += _dot(part, upper, TN_DIMS)
            cw_ref[blk, :] = cw
            cr_ref[:, blk] = cr[0:N_HEADS, :]
            carry_w = cw[ATT_BLOCK - 1:ATT_BLOCK, :]
            carry_r = cr[:, ATT_BLOCK - 1:ATT_BLOCK]

    return pl.pallas_call(
        body, name="fox_prep", grid=(batch,),
        in_specs=[pl.BlockSpec((seq, F_PAD), lambda b: (b, 0)), pl.BlockSpec((1, F_PAD), lambda b: (0, 0))],
        out_specs=[pl.BlockSpec((seq, D_ATT), lambda b: (b, 0)), pl.BlockSpec((N_HEADS, seq), lambda b: (b, 0))],
        out_shape=[jax.ShapeDtypeStruct((batch * seq, D_ATT), F32), jax.ShapeDtypeStruct((batch * N_HEADS, seq), F32)],
        compiler_params=_cparams(("parallel",)),
    )(fl, b_forget)


def _fox_post(dcs_wide, drs_wide, fl, b_forget, batch, seq):
    nb = seq // ATT_BLOCK

    def body(dcs_ref, drs_ref, fl_ref, b_ref, dfl_ref, db_ref):
        pick = (lax.broadcasted_iota(jnp.int32, (D_ATT, F_PAD), 0)
                == lax.broadcasted_iota(jnp.int32, (D_ATT, F_PAD), 1) * HEAD_DIM).astype(BF16)
        upper = _tri(ATT_BLOCK, lambda r, c: r <= c)
        col = lax.broadcasted_iota(jnp.int32, (ATT_BLOCK, F_PAD), 1)

        @pl.when(pl.program_id(0) == 0)
        def _():
            db_ref[...] = jnp.zeros_like(db_ref)

        carry = jnp.zeros((1, F_PAD), F32)
        for i in reversed(range(nb)):
            blk = slice(i * ATT_BLOCK, (i + 1) * ATT_BLOCK)
            narrow = jnp.zeros((ATT_BLOCK, F_PAD), F32)
            for part in _split3(drs_ref[blk, :] - dcs_ref[blk, :]):
                narrow += _dot(part, pick)
            after = jnp.zeros((ATT_BLOCK, F_PAD), F32) + carry
            for part in _split3(narrow):
                after += _dot(upper, part)
            carry = after[0:1, :]
            pre = fl_ref[blk, :] + b_ref[...]
            dfl = jnp.where(col < N_HEADS, after * jax.nn.sigmoid(-pre), 0.0)
            dfl_ref[blk, :] = dfl.astype(BF16)
            db_ref[...] += jnp.sum(dfl, axis=0, keepdims=True)

    return pl.pallas_call(
        body, name="fox_post", grid=(batch,),
        in_specs=[pl.BlockSpec((seq, D_ATT), lambda b: (b, 0)), pl.BlockSpec((seq, D_ATT), lambda b: (b, 0)),
                  pl.BlockSpec((seq, F_PAD), lambda b: (b, 0)), pl.BlockSpec((1, F_PAD), lambda b: (0, 0))],
        out_specs=[pl.BlockSpec((seq, F_PAD), lambda b: (b, 0)), pl.BlockSpec((1, F_PAD), lambda b: (0, 0))],
        out_shape=[jax.ShapeDtypeStruct((batch * seq, F_PAD), BF16), jax.ShapeDtypeStruct((1, F_PAD), F32)],
        compiler_params=_cparams(("arbitrary",)),
    )(dcs_wide, drs_wide, fl, b_forget)


N_PAIRS = N_HEADS // 2


def _att_specs(seq, col0, tq):
    nq = seq // tq
    q = pl.BlockSpec((tq, LANES), lambda b, hp, qi: (b * nq + qi, col0 + hp))
    k = pl.BlockSpec((seq, LANES), lambda b, hp, qi: (b, col0 + N_PAIRS + hp))
    v = pl.BlockSpec((seq, LANES), lambda b, hp, qi: (b, col0 + 2 * N_PAIRS + hp))
    return q, k, v


def _qblock_spec(seq, tq):
    nq = seq // tq
    return pl.BlockSpec((tq, LANES), lambda b, hp, qi: (b * nq + qi, hp))


def _kv_out_spec(seq):
    return pl.BlockSpec((seq, LANES), lambda b, hp, qi: (b, hp))


def _head_masks():
    lane = lax.broadcasted_iota(jnp.int32, (1, LANES), 1)
    return [(lane >= HEAD_DIM * j) & (lane < HEAD_DIM * (j + 1)) for j in range(2)]


def _stack_heads(t, masks):
    zero = jnp.zeros_like(t)
    return jnp.concatenate([jnp.where(masks[0], t, zero), jnp.where(masks[1], t, zero)], axis=0)


def _stack_cols(t):
    return jnp.concatenate([t[:, 0:1], t[:, HEAD_DIM:HEAD_DIM + 1]], axis=0)


def _unstack(t2, masks):
    tq = t2.shape[0] // 2
    return jnp.where(masks[0], t2[:tq], t2[tq:])


def _stacked_ids(tq, tk):
    row = lax.broadcasted_iota(jnp.int32, (2 * tq, tk), 0)
    col = lax.broadcasted_iota(jnp.int32, (2 * tq, tk), 1)
    first = lax.broadcasted_iota(jnp.int32, (2 * tq, 1), 0) < tq
    return col - jnp.where(row < tq, row, row - tq), first


def _sweep(qi, tq, tk, step, init, leftward):
    per = tq // tk
    whole = lambda carry: lax.fori_loop(0, per * qi, lambda i, c: step(per * qi - 1 - i if leftward else i, c, None), carry)
    crossed = [(per * qi + j, -j * tk) for j in range(per)]
    if leftward:
        carry = init
        for kb, lead in reversed(crossed):
            carry = step(kb, carry, lead)
        return whole(carry)
    carry = whole(init)
    for kb, lead in crossed:
        carry = step(kb, carry, lead)
    return carry


def _att_fwd(qkv, c_wide, c_row, batch, seq, behind):
    tq, tkf = FOX_TILES
    tqs, tks = SB_TILES
    assert tq == tqs and tkf == 2 * tks
    nq = seq // tq

    def body(qa_ref, ka_ref, va_ref, cw_ref, cr_ref, qb_ref, kb_ref, vb_ref, of_ref, lse_ref, os_ref, rt_ref):
        hp, qi = pl.program_id(1), pl.program_id(2)
        masks = _head_masks()
        ahead_f, first = _stacked_ids(tq, tkf)
        ahead_s, _ = _stacked_ids(tq, tks)
        later = _tri2(tks, lambda r, c: r > c)
        q2f = _stack_heads(qa_ref[...], masks) * SCALE
        q2s = _stack_heads(qb_ref[...], masks) * SCALE
        ct = _stack_cols(cw_ref[...])

        def fox(kb, carry, lead):
            m, l, acc = carry
            k0 = pl.multiple_of(kb * tkf, tkf)
            cs = jnp.where(first, cr_ref[pl.ds(2 * hp, 1), pl.ds(k0, tkf)], cr_ref[pl.ds(2 * hp + 1, 1), pl.ds(k0, tkf)])
            s = _dot(q2f, ka_ref[pl.ds(k0, tkf), :], NT_DIMS) + ct - cs
            if lead is not None:
                s = jnp.where(ahead_f <= lead, s, NEG)
            m_new = jnp.maximum(m, jnp.max(s, axis=1, keepdims=True))
            p = jnp.exp(s - m_new)
            alpha = jnp.exp(m - m_new)
            l = alpha * l + jnp.sum(p, axis=1, keepdims=True)
            acc = alpha * acc + _dot(p.astype(BF16), va_ref[pl.ds(k0, tkf), :])
            return m_new, l, acc

        def sb(kb, carry, lead):
            run, acc = carry
            k0 = pl.multiple_of(kb * tks, tks)
            ls, lsn = _sb_logits(q2s, kb_ref[pl.ds(k0, tks), :])
            if lead is not None:
                lsn = jnp.where(ahead_s < lead, lsn, 0.0)
            w = jnp.exp(ls + _dot(_split2(lsn), later) + run)
            if lead is not None:
                w = jnp.where(ahead_s < lead, w, 0.0)
            return run + jnp.sum(lsn, axis=1, keepdims=True), acc + _dot(w.astype(BF16), vb_ref[pl.ds(k0, tks), :])

        fox_c = (jnp.full((2 * tq, 1), NEG, F32), jnp.zeros((2 * tq, 1), F32), jnp.zeros((2 * tq, LANES), F32))
        sb_c = (jnp.zeros((2 * tq, 1), F32), jnp.zeros((2 * tq, LANES), F32))
        sb_c = sb(2 * qi, sb(2 * qi + 1, sb_c, -tks), 0)

        def both(i, carries):
            fox_c, sb_c = carries
            return fox(i, fox_c, None), sb(2 * qi - 2 - 2 * i, sb(2 * qi - 1 - 2 * i, sb_c, None), None)

        fox_c, (run, acc_s) = lax.fori_loop(0, qi, both, (fox_c, sb_c))
        m, l, acc = fox(qi, fox_c, 0)
        of_ref[...] = _unstack(acc / l, masks).astype(BF16)
        lse_ref[...] = _unstack(m + jnp.log(l), masks)
        os_ref[...] = _unstack(acc_s, masks).astype(BF16)
        rt_ref[...] = _unstack(run, masks)

    qa, ka, va = _att_specs(seq, 0, tq)
    qb_, kb_, vb_ = _att_specs(seq, 3 * N_PAIRS, tq)
    qb = _qblock_spec(seq, tq)
    half, wide = jax.ShapeDtypeStruct((batch * seq, D_ATT), BF16), jax.ShapeDtypeStruct((batch * seq, D_ATT), F32)
    return _call_behind(
        body, behind, name="att_fwd", grid=(batch, N_PAIRS, nq),
        in_specs=[qa, ka, va, qb, pl.BlockSpec((N_HEADS, seq), lambda b, hp, qi: (b, 0)), qb_, kb_, vb_],
        out_specs=[qb, qb, qb, qb], out_shape=[half, wide, half, wide], scratch_shapes=[],
        operands=(qkv, qkv, qkv, c_wide, c_row, qkv, qkv, qkv))


def _fox_bwd(qkv, c_wide, c_row, o, do, lse_wide, batch, seq, behind):
    tq, tk = FOX_TILES
    nq = seq // tq

    def body(q_ref, k_ref, v_ref, cw_ref, cr_ref, o_ref, do_ref, lse_ref,
             dq_ref, dk_ref, dv_ref, dcs_ref, drs_ref, dkc_acc, dv_acc):
        hp, qi = pl.program_id(1), pl.program_id(2)

        @pl.when(qi == 0)
        def _():
            dkc_acc[...] = jnp.zeros_like(dkc_acc)
            dv_acc[...] = jnp.zeros_like(dv_acc)

        masks = _head_masks()
        ahead, first = _stacked_ids(tq, tk)
        q_t, do_t = q_ref[...], do_ref[...]
        q2 = _stack_heads(q_t, masks) * SCALE
        do2 = _stack_heads(do_t, masks)
        q_and_ones = jnp.concatenate([q2, _stack_heads(jnp.ones_like(q_t), masks)], axis=1)
        ct = _stack_cols(cw_ref[...])
        lse = _stack_cols(lse_ref[...])
        prod = do_t.astype(F32) * o_ref[...].astype(F32)
        delta = jnp.concatenate([jnp.sum(jnp.where(mk, prod, 0.0), axis=1, keepdims=True) for mk in masks], axis=0)

        def step(kb, carry, lead):
            dq_acc, rs = carry
            k0 = pl.multiple_of(kb * tk, tk)
            kblk = k_ref[pl.ds(k0, tk), :]
            cs = jnp.where(first, cr_ref[pl.ds(2 * hp, 1), pl.ds(k0, tk)], cr_ref[pl.ds(2 * hp + 1, 1), pl.ds(k0, tk)])
            p = jnp.exp(_dot(q2, kblk, NT_DIMS) + ct - cs - lse)
            if lead is not None:
                p = jnp.where(ahead <= lead, p, 0.0)
            dp = _dot(do2, v_ref[pl.ds(k0, tk), :], NT_DIMS)
            ds = (p * (dp - delta)).astype(BF16)
            dkc_acc[pl.ds(k0, tk), :] += _dot(ds, q_and_ones, TN_DIMS)
            dv_acc[pl.ds(k0, tk), :] += _dot(p.astype(BF16), do2, TN_DIMS)
            return dq_acc + _dot(ds, kblk), rs + jnp.sum(ds.astype(F32), axis=1, keepdims=True)

        init = (jnp.zeros((2 * tq, LANES), F32), jnp.zeros((2 * tq, 1), F32))
        dq_acc, rs = _sweep(qi, tq, tk, step, init, leftward=False)
        dq_ref[...] = (_unstack(dq_acc, masks) * SCALE).astype(BF16)
        drs_ref[...] = _unstack(rs, masks)

        @pl.when(qi == nq - 1)
        def _():
            dk_ref[...] = dkc_acc[:, 0:LANES].astype(BF16)
            dcs_ref[...] = dkc_acc[:, LANES:2 * LANES]
            dv_ref[...] = dv_acc[...].astype(BF16)

    q_spec, k_spec, v_spec = _att_specs(seq, 0, tq)
    qb = _qblock_spec(seq, tq)
    return _call_behind(
        body, behind, name="fox_bwd", grid=(batch, N_PAIRS, nq),
        in_specs=[q_spec, k_spec, v_spec, qb, pl.BlockSpec((N_HEADS, seq), lambda b, hp, qi: (b, 0)), qb, qb, qb],
        out_specs=[qb, _kv_out_spec(seq), _kv_out_spec(seq), _kv_out_spec(seq), qb],
        out_shape=[jax.ShapeDtypeStruct((batch * seq, D_ATT), BF16)] * 3 + [jax.ShapeDtypeStruct((batch * seq, D_ATT), F32)] * 2,
        scratch_shapes=[pltpu.VMEM((seq, 2 * LANES), F32), pltpu.VMEM((seq, LANES), F32)],
        operands=(qkv, qkv, qkv, c_wide, c_row, o, do, lse_wide))


def _sb_logits(q2, kblk):
    z = _dot(q2, kblk, NT_DIMS)
    lsn = jnp.minimum(-z, 0.0) - jnp.log(1.0 + jnp.exp(-jnp.abs(z)))
    return lsn + z, lsn


def _sb_bwd(qkv, do, rt_wide, batch, seq, behind):
    tq, tk = SB_TILES
    nq = seq // tq

    def body(q_ref, k_ref, v_ref, do_ref, rt_ref, dq_ref, dk_ref, dv_ref, dk_acc, dv_acc):
        qi = pl.program_id(2)

        @pl.when(qi == 0)
        def _():
            dk_acc[...] = jnp.zeros_like(dk_acc)
            dv_acc[...] = jnp.zeros_like(dv_acc)

        masks = _head_masks()
        ahead, _ = _stacked_ids(tq, tk)
        later = _tri2(tk, lambda r, c: r > c)
        earlier = _tri(tk, lambda r, c: r < c)
        q2 = _stack_heads(q_ref[...], masks) * SCALE
        do2 = _stack_heads(do_ref[...], masks)
        total = _stack_cols(rt_ref[...])

        def step(kb, carry, lead):
            pref, epre, dq_acc = carry
            k0 = pl.multiple_of(kb * tk, tk)
            kblk = k_ref[pl.ds(k0, tk), :]
            ls, lsn_all = _sb_logits(q2, kblk)
            lsn = lsn_all if lead is None else jnp.where(ahead < lead, lsn_all, 0.0)
            rs = jnp.sum(lsn, axis=1, keepdims=True)
            w = jnp.exp(ls + _dot(_split2(lsn), later) + (total - pref - rs))
            if lead is not None:
                w = jnp.where(ahead < lead, w, 0.0)
            e = w * _dot(do2, v_ref[pl.ds(k0, tk), :], NT_DIMS)
            before = _dot(e.astype(BF16), earlier) + epre
            dz = e * jnp.exp(lsn_all) - jnp.exp(ls) * before
            if lead is not None:
                dz = jnp.where(ahead < lead, dz, 0.0)
            dz = dz.astype(BF16)
            dk_acc[pl.ds(k0, tk), :] += _dot(dz, q2, TN_DIMS)
            dv_acc[pl.ds(k0, tk), :] += _dot(w.astype(BF16), do2, TN_DIMS)
            return pref + rs, epre + jnp.sum(e, axis=1, keepdims=True), dq_acc + _dot(dz, kblk)

        init = (jnp.zeros((2 * tq, 1), F32), jnp.zeros((2 * tq, 1), F32), jnp.zeros((2 * tq, LANES), F32))
        dq_acc = _sweep(qi, tq, tk, step, init, leftward=False)[2]
        dq_ref[...] = (_unstack(dq_acc, masks) * SCALE).astype(BF16)

        @pl.when(qi == nq - 1)
        def _():
            dk_ref[...] = dk_acc[...].astype(BF16)
            dv_ref[...] = dv_acc[...].astype(BF16)

    q_spec, k_spec, v_spec = _att_specs(seq, 3 * N_PAIRS, tq)
    qb = _qblock_spec(seq, tq)
    return _call_behind(
        body, behind, name="sb_bwd", grid=(batch, N_PAIRS, nq), in_specs=[q_spec, k_spec, v_spec, qb, qb],
        out_specs=[qb, _kv_out_spec(seq), _kv_out_spec(seq)], out_shape=[jax.ShapeDtypeStruct((batch * seq, D_ATT), BF16)] * 3,
        scratch_shapes=[pltpu.VMEM((seq, LANES), F32), pltpu.VMEM((seq, LANES), F32)], operands=(qkv, qkv, qkv, do, rt_wide))


def _local_step(x, p, target, w, rest, vec, place):
    batch, seq, _ = x.shape
    t = batch * seq
    x = x.reshape(t, D_MODEL)
    target = target.reshape(t, D_MODEL)
    p = p.reshape(t, D_PLE)
    big = dict(tm=1024, tn=1024, tk=1024)

    h1 = _norm_fwd(x, vec["g_mix"], "norm_mix")
    qkv = _mm(h1, w["qkv"], mode="nn", name="proj_qkv", out_dtype=BF16, **big)
    gl = _mm(h1, w["gate"], mode="nn", name="proj_gate", **big)
    fl = _mm(h1, w["forget"], mode="nn", name="proj_forget", **big)
    c_wide, c_row = _fox_prep(fl, vec["b_forget"], batch, seq)
    (o_fox, lse_wide, o_sb, rt_wide), gathered = _att_fwd(qkv, c_wide, c_row, batch, seq, rest)
    w = dict(w, **_rest_weights(dict(zip(EARLY + ("b_gate",), gathered))))
    merged, of, os_ = _gate_fwd(gl, w["b_gate"], o_fox, o_sb, w["branch_fox"], w["branch_sb"])
    x1, h2 = _mm_res_norm(merged, w["out"], x, vec["g_mlp"], "proj_out_norm")
    ar = _mm(h2, w["up"], mode="nn", name="mlp_up", out_dtype=BF16, epi=lambda acc, _: jnp.maximum(acc, 0.0),
             col_shards=True, **big)
    x2, h3 = _mm_res_norm(ar, w["down"], x1, vec["g_ple"], "mlp_down_norm", a_fn=_relu2)

    dx3, dpre, dpe, dg_final, loss = _head_and_loss(x2, h3, p, w["ple_gate"], w["ple"], vec["g_final"], target)
    gw = {}
    gw["ple"] = _mm(p, dpe, mode="tn", name="d_w_ple", col_shards=True, **big)
    gw["ple_gate"] = _mm(h3, dpre, mode="tn", name="d_w_ple_gate", **big)
    dx2, dx2b, dg_ple = _mm_norm_bwd([(dpre, w["ple_gate"])], None, x2, vec["g_ple"], dx3, "d_h_ple_norm_bwd")
    gw["down"] = _mm(ar, dx2b, mode="tn", name="d_w_down", a_fn=_relu2, **big)
    da = _mm(dx2b, w["down"], mode="nt", name="d_act", out_dtype=BF16,
             epi=lambda acc, r: acc * (2.0 * r.astype(F32)), extra=ar, **big)
    gw["up"] = _mm(h2, da, mode="tn", name="d_w_up", col_shards=True, **big)
    dx1, dx1b, dg_mlp = _mm_norm_bwd([(da, w["up"])], None, x1, vec["g_mlp"], dx2, "d_h_mlp_norm_bwd")
    gw["out"] = _mm(merged, dx1b, mode="tn", name="d_w_out", **big)
    dof, dos, dgl, gw["b_gate"], do_fox, do_sb = _gate_bwd(gl, w["b_gate"], of, os_, dx1b, w["out"], w["branch_fox"],
                                                                  w["branch_sb"])
    gw["branch_fox"] = _mm(o_fox, dof, mode="tn", name="d_w_branch_fox", col_shards=True, **big)
    gw["branch_sb"] = _mm(o_sb, dos, mode="tn", name="d_w_branch_sb", col_shards=True, **big)
    early = _early_slots(gw)
    early = [early[n] for n in EARLY]
    (dq_a, dk_a, dv_a, dcs_wide, drs_wide), received = _fox_bwd(qkv, c_wide, c_row, o_fox, do_fox, lse_wide, batch, seq,
                                                                _swap_halves(early))
    sums = _sum_sibling(place, early, received, "sum_sibling_early")
    (dq_b, dk_b, dv_b), others = _sb_bwd(qkv, do_sb, rt_wide, batch, seq, _exchange_chips(sums))
    mine = [None] * len(EARLY)
    for group, tag in ((BIG, "big"), (SMALL, "small")):
        for t, res in zip(group, _sum_chips(place, *[[a[t] for t in group] for a in (early, received, others)], "sum_chips_" + tag)):
            mine[t] = res
    dfl, db_forget = _fox_post(dcs_wide, drs_wide, fl, vec["b_forget"], batch, seq)
    dqkv = jnp.concatenate([dq_a, dk_a, dv_a, dq_b, dk_b, dv_b], axis=1)
    gw["qkv"], theirs = _mm(dqkv, h1, mode="tn", name="d_w_qkv", behind=_share_halves(mine), flat_out=True, **big)
    reduced = dict(zip(EARLY, zip(mine, theirs)))
    gw["gate"] = _mm(dgl, h1, mode="tn", name="d_w_gate", flat_out=True, **big)
    gw["forget"] = _mm(dfl, h1, mode="tn", name="d_w_forget", flat_out=True, **big)
    late = [_w_in_slots(gw)]
    dh1, received = _mm(dqkv, w["qkv"], mode="nt", name="d_h_qkv", behind=_swap_halves(late), **big)
    sums = _sum_sibling(place, late, received, "sum_sibling_w_in")
    (grad_x, _, dg_mix), others = _mm_norm_bwd([(dgl, w["gate"]), (dfl, w["forget"])], dh1, x, vec["g_mix"], dx1,
                                               "d_h_gate_norm_bwd", behind=_exchange_chips(sums))
    mine = _sum_chips(place, late, received, others, "sum_chips_w_in")
    reduced["w_in"] = (mine[0], _run_exchange(_share_halves(mine), "reduce_share_w_in")[0])
    gvec = {"g_mix": dg_mix, "b_forget": db_forget[:, 0:N_HEADS], "g_mlp": dg_mlp, "g_ple": dg_ple,
            "g_final": dg_final, "b_gate": gw["b_gate"]}
    return loss, grad_x.reshape(batch, seq, D_MODEL), reduced, gvec


ANY = pl.BlockSpec(memory_space=pl.ANY)
SHARDED = ("w_in", "w_branch_fox", "w_branch_sb", "w_out", "w_up", "w_down", "w_ple_gate", "w_ple")
ROW_ALIGN = 16
F32_ROWS = 8


def _place():
    return lax.axis_index("x"), lax.axis_index("y"), lax.axis_index("c")


def _other_chips(x, y):
    return [(1 - x, y), (x, 1 - y), (1 - x, 1 - y)]


def _half(ref, h):
    r = ref.shape[0] // 2
    assert r % ROW_ALIGN == 0
    return ref.at[pl.ds(pl.multiple_of(h * r, ROW_ALIGN), r)]


def _remote(src, dst, sems, idx, to):
    send_sems, recv_sems = sems
    return pltpu.make_async_remote_copy(src_ref=src, dst_ref=dst, send_sem=send_sems.at[idx], recv_sem=recv_sems.at[idx],
                                        device_id=to, device_id_type=MESH)


class _Exchange:
    def __init__(self, operands, out_shapes, sem_shape, start, finish):
        self.operands, self.out_shapes, self.sem_shape, self.start, self.finish = operands, out_shapes, sem_shape, start, finish

    def scratch(self):
        return [pltpu.SemaphoreType.DMA(self.sem_shape), pltpu.SemaphoreType.DMA(self.sem_shape)]


def _run_exchange(ex, name):
    n = len(ex.operands)

    def body(*refs):
        ex.start(refs[:n], refs[n:2 * n], refs[2 * n:])
        ex.finish(refs[:n], refs[n:2 * n], refs[2 * n:])

    return pl.pallas_call(body, name=name, in_specs=[ANY] * n, out_specs=[ANY] * n, out_shape=ex.out_shapes,
                          scratch_shapes=ex.scratch())(*ex.operands)


def _call_behind(body, ex, *, name, grid, in_specs, out_specs, out_shape, scratch_shapes, operands):
    n_in, n_out, nx = len(in_specs), len(out_specs), len(ex.operands)

    def wrapped(*refs):
        ins, x_in = refs[:n_in], refs[n_in:n_in + nx]
        outs, x_out = refs[n_in + nx:n_in + nx + n_out], refs[n_in + nx + n_out:n_in + 2 * nx + n_out]
        scratch, sems = refs[n_in + 2 * nx + n_out:-2], refs[-2:]
        first, last = None, None
        for d, steps in enumerate(grid):
            at_start, at_end = pl.program_id(d) == 0, pl.program_id(d) == steps - 1
            first = at_start if first is None else first & at_start
            last = at_end if last is None else last & at_end

        @pl.when(first)
        def _():
            ex.start(x_in, x_out, sems)

        body(*ins, *outs, *scratch)

        @pl.when(last)
        def _():
            ex.finish(x_in, x_out, sems)

    res = pl.pallas_call(
        wrapped, name=name, grid=grid, in_specs=list(in_specs) + [ANY] * nx, out_specs=list(out_specs) + [ANY] * nx,
        out_shape=list(out_shape) + list(ex.out_shapes), scratch_shapes=list(scratch_shapes) + ex.scratch(),
        compiler_params=_cparams(("arbitrary",) * len(grid)),
    )(*operands, *ex.operands)
    return res[:n_out], res[n_out:]


def _gather_weights(shards):
    n = len(shards)

    def first_copies(src, out, sems):
        x, y, c = _place()
        me = 2 * x + y
        copies = [_remote(_half(src[t], c), _half(out[t].at[me], c), sems, (t, k), (px, py, c))
                  for t in range(n) for k, (px, py) in enumerate(_other_chips(x, y))]
        return copies + [_remote(src[t], out[t].at[me], sems, (t, 3), (x, y, 1 - c)) for t in range(n)]

    def start(src, out, sems):
        for cp in first_copies(src, out, sems):
            cp.start()

    def finish(src, out, sems):
        x, y, c = _place()
        me = 2 * x + y
        sibling = (x, y, 1 - c)
        chips = _other_chips(x, y)
        passes = []
        for t in range(n):
            for k, (px, py) in enumerate(chips):
                landed = _half(out[t].at[2 * px + py], c)
                _remote(landed, landed, sems, (t, k), (px, py, c)).wait_recv()
                passes.append(_remote(landed, landed, sems, (t, 4 + k), sibling))
                passes[-1].start()
        for t in range(n):
            _remote(src[t], out[t].at[me], sems, (t, 3), sibling).wait_recv()
            for k, (px, py) in enumerate(chips):
                passed = _half(out[t].at[2 * px + py], 1 - c)
                _remote(passed, passed, sems, (t, 4 + k), sibling).wait_recv()
        for cp in first_copies(src, out, sems) + passes:
            cp.wait_send()

    return _Exchange(shards, [jax.ShapeDtypeStruct((N_CHIPS,) + s.shape, s.dtype) for s in shards], (n, 7), start, finish)


def _simple_exchange(operands, out_shapes, copies):
    def start(src, out, sems):
        for cp in copies(src, out, sems):
            cp.start()

    def finish(src, out, sems):
        for cp in copies(src, out, sems):
            cp.wait_recv()
        for cp in copies(src, out, sems):
            cp.wait_send()

    return _Exchange(operands, out_shapes, (len(operands),), start, finish)


def _swap_halves(slots):
    def copies(src, out, sems):
        x, y, c = _place()
        res = []
        for t in range(len(slots)):
            r = src[t].shape[1] // 2
            rows = pl.ds(pl.multiple_of((1 - c) * r, F32_ROWS), r)
            res.append(_remote(src[t].at[:, rows], out[t], sems, t, (x, y, 1 - c)))
        return res

    return _simple_exchange(slots, [jax.ShapeDtypeStruct((N_CHIPS, s.shape[1] // 2, s.shape[2]), s.dtype) for s in slots], copies)


def _exchange_chips(sums):
    n = len(sums)

    def copies(src, out, sems):
        x, y, c = _place()
        return [_remote(src[t].at[2 * px + py], out[t].at[k], sems, (t, k), (px, py, c))
                for t in range(n) for k, (px, py) in enumerate(_other_chips(x, y))]

    def start(src, out, sems):
        for cp in copies(src, out, sems):
            cp.start()

    def finish(src, out, sems):
        for cp in copies(src, out, sems):
            cp.wait_recv()
        for cp in copies(src, out, sems):
            cp.wait_send()

    return _Exchange(sums, [jax.ShapeDtypeStruct((3,) + s.shape[1:], s.dtype) for s in sums], (n, 3), start, finish)


def _share_halves(mine):
    def copies(src, out, sems):
        x, y, c = _place()
        return [_remote(src[t], out[t], sems, t, (x, y, 1 - c)) for t in range(len(mine))]

    return _simple_exchange(mine, [jax.ShapeDtypeStruct(s.shape, s.dtype) for s in mine], copies)


def _walk(name, place, parts):
    starts = [sum(p[0] for p in parts[:t]) for t in range(len(parts))]
    held = lambda index, start, steps: (lambda s, pr: index(jnp.clip(s - start, 0, steps - 1), pr))
    in_specs, out_specs, out_shapes, operands = [], [], [], []
    for (steps, ins, outs, shapes, ops, _), start in zip(parts, starts):
        in_specs += [pl.BlockSpec(blk, held(index, start, steps)) for blk, index in ins]
        out_specs += [pl.BlockSpec(blk, held(index, start, steps)) for blk, index in outs]
        out_shapes += list(shapes)
        operands += list(ops)

    def body(place_ref, *refs):
        s = pl.program_id(0)
        i, o = 0, len(in_specs)
        for (steps, ins, outs, _, _, fn), start in zip(parts, starts):
            mine_in, mine_out = refs[i:i + len(ins)], refs[o:o + len(outs)]
            i, o = i + len(ins), o + len(outs)

            @pl.when((s >= start) & (s < start + steps))
            def _(mine_in=mine_in, mine_out=mine_out, start=start, fn=fn):
                fn(s - start, mine_in, mine_out)

    res = pl.pallas_call(
        body, name=name, out_shape=out_shapes,
        grid_spec=pltpu.PrefetchScalarGridSpec(num_scalar_prefetch=1, grid=(sum(p[0] for p in parts),), in_specs=in_specs,
                                               out_specs=out_specs),
        compiler_params=_cparams(("arbitrary",)),
    )(place, *operands)
    counts = [len(p[2]) for p in parts]
    return [res[sum(counts[:t]):sum(counts[:t + 1])] for t in range(len(parts))]


def _sum_sibling(place, slots, received, name):
    def part(slot, got):
        n, rows, cols = got.shape
        block = (None, rows, cols)

        def fn(j, ins, outs):
            outs[0][...] = (ins[0][...] + ins[1][...]).astype(BF16)

        return (n, [(block, lambda j, pr: (j, pr[1], 0)), (block, lambda j, pr: (j, 0, 0))], [(block, lambda j, pr: (j, 0, 0))],
                [jax.ShapeDtypeStruct(got.shape, BF16)], [slot, got], fn)

    return [r[0] for r in _walk(name, place, [part(s, g) for s, g in zip(slots, received)])]


def _sum_chips(place, slots, received, others, name):
    def part(slot, got, other):
        _, rows, cols = got.shape
        block = (None, rows, cols)

        def fn(_, ins, outs):
            own = ins[0][...] + ins[1][...]
            outs[0][...] = ((own + ins[2][0].astype(F32)) + ins[2][1].astype(F32)) + ins[2][2].astype(F32)

        return (1, [(block, lambda _, pr: (pr[0], pr[1], 0)), (block, lambda _, pr: (pr[0], 0, 0)),
                    ((3, rows, cols), lambda _, pr: (0, 0, 0))], [((rows, cols), lambda _, pr: (0, 0))],
                [jax.ShapeDtypeStruct((rows, cols), F32)], [slot, got, other], fn)

    return [r[0] for r in _walk(name, place, [part(*t) for t in zip(slots, received, others)])]


N_DEVICES = 8


def _sum_devices(block, name):
    def body(v_ref, o_ref, land_ref, send_sems, recv_sems):
        x, y, c = _place()
        me = 4 * x + 2 * y + c
        copies = []
        for mask in range(1, N_DEVICES):
            peer = (x ^ (mask >> 2), y ^ ((mask >> 1) & 1), c ^ (mask & 1))
            copies.append(pltpu.make_async_remote_copy(src_ref=v_ref, dst_ref=land_ref.at[me], send_sem=send_sems.at[mask - 1],
                                                       recv_sem=recv_sems.at[mask - 1], device_id=peer, device_id_type=MESH))
        for cp in copies:
            cp.start()
        land_ref[me] = v_ref[...]
        for cp in copies:
            cp.wait_recv()
        total = land_ref[0]
        for d in range(1, N_DEVICES):
            total = total + land_ref[d]
        o_ref[...] = total
        for cp in copies:
            cp.wait_send()

    vmem = pl.BlockSpec(memory_space=pltpu.VMEM)
    return pl.pallas_call(
        body, name=name, in_specs=[vmem], out_specs=vmem, out_shape=jax.ShapeDtypeStruct(block.shape, F32),
        scratch_shapes=[pltpu.VMEM((N_DEVICES,) + block.shape, F32), pltpu.SemaphoreType.DMA((N_DEVICES - 1,)),
                        pltpu.SemaphoreType.DMA((N_DEVICES - 1,))],
    )(block)


def _vec_block(g_mix, g_mlp, g_ple, g_final, b_forget, b_gate_rows, last=None):
    pad = lambda a: jnp.concatenate([a, jnp.zeros((a.shape[0], D_MODEL - a.shape[1]), F32)], axis=1)
    last = jnp.zeros((1, 0), F32) if last is None else last
    return jnp.concatenate([g_mix, g_mlp, g_ple, g_final.reshape(1, D_MODEL), pad(b_forget), pad(b_gate_rows), pad(last)],
                           axis=0)


def _adam_math(w, g, m, v):
    m_new = ADAM_B1 * m + (1.0 - ADAM_B1) * g
    v_new = ADAM_B2 * v + (1.0 - ADAM_B2) * (g * g)
    m_hat = m_new / (1.0 - ADAM_B1 ** ADAM_STEP)
    v_hat = v_new / (1.0 - ADAM_B2 ** ADAM_STEP)
    return -ADAM_LR * (m_hat / (jnp.sqrt(v_hat) + ADAM_EPS) + ADAM_WD * w), m_new, v_new


def _adamw_halves(place, weights, name):
    def part(w, m, v, g_mine, g_theirs):
        rows, cols = g_mine.shape
        whole = ((rows, cols), lambda s, pr: (pr[1] + s - 2 * pr[1] * s, 0))
        half = ((rows, cols), lambda s, pr: (0, 0))

        def fn(s, ins, outs):
            g = jnp.where(s == 0, ins[3][...], ins[4][...])
            outs[0][...] = g
            outs[1][...], outs[2][...], outs[3][...] = _adam_math(ins[0][...], g, ins[1][...], ins[2][...])

        return (2, [whole] * 3 + [half] * 2, [whole] * 4, [jax.ShapeDtypeStruct(w.shape, F32)] * 4, [w, m, v, g_mine, g_theirs], fn)

    return _walk(name, place, [part(*t) for t in weights])


def _adamw_vec(w, g, m, v):
    def body(w_ref, g_ref, m_ref, v_ref, d_ref, nm_ref, nv_ref):
        d_ref[...], nm_ref[...], nv_ref[...] = _adam_math(w_ref[...], g_ref[...], m_ref[...], v_ref[...])

    return pl.pallas_call(body, name="adamw_vectors", out_shape=[jax.ShapeDtypeStruct(w.shape, F32)] * 3)(w, g, m, v)


WEIGHT_NAMES = ("g_mix", "w_in", "b_forget", "b_gate", "w_branch_fox", "w_branch_sb", "w_out", "g_mlp", "w_up", "w_down",
                "g_ple", "w_ple_gate", "w_ple", "g_final")
W_IN_SHARD = D_IN // N_CHIPS
Q_END, F_END, B_END = 3 * D_ATT, 3 * D_ATT + N_HEADS, 6 * D_ATT + N_HEADS
GATE_SHARD = D_MODEL // N_CHIPS


LATE = SHARDED[:1]
EARLY = SHARDED[1:]
BIG = tuple(t for t, n in enumerate(EARLY) if n in ("w_up", "w_down"))
SMALL = tuple(t for t in range(len(EARLY)) if t not in BIG)


def _first_weights(w_in_slots):
    def cols(*ranges):
        parts = []
        for lo, hi in ranges:
            for j in range(N_CHIPS):
                a, b = max(lo, j * W_IN_SHARD), min(hi, (j + 1) * W_IN_SHARD)
                if a < b:
                    parts.append(w_in_slots[j, :, a - j * W_IN_SHARD:b - j * W_IN_SHARD])
        return parts

    forget = jnp.concatenate(cols((Q_END, F_END)) + [jnp.zeros((D_MODEL, F_PAD - N_HEADS), BF16)], axis=1)
    return {"qkv": jnp.concatenate(cols((0, Q_END), (F_END, B_END)), axis=1), "gate": jnp.concatenate(cols((B_END, D_IN)), axis=1),
            "forget": forget}


GATE_ROWS = 2 * ROW_ALIGN


def _gate_bits(b_gate):
    bits = lax.bitcast_convert_type(b_gate, BF16).reshape(2, 2 * GATE_SHARD)
    return jnp.concatenate([bits, jnp.zeros((GATE_ROWS - 2, 2 * GATE_SHARD), BF16)], axis=0)


def _rest_weights(gathered):
    rows = lambda a: a.reshape(N_CHIPS * a.shape[1], a.shape[2])
    bits = gathered["b_gate"][:, :2].reshape(N_CHIPS, 2, GATE_SHARD, 2)
    b_gate = jnp.transpose(lax.bitcast_convert_type(bits, F32), (1, 0, 2)).reshape(2, D_MODEL)
    return {"branch_fox": gathered["w_branch_fox"], "branch_sb": gathered["w_branch_sb"], "out": rows(gathered["w_out"]),
            "up": gathered["w_up"], "down": rows(gathered["w_down"]), "ple_gate": rows(gathered["w_ple_gate"]),
            "ple": gathered["w_ple"], "b_gate": b_gate}


def _early_slots(gw):
    rows = lambda a: a.reshape(N_CHIPS, a.shape[0] // N_CHIPS, a.shape[1])
    return {"w_branch_fox": gw["branch_fox"], "w_branch_sb": gw["branch_sb"], "w_out": rows(gw["out"]), "w_up": gw["up"],
            "w_down": rows(gw["down"]), "w_ple_gate": rows(gw["ple_gate"]), "w_ple": gw["ple"]}


W_IN_FLAT = (W_IN_SHARD * D_MODEL // LANES, LANES)


def _w_in_slots(gw):
    c = D_MODEL // LANES
    g_t = jnp.concatenate([gw["qkv"][:Q_END * c], gw["forget"][:N_HEADS * c], gw["qkv"][Q_END * c:], gw["gate"]], axis=0)
    return g_t.reshape((N_CHIPS,) + W_IN_FLAT)


def _flat(a):
    return jnp.transpose(a, (2, 0, 1)).reshape(W_IN_FLAT)


def _unflat(a):
    return jnp.transpose(a.reshape(W_IN_SHARD, D_MODEL // LANES, LANES), (1, 2, 0)).reshape(1, D_MODEL, W_IN_SHARD)


def kernel(x, p, g_mix, w_in, b_forget, b_gate, w_branch_fox, w_branch_sb, w_out, g_mlp, w_up, w_down, g_ple, w_ple_gate, w_ple, g_final, loss_target, m_g_mix, m_w_in, m_b_forget, m_b_gate, m_w_branch_fox, m_w_branch_sb, m_w_out, m_g_mlp, m_w_up, m_w_down, m_g_ple, m_w_ple_gate, m_w_ple, m_g_final, v_g_mix, v_w_in, v_b_forget, v_b_gate, v_w_branch_fox, v_w_branch_sb, v_w_out, v_g_mlp, v_w_up, v_w_down, v_g_ple, v_w_ple_gate, v_w_ple, v_g_final):
    weights = dict(g_mix=g_mix, w_in=w_in, b_forget=b_forget, b_gate=b_gate, w_branch_fox=w_branch_fox,
                   w_branch_sb=w_branch_sb, w_out=w_out, g_mlp=g_mlp, w_up=w_up, w_down=w_down, g_ple=g_ple,
                   w_ple_gate=w_ple_gate, w_ple=w_ple, g_final=g_final)
    first = dict(g_mix=m_g_mix, w_in=m_w_in, b_forget=m_b_forget, b_gate=m_b_gate, w_branch_fox=m_w_branch_fox,
                 w_branch_sb=m_w_branch_sb, w_out=m_w_out, g_mlp=m_g_mlp, w_up=m_w_up, w_down=m_w_down, g_ple=m_g_ple,
                 w_ple_gate=m_w_ple_gate, w_ple=m_w_ple, g_final=m_g_final)
    second = dict(g_mix=v_g_mix, w_in=v_w_in, b_forget=v_b_forget, b_gate=v_b_gate, w_branch_fox=v_w_branch_fox,
                  w_branch_sb=v_w_branch_sb, w_out=v_w_out, g_mlp=v_g_mlp, w_up=v_w_up, w_down=v_w_down, g_ple=v_g_ple,
                  w_ple_gate=v_w_ple_gate, w_ple=v_w_ple, g_final=v_g_final)
    cx, cy, cc = _place()
    chip = 2 * cx + cy
    place = jnp.stack([chip, cc]).astype(jnp.int32)
    col0 = chip * GATE_SHARD

    (w_in_slots,) = _run_exchange(_gather_weights([weights[n][0].astype(BF16) for n in LATE]), "gather_w_in")
    rest = _gather_weights([weights[n][0].astype(BF16) for n in EARLY] + [_gate_bits(b_gate[0])])
    vec = {"g_mix": g_mix, "b_forget": jnp.concatenate([b_forget, jnp.zeros((1, F_PAD - N_HEADS), F32)], axis=1),
           "g_mlp": g_mlp, "g_ple": g_ple, "g_final": g_final.reshape(1, D_MODEL)}

    loss, grad_x, reduced, gvec = _local_step(x, p[0], loss_target, _first_weights(w_in_slots), rest, vec, place)

    out = {}
    args = lambda n: (weights[n][0], first[n][0], second[n][0]) + tuple(reduced[n])
    for names, tag in [([EARLY[t] for t in SMALL], "small")] + [([EARLY[t]], EARLY[t]) for t in BIG]:
        for n, res in zip(names, _adamw_halves(place, [args(n) for n in names], "adamw_" + tag)):
            out[n] = [r[None] for r in res]
    (res,) = _adamw_halves(place, [(_flat(w_in), _flat(m_w_in), _flat(v_w_in)) + tuple(reduced["w_in"])], "adamw_w_in")
    out["w_in"] = [_unflat(r) for r in res]

    g_block = _sum_devices(_vec_block(gvec["g_mix"], gvec["g_mlp"], gvec["g_ple"], gvec["g_final"][0], gvec["b_forget"],
                                      gvec["b_gate"], loss), "reduce_vectors")
    loss = g_block[7, 0]
    g_gate = lax.dynamic_slice(g_block[5:7], (0, col0), (2, GATE_SHARD))
    blocks = [_vec_block(d["g_mix"], d["g_mlp"], d["g_ple"], d["g_final"], d["b_forget"], d["b_gate"][0])
              for d in (weights, first, second)]
    g_rows = jnp.concatenate([g_block[0:5], jnp.concatenate([g_gate, jnp.zeros((2, D_MODEL - GATE_SHARD), F32)], axis=1),
                              jnp.zeros((1, D_MODEL), F32)], axis=0)
    res = (g_rows,) + tuple(_adamw_vec(blocks[0], g_rows, blocks[1], blocks[2]))
    out["g_mix"] = [r[0:1] for r in res]
    out["g_mlp"] = [r[1:2] for r in res]
    out["g_ple"] = [r[2:3] for r in res]
    out["g_final"] = [r[3] for r in res]
    out["b_forget"] = [r[4:5, :N_HEADS] for r in res]
    out["b_gate"] = [r[5:7, :GATE_SHARD][None] for r in res]
    return (loss, grad_x, *[out[n][0] for n in WEIGHT_NAMES], *[out[n][1] for n in WEIGHT_NAMES],
            *[out[n][2] for n in WEIGHT_NAMES], *[out[n][3] for n in WEIGHT_NAMES])
```

```python
import jax
import jax.numpy as jnp
from jax import lax
from jax.experimental import pallas as pl
from jax.experimental.pallas import tpu as pltpu

F32 = jnp.float32
BF16 = jnp.bfloat16

D_MODEL = 1024
HEAD_DIM = 64
N_HEADS = 8
D_ATT = N_HEADS * HEAD_DIM
D_PLE = 256
D_IN = 6 * D_ATT + N_HEADS + 2 * D_MODEL
F_PAD = 128
EPS = 1e-6
SCALE = HEAD_DIM ** -0.5
N_CHIPS = 4
LANES = 128
ATT_BLOCK = 256
FOX_TILES = (512, 512)
SB_TILES = (512, 256)
NEG = -1e30

ADAM_LR = 0.001
ADAM_B1 = 0.9
ADAM_B2 = 0.999
ADAM_EPS = 1e-08
ADAM_WD = 0.01
ADAM_STEP = 10

VMEM_LIMIT = 56 * 1024 * 1024

MESH = pl.DeviceIdType.MESH


def _cparams(sem=None):
    return pltpu.CompilerParams(dimension_semantics=sem, vmem_limit_bytes=VMEM_LIMIT)


def _relu2(t):
    t = t.astype(F32)
    return t * t


_DIMS = {"nn": (((1,), (0,)), ((), ())), "nt": (((1,), (1,)), ((), ())), "tn": (((0,), (0,)), ((), ()))}
NT_DIMS = _DIMS["nt"]
TN_DIMS = _DIMS["tn"]


def _mm(a, b, *, mode, name, out_dtype=F32, tm=512, tn=512, tk=512, add=None, a_fn=None, epi=None, extra=None,
        col_shards=False, behind=None, flat_out=False):
    if mode == "nn":
        (m, k), n = a.shape, b.shape[-1]
    elif mode == "nt":
        (m, k), n = a.shape, b.shape[-2]
    else:
        (k, m), n = a.shape, b.shape[1]
    shard = None
    if col_shards:
        if mode == "nn":
            shard, n = n, N_CHIPS * n
            tn = min(tn, shard)
        elif mode == "nt":
            shard = b.shape[-1]
            tk = min(tk, shard)
        else:
            shard = n // N_CHIPS
            tn = min(tn, shard)
    tm, tn, tk = min(tm, m), min(tn, n), min(tk, k)
    assert m % tm == 0 and n % tn == 0 and k % tk == 0, (name, m, n, k)
    nk = k // tk
    a_spec = {"nn": pl.BlockSpec((tm, tk), lambda i, j, kk: (i, kk)),
              "nt": pl.BlockSpec((tm, tk), lambda i, j, kk: (i, kk)),
              "tn": pl.BlockSpec((tk, tm), lambda i, j, kk: (kk, i))}[mode]
    b_spec = {"nn": pl.BlockSpec((tk, tn), lambda i, j, kk: (kk, j)),
              "nt": pl.BlockSpec((tn, tk), lambda i, j, kk: (j, kk)),
              "tn": pl.BlockSpec((tk, tn), lambda i, j, kk: (kk, j))}[mode]
    o_spec = pl.BlockSpec((tm, tn), lambda i, j, kk: (i, j))
    out_shape = (m, n)
    if col_shards and mode == "nn":
        per = shard // tn
        b_spec = pl.BlockSpec((None, tk, tn), lambda i, j, kk: (j // per, kk, j % per))
    elif col_shards and mode == "nt":
        per = shard // tk
        b_spec = pl.BlockSpec((None, tn, tk), lambda i, j, kk: (kk // per, j, kk % per))
    elif col_shards:
        assert add is None and extra is None
        per = shard // tn
        o_spec = pl.BlockSpec((None, tm, tn), lambda i, j, kk: (j // per, i, j % per))
        out_shape = (N_CHIPS, m, shard)
    if flat_out:
        assert mode == "tn" and tn == n == D_MODEL and not col_shards and add is None and extra is None
        chunks = D_MODEL // LANES
        o_spec = pl.BlockSpec((tm * chunks, LANES), lambda i, j, kk: (i, 0))
        out_shape = (m * chunks, LANES)
    operands, in_specs = [a, b], [a_spec, b_spec]
    third = add if add is not None else extra
    if third is not None:
        operands.append(third)
        in_specs.append(o_spec)

    def body(*refs):
        a_ref, b_ref = refs[0], refs[1]
        t_ref = refs[2] if third is not None else None
        o_ref = refs[3] if third is not None else refs[2]
        acc_ref = refs[-1] if nk > 1 else None
        at = a_ref[...]
        if a_fn is not None:
            at = a_fn(at)
        part = lax.dot_general(at.astype(BF16), b_ref[...].astype(BF16), _DIMS[mode], preferred_element_type=F32)

        def finish(acc):
            if epi is not None:
                acc = epi(acc, None if t_ref is None else t_ref[...])
            elif add is not None:
                acc = acc + t_ref[...].astype(F32)
            if flat_out:
                for q in range(D_MODEL // LANES):
                    o_ref[pl.ds(q, tm, stride=D_MODEL // LANES), :] = acc[:, q * LANES:(q + 1) * LANES].astype(o_ref.dtype)
                return
            o_ref[...] = acc.astype(o_ref.dtype)

        if nk == 1:
            finish(part)
        else:
            kk = pl.program_id(2)

            @pl.when(kk == 0)
            def _():
                acc_ref[...] = part

            @pl.when(kk > 0)
            def _():
                acc_ref[...] += part

            @pl.when(kk == nk - 1)
            def _():
                finish(acc_ref[...])

    call = dict(name=name, grid=(m // tm, n // tn, nk), in_specs=in_specs,
                scratch_shapes=[pltpu.VMEM((tm, tn), F32)] if nk > 1 else [])
    if behind is not None:
        (res,), exchanged = _call_behind(body, behind, out_specs=[o_spec], out_shape=[jax.ShapeDtypeStruct(out_shape, out_dtype)],
                                         operands=operands, **call)
        return res, exchanged
    return pl.pallas_call(body, out_specs=o_spec, out_shape=jax.ShapeDtypeStruct(out_shape, out_dtype),
                          compiler_params=_cparams(("parallel", "parallel", "arbitrary")), **call)(*operands)


ROW_TILE = 512


def _row_spec(width=D_MODEL, rows=ROW_TILE):
    return pl.BlockSpec((rows, width), lambda i: (i, 0))


def _vec_spec(rows=1, width=D_MODEL):
    return pl.BlockSpec((rows, width), lambda i: (0, 0))


def _xhat(x):
    r = lax.rsqrt(jnp.mean(x * x, axis=-1, keepdims=True) + EPS)
    return x * r, r


def _rms_bwd_rows(dh, x, g):
    xh, r = _xhat(x)
    dxh = dh * g
    dx = r * (dxh - xh * jnp.mean(dxh * xh, axis=-1, keepdims=True))
    return dx, jnp.sum(dh * xh, axis=0, keepdims=True)


def _norm_fwd(x, g, name):
    t = x.shape[0]

    def body(x_ref, g_ref, h_ref):
        xh, _ = _xhat(x_ref[...])
        h_ref[...] = (xh * g_ref[...]).astype(BF16)

    return pl.pallas_call(
        body, name=name, grid=(t // ROW_TILE,), in_specs=[_row_spec(), _vec_spec()], out_specs=_row_spec(),
        out_shape=jax.ShapeDtypeStruct((t, D_MODEL), BF16), compiler_params=_cparams(("parallel",)),
    )(x, g)


def _mm_res_norm(a, b, res, g, name, a_fn=None):
    t, k = a.shape

    def body(a_ref, b_ref, res_ref, g_ref, x_ref, h_ref):
        at = a_ref[...] if a_fn is None else a_fn(a_ref[...])
        x_new = res_ref[...] + _dot(at.astype(BF16), b_ref[...])
        x_ref[...] = x_new
        h_ref[...] = (_xhat(x_new)[0] * g_ref[...]).astype(BF16)

    return pl.pallas_call(
        body, name=name, grid=(t // ROW_TILE,),
        in_specs=[pl.BlockSpec((ROW_TILE, k), lambda i: (i, 0)), pl.BlockSpec(b.shape, lambda i: (0, 0)), _row_spec(), _vec_spec()],
        out_specs=[_row_spec(), _row_spec()],
        out_shape=[jax.ShapeDtypeStruct((t, D_MODEL), F32), jax.ShapeDtypeStruct((t, D_MODEL), BF16)],
        compiler_params=_cparams(("parallel",)),
    )(a, b, res, g)


def _mm_norm_bwd(pairs, dh_first, x, g, dres, name, behind=None):
    t = x.shape[0]
    operands, in_specs = [], []
    for a, b in pairs:
        if b.ndim == 3:
            for j in range(b.shape[0]):
                operands += [a, b]
                in_specs += [pl.BlockSpec((ROW_TILE, b.shape[2]), lambda i, j=j: (i, j)),
                             pl.BlockSpec((None, D_MODEL, b.shape[2]), lambda i, j=j: (j, 0, 0))]
        else:
            operands += [a, b]
            in_specs += [pl.BlockSpec((ROW_TILE, a.shape[1]), lambda i: (i, 0)), pl.BlockSpec(b.shape, lambda i: (0, 0))]
    n_mm = len(operands)
    operands += [x, g, dres] + ([] if dh_first is None else [dh_first])
    in_specs += [_row_spec(), _vec_spec(), _row_spec()] + ([] if dh_first is None else [_row_spec()])

    def body(*refs):
        x_ref, g_ref, dres_ref = refs[n_mm:n_mm + 3]
        dx_ref, dxb_ref, dg_ref = refs[-3:]
        dh = 0.0 if dh_first is None else refs[n_mm + 3][...]
        for k in range(0, n_mm, 2):
            dh = dh + lax.dot_general(refs[k][...].astype(BF16), refs[k + 1][...].astype(BF16), NT_DIMS,
                                      preferred_element_type=F32)
        dx, dg = _rms_bwd_rows(dh, x_ref[...], g_ref[...])
        dx = dx + dres_ref[...]
        dx_ref[...] = dx
        dxb_ref[...] = dx.astype(BF16)

        @pl.when(pl.program_id(0) == 0)
        def _():
            dg_ref[...] = jnp.zeros_like(dg_ref)

        dg_ref[...] += dg

    call = dict(name=name, grid=(t // ROW_TILE,), in_specs=in_specs, out_specs=[_row_spec(), _row_spec(), _vec_spec()],
                out_shape=[jax.ShapeDtypeStruct((t, D_MODEL), F32), jax.ShapeDtypeStruct((t, D_MODEL), BF16),
                           jax.ShapeDtypeStruct((1, D_MODEL), F32)])
    if behind is not None:
        return _call_behind(body, behind, scratch_shapes=[], operands=operands, **call)
    return pl.pallas_call(body, compiler_params=_cparams(("arbitrary",)), **call)(*operands)


def _shards_spec(w):
    return pl.BlockSpec(w.shape, lambda i: (0, 0, 0))


def _gate_fwd(gl, b_gate, o_fox, o_sb, w_fox, w_sb):
    t = o_fox.shape[0]

    def body(gla_ref, glb_ref, b_ref, ofox_ref, osb_ref, wf_ref, ws_ref, m_ref, of_ref, os_ref):
        of = jnp.concatenate([_dot(ofox_ref[...], wf_ref[j]) for j in range(N_CHIPS)], axis=1)
        os_ = jnp.concatenate([_dot(osb_ref[...], ws_ref[j]) for j in range(N_CHIPS)], axis=1)
        ga = jax.nn.sigmoid(gla_ref[...] + b_ref[0:1, :])
        gb = jax.nn.sigmoid(glb_ref[...] + b_ref[1:2, :])
        of_ref[...] = of
        os_ref[...] = os_
        m_ref[...] = (ga * of + gb * os_).astype(BF16)

    return pl.pallas_call(
        body, name="gate_fwd", grid=(t // ROW_TILE,),
        in_specs=[pl.BlockSpec((ROW_TILE, D_MODEL), lambda i: (i, 0)), pl.BlockSpec((ROW_TILE, D_MODEL), lambda i: (i, 1)),
                  _vec_spec(2), _row_spec(D_ATT), _row_spec(D_ATT), _shards_spec(w_fox), _shards_spec(w_sb)],
        out_specs=[_row_spec(), _row_spec(), _row_spec()],
        out_shape=[jax.ShapeDtypeStruct((t, D_MODEL), BF16)] + [jax.ShapeDtypeStruct((t, D_MODEL), F32)] * 2,
        compiler_params=_cparams(("parallel",)),
    )(gl, gl, b_gate, o_fox, o_sb, w_fox, w_sb)


def _gate_bwd(gl, b_gate, of, os_, dx, w_out, w_fox, w_sb):
    t = of.shape[0]
    shard = D_MODEL // N_CHIPS

    def back(d, w_ref):
        return sum(_dot(d[:, j * shard:(j + 1) * shard], w_ref[j], NT_DIMS) for j in range(N_CHIPS)).astype(BF16)

    def body(gla_ref, glb_ref, b_ref, of_ref, os_ref, dx_ref, w_ref, wf_ref, ws_ref,
             dof_ref, dos_ref, dgl_ref, db_ref, dofox_ref, dosb_ref):
        dm = _dot(dx_ref[...], w_ref[...], NT_DIMS)
        ga = jax.nn.sigmoid(gla_ref[...] + b_ref[0:1, :])
        gb = jax.nn.sigmoid(glb_ref[...] + b_ref[1:2, :])
        dof = (dm * ga).astype(BF16)
        dos = (dm * gb).astype(BF16)
        dof_ref[...] = dof
        dos_ref[...] = dos
        dofox_ref[...] = back(dof, wf_ref)
        dosb_ref[...] = back(dos, ws_ref)
        dgla = dm * of_ref[...] * ga * (1.0 - ga)
        dglb = dm * os_ref[...] * gb * (1.0 - gb)
        dgl_ref[:, 0:D_MODEL] = dgla.astype(BF16)
        dgl_ref[:, D_MODEL:2 * D_MODEL] = dglb.astype(BF16)

        @pl.when(pl.program_id(0) == 0)
        def _():
            db_ref[...] = jnp.zeros_like(db_ref)

        db_ref[0:1, :] += jnp.sum(dgla, axis=0, keepdims=True)
        db_ref[1:2, :] += jnp.sum(dglb, axis=0, keepdims=True)

    outs = pl.pallas_call(
        body, name="gate_bwd", grid=(t // ROW_TILE,),
        in_specs=[pl.BlockSpec((ROW_TILE, D_MODEL), lambda i: (i, 0)), pl.BlockSpec((ROW_TILE, D_MODEL), lambda i: (i, 1)),
                  _vec_spec(2), _row_spec(), _row_spec(), _row_spec(), pl.BlockSpec(w_out.shape, lambda i: (0, 0)),
                  _shards_spec(w_fox), _shards_spec(w_sb)],
        out_specs=[_row_spec(), _row_spec(), _row_spec(2 * D_MODEL), _vec_spec(2), _row_spec(D_ATT), _row_spec(D_ATT)],
        out_shape=[jax.ShapeDtypeStruct((t, D_MODEL), BF16)] * 2 + [jax.ShapeDtypeStruct((t, 2 * D_MODEL), BF16),
                                                                      jax.ShapeDtypeStruct((2, D_MODEL), F32)]
        + [jax.ShapeDtypeStruct((t, D_ATT), BF16)] * 2,
        compiler_params=_cparams(("arbitrary",)),
    )(gl, gl, b_gate, of, os_, dx, w_out, w_fox, w_sb)
    return outs


def _head_and_loss(x2, h3, p, w_gate, w_ple, g_final, target):
    t = x2.shape[0]

    def body(x2_ref, h3_ref, p_ref, wg_ref, wp_ref, g_ref, tgt_ref, dx3_ref, dpre_ref, dpe_ref, dg_ref, loss_ref):
        gp = jax.nn.sigmoid(_dot(h3_ref[...], wg_ref[...]))
        p_t = p_ref[...].astype(BF16)
        pe_t = jnp.concatenate([_dot(p_t, wp_ref[j]) for j in range(N_CHIPS)], axis=1)
        x3 = x2_ref[...] + gp * pe_t
        g = g_ref[...]
        xh, _ = _xhat(x3)
        err = xh * g - tgt_ref[...]
        dy = err * (1.0 / D_MODEL)
        dx3, dg = _rms_bwd_rows(dy, x3, g)
        dx3_ref[...] = dx3
        dpre_ref[...] = (dx3 * pe_t * gp * (1.0 - gp)).astype(BF16)
        dpe_ref[...] = (dx3 * gp).astype(BF16)

        @pl.when(pl.program_id(0) == 0)
        def _():
            dg_ref[...] = jnp.zeros_like(dg_ref)
            loss_ref[...] = jnp.zeros_like(loss_ref)

        dg_ref[...] += dg
        loss_ref[...] += 0.5 * jnp.sum(jnp.mean(err * err, axis=-1, keepdims=True), axis=0, keepdims=True)

    return pl.pallas_call(
        body, name="head_and_loss", grid=(t // ROW_TILE,),
        in_specs=[_row_spec(), _row_spec(), _row_spec(D_PLE), pl.BlockSpec(w_gate.shape, lambda i: (0, 0)),
                  pl.BlockSpec(w_ple.shape, lambda i: (0, 0, 0)), _vec_spec(), _row_spec()],
        out_specs=[_row_spec(), _row_spec(), _row_spec(), _vec_spec(), _vec_spec(1, LANES)],
        out_shape=[jax.ShapeDtypeStruct((t, D_MODEL), F32), jax.ShapeDtypeStruct((t, D_MODEL), BF16),
                   jax.ShapeDtypeStruct((t, D_MODEL), BF16), jax.ShapeDtypeStruct((1, D_MODEL), F32),
                   jax.ShapeDtypeStruct((1, LANES), F32)],
        compiler_params=_cparams(("arbitrary",)),
    )(x2, h3, p, w_gate, w_ple, g_final, target)


def _split3(v):
    hi = v.astype(BF16)
    r1 = v - hi.astype(F32)
    mid = r1.astype(BF16)
    lo = (r1 - mid.astype(F32)).astype(BF16)
    return hi, mid, lo


def _split2(v):
    hi = v.astype(BF16)
    return jnp.concatenate([hi, (v - hi.astype(F32)).astype(BF16)], axis=1)


def _dot(a, b, dims=_DIMS["nn"]):
    return lax.dot_general(a, b, dims, preferred_element_type=F32)


def _tri(n, rel):
    row = lax.broadcasted_iota(jnp.int32, (n, n), 0)
    col = lax.broadcasted_iota(jnp.int32, (n, n), 1)
    return rel(row, col).astype(BF16)


def _tri2(n, rel):
    t = _tri(n, rel)
    return jnp.concatenate([t, t], axis=0)


def _log_sigmoid(v):
    return -(jnp.maximum(-v, 0.0) + jnp.log(1.0 + jnp.exp(-jnp.abs(v))))


def _fox_prep(fl, b_forget, batch, seq):
    nb = seq // ATT_BLOCK

    def body(fl_ref, b_ref, cw_ref, cr_ref):
        col = lax.broadcasted_iota(jnp.int32, (ATT_BLOCK, F_PAD), 1)
        lower = _tri(ATT_BLOCK, lambda r, c: c <= r)
        upper = _tri(ATT_BLOCK, lambda r, c: r <= c)
        expand = (lax.broadcasted_iota(jnp.int32, (F_PAD, D_ATT), 1) // HEAD_DIM
                  == lax.broadcasted_iota(jnp.int32, (F_PAD, D_ATT), 0)).astype(BF16)
        carry_w = jnp.zeros((1, D_ATT), F32)
        carry_r = jnp.zeros((F_PAD, 1), F32)
        for i in range(nb):
            blk = slice(i * ATT_BLOCK, (i + 1) * ATT_BLOCK)
            logf = jnp.where(col < N_HEADS, _log_sigmoid(fl_ref[blk, :] + b_ref[...]), 0.0)
            cw = jnp.zeros((ATT_BLOCK, D_ATT), F32) + carry_w
            cr = jnp.zeros((F_PAD, ATT_BLOCK), F32) + carry_r
            for part in _split3(logf):
                cw += _dot(lower, _dot(part, expand).astype(BF16))
                cr += _dot(part, upper, TN_DIMS)
            cw_ref[blk, :] = cw
            cr_ref[:, blk] = cr[0:N_HEADS, :]
            carry_w = cw[ATT_BLOCK - 1:ATT_BLOCK, :]
            carry_r = cr[:, ATT_BLOCK - 1:ATT_BLOCK]

    return pl.pallas_call(
        body, name="fox_prep", grid=(batch,),
        in_specs=[pl.BlockSpec((seq, F_PAD), lambda b: (b, 0)), pl.BlockSpec((1, F_PAD), lambda b: (0, 0))],
        out_specs=[pl.BlockSpec((seq, D_ATT), lambda b: (b, 0)), pl.BlockSpec((N_HEADS, seq), lambda b: (b, 0))],
        out_shape=[jax.ShapeDtypeStruct((batch * seq, D_ATT), F32), jax.ShapeDtypeStruct((batch * N_HEADS, seq), F32)],
        compiler_params=_cparams(("parallel",)),
    )(fl, b_forget)


def _fox_post(dcs_wide, drs_wide, fl, b_forget, batch, seq):
    nb = seq // ATT_BLOCK

    def body(dcs_ref, drs_ref, fl_ref, b_ref, dfl_ref, db_ref):
        pick = (lax.broadcasted_iota(jnp.int32, (D_ATT, F_PAD), 0)
                == lax.broadcasted_iota(jnp.int32, (D_ATT, F_PAD), 1) * HEAD_DIM).astype(BF16)
        upper = _tri(ATT_BLOCK, lambda r, c: r <= c)
        col = lax.broadcasted_iota(jnp.int32, (ATT_BLOCK, F_PAD), 1)

        @pl.when(pl.program_id(0) == 0)
        def _():
            db_ref[...] = jnp.zeros_like(db_ref)

        carry = jnp.zeros((1, F_PAD), F32)
        for i in reversed(range(nb)):
            blk = slice(i * ATT_BLOCK, (i + 1) * ATT_BLOCK)
            narrow = jnp.zeros((ATT_BLOCK, F_PAD), F32)
            for part in _split3(drs_ref[blk, :] - dcs_ref[blk, :]):
                narrow += _dot(part, pick)
            after = jnp.zeros((ATT_BLOCK, F_PAD), F32) + carry
            for part in _split3(narrow):
                after += _dot(upper, part)
            carry = after[0:1, :]
            pre = fl_ref[blk, :] + b_ref[...]
            dfl = jnp.where(col < N_HEADS, after * jax.nn.sigmoid(-pre), 0.0)
            dfl_ref[blk, :] = dfl.astype(BF16)
            db_ref[...] += jnp.sum(dfl, axis=0, keepdims=True)

    return pl.pallas_call(
        body, name="fox_post", grid=(batch,),
        in_specs=[pl.BlockSpec((seq, D_ATT), lambda b: (b, 0)), pl.BlockSpec((seq, D_ATT), lambda b: (b, 0)),
                  pl.BlockSpec((seq, F_PAD), lambda b: (b, 0)), pl.BlockSpec((1, F_PAD), lambda b: (0, 0))],
        out_specs=[pl.BlockSpec((seq, F_PAD), lambda b: (b, 0)), pl.BlockSpec((1, F_PAD), lambda b: (0, 0))],
        out_shape=[jax.ShapeDtypeStruct((batch * seq, F_PAD), BF16), jax.ShapeDtypeStruct((1, F_PAD), F32)],
        compiler_params=_cparams(("arbitrary",)),
    )(dcs_wide, drs_wide, fl, b_forget)


N_PAIRS = N_HEADS // 2


def _att_specs(seq, col0, tq):
    nq = seq // tq
    q = pl.BlockSpec((tq, LANES), lambda b, hp, qi: (b * nq + qi, col0 + hp))
    k = pl.BlockSpec((seq, LANES), lambda b, hp, qi: (b, col0 + N_PAIRS + hp))
    v = pl.BlockSpec((seq, LANES), lambda b, hp, qi: (b, col0 + 2 * N_PAIRS + hp))
    return q, k, v


def _qblock_spec(seq, tq):
    nq = seq // tq
    return pl.BlockSpec((tq, LANES), lambda b, hp, qi: (b * nq + qi, hp))


def _kv_out_spec(seq):
    return pl.BlockSpec((seq, LANES), lambda b, hp, qi: (b, hp))


def _head_masks():
    lane = lax.broadcasted_iota(jnp.int32, (1, LANES), 1)
    return [(lane >= HEAD_DIM * j) & (lane < HEAD_DIM * (j + 1)) for j in range(2)]


def _stack_heads(t, masks):
    zero = jnp.zeros_like(t)
    return jnp.concatenate([jnp.where(masks[0], t, zero), jnp.where(masks[1], t, zero)], axis=0)


def _stack_cols(t):
    return jnp.concatenate([t[:, 0:1], t[:, HEAD_DIM:HEAD_DIM + 1]], axis=0)


def _unstack(t2, masks):
    tq = t2.shape[0] // 2
    return jnp.where(masks[0], t2[:tq], t2[tq:])


def _stacked_ids(tq, tk):
    row = lax.broadcasted_iota(jnp.int32, (2 * tq, tk), 0)
    col = lax.broadcasted_iota(jnp.int32, (2 * tq, tk), 1)
    first = lax.broadcasted_iota(jnp.int32, (2 * tq, 1), 0) < tq
    return col - jnp.where(row < tq, row, row - tq), first


def _sweep(qi, tq, tk, step, init):
    per = tq // tk
    carry = lax.fori_loop(0, per * qi, lambda kb, c: step(kb, c, None, 0), init)
    for j in range(per):
        carry = step(per * qi + j, carry, -j * tk, j * tk)
    return carry


def _below(t2, top):
    tq = t2.shape[0] // 2
    return t2 if top == 0 else jnp.concatenate([t2[top:tq], t2[tq + top:]], axis=0)


def _put_below(old, new, top):
    if top == 0:
        return new
    tq = old.shape[0] // 2
    return jnp.concatenate([old[:top], new[:tq - top], old[tq:tq + top], new[tq - top:]], axis=0)


def _att_fwd(qkv, c_wide, c_row, batch, seq, behind):
    tq, tkf = FOX_TILES
    tqs, tks = SB_TILES
    assert tq == tqs and tkf == 2 * tks
    nq = seq // tq

    def body(qa_ref, ka_ref, va_ref, cw_ref, cr_ref, qb_ref, kb_ref, vb_ref, of_ref, lse_ref, os_ref, rt_ref):
        hp, qi = pl.program_id(1), pl.program_id(2)
        masks = _head_masks()
        ahead_f, first = _stacked_ids(tq, tkf)
        ahead_s, _ = _stacked_ids(tq, tks)
        later = _tri2(tks, lambda r, c: r > c)
        q2f = _stack_heads(qa_ref[...], masks) * SCALE
        q2s = _stack_heads(qb_ref[...], masks) * SCALE
        ct = _stack_cols(cw_ref[...])

        def fox(kb, carry, lead):
            m, l, acc = carry
            k0 = pl.multiple_of(kb * tkf, tkf)
            cs = jnp.where(first, cr_ref[pl.ds(2 * hp, 1), pl.ds(k0, tkf)], cr_ref[pl.ds(2 * hp + 1, 1), pl.ds(k0, tkf)])
            s = _dot(q2f, ka_ref[pl.ds(k0, tkf), :], NT_DIMS) + ct - cs
            if lead is not None:
                s = jnp.where(ahead_f <= lead, s, NEG)
            m_new = jnp.maximum(m, jnp.max(s, axis=1, keepdims=True))
            p = jnp.exp(s - m_new)
            alpha = jnp.exp(m - m_new)
            l = alpha * l + jnp.sum(p, axis=1, keepdims=True)
            acc = alpha * acc + _dot(p.astype(BF16), va_ref[pl.ds(k0, tkf), :])
            return m_new, l, acc

        def sb(kb, carry, lead):
            run, acc = carry
            k0 = pl.multiple_of(kb * tks, tks)
            ls, lsn = _sb_logits(q2s, kb_ref[pl.ds(k0, tks), :])
            if lead is not None:
                lsn = jnp.where(ahead_s < lead, lsn, 0.0)
            w = jnp.exp(ls + _dot(_split2(lsn), later) + run)
            if lead is not None:
                w = jnp.where(ahead_s < lead, w, 0.0)
            return run + jnp.sum(lsn, axis=1, keepdims=True), acc + _dot(w.astype(BF16), vb_ref[pl.ds(k0, tks), :])

        fox_c = (jnp.full((2 * tq, 1), NEG, F32), jnp.zeros((2 * tq, 1), F32), jnp.zeros((2 * tq, LANES), F32))
        sb_c = (jnp.zeros((2 * tq, 1), F32), jnp.zeros((2 * tq, LANES), F32))
        sb_c = sb(2 * qi, sb(2 * qi + 1, sb_c, -tks), 0)

        def both(i, carries):
            fox_c, sb_c = carries
            return fox(i, fox_c, None), sb(2 * qi - 2 - 2 * i, sb(2 * qi - 1 - 2 * i, sb_c, None), None)

        fox_c, (run, acc_s) = lax.fori_loop(0, qi, both, (fox_c, sb_c))
        m, l, acc = fox(qi, fox_c, 0)
        of_ref[...] = _unstack(acc / l, masks).astype(BF16)
        lse_ref[...] = _unstack(m + jnp.log(l), masks)
        os_ref[...] = _unstack(acc_s, masks).astype(BF16)
        rt_ref[...] = _unstack(run, masks)

    qa, ka, va = _att_specs(seq, 0, tq)
    qb_, kb_, vb_ = _att_specs(seq, 3 * N_PAIRS, tq)
    qb = _qblock_spec(seq, tq)
    half, wide = jax.ShapeDtypeStruct((batch * seq, D_ATT), BF16), jax.ShapeDtypeStruct((batch * seq, D_ATT), F32)
    return _call_behind(
        body, behind, name="att_fwd", grid=(batch, N_PAIRS, nq),
        in_specs=[qa, ka, va, qb, pl.BlockSpec((N_HEADS, seq), lambda b, hp, qi: (b, 0)), qb_, kb_, vb_],
        out_specs=[qb, qb, qb, qb], out_shape=[half, wide, half, wide], scratch_shapes=[],
        operands=(qkv, qkv, qkv, c_wide, c_row, qkv, qkv, qkv))


def _fox_bwd(qkv, c_wide, c_row, o, do, lse_wide, batch, seq, behind):
    tq, tk = FOX_TILES
    nq = seq // tq

    def body(q_ref, k_ref, v_ref, cw_ref, cr_ref, o_ref, do_ref, lse_ref,
             dq_ref, dk_ref, dv_ref, dcs_ref, drs_ref, dkc_acc, dv_acc):
        hp, qi = pl.program_id(1), pl.program_id(2)

        @pl.when(qi == 0)
        def _():
            dkc_acc[...] = jnp.zeros_like(dkc_acc)
            dv_acc[...] = jnp.zeros_like(dv_acc)

        masks = _head_masks()
        ahead, first = _stacked_ids(tq, tk)
        q_t, do_t = q_ref[...], do_ref[...]
        q2 = _stack_heads(q_t, masks) * SCALE
        do2 = _stack_heads(do_t, masks)
        q_and_ones = jnp.concatenate([q2, _stack_heads(jnp.ones_like(q_t), masks)], axis=1)
        ct = _stack_cols(cw_ref[...])
        lse = _stack_cols(lse_ref[...])
        prod = do_t.astype(F32) * o_ref[...].astype(F32)
        delta = jnp.concatenate([jnp.sum(jnp.where(mk, prod, 0.0), axis=1, keepdims=True) for mk in masks], axis=0)

        def step(kb, carry, lead, top):
            dq_acc, rs = carry
            k0 = pl.multiple_of(kb * tk, tk)
            kblk = k_ref[pl.ds(k0, tk), :]
            cs = jnp.where(first, cr_ref[pl.ds(2 * hp, 1), pl.ds(k0, tk)], cr_ref[pl.ds(2 * hp + 1, 1), pl.ds(k0, tk)])
            p = jnp.exp(_dot(q2, kblk, NT_DIMS) + ct - cs - lse)
            if lead is not None:
                p = jnp.where(ahead <= lead, p, 0.0)
            dp = _dot(do2, v_ref[pl.ds(k0, tk), :], NT_DIMS)
            ds = (p * (dp - delta)).astype(BF16)
            dkc_acc[pl.ds(k0, tk), :] += _dot(ds, q_and_ones, TN_DIMS)
            dv_acc[pl.ds(k0, tk), :] += _dot(p.astype(BF16), do2, TN_DIMS)
            return dq_acc + _dot(ds, kblk), rs + jnp.sum(ds.astype(F32), axis=1, keepdims=True)

        init = (jnp.zeros((2 * tq, LANES), F32), jnp.zeros((2 * tq, 1), F32))
        dq_acc, rs = _sweep(qi, tq, tk, step, init)
        dq_ref[...] = (_unstack(dq_acc, masks) * SCALE).astype(BF16)
        drs_ref[...] = _unstack(rs, masks)

        @pl.when(qi == nq - 1)
        def _():
            dk_ref[...] = dkc_acc[:, 0:LANES].astype(BF16)
            dcs_ref[...] = dkc_acc[:, LANES:2 * LANES]
            dv_ref[...] = dv_acc[...].astype(BF16)

    q_spec, k_spec, v_spec = _att_specs(seq, 0, tq)
    qb = _qblock_spec(seq, tq)
    return _call_behind(
        body, behind, name="fox_bwd", grid=(batch, N_PAIRS, nq),
        in_specs=[q_spec, k_spec, v_spec, qb, pl.BlockSpec((N_HEADS, seq), lambda b, hp, qi: (b, 0)), qb, qb, qb],
        out_specs=[qb, _kv_out_spec(seq), _kv_out_spec(seq), _kv_out_spec(seq), qb],
        out_shape=[jax.ShapeDtypeStruct((batch * seq, D_ATT), BF16)] * 3 + [jax.ShapeDtypeStruct((batch * seq, D_ATT), F32)] * 2,
        scratch_shapes=[pltpu.VMEM((seq, 2 * LANES), F32), pltpu.VMEM((seq, LANES), F32)],
        operands=(qkv, qkv, qkv, c_wide, c_row, o, do, lse_wide))


def _sb_logits(q2, kblk):
    z = _dot(q2, kblk, NT_DIMS)
    lsn = jnp.minimum(-z, 0.0) - jnp.log(1.0 + jnp.exp(-jnp.abs(z)))
    return lsn + z, lsn


def _sb_bwd(qkv, do, rt_wide, batch, seq, behind):
    tq, tk = SB_TILES
    nq = seq // tq

    def body(q_ref, k_ref, v_ref, do_ref, rt_ref, dq_ref, dk_ref, dv_ref, dk_acc, dv_acc):
        qi = pl.program_id(2)

        @pl.when(qi == 0)
        def _():
            dk_acc[...] = jnp.zeros_like(dk_acc)
            dv_acc[...] = jnp.zeros_like(dv_acc)

        masks = _head_masks()
        ahead, _ = _stacked_ids(tq, tk)
        later = _tri2(tk, lambda r, c: r > c)
        earlier = _tri(tk, lambda r, c: r < c)
        q2 = _stack_heads(q_ref[...], masks) * SCALE
        do2 = _stack_heads(do_ref[...], masks)
        total = _stack_cols(rt_ref[...])

        def step(kb, carry, lead, top):
            pref, epre, dq_acc = (_below(t, top) for t in carry)
            q_s, do_s = _below(q2, top), _below(do2, top)
            seen = None if lead is None else _below(ahead, top) < lead
            k0 = pl.multiple_of(kb * tk, tk)
            kblk = k_ref[pl.ds(k0, tk), :]
            ls, lsn_all = _sb_logits(q_s, kblk)
            lsn = lsn_all if lead is None else jnp.where(seen, lsn_all, 0.0)
            rs = jnp.sum(lsn, axis=1, keepdims=True)
            w = jnp.exp(ls + _dot(_split2(lsn), later) + (_below(total, top) - pref - rs))
            if lead is not None:
                w = jnp.where(seen, w, 0.0)
            e = w * _dot(do_s, v_ref[pl.ds(k0, tk), :], NT_DIMS)
            before = _dot(e.astype(BF16), earlier) + epre
            dz = e * jnp.exp(lsn_all) - jnp.exp(ls) * before
            if lead is not None:
                dz = jnp.where(seen, dz, 0.0)
            dz = dz.astype(BF16)
            dk_acc[pl.ds(k0, tk), :] += _dot(dz, q_s, TN_DIMS)
            dv_acc[pl.ds(k0, tk), :] += _dot(w.astype(BF16), do_s, TN_DIMS)
            new = (pref + rs, epre + jnp.sum(e, axis=1, keepdims=True), dq_acc + _dot(dz, kblk))
            return tuple(_put_below(o, n, top) for o, n in zip(carry, new))

        init = (jnp.zeros((2 * tq, 1), F32), jnp.zeros((2 * tq, 1), F32), jnp.zeros((2 * tq, LANES), F32))
        dq_acc = _sweep(qi, tq, tk, step, init)[2]
        dq_ref[...] = (_unstack(dq_acc, masks) * SCALE).astype(BF16)

        @pl.when(qi == nq - 1)
        def _():
            dk_ref[...] = dk_acc[...].astype(BF16)
            dv_ref[...] = dv_acc[...].astype(BF16)

    q_spec, k_spec, v_spec = _att_specs(seq, 3 * N_PAIRS, tq)
    qb = _qblock_spec(seq, tq)
    return _call_behind(
        body, behind, name="sb_bwd", grid=(batch, N_PAIRS, nq), in_specs=[q_spec, k_spec, v_spec, qb, qb],
        out_specs=[qb, _kv_out_spec(seq), _kv_out_spec(seq)], out_shape=[jax.ShapeDtypeStruct((batch * seq, D_ATT), BF16)] * 3,
        scratch_shapes=[pltpu.VMEM((seq, LANES), F32), pltpu.VMEM((seq, LANES), F32)], operands=(qkv, qkv, qkv, do, rt_wide))


def _local_step(x, p, target, w, rest, vec, place):
    batch, seq, _ = x.shape
    t = batch * seq
    x = x.reshape(t, D_MODEL)
    target = target.reshape(t, D_MODEL)
    p = p.reshape(t, D_PLE)
    big = dict(tm=1024, tn=1024, tk=1024)

    h1 = _norm_fwd(x, vec["g_mix"], "norm_mix")
    qkv = _mm(h1, w["qkv"], mode="nn", name="proj_qkv", out_dtype=BF16, **big)
    gl = _mm(h1, w["gate"], mode="nn", name="proj_gate", **big)
    fl = _mm(h1, w["forget"], mode="nn", name="proj_forget", **big)
    c_wide, c_row = _fox_prep(fl, vec["b_forget"], batch, seq)
    (o_fox, lse_wide, o_sb, rt_wide), gathered = _att_fwd(qkv, c_wide, c_row, batch, seq, rest)
    w = dict(w, **_rest_weights(dict(zip(EARLY + ("b_gate",), gathered))))
    merged, of, os_ = _gate_fwd(gl, w["b_gate"], o_fox, o_sb, w["branch_fox"], w["branch_sb"])
    x1, h2 = _mm_res_norm(merged, w["out"], x, vec["g_mlp"], "proj_out_norm")
    ar = _mm(h2, w["up"], mode="nn", name="mlp_up", out_dtype=BF16, epi=lambda acc, _: jnp.maximum(acc, 0.0),
             col_shards=True, **big)
    x2, h3 = _mm_res_norm(ar, w["down"], x1, vec["g_ple"], "mlp_down_norm", a_fn=_relu2)

    dx3, dpre, dpe, dg_final, loss = _head_and_loss(x2, h3, p, w["ple_gate"], w["ple"], vec["g_final"], target)
    gw = {}
    gw["ple"] = _mm(p, dpe, mode="tn", name="d_w_ple", col_shards=True, **big)
    gw["ple_gate"] = _mm(h3, dpre, mode="tn", name="d_w_ple_gate", **big)
    dx2, dx2b, dg_ple = _mm_norm_bwd([(dpre, w["ple_gate"])], None, x2, vec["g_ple"], dx3, "d_h_ple_norm_bwd")
    gw["down"] = _mm(ar, dx2b, mode="tn", name="d_w_down", a_fn=_relu2, **big)
    da = _mm(dx2b, w["down"], mode="nt", name="d_act", out_dtype=BF16,
             epi=lambda acc, r: acc * (2.0 * r.astype(F32)), extra=ar, **big)
    gw["up"] = _mm(h2, da, mode="tn", name="d_w_up", col_shards=True, **big)
    dx1, dx1b, dg_mlp = _mm_norm_bwd([(da, w["up"])], None, x1, vec["g_mlp"], dx2, "d_h_mlp_norm_bwd")
    gw["out"] = _mm(merged, dx1b, mode="tn", name="d_w_out", **big)
    dof, dos, dgl, gw["b_gate"], do_fox, do_sb = _gate_bwd(gl, w["b_gate"], of, os_, dx1b, w["out"], w["branch_fox"],
                                                                  w["branch_sb"])
    gw["branch_fox"] = _mm(o_fox, dof, mode="tn", name="d_w_branch_fox", col_shards=True, **big)
    gw["branch_sb"] = _mm(o_sb, dos, mode="tn", name="d_w_branch_sb", col_shards=True, **big)
    early = _early_slots(gw)
    early = [early[n] for n in EARLY]
    (dq_a, dk_a, dv_a, dcs_wide, drs_wide), received = _fox_bwd(qkv, c_wide, c_row, o_fox, do_fox, lse_wide, batch, seq,
                                                                _swap_halves(early))
    sums = _sum_sibling(place, early, received, "sum_sibling_early")
    (dq_b, dk_b, dv_b), others = _sb_bwd(qkv, do_sb, rt_wide, batch, seq, _exchange_chips(sums))
    mine = [None] * len(EARLY)
    for group, tag in ((BIG, "big"), (SMALL, "small")):
        for t, res in zip(group, _sum_chips(place, *[[a[t] for t in group] for a in (early, received, others)], "sum_chips_" + tag)):
            mine[t] = res
    dfl, db_forget = _fox_post(dcs_wide, drs_wide, fl, vec["b_forget"], batch, seq)
    dqkv = jnp.concatenate([dq_a, dk_a, dv_a, dq_b, dk_b, dv_b], axis=1)
    gw["qkv"], theirs = _mm(dqkv, h1, mode="tn", name="d_w_qkv", behind=_share_halves(mine), flat_out=True, **big)
    reduced = dict(zip(EARLY, zip(mine, theirs)))
    gw["gate"] = _mm(dgl, h1, mode="tn", name="d_w_gate", flat_out=True, **big)
    gw["forget"] = _mm(dfl, h1, mode="tn", name="d_w_forget", flat_out=True, **big)
    late = [_w_in_slots(gw)]
    dh1, received = _mm(dqkv, w["qkv"], mode="nt", name="d_h_qkv", behind=_swap_halves(late), **big)
    sums = _sum_sibling(place, late, received, "sum_sibling_w_in")
    (grad_x, _, dg_mix), others = _mm_norm_bwd([(dgl, w["gate"]), (dfl, w["forget"])], dh1, x, vec["g_mix"], dx1,
                                               "d_h_gate_norm_bwd", behind=_exchange_chips(sums))
    mine = _sum_chips(place, late, received, others, "sum_chips_w_in")
    reduced["w_in"] = (mine[0], _run_exchange(_share_halves(mine), "reduce_share_w_in")[0])
    gvec = {"g_mix": dg_mix, "b_forget": db_forget[:, 0:N_HEADS], "g_mlp": dg_mlp, "g_ple": dg_ple,
            "g_final": dg_final, "b_gate": gw["b_gate"]}
    return loss, grad_x.reshape(batch, seq, D_MODEL), reduced, gvec


ANY = pl.BlockSpec(memory_space=pl.ANY)
SHARDED = ("w_in", "w_branch_fox", "w_branch_sb", "w_out", "w_up", "w_down", "w_ple_gate", "w_ple")
ROW_ALIGN = 16
F32_ROWS = 8


def _place():
    return lax.axis_index("x"), lax.axis_index("y"), lax.axis_index("c")


def _other_chips(x, y):
    return [(1 - x, y), (x, 1 - y), (1 - x, 1 - y)]


def _half(ref, h):
    r = ref.shape[0] // 2
    assert r % ROW_ALIGN == 0
    return ref.at[pl.ds(pl.multiple_of(h * r, ROW_ALIGN), r)]


def _remote(src, dst, sems, idx, to):
    send_sems, recv_sems = sems
    return pltpu.make_async_remote_copy(src_ref=src, dst_ref=dst, send_sem=send_sems.at[idx], recv_sem=recv_sems.at[idx],
                                        device_id=to, device_id_type=MESH)


class _Exchange:
    def __init__(self, operands, out_shapes, sem_shape, start, finish):
        self.operands, self.out_shapes, self.sem_shape, self.start, self.finish = operands, out_shapes, sem_shape, start, finish

    def scratch(self):
        return [pltpu.SemaphoreType.DMA(self.sem_shape), pltpu.SemaphoreType.DMA(self.sem_shape)]


def _run_exchange(ex, name):
    n = len(ex.operands)

    def body(*refs):
        ex.start(refs[:n], refs[n:2 * n], refs[2 * n:])
        ex.finish(refs[:n], refs[n:2 * n], refs[2 * n:])

    return pl.pallas_call(body, name=name, in_specs=[ANY] * n, out_specs=[ANY] * n, out_shape=ex.out_shapes,
                          scratch_shapes=ex.scratch())(*ex.operands)


def _call_behind(body, ex, *, name, grid, in_specs, out_specs, out_shape, scratch_shapes, operands):
    n_in, n_out, nx = len(in_specs), len(out_specs), len(ex.operands)

    def wrapped(*refs):
        ins, x_in = refs[:n_in], refs[n_in:n_in + nx]
        outs, x_out = refs[n_in + nx:n_in + nx + n_out], refs[n_in + nx + n_out:n_in + 2 * nx + n_out]
        scratch, sems = refs[n_in + 2 * nx + n_out:-2], refs[-2:]
        first, last = None, None
        for d, steps in enumerate(grid):
            at_start, at_end = pl.program_id(d) == 0, pl.program_id(d) == steps - 1
            first = at_start if first is None else first & at_start
            last = at_end if last is None else last & at_end

        @pl.when(first)
        def _():
            ex.start(x_in, x_out, sems)

        body(*ins, *outs, *scratch)

        @pl.when(last)
        def _():
            ex.finish(x_in, x_out, sems)

    res = pl.pallas_call(
        wrapped, name=name, grid=grid, in_specs=list(in_specs) + [ANY] * nx, out_specs=list(out_specs) + [ANY] * nx,
        out_shape=list(out_shape) + list(ex.out_shapes), scratch_shapes=list(scratch_shapes) + ex.scratch(),
        compiler_params=_cparams(("arbitrary",) * len(grid)),
    )(*operands, *ex.operands)
    return res[:n_out], res[n_out:]


def _gather_weights(shards):
    n = len(shards)

    def first_copies(src, out, sems):
        x, y, c = _place()
        me = 2 * x + y
        copies = [_remote(_half(src[t], c), _half(out[t].at[me], c), sems, (t, k), (px, py, c))
                  for t in range(n) for k, (px, py) in enumerate(_other_chips(x, y))]
        return copies + [_remote(src[t], out[t].at[me], sems, (t, 3), (x, y, 1 - c)) for t in range(n)]

    def start(src, out, sems):
        for cp in first_copies(src, out, sems):
            cp.start()

    def finish(src, out, sems):
        x, y, c = _place()
        me = 2 * x + y
        sibling = (x, y, 1 - c)
        chips = _other_chips(x, y)
        passes = []
        for t in range(n):
            for k, (px, py) in enumerate(chips):
                landed = _half(out[t].at[2 * px + py], c)
                _remote(landed, landed, sems, (t, k), (px, py, c)).wait_recv()
                passes.append(_remote(landed, landed, sems, (t, 4 + k), sibling))
                passes[-1].start()
        for t in range(n):
            _remote(src[t], out[t].at[me], sems, (t, 3), sibling).wait_recv()
            for k, (px, py) in enumerate(chips):
                passed = _half(out[t].at[2 * px + py], 1 - c)
                _remote(passed, passed, sems, (t, 4 + k), sibling).wait_recv()
        for cp in first_copies(src, out, sems) + passes:
            cp.wait_send()

    return _Exchange(shards, [jax.ShapeDtypeStruct((N_CHIPS,) + s.shape, s.dtype) for s in shards], (n, 7), start, finish)


def _simple_exchange(operands, out_shapes, copies):
    def start(src, out, sems):
        for cp in copies(src, out, sems):
            cp.start()

    def finish(src, out, sems):
        for cp in copies(src, out, sems):
            cp.wait_recv()
        for cp in copies(src, out, sems):
            cp.wait_send()

    return _Exchange(operands, out_shapes, (len(operands),), start, finish)


def _swap_halves(slots):
    def copies(src, out, sems):
        x, y, c = _place()
        res = []
        for t in range(len(slots)):
            r = src[t].shape[1] // 2
            rows = pl.ds(pl.multiple_of((1 - c) * r, F32_ROWS), r)
            res.append(_remote(src[t].at[:, rows], out[t], sems, t, (x, y, 1 - c)))
        return res

    return _simple_exchange(slots, [jax.ShapeDtypeStruct((N_CHIPS, s.shape[1] // 2, s.shape[2]), s.dtype) for s in slots], copies)


def _exchange_chips(sums):
    n = len(sums)

    def copies(src, out, sems):
        x, y, c = _place()
        return [_remote(src[t].at[2 * px + py], out[t].at[k], sems, (t, k), (px, py, c))
                for t in range(n) for k, (px, py) in enumerate(_other_chips(x, y))]

    def start(src, out, sems):
        for cp in copies(src, out, sems):
            cp.start()

    def finish(src, out, sems):
        for cp in copies(src, out, sems):
            cp.wait_recv()
        for cp in copies(src, out, sems):
            cp.wait_send()

    return _Exchange(sums, [jax.ShapeDtypeStruct((3,) + s.shape[1:], s.dtype) for s in sums], (n, 3), start, finish)


def _share_halves(mine):
    def copies(src, out, sems):
        x, y, c = _place()
        return [_remote(src[t], out[t], sems, t, (x, y, 1 - c)) for t in range(len(mine))]

    return _simple_exchange(mine, [jax.ShapeDtypeStruct(s.shape, s.dtype) for s in mine], copies)


def _walk(name, place, parts):
    starts = [sum(p[0] for p in parts[:t]) for t in range(len(parts))]
    held = lambda index, start, steps: (lambda s, pr: index(jnp.clip(s - start, 0, steps - 1), pr))
    in_specs, out_specs, out_shapes, operands = [], [], [], []
    for (steps, ins, outs, shapes, ops, _), start in zip(parts, starts):
        in_specs += [pl.BlockSpec(blk, held(index, start, steps)) for blk, index in ins]
        out_specs += [pl.BlockSpec(blk, held(index, start, steps)) for blk, index in outs]
        out_shapes += list(shapes)
        operands += list(ops)

    def body(place_ref, *refs):
        s = pl.program_id(0)
        i, o = 0, len(in_specs)
        for (steps, ins, outs, _, _, fn), start in zip(parts, starts):
            mine_in, mine_out = refs[i:i + len(ins)], refs[o:o + len(outs)]
            i, o = i + len(ins), o + len(outs)

            @pl.when((s >= start) & (s < start + steps))
            def _(mine_in=mine_in, mine_out=mine_out, start=start, fn=fn):
                fn(s - start, mine_in, mine_out)

    res = pl.pallas_call(
        body, name=name, out_shape=out_shapes,
        grid_spec=pltpu.PrefetchScalarGridSpec(num_scalar_prefetch=1, grid=(sum(p[0] for p in parts),), in_specs=in_specs,
                                               out_specs=out_specs),
        compiler_params=_cparams(("arbitrary",)),
    )(place, *operands)
    counts = [len(p[2]) for p in parts]
    return [res[sum(counts[:t]):sum(counts[:t + 1])] for t in range(len(parts))]


def _sum_sibling(place, slots, received, name):
    def part(slot, got):
        n, rows, cols = got.shape
        block = (None, rows, cols)

        def fn(j, ins, outs):
            outs[0][...] = (ins[0][...] + ins[1][...]).astype(BF16)

        return (n, [(block, lambda j, pr: (j, pr[1], 0)), (block, lambda j, pr: (j, 0, 0))], [(block, lambda j, pr: (j, 0, 0))],
                [jax.ShapeDtypeStruct(got.shape, BF16)], [slot, got], fn)

    return [r[0] for r in _walk(name, place, [part(s, g) for s, g in zip(slots, received)])]


def _sum_chips(place, slots, received, others, name):
    def part(slot, got, other):
        _, rows, cols = got.shape
        block = (None, rows, cols)

        def fn(_, ins, outs):
            own = ins[0][...] + ins[1][...]
            outs[0][...] = ((own + ins[2][0].astype(F32)) + ins[2][1].astype(F32)) + ins[2][2].astype(F32)

        return (1, [(block, lambda _, pr: (pr[0], pr[1], 0)), (block, lambda _, pr: (pr[0], 0, 0)),
                    ((3, rows, cols), lambda _, pr: (0, 0, 0))], [((rows, cols), lambda _, pr: (0, 0))],
                [jax.ShapeDtypeStruct((rows, cols), F32)], [slot, got, other], fn)

    return [r[0] for r in _walk(name, place, [part(*t) for t in zip(slots, received, others)])]


N_DEVICES = 8


def _sum_devices(block, name):
    def body(v_ref, o_ref, land_ref, send_sems, recv_sems):
        x, y, c = _place()
        me = 4 * x + 2 * y + c
        copies = []
        for mask in range(1, N_DEVICES):
            peer = (x ^ (mask >> 2), y ^ ((mask >> 1) & 1), c ^ (mask & 1))
            copies.append(pltpu.make_async_remote_copy(src_ref=v_ref, dst_ref=land_ref.at[me], send_sem=send_sems.at[mask - 1],
                                                       recv_sem=recv_sems.at[mask - 1], device_id=peer, device_id_type=MESH))
        for cp in copies:
            cp.start()
        land_ref[me] = v_ref[...]
        for cp in copies:
            cp.wait_recv()
        total = land_ref[0]
        for d in range(1, N_DEVICES):
            total = total + land_ref[d]
        o_ref[...] = total
        for cp in copies:
            cp.wait_send()

    vmem = pl.BlockSpec(memory_space=pltpu.VMEM)
    return pl.pallas_call(
        body, name=name, in_specs=[vmem], out_specs=vmem, out_shape=jax.ShapeDtypeStruct(block.shape, F32),
        scratch_shapes=[pltpu.VMEM((N_DEVICES,) + block.shape, F32), pltpu.SemaphoreType.DMA((N_DEVICES - 1,)),
                        pltpu.SemaphoreType.DMA((N_DEVICES - 1,))],
    )(block)


def _vec_block(g_mix, g_mlp, g_ple, g_final, b_forget, b_gate_rows, last=None):
    pad = lambda a: jnp.concatenate([a, jnp.zeros((a.shape[0], D_MODEL - a.shape[1]), F32)], axis=1)
    last = jnp.zeros((1, 0), F32) if last is None else last
    return jnp.concatenate([g_mix, g_mlp, g_ple, g_final.reshape(1, D_MODEL), pad(b_forget), pad(b_gate_rows), pad(last)],
                           axis=0)


def _adam_math(w, g, m, v):
    m_new = ADAM_B1 * m + (1.0 - ADAM_B1) * g
    v_new = ADAM_B2 * v + (1.0 - ADAM_B2) * (g * g)
    m_hat = m_new / (1.0 - ADAM_B1 ** ADAM_STEP)
    v_hat = v_new / (1.0 - ADAM_B2 ** ADAM_STEP)
    return -ADAM_LR * (m_hat / (jnp.sqrt(v_hat) + ADAM_EPS) + ADAM_WD * w), m_new, v_new


def _adamw_halves(place, weights, name):
    def part(w, m, v, g_mine, g_theirs):
        rows, cols = g_mine.shape
        whole = ((rows, cols), lambda s, pr: (pr[1] + s - 2 * pr[1] * s, 0))
        half = ((rows, cols), lambda s, pr: (0, 0))

        def fn(s, ins, outs):
            g = jnp.where(s == 0, ins[3][...], ins[4][...])
            outs[0][...] = g
            outs[1][...], outs[2][...], outs[3][...] = _adam_math(ins[0][...], g, ins[1][...], ins[2][...])

        return (2, [whole] * 3 + [half] * 2, [whole] * 4, [jax.ShapeDtypeStruct(w.shape, F32)] * 4, [w, m, v, g_mine, g_theirs], fn)

    return _walk(name, place, [part(*t) for t in weights])


def _adamw_vec(w, g, m, v):
    def body(w_ref, g_ref, m_ref, v_ref, d_ref, nm_ref, nv_ref):
        d_ref[...], nm_ref[...], nv_ref[...] = _adam_math(w_ref[...], g_ref[...], m_ref[...], v_ref[...])

    return pl.pallas_call(body, name="adamw_vectors", out_shape=[jax.ShapeDtypeStruct(w.shape, F32)] * 3)(w, g, m, v)


WEIGHT_NAMES = ("g_mix", "w_in", "b_forget", "b_gate", "w_branch_fox", "w_branch_sb", "w_out", "g_mlp", "w_up", "w_down",
                "g_ple", "w_ple_gate", "w_ple", "g_final")
W_IN_SHARD = D_IN // N_CHIPS
Q_END, F_END, B_END = 3 * D_ATT, 3 * D_ATT + N_HEADS, 6 * D_ATT + N_HEADS
GATE_SHARD = D_MODEL // N_CHIPS


LATE = SHARDED[:1]
EARLY = SHARDED[1:]
BIG = tuple(t for t, n in enumerate(EARLY) if n in ("w_up", "w_down"))
SMALL = tuple(t for t in range(len(EARLY)) if t not in BIG)


def _first_weights(w_in_slots):
    def cols(*ranges):
        parts = []
        for lo, hi in ranges:
            for j in range(N_CHIPS):
                a, b = max(lo, j * W_IN_SHARD), min(hi, (j + 1) * W_IN_SHARD)
                if a < b:
                    parts.append(w_in_slots[j, :, a - j * W_IN_SHARD:b - j * W_IN_SHARD])
        return parts

    forget = jnp.concatenate(cols((Q_END, F_END)) + [jnp.zeros((D_MODEL, F_PAD - N_HEADS), BF16)], axis=1)
    return {"qkv": jnp.concatenate(cols((0, Q_END), (F_END, B_END)), axis=1), "gate": jnp.concatenate(cols((B_END, D_IN)), axis=1),
            "forget": forget}


GATE_ROWS = 2 * ROW_ALIGN


def _gate_bits(b_gate):
    bits = lax.bitcast_convert_type(b_gate, BF16).reshape(2, 2 * GATE_SHARD)
    return jnp.concatenate([bits, jnp.zeros((GATE_ROWS - 2, 2 * GATE_SHARD), BF16)], axis=0)


def _rest_weights(gathered):
    rows = lambda a: a.reshape(N_CHIPS * a.shape[1], a.shape[2])
    bits = gathered["b_gate"][:, :2].reshape(N_CHIPS, 2, GATE_SHARD, 2)
    b_gate = jnp.transpose(lax.bitcast_convert_type(bits, F32), (1, 0, 2)).reshape(2, D_MODEL)
    return {"branch_fox": gathered["w_branch_fox"], "branch_sb": gathered["w_branch_sb"], "out": rows(gathered["w_out"]),
            "up": gathered["w_up"], "down": rows(gathered["w_down"]), "ple_gate": rows(gathered["w_ple_gate"]),
            "ple": gathered["w_ple"], "b_gate": b_gate}


def _early_slots(gw):
    rows = lambda a: a.reshape(N_CHIPS, a.shape[0] // N_CHIPS, a.shape[1])
    return {"w_branch_fox": gw["branch_fox"], "w_branch_sb": gw["branch_sb"], "w_out": rows(gw["out"]), "w_up": gw["up"],
            "w_down": rows(gw["down"]), "w_ple_gate": rows(gw["ple_gate"]), "w_ple": gw["ple"]}


W_IN_FLAT = (W_IN_SHARD * D_MODEL // LANES, LANES)


def _w_in_slots(gw):
    c = D_MODEL // LANES
    g_t = jnp.concatenate([gw["qkv"][:Q_END * c], gw["forget"][:N_HEADS * c], gw["qkv"][Q_END * c:], gw["gate"]], axis=0)
    return g_t.reshape((N_CHIPS,) + W_IN_FLAT)


def _flat(a):
    return jnp.transpose(a, (2, 0, 1)).reshape(W_IN_FLAT)


def _unflat(a):
    return jnp.transpose(a.reshape(W_IN_SHARD, D_MODEL // LANES, LANES), (1, 2, 0)).reshape(1, D_MODEL, W_IN_SHARD)


def kernel(x, p, g_mix, w_in, b_forget, b_gate, w_branch_fox, w_branch_sb, w_out, g_mlp, w_up, w_down, g_ple, w_ple_gate, w_ple, g_final, loss_target, m_g_mix, m_w_in, m_b_forget, m_b_gate, m_w_branch_fox, m_w_branch_sb, m_w_out, m_g_mlp, m_w_up, m_w_down, m_g_ple, m_w_ple_gate, m_w_ple, m_g_final, v_g_mix, v_w_in, v_b_forget, v_b_gate, v_w_branch_fox, v_w_branch_sb, v_w_out, v_g_mlp, v_w_up, v_w_down, v_g_ple, v_w_ple_gate, v_w_ple, v_g_final):
    weights = dict(g_mix=g_mix, w_in=w_in, b_forget=b_forget, b_gate=b_gate, w_branch_fox=w_branch_fox,
                   w_branch_sb=w_branch_sb, w_out=w_out, g_mlp=g_mlp, w_up=w_up, w_down=w_down, g_ple=g_ple,
                   w_ple_gate=w_ple_gate, w_ple=w_ple, g_final=g_final)
    first = dict(g_mix=m_g_mix, w_in=m_w_in, b_forget=m_b_forget, b_gate=m_b_gate, w_branch_fox=m_w_branch_fox,
                 w_branch_sb=m_w_branch_sb, w_out=m_w_out, g_mlp=m_g_mlp, w_up=m_w_up, w_down=m_w_down, g_ple=m_g_ple,
                 w_ple_gate=m_w_ple_gate, w_ple=m_w_ple, g_final=m_g_final)
    second = dict(g_mix=v_g_mix, w_in=v_w_in, b_forget=v_b_forget, b_gate=v_b_gate, w_branch_fox=v_w_branch_fox,
                  w_branch_sb=v_w_branch_sb, w_out=v_w_out, g_mlp=v_g_mlp, w_up=v_w_up, w_down=v_w_down, g_ple=v_g_ple,
                  w_ple_gate=v_w_ple_gate, w_ple=v_w_ple, g_final=v_g_final)
    cx, cy, cc = _place()
    chip = 2 * cx + cy
    place = jnp.stack([chip, cc]).astype(jnp.int32)
    col0 = chip * GATE_SHARD

    (w_in_slots,) = _run_exchange(_gather_weights([weights[n][0].astype(BF16) for n in LATE]), "gather_w_in")
    rest = _gather_weights([weights[n][0].astype(BF16) for n in EARLY] + [_gate_bits(b_gate[0])])
    vec = {"g_mix": g_mix, "b_forget": jnp.concatenate([b_forget, jnp.zeros((1, F_PAD - N_HEADS), F32)], axis=1),
           "g_mlp": g_mlp, "g_ple": g_ple, "g_final": g_final.reshape(1, D_MODEL)}

    loss, grad_x, reduced, gvec = _local_step(x, p[0], loss_target, _first_weights(w_in_slots), rest, vec, place)

    out = {}
    args = lambda n: (weights[n][0], first[n][0], second[n][0]) + tuple(reduced[n])
    for names, tag in [([EARLY[t] for t in SMALL], "small")] + [([EARLY[t]], EARLY[t]) for t in BIG]:
        for n, res in zip(names, _adamw_halves(place, [args(n) for n in names], "adamw_" + tag)):
            out[n] = [r[None] for r in res]
    (res,) = _adamw_halves(place, [(_flat(w_in), _flat(m_w_in), _flat(v_w_in)) + tuple(reduced["w_in"])], "adamw_w_in")
    out["w_in"] = [_unflat(r) for r in res]

    g_block = _sum_devices(_vec_block(gvec["g_mix"], gvec["g_mlp"], gvec["g_ple"], gvec["g_final"][0], gvec["b_forget"],
                                      gvec["b_gate"], loss), "reduce_vectors")
    loss = g_block[7, 0]
    g_gate = lax.dynamic_slice(g_block[5:7], (0, col0), (2, GATE_SHARD))
    blocks = [_vec_block(d["g_mix"], d["g_mlp"], d["g_ple"], d["g_final"], d["b_forget"], d["b_gate"][0])
              for d in (weights, first, second)]
    g_rows = jnp.concatenate([g_block[0:5], jnp.concatenate([g_gate, jnp.zeros((2, D_MODEL - GATE_SHARD), F32)], axis=1),
                              jnp.zeros((1, D_MODEL), F32)], axis=0)
    res = (g_rows,) + tuple(_adamw_vec(blocks[0], g_rows, blocks[1], blocks[2]))
    out["g_mix"] = [r[0:1] for r in res]
    out["g_mlp"] = [r[1:2] for r in res]
    out["g_ple"] = [r[2:3] for r in res]
    out["g_final"] = [r[3] for r in res]
    out["b_forget"] = [r[4:5, :N_HEADS] for r in res]
    out["b_gate"] = [r[5:7, :GATE_SHARD][None] for r in res]
    return (loss, grad_x, *[out[n][0] for n in WEIGHT_NAMES], *[out[n][1] for n in WEIGHT_NAMES],
            *[out[n][2] for n in WEIGHT_NAMES], *[out[n][3] for n in WEIGHT_NAMES])
```

```python
import jax
import jax.numpy as jnp
from jax import lax
from jax.experimental import pallas as pl
from jax.experimental.pallas import tpu as pltpu

F32 = jnp.float32
BF16 = jnp.bfloat16

D_MODEL = 1024
HEAD_DIM = 64
N_HEADS = 8
D_ATT = N_HEADS * HEAD_DIM
D_PLE = 256
D_IN = 6 * D_ATT + N_HEADS + 2 * D_MODEL
F_PAD = 128
EPS = 1e-6
SCALE = HEAD_DIM ** -0.5
N_CHIPS = 4
LANES = 128
ATT_BLOCK = 256
FOX_TILES = (512, 512)
SB_TILES = (512, 256)
NEG = -1e30

ADAM_LR = 0.001
ADAM_B1 = 0.9
ADAM_B2 = 0.999
ADAM_EPS = 1e-08
ADAM_WD = 0.01
ADAM_STEP = 10

VMEM_LIMIT = 56 * 1024 * 1024

MESH = pl.DeviceIdType.MESH


def _cparams(sem=None):
    return pltpu.CompilerParams(dimension_semantics=sem, vmem_limit_bytes=VMEM_LIMIT)


def _relu2(t):
    t = t.astype(F32)
    return t * t


_DIMS = {"nn": (((1,), (0,)), ((), ())), "nt": (((1,), (1,)), ((), ())), "tn": (((0,), (0,)), ((), ()))}
NT_DIMS = _DIMS["nt"]
TN_DIMS = _DIMS["tn"]


def _mm(a, b, *, mode, name, out_dtype=F32, tm=512, tn=512, tk=512, add=None, a_fn=None, epi=None, extra=None,
        col_shards=False, behind=None, flat_out=False):
    if mode == "nn":
        (m, k), n = a.shape, b.shape[-1]
    elif mode == "nt":
        (m, k), n = a.shape, b.shape[-2]
    else:
        (k, m), n = a.shape, b.shape[1]
    shard = None
    if col_shards:
        if mode == "nn":
            shard, n = n, N_CHIPS * n
            tn = min(tn, shard)
        elif mode == "nt":
            shard = b.shape[-1]
            tk = min(tk, shard)
        else:
            shard = n // N_CHIPS
            tn = min(tn, shard)
    tm, tn, tk = min(tm, m), min(tn, n), min(tk, k)
    assert m % tm == 0 and n % tn == 0 and k % tk == 0, (name, m, n, k)
    nk = k // tk
    a_spec = {"nn": pl.BlockSpec((tm, tk), lambda i, j, kk: (i, kk)),
              "nt": pl.BlockSpec((tm, tk), lambda i, j, kk: (i, kk)),
              "tn": pl.BlockSpec((tk, tm), lambda i, j, kk: (kk, i))}[mode]
    b_spec = {"nn": pl.BlockSpec((tk, tn), lambda i, j, kk: (kk, j)),
              "nt": pl.BlockSpec((tn, tk), lambda i, j, kk: (j, kk)),
              "tn": pl.BlockSpec((tk, tn), lambda i, j, kk: (kk, j))}[mode]
    o_spec = pl.BlockSpec((tm, tn), lambda i, j, kk: (i, j))
    out_shape = (m, n)
    if col_shards and mode == "nn":
        per = shard // tn
        b_spec = pl.BlockSpec((None, tk, tn), lambda i, j, kk: (j // per, kk, j % per))
    elif col_shards and mode == "nt":
        per = shard // tk
        b_spec = pl.BlockSpec((None, tn, tk), lambda i, j, kk: (kk // per, j, kk % per))
    elif col_shards:
        assert add is None and extra is None
        per = shard // tn
        o_spec = pl.BlockSpec((None, tm, tn), lambda i, j, kk: (j // per, i, j % per))
        out_shape = (N_CHIPS, m, shard)
    if flat_out:
        assert mode == "tn" and tn == n == D_MODEL and not col_shards and add is None and extra is None
        chunks = D_MODEL // LANES
        o_spec = pl.BlockSpec((tm * chunks, LANES), lambda i, j, kk: (i, 0))
        out_shape = (m * chunks, LANES)
    operands, in_specs = [a, b], [a_spec, b_spec]
    third = add if add is not None else extra
    if third is not None:
        operands.append(third)
        in_specs.append(o_spec)

    def body(*refs):
        a_ref, b_ref = refs[0], refs[1]
        t_ref = refs[2] if third is not None else None
        o_ref = refs[3] if third is not None else refs[2]
        acc_ref = refs[-1] if nk > 1 else None
        at = a_ref[...]
        if a_fn is not None:
            at = a_fn(at)
        part = lax.dot_general(at.astype(BF16), b_ref[...].astype(BF16), _DIMS[mode], preferred_element_type=F32)

        def finish(acc):
            if epi is not None:
                acc = epi(acc, None if t_ref is None else t_ref[...])
            elif add is not None:
                acc = acc + t_ref[...].astype(F32)
            if flat_out:
                for q in range(D_MODEL // LANES):
                    o_ref[pl.ds(q, tm, stride=D_MODEL // LANES), :] = acc[:, q * LANES:(q + 1) * LANES].astype(o_ref.dtype)
                return
            o_ref[...] = acc.astype(o_ref.dtype)

        if nk == 1:
            finish(part)
        else:
            kk = pl.program_id(2)

            @pl.when(kk == 0)
            def _():
                acc_ref[...] = part

            @pl.when(kk > 0)
            def _():
                acc_ref[...] += part

            @pl.when(kk == nk - 1)
            def _():
                finish(acc_ref[...])

    call = dict(name=name, grid=(m // tm, n // tn, nk), in_specs=in_specs,
                scratch_shapes=[pltpu.VMEM((tm, tn), F32)] if nk > 1 else [])
    if behind is not None:
        (res,), exchanged = _call_behind(body, behind, out_specs=[o_spec], out_shape=[jax.ShapeDtypeStruct(out_shape, out_dtype)],
                                         operands=operands, **call)
        return res, exchanged
    return pl.pallas_call(body, out_specs=o_spec, out_shape=jax.ShapeDtypeStruct(out_shape, out_dtype),
                          compiler_params=_cparams(("parallel", "parallel", "arbitrary")), **call)(*operands)


ROW_TILE = 512


def _row_spec(width=D_MODEL, rows=ROW_TILE):
    return pl.BlockSpec((rows, width), lambda i: (i, 0))


def _vec_spec(rows=1, width=D_MODEL):
    return pl.BlockSpec((rows, width), lambda i: (0, 0))


def _xhat(x):
    r = lax.rsqrt(jnp.mean(x * x, axis=-1, keepdims=True) + EPS)
    return x * r, r


def _rms_bwd_rows(dh, x, g):
    xh, r = _xhat(x)
    dxh = dh * g
    dx = r * (dxh - xh * jnp.mean(dxh * xh, axis=-1, keepdims=True))
    return dx, jnp.sum(dh * xh, axis=0, keepdims=True)


def _norm_fwd(x, g, name, behind):
    t = x.shape[0]

    def body(x_ref, g_ref, h_ref):
        xh, _ = _xhat(x_ref[...])
        h_ref[...] = (xh * g_ref[...]).astype(BF16)

    return _call_behind(body, behind, name=name, grid=(t // ROW_TILE,), in_specs=[_row_spec(), _vec_spec()],
                        out_specs=[_row_spec()], out_shape=[jax.ShapeDtypeStruct((t, D_MODEL), BF16)], scratch_shapes=[],
                        operands=(x, g))


def _mm_res_norm(a, b, res, g, name, a_fn=None):
    t, k = a.shape

    def body(a_ref, b_ref, res_ref, g_ref, x_ref, h_ref):
        at = a_ref[...] if a_fn is None else a_fn(a_ref[...])
        x_new = res_ref[...] + _dot(at.astype(BF16), b_ref[...])
        x_ref[...] = x_new
        h_ref[...] = (_xhat(x_new)[0] * g_ref[...]).astype(BF16)

    return pl.pallas_call(
        body, name=name, grid=(t // ROW_TILE,),
        in_specs=[pl.BlockSpec((ROW_TILE, k), lambda i: (i, 0)), pl.BlockSpec(b.shape, lambda i: (0, 0)), _row_spec(), _vec_spec()],
        out_specs=[_row_spec(), _row_spec()],
        out_shape=[jax.ShapeDtypeStruct((t, D_MODEL), F32), jax.ShapeDtypeStruct((t, D_MODEL), BF16)],
        compiler_params=_cparams(("parallel",)),
    )(a, b, res, g)


def _mm_norm_bwd(pairs, dh_first, x, g, dres, name, behind=None):
    t = x.shape[0]
    operands, in_specs = [], []
    for a, b in pairs:
        if b.ndim == 3:
            for j in range(b.shape[0]):
                operands += [a, b]
                in_specs += [pl.BlockSpec((ROW_TILE, b.shape[2]), lambda i, j=j: (i, j)),
                             pl.BlockSpec((None, D_MODEL, b.shape[2]), lambda i, j=j: (j, 0, 0))]
        else:
            operands += [a, b]
            in_specs += [pl.BlockSpec((ROW_TILE, a.shape[1]), lambda i: (i, 0)), pl.BlockSpec(b.shape, lambda i: (0, 0))]
    n_mm = len(operands)
    operands += [x, g, dres] + ([] if dh_first is None else [dh_first])
    in_specs += [_row_spec(), _vec_spec(), _row_spec()] + ([] if dh_first is None else [_row_spec()])

    def body(*refs):
        x_ref, g_ref, dres_ref = refs[n_mm:n_mm + 3]
        dx_ref, dxb_ref, dg_ref = refs[-3:]
        dh = 0.0 if dh_first is None else refs[n_mm + 3][...]
        for k in range(0, n_mm, 2):
            dh = dh + lax.dot_general(refs[k][...].astype(BF16), refs[k + 1][...].astype(BF16), NT_DIMS,
                                      preferred_element_type=F32)
        dx, dg = _rms_bwd_rows(dh, x_ref[...], g_ref[...])
        dx = dx + dres_ref[...]
        dx_ref[...] = dx
        dxb_ref[...] = dx.astype(BF16)

        @pl.when(pl.program_id(0) == 0)
        def _():
            dg_ref[...] = jnp.zeros_like(dg_ref)

        dg_ref[...] += dg

    call = dict(name=name, grid=(t // ROW_TILE,), in_specs=in_specs, out_specs=[_row_spec(), _row_spec(), _vec_spec()],
                out_shape=[jax.ShapeDtypeStruct((t, D_MODEL), F32), jax.ShapeDtypeStruct((t, D_MODEL), BF16),
                           jax.ShapeDtypeStruct((1, D_MODEL), F32)])
    if behind is not None:
        return _call_behind(body, behind, scratch_shapes=[], operands=operands, **call)
    return pl.pallas_call(body, compiler_params=_cparams(("arbitrary",)), **call)(*operands)


def _shards_spec(w):
    return pl.BlockSpec(w.shape, lambda i: (0, 0, 0))


def _gate_fwd(gl, b_gate, o_fox, o_sb, w_fox, w_sb):
    t = o_fox.shape[0]

    def body(gla_ref, glb_ref, b_ref, ofox_ref, osb_ref, wf_ref, ws_ref, m_ref, of_ref, os_ref):
        of = jnp.concatenate([_dot(ofox_ref[...], wf_ref[j]) for j in range(N_CHIPS)], axis=1)
        os_ = jnp.concatenate([_dot(osb_ref[...], ws_ref[j]) for j in range(N_CHIPS)], axis=1)
        ga = jax.nn.sigmoid(gla_ref[...] + b_ref[0:1, :])
        gb = jax.nn.sigmoid(glb_ref[...] + b_ref[1:2, :])
        of_ref[...] = of
        os_ref[...] = os_
        m_ref[...] = (ga * of + gb * os_).astype(BF16)

    return pl.pallas_call(
        body, name="gate_fwd", grid=(t // ROW_TILE,),
        in_specs=[pl.BlockSpec((ROW_TILE, D_MODEL), lambda i: (i, 0)), pl.BlockSpec((ROW_TILE, D_MODEL), lambda i: (i, 1)),
                  _vec_spec(2), _row_spec(D_ATT), _row_spec(D_ATT), _shards_spec(w_fox), _shards_spec(w_sb)],
        out_specs=[_row_spec(), _row_spec(), _row_spec()],
        out_shape=[jax.ShapeDtypeStruct((t, D_MODEL), BF16)] + [jax.ShapeDtypeStruct((t, D_MODEL), F32)] * 2,
        compiler_params=_cparams(("parallel",)),
    )(gl, gl, b_gate, o_fox, o_sb, w_fox, w_sb)


def _gate_bwd(gl, b_gate, of, os_, dx, w_out, w_fox, w_sb):
    t = of.shape[0]
    shard = D_MODEL // N_CHIPS

    def back(d, w_ref):
        return sum(_dot(d[:, j * shard:(j + 1) * shard], w_ref[j], NT_DIMS) for j in range(N_CHIPS)).astype(BF16)

    def body(gla_ref, glb_ref, b_ref, of_ref, os_ref, dx_ref, w_ref, wf_ref, ws_ref,
             dof_ref, dos_ref, dgl_ref, db_ref, dofox_ref, dosb_ref):
        dm = _dot(dx_ref[...], w_ref[...], NT_DIMS)
        ga = jax.nn.sigmoid(gla_ref[...] + b_ref[0:1, :])
        gb = jax.nn.sigmoid(glb_ref[...] + b_ref[1:2, :])
        dof = (dm * ga).astype(BF16)
        dos = (dm * gb).astype(BF16)
        dof_ref[...] = dof
        dos_ref[...] = dos
        dofox_ref[...] = back(dof, wf_ref)
        dosb_ref[...] = back(dos, ws_ref)
        dgla = dm * of_ref[...] * ga * (1.0 - ga)
        dglb = dm * os_ref[...] * gb * (1.0 - gb)
        dgl_ref[:, 0:D_MODEL] = dgla.astype(BF16)
        dgl_ref[:, D_MODEL:2 * D_MODEL] = dglb.astype(BF16)

        @pl.when(pl.program_id(0) == 0)
        def _():
            db_ref[...] = jnp.zeros_like(db_ref)

        db_ref[0:1, :] += jnp.sum(dgla, axis=0, keepdims=True)
        db_ref[1:2, :] += jnp.sum(dglb, axis=0, keepdims=True)

    outs = pl.pallas_call(
        body, name="gate_bwd", grid=(t // ROW_TILE,),
        in_specs=[pl.BlockSpec((ROW_TILE, D_MODEL), lambda i: (i, 0)), pl.BlockSpec((ROW_TILE, D_MODEL), lambda i: (i, 1)),
                  _vec_spec(2), _row_spec(), _row_spec(), _row_spec(), pl.BlockSpec(w_out.shape, lambda i: (0, 0)),
                  _shards_spec(w_fox), _shards_spec(w_sb)],
        out_specs=[_row_spec(), _row_spec(), _row_spec(2 * D_MODEL), _vec_spec(2), _row_spec(D_ATT), _row_spec(D_ATT)],
        out_shape=[jax.ShapeDtypeStruct((t, D_MODEL), BF16)] * 2 + [jax.ShapeDtypeStruct((t, 2 * D_MODEL), BF16),
                                                                      jax.ShapeDtypeStruct((2, D_MODEL), F32)]
        + [jax.ShapeDtypeStruct((t, D_ATT), BF16)] * 2,
        compiler_params=_cparams(("arbitrary",)),
    )(gl, gl, b_gate, of, os_, dx, w_out, w_fox, w_sb)
    return outs


def _head_and_loss(x2, h3, p, w_gate, w_ple, g_final, target):
    t = x2.shape[0]

    def body(x2_ref, h3_ref, p_ref, wg_ref, wp_ref, g_ref, tgt_ref, dx3_ref, dpre_ref, dpe_ref, dg_ref, loss_ref):
        gp = jax.nn.sigmoid(_dot(h3_ref[...], wg_ref[...]))
        p_t = p_ref[...].astype(BF16)
        pe_t = jnp.concatenate([_dot(p_t, wp_ref[j]) for j in range(N_CHIPS)], axis=1)
        x3 = x2_ref[...] + gp * pe_t
        g = g_ref[...]
        xh, _ = _xhat(x3)
        err = xh * g - tgt_ref[...]
        dy = err * (1.0 / D_MODEL)
        dx3, dg = _rms_bwd_rows(dy, x3, g)
        dx3_ref[...] = dx3
        dpre_ref[...] = (dx3 * pe_t * gp * (1.0 - gp)).astype(BF16)
        dpe_ref[...] = (dx3 * gp).astype(BF16)

        @pl.when(pl.program_id(0) == 0)
        def _():
            dg_ref[...] = jnp.zeros_like(dg_ref)
            loss_ref[...] = jnp.zeros_like(loss_ref)

        dg_ref[...] += dg
        loss_ref[...] += 0.5 * jnp.sum(jnp.mean(err * err, axis=-1, keepdims=True), axis=0, keepdims=True)

    return pl.pallas_call(
        body, name="head_and_loss", grid=(t // ROW_TILE,),
        in_specs=[_row_spec(), _row_spec(), _row_spec(D_PLE), pl.BlockSpec(w_gate.shape, lambda i: (0, 0)),
                  pl.BlockSpec(w_ple.shape, lambda i: (0, 0, 0)), _vec_spec(), _row_spec()],
        out_specs=[_row_spec(), _row_spec(), _row_spec(), _vec_spec(), _vec_spec(1, LANES)],
        out_shape=[jax.ShapeDtypeStruct((t, D_MODEL), F32), jax.ShapeDtypeStruct((t, D_MODEL), BF16),
                   jax.ShapeDtypeStruct((t, D_MODEL), BF16), jax.ShapeDtypeStruct((1, D_MODEL), F32),
                   jax.ShapeDtypeStruct((1, LANES), F32)],
        compiler_params=_cparams(("arbitrary",)),
    )(x2, h3, p, w_gate, w_ple, g_final, target)


def _split3(v):
    hi = v.astype(BF16)
    r1 = v - hi.astype(F32)
    mid = r1.astype(BF16)
    lo = (r1 - mid.astype(F32)).astype(BF16)
    return hi, mid, lo


def _split2(v):
    hi = v.astype(BF16)
    return jnp.concatenate([hi, (v - hi.astype(F32)).astype(BF16)], axis=1)


def _dot(a, b, dims=_DIMS["nn"]):
    return lax.dot_general(a, b, dims, preferred_element_type=F32)


def _tri(n, rel):
    row = lax.broadcasted_iota(jnp.int32, (n, n), 0)
    col = lax.broadcasted_iota(jnp.int32, (n, n), 1)
    return rel(row, col).astype(BF16)


def _tri2(n, rel):
    t = _tri(n, rel)
    return jnp.concatenate([t, t], axis=0)


def _log_sigmoid(v):
    return -(jnp.maximum(-v, 0.0) + jnp.log(1.0 + jnp.exp(-jnp.abs(v))))


def _fox_prep(fl, b_forget, batch, seq):
    nb = seq // ATT_BLOCK

    def body(fl_ref, b_ref, cw_ref, cr_ref):
        col = lax.broadcasted_iota(jnp.int32, (ATT_BLOCK, F_PAD), 1)
        lower = _tri(ATT_BLOCK, lambda r, c: c <= r)
        upper = _tri(ATT_BLOCK, lambda r, c: r <= c)
        expand = (lax.broadcasted_iota(jnp.int32, (F_PAD, D_ATT), 1) // HEAD_DIM
                  == lax.broadcasted_iota(jnp.int32, (F_PAD, D_ATT), 0)).astype(BF16)
        carry_w = jnp.zeros((1, D_ATT), F32)
        carry_r = jnp.zeros((F_PAD, 1), F32)
        for i in range(nb):
            blk = slice(i * ATT_BLOCK, (i + 1) * ATT_BLOCK)
            logf = jnp.where(col < N_HEADS, _log_sigmoid(fl_ref[blk, :] + b_ref[...]), 0.0)
            cw = jnp.zeros((ATT_BLOCK, D_ATT), F32) + carry_w
            cr = jnp.zeros((F_PAD, ATT_BLOCK), F32) + carry_r
            for part in _split3(logf):
                cw += _dot(lower, _dot(part, expand).astype(BF16))
                cr += _dot(part, upper, TN_DIMS)
            cw_ref[blk, :] = cw
            cr_ref[:, blk] = cr[0:N_HEADS, :]
            carry_w = cw[ATT_BLOCK - 1:ATT_BLOCK, :]
            carry_r = cr[:, ATT_BLOCK - 1:ATT_BLOCK]

    return pl.pallas_call(
        body, name="fox_prep", grid=(batch,),
        in_specs=[pl.BlockSpec((seq, F_PAD), lambda b: (b, 0)), pl.BlockSpec((1, F_PAD), lambda b: (0, 0))],
        out_specs=[pl.BlockSpec((seq, D_ATT), lambda b: (b, 0)), pl.BlockSpec((N_HEADS, seq), lambda b: (b, 0))],
        out_shape=[jax.ShapeDtypeStruct((batch * seq, D_ATT), F32), jax.ShapeDtypeStruct((batch * N_HEADS, seq), F32)],
        compiler_params=_cparams(("parallel",)),
    )(fl, b_forget)


def _fox_post(dcs_wide, drs_wide, fl, b_forget, batch, seq):
    nb = seq // ATT_BLOCK

    def body(dcs_ref, drs_ref, fl_ref, b_ref, dfl_ref, db_ref):
        pick = (lax.broadcasted_iota(jnp.int32, (D_ATT, F_PAD), 0)
                == lax.broadcasted_iota(jnp.int32, (D_ATT, F_PAD), 1) * HEAD_DIM).astype(BF16)
        upper = _tri(ATT_BLOCK, lambda r, c: r <= c)
        col = lax.broadcasted_iota(jnp.int32, (ATT_BLOCK, F_PAD), 1)

        @pl.when(pl.program_id(0) == 0)
        def _():
            db_ref[...] = jnp.zeros_like(db_ref)

        carry = jnp.zeros((1, F_PAD), F32)
        for i in reversed(range(nb)):
            blk = slice(i * ATT_BLOCK, (i + 1) * ATT_BLOCK)
            narrow = jnp.zeros((ATT_BLOCK, F_PAD), F32)
            for part in _split3(drs_ref[blk, :] - dcs_ref[blk, :]):
                narrow += _dot(part, pick)
            after = jnp.zeros((ATT_BLOCK, F_PAD), F32) + carry
            for part in _split3(narrow):
                after += _dot(upper, part)
            carry = after[0:1, :]
            pre = fl_ref[blk, :] + b_ref[...]
            dfl = jnp.where(col < N_HEADS, after * jax.nn.sigmoid(-pre), 0.0)
            dfl_ref[blk, :] = dfl.astype(BF16)
            db_ref[...] += jnp.sum(dfl, axis=0, keepdims=True)

    return pl.pallas_call(
        body, name="fox_post", grid=(batch,),
        in_specs=[pl.BlockSpec((seq, D_ATT), lambda b: (b, 0)), pl.BlockSpec((seq, D_ATT), lambda b: (b, 0)),
                  pl.BlockSpec((seq, F_PAD), lambda b: (b, 0)), pl.BlockSpec((1, F_PAD), lambda b: (0, 0))],
        out_specs=[pl.BlockSpec((seq, F_PAD), lambda b: (b, 0)), pl.BlockSpec((1, F_PAD), lambda b: (0, 0))],
        out_shape=[jax.ShapeDtypeStruct((batch * seq, F_PAD), BF16), jax.ShapeDtypeStruct((1, F_PAD), F32)],
        compiler_params=_cparams(("arbitrary",)),
    )(dcs_wide, drs_wide, fl, b_forget)


N_PAIRS = N_HEADS // 2


def _att_specs(seq, col0, tq):
    nq = seq // tq
    q = pl.BlockSpec((tq, LANES), lambda b, hp, qi: (b * nq + qi, col0 + hp))
    k = pl.BlockSpec((seq, LANES), lambda b, hp, qi: (b, col0 + N_PAIRS + hp))
    v = pl.BlockSpec((seq, LANES), lambda b, hp, qi: (b, col0 + 2 * N_PAIRS + hp))
    return q, k, v


def _qblock_spec(seq, tq):
    nq = seq // tq
    return pl.BlockSpec((tq, LANES), lambda b, hp, qi: (b * nq + qi, hp))


def _kv_out_spec(seq):
    return pl.BlockSpec((seq, LANES), lambda b, hp, qi: (b, hp))


def _head_masks():
    lane = lax.broadcasted_iota(jnp.int32, (1, LANES), 1)
    return [(lane >= HEAD_DIM * j) & (lane < HEAD_DIM * (j + 1)) for j in range(2)]


def _stack_heads(t, masks):
    zero = jnp.zeros_like(t)
    return jnp.concatenate([jnp.where(masks[0], t, zero), jnp.where(masks[1], t, zero)], axis=0)


def _stack_cols(t):
    return jnp.concatenate([t[:, 0:1], t[:, HEAD_DIM:HEAD_DIM + 1]], axis=0)


def _unstack(t2, masks):
    tq = t2.shape[0] // 2
    return jnp.where(masks[0], t2[:tq], t2[tq:])


def _stacked_ids(tq, tk):
    row = lax.broadcasted_iota(jnp.int32, (2 * tq, tk), 0)
    col = lax.broadcasted_iota(jnp.int32, (2 * tq, tk), 1)
    first = lax.broadcasted_iota(jnp.int32, (2 * tq, 1), 0) < tq
    return col - jnp.where(row < tq, row, row - tq), first


def _sweep(qi, tq, tk, step, init):
    per = tq // tk
    carry = lax.fori_loop(0, per * qi, lambda kb, c: step(kb, c, None, 0), init)
    for j in range(per):
        carry = step(per * qi + j, carry, -j * tk, j * tk)
    return carry


def _below(t2, top):
    tq = t2.shape[0] // 2
    return t2 if top == 0 else jnp.concatenate([t2[top:tq], t2[tq + top:]], axis=0)


def _put_below(old, new, top):
    if top == 0:
        return new
    tq = old.shape[0] // 2
    return jnp.concatenate([old[:top], new[:tq - top], old[tq:tq + top], new[tq - top:]], axis=0)


def _att_fwd(qkv, c_wide, c_row, batch, seq, behind):
    tq, tkf = FOX_TILES
    tqs, tks = SB_TILES
    assert tq == tqs and tkf == 2 * tks
    nq = seq // tq

    def body(qa_ref, ka_ref, va_ref, cw_ref, cr_ref, qb_ref, kb_ref, vb_ref, of_ref, lse_ref, os_ref, rt_ref):
        hp, qi = pl.program_id(1), pl.program_id(2)
        masks = _head_masks()
        ahead_f, first = _stacked_ids(tq, tkf)
        ahead_s, _ = _stacked_ids(tq, tks)
        later = _tri2(tks, lambda r, c: r > c)
        q2f = _stack_heads(qa_ref[...], masks) * SCALE
        q2s = _stack_heads(qb_ref[...], masks) * SCALE
        ct = _stack_cols(cw_ref[...])

        def fox(kb, carry, lead):
            m, l, acc = carry
            k0 = pl.multiple_of(kb * tkf, tkf)
            cs = jnp.where(first, cr_ref[pl.ds(2 * hp, 1), pl.ds(k0, tkf)], cr_ref[pl.ds(2 * hp + 1, 1), pl.ds(k0, tkf)])
            s = _dot(q2f, ka_ref[pl.ds(k0, tkf), :], NT_DIMS) + ct - cs
            if lead is not None:
                s = jnp.where(ahead_f <= lead, s, NEG)
            m_new = jnp.maximum(m, jnp.max(s, axis=1, keepdims=True))
            p = jnp.exp(s - m_new)
            alpha = jnp.exp(m - m_new)
            l = alpha * l + jnp.sum(p, axis=1, keepdims=True)
            acc = alpha * acc + _dot(p.astype(BF16), va_ref[pl.ds(k0, tkf), :])
            return m_new, l, acc

        def sb(kb, carry, lead):
            run, acc = carry
            k0 = pl.multiple_of(kb * tks, tks)
            ls, lsn = _sb_logits(q2s, kb_ref[pl.ds(k0, tks), :])
            if lead is not None:
                lsn = jnp.where(ahead_s < lead, lsn, 0.0)
            w = jnp.exp(ls + _dot(_split2(lsn), later) + run)
            if lead is not None:
                w = jnp.where(ahead_s < lead, w, 0.0)
            return run + jnp.sum(lsn, axis=1, keepdims=True), acc + _dot(w.astype(BF16), vb_ref[pl.ds(k0, tks), :])

        fox_c = (jnp.full((2 * tq, 1), NEG, F32), jnp.zeros((2 * tq, 1), F32), jnp.zeros((2 * tq, LANES), F32))
        sb_c = (jnp.zeros((2 * tq, 1), F32), jnp.zeros((2 * tq, LANES), F32))
        sb_c = sb(2 * qi, sb(2 * qi + 1, sb_c, -tks), 0)

        def both(i, carries):
            fox_c, sb_c = carries
            return fox(i, fox_c, None), sb(2 * qi - 2 - 2 * i, sb(2 * qi - 1 - 2 * i, sb_c, None), None)

        fox_c, (run, acc_s) = lax.fori_loop(0, qi, both, (fox_c, sb_c))
        m, l, acc = fox(qi, fox_c, 0)
        of_ref[...] = _unstack(acc / l, masks).astype(BF16)
        lse_ref[...] = _unstack(m + jnp.log(l), masks)
        os_ref[...] = _unstack(acc_s, masks).astype(BF16)
        rt_ref[...] = _unstack(run, masks)

    qa, ka, va = _att_specs(seq, 0, tq)
    qb_, kb_, vb_ = _att_specs(seq, 3 * N_PAIRS, tq)
    qb = _qblock_spec(seq, tq)
    half, wide = jax.ShapeDtypeStruct((batch * seq, D_ATT), BF16), jax.ShapeDtypeStruct((batch * seq, D_ATT), F32)
    return _call_behind(
        body, behind, name="att_fwd", grid=(batch, N_PAIRS, nq),
        in_specs=[qa, ka, va, qb, pl.BlockSpec((N_HEADS, seq), lambda b, hp, qi: (b, 0)), qb_, kb_, vb_],
        out_specs=[qb, qb, qb, qb], out_shape=[half, wide, half, wide], scratch_shapes=[],
        operands=(qkv, qkv, qkv, c_wide, c_row, qkv, qkv, qkv))


def _fox_bwd(qkv, c_wide, c_row, o, do, lse_wide, batch, seq, behind):
    tq, tk = FOX_TILES
    nq = seq // tq

    def body(q_ref, k_ref, v_ref, cw_ref, cr_ref, o_ref, do_ref, lse_ref,
             dq_ref, dk_ref, dv_ref, dcs_ref, drs_ref, dkc_acc, dv_acc):
        hp, qi = pl.program_id(1), pl.program_id(2)

        @pl.when(qi == 0)
        def _():
            dkc_acc[...] = jnp.zeros_like(dkc_acc)
            dv_acc[...] = jnp.zeros_like(dv_acc)

        masks = _head_masks()
        ahead, first = _stacked_ids(tq, tk)
        q_t, do_t = q_ref[...], do_ref[...]
        q2 = _stack_heads(q_t, masks) * SCALE
        do2 = _stack_heads(do_t, masks)
        q_and_ones = jnp.concatenate([q2, _stack_heads(jnp.ones_like(q_t), masks)], axis=1)
        ct = _stack_cols(cw_ref[...])
        lse = _stack_cols(lse_ref[...])
        prod = do_t.astype(F32) * o_ref[...].astype(F32)
        delta = jnp.concatenate([jnp.sum(jnp.where(mk, prod, 0.0), axis=1, keepdims=True) for mk in masks], axis=0)

        def step(kb, carry, lead, top):
            dq_acc, rs = carry
            k0 = pl.multiple_of(kb * tk, tk)
            kblk = k_ref[pl.ds(k0, tk), :]
            cs = jnp.where(first, cr_ref[pl.ds(2 * hp, 1), pl.ds(k0, tk)], cr_ref[pl.ds(2 * hp + 1, 1), pl.ds(k0, tk)])
            p = jnp.exp(_dot(q2, kblk, NT_DIMS) + ct - cs - lse)
            if lead is not None:
                p = jnp.where(ahead <= lead, p, 0.0)
            dp = _dot(do2, v_ref[pl.ds(k0, tk), :], NT_DIMS)
            ds = (p * (dp - delta)).astype(BF16)
            dkc_acc[pl.ds(k0, tk), :] += _dot(ds, q_and_ones, TN_DIMS)
            dv_acc[pl.ds(k0, tk), :] += _dot(p.astype(BF16), do2, TN_DIMS)
            return dq_acc + _dot(ds, kblk), rs + jnp.sum(ds.astype(F32), axis=1, keepdims=True)

        init = (jnp.zeros((2 * tq, LANES), F32), jnp.zeros((2 * tq, 1), F32))
        dq_acc, rs = _sweep(qi, tq, tk, step, init)
        dq_ref[...] = (_unstack(dq_acc, masks) * SCALE).astype(BF16)
        drs_ref[...] = _unstack(rs, masks)

        @pl.when(qi == nq - 1)
        def _():
            dk_ref[...] = dkc_acc[:, 0:LANES].astype(BF16)
            dcs_ref[...] = dkc_acc[:, LANES:2 * LANES]
            dv_ref[...] = dv_acc[...].astype(BF16)

    q_spec, k_spec, v_spec = _att_specs(seq, 0, tq)
    qb = _qblock_spec(seq, tq)
    return _call_behind(
        body, behind, name="fox_bwd", grid=(batch, N_PAIRS, nq),
        in_specs=[q_spec, k_spec, v_spec, qb, pl.BlockSpec((N_HEADS, seq), lambda b, hp, qi: (b, 0)), qb, qb, qb],
        out_specs=[qb, _kv_out_spec(seq), _kv_out_spec(seq), _kv_out_spec(seq), qb],
        out_shape=[jax.ShapeDtypeStruct((batch * seq, D_ATT), BF16)] * 3 + [jax.ShapeDtypeStruct((batch * seq, D_ATT), F32)] * 2,
        scratch_shapes=[pltpu.VMEM((seq, 2 * LANES), F32), pltpu.VMEM((seq, LANES), F32)],
        operands=(qkv, qkv, qkv, c_wide, c_row, o, do, lse_wide))


def _sb_logits(q2, kblk):
    z = _dot(q2, kblk, NT_DIMS)
    lsn = jnp.minimum(-z, 0.0) - jnp.log(1.0 + jnp.exp(-jnp.abs(z)))
    return lsn + z, lsn


def _sb_bwd(qkv, do, rt_wide, batch, seq, behind):
    tq, tk = SB_TILES
    nq = seq // tq

    def body(q_ref, k_ref, v_ref, do_ref, rt_ref, dq_ref, dk_ref, dv_ref, dk_acc, dv_acc):
        qi = pl.program_id(2)

        @pl.when(qi == 0)
        def _():
            dk_acc[...] = jnp.zeros_like(dk_acc)
            dv_acc[...] = jnp.zeros_like(dv_acc)

        masks = _head_masks()
        ahead, _ = _stacked_ids(tq, tk)
        later = _tri2(tk, lambda r, c: r > c)
        earlier = _tri(tk, lambda r, c: r < c)
        q2 = _stack_heads(q_ref[...], masks) * SCALE
        do2 = _stack_heads(do_ref[...], masks)
        total = _stack_cols(rt_ref[...])

        def step(kb, carry, lead, top):
            pref, epre, dq_acc = (_below(t, top) for t in carry)
            q_s, do_s = _below(q2, top), _below(do2, top)
            seen = None if lead is None else _below(ahead, top) < lead
            k0 = pl.multiple_of(kb * tk, tk)
            kblk = k_ref[pl.ds(k0, tk), :]
            ls, lsn_all = _sb_logits(q_s, kblk)
            lsn = lsn_all if lead is None else jnp.where(seen, lsn_all, 0.0)
            rs = jnp.sum(lsn, axis=1, keepdims=True)
            w = jnp.exp(ls + _dot(_split2(lsn), later) + (_below(total, top) - pref - rs))
            if lead is not None:
                w = jnp.where(seen, w, 0.0)
            e = w * _dot(do_s, v_ref[pl.ds(k0, tk), :], NT_DIMS)
            before = _dot(e.astype(BF16), earlier) + epre
            dz = e * jnp.exp(lsn_all) - jnp.exp(ls) * before
            if lead is not None:
                dz = jnp.where(seen, dz, 0.0)
            dz = dz.astype(BF16)
            dk_acc[pl.ds(k0, tk), :] += _dot(dz, q_s, TN_DIMS)
            dv_acc[pl.ds(k0, tk), :] += _dot(w.astype(BF16), do_s, TN_DIMS)
            new = (pref + rs, epre + jnp.sum(e, axis=1, keepdims=True), dq_acc + _dot(dz, kblk))
            return tuple(_put_below(o, n, top) for o, n in zip(carry, new))

        init = (jnp.zeros((2 * tq, 1), F32), jnp.zeros((2 * tq, 1), F32), jnp.zeros((2 * tq, LANES), F32))
        dq_acc = _sweep(qi, tq, tk, step, init)[2]
        dq_ref[...] = (_unstack(dq_acc, masks) * SCALE).astype(BF16)

        @pl.when(qi == nq - 1)
        def _():
            dk_ref[...] = dk_acc[...].astype(BF16)
            dv_ref[...] = dv_acc[...].astype(BF16)

    q_spec, k_spec, v_spec = _att_specs(seq, 3 * N_PAIRS, tq)
    qb = _qblock_spec(seq, tq)
    return _call_behind(
        body, behind, name="sb_bwd", grid=(batch, N_PAIRS, nq), in_specs=[q_spec, k_spec, v_spec, qb, qb],
        out_specs=[qb, _kv_out_spec(seq), _kv_out_spec(seq)], out_shape=[jax.ShapeDtypeStruct((batch * seq, D_ATT), BF16)] * 3,
        scratch_shapes=[pltpu.VMEM((seq, LANES), F32), pltpu.VMEM((seq, LANES), F32)], operands=(qkv, qkv, qkv, do, rt_wide))


def _local_step(x, p, target, first, rest, vec, place):
    batch, seq, _ = x.shape
    t = batch * seq
    x = x.reshape(t, D_MODEL)
    target = target.reshape(t, D_MODEL)
    p = p.reshape(t, D_PLE)
    big = dict(tm=1024, tn=1024, tk=1024)

    (h1,), (w_in_slots,) = _norm_fwd(x, vec["g_mix"], "norm_mix", first)
    w = _first_weights(w_in_slots)
    qkv = _mm(h1, w["qkv"], mode="nn", name="proj_qkv", out_dtype=BF16, **big)
    gl = _mm(h1, w["gate"], mode="nn", name="proj_gate", **big)
    fl = _mm(h1, w["forget"], mode="nn", name="proj_forget", **big)
    c_wide, c_row = _fox_prep(fl, vec["b_forget"], batch, seq)
    (o_fox, lse_wide, o_sb, rt_wide), gathered = _att_fwd(qkv, c_wide, c_row, batch, seq, rest)
    w = dict(w, **_rest_weights(dict(zip(EARLY + ("b_gate",), gathered))))
    merged, of, os_ = _gate_fwd(gl, w["b_gate"], o_fox, o_sb, w["branch_fox"], w["branch_sb"])
    x1, h2 = _mm_res_norm(merged, w["out"], x, vec["g_mlp"], "proj_out_norm")
    ar = _mm(h2, w["up"], mode="nn", name="mlp_up", out_dtype=BF16, epi=lambda acc, _: jnp.maximum(acc, 0.0),
             col_shards=True, **big)
    x2, h3 = _mm_res_norm(ar, w["down"], x1, vec["g_ple"], "mlp_down_norm", a_fn=_relu2)

    dx3, dpre, dpe, dg_final, loss = _head_and_loss(x2, h3, p, w["ple_gate"], w["ple"], vec["g_final"], target)
    gw = {}
    gw["ple"] = _mm(p, dpe, mode="tn", name="d_w_ple", col_shards=True, **big)
    gw["ple_gate"] = _mm(h3, dpre, mode="tn", name="d_w_ple_gate", **big)
    dx2, dx2b, dg_ple = _mm_norm_bwd([(dpre, w["ple_gate"])], None, x2, vec["g_ple"], dx3, "d_h_ple_norm_bwd")
    gw["down"] = _mm(ar, dx2b, mode="tn", name="d_w_down", a_fn=_relu2, **big)
    da = _mm(dx2b, w["down"], mode="nt", name="d_act", out_dtype=BF16,
             epi=lambda acc, r: acc * (2.0 * r.astype(F32)), extra=ar, **big)
    gw["up"] = _mm(h2, da, mode="tn", name="d_w_up", col_shards=True, **big)
    dx1, dx1b, dg_mlp = _mm_norm_bwd([(da, w["up"])], None, x1, vec["g_mlp"], dx2, "d_h_mlp_norm_bwd")
    gw["out"] = _mm(merged, dx1b, mode="tn", name="d_w_out", **big)
    dof, dos, dgl, gw["b_gate"], do_fox, do_sb = _gate_bwd(gl, w["b_gate"], of, os_, dx1b, w["out"], w["branch_fox"],
                                                                  w["branch_sb"])
    gw["branch_fox"] = _mm(o_fox, dof, mode="tn", name="d_w_branch_fox", col_shards=True, **big)
    gw["branch_sb"] = _mm(o_sb, dos, mode="tn", name="d_w_branch_sb", col_shards=True, **big)
    early = _early_slots(gw)
    early = [early[n] for n in EARLY]
    (dq_a, dk_a, dv_a, dcs_wide, drs_wide), received = _fox_bwd(qkv, c_wide, c_row, o_fox, do_fox, lse_wide, batch, seq,
                                                                _swap_halves(early))
    sums = _sum_sibling(place, early, received, "sum_sibling_early")
    (dq_b, dk_b, dv_b), others = _sb_bwd(qkv, do_sb, rt_wide, batch, seq, _exchange_chips(sums))
    mine = [None] * len(EARLY)
    for group, tag in ((BIG, "big"), (SMALL, "small")):
        for t, res in zip(group, _sum_chips(place, *[[a[t] for t in group] for a in (early, received, others)], "sum_chips_" + tag)):
            mine[t] = res
    dfl, db_forget = _fox_post(dcs_wide, drs_wide, fl, vec["b_forget"], batch, seq)
    dqkv = jnp.concatenate([dq_a, dk_a, dv_a, dq_b, dk_b, dv_b], axis=1)
    gw["qkv"], theirs = _mm(dqkv, h1, mode="tn", name="d_w_qkv", behind=_share_halves(mine), flat_out=True, **big)
    reduced = dict(zip(EARLY, zip(mine, theirs)))
    gw["gate"] = _mm(dgl, h1, mode="tn", name="d_w_gate", flat_out=True, **big)
    gw["forget"] = _mm(dfl, h1, mode="tn", name="d_w_forget", flat_out=True, **big)
    late = [_w_in_slots(gw)]
    dh1, received = _mm(dqkv, w["qkv"], mode="nt", name="d_h_qkv", behind=_swap_halves(late), **big)
    sums = _sum_sibling(place, late, received, "sum_sibling_w_in")
    (grad_x, _, dg_mix), others = _mm_norm_bwd([(dgl, w["gate"]), (dfl, w["forget"])], dh1, x, vec["g_mix"], dx1,
                                               "d_h_gate_norm_bwd", behind=_exchange_chips(sums))
    mine = _sum_chips(place, late, received, others, "sum_chips_w_in")
    reduced["w_in"] = (mine[0], _run_exchange(_share_halves(mine), "reduce_share_w_in")[0])
    gvec = {"g_mix": dg_mix, "b_forget": db_forget[:, 0:N_HEADS], "g_mlp": dg_mlp, "g_ple": dg_ple,
            "g_final": dg_final, "b_gate": gw["b_gate"]}
    return loss, grad_x.reshape(batch, seq, D_MODEL), reduced, gvec


ANY = pl.BlockSpec(memory_space=pl.ANY)
SHARDED = ("w_in", "w_branch_fox", "w_branch_sb", "w_out", "w_up", "w_down", "w_ple_gate", "w_ple")
ROW_ALIGN = 16
F32_ROWS = 8


def _place():
    return lax.axis_index("x"), lax.axis_index("y"), lax.axis_index("c")


def _other_chips(x, y):
    return [(1 - x, y), (x, 1 - y), (1 - x, 1 - y)]


def _half(ref, h):
    r = ref.shape[0] // 2
    assert r % ROW_ALIGN == 0
    return ref.at[pl.ds(pl.multiple_of(h * r, ROW_ALIGN), r)]


def _remote(src, dst, sems, idx, to):
    send_sems, recv_sems = sems
    return pltpu.make_async_remote_copy(src_ref=src, dst_ref=dst, send_sem=send_sems.at[idx], recv_sem=recv_sems.at[idx],
                                        device_id=to, device_id_type=MESH)


class _Exchange:
    def __init__(self, operands, out_shapes, sem_shape, start, finish):
        self.operands, self.out_shapes, self.sem_shape, self.start, self.finish = operands, out_shapes, sem_shape, start, finish

    def scratch(self):
        return [pltpu.SemaphoreType.DMA(self.sem_shape), pltpu.SemaphoreType.DMA(self.sem_shape)]


def _run_exchange(ex, name):
    n = len(ex.operands)

    def body(*refs):
        ex.start(refs[:n], refs[n:2 * n], refs[2 * n:])
        ex.finish(refs[:n], refs[n:2 * n], refs[2 * n:])

    return pl.pallas_call(body, name=name, in_specs=[ANY] * n, out_specs=[ANY] * n, out_shape=ex.out_shapes,
                          scratch_shapes=ex.scratch())(*ex.operands)


def _call_behind(body, ex, *, name, grid, in_specs, out_specs, out_shape, scratch_shapes, operands):
    n_in, n_out, nx = len(in_specs), len(out_specs), len(ex.operands)

    def wrapped(*refs):
        ins, x_in = refs[:n_in], refs[n_in:n_in + nx]
        outs, x_out = refs[n_in + nx:n_in + nx + n_out], refs[n_in + nx + n_out:n_in + 2 * nx + n_out]
        scratch, sems = refs[n_in + 2 * nx + n_out:-2], refs[-2:]
        first, last = None, None
        for d, steps in enumerate(grid):
            at_start, at_end = pl.program_id(d) == 0, pl.program_id(d) == steps - 1
            first = at_start if first is None else first & at_start
            last = at_end if last is None else last & at_end

        @pl.when(first)
        def _():
            ex.start(x_in, x_out, sems)

        body(*ins, *outs, *scratch)

        @pl.when(last)
        def _():
            ex.finish(x_in, x_out, sems)

    res = pl.pallas_call(
        wrapped, name=name, grid=grid, in_specs=list(in_specs) + [ANY] * nx, out_specs=list(out_specs) + [ANY] * nx,
        out_shape=list(out_shape) + list(ex.out_shapes), scratch_shapes=list(scratch_shapes) + ex.scratch(),
        compiler_params=_cparams(("arbitrary",) * len(grid)),
    )(*operands, *ex.operands)
    return res[:n_out], res[n_out:]


def _gather_weights(shards):
    n = len(shards)

    def first_copies(src, out, sems):
        x, y, c = _place()
        me = 2 * x + y
        copies = [_remote(_half(src[t], c), _half(out[t].at[me], c), sems, (t, k), (px, py, c))
                  for t in range(n) for k, (px, py) in enumerate(_other_chips(x, y))]
        return copies + [_remote(src[t], out[t].at[me], sems, (t, 3), (x, y, 1 - c)) for t in range(n)]

    def start(src, out, sems):
        for cp in first_copies(src, out, sems):
            cp.start()

    def finish(src, out, sems):
        x, y, c = _place()
        me = 2 * x + y
        sibling = (x, y, 1 - c)
        chips = _other_chips(x, y)
        passes = []
        for t in range(n):
            for k, (px, py) in enumerate(chips):
                landed = _half(out[t].at[2 * px + py], c)
                _remote(landed, landed, sems, (t, k), (px, py, c)).wait_recv()
                passes.append(_remote(landed, landed, sems, (t, 4 + k), sibling))
                passes[-1].start()
        for t in range(n):
            _remote(src[t], out[t].at[me], sems, (t, 3), sibling).wait_recv()
            for k, (px, py) in enumerate(chips):
                passed = _half(out[t].at[2 * px + py], 1 - c)
                _remote(passed, passed, sems, (t, 4 + k), sibling).wait_recv()
        for cp in first_copies(src, out, sems) + passes:
            cp.wait_send()

    return _Exchange(shards, [jax.ShapeDtypeStruct((N_CHIPS,) + s.shape, s.dtype) for s in shards], (n, 7), start, finish)


def _simple_exchange(operands, out_shapes, copies):
    def start(src, out, sems):
        for cp in copies(src, out, sems):
            cp.start()

    def finish(src, out, sems):
        for cp in copies(src, out, sems):
            cp.wait_recv()
        for cp in copies(src, out, sems):
            cp.wait_send()

    return _Exchange(operands, out_shapes, (len(operands),), start, finish)


def _swap_halves(slots):
    def copies(src, out, sems):
        x, y, c = _place()
        res = []
        for t in range(len(slots)):
            r = src[t].shape[1] // 2
            rows = pl.ds(pl.multiple_of((1 - c) * r, F32_ROWS), r)
            res.append(_remote(src[t].at[:, rows], out[t], sems, t, (x, y, 1 - c)))
        return res

    return _simple_exchange(slots, [jax.ShapeDtypeStruct((N_CHIPS, s.shape[1] // 2, s.shape[2]), s.dtype) for s in slots], copies)


def _exchange_chips(sums):
    n = len(sums)

    def copies(src, out, sems):
        x, y, c = _place()
        return [_remote(src[t].at[2 * px + py], out[t].at[k], sems, (t, k), (px, py, c))
                for t in range(n) for k, (px, py) in enumerate(_other_chips(x, y))]

    def start(src, out, sems):
        for cp in copies(src, out, sems):
            cp.start()

    def finish(src, out, sems):
        for cp in copies(src, out, sems):
            cp.wait_recv()
        for cp in copies(src, out, sems):
            cp.wait_send()

    return _Exchange(sums, [jax.ShapeDtypeStruct((3,) + s.shape[1:], s.dtype) for s in sums], (n, 3), start, finish)


def _share_halves(mine):
    def copies(src, out, sems):
        x, y, c = _place()
        return [_remote(src[t], out[t], sems, t, (x, y, 1 - c)) for t in range(len(mine))]

    return _simple_exchange(mine, [jax.ShapeDtypeStruct(s.shape, s.dtype) for s in mine], copies)


def _walk(name, place, parts):
    starts = [sum(p[0] for p in parts[:t]) for t in range(len(parts))]
    held = lambda index, start, steps: (lambda s, pr: index(jnp.clip(s - start, 0, steps - 1), pr))
    in_specs, out_specs, out_shapes, operands = [], [], [], []
    for (steps, ins, outs, shapes, ops, _), start in zip(parts, starts):
        in_specs += [pl.BlockSpec(blk, held(index, start, steps)) for blk, index in ins]
        out_specs += [pl.BlockSpec(blk, held(index, start, steps)) for blk, index in outs]
        out_shapes += list(shapes)
        operands += list(ops)

    def body(place_ref, *refs):
        s = pl.program_id(0)
        i, o = 0, len(in_specs)
        for (steps, ins, outs, _, _, fn), start in zip(parts, starts):
            mine_in, mine_out = refs[i:i + len(ins)], refs[o:o + len(outs)]
            i, o = i + len(ins), o + len(outs)

            @pl.when((s >= start) & (s < start + steps))
            def _(mine_in=mine_in, mine_out=mine_out, start=start, fn=fn):
                fn(s - start, mine_in, mine_out)

    res = pl.pallas_call(
        body, name=name, out_shape=out_shapes,
        grid_spec=pltpu.PrefetchScalarGridSpec(num_scalar_prefetch=1, grid=(sum(p[0] for p in parts),), in_specs=in_specs,
                                               out_specs=out_specs),
        compiler_params=_cparams(("arbitrary",)),
    )(place, *operands)
    counts = [len(p[2]) for p in parts]
    return [res[sum(counts[:t]):sum(counts[:t + 1])] for t in range(len(parts))]


def _sum_sibling(place, slots, received, name):
    def part(slot, got):
        n, rows, cols = got.shape
        block = (None, rows, cols)

        def fn(j, ins, outs):
            outs[0][...] = (ins[0][...] + ins[1][...]).astype(BF16)

        return (n, [(block, lambda j, pr: (j, pr[1], 0)), (block, lambda j, pr: (j, 0, 0))], [(block, lambda j, pr: (j, 0, 0))],
                [jax.ShapeDtypeStruct(got.shape, BF16)], [slot, got], fn)

    return [r[0] for r in _walk(name, place, [part(s, g) for s, g in zip(slots, received)])]


def _sum_chips(place, slots, received, others, name):
    def part(slot, got, other):
        _, rows, cols = got.shape
        block = (None, rows, cols)

        def fn(_, ins, outs):
            own = ins[0][...] + ins[1][...]
            outs[0][...] = ((own + ins[2][0].astype(F32)) + ins[2][1].astype(F32)) + ins[2][2].astype(F32)

        return (1, [(block, lambda _, pr: (pr[0], pr[1], 0)), (block, lambda _, pr: (pr[0], 0, 0)),
                    ((3, rows, cols), lambda _, pr: (0, 0, 0))], [((rows, cols), lambda _, pr: (0, 0))],
                [jax.ShapeDtypeStruct((rows, cols), F32)], [slot, got, other], fn)

    return [r[0] for r in _walk(name, place, [part(*t) for t in zip(slots, received, others)])]


N_DEVICES = 8


def _sum_devices(block, name):
    def body(v_ref, o_ref, land_ref, send_sems, recv_sems):
        x, y, c = _place()
        me = 4 * x + 2 * y + c
        copies = []
        for mask in range(1, N_DEVICES):
            peer = (x ^ (mask >> 2), y ^ ((mask >> 1) & 1), c ^ (mask & 1))
            copies.append(pltpu.make_async_remote_copy(src_ref=v_ref, dst_ref=land_ref.at[me], send_sem=send_sems.at[mask - 1],
                                                       recv_sem=recv_sems.at[mask - 1], device_id=peer, device_id_type=MESH))
        for cp in copies:
            cp.start()
        land_ref[me] = v_ref[...]
        for cp in copies:
            cp.wait_recv()
        total = land_ref[0]
        for d in range(1, N_DEVICES):
            total = total + land_ref[d]
        o_ref[...] = total
        for cp in copies:
            cp.wait_send()

    vmem = pl.BlockSpec(memory_space=pltpu.VMEM)
    return pl.pallas_call(
        body, name=name, in_specs=[vmem], out_specs=vmem, out_shape=jax.ShapeDtypeStruct(block.shape, F32),
        scratch_shapes=[pltpu.VMEM((N_DEVICES,) + block.shape, F32), pltpu.SemaphoreType.DMA((N_DEVICES - 1,)),
                        pltpu.SemaphoreType.DMA((N_DEVICES - 1,))],
    )(block)


def _vec_block(g_mix, g_mlp, g_ple, g_final, b_forget, b_gate_rows, last=None):
    pad = lambda a: jnp.concatenate([a, jnp.zeros((a.shape[0], D_MODEL - a.shape[1]), F32)], axis=1)
    last = jnp.zeros((1, 0), F32) if last is None else last
    return jnp.concatenate([g_mix, g_mlp, g_ple, g_final.reshape(1, D_MODEL), pad(b_forget), pad(b_gate_rows), pad(last)],
                           axis=0)


def _adam_math(w, g, m, v):
    m_new = ADAM_B1 * m + (1.0 - ADAM_B1) * g
    v_new = ADAM_B2 * v + (1.0 - ADAM_B2) * (g * g)
    m_hat = m_new / (1.0 - ADAM_B1 ** ADAM_STEP)
    v_hat = v_new / (1.0 - ADAM_B2 ** ADAM_STEP)
    return -ADAM_LR * (m_hat / (jnp.sqrt(v_hat) + ADAM_EPS) + ADAM_WD * w), m_new, v_new


def _adamw_halves(place, weights, name):
    def part(w, m, v, g_mine, g_theirs):
        rows, cols = g_mine.shape
        whole = ((rows, cols), lambda s, pr: (pr[1] + s - 2 * pr[1] * s, 0))
        half = ((rows, cols), lambda s, pr: (0, 0))

        def fn(s, ins, outs):
            g = jnp.where(s == 0, ins[3][...], ins[4][...])
            outs[0][...] = g
            outs[1][...], outs[2][...], outs[3][...] = _adam_math(ins[0][...], g, ins[1][...], ins[2][...])

        return (2, [whole] * 3 + [half] * 2, [whole] * 4, [jax.ShapeDtypeStruct(w.shape, F32)] * 4, [w, m, v, g_mine, g_theirs], fn)

    return _walk(name, place, [part(*t) for t in weights])


def _adamw_vec(w, g, m, v):
    def body(w_ref, g_ref, m_ref, v_ref, d_ref, nm_ref, nv_ref):
        d_ref[...], nm_ref[...], nv_ref[...] = _adam_math(w_ref[...], g_ref[...], m_ref[...], v_ref[...])

    return pl.pallas_call(body, name="adamw_vectors", out_shape=[jax.ShapeDtypeStruct(w.shape, F32)] * 3)(w, g, m, v)


WEIGHT_NAMES = ("g_mix", "w_in", "b_forget", "b_gate", "w_branch_fox", "w_branch_sb", "w_out", "g_mlp", "w_up", "w_down",
                "g_ple", "w_ple_gate", "w_ple", "g_final")
W_IN_SHARD = D_IN // N_CHIPS
Q_END, F_END, B_END = 3 * D_ATT, 3 * D_ATT + N_HEADS, 6 * D_ATT + N_HEADS
GATE_SHARD = D_MODEL // N_CHIPS


LATE = SHARDED[:1]
EARLY = SHARDED[1:]
BIG = tuple(t for t, n in enumerate(EARLY) if n in ("w_up", "w_down"))
SMALL = tuple(t for t in range(len(EARLY)) if t not in BIG)


def _first_weights(w_in_slots):
    def cols(*ranges):
        parts = []
        for lo, hi in ranges:
            for j in range(N_CHIPS):
                a, b = max(lo, j * W_IN_SHARD), min(hi, (j + 1) * W_IN_SHARD)
                if a < b:
                    parts.append(w_in_slots[j, :, a - j * W_IN_SHARD:b - j * W_IN_SHARD])
        return parts

    forget = jnp.concatenate(cols((Q_END, F_END)) + [jnp.zeros((D_MODEL, F_PAD - N_HEADS), BF16)], axis=1)
    return {"qkv": jnp.concatenate(cols((0, Q_END), (F_END, B_END)), axis=1), "gate": jnp.concatenate(cols((B_END, D_IN)), axis=1),
            "forget": forget}


GATE_ROWS = 2 * ROW_ALIGN


def _gate_bits(b_gate):
    bits = lax.bitcast_convert_type(b_gate, BF16).reshape(2, 2 * GATE_SHARD)
    return jnp.concatenate([bits, jnp.zeros((GATE_ROWS - 2, 2 * GATE_SHARD), BF16)], axis=0)


def _rest_weights(gathered):
    rows = lambda a: a.reshape(N_CHIPS * a.shape[1], a.shape[2])
    bits = gathered["b_gate"][:, :2].reshape(N_CHIPS, 2, GATE_SHARD, 2)
    b_gate = jnp.transpose(lax.bitcast_convert_type(bits, F32), (1, 0, 2)).reshape(2, D_MODEL)
    return {"branch_fox": gathered["w_branch_fox"], "branch_sb": gathered["w_branch_sb"], "out": rows(gathered["w_out"]),
            "up": gathered["w_up"], "down": rows(gathered["w_down"]), "ple_gate": rows(gathered["w_ple_gate"]),
            "ple": gathered["w_ple"], "b_gate": b_gate}


def _early_slots(gw):
    rows = lambda a: a.reshape(N_CHIPS, a.shape[0] // N_CHIPS, a.shape[1])
    return {"w_branch_fox": gw["branch_fox"], "w_branch_sb": gw["branch_sb"], "w_out": rows(gw["out"]), "w_up": gw["up"],
            "w_down": rows(gw["down"]), "w_ple_gate": rows(gw["ple_gate"]), "w_ple": gw["ple"]}


W_IN_FLAT = (W_IN_SHARD * D_MODEL // LANES, LANES)


def _w_in_slots(gw):
    c = D_MODEL // LANES
    g_t = jnp.concatenate([gw["qkv"][:Q_END * c], gw["forget"][:N_HEADS * c], gw["qkv"][Q_END * c:], gw["gate"]], axis=0)
    return g_t.reshape((N_CHIPS,) + W_IN_FLAT)


def _flat(a):
    return jnp.transpose(a, (2, 0, 1)).reshape(W_IN_FLAT)


def _unflat(a):
    return jnp.transpose(a.reshape(W_IN_SHARD, D_MODEL // LANES, LANES), (1, 2, 0)).reshape(1, D_MODEL, W_IN_SHARD)


def kernel(x, p, g_mix, w_in, b_forget, b_gate, w_branch_fox, w_branch_sb, w_out, g_mlp, w_up, w_down, g_ple, w_ple_gate, w_ple, g_final, loss_target, m_g_mix, m_w_in, m_b_forget, m_b_gate, m_w_branch_fox, m_w_branch_sb, m_w_out, m_g_mlp, m_w_up, m_w_down, m_g_ple, m_w_ple_gate, m_w_ple, m_g_final, v_g_mix, v_w_in, v_b_forget, v_b_gate, v_w_branch_fox, v_w_branch_sb, v_w_out, v_g_mlp, v_w_up, v_w_down, v_g_ple, v_w_ple_gate, v_w_ple, v_g_final):
    weights = dict(g_mix=g_mix, w_in=w_in, b_forget=b_forget, b_gate=b_gate, w_branch_fox=w_branch_fox,
                   w_branch_sb=w_branch_sb, w_out=w_out, g_mlp=g_mlp, w_up=w_up, w_down=w_down, g_ple=g_ple,
                   w_ple_gate=w_ple_gate, w_ple=w_ple, g_final=g_final)
    first = dict(g_mix=m_g_mix, w_in=m_w_in, b_forget=m_b_forget, b_gate=m_b_gate, w_branch_fox=m_w_branch_fox,
                 w_branch_sb=m_w_branch_sb, w_out=m_w_out, g_mlp=m_g_mlp, w_up=m_w_up, w_down=m_w_down, g_ple=m_g_ple,
                 w_ple_gate=m_w_ple_gate, w_ple=m_w_ple, g_final=m_g_final)
    second = dict(g_mix=v_g_mix, w_in=v_w_in, b_forget=v_b_forget, b_gate=v_b_gate, w_branch_fox=v_w_branch_fox,
                  w_branch_sb=v_w_branch_sb, w_out=v_w_out, g_mlp=v_g_mlp, w_up=v_w_up, w_down=v_w_down, g_ple=v_g_ple,
                  w_ple_gate=v_w_ple_gate, w_ple=v_w_ple, g_final=v_g_final)
    cx, cy, cc = _place()
    chip = 2 * cx + cy
    place = jnp.stack([chip, cc]).astype(jnp.int32)
    col0 = chip * GATE_SHARD

    first_gather = _gather_weights([weights[n][0].astype(BF16) for n in LATE])
    rest = _gather_weights([weights[n][0].astype(BF16) for n in EARLY] + [_gate_bits(b_gate[0])])
    vec = {"g_mix": g_mix, "b_forget": jnp.concatenate([b_forget, jnp.zeros((1, F_PAD - N_HEADS), F32)], axis=1),
           "g_mlp": g_mlp, "g_ple": g_ple, "g_final": g_final.reshape(1, D_MODEL)}

    loss, grad_x, reduced, gvec = _local_step(x, p[0], loss_target, first_gather, rest, vec, place)

    out = {}
    args = lambda n: (weights[n][0], first[n][0], second[n][0]) + tuple(reduced[n])
    for names, tag in [([EARLY[t] for t in SMALL], "small")] + [([EARLY[t]], EARLY[t]) for t in BIG]:
        for n, res in zip(names, _adamw_halves(place, [args(n) for n in names], "adamw_" + tag)):
            out[n] = [r[None] for r in res]
    (res,) = _adamw_halves(place, [(_flat(w_in), _flat(m_w_in), _flat(v_w_in)) + tuple(reduced["w_in"])], "adamw_w_in")
    out["w_in"] = [_unflat(r) for r in res]

    g_block = _sum_devices(_vec_block(gvec["g_mix"], gvec["g_mlp"], gvec["g_ple"], gvec["g_final"][0], gvec["b_forget"],
                                      gvec["b_gate"], loss), "reduce_vectors")
    loss = g_block[7, 0]
    g_gate = lax.dynamic_slice(g_block[5:7], (0, col0), (2, GATE_SHARD))
    blocks = [_vec_block(d["g_mix"], d["g_mlp"], d["g_ple"], d["g_final"], d["b_forget"], d["b_gate"][0])
              for d in (weights, first, second)]
    g_rows = jnp.concatenate([g_block[0:5], jnp.concatenate([g_gate, jnp.zeros((2, D_MODEL - GATE_SHARD), F32)], axis=1),
                              jnp.zeros((1, D_MODEL), F32)], axis=0)
    res = (g_rows,) + tuple(_adamw_vec(blocks[0], g_rows, blocks[1], blocks[2]))
    out["g_mix"] = [r[0:1] for r in res]
    out["g_mlp"] = [r[1:2] for r in res]
    out["g_ple"] = [r[2:3] for r in res]
    out["g_final"] = [r[3] for r in res]
    out["b_forget"] = [r[4:5, :N_HEADS] for r in res]
    out["b_gate"] = [r[5:7, :GATE_SHARD][None] for r in res]
    return (loss, grad_x, *[out[n][0] for n in WEIGHT_NAMES], *[out[n][1] for n in WEIGHT_NAMES],
            *[out[n][2] for n in WEIGHT_NAMES], *[out[n][3] for n in WEIGHT_NAMES])
```

```python
import jax
import jax.numpy as jnp
from jax import lax
from jax.experimental import pallas as pl
from jax.experimental.pallas import tpu as pltpu

F32 = jnp.float32
BF16 = jnp.bfloat16

D_MODEL = 1024
HEAD_DIM = 64
N_HEADS = 8
D_ATT = N_HEADS * HEAD_DIM
D_PLE = 256
D_IN = 6 * D_ATT + N_HEADS + 2 * D_MODEL
F_PAD = 128
EPS = 1e-6
SCALE = HEAD_DIM ** -0.5
N_CHIPS = 4
LANES = 128
ATT_BLOCK = 256
FOX_TILES = (512, 512)
SB_TILES = (512, 256)
NEG = -1e30

ADAM_LR = 0.001
ADAM_B1 = 0.9
ADAM_B2 = 0.999
ADAM_EPS = 1e-08
ADAM_WD = 0.01
ADAM_STEP = 10

VMEM_LIMIT = 56 * 1024 * 1024

MESH = pl.DeviceIdType.MESH


def _cparams(sem=None):
    return pltpu.CompilerParams(dimension_semantics=sem, vmem_limit_bytes=VMEM_LIMIT)


def _relu2(t):
    t = t.astype(F32)
    return t * t


_DIMS = {"nn": (((1,), (0,)), ((), ())), "nt": (((1,), (1,)), ((), ())), "tn": (((0,), (0,)), ((), ()))}
NT_DIMS = _DIMS["nt"]
TN_DIMS = _DIMS["tn"]


def _mm(a, b, *, mode, name, out_dtype=F32, tm=512, tn=512, tk=512, add=None, a_fn=None, epi=None, extra=None,
        col_shards=False, behind=None, flat_out=False):
    if mode == "nn":
        (m, k), n = a.shape, b.shape[-1]
    elif mode == "nt":
        (m, k), n = a.shape, b.shape[-2]
    else:
        (k, m), n = a.shape, b.shape[1]
    shard = None
    if col_shards:
        if mode == "nn":
            shard, n = n, N_CHIPS * n
            tn = min(tn, shard)
        elif mode == "nt":
            shard = b.shape[-1]
            tk = min(tk, shard)
        else:
            shard = n // N_CHIPS
            if tn < n:
                tn = min(tn, shard)
    tm, tn, tk = min(tm, m), min(tn, n), min(tk, k)
    assert m % tm == 0 and n % tn == 0 and k % tk == 0, (name, m, n, k)
    nk = k // tk
    all_shards = col_shards and mode == "tn" and tn == n
    a_spec = {"nn": pl.BlockSpec((tm, tk), lambda i, j, kk: (i, kk)),
              "nt": pl.BlockSpec((tm, tk), lambda i, j, kk: (i, kk)),
              "tn": pl.BlockSpec((tk, tm), lambda i, j, kk: (kk, i))}[mode]
    b_spec = {"nn": pl.BlockSpec((tk, tn), lambda i, j, kk: (kk, j)),
              "nt": pl.BlockSpec((tn, tk), lambda i, j, kk: (j, kk)),
              "tn": pl.BlockSpec((tk, tn), lambda i, j, kk: (kk, j))}[mode]
    o_spec = pl.BlockSpec((tm, tn), lambda i, j, kk: (i, j))
    out_shape = (m, n)
    if col_shards and mode == "nn":
        per = shard // tn
        b_spec = pl.BlockSpec((None, tk, tn), lambda i, j, kk: (j // per, kk, j % per))
    elif col_shards and mode == "nt":
        per = shard // tk
        b_spec = pl.BlockSpec((None, tn, tk), lambda i, j, kk: (kk // per, j, kk % per))
    elif col_shards:
        assert add is None and extra is None
        if all_shards:
            o_spec = pl.BlockSpec((N_CHIPS, tm, shard), lambda i, j, kk: (0, i, 0))
        else:
            per = shard // tn
            o_spec = pl.BlockSpec((None, tm, tn), lambda i, j, kk: (j // per, i, j % per))
        out_shape = (N_CHIPS, m, shard)
    if flat_out:
        assert mode == "tn" and tn == n == D_MODEL and not col_shards and add is None and extra is None
        chunks = D_MODEL // LANES
        o_spec = pl.BlockSpec((tm * chunks, LANES), lambda i, j, kk: (i, 0))
        out_shape = (m * chunks, LANES)
    operands, in_specs = [a, b], [a_spec, b_spec]
    third = add if add is not None else extra
    if third is not None:
        operands.append(third)
        in_specs.append(o_spec)

    def body(*refs):
        a_ref, b_ref = refs[0], refs[1]
        t_ref = refs[2] if third is not None else None
        o_ref = refs[3] if third is not None else refs[2]
        acc_ref = refs[-1] if nk > 1 else None
        at = a_ref[...]
        if a_fn is not None:
            at = a_fn(at)
        part = lax.dot_general(at.astype(BF16), b_ref[...].astype(BF16), _DIMS[mode], preferred_element_type=F32)

        def finish(acc):
            if epi is not None:
                acc = epi(acc, None if t_ref is None else t_ref[...])
            elif add is not None:
                acc = acc + t_ref[...].astype(F32)
            if flat_out:
                for q in range(D_MODEL // LANES):
                    o_ref[pl.ds(q, tm, stride=D_MODEL // LANES), :] = acc[:, q * LANES:(q + 1) * LANES].astype(o_ref.dtype)
                return
            if all_shards:
                for slot in range(N_CHIPS):
                    o_ref[slot] = acc[:, slot * shard:(slot + 1) * shard].astype(o_ref.dtype)
                return
            o_ref[...] = acc.astype(o_ref.dtype)

        if nk == 1:
            finish(part)
        else:
            kk = pl.program_id(2)

            @pl.when(kk == 0)
            def _():
                acc_ref[...] = part

            @pl.when(kk > 0)
            def _():
                acc_ref[...] += part

            @pl.when(kk == nk - 1)
            def _():
                finish(acc_ref[...])

    call = dict(name=name, grid=(m // tm, n // tn, nk), in_specs=in_specs,
                scratch_shapes=[pltpu.VMEM((tm, tn), F32)] if nk > 1 else [])
    if behind is not None:
        (res,), exchanged = _call_behind(body, behind, out_specs=[o_spec], out_shape=[jax.ShapeDtypeStruct(out_shape, out_dtype)],
                                         operands=operands, **call)
        return res, exchanged
    return pl.pallas_call(body, out_specs=o_spec, out_shape=jax.ShapeDtypeStruct(out_shape, out_dtype),
                          compiler_params=_cparams(("parallel", "parallel", "arbitrary")), **call)(*operands)


ROW_TILE = 512


def _row_spec(width=D_MODEL, rows=ROW_TILE):
    return pl.BlockSpec((rows, width), lambda i: (i, 0))


def _vec_spec(rows=1, width=D_MODEL):
    return pl.BlockSpec((rows, width), lambda i: (0, 0))


def _xhat(x):
    r = lax.rsqrt(jnp.mean(x * x, axis=-1, keepdims=True) + EPS)
    return x * r, r


def _rms_bwd_rows(dh, x, g):
    xh, r = _xhat(x)
    dxh = dh * g
    dx = r * (dxh - xh * jnp.mean(dxh * xh, axis=-1, keepdims=True))
    return dx, jnp.sum(dh * xh, axis=0, keepdims=True)


def _norm_fwd(x, g, name, behind):
    t = x.shape[0]

    def body(x_ref, g_ref, h_ref):
        xh, _ = _xhat(x_ref[...])
        h_ref[...] = (xh * g_ref[...]).astype(BF16)

    return _call_behind(body, behind, name=name, grid=(t // ROW_TILE,), in_specs=[_row_spec(), _vec_spec()],
                        out_specs=[_row_spec()], out_shape=[jax.ShapeDtypeStruct((t, D_MODEL), BF16)], scratch_shapes=[],
                        operands=(x, g))


def _mm_res_norm(a, b, res, g, name, a_fn=None):
    t, k = a.shape

    def body(a_ref, b_ref, res_ref, g_ref, x_ref, h_ref):
        at = a_ref[...] if a_fn is None else a_fn(a_ref[...])
        x_new = res_ref[...] + _dot(at.astype(BF16), b_ref[...])
        x_ref[...] = x_new
        h_ref[...] = (_xhat(x_new)[0] * g_ref[...]).astype(BF16)

    return pl.pallas_call(
        body, name=name, grid=(t // ROW_TILE,),
        in_specs=[pl.BlockSpec((ROW_TILE, k), lambda i: (i, 0)), pl.BlockSpec(b.shape, lambda i: (0, 0)), _row_spec(), _vec_spec()],
        out_specs=[_row_spec(), _row_spec()],
        out_shape=[jax.ShapeDtypeStruct((t, D_MODEL), F32), jax.ShapeDtypeStruct((t, D_MODEL), BF16)],
        compiler_params=_cparams(("parallel",)),
    )(a, b, res, g)


def _mm_norm_bwd(pairs, dh_first, x, g, dres, name, behind=None):
    t = x.shape[0]
    operands, in_specs = [], []
    for a, b in pairs:
        if b.ndim == 3:
            for j in range(b.shape[0]):
                operands += [a, b]
                in_specs += [pl.BlockSpec((ROW_TILE, b.shape[2]), lambda i, j=j: (i, j)),
                             pl.BlockSpec((None, D_MODEL, b.shape[2]), lambda i, j=j: (j, 0, 0))]
        else:
            operands += [a, b]
            in_specs += [pl.BlockSpec((ROW_TILE, a.shape[1]), lambda i: (i, 0)), pl.BlockSpec(b.shape, lambda i: (0, 0))]
    n_mm = len(operands)
    operands += [x, g, dres] + ([] if dh_first is None else [dh_first])
    in_specs += [_row_spec(), _vec_spec(), _row_spec()] + ([] if dh_first is None else [_row_spec()])

    def body(*refs):
        x_ref, g_ref, dres_ref = refs[n_mm:n_mm + 3]
        dx_ref, dxb_ref, dg_ref = refs[-3:]
        dh = 0.0 if dh_first is None else refs[n_mm + 3][...]
        for k in range(0, n_mm, 2):
            dh = dh + lax.dot_general(refs[k][...].astype(BF16), refs[k + 1][...].astype(BF16), NT_DIMS,
                                      preferred_element_type=F32)
        dx, dg = _rms_bwd_rows(dh, x_ref[...], g_ref[...])
        dx = dx + dres_ref[...]
        dx_ref[...] = dx
        dxb_ref[...] = dx.astype(BF16)

        @pl.when(pl.program_id(0) == 0)
        def _():
            dg_ref[...] = jnp.zeros_like(dg_ref)

        dg_ref[...] += dg

    call = dict(name=name, grid=(t // ROW_TILE,), in_specs=in_specs, out_specs=[_row_spec(), _row_spec(), _vec_spec()],
                out_shape=[jax.ShapeDtypeStruct((t, D_MODEL), F32), jax.ShapeDtypeStruct((t, D_MODEL), BF16),
                           jax.ShapeDtypeStruct((1, D_MODEL), F32)])
    if behind is not None:
        return _call_behind(body, behind, scratch_shapes=[], operands=operands, **call)
    return pl.pallas_call(body, compiler_params=_cparams(("arbitrary",)), **call)(*operands)


def _shards_spec(w):
    return pl.BlockSpec(w.shape, lambda i: (0, 0, 0))


def _gate_fwd(gl, b_gate, o_fox, o_sb, w_fox, w_sb):
    t = o_fox.shape[0]

    def body(gla_ref, glb_ref, b_ref, ofox_ref, osb_ref, wf_ref, ws_ref, m_ref, of_ref, os_ref):
        of = jnp.concatenate([_dot(ofox_ref[...], wf_ref[j]) for j in range(N_CHIPS)], axis=1)
        os_ = jnp.concatenate([_dot(osb_ref[...], ws_ref[j]) for j in range(N_CHIPS)], axis=1)
        ga = jax.nn.sigmoid(gla_ref[...] + b_ref[0:1, :])
        gb = jax.nn.sigmoid(glb_ref[...] + b_ref[1:2, :])
        of_ref[...] = of
        os_ref[...] = os_
        m_ref[...] = (ga * of + gb * os_).astype(BF16)

    return pl.pallas_call(
        body, name="gate_fwd", grid=(t // ROW_TILE,),
        in_specs=[pl.BlockSpec((ROW_TILE, D_MODEL), lambda i: (i, 0)), pl.BlockSpec((ROW_TILE, D_MODEL), lambda i: (i, 1)),
                  _vec_spec(2), _row_spec(D_ATT), _row_spec(D_ATT), _shards_spec(w_fox), _shards_spec(w_sb)],
        out_specs=[_row_spec(), _row_spec(), _row_spec()],
        out_shape=[jax.ShapeDtypeStruct((t, D_MODEL), BF16)] + [jax.ShapeDtypeStruct((t, D_MODEL), F32)] * 2,
        compiler_params=_cparams(("parallel",)),
    )(gl, gl, b_gate, o_fox, o_sb, w_fox, w_sb)


def _gate_bwd(gl, b_gate, of, os_, dx, w_out, w_fox, w_sb):
    t = of.shape[0]
    shard = D_MODEL // N_CHIPS

    def back(d, w_ref):
        return sum(_dot(d[:, j * shard:(j + 1) * shard], w_ref[j], NT_DIMS) for j in range(N_CHIPS)).astype(BF16)

    def body(gla_ref, glb_ref, b_ref, of_ref, os_ref, dx_ref, w_ref, wf_ref, ws_ref,
             dof_ref, dos_ref, dgl_ref, db_ref, dofox_ref, dosb_ref):
        dm = _dot(dx_ref[...], w_ref[...], NT_DIMS)
        ga = jax.nn.sigmoid(gla_ref[...] + b_ref[0:1, :])
        gb = jax.nn.sigmoid(glb_ref[...] + b_ref[1:2, :])
        dof = (dm * ga).astype(BF16)
        dos = (dm * gb).astype(BF16)
        dof_ref[...] = dof
        dos_ref[...] = dos
        dofox_ref[...] = back(dof, wf_ref)
        dosb_ref[...] = back(dos, ws_ref)
        dgla = dm * of_ref[...] * ga * (1.0 - ga)
        dglb = dm * os_ref[...] * gb * (1.0 - gb)
        dgl_ref[:, 0:D_MODEL] = dgla.astype(BF16)
        dgl_ref[:, D_MODEL:2 * D_MODEL] = dglb.astype(BF16)

        @pl.when(pl.program_id(0) == 0)
        def _():
            db_ref[...] = jnp.zeros_like(db_ref)

        db_ref[0:1, :] += jnp.sum(dgla, axis=0, keepdims=True)
        db_ref[1:2, :] += jnp.sum(dglb, axis=0, keepdims=True)

    outs = pl.pallas_call(
        body, name="gate_bwd", grid=(t // ROW_TILE,),
        in_specs=[pl.BlockSpec((ROW_TILE, D_MODEL), lambda i: (i, 0)), pl.BlockSpec((ROW_TILE, D_MODEL), lambda i: (i, 1)),
                  _vec_spec(2), _row_spec(), _row_spec(), _row_spec(), pl.BlockSpec(w_out.shape, lambda i: (0, 0)),
                  _shards_spec(w_fox), _shards_spec(w_sb)],
        out_specs=[_row_spec(), _row_spec(), _row_spec(2 * D_MODEL), _vec_spec(2), _row_spec(D_ATT), _row_spec(D_ATT)],
        out_shape=[jax.ShapeDtypeStruct((t, D_MODEL), BF16)] * 2 + [jax.ShapeDtypeStruct((t, 2 * D_MODEL), BF16),
                                                                      jax.ShapeDtypeStruct((2, D_MODEL), F32)]
        + [jax.ShapeDtypeStruct((t, D_ATT), BF16)] * 2,
        compiler_params=_cparams(("arbitrary",)),
    )(gl, gl, b_gate, of, os_, dx, w_out, w_fox, w_sb)
    return outs


def _head_and_loss(x2, h3, p, w_gate, w_ple, g_final, target):
    t = x2.shape[0]

    def body(x2_ref, h3_ref, p_ref, wg_ref, wp_ref, g_ref, tgt_ref, dx3_ref, dpre_ref, dpe_ref, dg_ref, loss_ref):
        gp = jax.nn.sigmoid(_dot(h3_ref[...], wg_ref[...]))
        p_t = p_ref[...].astype(BF16)
        pe_t = jnp.concatenate([_dot(p_t, wp_ref[j]) for j in range(N_CHIPS)], axis=1)
        x3 = x2_ref[...] + gp * pe_t
        g = g_ref[...]
        xh, _ = _xhat(x3)
        err = xh * g - tgt_ref[...]
        dy = err * (1.0 / D_MODEL)
        dx3, dg = _rms_bwd_rows(dy, x3, g)
        dx3_ref[...] = dx3
        dpre_ref[...] = (dx3 * pe_t * gp * (1.0 - gp)).astype(BF16)
        dpe_ref[...] = (dx3 * gp).astype(BF16)

        @pl.when(pl.program_id(0) == 0)
        def _():
            dg_ref[...] = jnp.zeros_like(dg_ref)
            loss_ref[...] = jnp.zeros_like(loss_ref)

        dg_ref[...] += dg
        loss_ref[...] += 0.5 * jnp.sum(jnp.mean(err * err, axis=-1, keepdims=True), axis=0, keepdims=True)

    return pl.pallas_call(
        body, name="head_and_loss", grid=(t // ROW_TILE,),
        in_specs=[_row_spec(), _row_spec(), _row_spec(D_PLE), pl.BlockSpec(w_gate.shape, lambda i: (0, 0)),
                  pl.BlockSpec(w_ple.shape, lambda i: (0, 0, 0)), _vec_spec(), _row_spec()],
        out_specs=[_row_spec(), _row_spec(), _row_spec(), _vec_spec(), _vec_spec(1, LANES)],
        out_shape=[jax.ShapeDtypeStruct((t, D_MODEL), F32), jax.ShapeDtypeStruct((t, D_MODEL), BF16),
                   jax.ShapeDtypeStruct((t, D_MODEL), BF16), jax.ShapeDtypeStruct((1, D_MODEL), F32),
                   jax.ShapeDtypeStruct((1, LANES), F32)],
        compiler_params=_cparams(("arbitrary",)),
    )(x2, h3, p, w_gate, w_ple, g_final, target)


def _split3(v):
    hi = v.astype(BF16)
    r1 = v - hi.astype(F32)
    mid = r1.astype(BF16)
    lo = (r1 - mid.astype(F32)).astype(BF16)
    return hi, mid, lo


def _split2(v):
    hi = v.astype(BF16)
    return jnp.concatenate([hi, (v - hi.astype(F32)).astype(BF16)], axis=1)


def _dot(a, b, dims=_DIMS["nn"]):
    return lax.dot_general(a, b, dims, preferred_element_type=F32)


def _tri(n, rel):
    row = lax.broadcasted_iota(jnp.int32, (n, n), 0)
    col = lax.broadcasted_iota(jnp.int32, (n, n), 1)
    return rel(row, col).astype(BF16)


def _tri2(n, rel):
    t = _tri(n, rel)
    return jnp.concatenate([t, t], axis=0)


def _log_sigmoid(v):
    return -(jnp.maximum(-v, 0.0) + jnp.log(1.0 + jnp.exp(-jnp.abs(v))))


def _fox_prep(fl, b_forget, batch, seq):
    nb = seq // ATT_BLOCK

    def body(fl_ref, b_ref, cw_ref, cr_ref):
        col = lax.broadcasted_iota(jnp.int32, (ATT_BLOCK, F_PAD), 1)
        lower = _tri(ATT_BLOCK, lambda r, c: c <= r)
        upper = _tri(ATT_BLOCK, lambda r, c: r <= c)
        expand = (lax.broadcasted_iota(jnp.int32, (F_PAD, D_ATT), 1) // HEAD_DIM
                  == lax.broadcasted_iota(jnp.int32, (F_PAD, D_ATT), 0)).astype(BF16)
        carry_w = jnp.zeros((1, D_ATT), F32)
        carry_r = jnp.zeros((F_PAD, 1), F32)
        for i in range(nb):
            blk = slice(i * ATT_BLOCK, (i + 1) * ATT_BLOCK)
            logf = jnp.where(col < N_HEADS, _log_sigmoid(fl_ref[blk, :] + b_ref[...]), 0.0)
            cw = jnp.zeros((ATT_BLOCK, D_ATT), F32) + carry_w
            cr = jnp.zeros((F_PAD, ATT_BLOCK), F32) + carry_r
            for part in _split3(logf):
                cw += _dot(lower, _dot(part, expand).astype(BF16))
                cr += _dot(part, upper, TN_DIMS)
            cw_ref[blk, :] = cw
            cr_ref[:, blk] = cr[0:N_HEADS, :]
            carry_w = cw[ATT_BLOCK - 1:ATT_BLOCK, :]
            carry_r = cr[:, ATT_BLOCK - 1:ATT_BLOCK]

    return pl.pallas_call(
        body, name="fox_prep", grid=(batch,),
        in_specs=[pl.BlockSpec((seq, F_PAD), lambda b: (b, 0)), pl.BlockSpec((1, F_PAD), lambda b: (0, 0))],
        out_specs=[pl.BlockSpec((seq, D_ATT), lambda b: (b, 0)), pl.BlockSpec((N_HEADS, seq), lambda b: (b, 0))],
        out_shape=[jax.ShapeDtypeStruct((batch * seq, D_ATT), F32), jax.ShapeDtypeStruct((batch * N_HEADS, seq), F32)],
        compiler_params=_cparams(("parallel",)),
    )(fl, b_forget)


def _fox_post(dcs_wide, drs_wide, fl, b_forget, batch, seq):
    nb = seq // ATT_BLOCK

    def body(dcs_ref, drs_ref, fl_ref, b_ref, dfl_ref, db_ref):
        pick = (lax.broadcasted_iota(jnp.int32, (D_ATT, F_PAD), 0)
                == lax.broadcasted_iota(jnp.int32, (D_ATT, F_PAD), 1) * HEAD_DIM).astype(BF16)
        upper = _tri(ATT_BLOCK, lambda r, c: r <= c)
        col = lax.broadcasted_iota(jnp.int32, (ATT_BLOCK, F_PAD), 1)

        @pl.when(pl.program_id(0) == 0)
        def _():
            db_ref[...] = jnp.zeros_like(db_ref)

        carry = jnp.zeros((1, F_PAD), F32)
        for i in reversed(range(nb)):
            blk = slice(i * ATT_BLOCK, (i + 1) * ATT_BLOCK)
            narrow = jnp.zeros((ATT_BLOCK, F_PAD), F32)
            for part in _split3(drs_ref[blk, :] - dcs_ref[blk, :]):
                narrow += _dot(part, pick)
            after = jnp.zeros((ATT_BLOCK, F_PAD), F32) + carry
            for part in _split3(narrow):
                after += _dot(upper, part)
            carry = after[0:1, :]
            pre = fl_ref[blk, :] + b_ref[...]
            dfl = jnp.where(col < N_HEADS, after * jax.nn.sigmoid(-pre), 0.0)
            dfl_ref[blk, :] = dfl.astype(BF16)
            db_ref[...] += jnp.sum(dfl, axis=0, keepdims=True)

    return pl.pallas_call(
        body, name="fox_post", grid=(batch,),
        in_specs=[pl.BlockSpec((seq, D_ATT), lambda b: (b, 0)), pl.BlockSpec((seq, D_ATT), lambda b: (b, 0)),
                  pl.BlockSpec((seq, F_PAD), lambda b: (b, 0)), pl.BlockSpec((1, F_PAD), lambda b: (0, 0))],
        out_specs=[pl.BlockSpec((seq, F_PAD), lambda b: (b, 0)), pl.BlockSpec((1, F_PAD), lambda b: (0, 0))],
        out_shape=[jax.ShapeDtypeStruct((batch * seq, F_PAD), BF16), jax.ShapeDtypeStruct((1, F_PAD), F32)],
        compiler_params=_cparams(("arbitrary",)),
    )(dcs_wide, drs_wide, fl, b_forget)


N_PAIRS = N_HEADS // 2


def _att_specs(seq, col0, tq):
    nq = seq // tq
    q = pl.BlockSpec((tq, LANES), lambda b, hp, qi: (b * nq + qi, col0 + hp))
    k = pl.BlockSpec((seq, LANES), lambda b, hp, qi: (b, col0 + N_PAIRS + hp))
    v = pl.BlockSpec((seq, LANES), lambda b, hp, qi: (b, col0 + 2 * N_PAIRS + hp))
    return q, k, v


def _qblock_spec(seq, tq):
    nq = seq // tq
    return pl.BlockSpec((tq, LANES), lambda b, hp, qi: (b * nq + qi, hp))


def _kv_out_spec(seq):
    return pl.BlockSpec((seq, LANES), lambda b, hp, qi: (b, hp))


def _head_masks():
    lane = lax.broadcasted_iota(jnp.int32, (1, LANES), 1)
    return [(lane >= HEAD_DIM * j) & (lane < HEAD_DIM * (j + 1)) for j in range(2)]


def _stack_heads(t, masks):
    zero = jnp.zeros_like(t)
    return jnp.concatenate([jnp.where(masks[0], t, zero), jnp.where(masks[1], t, zero)], axis=0)


def _stack_cols(t):
    return jnp.concatenate([t[:, 0:1], t[:, HEAD_DIM:HEAD_DIM + 1]], axis=0)


def _unstack(t2, masks):
    tq = t2.shape[0] // 2
    return jnp.where(masks[0], t2[:tq], t2[tq:])


def _stacked_ids(tq, tk):
    row = lax.broadcasted_iota(jnp.int32, (2 * tq, tk), 0)
    col = lax.broadcasted_iota(jnp.int32, (2 * tq, tk), 1)
    first = lax.broadcasted_iota(jnp.int32, (2 * tq, 1), 0) < tq
    return col - jnp.where(row < tq, row, row - tq), first


def _sweep(qi, tq, tk, step, init):
    per = tq // tk
    carry = lax.fori_loop(0, per * qi, lambda kb, c: step(kb, c, None, 0), init)
    for j in range(per):
        carry = step(per * qi + j, carry, -j * tk, j * tk)
    return carry


def _below(t2, top):
    tq = t2.shape[0] // 2
    return t2 if top == 0 else jnp.concatenate([t2[top:tq], t2[tq + top:]], axis=0)


def _put_below(old, new, top):
    if top == 0:
        return new
    tq = old.shape[0] // 2
    return jnp.concatenate([old[:top], new[:tq - top], old[tq:tq + top], new[tq - top:]], axis=0)


def _att_fwd(qkv, c_wide, c_row, batch, seq, behind):
    tq, tkf = FOX_TILES
    tqs, tks = SB_TILES
    assert tq == tqs and tkf == 2 * tks
    nq = seq // tq

    def body(qa_ref, ka_ref, va_ref, cw_ref, cr_ref, qb_ref, kb_ref, vb_ref, of_ref, lse_ref, os_ref, rt_ref):
        hp, qi = pl.program_id(1), pl.program_id(2)
        masks = _head_masks()
        ahead_f, first = _stacked_ids(tq, tkf)
        ahead_s, _ = _stacked_ids(tq, tks)
        later = _tri2(tks, lambda r, c: r > c)
        q2f = _stack_heads(qa_ref[...], masks) * SCALE
        q2s = _stack_heads(qb_ref[...], masks) * SCALE
        ct = _stack_cols(cw_ref[...])

        def fox(kb, carry, lead):
            m, l, acc = carry
            k0 = pl.multiple_of(kb * tkf, tkf)
            cs = jnp.where(first, cr_ref[pl.ds(2 * hp, 1), pl.ds(k0, tkf)], cr_ref[pl.ds(2 * hp + 1, 1), pl.ds(k0, tkf)])
            s = _dot(q2f, ka_ref[pl.ds(k0, tkf), :], NT_DIMS) + ct - cs
            if lead is not None:
                s = jnp.where(ahead_f <= lead, s, NEG)
            m_new = jnp.maximum(m, jnp.max(s, axis=1, keepdims=True))
            p = jnp.exp(s - m_new)
            alpha = jnp.exp(m - m_new)
            l = alpha * l + jnp.sum(p, axis=1, keepdims=True)
            acc = alpha * acc + _dot(p.astype(BF16), va_ref[pl.ds(k0, tkf), :])
            return m_new, l, acc

        def sb(kb, carry, lead):
            run, acc = carry
            k0 = pl.multiple_of(kb * tks, tks)
            ls, lsn = _sb_logits(q2s, kb_ref[pl.ds(k0, tks), :])
            if lead is not None:
                lsn = jnp.where(ahead_s < lead, lsn, 0.0)
            w = jnp.exp(ls + _dot(_split2(lsn), later) + run)
            if lead is not None:
                w = jnp.where(ahead_s < lead, w, 0.0)
            return run + jnp.sum(lsn, axis=1, keepdims=True), acc + _dot(w.astype(BF16), vb_ref[pl.ds(k0, tks), :])

        fox_c = (jnp.full((2 * tq, 1), NEG, F32), jnp.zeros((2 * tq, 1), F32), jnp.zeros((2 * tq, LANES), F32))
        sb_c = (jnp.zeros((2 * tq, 1), F32), jnp.zeros((2 * tq, LANES), F32))
        sb_c = sb(2 * qi, sb(2 * qi + 1, sb_c, -tks), 0)

        def both(i, carries):
            fox_c, sb_c = carries
            return fox(i, fox_c, None), sb(2 * qi - 2 - 2 * i, sb(2 * qi - 1 - 2 * i, sb_c, None), None)

        fox_c, (run, acc_s) = lax.fori_loop(0, qi, both, (fox_c, sb_c))
        m, l, acc = fox(qi, fox_c, 0)
        of_ref[...] = _unstack(acc / l, masks).astype(BF16)
        lse_ref[...] = _unstack(m + jnp.log(l), masks)
        os_ref[...] = _unstack(acc_s, masks).astype(BF16)
        rt_ref[...] = _unstack(run, masks)

    qa, ka, va = _att_specs(seq, 0, tq)
    qb_, kb_, vb_ = _att_specs(seq, 3 * N_PAIRS, tq)
    qb = _qblock_spec(seq, tq)
    half, wide = jax.ShapeDtypeStruct((batch * seq, D_ATT), BF16), jax.ShapeDtypeStruct((batch * seq, D_ATT), F32)
    return _call_behind(
        body, behind, name="att_fwd", grid=(batch, N_PAIRS, nq),
        in_specs=[qa, ka, va, qb, pl.BlockSpec((N_HEADS, seq), lambda b, hp, qi: (b, 0)), qb_, kb_, vb_],
        out_specs=[qb, qb, qb, qb], out_shape=[half, wide, half, wide], scratch_shapes=[],
        operands=(qkv, qkv, qkv, c_wide, c_row, qkv, qkv, qkv))


def _fox_bwd(qkv, c_wide, c_row, o, do, lse_wide, batch, seq, behind):
    tq, tk = FOX_TILES
    nq = seq // tq

    def body(q_ref, k_ref, v_ref, cw_ref, cr_ref, o_ref, do_ref, lse_ref,
             dq_ref, dk_ref, dv_ref, dcs_ref, drs_ref, dkc_acc, dv_acc):
        hp, qi = pl.program_id(1), pl.program_id(2)

        @pl.when(qi == 0)
        def _():
            dkc_acc[...] = jnp.zeros_like(dkc_acc)
            dv_acc[...] = jnp.zeros_like(dv_acc)

        masks = _head_masks()
        ahead, first = _stacked_ids(tq, tk)
        q_t, do_t = q_ref[...], do_ref[...]
        q2 = _stack_heads(q_t, masks) * SCALE
        do2 = _stack_heads(do_t, masks)
        q_and_ones = jnp.concatenate([q2, _stack_heads(jnp.ones_like(q_t), masks)], axis=1)
        ct = _stack_cols(cw_ref[...])
        lse = _stack_cols(lse_ref[...])
        prod = do_t.astype(F32) * o_ref[...].astype(F32)
        delta = jnp.concatenate([jnp.sum(jnp.where(mk, prod, 0.0), axis=1, keepdims=True) for mk in masks], axis=0)

        def step(kb, carry, lead, top):
            dq_acc, rs = carry
            k0 = pl.multiple_of(kb * tk, tk)
            kblk = k_ref[pl.ds(k0, tk), :]
            cs = jnp.where(first, cr_ref[pl.ds(2 * hp, 1), pl.ds(k0, tk)], cr_ref[pl.ds(2 * hp + 1, 1), pl.ds(k0, tk)])
            p = jnp.exp(_dot(q2, kblk, NT_DIMS) + ct - cs - lse)
            if lead is not None:
                p = jnp.where(ahead <= lead, p, 0.0)
            dp = _dot(do2, v_ref[pl.ds(k0, tk), :], NT_DIMS)
            ds = (p * (dp - delta)).astype(BF16)
            dkc_acc[pl.ds(k0, tk), :] += _dot(ds, q_and_ones, TN_DIMS)
            dv_acc[pl.ds(k0, tk), :] += _dot(p.astype(BF16), do2, TN_DIMS)
            return dq_acc + _dot(ds, kblk), rs + jnp.sum(ds.astype(F32), axis=1, keepdims=True)

        init = (jnp.zeros((2 * tq, LANES), F32), jnp.zeros((2 * tq, 1), F32))
        dq_acc, rs = _sweep(qi, tq, tk, step, init)
        dq_ref[...] = (_unstack(dq_acc, masks) * SCALE).astype(BF16)
        drs_ref[...] = _unstack(rs, masks)

        @pl.when(qi == nq - 1)
        def _():
            dk_ref[...] = dkc_acc[:, 0:LANES].astype(BF16)
            dcs_ref[...] = dkc_acc[:, LANES:2 * LANES]
            dv_ref[...] = dv_acc[...].astype(BF16)

    q_spec, k_spec, v_spec = _att_specs(seq, 0, tq)
    qb = _qblock_spec(seq, tq)
    return _call_behind(
        body, behind, name="fox_bwd", grid=(batch, N_PAIRS, nq),
        in_specs=[q_spec, k_spec, v_spec, qb, pl.BlockSpec((N_HEADS, seq), lambda b, hp, qi: (b, 0)), qb, qb, qb],
        out_specs=[qb, _kv_out_spec(seq), _kv_out_spec(seq), _kv_out_spec(seq), qb],
        out_shape=[jax.ShapeDtypeStruct((batch * seq, D_ATT), BF16)] * 3 + [jax.ShapeDtypeStruct((batch * seq, D_ATT), F32)] * 2,
        scratch_shapes=[pltpu.VMEM((seq, 2 * LANES), F32), pltpu.VMEM((seq, LANES), F32)],
        operands=(qkv, qkv, qkv, c_wide, c_row, o, do, lse_wide))


def _sb_logits(q2, kblk):
    z = _dot(q2, kblk, NT_DIMS)
    lsn = jnp.minimum(-z, 0.0) - jnp.log(1.0 + jnp.exp(-jnp.abs(z)))
    return lsn + z, lsn


def _sb_bwd(qkv, do, rt_wide, batch, seq, behind):
    tq, tk = SB_TILES
    nq = seq // tq

    def body(q_ref, k_ref, v_ref, do_ref, rt_ref, dq_ref, dk_ref, dv_ref, dk_acc, dv_acc):
        qi = pl.program_id(2)

        @pl.when(qi == 0)
        def _():
            dk_acc[...] = jnp.zeros_like(dk_acc)
            dv_acc[...] = jnp.zeros_like(dv_acc)

        masks = _head_masks()
        ahead, _ = _stacked_ids(tq, tk)
        later = _tri2(tk, lambda r, c: r > c)
        earlier = _tri(tk, lambda r, c: r < c)
        q2 = _stack_heads(q_ref[...], masks) * SCALE
        do2 = _stack_heads(do_ref[...], masks)
        total = _stack_cols(rt_ref[...])

        def step(kb, carry, lead, top):
            pref, epre, dq_acc = (_below(t, top) for t in carry)
            q_s, do_s = _below(q2, top), _below(do2, top)
            seen = None if lead is None else _below(ahead, top) < lead
            k0 = pl.multiple_of(kb * tk, tk)
            kblk = k_ref[pl.ds(k0, tk), :]
            ls, lsn_all = _sb_logits(q_s, kblk)
            lsn = lsn_all if lead is None else jnp.where(seen, lsn_all, 0.0)
            rs = jnp.sum(lsn, axis=1, keepdims=True)
            w = jnp.exp(ls + _dot(_split2(lsn), later) + (_below(total, top) - pref - rs))
            if lead is not None:
                w = jnp.where(seen, w, 0.0)
            e = w * _dot(do_s, v_ref[pl.ds(k0, tk), :], NT_DIMS)
            before = _dot(e.astype(BF16), earlier) + epre
            dz = e * jnp.exp(lsn_all) - jnp.exp(ls) * before
            if lead is not None:
                dz = jnp.where(seen, dz, 0.0)
            dz = dz.astype(BF16)
            dk_acc[pl.ds(k0, tk), :] += _dot(dz, q_s, TN_DIMS)
            dv_acc[pl.ds(k0, tk), :] += _dot(w.astype(BF16), do_s, TN_DIMS)
            new = (pref + rs, epre + jnp.sum(e, axis=1, keepdims=True), dq_acc + _dot(dz, kblk))
            return tuple(_put_below(o, n, top) for o, n in zip(carry, new))

        init = (jnp.zeros((2 * tq, 1), F32), jnp.zeros((2 * tq, 1), F32), jnp.zeros((2 * tq, LANES), F32))
        dq_acc = _sweep(qi, tq, tk, step, init)[2]
        dq_ref[...] = (_unstack(dq_acc, masks) * SCALE).astype(BF16)

        @pl.when(qi == nq - 1)
        def _():
            dk_ref[...] = dk_acc[...].astype(BF16)
            dv_ref[...] = dv_acc[...].astype(BF16)

    q_spec, k_spec, v_spec = _att_specs(seq, 3 * N_PAIRS, tq)
    qb = _qblock_spec(seq, tq)
    return _call_behind(
        body, behind, name="sb_bwd", grid=(batch, N_PAIRS, nq), in_specs=[q_spec, k_spec, v_spec, qb, qb],
        out_specs=[qb, _kv_out_spec(seq), _kv_out_spec(seq)], out_shape=[jax.ShapeDtypeStruct((batch * seq, D_ATT), BF16)] * 3,
        scratch_shapes=[pltpu.VMEM((seq, LANES), F32), pltpu.VMEM((seq, LANES), F32)], operands=(qkv, qkv, qkv, do, rt_wide))


def _local_step(x, p, target, first, rest, vec, place):
    batch, seq, _ = x.shape
    t = batch * seq
    x = x.reshape(t, D_MODEL)
    target = target.reshape(t, D_MODEL)
    p = p.reshape(t, D_PLE)
    big = dict(tm=1024, tn=1024, tk=1024)

    (h1,), (w_in_slots,) = _norm_fwd(x, vec["g_mix"], "norm_mix", first)
    w = _first_weights(w_in_slots)
    qkv = _mm(h1, w["qkv"], mode="nn", name="proj_qkv", out_dtype=BF16, **big)
    gl = _mm(h1, w["gate"], mode="nn", name="proj_gate", **big)
    fl = _mm(h1, w["forget"], mode="nn", name="proj_forget", **big)
    c_wide, c_row = _fox_prep(fl, vec["b_forget"], batch, seq)
    (o_fox, lse_wide, o_sb, rt_wide), gathered = _att_fwd(qkv, c_wide, c_row, batch, seq, rest)
    w = dict(w, **_rest_weights(dict(zip(EARLY + ("b_gate",), gathered))))
    merged, of, os_ = _gate_fwd(gl, w["b_gate"], o_fox, o_sb, w["branch_fox"], w["branch_sb"])
    x1, h2 = _mm_res_norm(merged, w["out"], x, vec["g_mlp"], "proj_out_norm")
    ar = _mm(h2, w["up"], mode="nn", name="mlp_up", out_dtype=BF16, epi=lambda acc, _: jnp.maximum(acc, 0.0),
             col_shards=True, **big)
    x2, h3 = _mm_res_norm(ar, w["down"], x1, vec["g_ple"], "mlp_down_norm", a_fn=_relu2)

    dx3, dpre, dpe, dg_final, loss = _head_and_loss(x2, h3, p, w["ple_gate"], w["ple"], vec["g_final"], target)
    gw = {}
    gw["ple"] = _mm(p, dpe, mode="tn", name="d_w_ple", col_shards=True, **big)
    gw["ple_gate"] = _mm(h3, dpre, mode="tn", name="d_w_ple_gate", **big)
    dx2, dx2b, dg_ple = _mm_norm_bwd([(dpre, w["ple_gate"])], None, x2, vec["g_ple"], dx3, "d_h_ple_norm_bwd")
    gw["down"] = _mm(ar, dx2b, mode="tn", name="d_w_down", a_fn=_relu2, **big)
    da = _mm(dx2b, w["down"], mode="nt", name="d_act", out_dtype=BF16,
             epi=lambda acc, r: acc * (2.0 * r.astype(F32)), extra=ar, **big)
    gw["up"] = _mm(h2, da, mode="tn", name="d_w_up", col_shards=True, **big)
    dx1, dx1b, dg_mlp = _mm_norm_bwd([(da, w["up"])], None, x1, vec["g_mlp"], dx2, "d_h_mlp_norm_bwd")
    gw["out"] = _mm(merged, dx1b, mode="tn", name="d_w_out", **big)
    dof, dos, dgl, gw["b_gate"], do_fox, do_sb = _gate_bwd(gl, w["b_gate"], of, os_, dx1b, w["out"], w["branch_fox"],
                                                                  w["branch_sb"])
    gw["branch_fox"] = _mm(o_fox, dof, mode="tn", name="d_w_branch_fox", col_shards=True, **big)
    gw["branch_sb"] = _mm(o_sb, dos, mode="tn", name="d_w_branch_sb", col_shards=True, **big)
    early = _early_slots(gw)
    early = [early[n] for n in EARLY]
    (dq_a, dk_a, dv_a, dcs_wide, drs_wide), received = _fox_bwd(qkv, c_wide, c_row, o_fox, do_fox, lse_wide, batch, seq,
                                                                _swap_halves(early))
    sums = _sum_sibling(place, early, received, "sum_sibling_early")
    (dq_b, dk_b, dv_b), others = _sb_bwd(qkv, do_sb, rt_wide, batch, seq, _exchange_chips(sums))
    mine = [None] * len(EARLY)
    for group, tag in ((BIG, "big"), (SMALL, "small")):
        for t, res in zip(group, _sum_chips(place, *[[a[t] for t in group] for a in (early, received, others)], "sum_chips_" + tag)):
            mine[t] = res
    dfl, db_forget = _fox_post(dcs_wide, drs_wide, fl, vec["b_forget"], batch, seq)
    dqkv = jnp.concatenate([dq_a, dk_a, dv_a, dq_b, dk_b, dv_b], axis=1)
    gw["qkv"], theirs = _mm(dqkv, h1, mode="tn", name="d_w_qkv", behind=_share_halves(mine), flat_out=True, **big)
    reduced = dict(zip(EARLY, zip(mine, theirs)))
    gw["gate"] = _mm(dgl, h1, mode="tn", name="d_w_gate", flat_out=True, **big)
    gw["forget"] = _mm(dfl, h1, mode="tn", name="d_w_forget", flat_out=True, **big)
    late = [_w_in_slots(gw)]
    dh1, received = _mm(dqkv, w["qkv"], mode="nt", name="d_h_qkv", behind=_swap_halves(late), **big)
    sums = _sum_sibling(place, late, received, "sum_sibling_w_in")
    (grad_x, _, dg_mix), others = _mm_norm_bwd([(dgl, w["gate"]), (dfl, w["forget"])], dh1, x, vec["g_mix"], dx1,
                                               "d_h_gate_norm_bwd", behind=_exchange_chips(sums))
    mine = _sum_chips(place, late, received, others, "sum_chips_w_in")
    reduced["w_in"] = (mine[0], _run_exchange(_share_halves(mine), "reduce_share_w_in")[0])
    gvec = {"g_mix": dg_mix, "b_forget": db_forget[:, 0:N_HEADS], "g_mlp": dg_mlp, "g_ple": dg_ple,
            "g_final": dg_final, "b_gate": gw["b_gate"]}
    return loss, grad_x.reshape(batch, seq, D_MODEL), reduced, gvec


ANY = pl.BlockSpec(memory_space=pl.ANY)
SHARDED = ("w_in", "w_branch_fox", "w_branch_sb", "w_out", "w_up", "w_down", "w_ple_gate", "w_ple")
ROW_ALIGN = 16
F32_ROWS = 8


def _place():
    return lax.axis_index("x"), lax.axis_index("y"), lax.axis_index("c")


def _other_chips(x, y):
    return [(1 - x, y), (x, 1 - y), (1 - x, 1 - y)]


def _half(ref, h):
    r = ref.shape[0] // 2
    assert r % ROW_ALIGN == 0
    return ref.at[pl.ds(pl.multiple_of(h * r, ROW_ALIGN), r)]


def _remote(src, dst, sems, idx, to):
    send_sems, recv_sems = sems
    return pltpu.make_async_remote_copy(src_ref=src, dst_ref=dst, send_sem=send_sems.at[idx], recv_sem=recv_sems.at[idx],
                                        device_id=to, device_id_type=MESH)


class _Exchange:
    def __init__(self, operands, out_shapes, sem_shape, start, finish):
        self.operands, self.out_shapes, self.sem_shape, self.start, self.finish = operands, out_shapes, sem_shape, start, finish

    def scratch(self):
        return [pltpu.SemaphoreType.DMA(self.sem_shape), pltpu.SemaphoreType.DMA(self.sem_shape)]


def _run_exchange(ex, name):
    n = len(ex.operands)

    def body(*refs):
        ex.start(refs[:n], refs[n:2 * n], refs[2 * n:])
        ex.finish(refs[:n], refs[n:2 * n], refs[2 * n:])

    return pl.pallas_call(body, name=name, in_specs=[ANY] * n, out_specs=[ANY] * n, out_shape=ex.out_shapes,
                          scratch_shapes=ex.scratch())(*ex.operands)


def _call_behind(body, ex, *, name, grid, in_specs, out_specs, out_shape, scratch_shapes, operands):
    n_in, n_out, nx = len(in_specs), len(out_specs), len(ex.operands)

    def wrapped(*refs):
        ins, x_in = refs[:n_in], refs[n_in:n_in + nx]
        outs, x_out = refs[n_in + nx:n_in + nx + n_out], refs[n_in + nx + n_out:n_in + 2 * nx + n_out]
        scratch, sems = refs[n_in + 2 * nx + n_out:-2], refs[-2:]
        first, last = None, None
        for d, steps in enumerate(grid):
            at_start, at_end = pl.program_id(d) == 0, pl.program_id(d) == steps - 1
            first = at_start if first is None else first & at_start
            last = at_end if last is None else last & at_end

        @pl.when(first)
        def _():
            ex.start(x_in, x_out, sems)

        body(*ins, *outs, *scratch)

        @pl.when(last)
        def _():
            ex.finish(x_in, x_out, sems)

    res = pl.pallas_call(
        wrapped, name=name, grid=grid, in_specs=list(in_specs) + [ANY] * nx, out_specs=list(out_specs) + [ANY] * nx,
        out_shape=list(out_shape) + list(ex.out_shapes), scratch_shapes=list(scratch_shapes) + ex.scratch(),
        compiler_params=_cparams(("arbitrary",) * len(grid)),
    )(*operands, *ex.operands)
    return res[:n_out], res[n_out:]


def _gather_weights(shards):
    n = len(shards)

    def first_copies(src, out, sems):
        x, y, c = _place()
        me = 2 * x + y
        copies = [_remote(_half(src[t], c), _half(out[t].at[me], c), sems, (t, k), (px, py, c))
                  for t in range(n) for k, (px, py) in enumerate(_other_chips(x, y))]
        return copies + [_remote(src[t], out[t].at[me], sems, (t, 3), (x, y, 1 - c)) for t in range(n)]

    def start(src, out, sems):
        for cp in first_copies(src, out, sems):
            cp.start()

    def finish(src, out, sems):
        x, y, c = _place()
        me = 2 * x + y
        sibling = (x, y, 1 - c)
        chips = _other_chips(x, y)
        passes = []
        for t in range(n):
            for k, (px, py) in enumerate(chips):
                landed = _half(out[t].at[2 * px + py], c)
                _remote(landed, landed, sems, (t, k), (px, py, c)).wait_recv()
                passes.append(_remote(landed, landed, sems, (t, 4 + k), sibling))
                passes[-1].start()
        for t in range(n):
            _remote(src[t], out[t].at[me], sems, (t, 3), sibling).wait_recv()
            for k, (px, py) in enumerate(chips):
                passed = _half(out[t].at[2 * px + py], 1 - c)
                _remote(passed, passed, sems, (t, 4 + k), sibling).wait_recv()
        for cp in first_copies(src, out, sems) + passes:
            cp.wait_send()

    return _Exchange(shards, [jax.ShapeDtypeStruct((N_CHIPS,) + s.shape, s.dtype) for s in shards], (n, 7), start, finish)


def _simple_exchange(operands, out_shapes, copies):
    def start(src, out, sems):
        for cp in copies(src, out, sems):
            cp.start()

    def finish(src, out, sems):
        for cp in copies(src, out, sems):
            cp.wait_recv()
        for cp in copies(src, out, sems):
            cp.wait_send()

    return _Exchange(operands, out_shapes, (len(operands),), start, finish)


def _swap_halves(slots):
    def copies(src, out, sems):
        x, y, c = _place()
        res = []
        for t in range(len(slots)):
            r = src[t].shape[1] // 2
            rows = pl.ds(pl.multiple_of((1 - c) * r, F32_ROWS), r)
            res.append(_remote(src[t].at[:, rows], out[t], sems, t, (x, y, 1 - c)))
        return res

    return _simple_exchange(slots, [jax.ShapeDtypeStruct((N_CHIPS, s.shape[1] // 2, s.shape[2]), s.dtype) for s in slots], copies)


def _exchange_chips(sums):
    n = len(sums)

    def copies(src, out, sems):
        x, y, c = _place()
        return [_remote(src[t].at[2 * px + py], out[t].at[k], sems, (t, k), (px, py, c))
                for t in range(n) for k, (px, py) in enumerate(_other_chips(x, y))]

    def start(src, out, sems):
        for cp in copies(src, out, sems):
            cp.start()

    def finish(src, out, sems):
        for cp in copies(src, out, sems):
            cp.wait_recv()
        for cp in copies(src, out, sems):
            cp.wait_send()

    return _Exchange(sums, [jax.ShapeDtypeStruct((3,) + s.shape[1:], s.dtype) for s in sums], (n, 3), start, finish)


def _share_halves(mine):
    def copies(src, out, sems):
        x, y, c = _place()
        return [_remote(src[t], out[t], sems, t, (x, y, 1 - c)) for t in range(len(mine))]

    return _simple_exchange(mine, [jax.ShapeDtypeStruct(s.shape, s.dtype) for s in mine], copies)


def _walk(name, place, parts):
    starts = [sum(p[0] for p in parts[:t]) for t in range(len(parts))]
    held = lambda index, start, steps: (lambda s, pr: index(jnp.clip(s - start, 0, steps - 1), pr))
    in_specs, out_specs, out_shapes, operands = [], [], [], []
    for (steps, ins, outs, shapes, ops, _), start in zip(parts, starts):
        in_specs += [pl.BlockSpec(blk, held(index, start, steps)) for blk, index in ins]
        out_specs += [pl.BlockSpec(blk, held(index, start, steps)) for blk, index in outs]
        out_shapes += list(shapes)
        operands += list(ops)

    def body(place_ref, *refs):
        s = pl.program_id(0)
        i, o = 0, len(in_specs)
        for (steps, ins, outs, _, _, fn), start in zip(parts, starts):
            mine_in, mine_out = refs[i:i + len(ins)], refs[o:o + len(outs)]
            i, o = i + len(ins), o + len(outs)

            @pl.when((s >= start) & (s < start + steps))
            def _(mine_in=mine_in, mine_out=mine_out, start=start, fn=fn):
                fn(s - start, mine_in, mine_out)

    res = pl.pallas_call(
        body, name=name, out_shape=out_shapes,
        grid_spec=pltpu.PrefetchScalarGridSpec(num_scalar_prefetch=1, grid=(sum(p[0] for p in parts),), in_specs=in_specs,
                                               out_specs=out_specs),
        compiler_params=_cparams(("arbitrary",)),
    )(place, *operands)
    counts = [len(p[2]) for p in parts]
    return [res[sum(counts[:t]):sum(counts[:t + 1])] for t in range(len(parts))]


def _sum_sibling(place, slots, received, name):
    def part(slot, got):
        n, rows, cols = got.shape
        block = (None, rows, cols)

        def fn(j, ins, outs):
            outs[0][...] = (ins[0][...] + ins[1][...]).astype(BF16)

        return (n, [(block, lambda j, pr: (j, pr[1], 0)), (block, lambda j, pr: (j, 0, 0))], [(block, lambda j, pr: (j, 0, 0))],
                [jax.ShapeDtypeStruct(got.shape, BF16)], [slot, got], fn)

    return [r[0] for r in _walk(name, place, [part(s, g) for s, g in zip(slots, received)])]


def _sum_chips(place, slots, received, others, name):
    def part(slot, got, other):
        _, rows, cols = got.shape
        block = (None, rows, cols)

        def fn(_, ins, outs):
            own = ins[0][...] + ins[1][...]
            outs[0][...] = ((own + ins[2][0].astype(F32)) + ins[2][1].astype(F32)) + ins[2][2].astype(F32)

        return (1, [(block, lambda _, pr: (pr[0], pr[1], 0)), (block, lambda _, pr: (pr[0], 0, 0)),
                    ((3, rows, cols), lambda _, pr: (0, 0, 0))], [((rows, cols), lambda _, pr: (0, 0))],
                [jax.ShapeDtypeStruct((rows, cols), F32)], [slot, got, other], fn)

    return [r[0] for r in _walk(name, place, [part(*t) for t in zip(slots, received, others)])]


N_DEVICES = 8


def _sum_devices(block, name):
    def body(v_ref, o_ref, land_ref, send_sems, recv_sems):
        x, y, c = _place()
        me = 4 * x + 2 * y + c
        copies = []
        for mask in range(1, N_DEVICES):
            peer = (x ^ (mask >> 2), y ^ ((mask >> 1) & 1), c ^ (mask & 1))
            copies.append(pltpu.make_async_remote_copy(src_ref=v_ref, dst_ref=land_ref.at[me], send_sem=send_sems.at[mask - 1],
                                                       recv_sem=recv_sems.at[mask - 1], device_id=peer, device_id_type=MESH))
        for cp in copies:
            cp.start()
        land_ref[me] = v_ref[...]
        for cp in copies:
            cp.wait_recv()
        total = land_ref[0]
        for d in range(1, N_DEVICES):
            total = total + land_ref[d]
        o_ref[...] = total
        for cp in copies:
            cp.wait_send()

    vmem = pl.BlockSpec(memory_space=pltpu.VMEM)
    return pl.pallas_call(
        body, name=name, in_specs=[vmem], out_specs=vmem, out_shape=jax.ShapeDtypeStruct(block.shape, F32),
        scratch_shapes=[pltpu.VMEM((N_DEVICES,) + block.shape, F32), pltpu.SemaphoreType.DMA((N_DEVICES - 1,)),
                        pltpu.SemaphoreType.DMA((N_DEVICES - 1,))],
    )(block)


def _vec_block(g_mix, g_mlp, g_ple, g_final, b_forget, b_gate_rows, last=None):
    pad = lambda a: jnp.concatenate([a, jnp.zeros((a.shape[0], D_MODEL - a.shape[1]), F32)], axis=1)
    last = jnp.zeros((1, 0), F32) if last is None else last
    return jnp.concatenate([g_mix, g_mlp, g_ple, g_final.reshape(1, D_MODEL), pad(b_forget), pad(b_gate_rows), pad(last)],
                           axis=0)


def _adam_math(w, g, m, v):
    m_new = ADAM_B1 * m + (1.0 - ADAM_B1) * g
    v_new = ADAM_B2 * v + (1.0 - ADAM_B2) * (g * g)
    m_hat = m_new / (1.0 - ADAM_B1 ** ADAM_STEP)
    v_hat = v_new / (1.0 - ADAM_B2 ** ADAM_STEP)
    return -ADAM_LR * (m_hat / (jnp.sqrt(v_hat) + ADAM_EPS) + ADAM_WD * w), m_new, v_new


def _adamw_halves(place, weights, name):
    def part(w, m, v, g_mine, g_theirs):
        rows, cols = g_mine.shape
        whole = ((rows, cols), lambda s, pr: (pr[1] + s - 2 * pr[1] * s, 0))
        half = ((rows, cols), lambda s, pr: (0, 0))

        def fn(s, ins, outs):
            g = jnp.where(s == 0, ins[3][...], ins[4][...])
            outs[0][...] = g
            outs[1][...], outs[2][...], outs[3][...] = _adam_math(ins[0][...], g, ins[1][...], ins[2][...])

        return (2, [whole] * 3 + [half] * 2, [whole] * 4, [jax.ShapeDtypeStruct(w.shape, F32)] * 4, [w, m, v, g_mine, g_theirs], fn)

    return _walk(name, place, [part(*t) for t in weights])


def _adamw_vec(w, g, m, v):
    def body(w_ref, g_ref, m_ref, v_ref, d_ref, nm_ref, nv_ref):
        d_ref[...], nm_ref[...], nv_ref[...] = _adam_math(w_ref[...], g_ref[...], m_ref[...], v_ref[...])

    return pl.pallas_call(body, name="adamw_vectors", out_shape=[jax.ShapeDtypeStruct(w.shape, F32)] * 3)(w, g, m, v)


WEIGHT_NAMES = ("g_mix", "w_in", "b_forget", "b_gate", "w_branch_fox", "w_branch_sb", "w_out", "g_mlp", "w_up", "w_down",
                "g_ple", "w_ple_gate", "w_ple", "g_final")
W_IN_SHARD = D_IN // N_CHIPS
Q_END, F_END, B_END = 3 * D_ATT, 3 * D_ATT + N_HEADS, 6 * D_ATT + N_HEADS
GATE_SHARD = D_MODEL // N_CHIPS


LATE = SHARDED[:1]
EARLY = SHARDED[1:]
BIG = tuple(t for t, n in enumerate(EARLY) if n in ("w_up", "w_down"))
SMALL = tuple(t for t in range(len(EARLY)) if t not in BIG)


def _first_weights(w_in_slots):
    def cols(*ranges):
        parts = []
        for lo, hi in ranges:
            for j in range(N_CHIPS):
                a, b = max(lo, j * W_IN_SHARD), min(hi, (j + 1) * W_IN_SHARD)
                if a < b:
                    parts.append(w_in_slots[j, :, a - j * W_IN_SHARD:b - j * W_IN_SHARD])
        return parts

    forget = jnp.concatenate(cols((Q_END, F_END)) + [jnp.zeros((D_MODEL, F_PAD - N_HEADS), BF16)], axis=1)
    return {"qkv": jnp.concatenate(cols((0, Q_END), (F_END, B_END)), axis=1), "gate": jnp.concatenate(cols((B_END, D_IN)), axis=1),
            "forget": forget}


GATE_ROWS = 2 * ROW_ALIGN


def _gate_bits(b_gate):
    bits = lax.bitcast_convert_type(b_gate, BF16).reshape(2, 2 * GATE_SHARD)
    return jnp.concatenate([bits, jnp.zeros((GATE_ROWS - 2, 2 * GATE_SHARD), BF16)], axis=0)


def _rest_weights(gathered):
    rows = lambda a: a.reshape(N_CHIPS * a.shape[1], a.shape[2])
    bits = gathered["b_gate"][:, :2].reshape(N_CHIPS, 2, GATE_SHARD, 2)
    b_gate = jnp.transpose(lax.bitcast_convert_type(bits, F32), (1, 0, 2)).reshape(2, D_MODEL)
    return {"branch_fox": gathered["w_branch_fox"], "branch_sb": gathered["w_branch_sb"], "out": rows(gathered["w_out"]),
            "up": gathered["w_up"], "down": rows(gathered["w_down"]), "ple_gate": rows(gathered["w_ple_gate"]),
            "ple": gathered["w_ple"], "b_gate": b_gate}


def _early_slots(gw):
    rows = lambda a: a.reshape(N_CHIPS, a.shape[0] // N_CHIPS, a.shape[1])
    return {"w_branch_fox": gw["branch_fox"], "w_branch_sb": gw["branch_sb"], "w_out": rows(gw["out"]), "w_up": gw["up"],
            "w_down": rows(gw["down"]), "w_ple_gate": rows(gw["ple_gate"]), "w_ple": gw["ple"]}


W_IN_FLAT = (W_IN_SHARD * D_MODEL // LANES, LANES)


def _w_in_slots(gw):
    c = D_MODEL // LANES
    g_t = jnp.concatenate([gw["qkv"][:Q_END * c], gw["forget"][:N_HEADS * c], gw["qkv"][Q_END * c:], gw["gate"]], axis=0)
    return g_t.reshape((N_CHIPS,) + W_IN_FLAT)


def _flat(a):
    return jnp.transpose(a, (2, 0, 1)).reshape(W_IN_FLAT)


def _unflat(a):
    return jnp.transpose(a.reshape(W_IN_SHARD, D_MODEL // LANES, LANES), (1, 2, 0)).reshape(1, D_MODEL, W_IN_SHARD)


def kernel(x, p, g_mix, w_in, b_forget, b_gate, w_branch_fox, w_branch_sb, w_out, g_mlp, w_up, w_down, g_ple, w_ple_gate, w_ple, g_final, loss_target, m_g_mix, m_w_in, m_b_forget, m_b_gate, m_w_branch_fox, m_w_branch_sb, m_w_out, m_g_mlp, m_w_up, m_w_down, m_g_ple, m_w_ple_gate, m_w_ple, m_g_final, v_g_mix, v_w_in, v_b_forget, v_b_gate, v_w_branch_fox, v_w_branch_sb, v_w_out, v_g_mlp, v_w_up, v_w_down, v_g_ple, v_w_ple_gate, v_w_ple, v_g_final):
    weights = dict(g_mix=g_mix, w_in=w_in, b_forget=b_forget, b_gate=b_gate, w_branch_fox=w_branch_fox,
                   w_branch_sb=w_branch_sb, w_out=w_out, g_mlp=g_mlp, w_up=w_up, w_down=w_down, g_ple=g_ple,
                   w_ple_gate=w_ple_gate, w_ple=w_ple, g_final=g_final)
    first = dict(g_mix=m_g_mix, w_in=m_w_in, b_forget=m_b_forget, b_gate=m_b_gate, w_branch_fox=m_w_branch_fox,
                 w_branch_sb=m_w_branch_sb, w_out=m_w_out, g_mlp=m_g_mlp, w_up=m_w_up, w_down=m_w_down, g_ple=m_g_ple,
                 w_ple_gate=m_w_ple_gate, w_ple=m_w_ple, g_final=m_g_final)
    second = dict(g_mix=v_g_mix, w_in=v_w_in, b_forget=v_b_forget, b_gate=v_b_gate, w_branch_fox=v_w_branch_fox,
                  w_branch_sb=v_w_branch_sb, w_out=v_w_out, g_mlp=v_g_mlp, w_up=v_w_up, w_down=v_w_down, g_ple=v_g_ple,
                  w_ple_gate=v_w_ple_gate, w_ple=v_w_ple, g_final=v_g_final)
    cx, cy, cc = _place()
    chip = 2 * cx + cy
    place = jnp.stack([chip, cc]).astype(jnp.int32)
    col0 = chip * GATE_SHARD

    first_gather = _gather_weights([weights[n][0].astype(BF16) for n in LATE])
    rest = _gather_weights([weights[n][0].astype(BF16) for n in EARLY] + [_gate_bits(b_gate[0])])
    vec = {"g_mix": g_mix, "b_forget": jnp.concatenate([b_forget, jnp.zeros((1, F_PAD - N_HEADS), F32)], axis=1),
           "g_mlp": g_mlp, "g_ple": g_ple, "g_final": g_final.reshape(1, D_MODEL)}

    loss, grad_x, reduced, gvec = _local_step(x, p[0], loss_target, first_gather, rest, vec, place)

    out = {}
    args = lambda n: (weights[n][0], first[n][0], second[n][0]) + tuple(reduced[n])
    for names, tag in [([EARLY[t] for t in SMALL], "small")] + [([EARLY[t]], EARLY[t]) for t in BIG]:
        for n, res in zip(names, _adamw_halves(place, [args(n) for n in names], "adamw_" + tag)):
            out[n] = [r[None] for r in res]
    (res,) = _adamw_halves(place, [(_flat(w_in), _flat(m_w_in), _flat(v_w_in)) + tuple(reduced["w_in"])], "adamw_w_in")
    out["w_in"] = [_unflat(r) for r in res]

    g_block = _sum_devices(_vec_block(gvec["g_mix"], gvec["g_mlp"], gvec["g_ple"], gvec["g_final"][0], gvec["b_forget"],
                                      gvec["b_gate"], loss), "reduce_vectors")
    loss = g_block[7, 0]
    g_gate = lax.dynamic_slice(g_block[5:7], (0, col0), (2, GATE_SHARD))
    blocks = [_vec_block(d["g_mix"], d["g_mlp"], d["g_ple"], d["g_final"], d["b_forget"], d["b_gate"][0])
              for d in (weights, first, second)]
    g_rows = jnp.concatenate([g_block[0:5], jnp.concatenate([g_gate, jnp.zeros((2, D_MODEL - GATE_SHARD), F32)], axis=1),
                              jnp.zeros((1, D_MODEL), F32)], axis=0)
    res = (g_rows,) + tuple(_adamw_vec(blocks[0], g_rows, blocks[1], blocks[2]))
    out["g_mix"] = [r[0:1] for r in res]
    out["g_mlp"] = [r[1:2] for r in res]
    out["g_ple"] = [r[2:3] for r in res]
    out["g_final"] = [r[3] for r in res]
    out["b_forget"] = [r[4:5, :N_HEADS] for r in res]
    out["b_gate"] = [r[5:7, :GATE_SHARD][None] for r in res]
    return (loss, grad_x, *[out[n][0] for n in WEIGHT_NAMES], *[out[n][1] for n in WEIGHT_NAMES],
            *[out[n][2] for n in WEIGHT_NAMES], *[out[n][3] for n in WEIGHT_NAMES])
```

```python
import jax
import jax.numpy as jnp
from jax import lax
from jax.experimental import pallas as pl
from jax.experimental.pallas import tpu as pltpu

F32 = jnp.float32
BF16 = jnp.bfloat16

D_MODEL = 1024
HEAD_DIM = 64
N_HEADS = 8
D_ATT = N_HEADS * HEAD_DIM
D_PLE = 256
D_IN = 6 * D_ATT + N_HEADS + 2 * D_MODEL
F_PAD = 128
EPS = 1e-6
SCALE = HEAD_DIM ** -0.5
N_CHIPS = 4
LANES = 128
ATT_BLOCK = 256
FOX_TILES = (512, 512)
SB_TILES = (512, 256)
NEG = -1e30

ADAM_LR = 0.001
ADAM_B1 = 0.9
ADAM_B2 = 0.999
ADAM_EPS = 1e-08
ADAM_WD = 0.01
ADAM_STEP = 10

VMEM_LIMIT = 56 * 1024 * 1024

MESH = pl.DeviceIdType.MESH


def _cparams(sem=None):
    return pltpu.CompilerParams(dimension_semantics=sem, vmem_limit_bytes=VMEM_LIMIT)


def _relu2(t):
    t = t.astype(F32)
    return t * t


_DIMS = {"nn": (((1,), (0,)), ((), ())), "nt": (((1,), (1,)), ((), ())), "tn": (((0,), (0,)), ((), ()))}
NT_DIMS = _DIMS["nt"]
TN_DIMS = _DIMS["tn"]


def _mm(a, b, *, mode, name, out_dtype=F32, tm=512, tn=512, tk=512, add=None, a_fn=None, epi=None, extra=None,
        col_shards=False, behind=None, flat_out=False):
    if mode == "nn":
        (m, k), n = a.shape, b.shape[-1]
    elif mode == "nt":
        (m, k), n = a.shape, b.shape[-2]
    else:
        (k, m), n = a.shape, b.shape[1]
    shard = None
    if col_shards:
        if mode == "nn":
            shard, n = n, N_CHIPS * n
            tn = min(tn, shard)
        elif mode == "nt":
            shard = b.shape[-1]
            tk = min(tk, shard)
        else:
            shard = n // N_CHIPS
            if tn < n:
                tn = min(tn, shard)
    tm, tn, tk = min(tm, m), min(tn, n), min(tk, k)
    assert m % tm == 0 and n % tn == 0 and k % tk == 0, (name, m, n, k)
    nk = k // tk
    all_shards = col_shards and mode == "tn" and tn == n
    keep = (mode == "nt" and m > tm) or (mode == "tn" and n > tn)
    a_spec = {"nn": pl.BlockSpec((tm, tk), lambda i, j, kk: (i, kk)),
              "nt": pl.BlockSpec((tm, tk), lambda i, j, kk: (i, kk)),
              "tn": pl.BlockSpec((tk, tm), lambda i, j, kk: (kk, i))}[mode]
    b_spec = {"nn": pl.BlockSpec((tk, tn), lambda i, j, kk: (kk, j)),
              "nt": pl.BlockSpec((tn, tk), lambda i, j, kk: (j, kk)),
              "tn": pl.BlockSpec((tk, tn), lambda i, j, kk: (kk, j))}[mode]
    o_spec = pl.BlockSpec((tm, tn), lambda i, j, kk: (i, j))
    out_shape = (m, n)
    if col_shards and mode == "nn":
        per = shard // tn
        b_spec = pl.BlockSpec((None, tk, tn), lambda i, j, kk: (j // per, kk, j % per))
    elif col_shards and mode == "nt":
        per = shard // tk
        b_spec = pl.BlockSpec((None, tn, tk), lambda i, j, kk: (kk // per, j, kk % per))
    elif col_shards:
        assert add is None and extra is None
        if all_shards:
            o_spec = pl.BlockSpec((N_CHIPS, tm, shard), lambda i, j, kk: (0, i, 0))
        else:
            per = shard // tn
            o_spec = pl.BlockSpec((None, tm, tn), lambda i, j, kk: (j // per, i, j % per))
        out_shape = (N_CHIPS, m, shard)
    if flat_out:
        assert mode == "tn" and tn == n == D_MODEL and not col_shards and add is None and extra is None
        chunks = D_MODEL // LANES
        o_spec = pl.BlockSpec((tm * chunks, LANES), lambda i, j, kk: (i, 0))
        out_shape = (m * chunks, LANES)
    operands, in_specs = [a, b], [a_spec, b_spec]
    third = add if add is not None else extra
    if third is not None:
        operands.append(third)
        in_specs.append(o_spec)

    def body(*refs):
        a_ref, b_ref = refs[0], refs[1]
        t_ref = refs[2] if third is not None else None
        o_ref = refs[3] if third is not None else refs[2]
        acc_ref = refs[-1 - keep] if nk > 1 else None
        at = a_ref[...]
        if a_fn is not None:
            at = a_fn(at)
        if keep:
            kept_ref = refs[-1]
            if mode == "nt":
                fresh, slot = pl.program_id(0) == 0, pl.program_id(1) * nk + pl.program_id(2)
            else:
                fresh, slot = pl.program_id(1) == 0, pl.program_id(2)

            @pl.when(fresh)
            def _():
                kept_ref[slot] = (b_ref[...] if mode == "nt" else at).astype(BF16).T

            pair = (at.astype(BF16), kept_ref[slot]) if mode == "nt" else (kept_ref[slot], b_ref[...].astype(BF16))
            part = lax.dot_general(*pair, _DIMS["nn"], preferred_element_type=F32)
        else:
            part = lax.dot_general(at.astype(BF16), b_ref[...].astype(BF16), _DIMS[mode], preferred_element_type=F32)

        def finish(acc):
            if epi is not None:
                acc = epi(acc, None if t_ref is None else t_ref[...])
            elif add is not None:
                acc = acc + t_ref[...].astype(F32)
            if flat_out:
                for q in range(D_MODEL // LANES):
                    o_ref[pl.ds(q, tm, stride=D_MODEL // LANES), :] = acc[:, q * LANES:(q + 1) * LANES].astype(o_ref.dtype)
                return
            if all_shards:
                for slot in range(N_CHIPS):
                    o_ref[slot] = acc[:, slot * shard:(slot + 1) * shard].astype(o_ref.dtype)
                return
            o_ref[...] = acc.astype(o_ref.dtype)

        if nk == 1:
            finish(part)
        else:
            kk = pl.program_id(2)

            @pl.when(kk == 0)
            def _():
                acc_ref[...] = part

            @pl.when(kk > 0)
            def _():
                acc_ref[...] += part

            @pl.when(kk == nk - 1)
            def _():
                finish(acc_ref[...])

    kept = {"nt": (n // tn * nk, tk, tn), "tn": (nk, tm, tk)}.get(mode)
    call = dict(name=name, grid=(m // tm, n // tn, nk), in_specs=in_specs,
                scratch_shapes=([pltpu.VMEM((tm, tn), F32)] if nk > 1 else []) + ([pltpu.VMEM(kept, BF16)] if keep else []))
    if behind is not None:
        (res,), exchanged = _call_behind(body, behind, out_specs=[o_spec], out_shape=[jax.ShapeDtypeStruct(out_shape, out_dtype)],
                                         operands=operands, **call)
        return res, exchanged
    return pl.pallas_call(body, out_specs=o_spec, out_shape=jax.ShapeDtypeStruct(out_shape, out_dtype),
                          compiler_params=_cparams(("arbitrary",) * 3 if keep else ("parallel", "parallel", "arbitrary")),
                          **call)(*operands)


ROW_TILE = 512


def _row_spec(width=D_MODEL, rows=ROW_TILE):
    return pl.BlockSpec((rows, width), lambda i: (i, 0))


def _vec_spec(rows=1, width=D_MODEL):
    return pl.BlockSpec((rows, width), lambda i: (0, 0))


def _xhat(x):
    r = lax.rsqrt(jnp.mean(x * x, axis=-1, keepdims=True) + EPS)
    return x * r, r


def _rms_bwd_rows(dh, x, g):
    xh, r = _xhat(x)
    dxh = dh * g
    dx = r * (dxh - xh * jnp.mean(dxh * xh, axis=-1, keepdims=True))
    return dx, jnp.sum(dh * xh, axis=0, keepdims=True)


def _norm_fwd(x, g, name, behind):
    t = x.shape[0]

    def body(x_ref, g_ref, h_ref):
        xh, _ = _xhat(x_ref[...])
        h_ref[...] = (xh * g_ref[...]).astype(BF16)

    return _call_behind(body, behind, name=name, grid=(t // ROW_TILE,), in_specs=[_row_spec(), _vec_spec()],
                        out_specs=[_row_spec()], out_shape=[jax.ShapeDtypeStruct((t, D_MODEL), BF16)], scratch_shapes=[],
                        operands=(x, g))


def _mm_res_norm(a, b, res, g, name, a_fn=None):
    t, k = a.shape

    def body(a_ref, b_ref, res_ref, g_ref, x_ref, h_ref):
        at = a_ref[...] if a_fn is None else a_fn(a_ref[...])
        x_new = res_ref[...] + _dot(at.astype(BF16), b_ref[...])
        x_ref[...] = x_new
        h_ref[...] = (_xhat(x_new)[0] * g_ref[...]).astype(BF16)

    return pl.pallas_call(
        body, name=name, grid=(t // ROW_TILE,),
        in_specs=[pl.BlockSpec((ROW_TILE, k), lambda i: (i, 0)), pl.BlockSpec(b.shape, lambda i: (0, 0)), _row_spec(), _vec_spec()],
        out_specs=[_row_spec(), _row_spec()],
        out_shape=[jax.ShapeDtypeStruct((t, D_MODEL), F32), jax.ShapeDtypeStruct((t, D_MODEL), BF16)],
        compiler_params=_cparams(("parallel",)),
    )(a, b, res, g)


def _mm_norm_bwd(pairs, dh_first, x, g, dres, name, behind=None):
    t = x.shape[0]
    operands, in_specs = [], []
    for a, b in pairs:
        if b.ndim == 3:
            for j in range(b.shape[0]):
                operands += [a, b]
                in_specs += [pl.BlockSpec((ROW_TILE, b.shape[2]), lambda i, j=j: (i, j)),
                             pl.BlockSpec((None, D_MODEL, b.shape[2]), lambda i, j=j: (j, 0, 0))]
        else:
            operands += [a, b]
            in_specs += [pl.BlockSpec((ROW_TILE, a.shape[1]), lambda i: (i, 0)), pl.BlockSpec(b.shape, lambda i: (0, 0))]
    n_mm = len(operands)
    operands += [x, g, dres] + ([] if dh_first is None else [dh_first])
    in_specs += [_row_spec(), _vec_spec(), _row_spec()] + ([] if dh_first is None else [_row_spec()])

    def body(*refs):
        x_ref, g_ref, dres_ref = refs[n_mm:n_mm + 3]
        dx_ref, dxb_ref, dg_ref = refs[-3:]
        dh = 0.0 if dh_first is None else refs[n_mm + 3][...]
        for k in range(0, n_mm, 2):
            dh = dh + lax.dot_general(refs[k][...].astype(BF16), refs[k + 1][...].astype(BF16), NT_DIMS,
                                      preferred_element_type=F32)
        dx, dg = _rms_bwd_rows(dh, x_ref[...], g_ref[...])
        dx = dx + dres_ref[...]
        dx_ref[...] = dx
        dxb_ref[...] = dx.astype(BF16)

        @pl.when(pl.program_id(0) == 0)
        def _():
            dg_ref[...] = jnp.zeros_like(dg_ref)

        dg_ref[...] += dg

    call = dict(name=name, grid=(t // ROW_TILE,), in_specs=in_specs, out_specs=[_row_spec(), _row_spec(), _vec_spec()],
                out_shape=[jax.ShapeDtypeStruct((t, D_MODEL), F32), jax.ShapeDtypeStruct((t, D_MODEL), BF16),
                           jax.ShapeDtypeStruct((1, D_MODEL), F32)])
    if behind is not None:
        return _call_behind(body, behind, scratch_shapes=[], operands=operands, **call)
    return pl.pallas_call(body, compiler_params=_cparams(("arbitrary",)), **call)(*operands)


def _shards_spec(w):
    return pl.BlockSpec(w.shape, lambda i: (0, 0, 0))


def _gate_fwd(gl, b_gate, o_fox, o_sb, w_fox, w_sb):
    t = o_fox.shape[0]

    def body(gla_ref, glb_ref, b_ref, ofox_ref, osb_ref, wf_ref, ws_ref, m_ref, of_ref, os_ref):
        of = jnp.concatenate([_dot(ofox_ref[...], wf_ref[j]) for j in range(N_CHIPS)], axis=1)
        os_ = jnp.concatenate([_dot(osb_ref[...], ws_ref[j]) for j in range(N_CHIPS)], axis=1)
        ga = jax.nn.sigmoid(gla_ref[...] + b_ref[0:1, :])
        gb = jax.nn.sigmoid(glb_ref[...] + b_ref[1:2, :])
        of_ref[...] = of
        os_ref[...] = os_
        m_ref[...] = (ga * of + gb * os_).astype(BF16)

    return pl.pallas_call(
        body, name="gate_fwd", grid=(t // ROW_TILE,),
        in_specs=[pl.BlockSpec((ROW_TILE, D_MODEL), lambda i: (i, 0)), pl.BlockSpec((ROW_TILE, D_MODEL), lambda i: (i, 1)),
                  _vec_spec(2), _row_spec(D_ATT), _row_spec(D_ATT), _shards_spec(w_fox), _shards_spec(w_sb)],
        out_specs=[_row_spec(), _row_spec(), _row_spec()],
        out_shape=[jax.ShapeDtypeStruct((t, D_MODEL), BF16)] + [jax.ShapeDtypeStruct((t, D_MODEL), F32)] * 2,
        compiler_params=_cparams(("parallel",)),
    )(gl, gl, b_gate, o_fox, o_sb, w_fox, w_sb)


def _gate_bwd(gl, b_gate, of, os_, dx, w_out, w_fox, w_sb):
    t = of.shape[0]
    shard = D_MODEL // N_CHIPS

    def back(d, w_ref):
        return sum(_dot(d[:, j * shard:(j + 1) * shard], w_ref[j], NT_DIMS) for j in range(N_CHIPS)).astype(BF16)

    def body(gla_ref, glb_ref, b_ref, of_ref, os_ref, dx_ref, w_ref, wf_ref, ws_ref,
             dof_ref, dos_ref, dgl_ref, db_ref, dofox_ref, dosb_ref):
        dm = _dot(dx_ref[...], w_ref[...], NT_DIMS)
        ga = jax.nn.sigmoid(gla_ref[...] + b_ref[0:1, :])
        gb = jax.nn.sigmoid(glb_ref[...] + b_ref[1:2, :])
        dof = (dm * ga).astype(BF16)
        dos = (dm * gb).astype(BF16)
        dof_ref[...] = dof
        dos_ref[...] = dos
        dofox_ref[...] = back(dof, wf_ref)
        dosb_ref[...] = back(dos, ws_ref)
        dgla = dm * of_ref[...] * ga * (1.0 - ga)
        dglb = dm * os_ref[...] * gb * (1.0 - gb)
        dgl_ref[:, 0:D_MODEL] = dgla.astype(BF16)
        dgl_ref[:, D_MODEL:2 * D_MODEL] = dglb.astype(BF16)

        @pl.when(pl.program_id(0) == 0)
        def _():
            db_ref[...] = jnp.zeros_like(db_ref)

        db_ref[0:1, :] += jnp.sum(dgla, axis=0, keepdims=True)
        db_ref[1:2, :] += jnp.sum(dglb, axis=0, keepdims=True)

    outs = pl.pallas_call(
        body, name="gate_bwd", grid=(t // ROW_TILE,),
        in_specs=[pl.BlockSpec((ROW_TILE, D_MODEL), lambda i: (i, 0)), pl.BlockSpec((ROW_TILE, D_MODEL), lambda i: (i, 1)),
                  _vec_spec(2), _row_spec(), _row_spec(), _row_spec(), pl.BlockSpec(w_out.shape, lambda i: (0, 0)),
                  _shards_spec(w_fox), _shards_spec(w_sb)],
        out_specs=[_row_spec(), _row_spec(), _row_spec(2 * D_MODEL), _vec_spec(2), _row_spec(D_ATT), _row_spec(D_ATT)],
        out_shape=[jax.ShapeDtypeStruct((t, D_MODEL), BF16)] * 2 + [jax.ShapeDtypeStruct((t, 2 * D_MODEL), BF16),
                                                                      jax.ShapeDtypeStruct((2, D_MODEL), F32)]
        + [jax.ShapeDtypeStruct((t, D_ATT), BF16)] * 2,
        compiler_params=_cparams(("arbitrary",)),
    )(gl, gl, b_gate, of, os_, dx, w_out, w_fox, w_sb)
    return outs


def _head_and_loss(x2, h3, p, w_gate, w_ple, g_final, target):
    t = x2.shape[0]

    def body(x2_ref, h3_ref, p_ref, wg_ref, wp_ref, g_ref, tgt_ref, dx3_ref, dpre_ref, dpe_ref, dg_ref, loss_ref):
        gp = jax.nn.sigmoid(_dot(h3_ref[...], wg_ref[...]))
        p_t = p_ref[...].astype(BF16)
        pe_t = jnp.concatenate([_dot(p_t, wp_ref[j]) for j in range(N_CHIPS)], axis=1)
        x3 = x2_ref[...] + gp * pe_t
        g = g_ref[...]
        xh, _ = _xhat(x3)
        err = xh * g - tgt_ref[...]
        dy = err * (1.0 / D_MODEL)
        dx3, dg = _rms_bwd_rows(dy, x3, g)
        dx3_ref[...] = dx3
        dpre_ref[...] = (dx3 * pe_t * gp * (1.0 - gp)).astype(BF16)
        dpe_ref[...] = (dx3 * gp).astype(BF16)

        @pl.when(pl.program_id(0) == 0)
        def _():
            dg_ref[...] = jnp.zeros_like(dg_ref)
            loss_ref[...] = jnp.zeros_like(loss_ref)

        dg_ref[...] += dg
        loss_ref[...] += 0.5 * jnp.sum(jnp.mean(err * err, axis=-1, keepdims=True), axis=0, keepdims=True)

    return pl.pallas_call(
        body, name="head_and_loss", grid=(t // ROW_TILE,),
        in_specs=[_row_spec(), _row_spec(), _row_spec(D_PLE), pl.BlockSpec(w_gate.shape, lambda i: (0, 0)),
                  pl.BlockSpec(w_ple.shape, lambda i: (0, 0, 0)), _vec_spec(), _row_spec()],
        out_specs=[_row_spec(), _row_spec(), _row_spec(), _vec_spec(), _vec_spec(1, LANES)],
        out_shape=[jax.ShapeDtypeStruct((t, D_MODEL), F32), jax.ShapeDtypeStruct((t, D_MODEL), BF16),
                   jax.ShapeDtypeStruct((t, D_MODEL), BF16), jax.ShapeDtypeStruct((1, D_MODEL), F32),
                   jax.ShapeDtypeStruct((1, LANES), F32)],
        compiler_params=_cparams(("arbitrary",)),
    )(x2, h3, p, w_gate, w_ple, g_final, target)


def _split3(v):
    hi = v.astype(BF16)
    r1 = v - hi.astype(F32)
    mid = r1.astype(BF16)
    lo = (r1 - mid.astype(F32)).astype(BF16)
    return hi, mid, lo


def _split2(v):
    hi = v.astype(BF16)
    return jnp.concatenate([hi, (v - hi.astype(F32)).astype(BF16)], axis=1)


def _dot(a, b, dims=_DIMS["nn"]):
    return lax.dot_general(a, b, dims, preferred_element_type=F32)


def _tri(n, rel):
    row = lax.broadcasted_iota(jnp.int32, (n, n), 0)
    col = lax.broadcasted_iota(jnp.int32, (n, n), 1)
    return rel(row, col).astype(BF16)


def _tri2(n, rel):
    t = _tri(n, rel)
    return jnp.concatenate([t, t], axis=0)


def _log_sigmoid(v):
    return -(jnp.maximum(-v, 0.0) + jnp.log(1.0 + jnp.exp(-jnp.abs(v))))


def _fox_prep(fl, b_forget, batch, seq):
    nb = seq // ATT_BLOCK

    def body(fl_ref, b_ref, cw_ref, cr_ref):
        col = lax.broadcasted_iota(jnp.int32, (ATT_BLOCK, F_PAD), 1)
        lower = _tri(ATT_BLOCK, lambda r, c: c <= r)
        upper = _tri(ATT_BLOCK, lambda r, c: r <= c)
        expand = (lax.broadcasted_iota(jnp.int32, (F_PAD, D_ATT), 1) // HEAD_DIM
                  == lax.broadcasted_iota(jnp.int32, (F_PAD, D_ATT), 0)).astype(BF16)
        carry_w = jnp.zeros((1, D_ATT), F32)
        carry_r = jnp.zeros((F_PAD, 1), F32)
        for i in range(nb):
            blk = slice(i * ATT_BLOCK, (i + 1) * ATT_BLOCK)
            logf = jnp.where(col < N_HEADS, _log_sigmoid(fl_ref[blk, :] + b_ref[...]), 0.0)
            cw = jnp.zeros((ATT_BLOCK, D_ATT), F32) + carry_w
            cr = jnp.zeros((F_PAD, ATT_BLOCK), F32) + carry_r
            for part in _split3(logf):
                cw += _dot(lower, _dot(part, expand).astype(BF16))
                cr += _dot(part, upper, TN_DIMS)
            cw_ref[blk, :] = cw
            cr_ref[:, blk] = cr[0:N_HEADS, :]
            carry_w = cw[ATT_BLOCK - 1:ATT_BLOCK, :]
            carry_r = cr[:, ATT_BLOCK - 1:ATT_BLOCK]

    return pl.pallas_call(
        body, name="fox_prep", grid=(batch,),
        in_specs=[pl.BlockSpec((seq, F_PAD), lambda b: (b, 0)), pl.BlockSpec((1, F_PAD), lambda b: (0, 0))],
        out_specs=[pl.BlockSpec((seq, D_ATT), lambda b: (b, 0)), pl.BlockSpec((N_HEADS, seq), lambda b: (b, 0))],
        out_shape=[jax.ShapeDtypeStruct((batch * seq, D_ATT), F32), jax.ShapeDtypeStruct((batch * N_HEADS, seq), F32)],
        compiler_params=_cparams(("parallel",)),
    )(fl, b_forget)


def _fox_post(dcs_wide, drs_wide, fl, b_forget, batch, seq):
    nb = seq // ATT_BLOCK

    def body(dcs_ref, drs_ref, fl_ref, b_ref, dfl_ref, db_ref):
        pick = (lax.broadcasted_iota(jnp.int32, (D_ATT, F_PAD), 0)
                == lax.broadcasted_iota(jnp.int32, (D_ATT, F_PAD), 1) * HEAD_DIM).astype(BF16)
        upper = _tri(ATT_BLOCK, lambda r, c: r <= c)
        col = lax.broadcasted_iota(jnp.int32, (ATT_BLOCK, F_PAD), 1)

        @pl.when(pl.program_id(0) == 0)
        def _():
            db_ref[...] = jnp.zeros_like(db_ref)

        carry = jnp.zeros((1, F_PAD), F32)
        for i in reversed(range(nb)):
            blk = slice(i * ATT_BLOCK, (i + 1) * ATT_BLOCK)
            narrow = jnp.zeros((ATT_BLOCK, F_PAD), F32)
            for part in _split3(drs_ref[blk, :] - dcs_ref[blk, :]):
                narrow += _dot(part, pick)
            after = jnp.zeros((ATT_BLOCK, F_PAD), F32) + carry
            for part in _split3(narrow):
                after += _dot(upper, part)
            carry = after[0:1, :]
            pre = fl_ref[blk, :] + b_ref[...]
            dfl = jnp.where(col < N_HEADS, after * jax.nn.sigmoid(-pre), 0.0)
            dfl_ref[blk, :] = dfl.astype(BF16)
            db_ref[...] += jnp.sum(dfl, axis=0, keepdims=True)

    return pl.pallas_call(
        body, name="fox_post", grid=(batch,),
        in_specs=[pl.BlockSpec((seq, D_ATT), lambda b: (b, 0)), pl.BlockSpec((seq, D_ATT), lambda b: (b, 0)),
                  pl.BlockSpec((seq, F_PAD), lambda b: (b, 0)), pl.BlockSpec((1, F_PAD), lambda b: (0, 0))],
        out_specs=[pl.BlockSpec((seq, F_PAD), lambda b: (b, 0)), pl.BlockSpec((1, F_PAD), lambda b: (0, 0))],
        out_shape=[jax.ShapeDtypeStruct((batch * seq, F_PAD), BF16), jax.ShapeDtypeStruct((1, F_PAD), F32)],
        compiler_params=_cparams(("arbitrary",)),
    )(dcs_wide, drs_wide, fl, b_forget)


N_PAIRS = N_HEADS // 2


def _att_specs(seq, col0, tq):
    nq = seq // tq
    q = pl.BlockSpec((tq, LANES), lambda b, hp, qi: (b * nq + qi, col0 + hp))
    k = pl.BlockSpec((seq, LANES), lambda b, hp, qi: (b, col0 + N_PAIRS + hp))
    v = pl.BlockSpec((seq, LANES), lambda b, hp, qi: (b, col0 + 2 * N_PAIRS + hp))
    return q, k, v


def _qblock_spec(seq, tq):
    nq = seq // tq
    return pl.BlockSpec((tq, LANES), lambda b, hp, qi: (b * nq + qi, hp))


def _kv_out_spec(seq):
    return pl.BlockSpec((seq, LANES), lambda b, hp, qi: (b, hp))


def _head_masks():
    lane = lax.broadcasted_iota(jnp.int32, (1, LANES), 1)
    return [(lane >= HEAD_DIM * j) & (lane < HEAD_DIM * (j + 1)) for j in range(2)]


def _stack_heads(t, masks):
    zero = jnp.zeros_like(t)
    return jnp.concatenate([jnp.where(masks[0], t, zero), jnp.where(masks[1], t, zero)], axis=0)


def _stack_cols(t):
    return jnp.concatenate([t[:, 0:1], t[:, HEAD_DIM:HEAD_DIM + 1]], axis=0)


def _unstack(t2, masks):
    tq = t2.shape[0] // 2
    return jnp.where(masks[0], t2[:tq], t2[tq:])


def _stacked_ids(tq, tk):
    row = lax.broadcasted_iota(jnp.int32, (2 * tq, tk), 0)
    col = lax.broadcasted_iota(jnp.int32, (2 * tq, tk), 1)
    first = lax.broadcasted_iota(jnp.int32, (2 * tq, 1), 0) < tq
    return col - jnp.where(row < tq, row, row - tq), first


def _sweep(qi, tq, tk, step, init):
    per = tq // tk
    carry = lax.fori_loop(0, per * qi, lambda kb, c: step(kb, c, None, 0), init)
    for j in range(per):
        carry = step(per * qi + j, carry, -j * tk, j * tk)
    return carry


def _below(t2, top):
    tq = t2.shape[0] // 2
    return t2 if top == 0 else jnp.concatenate([t2[top:tq], t2[tq + top:]], axis=0)


def _put_below(old, new, top):
    if top == 0:
        return new
    tq = old.shape[0] // 2
    return jnp.concatenate([old[:top], new[:tq - top], old[tq:tq + top], new[tq - top:]], axis=0)


def _att_fwd(qkv, c_wide, c_row, batch, seq, behind):
    tq, tkf = FOX_TILES
    tqs, tks = SB_TILES
    assert tq == tqs and tkf == 2 * tks
    nq = seq // tq

    def body(qa_ref, ka_ref, va_ref, cw_ref, cr_ref, qb_ref, kb_ref, vb_ref, of_ref, lse_ref, os_ref, rt_ref):
        hp, qi = pl.program_id(1), pl.program_id(2)
        masks = _head_masks()
        ahead_f, first = _stacked_ids(tq, tkf)
        ahead_s, _ = _stacked_ids(tq, tks)
        later = _tri2(tks, lambda r, c: r > c)
        q2f = _stack_heads(qa_ref[...], masks) * SCALE
        q2s = _stack_heads(qb_ref[...], masks) * SCALE
        ct = _stack_cols(cw_ref[...])

        def fox(kb, carry, lead):
            m, l, acc = carry
            k0 = pl.multiple_of(kb * tkf, tkf)
            cs = jnp.where(first, cr_ref[pl.ds(2 * hp, 1), pl.ds(k0, tkf)], cr_ref[pl.ds(2 * hp + 1, 1), pl.ds(k0, tkf)])
            s = _dot(q2f, ka_ref[pl.ds(k0, tkf), :], NT_DIMS) + ct - cs
            if lead is not None:
                s = jnp.where(ahead_f <= lead, s, NEG)
            m_new = jnp.maximum(m, jnp.max(s, axis=1, keepdims=True))
            p = jnp.exp(s - m_new)
            alpha = jnp.exp(m - m_new)
            l = alpha * l + jnp.sum(p, axis=1, keepdims=True)
            acc = alpha * acc + _dot(p.astype(BF16), va_ref[pl.ds(k0, tkf), :])
            return m_new, l, acc

        def sb(kb, carry, lead):
            run, acc = carry
            k0 = pl.multiple_of(kb * tks, tks)
            ls, lsn = _sb_logits(q2s, kb_ref[pl.ds(k0, tks), :])
            if lead is not None:
                lsn = jnp.where(ahead_s < lead, lsn, 0.0)
            w = jnp.exp(ls + _dot(_split2(lsn), later) + run)
            if lead is not None:
                w = jnp.where(ahead_s < lead, w, 0.0)
            return run + jnp.sum(lsn, axis=1, keepdims=True), acc + _dot(w.astype(BF16), vb_ref[pl.ds(k0, tks), :])

        fox_c = (jnp.full((2 * tq, 1), NEG, F32), jnp.zeros((2 * tq, 1), F32), jnp.zeros((2 * tq, LANES), F32))
        sb_c = (jnp.zeros((2 * tq, 1), F32), jnp.zeros((2 * tq, LANES), F32))
        sb_c = sb(2 * qi, sb(2 * qi + 1, sb_c, -tks), 0)

        def both(i, carries):
            fox_c, sb_c = carries
            return fox(i, fox_c, None), sb(2 * qi - 2 - 2 * i, sb(2 * qi - 1 - 2 * i, sb_c, None), None)

        fox_c, (run, acc_s) = lax.fori_loop(0, qi, both, (fox_c, sb_c))
        m, l, acc = fox(qi, fox_c, 0)
        of_ref[...] = _unstack(acc / l, masks).astype(BF16)
        lse_ref[...] = _unstack(m + jnp.log(l), masks)
        os_ref[...] = _unstack(acc_s, masks).astype(BF16)
        rt_ref[...] = _unstack(run, masks)

    qa, ka, va = _att_specs(seq, 0, tq)
    qb_, kb_, vb_ = _att_specs(seq, 3 * N_PAIRS, tq)
    qb = _qblock_spec(seq, tq)
    half, wide = jax.ShapeDtypeStruct((batch * seq, D_ATT), BF16), jax.ShapeDtypeStruct((batch * seq, D_ATT), F32)
    return _call_behind(
        body, behind, name="att_fwd", grid=(batch, N_PAIRS, nq),
        in_specs=[qa, ka, va, qb, pl.BlockSpec((N_HEADS, seq), lambda b, hp, qi: (b, 0)), qb_, kb_, vb_],
        out_specs=[qb, qb, qb, qb], out_shape=[half, wide, half, wide], scratch_shapes=[],
        operands=(qkv, qkv, qkv, c_wide, c_row, qkv, qkv, qkv))


def _fox_bwd(qkv, c_wide, c_row, o, do, lse_wide, batch, seq, behind):
    tq, tk = FOX_TILES
    nq = seq // tq

    def body(q_ref, k_ref, v_ref, cw_ref, cr_ref, o_ref, do_ref, lse_ref,
             dq_ref, dk_ref, dv_ref, dcs_ref, drs_ref, dkc_acc, dv_acc):
        hp, qi = pl.program_id(1), pl.program_id(2)

        @pl.when(qi == 0)
        def _():
            dkc_acc[...] = jnp.zeros_like(dkc_acc)
            dv_acc[...] = jnp.zeros_like(dv_acc)

        masks = _head_masks()
        ahead, first = _stacked_ids(tq, tk)
        q_t, do_t = q_ref[...], do_ref[...]
        q2 = _stack_heads(q_t, masks) * SCALE
        do2 = _stack_heads(do_t, masks)
        q_and_ones = jnp.concatenate([q2, _stack_heads(jnp.ones_like(q_t), masks)], axis=1)
        ct = _stack_cols(cw_ref[...])
        lse = _stack_cols(lse_ref[...])
        prod = do_t.astype(F32) * o_ref[...].astype(F32)
        delta = jnp.concatenate([jnp.sum(jnp.where(mk, prod, 0.0), axis=1, keepdims=True) for mk in masks], axis=0)

        def step(kb, carry, lead, top):
            dq_acc, rs = carry
            k0 = pl.multiple_of(kb * tk, tk)
            kblk = k_ref[pl.ds(k0, tk), :]
            cs = jnp.where(first, cr_ref[pl.ds(2 * hp, 1), pl.ds(k0, tk)], cr_ref[pl.ds(2 * hp + 1, 1), pl.ds(k0, tk)])
            p = jnp.exp(_dot(q2, kblk, NT_DIMS) + ct - cs - lse)
            if lead is not None:
                p = jnp.where(ahead <= lead, p, 0.0)
            dp = _dot(do2, v_ref[pl.ds(k0, tk), :], NT_DIMS)
            ds = (p * (dp - delta)).astype(BF16)
            dkc_acc[pl.ds(k0, tk), :] += _dot(ds, q_and_ones, TN_DIMS)
            dv_acc[pl.ds(k0, tk), :] += _dot(p.astype(BF16), do2, TN_DIMS)
            return dq_acc + _dot(ds, kblk), rs + jnp.sum(ds.astype(F32), axis=1, keepdims=True)

        init = (jnp.zeros((2 * tq, LANES), F32), jnp.zeros((2 * tq, 1), F32))
        dq_acc, rs = _sweep(qi, tq, tk, step, init)
        dq_ref[...] = (_unstack(dq_acc, masks) * SCALE).astype(BF16)
        drs_ref[...] = _unstack(rs, masks)

        @pl.when(qi == nq - 1)
        def _():
            dk_ref[...] = dkc_acc[:, 0:LANES].astype(BF16)
            dcs_ref[...] = dkc_acc[:, LANES:2 * LANES]
            dv_ref[...] = dv_acc[...].astype(BF16)

    q_spec, k_spec, v_spec = _att_specs(seq, 0, tq)
    qb = _qblock_spec(seq, tq)
    return _call_behind(
        body, behind, name="fox_bwd", grid=(batch, N_PAIRS, nq),
        in_specs=[q_spec, k_spec, v_spec, qb, pl.BlockSpec((N_HEADS, seq), lambda b, hp, qi: (b, 0)), qb, qb, qb],
        out_specs=[qb, _kv_out_spec(seq), _kv_out_spec(seq), _kv_out_spec(seq), qb],
        out_shape=[jax.ShapeDtypeStruct((batch * seq, D_ATT), BF16)] * 3 + [jax.ShapeDtypeStruct((batch * seq, D_ATT), F32)] * 2,
        scratch_shapes=[pltpu.VMEM((seq, 2 * LANES), F32), pltpu.VMEM((seq, LANES), F32)],
        operands=(qkv, qkv, qkv, c_wide, c_row, o, do, lse_wide))


def _sb_logits(q2, kblk):
    z = _dot(q2, kblk, NT_DIMS)
    lsn = jnp.minimum(-z, 0.0) - jnp.log(1.0 + jnp.exp(-jnp.abs(z)))
    return lsn + z, lsn


def _sb_bwd(qkv, do, rt_wide, batch, seq, behind):
    tq, tk = SB_TILES
    nq = seq // tq

    def body(q_ref, k_ref, v_ref, do_ref, rt_ref, dq_ref, dk_ref, dv_ref, dk_acc, dv_acc):
        qi = pl.program_id(2)

        @pl.when(qi == 0)
        def _():
            dk_acc[...] = jnp.zeros_like(dk_acc)
            dv_acc[...] = jnp.zeros_like(dv_acc)

        masks = _head_masks()
        ahead, _ = _stacked_ids(tq, tk)
        later = _tri2(tk, lambda r, c: r > c)
        earlier = _tri(tk, lambda r, c: r < c)
        q2 = _stack_heads(q_ref[...], masks) * SCALE
        do2 = _stack_heads(do_ref[...], masks)
        total = _stack_cols(rt_ref[...])

        def step(kb, carry, lead, top):
            pref, epre, dq_acc = (_below(t, top) for t in carry)
            q_s, do_s = _below(q2, top), _below(do2, top)
            seen = None if lead is None else _below(ahead, top) < lead
            k0 = pl.multiple_of(kb * tk, tk)
            kblk = k_ref[pl.ds(k0, tk), :]
            ls, lsn_all = _sb_logits(q_s, kblk)
            lsn = lsn_all if lead is None else jnp.where(seen, lsn_all, 0.0)
            rs = jnp.sum(lsn, axis=1, keepdims=True)
            w = jnp.exp(ls + _dot(_split2(lsn), later) + (_below(total, top) - pref - rs))
            if lead is not None:
                w = jnp.where(seen, w, 0.0)
            e = w * _dot(do_s, v_ref[pl.ds(k0, tk), :], NT_DIMS)
            before = _dot(e.astype(BF16), earlier) + epre
            dz = e * jnp.exp(lsn_all) - jnp.exp(ls) * before
            if lead is not None:
                dz = jnp.where(seen, dz, 0.0)
            dz = dz.astype(BF16)
            dk_acc[pl.ds(k0, tk), :] += _dot(dz, q_s, TN_DIMS)
            dv_acc[pl.ds(k0, tk), :] += _dot(w.astype(BF16), do_s, TN_DIMS)
            new = (pref + rs, epre + jnp.sum(e, axis=1, keepdims=True), dq_acc + _dot(dz, kblk))
            return tuple(_put_below(o, n, top) for o, n in zip(carry, new))

        init = (jnp.zeros((2 * tq, 1), F32), jnp.zeros((2 * tq, 1), F32), jnp.zeros((2 * tq, LANES), F32))
        dq_acc = _sweep(qi, tq, tk, step, init)[2]
        dq_ref[...] = (_unstack(dq_acc, masks) * SCALE).astype(BF16)

        @pl.when(qi == nq - 1)
        def _():
            dk_ref[...] = dk_acc[...].astype(BF16)
            dv_ref[...] = dv_acc[...].astype(BF16)

    q_spec, k_spec, v_spec = _att_specs(seq, 3 * N_PAIRS, tq)
    qb = _qblock_spec(seq, tq)
    return _call_behind(
        body, behind, name="sb_bwd", grid=(batch, N_PAIRS, nq), in_specs=[q_spec, k_spec, v_spec, qb, qb],
        out_specs=[qb, _kv_out_spec(seq), _kv_out_spec(seq)], out_shape=[jax.ShapeDtypeStruct((batch * seq, D_ATT), BF16)] * 3,
        scratch_shapes=[pltpu.VMEM((seq, LANES), F32), pltpu.VMEM((seq, LANES), F32)], operands=(qkv, qkv, qkv, do, rt_wide))


def _local_step(x, p, target, first, rest, vec, place):
    batch, seq, _ = x.shape
    t = batch * seq
    x = x.reshape(t, D_MODEL)
    target = target.reshape(t, D_MODEL)
    p = p.reshape(t, D_PLE)
    big = dict(tm=1024, tn=1024, tk=1024)

    (h1,), (w_in_slots,) = _norm_fwd(x, vec["g_mix"], "norm_mix", first)
    w = _first_weights(w_in_slots)
    qkv = _mm(h1, w["qkv"], mode="nn", name="proj_qkv", out_dtype=BF16, **big)
    gl = _mm(h1, w["gate"], mode="nn", name="proj_gate", **big)
    fl = _mm(h1, w["forget"], mode="nn", name="proj_forget", **big)
    c_wide, c_row = _fox_prep(fl, vec["b_forget"], batch, seq)
    (o_fox, lse_wide, o_sb, rt_wide), gathered = _att_fwd(qkv, c_wide, c_row, batch, seq, rest)
    w = dict(w, **_rest_weights(dict(zip(EARLY + ("b_gate",), gathered))))
    merged, of, os_ = _gate_fwd(gl, w["b_gate"], o_fox, o_sb, w["branch_fox"], w["branch_sb"])
    x1, h2 = _mm_res_norm(merged, w["out"], x, vec["g_mlp"], "proj_out_norm")
    ar = _mm(h2, w["up"], mode="nn", name="mlp_up", out_dtype=BF16, epi=lambda acc, _: jnp.maximum(acc, 0.0),
             col_shards=True, **big)
    x2, h3 = _mm_res_norm(ar, w["down"], x1, vec["g_ple"], "mlp_down_norm", a_fn=_relu2)

    dx3, dpre, dpe, dg_final, loss = _head_and_loss(x2, h3, p, w["ple_gate"], w["ple"], vec["g_final"], target)
    gw = {}
    gw["ple"] = _mm(p, dpe, mode="tn", name="d_w_ple", col_shards=True, **big)
    gw["ple_gate"] = _mm(h3, dpre, mode="tn", name="d_w_ple_gate", **big)
    dx2, dx2b, dg_ple = _mm_norm_bwd([(dpre, w["ple_gate"])], None, x2, vec["g_ple"], dx3, "d_h_ple_norm_bwd")
    gw["down"] = _mm(ar, dx2b, mode="tn", name="d_w_down", a_fn=_relu2, **big)
    da = _mm(dx2b, w["down"], mode="nt", name="d_act", out_dtype=BF16,
             epi=lambda acc, r: acc * (2.0 * r.astype(F32)), extra=ar, **big)
    gw["up"] = _mm(h2, da, mode="tn", name="d_w_up", col_shards=True, **big)
    dx1, dx1b, dg_mlp = _mm_norm_bwd([(da, w["up"])], None, x1, vec["g_mlp"], dx2, "d_h_mlp_norm_bwd")
    gw["out"] = _mm(merged, dx1b, mode="tn", name="d_w_out", **big)
    dof, dos, dgl, gw["b_gate"], do_fox, do_sb = _gate_bwd(gl, w["b_gate"], of, os_, dx1b, w["out"], w["branch_fox"],
                                                                  w["branch_sb"])
    gw["branch_fox"] = _mm(o_fox, dof, mode="tn", name="d_w_branch_fox", col_shards=True, **big)
    gw["branch_sb"] = _mm(o_sb, dos, mode="tn", name="d_w_branch_sb", col_shards=True, **big)
    early = _early_slots(gw)
    early = [early[n] for n in EARLY]
    (dq_a, dk_a, dv_a, dcs_wide, drs_wide), received = _fox_bwd(qkv, c_wide, c_row, o_fox, do_fox, lse_wide, batch, seq,
                                                                _swap_halves(early))
    sums = _sum_sibling(place, early, received, "sum_sibling_early")
    (dq_b, dk_b, dv_b), others = _sb_bwd(qkv, do_sb, rt_wide, batch, seq, _exchange_chips(sums))
    mine = [None] * len(EARLY)
    for group, tag in ((BIG, "big"), (SMALL, "small")):
        for t, res in zip(group, _sum_chips(place, *[[a[t] for t in group] for a in (early, received, others)], "sum_chips_" + tag)):
            mine[t] = res
    dfl, db_forget = _fox_post(dcs_wide, drs_wide, fl, vec["b_forget"], batch, seq)
    dqkv = jnp.concatenate([dq_a, dk_a, dv_a, dq_b, dk_b, dv_b], axis=1)
    gw["qkv"], theirs = _mm(dqkv, h1, mode="tn", name="d_w_qkv", behind=_share_halves(mine), flat_out=True, **big)
    reduced = dict(zip(EARLY, zip(mine, theirs)))
    gw["gate"] = _mm(dgl, h1, mode="tn", name="d_w_gate", flat_out=True, **big)
    gw["forget"] = _mm(dfl, h1, mode="tn", name="d_w_forget", flat_out=True, **big)
    late = [_w_in_slots(gw)]
    dh1, received = _mm(dqkv, w["qkv"], mode="nt", name="d_h_qkv", behind=_swap_halves(late), **big)
    sums = _sum_sibling(place, late, received, "sum_sibling_w_in")
    (grad_x, _, dg_mix), others = _mm_norm_bwd([(dgl, w["gate"]), (dfl, w["forget"])], dh1, x, vec["g_mix"], dx1,
                                               "d_h_gate_norm_bwd", behind=_exchange_chips(sums))
    mine = _sum_chips(place, late, received, others, "sum_chips_w_in")
    reduced["w_in"] = (mine[0], _run_exchange(_share_halves(mine), "reduce_share_w_in")[0])
    gvec = {"g_mix": dg_mix, "b_forget": db_forget[:, 0:N_HEADS], "g_mlp": dg_mlp, "g_ple": dg_ple,
            "g_final": dg_final, "b_gate": gw["b_gate"]}
    return loss, grad_x.reshape(batch, seq, D_MODEL), reduced, gvec


ANY = pl.BlockSpec(memory_space=pl.ANY)
SHARDED = ("w_in", "w_branch_fox", "w_branch_sb", "w_out", "w_up", "w_down", "w_ple_gate", "w_ple")
ROW_ALIGN = 16
F32_ROWS = 8


def _place():
    return lax.axis_index("x"), lax.axis_index("y"), lax.axis_index("c")


def _other_chips(x, y):
    return [(1 - x, y), (x, 1 - y), (1 - x, 1 - y)]


def _half(ref, h):
    r = ref.shape[0] // 2
    assert r % ROW_ALIGN == 0
    return ref.at[pl.ds(pl.multiple_of(h * r, ROW_ALIGN), r)]


def _remote(src, dst, sems, idx, to):
    send_sems, recv_sems = sems
    return pltpu.make_async_remote_copy(src_ref=src, dst_ref=dst, send_sem=send_sems.at[idx], recv_sem=recv_sems.at[idx],
                                        device_id=to, device_id_type=MESH)


class _Exchange:
    def __init__(self, operands, out_shapes, sem_shape, start, finish):
        self.operands, self.out_shapes, self.sem_shape, self.start, self.finish = operands, out_shapes, sem_shape, start, finish

    def scratch(self):
        return [pltpu.SemaphoreType.DMA(self.sem_shape), pltpu.SemaphoreType.DMA(self.sem_shape)]


def _run_exchange(ex, name):
    n = len(ex.operands)

    def body(*refs):
        ex.start(refs[:n], refs[n:2 * n], refs[2 * n:])
        ex.finish(refs[:n], refs[n:2 * n], refs[2 * n:])

    return pl.pallas_call(body, name=name, in_specs=[ANY] * n, out_specs=[ANY] * n, out_shape=ex.out_shapes,
                          scratch_shapes=ex.scratch())(*ex.operands)


def _call_behind(body, ex, *, name, grid, in_specs, out_specs, out_shape, scratch_shapes, operands):
    n_in, n_out, nx = len(in_specs), len(out_specs), len(ex.operands)

    def wrapped(*refs):
        ins, x_in = refs[:n_in], refs[n_in:n_in + nx]
        outs, x_out = refs[n_in + nx:n_in + nx + n_out], refs[n_in + nx + n_out:n_in + 2 * nx + n_out]
        scratch, sems = refs[n_in + 2 * nx + n_out:-2], refs[-2:]
        first, last = None, None
        for d, steps in enumerate(grid):
            at_start, at_end = pl.program_id(d) == 0, pl.program_id(d) == steps - 1
            first = at_start if first is None else first & at_start
            last = at_end if last is None else last & at_end

        @pl.when(first)
        def _():
            ex.start(x_in, x_out, sems)

        body(*ins, *outs, *scratch)

        @pl.when(last)
        def _():
            ex.finish(x_in, x_out, sems)

    res = pl.pallas_call(
        wrapped, name=name, grid=grid, in_specs=list(in_specs) + [ANY] * nx, out_specs=list(out_specs) + [ANY] * nx,
        out_shape=list(out_shape) + list(ex.out_shapes), scratch_shapes=list(scratch_shapes) + ex.scratch(),
        compiler_params=_cparams(("arbitrary",) * len(grid)),
    )(*operands, *ex.operands)
    return res[:n_out], res[n_out:]


def _gather_weights(shards):
    n = len(shards)

    def first_copies(src, out, sems):
        x, y, c = _place()
        me = 2 * x + y
        copies = [_remote(_half(src[t], c), _half(out[t].at[me], c), sems, (t, k), (px, py, c))
                  for t in range(n) for k, (px, py) in enumerate(_other_chips(x, y))]
        return copies + [_remote(src[t], out[t].at[me], sems, (t, 3), (x, y, 1 - c)) for t in range(n)]

    def start(src, out, sems):
        for cp in first_copies(src, out, sems):
            cp.start()

    def finish(src, out, sems):
        x, y, c = _place()
        me = 2 * x + y
        sibling = (x, y, 1 - c)
        chips = _other_chips(x, y)
        passes = []
        for t in range(n):
            for k, (px, py) in enumerate(chips):
                landed = _half(out[t].at[2 * px + py], c)
                _remote(landed, landed, sems, (t, k), (px, py, c)).wait_recv()
                passes.append(_remote(landed, landed, sems, (t, 4 + k), sibling))
                passes[-1].start()
        for t in range(n):
            _remote(src[t], out[t].at[me], sems, (t, 3), sibling).wait_recv()
            for k, (px, py) in enumerate(chips):
                passed = _half(out[t].at[2 * px + py], 1 - c)
                _remote(passed, passed, sems, (t, 4 + k), sibling).wait_recv()
        for cp in first_copies(src, out, sems) + passes:
            cp.wait_send()

    return _Exchange(shards, [jax.ShapeDtypeStruct((N_CHIPS,) + s.shape, s.dtype) for s in shards], (n, 7), start, finish)


def _simple_exchange(operands, out_shapes, copies):
    def start(src, out, sems):
        for cp in copies(src, out, sems):
            cp.start()

    def finish(src, out, sems):
        for cp in copies(src, out, sems):
            cp.wait_recv()
        for cp in copies(src, out, sems):
            cp.wait_send()

    return _Exchange(operands, out_shapes, (len(operands),), start, finish)


def _swap_halves(slots):
    def copies(src, out, sems):
        x, y, c = _place()
        res = []
        for t in range(len(slots)):
            r = src[t].shape[1] // 2
            rows = pl.ds(pl.multiple_of((1 - c) * r, F32_ROWS), r)
            res.append(_remote(src[t].at[:, rows], out[t], sems, t, (x, y, 1 - c)))
        return res

    return _simple_exchange(slots, [jax.ShapeDtypeStruct((N_CHIPS, s.shape[1] // 2, s.shape[2]), s.dtype) for s in slots], copies)


def _exchange_chips(sums):
    n = len(sums)

    def copies(src, out, sems):
        x, y, c = _place()
        return [_remote(src[t].at[2 * px + py], out[t].at[k], sems, (t, k), (px, py, c))
                for t in range(n) for k, (px, py) in enumerate(_other_chips(x, y))]

    def start(src, out, sems):
        for cp in copies(src, out, sems):
            cp.start()

    def finish(src, out, sems):
        for cp in copies(src, out, sems):
            cp.wait_recv()
        for cp in copies(src, out, sems):
            cp.wait_send()

    return _Exchange(sums, [jax.ShapeDtypeStruct((3,) + s.shape[1:], s.dtype) for s in sums], (n, 3), start, finish)


def _share_halves(mine):
    def copies(src, out, sems):
        x, y, c = _place()
        return [_remote(src[t], out[t], sems, t, (x, y, 1 - c)) for t in range(len(mine))]

    return _simple_exchange(mine, [jax.ShapeDtypeStruct(s.shape, s.dtype) for s in mine], copies)


def _walk(name, place, parts):
    starts = [sum(p[0] for p in parts[:t]) for t in range(len(parts))]
    held = lambda index, start, steps: (lambda s, pr: index(jnp.clip(s - start, 0, steps - 1), pr))
    in_specs, out_specs, out_shapes, operands = [], [], [], []
    for (steps, ins, outs, shapes, ops, _), start in zip(parts, starts):
        in_specs += [pl.BlockSpec(blk, held(index, start, steps)) for blk, index in ins]
        out_specs += [pl.BlockSpec(blk, held(index, start, steps)) for blk, index in outs]
        out_shapes += list(shapes)
        operands += list(ops)

    def body(place_ref, *refs):
        s = pl.program_id(0)
        i, o = 0, len(in_specs)
        for (steps, ins, outs, _, _, fn), start in zip(parts, starts):
            mine_in, mine_out = refs[i:i + len(ins)], refs[o:o + len(outs)]
            i, o = i + len(ins), o + len(outs)

            @pl.when((s >= start) & (s < start + steps))
            def _(mine_in=mine_in, mine_out=mine_out, start=start, fn=fn):
                fn(s - start, mine_in, mine_out)

    res = pl.pallas_call(
        body, name=name, out_shape=out_shapes,
        grid_spec=pltpu.PrefetchScalarGridSpec(num_scalar_prefetch=1, grid=(sum(p[0] for p in parts),), in_specs=in_specs,
                                               out_specs=out_specs),
        compiler_params=_cparams(("arbitrary",)),
    )(place, *operands)
    counts = [len(p[2]) for p in parts]
    return [res[sum(counts[:t]):sum(counts[:t + 1])] for t in range(len(parts))]


def _sum_sibling(place, slots, received, name):
    def part(slot, got):
        n, rows, cols = got.shape
        block = (None, rows, cols)

        def fn(j, ins, outs):
            outs[0][...] = (ins[0][...] + ins[1][...]).astype(BF16)

        return (n, [(block, lambda j, pr: (j, pr[1], 0)), (block, lambda j, pr: (j, 0, 0))], [(block, lambda j, pr: (j, 0, 0))],
                [jax.ShapeDtypeStruct(got.shape, BF16)], [slot, got], fn)

    return [r[0] for r in _walk(name, place, [part(s, g) for s, g in zip(slots, received)])]


def _sum_chips(place, slots, received, others, name):
    def part(slot, got, other):
        _, rows, cols = got.shape
        block = (None, rows, cols)

        def fn(_, ins, outs):
            own = ins[0][...] + ins[1][...]
            outs[0][...] = ((own + ins[2][0].astype(F32)) + ins[2][1].astype(F32)) + ins[2][2].astype(F32)

        return (1, [(block, lambda _, pr: (pr[0], pr[1], 0)), (block, lambda _, pr: (pr[0], 0, 0)),
                    ((3, rows, cols), lambda _, pr: (0, 0, 0))], [((rows, cols), lambda _, pr: (0, 0))],
                [jax.ShapeDtypeStruct((rows, cols), F32)], [slot, got, other], fn)

    return [r[0] for r in _walk(name, place, [part(*t) for t in zip(slots, received, others)])]


N_DEVICES = 8


def _sum_devices(block, name):
    def body(v_ref, o_ref, land_ref, send_sems, recv_sems):
        x, y, c = _place()
        me = 4 * x + 2 * y + c
        copies = []
        for mask in range(1, N_DEVICES):
            peer = (x ^ (mask >> 2), y ^ ((mask >> 1) & 1), c ^ (mask & 1))
            copies.append(pltpu.make_async_remote_copy(src_ref=v_ref, dst_ref=land_ref.at[me], send_sem=send_sems.at[mask - 1],
                                                       recv_sem=recv_sems.at[mask - 1], device_id=peer, device_id_type=MESH))
        for cp in copies:
            cp.start()
        land_ref[me] = v_ref[...]
        for cp in copies:
            cp.wait_recv()
        total = land_ref[0]
        for d in range(1, N_DEVICES):
            total = total + land_ref[d]
        o_ref[...] = total
        for cp in copies:
            cp.wait_send()

    vmem = pl.BlockSpec(memory_space=pltpu.VMEM)
    return pl.pallas_call(
        body, name=name, in_specs=[vmem], out_specs=vmem, out_shape=jax.ShapeDtypeStruct(block.shape, F32),
        scratch_shapes=[pltpu.VMEM((N_DEVICES,) + block.shape, F32), pltpu.SemaphoreType.DMA((N_DEVICES - 1,)),
                        pltpu.SemaphoreType.DMA((N_DEVICES - 1,))],
    )(block)


def _vec_block(g_mix, g_mlp, g_ple, g_final, b_forget, b_gate_rows, last=None):
    pad = lambda a: jnp.concatenate([a, jnp.zeros((a.shape[0], D_MODEL - a.shape[1]), F32)], axis=1)
    last = jnp.zeros((1, 0), F32) if last is None else last
    return jnp.concatenate([g_mix, g_mlp, g_ple, g_final.reshape(1, D_MODEL), pad(b_forget), pad(b_gate_rows), pad(last)],
                           axis=0)


def _adam_math(w, g, m, v):
    m_new = ADAM_B1 * m + (1.0 - ADAM_B1) * g
    v_new = ADAM_B2 * v + (1.0 - ADAM_B2) * (g * g)
    m_hat = m_new / (1.0 - ADAM_B1 ** ADAM_STEP)
    v_hat = v_new / (1.0 - ADAM_B2 ** ADAM_STEP)
    return -ADAM_LR * (m_hat / (jnp.sqrt(v_hat) + ADAM_EPS) + ADAM_WD * w), m_new, v_new


def _adamw_halves(place, weights, name):
    def part(w, m, v, g_mine, g_theirs):
        rows, cols = g_mine.shape
        whole = ((rows, cols), lambda s, pr: (pr[1] + s - 2 * pr[1] * s, 0))
        half = ((rows, cols), lambda s, pr: (0, 0))

        def fn(s, ins, outs):
            g = jnp.where(s == 0, ins[3][...], ins[4][...])
            outs[0][...] = g
            outs[1][...], outs[2][...], outs[3][...] = _adam_math(ins[0][...], g, ins[1][...], ins[2][...])

        return (2, [whole] * 3 + [half] * 2, [whole] * 4, [jax.ShapeDtypeStruct(w.shape, F32)] * 4, [w, m, v, g_mine, g_theirs], fn)

    return _walk(name, place, [part(*t) for t in weights])


def _adamw_vec(w, g, m, v):
    def body(w_ref, g_ref, m_ref, v_ref, d_ref, nm_ref, nv_ref):
        d_ref[...], nm_ref[...], nv_ref[...] = _adam_math(w_ref[...], g_ref[...], m_ref[...], v_ref[...])

    return pl.pallas_call(body, name="adamw_vectors", out_shape=[jax.ShapeDtypeStruct(w.shape, F32)] * 3)(w, g, m, v)


WEIGHT_NAMES = ("g_mix", "w_in", "b_forget", "b_gate", "w_branch_fox", "w_branch_sb", "w_out", "g_mlp", "w_up", "w_down",
                "g_ple", "w_ple_gate", "w_ple", "g_final")
W_IN_SHARD = D_IN // N_CHIPS
Q_END, F_END, B_END = 3 * D_ATT, 3 * D_ATT + N_HEADS, 6 * D_ATT + N_HEADS
GATE_SHARD = D_MODEL // N_CHIPS


LATE = SHARDED[:1]
EARLY = SHARDED[1:]
BIG = tuple(t for t, n in enumerate(EARLY) if n in ("w_up", "w_down"))
SMALL = tuple(t for t in range(len(EARLY)) if t not in BIG)


def _first_weights(w_in_slots):
    def cols(*ranges):
        parts = []
        for lo, hi in ranges:
            for j in range(N_CHIPS):
                a, b = max(lo, j * W_IN_SHARD), min(hi, (j + 1) * W_IN_SHARD)
                if a < b:
                    parts.append(w_in_slots[j, :, a - j * W_IN_SHARD:b - j * W_IN_SHARD])
        return parts

    forget = jnp.concatenate(cols((Q_END, F_END)) + [jnp.zeros((D_MODEL, F_PAD - N_HEADS), BF16)], axis=1)
    return {"qkv": jnp.concatenate(cols((0, Q_END), (F_END, B_END)), axis=1), "gate": jnp.concatenate(cols((B_END, D_IN)), axis=1),
            "forget": forget}


GATE_ROWS = 2 * ROW_ALIGN


def _gate_bits(b_gate):
    bits = lax.bitcast_convert_type(b_gate, BF16).reshape(2, 2 * GATE_SHARD)
    return jnp.concatenate([bits, jnp.zeros((GATE_ROWS - 2, 2 * GATE_SHARD), BF16)], axis=0)


def _rest_weights(gathered):
    rows = lambda a: a.reshape(N_CHIPS * a.shape[1], a.shape[2])
    bits = gathered["b_gate"][:, :2].reshape(N_CHIPS, 2, GATE_SHARD, 2)
    b_gate = jnp.transpose(lax.bitcast_convert_type(bits, F32), (1, 0, 2)).reshape(2, D_MODEL)
    return {"branch_fox": gathered["w_branch_fox"], "branch_sb": gathered["w_branch_sb"], "out": rows(gathered["w_out"]),
            "up": gathered["w_up"], "down": rows(gathered["w_down"]), "ple_gate": rows(gathered["w_ple_gate"]),
            "ple": gathered["w_ple"], "b_gate": b_gate}


def _early_slots(gw):
    rows = lambda a: a.reshape(N_CHIPS, a.shape[0] // N_CHIPS, a.shape[1])
    return {"w_branch_fox": gw["branch_fox"], "w_branch_sb": gw["branch_sb"], "w_out": rows(gw["out"]), "w_up": gw["up"],
            "w_down": rows(gw["down"]), "w_ple_gate": rows(gw["ple_gate"]), "w_ple": gw["ple"]}


W_IN_FLAT = (W_IN_SHARD * D_MODEL // LANES, LANES)


def _w_in_slots(gw):
    c = D_MODEL // LANES
    g_t = jnp.concatenate([gw["qkv"][:Q_END * c], gw["forget"][:N_HEADS * c], gw["qkv"][Q_END * c:], gw["gate"]], axis=0)
    return g_t.reshape((N_CHIPS,) + W_IN_FLAT)


def _flat(a):
    return jnp.transpose(a, (2, 0, 1)).reshape(W_IN_FLAT)


def _unflat(a):
    return jnp.transpose(a.reshape(W_IN_SHARD, D_MODEL // LANES, LANES), (1, 2, 0)).reshape(1, D_MODEL, W_IN_SHARD)


def kernel(x, p, g_mix, w_in, b_forget, b_gate, w_branch_fox, w_branch_sb, w_out, g_mlp, w_up, w_down, g_ple, w_ple_gate, w_ple, g_final, loss_target, m_g_mix, m_w_in, m_b_forget, m_b_gate, m_w_branch_fox, m_w_branch_sb, m_w_out, m_g_mlp, m_w_up, m_w_down, m_g_ple, m_w_ple_gate, m_w_ple, m_g_final, v_g_mix, v_w_in, v_b_forget, v_b_gate, v_w_branch_fox, v_w_branch_sb, v_w_out, v_g_mlp, v_w_up, v_w_down, v_g_ple, v_w_ple_gate, v_w_ple, v_g_final):
    weights = dict(g_mix=g_mix, w_in=w_in, b_forget=b_forget, b_gate=b_gate, w_branch_fox=w_branch_fox,
                   w_branch_sb=w_branch_sb, w_out=w_out, g_mlp=g_mlp, w_up=w_up, w_down=w_down, g_ple=g_ple,
                   w_ple_gate=w_ple_gate, w_ple=w_ple, g_final=g_final)
    first = dict(g_mix=m_g_mix, w_in=m_w_in, b_forget=m_b_forget, b_gate=m_b_gate, w_branch_fox=m_w_branch_fox,
                 w_branch_sb=m_w_branch_sb, w_out=m_w_out, g_mlp=m_g_mlp, w_up=m_w_up, w_down=m_w_down, g_ple=m_g_ple,
                 w_ple_gate=m_w_ple_gate, w_ple=m_w_ple, g_final=m_g_final)
    second = dict(g_mix=v_g_mix, w_in=v_w_in, b_forget=v_b_forget, b_gate=v_b_gate, w_branch_fox=v_w_branch_fox,
                  w_branch_sb=v_w_branch_sb, w_out=v_w_out, g_mlp=v_g_mlp, w_up=v_w_up, w_down=v_w_down, g_ple=v_g_ple,
                  w_ple_gate=v_w_ple_gate, w_ple=v_w_ple, g_final=v_g_final)
    cx, cy, cc = _place()
    chip = 2 * cx + cy
    place = jnp.stack([chip, cc]).astype(jnp.int32)
    col0 = chip * GATE_SHARD

    first_gather = _gather_weights([weights[n][0].astype(BF16) for n in LATE])
    rest = _gather_weights([weights[n][0].astype(BF16) for n in EARLY] + [_gate_bits(b_gate[0])])
    vec = {"g_mix": g_mix, "b_forget": jnp.concatenate([b_forget, jnp.zeros((1, F_PAD - N_HEADS), F32)], axis=1),
           "g_mlp": g_mlp, "g_ple": g_ple, "g_final": g_final.reshape(1, D_MODEL)}

    loss, grad_x, reduced, gvec = _local_step(x, p[0], loss_target, first_gather, rest, vec, place)

    out = {}
    args = lambda n: (weights[n][0], first[n][0], second[n][0]) + tuple(reduced[n])
    for names, tag in [([EARLY[t] for t in SMALL], "small")] + [([EARLY[t]], EARLY[t]) for t in BIG]:
        for n, res in zip(names, _adamw_halves(place, [args(n) for n in names], "adamw_" + tag)):
            out[n] = [r[None] for r in res]
    (res,) = _adamw_halves(place, [(_flat(w_in), _flat(m_w_in), _flat(v_w_in)) + tuple(reduced["w_in"])], "adamw_w_in")
    out["w_in"] = [_unflat(r) for r in res]

    g_block = _sum_devices(_vec_block(gvec["g_mix"], gvec["g_mlp"], gvec["g_ple"], gvec["g_final"][0], gvec["b_forget"],
                                      gvec["b_gate"], loss), "reduce_vectors")
    loss = g_block[7, 0]
    g_gate = lax.dynamic_slice(g_block[5:7], (0, col0), (2, GATE_SHARD))
    blocks = [_vec_block(d["g_mix"], d["g_mlp"], d["g_ple"], d["g_final"], d["b_forget"], d["b_gate"][0])
              for d in (weights, first, second)]
    g_rows = jnp.concatenate([g_block[0:5], jnp.concatenate([g_gate, jnp.zeros((2, D_MODEL - GATE_SHARD), F32)], axis=1),
                              jnp.zeros((1, D_MODEL), F32)], axis=0)
    res = (g_rows,) + tuple(_adamw_vec(blocks[0], g_rows, blocks[1], blocks[2]))
    out["g_mix"] = [r[0:1] for r in res]
    out["g_mlp"] = [r[1:2] for r in res]
    out["g_ple"] = [r[2:3] for r in res]
    out["g_final"] = [r[3] for r in res]
    out["b_forget"] = [r[4:5, :N_HEADS] for r in res]
    out["b_gate"] = [r[5:7, :GATE_SHARD][None] for r in res]
    return (loss, grad_x, *[out[n][0] for n in WEIGHT_NAMES], *[out[n][1] for n in WEIGHT_NAMES],
            *[out[n][2] for n in WEIGHT_NAMES], *[out[n][3] for n in WEIGHT_NAMES])
```

```python
import jax
import jax.numpy as jnp
from jax import lax
from jax.experimental import pallas as pl
from jax.experimental.pallas import tpu as pltpu

F32 = jnp.float32
BF16 = jnp.bfloat16

D_MODEL = 1024
HEAD_DIM = 64
N_HEADS = 8
D_ATT = N_HEADS * HEAD_DIM
D_PLE = 256
D_IN = 6 * D_ATT + N_HEADS + 2 * D_MODEL
F_PAD = 128
EPS = 1e-6
SCALE = HEAD_DIM ** -0.5
N_CHIPS = 4
LANES = 128
ATT_BLOCK = 256
FOX_TILES = (512, 512)
SB_TILES = (512, 256)
NEG = -1e30

ADAM_LR = 0.001
ADAM_B1 = 0.9
ADAM_B2 = 0.999
ADAM_EPS = 1e-08
ADAM_WD = 0.01
ADAM_STEP = 10

VMEM_LIMIT = 56 * 1024 * 1024

MESH = pl.DeviceIdType.MESH


def _cparams(sem=None):
    return pltpu.CompilerParams(dimension_semantics=sem, vmem_limit_bytes=VMEM_LIMIT)


def _relu2(t):
    t = t.astype(F32)
    return t * t


_DIMS = {"nn": (((1,), (0,)), ((), ())), "nt": (((1,), (1,)), ((), ())), "tn": (((0,), (0,)), ((), ()))}
NT_DIMS = _DIMS["nt"]
TN_DIMS = _DIMS["tn"]


def _mm(a, b, *, mode, name, out_dtype=F32, tm=512, tn=512, tk=512, add=None, a_fn=None, epi=None, extra=None,
        col_shards=False, behind=None, flat_out=False, t_out=False):
    if mode == "nn":
        (m, k), n = a.shape, b.shape[-1]
    elif mode == "nt":
        (m, k), n = a.shape, b.shape[-2]
    else:
        (k, m), n = a.shape, b.shape[1]
    shard = None
    if col_shards:
        if mode == "nn":
            shard, n = n, N_CHIPS * n
            tn = min(tn, shard)
        elif mode == "nt":
            shard = b.shape[-1]
            tk = min(tk, shard)
        else:
            shard = n // N_CHIPS
            if tn < n:
                tn = min(tn, shard)
    tm, tn, tk = min(tm, m), min(tn, n), min(tk, k)
    assert m % tm == 0 and n % tn == 0 and k % tk == 0, (name, m, n, k)
    nk = k // tk
    all_shards = col_shards and mode == "tn" and tn == n
    a_spec = {"nn": pl.BlockSpec((tm, tk), lambda i, j, kk: (i, kk)),
              "nt": pl.BlockSpec((tm, tk), lambda i, j, kk: (i, kk)),
              "tn": pl.BlockSpec((tk, tm), lambda i, j, kk: (kk, i))}[mode]
    b_spec = {"nn": pl.BlockSpec((tk, tn), lambda i, j, kk: (kk, j)),
              "nt": pl.BlockSpec((tn, tk), lambda i, j, kk: (j, kk)),
              "tn": pl.BlockSpec((tk, tn), lambda i, j, kk: (kk, j))}[mode]
    o_spec = pl.BlockSpec((tm, tn), lambda i, j, kk: (i, j))
    out_shape = (m, n)
    if col_shards and mode == "nn":
        per = shard // tn
        b_spec = pl.BlockSpec((None, tk, tn), lambda i, j, kk: (j // per, kk, j % per))
    elif col_shards and mode == "nt":
        per = shard // tk
        b_spec = pl.BlockSpec((None, tn, tk), lambda i, j, kk: (kk // per, j, kk % per))
    elif col_shards:
        assert add is None and extra is None
        if all_shards:
            o_spec = pl.BlockSpec((N_CHIPS, tm, shard), lambda i, j, kk: (0, i, 0))
        else:
            per = shard // tn
            o_spec = pl.BlockSpec((None, tm, tn), lambda i, j, kk: (j // per, i, j % per))
        out_shape = (N_CHIPS, m, shard)
    if flat_out:
        assert mode == "tn" and tn == n == D_MODEL and not col_shards and add is None and extra is None
        chunks = D_MODEL // LANES
        o_spec = pl.BlockSpec((tm * chunks, LANES), lambda i, j, kk: (i, 0))
        out_shape = (m * chunks, LANES)
    if t_out:
        assert mode == "tn" and not col_shards and not flat_out and add is None and extra is None
        o_spec = pl.BlockSpec((tn, tm), lambda i, j, kk: (j, i))
        out_shape = (n, m)
    operands, in_specs = [a, b], [a_spec, b_spec]
    third = add if add is not None else extra
    if third is not None:
        operands.append(third)
        in_specs.append(o_spec)

    def body(*refs):
        a_ref, b_ref = refs[0], refs[1]
        t_ref = refs[2] if third is not None else None
        o_ref = refs[3] if third is not None else refs[2]
        acc_ref = refs[-1] if nk > 1 else None
        at = a_ref[...]
        if a_fn is not None:
            at = a_fn(at)
        part = lax.dot_general(at.astype(BF16), b_ref[...].astype(BF16), _DIMS[mode], preferred_element_type=F32)

        def finish(acc):
            if epi is not None:
                acc = epi(acc, None if t_ref is None else t_ref[...])
            elif add is not None:
                acc = acc + t_ref[...].astype(F32)
            if flat_out:
                for q in range(D_MODEL // LANES):
                    o_ref[pl.ds(q, tm, stride=D_MODEL // LANES), :] = acc[:, q * LANES:(q + 1) * LANES].astype(o_ref.dtype)
                return
            if t_out:
                o_ref[...] = acc.T.astype(o_ref.dtype)
                return
            if all_shards:
                for slot in range(N_CHIPS):
                    o_ref[slot] = acc[:, slot * shard:(slot + 1) * shard].astype(o_ref.dtype)
                return
            o_ref[...] = acc.astype(o_ref.dtype)

        if nk == 1:
            finish(part)
        else:
            kk = pl.program_id(2)

            @pl.when(kk == 0)
            def _():
                acc_ref[...] = part

            @pl.when(kk > 0)
            def _():
                acc_ref[...] += part

            @pl.when(kk == nk - 1)
            def _():
                finish(acc_ref[...])

    call = dict(name=name, grid=(m // tm, n // tn, nk), in_specs=in_specs,
                scratch_shapes=[pltpu.VMEM((tm, tn), F32)] if nk > 1 else [])
    if behind is not None:
        (res,), exchanged = _call_behind(body, behind, out_specs=[o_spec], out_shape=[jax.ShapeDtypeStruct(out_shape, out_dtype)],
                                         operands=operands, **call)
        return res, exchanged
    return pl.pallas_call(body, out_specs=o_spec, out_shape=jax.ShapeDtypeStruct(out_shape, out_dtype),
                          compiler_params=_cparams(("parallel", "parallel", "arbitrary")), **call)(*operands)


ROW_TILE = 512


def _row_spec(width=D_MODEL, rows=ROW_TILE):
    return pl.BlockSpec((rows, width), lambda i: (i, 0))


def _vec_spec(rows=1, width=D_MODEL):
    return pl.BlockSpec((rows, width), lambda i: (0, 0))


def _xhat(x):
    r = lax.rsqrt(jnp.mean(x * x, axis=-1, keepdims=True) + EPS)
    return x * r, r


def _rms_bwd_rows(dh, x, g):
    xh, r = _xhat(x)
    dxh = dh * g
    dx = r * (dxh - xh * jnp.mean(dxh * xh, axis=-1, keepdims=True))
    return dx, jnp.sum(dh * xh, axis=0, keepdims=True)


def _norm_fwd(x, g, name, behind):
    t = x.shape[0]

    def body(x_ref, g_ref, h_ref):
        xh, _ = _xhat(x_ref[...])
        h_ref[...] = (xh * g_ref[...]).astype(BF16)

    return _call_behind(body, behind, name=name, grid=(t // ROW_TILE,), in_specs=[_row_spec(), _vec_spec()],
                        out_specs=[_row_spec()], out_shape=[jax.ShapeDtypeStruct((t, D_MODEL), BF16)], scratch_shapes=[],
                        operands=(x, g))


def _mm_res_norm(a, b, res, g, name, a_fn=None):
    t, k = a.shape

    def body(a_ref, b_ref, res_ref, g_ref, x_ref, h_ref):
        at = a_ref[...] if a_fn is None else a_fn(a_ref[...])
        x_new = res_ref[...] + _dot(at.astype(BF16), b_ref[...])
        x_ref[...] = x_new
        h_ref[...] = (_xhat(x_new)[0] * g_ref[...]).astype(BF16)

    return pl.pallas_call(
        body, name=name, grid=(t // ROW_TILE,),
        in_specs=[pl.BlockSpec((ROW_TILE, k), lambda i: (i, 0)), pl.BlockSpec(b.shape, lambda i: (0, 0)), _row_spec(), _vec_spec()],
        out_specs=[_row_spec(), _row_spec()],
        out_shape=[jax.ShapeDtypeStruct((t, D_MODEL), F32), jax.ShapeDtypeStruct((t, D_MODEL), BF16)],
        compiler_params=_cparams(("parallel",)),
    )(a, b, res, g)


def _mm_norm_bwd(pairs, dh_first, x, g, dres, name, behind=None):
    t = x.shape[0]
    operands, in_specs = [], []
    for a, b in pairs:
        if b.ndim == 3:
            for j in range(b.shape[0]):
                operands += [a, b]
                in_specs += [pl.BlockSpec((ROW_TILE, b.shape[2]), lambda i, j=j: (i, j)),
                             pl.BlockSpec((None, D_MODEL, b.shape[2]), lambda i, j=j: (j, 0, 0))]
        else:
            operands += [a, b]
            in_specs += [pl.BlockSpec((ROW_TILE, a.shape[1]), lambda i: (i, 0)), pl.BlockSpec(b.shape, lambda i: (0, 0))]
    n_mm = len(operands)
    operands += [x, g, dres] + ([] if dh_first is None else [dh_first])
    in_specs += [_row_spec(), _vec_spec(), _row_spec()] + ([] if dh_first is None else [_row_spec()])

    def body(*refs):
        x_ref, g_ref, dres_ref = refs[n_mm:n_mm + 3]
        dx_ref, dxb_ref, dg_ref = refs[-3:]
        dh = 0.0 if dh_first is None else refs[n_mm + 3][...]
        for k in range(0, n_mm, 2):
            dh = dh + lax.dot_general(refs[k][...].astype(BF16), refs[k + 1][...].astype(BF16), NT_DIMS,
                                      preferred_element_type=F32)
        dx, dg = _rms_bwd_rows(dh, x_ref[...], g_ref[...])
        dx = dx + dres_ref[...]
        dx_ref[...] = dx
        dxb_ref[...] = dx.astype(BF16)

        @pl.when(pl.program_id(0) == 0)
        def _():
            dg_ref[...] = jnp.zeros_like(dg_ref)

        dg_ref[...] += dg

    call = dict(name=name, grid=(t // ROW_TILE,), in_specs=in_specs, out_specs=[_row_spec(), _row_spec(), _vec_spec()],
                out_shape=[jax.ShapeDtypeStruct((t, D_MODEL), F32), jax.ShapeDtypeStruct((t, D_MODEL), BF16),
                           jax.ShapeDtypeStruct((1, D_MODEL), F32)])
    if behind is not None:
        return _call_behind(body, behind, scratch_shapes=[], operands=operands, **call)
    return pl.pallas_call(body, compiler_params=_cparams(("arbitrary",)), **call)(*operands)


def _shards_spec(w):
    return pl.BlockSpec(w.shape, lambda i: (0, 0, 0))


def _gate_fwd(gl, b_gate, o_fox, o_sb, w_fox, w_sb):
    t = o_fox.shape[0]

    def body(gla_ref, glb_ref, b_ref, ofox_ref, osb_ref, wf_ref, ws_ref, m_ref, of_ref, os_ref):
        of = jnp.concatenate([_dot(ofox_ref[...], wf_ref[j]) for j in range(N_CHIPS)], axis=1)
        os_ = jnp.concatenate([_dot(osb_ref[...], ws_ref[j]) for j in range(N_CHIPS)], axis=1)
        ga = jax.nn.sigmoid(gla_ref[...] + b_ref[0:1, :])
        gb = jax.nn.sigmoid(glb_ref[...] + b_ref[1:2, :])
        of_ref[...] = of
        os_ref[...] = os_
        m_ref[...] = (ga * of + gb * os_).astype(BF16)

    return pl.pallas_call(
        body, name="gate_fwd", grid=(t // ROW_TILE,),
        in_specs=[pl.BlockSpec((ROW_TILE, D_MODEL), lambda i: (i, 0)), pl.BlockSpec((ROW_TILE, D_MODEL), lambda i: (i, 1)),
                  _vec_spec(2), _row_spec(D_ATT), _row_spec(D_ATT), _shards_spec(w_fox), _shards_spec(w_sb)],
        out_specs=[_row_spec(), _row_spec(), _row_spec()],
        out_shape=[jax.ShapeDtypeStruct((t, D_MODEL), BF16)] + [jax.ShapeDtypeStruct((t, D_MODEL), F32)] * 2,
        compiler_params=_cparams(("parallel",)),
    )(gl, gl, b_gate, o_fox, o_sb, w_fox, w_sb)


def _gate_bwd(gl, b_gate, of, os_, dx, w_out, w_fox, w_sb):
    t = of.shape[0]
    shard = D_MODEL // N_CHIPS

    def back(d, w_ref):
        return sum(_dot(d[:, j * shard:(j + 1) * shard], w_ref[j], NT_DIMS) for j in range(N_CHIPS)).astype(BF16)

    def body(gla_ref, glb_ref, b_ref, of_ref, os_ref, dx_ref, w_ref, wf_ref, ws_ref,
             dof_ref, dos_ref, dgl_ref, db_ref, dofox_ref, dosb_ref):
        dm = _dot(dx_ref[...], w_ref[...], NT_DIMS)
        ga = jax.nn.sigmoid(gla_ref[...] + b_ref[0:1, :])
        gb = jax.nn.sigmoid(glb_ref[...] + b_ref[1:2, :])
        dof = (dm * ga).astype(BF16)
        dos = (dm * gb).astype(BF16)
        dof_ref[...] = dof
        dos_ref[...] = dos
        dofox_ref[...] = back(dof, wf_ref)
        dosb_ref[...] = back(dos, ws_ref)
        dgla = dm * of_ref[...] * ga * (1.0 - ga)
        dglb = dm * os_ref[...] * gb * (1.0 - gb)
        dgl_ref[:, 0:D_MODEL] = dgla.astype(BF16)
        dgl_ref[:, D_MODEL:2 * D_MODEL] = dglb.astype(BF16)

        @pl.when(pl.program_id(0) == 0)
        def _():
            db_ref[...] = jnp.zeros_like(db_ref)

        db_ref[0:1, :] += jnp.sum(dgla, axis=0, keepdims=True)
        db_ref[1:2, :] += jnp.sum(dglb, axis=0, keepdims=True)

    outs = pl.pallas_call(
        body, name="gate_bwd", grid=(t // ROW_TILE,),
        in_specs=[pl.BlockSpec((ROW_TILE, D_MODEL), lambda i: (i, 0)), pl.BlockSpec((ROW_TILE, D_MODEL), lambda i: (i, 1)),
                  _vec_spec(2), _row_spec(), _row_spec(), _row_spec(), pl.BlockSpec(w_out.shape, lambda i: (0, 0)),
                  _shards_spec(w_fox), _shards_spec(w_sb)],
        out_specs=[_row_spec(), _row_spec(), _row_spec(2 * D_MODEL), _vec_spec(2), _row_spec(D_ATT), _row_spec(D_ATT)],
        out_shape=[jax.ShapeDtypeStruct((t, D_MODEL), BF16)] * 2 + [jax.ShapeDtypeStruct((t, 2 * D_MODEL), BF16),
                                                                      jax.ShapeDtypeStruct((2, D_MODEL), F32)]
        + [jax.ShapeDtypeStruct((t, D_ATT), BF16)] * 2,
        compiler_params=_cparams(("arbitrary",)),
    )(gl, gl, b_gate, of, os_, dx, w_out, w_fox, w_sb)
    return outs


def _head_and_loss(x2, h3, p, w_gate, w_ple, g_final, target):
    t = x2.shape[0]

    def body(x2_ref, h3_ref, p_ref, wg_ref, wp_ref, g_ref, tgt_ref, dx3_ref, dpre_ref, dpe_ref, dg_ref, loss_ref):
        gp = jax.nn.sigmoid(_dot(h3_ref[...], wg_ref[...]))
        p_t = p_ref[...].astype(BF16)
        pe_t = jnp.concatenate([_dot(p_t, wp_ref[j]) for j in range(N_CHIPS)], axis=1)
        x3 = x2_ref[...] + gp * pe_t
        g = g_ref[...]
        xh, _ = _xhat(x3)
        err = xh * g - tgt_ref[...]
        dy = err * (1.0 / D_MODEL)
        dx3, dg = _rms_bwd_rows(dy, x3, g)
        dx3_ref[...] = dx3
        dpre_ref[...] = (dx3 * pe_t * gp * (1.0 - gp)).astype(BF16)
        dpe_ref[...] = (dx3 * gp).astype(BF16)

        @pl.when(pl.program_id(0) == 0)
        def _():
            dg_ref[...] = jnp.zeros_like(dg_ref)
            loss_ref[...] = jnp.zeros_like(loss_ref)

        dg_ref[...] += dg
        loss_ref[...] += 0.5 * jnp.sum(jnp.mean(err * err, axis=-1, keepdims=True), axis=0, keepdims=True)

    return pl.pallas_call(
        body, name="head_and_loss", grid=(t // ROW_TILE,),
        in_specs=[_row_spec(), _row_spec(), _row_spec(D_PLE), pl.BlockSpec(w_gate.shape, lambda i: (0, 0)),
                  pl.BlockSpec(w_ple.shape, lambda i: (0, 0, 0)), _vec_spec(), _row_spec()],
        out_specs=[_row_spec(), _row_spec(), _row_spec(), _vec_spec(), _vec_spec(1, LANES)],
        out_shape=[jax.ShapeDtypeStruct((t, D_MODEL), F32), jax.ShapeDtypeStruct((t, D_MODEL), BF16),
                   jax.ShapeDtypeStruct((t, D_MODEL), BF16), jax.ShapeDtypeStruct((1, D_MODEL), F32),
                   jax.ShapeDtypeStruct((1, LANES), F32)],
        compiler_params=_cparams(("arbitrary",)),
    )(x2, h3, p, w_gate, w_ple, g_final, target)


def _split3(v):
    hi = v.astype(BF16)
    r1 = v - hi.astype(F32)
    mid = r1.astype(BF16)
    lo = (r1 - mid.astype(F32)).astype(BF16)
    return hi, mid, lo


def _split2(v):
    hi = v.astype(BF16)
    return jnp.concatenate([hi, (v - hi.astype(F32)).astype(BF16)], axis=1)


def _dot(a, b, dims=_DIMS["nn"]):
    return lax.dot_general(a, b, dims, preferred_element_type=F32)


def _tri(n, rel):
    row = lax.broadcasted_iota(jnp.int32, (n, n), 0)
    col = lax.broadcasted_iota(jnp.int32, (n, n), 1)
    return rel(row, col).astype(BF16)


def _tri2(n, rel):
    t = _tri(n, rel)
    return jnp.concatenate([t, t], axis=0)


def _log_sigmoid(v):
    return -(jnp.maximum(-v, 0.0) + jnp.log(1.0 + jnp.exp(-jnp.abs(v))))


def _fox_prep(fl, b_forget, batch, seq):
    nb = seq // ATT_BLOCK

    def body(fl_ref, b_ref, cw_ref, cr_ref):
        col = lax.broadcasted_iota(jnp.int32, (ATT_BLOCK, F_PAD), 1)
        lower = _tri(ATT_BLOCK, lambda r, c: c <= r)
        upper = _tri(ATT_BLOCK, lambda r, c: r <= c)
        expand = (lax.broadcasted_iota(jnp.int32, (F_PAD, D_ATT), 1) // HEAD_DIM
                  == lax.broadcasted_iota(jnp.int32, (F_PAD, D_ATT), 0)).astype(BF16)
        carry_w = jnp.zeros((1, D_ATT), F32)
        carry_r = jnp.zeros((F_PAD, 1), F32)
        for i in range(nb):
            blk = slice(i * ATT_BLOCK, (i + 1) * ATT_BLOCK)
            logf = jnp.where(col < N_HEADS, _log_sigmoid(fl_ref[blk, :] + b_ref[...]), 0.0)
            cw = jnp.zeros((ATT_BLOCK, D_ATT), F32) + carry_w
            cr = jnp.zeros((F_PAD, ATT_BLOCK), F32) + carry_r
            for part in _split3(logf):
                cw += _dot(lower, _dot(part, expand).astype(BF16))
                cr += _dot(part, upper, TN_DIMS)
            cw_ref[blk, :] = cw
            cr_ref[:, blk] = cr[0:N_HEADS, :]
            carry_w = cw[ATT_BLOCK - 1:ATT_BLOCK, :]
            carry_r = cr[:, ATT_BLOCK - 1:ATT_BLOCK]

    return pl.pallas_call(
        body, name="fox_prep", grid=(batch,),
        in_specs=[pl.BlockSpec((seq, F_PAD), lambda b: (b, 0)), pl.BlockSpec((1, F_PAD), lambda b: (0, 0))],
        out_specs=[pl.BlockSpec((seq, D_ATT), lambda b: (b, 0)), pl.BlockSpec((N_HEADS, seq), lambda b: (b, 0))],
        out_shape=[jax.ShapeDtypeStruct((batch * seq, D_ATT), F32), jax.ShapeDtypeStruct((batch * N_HEADS, seq), F32)],
        compiler_params=_cparams(("parallel",)),
    )(fl, b_forget)


def _fox_post(dcs_wide, drs_wide, fl, b_forget, batch, seq):
    nb = seq // ATT_BLOCK

    def body(dcs_ref, drs_ref, fl_ref, b_ref, dfl_ref, db_ref):
        pick = (lax.broadcasted_iota(jnp.int32, (D_ATT, F_PAD), 0)
                == lax.broadcasted_iota(jnp.int32, (D_ATT, F_PAD), 1) * HEAD_DIM).astype(BF16)
        upper = _tri(ATT_BLOCK, lambda r, c: r <= c)
        col = lax.broadcasted_iota(jnp.int32, (ATT_BLOCK, F_PAD), 1)

        @pl.when(pl.program_id(0) == 0)
        def _():
            db_ref[...] = jnp.zeros_like(db_ref)

        carry = jnp.zeros((1, F_PAD), F32)
        for i in reversed(range(nb)):
            blk = slice(i * ATT_BLOCK, (i + 1) * ATT_BLOCK)
            narrow = jnp.zeros((ATT_BLOCK, F_PAD), F32)
            for part in _split3(drs_ref[blk, :] - dcs_ref[blk, :]):
                narrow += _dot(part, pick)
            after = jnp.zeros((ATT_BLOCK, F_PAD), F32) + carry
            for part in _split3(narrow):
                after += _dot(upper, part)
            carry = after[0:1, :]
            pre = fl_ref[blk, :] + b_ref[...]
            dfl = jnp.where(col < N_HEADS, after * jax.nn.sigmoid(-pre), 0.0)
            dfl_ref[blk, :] = dfl.astype(BF16)
            db_ref[...] += jnp.sum(dfl, axis=0, keepdims=True)

    return pl.pallas_call(
        body, name="fox_post", grid=(batch,),
        in_specs=[pl.BlockSpec((seq, D_ATT), lambda b: (b, 0)), pl.BlockSpec((seq, D_ATT), lambda b: (b, 0)),
                  pl.BlockSpec((seq, F_PAD), lambda b: (b, 0)), pl.BlockSpec((1, F_PAD), lambda b: (0, 0))],
        out_specs=[pl.BlockSpec((seq, F_PAD), lambda b: (b, 0)), pl.BlockSpec((1, F_PAD), lambda b: (0, 0))],
        out_shape=[jax.ShapeDtypeStruct((batch * seq, F_PAD), BF16), jax.ShapeDtypeStruct((1, F_PAD), F32)],
        compiler_params=_cparams(("arbitrary",)),
    )(dcs_wide, drs_wide, fl, b_forget)


N_PAIRS = N_HEADS // 2


def _att_specs(seq, col0, tq):
    nq = seq // tq
    q = pl.BlockSpec((tq, LANES), lambda b, hp, qi: (b * nq + qi, col0 + hp))
    k = pl.BlockSpec((seq, LANES), lambda b, hp, qi: (b, col0 + N_PAIRS + hp))
    v = pl.BlockSpec((seq, LANES), lambda b, hp, qi: (b, col0 + 2 * N_PAIRS + hp))
    return q, k, v


def _qblock_spec(seq, tq):
    nq = seq // tq
    return pl.BlockSpec((tq, LANES), lambda b, hp, qi: (b * nq + qi, hp))


def _kv_out_spec(seq):
    return pl.BlockSpec((seq, LANES), lambda b, hp, qi: (b, hp))


def _head_masks():
    lane = lax.broadcasted_iota(jnp.int32, (1, LANES), 1)
    return [(lane >= HEAD_DIM * j) & (lane < HEAD_DIM * (j + 1)) for j in range(2)]


def _stack_heads(t, masks):
    zero = jnp.zeros_like(t)
    return jnp.concatenate([jnp.where(masks[0], t, zero), jnp.where(masks[1], t, zero)], axis=0)


def _stack_cols(t):
    return jnp.concatenate([t[:, 0:1], t[:, HEAD_DIM:HEAD_DIM + 1]], axis=0)


def _unstack(t2, masks):
    tq = t2.shape[0] // 2
    return jnp.where(masks[0], t2[:tq], t2[tq:])


def _stacked_ids(tq, tk):
    row = lax.broadcasted_iota(jnp.int32, (2 * tq, tk), 0)
    col = lax.broadcasted_iota(jnp.int32, (2 * tq, tk), 1)
    first = lax.broadcasted_iota(jnp.int32, (2 * tq, 1), 0) < tq
    return col - jnp.where(row < tq, row, row - tq), first


def _sweep(qi, tq, tk, step, init):
    per = tq // tk
    carry = lax.fori_loop(0, per * qi, lambda kb, c: step(kb, c, None, 0), init)
    for j in range(per):
        carry = step(per * qi + j, carry, -j * tk, j * tk)
    return carry


def _below(t2, top):
    tq = t2.shape[0] // 2
    return t2 if top == 0 else jnp.concatenate([t2[top:tq], t2[tq + top:]], axis=0)


def _put_below(old, new, top):
    if top == 0:
        return new
    tq = old.shape[0] // 2
    return jnp.concatenate([old[:top], new[:tq - top], old[tq:tq + top], new[tq - top:]], axis=0)


def _att_fwd(qkv, c_wide, c_row, batch, seq, behind):
    tq, tkf = FOX_TILES
    tqs, tks = SB_TILES
    assert tq == tqs and tkf == 2 * tks
    nq = seq // tq

    def body(qa_ref, ka_ref, va_ref, cw_ref, cr_ref, qb_ref, kb_ref, vb_ref, of_ref, lse_ref, os_ref, rt_ref):
        hp, qi = pl.program_id(1), pl.program_id(2)
        masks = _head_masks()
        ahead_f, first = _stacked_ids(tq, tkf)
        ahead_s, _ = _stacked_ids(tq, tks)
        later = _tri2(tks, lambda r, c: r > c)
        q2f = _stack_heads(qa_ref[...], masks) * SCALE
        q2s = _stack_heads(qb_ref[...], masks) * SCALE
        ct = _stack_cols(cw_ref[...])

        def fox(kb, carry, lead):
            m, l, acc = carry
            k0 = pl.multiple_of(kb * tkf, tkf)
            cs = jnp.where(first, cr_ref[pl.ds(2 * hp, 1), pl.ds(k0, tkf)], cr_ref[pl.ds(2 * hp + 1, 1), pl.ds(k0, tkf)])
            s = _dot(q2f, ka_ref[pl.ds(k0, tkf), :], NT_DIMS) + ct - cs
            if lead is not None:
                s = jnp.where(ahead_f <= lead, s, NEG)
            m_new = jnp.maximum(m, jnp.max(s, axis=1, keepdims=True))
            p = jnp.exp(s - m_new)
            alpha = jnp.exp(m - m_new)
            l = alpha * l + jnp.sum(p, axis=1, keepdims=True)
            acc = alpha * acc + _dot(p.astype(BF16), va_ref[pl.ds(k0, tkf), :])
            return m_new, l, acc

        def sb(kb, carry, lead):
            run, acc = carry
            k0 = pl.multiple_of(kb * tks, tks)
            ls, lsn = _sb_logits(q2s, kb_ref[pl.ds(k0, tks), :])
            if lead is not None:
                lsn = jnp.where(ahead_s < lead, lsn, 0.0)
            w = jnp.exp(ls + _dot(_split2(lsn), later) + run)
            if lead is not None:
                w = jnp.where(ahead_s < lead, w, 0.0)
            return run + jnp.sum(lsn, axis=1, keepdims=True), acc + _dot(w.astype(BF16), vb_ref[pl.ds(k0, tks), :])

        fox_c = (jnp.full((2 * tq, 1), NEG, F32), jnp.zeros((2 * tq, 1), F32), jnp.zeros((2 * tq, LANES), F32))
        sb_c = (jnp.zeros((2 * tq, 1), F32), jnp.zeros((2 * tq, LANES), F32))
        sb_c = sb(2 * qi, sb(2 * qi + 1, sb_c, -tks), 0)

        def both(i, carries):
            fox_c, sb_c = carries
            return fox(i, fox_c, None), sb(2 * qi - 2 - 2 * i, sb(2 * qi - 1 - 2 * i, sb_c, None), None)

        fox_c, (run, acc_s) = lax.fori_loop(0, qi, both, (fox_c, sb_c))
        m, l, acc = fox(qi, fox_c, 0)
        of_ref[...] = _unstack(acc / l, masks).astype(BF16)
        lse_ref[...] = _unstack(m + jnp.log(l), masks)
        os_ref[...] = _unstack(acc_s, masks).astype(BF16)
        rt_ref[...] = _unstack(run, masks)

    qa, ka, va = _att_specs(seq, 0, tq)
    qb_, kb_, vb_ = _att_specs(seq, 3 * N_PAIRS, tq)
    qb = _qblock_spec(seq, tq)
    half, wide = jax.ShapeDtypeStruct((batch * seq, D_ATT), BF16), jax.ShapeDtypeStruct((batch * seq, D_ATT), F32)
    return _call_behind(
        body, behind, name="att_fwd", grid=(batch, N_PAIRS, nq),
        in_specs=[qa, ka, va, qb, pl.BlockSpec((N_HEADS, seq), lambda b, hp, qi: (b, 0)), qb_, kb_, vb_],
        out_specs=[qb, qb, qb, qb], out_shape=[half, wide, half, wide], scratch_shapes=[],
        operands=(qkv, qkv, qkv, c_wide, c_row, qkv, qkv, qkv))


def _fox_bwd(qkv, c_wide, c_row, o, do, lse_wide, batch, seq, behind):
    tq, tk = FOX_TILES
    nq = seq // tq

    def body(q_ref, k_ref, v_ref, cw_ref, cr_ref, o_ref, do_ref, lse_ref,
             dq_ref, dk_ref, dv_ref, dcs_ref, drs_ref, dkc_acc, dv_acc):
        hp, qi = pl.program_id(1), pl.program_id(2)

        @pl.when(qi == 0)
        def _():
            dkc_acc[...] = jnp.zeros_like(dkc_acc)
            dv_acc[...] = jnp.zeros_like(dv_acc)

        masks = _head_masks()
        ahead, first = _stacked_ids(tq, tk)
        q_t, do_t = q_ref[...], do_ref[...]
        q2 = _stack_heads(q_t, masks) * SCALE
        do2 = _stack_heads(do_t, masks)
        q_and_ones = jnp.concatenate([q2, _stack_heads(jnp.ones_like(q_t), masks)], axis=1)
        ct = _stack_cols(cw_ref[...])
        lse = _stack_cols(lse_ref[...])
        prod = do_t.astype(F32) * o_ref[...].astype(F32)
        delta = jnp.concatenate([jnp.sum(jnp.where(mk, prod, 0.0), axis=1, keepdims=True) for mk in masks], axis=0)

        def step(kb, carry, lead, top):
            dq_acc, rs = carry
            k0 = pl.multiple_of(kb * tk, tk)
            kblk = k_ref[pl.ds(k0, tk), :]
            cs = jnp.where(first, cr_ref[pl.ds(2 * hp, 1), pl.ds(k0, tk)], cr_ref[pl.ds(2 * hp + 1, 1), pl.ds(k0, tk)])
            p = jnp.exp(_dot(q2, kblk, NT_DIMS) + ct - cs - lse)
            if lead is not None:
                p = jnp.where(ahead <= lead, p, 0.0)
            dp = _dot(do2, v_ref[pl.ds(k0, tk), :], NT_DIMS)
            ds = (p * (dp - delta)).astype(BF16)
            dkc_acc[pl.ds(k0, tk), :] += _dot(ds, q_and_ones, TN_DIMS)
            dv_acc[pl.ds(k0, tk), :] += _dot(p.astype(BF16), do2, TN_DIMS)
            return dq_acc + _dot(ds, kblk), rs + jnp.sum(ds.astype(F32), axis=1, keepdims=True)

        init = (jnp.zeros((2 * tq, LANES), F32), jnp.zeros((2 * tq, 1), F32))
        dq_acc, rs = _sweep(qi, tq, tk, step, init)
        dq_ref[...] = (_unstack(dq_acc, masks) * SCALE).astype(BF16)
        drs_ref[...] = _unstack(rs, masks)

        @pl.when(qi == nq - 1)
        def _():
            dk_ref[...] = dkc_acc[:, 0:LANES].astype(BF16)
            dcs_ref[...] = dkc_acc[:, LANES:2 * LANES]
            dv_ref[...] = dv_acc[...].astype(BF16)

    q_spec, k_spec, v_spec = _att_specs(seq, 0, tq)
    qb = _qblock_spec(seq, tq)
    return _call_behind(
        body, behind, name="fox_bwd", grid=(batch, N_PAIRS, nq),
        in_specs=[q_spec, k_spec, v_spec, qb, pl.BlockSpec((N_HEADS, seq), lambda b, hp, qi: (b, 0)), qb, qb, qb],
        out_specs=[qb, _kv_out_spec(seq), _kv_out_spec(seq), _kv_out_spec(seq), qb],
        out_shape=[jax.ShapeDtypeStruct((batch * seq, D_ATT), BF16)] * 3 + [jax.ShapeDtypeStruct((batch * seq, D_ATT), F32)] * 2,
        scratch_shapes=[pltpu.VMEM((seq, 2 * LANES), F32), pltpu.VMEM((seq, LANES), F32)],
        operands=(qkv, qkv, qkv, c_wide, c_row, o, do, lse_wide))


def _sb_logits(q2, kblk):
    z = _dot(q2, kblk, NT_DIMS)
    lsn = jnp.minimum(-z, 0.0) - jnp.log(1.0 + jnp.exp(-jnp.abs(z)))
    return lsn + z, lsn


def _sb_bwd(qkv, do, rt_wide, batch, seq, behind):
    tq, tk = SB_TILES
    nq = seq // tq

    def body(q_ref, k_ref, v_ref, do_ref, rt_ref, dq_ref, dk_ref, dv_ref, dk_acc, dv_acc):
        qi = pl.program_id(2)

        @pl.when(qi == 0)
        def _():
            dk_acc[...] = jnp.zeros_like(dk_acc)
            dv_acc[...] = jnp.zeros_like(dv_acc)

        masks = _head_masks()
        ahead, _ = _stacked_ids(tq, tk)
        later = _tri2(tk, lambda r, c: r > c)
        earlier = _tri(tk, lambda r, c: r < c)
        q2 = _stack_heads(q_ref[...], masks) * SCALE
        do2 = _stack_heads(do_ref[...], masks)
        total = _stack_cols(rt_ref[...])

        def step(kb, carry, lead, top):
            pref, epre, dq_acc = (_below(t, top) for t in carry)
            q_s, do_s = _below(q2, top), _below(do2, top)
            seen = None if lead is None else _below(ahead, top) < lead
            k0 = pl.multiple_of(kb * tk, tk)
            kblk = k_ref[pl.ds(k0, tk), :]
            ls, lsn_all = _sb_logits(q_s, kblk)
            lsn = lsn_all if lead is None else jnp.where(seen, lsn_all, 0.0)
            rs = jnp.sum(lsn, axis=1, keepdims=True)
            w = jnp.exp(ls + _dot(_split2(lsn), later) + (_below(total, top) - pref - rs))
            if lead is not None:
                w = jnp.where(seen, w, 0.0)
            e = w * _dot(do_s, v_ref[pl.ds(k0, tk), :], NT_DIMS)
            before = _dot(e.astype(BF16), earlier) + epre
            dz = e * jnp.exp(lsn_all) - jnp.exp(ls) * before
            if lead is not None:
                dz = jnp.where(seen, dz, 0.0)
            dz = dz.astype(BF16)
            dk_acc[pl.ds(k0, tk), :] += _dot(dz, q_s, TN_DIMS)
            dv_acc[pl.ds(k0, tk), :] += _dot(w.astype(BF16), do_s, TN_DIMS)
            new = (pref + rs, epre + jnp.sum(e, axis=1, keepdims=True), dq_acc + _dot(dz, kblk))
            return tuple(_put_below(o, n, top) for o, n in zip(carry, new))

        init = (jnp.zeros((2 * tq, 1), F32), jnp.zeros((2 * tq, 1), F32), jnp.zeros((2 * tq, LANES), F32))
        dq_acc = _sweep(qi, tq, tk, step, init)[2]
        dq_ref[...] = (_unstack(dq_acc, masks) * SCALE).astype(BF16)

        @pl.when(qi == nq - 1)
        def _():
            dk_ref[...] = dk_acc[...].astype(BF16)
            dv_ref[...] = dv_acc[...].astype(BF16)

    q_spec, k_spec, v_spec = _att_specs(seq, 3 * N_PAIRS, tq)
    qb = _qblock_spec(seq, tq)
    return _call_behind(
        body, behind, name="sb_bwd", grid=(batch, N_PAIRS, nq), in_specs=[q_spec, k_spec, v_spec, qb, qb],
        out_specs=[qb, _kv_out_spec(seq), _kv_out_spec(seq)], out_shape=[jax.ShapeDtypeStruct((batch * seq, D_ATT), BF16)] * 3,
        scratch_shapes=[pltpu.VMEM((seq, LANES), F32), pltpu.VMEM((seq, LANES), F32)], operands=(qkv, qkv, qkv, do, rt_wide))


def _local_step(x, p, target, first, rest, vec, place):
    batch, seq, _ = x.shape
    t = batch * seq
    x = x.reshape(t, D_MODEL)
    target = target.reshape(t, D_MODEL)
    p = p.reshape(t, D_PLE)
    big = dict(tm=1024, tn=1024, tk=1024)

    (h1,), (w_in_slots,) = _norm_fwd(x, vec["g_mix"], "norm_mix", first)
    w = _first_weights(w_in_slots)
    qkv = _mm(h1, w["qkv"], mode="nn", name="proj_qkv", out_dtype=BF16, **big)
    gl = _mm(h1, w["gate"], mode="nn", name="proj_gate", **big)
    fl = _mm(h1, w["forget"], mode="nn", name="proj_forget", **big)
    c_wide, c_row = _fox_prep(fl, vec["b_forget"], batch, seq)
    (o_fox, lse_wide, o_sb, rt_wide), gathered = _att_fwd(qkv, c_wide, c_row, batch, seq, rest)
    w = dict(w, **_rest_weights(dict(zip(EARLY + ("b_gate",), gathered))))
    merged, of, os_ = _gate_fwd(gl, w["b_gate"], o_fox, o_sb, w["branch_fox"], w["branch_sb"])
    x1, h2 = _mm_res_norm(merged, w["out"], x, vec["g_mlp"], "proj_out_norm")
    ar = _mm(h2, w["up"], mode="nn", name="mlp_up", out_dtype=BF16, epi=lambda acc, _: jnp.maximum(acc, 0.0),
             col_shards=True, **big)
    x2, h3 = _mm_res_norm(ar, w["down"], x1, vec["g_ple"], "mlp_down_norm", a_fn=_relu2)

    dx3, dpre, dpe, dg_final, loss = _head_and_loss(x2, h3, p, w["ple_gate"], w["ple"], vec["g_final"], target)
    gw = {}
    gw["ple"] = _mm(p, dpe, mode="tn", name="d_w_ple", col_shards=True, **big)
    gw["ple_gate"] = _mm(dpre, h3, mode="tn", name="d_w_ple_gate", t_out=True, **big)
    dx2, dx2b, dg_ple = _mm_norm_bwd([(dpre, w["ple_gate"])], None, x2, vec["g_ple"], dx3, "d_h_ple_norm_bwd")
    gw["down"] = _mm(ar, dx2b, mode="tn", name="d_w_down", a_fn=_relu2, **big)
    da = _mm(dx2b, w["down"], mode="nt", name="d_act", out_dtype=BF16,
             epi=lambda acc, r: acc * (2.0 * r.astype(F32)), extra=ar, **big)
    gw["up"] = _mm(h2, da, mode="tn", name="d_w_up", col_shards=True, **big)
    dx1, dx1b, dg_mlp = _mm_norm_bwd([(da, w["up"])], None, x1, vec["g_mlp"], dx2, "d_h_mlp_norm_bwd")
    gw["out"] = _mm(merged, dx1b, mode="tn", name="d_w_out", **big)
    dof, dos, dgl, gw["b_gate"], do_fox, do_sb = _gate_bwd(gl, w["b_gate"], of, os_, dx1b, w["out"], w["branch_fox"],
                                                                  w["branch_sb"])
    gw["branch_fox"] = _mm(o_fox, dof, mode="tn", name="d_w_branch_fox", col_shards=True, **big)
    gw["branch_sb"] = _mm(o_sb, dos, mode="tn", name="d_w_branch_sb", col_shards=True, **big)
    early = _early_slots(gw)
    early = [early[n] for n in EARLY]
    (dq_a, dk_a, dv_a, dcs_wide, drs_wide), received = _fox_bwd(qkv, c_wide, c_row, o_fox, do_fox, lse_wide, batch, seq,
                                                                _swap_halves(early))
    sums = _sum_sibling(place, early, received, "sum_sibling_early")
    (dq_b, dk_b, dv_b), others = _sb_bwd(qkv, do_sb, rt_wide, batch, seq, _exchange_chips(sums))
    mine = [None] * len(EARLY)
    for group, tag in ((BIG, "big"), (SMALL, "small")):
        for t, res in zip(group, _sum_chips(place, *[[a[t] for t in group] for a in (early, received, others)], "sum_chips_" + tag)):
            mine[t] = res
    dfl, db_forget = _fox_post(dcs_wide, drs_wide, fl, vec["b_forget"], batch, seq)
    dqkv = jnp.concatenate([dq_a, dk_a, dv_a, dq_b, dk_b, dv_b], axis=1)
    gw["qkv"], theirs = _mm(dqkv, h1, mode="tn", name="d_w_qkv", behind=_share_halves(mine), flat_out=True, **big)
    reduced = dict(zip(EARLY, zip(mine, theirs)))
    gw["gate"] = _mm(dgl, h1, mode="tn", name="d_w_gate", flat_out=True, **big)
    gw["forget"] = _mm(dfl, h1, mode="tn", name="d_w_forget", flat_out=True, **big)
    late = [_w_in_slots(gw)]
    dh1, received = _mm(dqkv, w["qkv"], mode="nt", name="d_h_qkv", behind=_swap_halves(late), **big)
    sums = _sum_sibling(place, late, received, "sum_sibling_w_in")
    (grad_x, _, dg_mix), others = _mm_norm_bwd([(dgl, w["gate"]), (dfl, w["forget"])], dh1, x, vec["g_mix"], dx1,
                                               "d_h_gate_norm_bwd", behind=_exchange_chips(sums))
    mine = _sum_chips(place, late, received, others, "sum_chips_w_in")
    reduced["w_in"] = (mine[0], _run_exchange(_share_halves(mine), "reduce_share_w_in")[0])
    gvec = {"g_mix": dg_mix, "b_forget": db_forget[:, 0:N_HEADS], "g_mlp": dg_mlp, "g_ple": dg_ple,
            "g_final": dg_final, "b_gate": gw["b_gate"]}
    return loss, grad_x.reshape(batch, seq, D_MODEL), reduced, gvec


ANY = pl.BlockSpec(memory_space=pl.ANY)
SHARDED = ("w_in", "w_branch_fox", "w_branch_sb", "w_out", "w_up", "w_down", "w_ple_gate", "w_ple")
ROW_ALIGN = 16
F32_ROWS = 8


def _place():
    return lax.axis_index("x"), lax.axis_index("y"), lax.axis_index("c")


def _other_chips(x, y):
    return [(1 - x, y), (x, 1 - y), (1 - x, 1 - y)]


def _half(ref, h):
    r = ref.shape[0] // 2
    assert r % ROW_ALIGN == 0
    return ref.at[pl.ds(pl.multiple_of(h * r, ROW_ALIGN), r)]


def _remote(src, dst, sems, idx, to):
    send_sems, recv_sems = sems
    return pltpu.make_async_remote_copy(src_ref=src, dst_ref=dst, send_sem=send_sems.at[idx], recv_sem=recv_sems.at[idx],
                                        device_id=to, device_id_type=MESH)


class _Exchange:
    def __init__(self, operands, out_shapes, sem_shape, start, finish):
        self.operands, self.out_shapes, self.sem_shape, self.start, self.finish = operands, out_shapes, sem_shape, start, finish

    def scratch(self):
        return [pltpu.SemaphoreType.DMA(self.sem_shape), pltpu.SemaphoreType.DMA(self.sem_shape)]


def _run_exchange(ex, name):
    n = len(ex.operands)

    def body(*refs):
        ex.start(refs[:n], refs[n:2 * n], refs[2 * n:])
        ex.finish(refs[:n], refs[n:2 * n], refs[2 * n:])

    return pl.pallas_call(body, name=name, in_specs=[ANY] * n, out_specs=[ANY] * n, out_shape=ex.out_shapes,
                          scratch_shapes=ex.scratch())(*ex.operands)


def _call_behind(body, ex, *, name, grid, in_specs, out_specs, out_shape, scratch_shapes, operands):
    n_in, n_out, nx = len(in_specs), len(out_specs), len(ex.operands)

    def wrapped(*refs):
        ins, x_in = refs[:n_in], refs[n_in:n_in + nx]
        outs, x_out = refs[n_in + nx:n_in + nx + n_out], refs[n_in + nx + n_out:n_in + 2 * nx + n_out]
        scratch, sems = refs[n_in + 2 * nx + n_out:-2], refs[-2:]
        first, last = None, None
        for d, steps in enumerate(grid):
            at_start, at_end = pl.program_id(d) == 0, pl.program_id(d) == steps - 1
            first = at_start if first is None else first & at_start
            last = at_end if last is None else last & at_end

        @pl.when(first)
        def _():
            ex.start(x_in, x_out, sems)

        body(*ins, *outs, *scratch)

        @pl.when(last)
        def _():
            ex.finish(x_in, x_out, sems)

    res = pl.pallas_call(
        wrapped, name=name, grid=grid, in_specs=list(in_specs) + [ANY] * nx, out_specs=list(out_specs) + [ANY] * nx,
        out_shape=list(out_shape) + list(ex.out_shapes), scratch_shapes=list(scratch_shapes) + ex.scratch(),
        compiler_params=_cparams(("arbitrary",) * len(grid)),
    )(*operands, *ex.operands)
    return res[:n_out], res[n_out:]


def _gather_weights(shards):
    n = len(shards)

    def first_copies(src, out, sems):
        x, y, c = _place()
        me = 2 * x + y
        copies = [_remote(_half(src[t], c), _half(out[t].at[me], c), sems, (t, k), (px, py, c))
                  for t in range(n) for k, (px, py) in enumerate(_other_chips(x, y))]
        return copies + [_remote(src[t], out[t].at[me], sems, (t, 3), (x, y, 1 - c)) for t in range(n)]

    def start(src, out, sems):
        for cp in first_copies(src, out, sems):
            cp.start()

    def finish(src, out, sems):
        x, y, c = _place()
        me = 2 * x + y
        sibling = (x, y, 1 - c)
        chips = _other_chips(x, y)
        passes = []
        for t in range(n):
            for k, (px, py) in enumerate(chips):
                landed = _half(out[t].at[2 * px + py], c)
                _remote(landed, landed, sems, (t, k), (px, py, c)).wait_recv()
                passes.append(_remote(landed, landed, sems, (t, 4 + k), sibling))
                passes[-1].start()
        for t in range(n):
            _remote(src[t], out[t].at[me], sems, (t, 3), sibling).wait_recv()
            for k, (px, py) in enumerate(chips):
                passed = _half(out[t].at[2 * px + py], 1 - c)
                _remote(passed, passed, sems, (t, 4 + k), sibling).wait_recv()
        for cp in first_copies(src, out, sems) + passes:
            cp.wait_send()

    return _Exchange(shards, [jax.ShapeDtypeStruct((N_CHIPS,) + s.shape, s.dtype) for s in shards], (n, 7), start, finish)


def _simple_exchange(operands, out_shapes, copies):
    def start(src, out, sems):
        for cp in copies(src, out, sems):
            cp.start()

    def finish(src, out, sems):
        for cp in copies(src, out, sems):
            cp.wait_recv()
        for cp in copies(src, out, sems):
            cp.wait_send()

    return _Exchange(operands, out_shapes, (len(operands),), start, finish)


def _swap_halves(slots):
    def copies(src, out, sems):
        x, y, c = _place()
        res = []
        for t in range(len(slots)):
            r = src[t].shape[1] // 2
            rows = pl.ds(pl.multiple_of((1 - c) * r, F32_ROWS), r)
            res.append(_remote(src[t].at[:, rows], out[t], sems, t, (x, y, 1 - c)))
        return res

    return _simple_exchange(slots, [jax.ShapeDtypeStruct((N_CHIPS, s.shape[1] // 2, s.shape[2]), s.dtype) for s in slots], copies)


def _exchange_chips(sums):
    n = len(sums)

    def copies(src, out, sems):
        x, y, c = _place()
        return [_remote(src[t].at[2 * px + py], out[t].at[k], sems, (t, k), (px, py, c))
                for t in range(n) for k, (px, py) in enumerate(_other_chips(x, y))]

    def start(src, out, sems):
        for cp in copies(src, out, sems):
            cp.start()

    def finish(src, out, sems):
        for cp in copies(src, out, sems):
            cp.wait_recv()
        for cp in copies(src, out, sems):
            cp.wait_send()

    return _Exchange(sums, [jax.ShapeDtypeStruct((3,) + s.shape[1:], s.dtype) for s in sums], (n, 3), start, finish)


def _share_halves(mine):
    def copies(src, out, sems):
        x, y, c = _place()
        return [_remote(src[t], out[t], sems, t, (x, y, 1 - c)) for t in range(len(mine))]

    return _simple_exchange(mine, [jax.ShapeDtypeStruct(s.shape, s.dtype) for s in mine], copies)


def _walk(name, place, parts):
    starts = [sum(p[0] for p in parts[:t]) for t in range(len(parts))]
    held = lambda index, start, steps: (lambda s, pr: index(jnp.clip(s - start, 0, steps - 1), pr))
    in_specs, out_specs, out_shapes, operands = [], [], [], []
    for (steps, ins, outs, shapes, ops, _), start in zip(parts, starts):
        in_specs += [pl.BlockSpec(blk, held(index, start, steps)) for blk, index in ins]
        out_specs += [pl.BlockSpec(blk, held(index, start, steps)) for blk, index in outs]
        out_shapes += list(shapes)
        operands += list(ops)

    def body(place_ref, *refs):
        s = pl.program_id(0)
        i, o = 0, len(in_specs)
        for (steps, ins, outs, _, _, fn), start in zip(parts, starts):
            mine_in, mine_out = refs[i:i + len(ins)], refs[o:o + len(outs)]
            i, o = i + len(ins), o + len(outs)

            @pl.when((s >= start) & (s < start + steps))
            def _(mine_in=mine_in, mine_out=mine_out, start=start, fn=fn):
                fn(s - start, mine_in, mine_out)

    res = pl.pallas_call(
        body, name=name, out_shape=out_shapes,
        grid_spec=pltpu.PrefetchScalarGridSpec(num_scalar_prefetch=1, grid=(sum(p[0] for p in parts),), in_specs=in_specs,
                                               out_specs=out_specs),
        compiler_params=_cparams(("arbitrary",)),
    )(place, *operands)
    counts = [len(p[2]) for p in parts]
    return [res[sum(counts[:t]):sum(counts[:t + 1])] for t in range(len(parts))]


def _sum_sibling(place, slots, received, name):
    def part(slot, got):
        n, rows, cols = got.shape
        block = (None, rows, cols)

        def fn(j, ins, outs):
            outs[0][...] = (ins[0][...] + ins[1][...]).astype(BF16)

        return (n, [(block, lambda j, pr: (j, pr[1], 0)), (block, lambda j, pr: (j, 0, 0))], [(block, lambda j, pr: (j, 0, 0))],
                [jax.ShapeDtypeStruct(got.shape, BF16)], [slot, got], fn)

    return [r[0] for r in _walk(name, place, [part(s, g) for s, g in zip(slots, received)])]


def _sum_chips(place, slots, received, others, name):
    def part(slot, got, other):
        _, rows, cols = got.shape
        block = (None, rows, cols)

        def fn(_, ins, outs):
            own = ins[0][...] + ins[1][...]
            outs[0][...] = ((own + ins[2][0].astype(F32)) + ins[2][1].astype(F32)) + ins[2][2].astype(F32)

        return (1, [(block, lambda _, pr: (pr[0], pr[1], 0)), (block, lambda _, pr: (pr[0], 0, 0)),
                    ((3, rows, cols), lambda _, pr: (0, 0, 0))], [((rows, cols), lambda _, pr: (0, 0))],
                [jax.ShapeDtypeStruct((rows, cols), F32)], [slot, got, other], fn)

    return [r[0] for r in _walk(name, place, [part(*t) for t in zip(slots, received, others)])]


N_DEVICES = 8


def _sum_devices(block, name):
    def body(v_ref, o_ref, land_ref, send_sems, recv_sems):
        x, y, c = _place()
        me = 4 * x + 2 * y + c
        copies = []
        for mask in range(1, N_DEVICES):
            peer = (x ^ (mask >> 2), y ^ ((mask >> 1) & 1), c ^ (mask & 1))
            copies.append(pltpu.make_async_remote_copy(src_ref=v_ref, dst_ref=land_ref.at[me], send_sem=send_sems.at[mask - 1],
                                                       recv_sem=recv_sems.at[mask - 1], device_id=peer, device_id_type=MESH))
        for cp in copies:
            cp.start()
        land_ref[me] = v_ref[...]
        for cp in copies:
            cp.wait_recv()
        total = land_ref[0]
        for d in range(1, N_DEVICES):
            total = total + land_ref[d]
        o_ref[...] = total
        for cp in copies:
            cp.wait_send()

    vmem = pl.BlockSpec(memory_space=pltpu.VMEM)
    return pl.pallas_call(
        body, name=name, in_specs=[vmem], out_specs=vmem, out_shape=jax.ShapeDtypeStruct(block.shape, F32),
        scratch_shapes=[pltpu.VMEM((N_DEVICES,) + block.shape, F32), pltpu.SemaphoreType.DMA((N_DEVICES - 1,)),
                        pltpu.SemaphoreType.DMA((N_DEVICES - 1,))],
    )(block)


def _vec_block(g_mix, g_mlp, g_ple, g_final, b_forget, b_gate_rows, last=None):
    pad = lambda a: jnp.concatenate([a, jnp.zeros((a.shape[0], D_MODEL - a.shape[1]), F32)], axis=1)
    last = jnp.zeros((1, 0), F32) if last is None else last
    return jnp.concatenate([g_mix, g_mlp, g_ple, g_final.reshape(1, D_MODEL), pad(b_forget), pad(b_gate_rows), pad(last)],
                           axis=0)


def _adam_math(w, g, m, v):
    m_new = ADAM_B1 * m + (1.0 - ADAM_B1) * g
    v_new = ADAM_B2 * v + (1.0 - ADAM_B2) * (g * g)
    m_hat = m_new / (1.0 - ADAM_B1 ** ADAM_STEP)
    v_hat = v_new / (1.0 - ADAM_B2 ** ADAM_STEP)
    return -ADAM_LR * (m_hat / (jnp.sqrt(v_hat) + ADAM_EPS) + ADAM_WD * w), m_new, v_new


def _adamw_halves(place, weights, name):
    def part(w, m, v, g_mine, g_theirs):
        rows, cols = g_mine.shape
        whole = ((rows, cols), lambda s, pr: (pr[1] + s - 2 * pr[1] * s, 0))
        half = ((rows, cols), lambda s, pr: (0, 0))

        def fn(s, ins, outs):
            g = jnp.where(s == 0, ins[3][...], ins[4][...])
            outs[0][...] = g
            outs[1][...], outs[2][...], outs[3][...] = _adam_math(ins[0][...], g, ins[1][...], ins[2][...])

        return (2, [whole] * 3 + [half] * 2, [whole] * 4, [jax.ShapeDtypeStruct(w.shape, F32)] * 4, [w, m, v, g_mine, g_theirs], fn)

    return _walk(name, place, [part(*t) for t in weights])


def _adamw_vec(w, g, m, v):
    def body(w_ref, g_ref, m_ref, v_ref, d_ref, nm_ref, nv_ref):
        d_ref[...], nm_ref[...], nv_ref[...] = _adam_math(w_ref[...], g_ref[...], m_ref[...], v_ref[...])

    return pl.pallas_call(body, name="adamw_vectors", out_shape=[jax.ShapeDtypeStruct(w.shape, F32)] * 3)(w, g, m, v)


WEIGHT_NAMES = ("g_mix", "w_in", "b_forget", "b_gate", "w_branch_fox", "w_branch_sb", "w_out", "g_mlp", "w_up", "w_down",
                "g_ple", "w_ple_gate", "w_ple", "g_final")
W_IN_SHARD = D_IN // N_CHIPS
Q_END, F_END, B_END = 3 * D_ATT, 3 * D_ATT + N_HEADS, 6 * D_ATT + N_HEADS
GATE_SHARD = D_MODEL // N_CHIPS


LATE = SHARDED[:1]
EARLY = SHARDED[1:]
BIG = tuple(t for t, n in enumerate(EARLY) if n in ("w_up", "w_down"))
SMALL = tuple(t for t in range(len(EARLY)) if t not in BIG)


def _first_weights(w_in_slots):
    def cols(*ranges):
        parts = []
        for lo, hi in ranges:
            for j in range(N_CHIPS):
                a, b = max(lo, j * W_IN_SHARD), min(hi, (j + 1) * W_IN_SHARD)
                if a < b:
                    parts.append(w_in_slots[j, :, a - j * W_IN_SHARD:b - j * W_IN_SHARD])
        return parts

    forget = jnp.concatenate(cols((Q_END, F_END)) + [jnp.zeros((D_MODEL, F_PAD - N_HEADS), BF16)], axis=1)
    return {"qkv": jnp.concatenate(cols((0, Q_END), (F_END, B_END)), axis=1), "gate": jnp.concatenate(cols((B_END, D_IN)), axis=1),
            "forget": forget}


GATE_ROWS = 2 * ROW_ALIGN


def _gate_bits(b_gate):
    bits = lax.bitcast_convert_type(b_gate, BF16).reshape(2, 2 * GATE_SHARD)
    return jnp.concatenate([bits, jnp.zeros((GATE_ROWS - 2, 2 * GATE_SHARD), BF16)], axis=0)


def _rest_weights(gathered):
    rows = lambda a: a.reshape(N_CHIPS * a.shape[1], a.shape[2])
    bits = gathered["b_gate"][:, :2].reshape(N_CHIPS, 2, GATE_SHARD, 2)
    b_gate = jnp.transpose(lax.bitcast_convert_type(bits, F32), (1, 0, 2)).reshape(2, D_MODEL)
    return {"branch_fox": gathered["w_branch_fox"], "branch_sb": gathered["w_branch_sb"], "out": rows(gathered["w_out"]),
            "up": gathered["w_up"], "down": rows(gathered["w_down"]), "ple_gate": rows(gathered["w_ple_gate"]),
            "ple": gathered["w_ple"], "b_gate": b_gate}


def _early_slots(gw):
    rows = lambda a: a.reshape(N_CHIPS, a.shape[0] // N_CHIPS, a.shape[1])
    return {"w_branch_fox": gw["branch_fox"], "w_branch_sb": gw["branch_sb"], "w_out": rows(gw["out"]), "w_up": gw["up"],
            "w_down": rows(gw["down"]), "w_ple_gate": rows(gw["ple_gate"]), "w_ple": gw["ple"]}


W_IN_FLAT = (W_IN_SHARD * D_MODEL // LANES, LANES)


def _w_in_slots(gw):
    c = D_MODEL // LANES
    g_t = jnp.concatenate([gw["qkv"][:Q_END * c], gw["forget"][:N_HEADS * c], gw["qkv"][Q_END * c:], gw["gate"]], axis=0)
    return g_t.reshape((N_CHIPS,) + W_IN_FLAT)


def _flat(a):
    return jnp.transpose(a, (2, 0, 1)).reshape(W_IN_FLAT)


def _unflat(a):
    return jnp.transpose(a.reshape(W_IN_SHARD, D_MODEL // LANES, LANES), (1, 2, 0)).reshape(1, D_MODEL, W_IN_SHARD)


def kernel(x, p, g_mix, w_in, b_forget, b_gate, w_branch_fox, w_branch_sb, w_out, g_mlp, w_up, w_down, g_ple, w_ple_gate, w_ple, g_final, loss_target, m_g_mix, m_w_in, m_b_forget, m_b_gate, m_w_branch_fox, m_w_branch_sb, m_w_out, m_g_mlp, m_w_up, m_w_down, m_g_ple, m_w_ple_gate, m_w_ple, m_g_final, v_g_mix, v_w_in, v_b_forget, v_b_gate, v_w_branch_fox, v_w_branch_sb, v_w_out, v_g_mlp, v_w_up, v_w_down, v_g_ple, v_w_ple_gate, v_w_ple, v_g_final):
    weights = dict(g_mix=g_mix, w_in=w_in, b_forget=b_forget, b_gate=b_gate, w_branch_fox=w_branch_fox,
                   w_branch_sb=w_branch_sb, w_out=w_out, g_mlp=g_mlp, w_up=w_up, w_down=w_down, g_ple=g_ple,
                   w_ple_gate=w_ple_gate, w_ple=w_ple, g_final=g_final)
    first = dict(g_mix=m_g_mix, w_in=m_w_in, b_forget=m_b_forget, b_gate=m_b_gate, w_branch_fox=m_w_branch_fox,
                 w_branch_sb=m_w_branch_sb, w_out=m_w_out, g_mlp=m_g_mlp, w_up=m_w_up, w_down=m_w_down, g_ple=m_g_ple,
                 w_ple_gate=m_w_ple_gate, w_ple=m_w_ple, g_final=m_g_final)
    second = dict(g_mix=v_g_mix, w_in=v_w_in, b_forget=v_b_forget, b_gate=v_b_gate, w_branch_fox=v_w_branch_fox,
                  w_branch_sb=v_w_branch_sb, w_out=v_w_out, g_mlp=v_g_mlp, w_up=v_w_up, w_down=v_w_down, g_ple=v_g_ple,
                  w_ple_gate=v_w_ple_gate, w_ple=v_w_ple, g_final=v_g_final)
    cx, cy, cc = _place()
    chip = 2 * cx + cy
    place = jnp.stack([chip, cc]).astype(jnp.int32)
    col0 = chip * GATE_SHARD

    first_gather = _gather_weights([weights[n][0].astype(BF16) for n in LATE])
    rest = _gather_weights([weights[n][0].astype(BF16) for n in EARLY] + [_gate_bits(b_gate[0])])
    vec = {"g_mix": g_mix, "b_forget": jnp.concatenate([b_forget, jnp.zeros((1, F_PAD - N_HEADS), F32)], axis=1),
           "g_mlp": g_mlp, "g_ple": g_ple, "g_final": g_final.reshape(1, D_MODEL)}

    loss, grad_x, reduced, gvec = _local_step(x, p[0], loss_target, first_gather, rest, vec, place)

    out = {}
    args = lambda n: (weights[n][0], first[n][0], second[n][0]) + tuple(reduced[n])
    for names, tag in [([EARLY[t] for t in SMALL], "small")] + [([EARLY[t]], EARLY[t]) for t in BIG]:
        for n, res in zip(names, _adamw_halves(place, [args(n) for n in names], "adamw_" + tag)):
            out[n] = [r[None] for r in res]
    (res,) = _adamw_halves(place, [(_flat(w_in), _flat(m_w_in), _flat(v_w_in)) + tuple(reduced["w_in"])], "adamw_w_in")
    out["w_in"] = [_unflat(r) for r in res]

    g_block = _sum_devices(_vec_block(gvec["g_mix"], gvec["g_mlp"], gvec["g_ple"], gvec["g_final"][0], gvec["b_forget"],
                                      gvec["b_gate"], loss), "reduce_vectors")
    loss = g_block[7, 0]
    g_gate = lax.dynamic_slice(g_block[5:7], (0, col0), (2, GATE_SHARD))
    blocks = [_vec_block(d["g_mix"], d["g_mlp"], d["g_ple"], d["g_final"], d["b_forget"], d["b_gate"][0])
              for d in (weights, first, second)]
    g_rows = jnp.concatenate([g_block[0:5], jnp.concatenate([g_gate, jnp.zeros((2, D_MODEL - GATE_SHARD), F32)], axis=1),
                              jnp.zeros((1, D_MODEL), F32)], axis=0)
    res = (g_rows,) + tuple(_adamw_vec(blocks[0], g_rows, blocks[1], blocks[2]))
    out["g_mix"] = [r[0:1] for r in res]
    out["g_mlp"] = [r[1:2] for r in res]
    out["g_ple"] = [r[2:3] for r in res]
    out["g_final"] = [r[3] for r in res]
    out["b_forget"] = [r[4:5, :N_HEADS] for r in res]
    out["b_gate"] = [r[5:7, :GATE_SHARD][None] for r in res]
    return (loss, grad_x, *[out[n][0] for n in WEIGHT_NAMES], *[out[n][1] for n in WEIGHT_NAMES],
            *[out[n][2] for n in WEIGHT_NAMES], *[out[n][3] for n in WEIGHT_NAMES])
```

```python
import jax
import jax.numpy as jnp
from jax import lax
from jax.experimental import pallas as pl
from jax.experimental.pallas import tpu as pltpu

F32 = jnp.float32
BF16 = jnp.bfloat16

D_MODEL = 1024
HEAD_DIM = 64
N_HEADS = 8
D_ATT = N_HEADS * HEAD_DIM
D_PLE = 256
D_IN = 6 * D_ATT + N_HEADS + 2 * D_MODEL
F_PAD = 128
EPS = 1e-6
SCALE = HEAD_DIM ** -0.5
N_CHIPS = 4
LANES = 128
ATT_BLOCK = 256
FOX_TILES = (512, 512)
SB_TILES = (512, 256)
NEG = -1e30

ADAM_LR = 0.001
ADAM_B1 = 0.9
ADAM_B2 = 0.999
ADAM_EPS = 1e-08
ADAM_WD = 0.01
ADAM_STEP = 10

VMEM_LIMIT = 56 * 1024 * 1024

MESH = pl.DeviceIdType.MESH


def _cparams(sem=None):
    return pltpu.CompilerParams(dimension_semantics=sem, vmem_limit_bytes=VMEM_LIMIT)


def _relu2(t):
    t = t.astype(F32)
    return t * t


_DIMS = {"nn": (((1,), (0,)), ((), ())), "nt": (((1,), (1,)), ((), ())), "tn": (((0,), (0,)), ((), ()))}
NT_DIMS = _DIMS["nt"]
TN_DIMS = _DIMS["tn"]


def _mm(a, b, *, mode, name, out_dtype=F32, tm=512, tn=512, tk=512, add=None, a_fn=None, epi=None, extra=None,
        col_shards=False, behind=None, flat_out=False):
    if mode == "nn":
        (m, k), n = a.shape, b.shape[-1]
    elif mode == "nt":
        (m, k), n = a.shape, b.shape[-2]
    else:
        (k, m), n = a.shape, b.shape[1]
    shard = None
    if col_shards:
        if mode == "nn":
            shard, n = n, N_CHIPS * n
            tn = min(tn, shard)
        elif mode == "nt":
            shard = b.shape[-1]
            tk = min(tk, shard)
        else:
            shard = n // N_CHIPS
            if tn < n:
                tn = min(tn, shard)
    tm, tn, tk = min(tm, m), min(tn, n), min(tk, k)
    assert m % tm == 0 and n % tn == 0 and k % tk == 0, (name, m, n, k)
    nk = k // tk
    all_shards = col_shards and mode == "tn" and tn == n
    a_spec = {"nn": pl.BlockSpec((tm, tk), lambda i, j, kk: (i, kk)),
              "nt": pl.BlockSpec((tm, tk), lambda i, j, kk: (i, kk)),
              "tn": pl.BlockSpec((tk, tm), lambda i, j, kk: (kk, i))}[mode]
    b_spec = {"nn": pl.BlockSpec((tk, tn), lambda i, j, kk: (kk, j)),
              "nt": pl.BlockSpec((tn, tk), lambda i, j, kk: (j, kk)),
              "tn": pl.BlockSpec((tk, tn), lambda i, j, kk: (kk, j))}[mode]
    o_spec = pl.BlockSpec((tm, tn), lambda i, j, kk: (i, j))
    out_shape = (m, n)
    if col_shards and mode == "nn":
        per = shard // tn
        b_spec = pl.BlockSpec((None, tk, tn), lambda i, j, kk: (j // per, kk, j % per))
    elif col_shards and mode == "nt":
        per = shard // tk
        b_spec = pl.BlockSpec((None, tn, tk), lambda i, j, kk: (kk // per, j, kk % per))
    elif col_shards:
        assert add is None and extra is None
        if all_shards:
            o_spec = pl.BlockSpec((N_CHIPS, tm, shard), lambda i, j, kk: (0, i, 0))
        else:
            per = shard // tn
            o_spec = pl.BlockSpec((None, tm, tn), lambda i, j, kk: (j // per, i, j % per))
        out_shape = (N_CHIPS, m, shard)
    if flat_out:
        assert mode == "tn" and tn == n == D_MODEL and not col_shards and add is None and extra is None
        chunks = D_MODEL // LANES
        o_spec = pl.BlockSpec((tm * chunks, LANES), lambda i, j, kk: (i, 0))
        out_shape = (m * chunks, LANES)
    operands, in_specs = [a, b], [a_spec, b_spec]
    third = add if add is not None else extra
    if third is not None:
        operands.append(third)
        in_specs.append(o_spec)

    def body(*refs):
        a_ref, b_ref = refs[0], refs[1]
        t_ref = refs[2] if third is not None else None
        o_ref = refs[3] if third is not None else refs[2]
        acc_ref = refs[-1] if nk > 1 else None
        at = a_ref[...]
        if a_fn is not None:
            at = a_fn(at)
        part = lax.dot_general(at.astype(BF16), b_ref[...].astype(BF16), _DIMS[mode], preferred_element_type=F32)

        def finish(acc):
            if epi is not None:
                acc = epi(acc, None if t_ref is None else t_ref[...])
            elif add is not None:
                acc = acc + t_ref[...].astype(F32)
            if flat_out:
                for q in range(D_MODEL // LANES):
                    o_ref[pl.ds(q, tm, stride=D_MODEL // LANES), :] = acc[:, q * LANES:(q + 1) * LANES].astype(o_ref.dtype)
                return
            if all_shards:
                for slot in range(N_CHIPS):
                    o_ref[slot] = acc[:, slot * shard:(slot + 1) * shard].astype(o_ref.dtype)
                return
            o_ref[...] = acc.astype(o_ref.dtype)

        if nk == 1:
            finish(part)
        else:
            kk = pl.program_id(2)

            @pl.when(kk == 0)
            def _():
                acc_ref[...] = part

            @pl.when(kk > 0)
            def _():
                acc_ref[...] += part

            @pl.when(kk == nk - 1)
            def _():
                finish(acc_ref[...])

    call = dict(name=name, grid=(m // tm, n // tn, nk), in_specs=in_specs,
                scratch_shapes=[pltpu.VMEM((tm, tn), F32)] if nk > 1 else [])
    if behind is not None:
        (res,), exchanged = _call_behind(body, behind, out_specs=[o_spec], out_shape=[jax.ShapeDtypeStruct(out_shape, out_dtype)],
                                         operands=operands, **call)
        return res, exchanged
    return pl.pallas_call(body, out_specs=o_spec, out_shape=jax.ShapeDtypeStruct(out_shape, out_dtype),
                          compiler_params=_cparams(("parallel", "parallel", "arbitrary")), **call)(*operands)


ROW_TILE = 512


def _row_spec(width=D_MODEL, rows=ROW_TILE):
    return pl.BlockSpec((rows, width), lambda i: (i, 0))


def _vec_spec(rows=1, width=D_MODEL):
    return pl.BlockSpec((rows, width), lambda i: (0, 0))


def _xhat(x):
    r = lax.rsqrt(jnp.mean(x * x, axis=-1, keepdims=True) + EPS)
    return x * r, r


def _rms_bwd_rows(dh, x, g):
    xh, r = _xhat(x)
    dxh = dh * g
    dx = r * (dxh - xh * jnp.mean(dxh * xh, axis=-1, keepdims=True))
    return dx, jnp.sum(dh * xh, axis=0, keepdims=True)


def _norm_fwd(x, g, name, behind):
    t = x.shape[0]

    def body(x_ref, g_ref, h_ref):
        xh, _ = _xhat(x_ref[...])
        h_ref[...] = (xh * g_ref[...]).astype(BF16)

    return _call_behind(body, behind, name=name, grid=(t // ROW_TILE,), in_specs=[_row_spec(), _vec_spec()],
                        out_specs=[_row_spec()], out_shape=[jax.ShapeDtypeStruct((t, D_MODEL), BF16)], scratch_shapes=[],
                        operands=(x, g))


def _mm_res_norm(a, b, res, g, name, a_fn=None):
    t, k = a.shape

    def body(a_ref, b_ref, res_ref, g_ref, x_ref, h_ref):
        at = a_ref[...] if a_fn is None else a_fn(a_ref[...])
        x_new = res_ref[...] + _dot(at.astype(BF16), b_ref[...])
        x_ref[...] = x_new
        h_ref[...] = (_xhat(x_new)[0] * g_ref[...]).astype(BF16)

    return pl.pallas_call(
        body, name=name, grid=(t // ROW_TILE,),
        in_specs=[pl.BlockSpec((ROW_TILE, k), lambda i: (i, 0)), pl.BlockSpec(b.shape, lambda i: (0, 0)), _row_spec(), _vec_spec()],
        out_specs=[_row_spec(), _row_spec()],
        out_shape=[jax.ShapeDtypeStruct((t, D_MODEL), F32), jax.ShapeDtypeStruct((t, D_MODEL), BF16)],
        compiler_params=_cparams(("parallel",)),
    )(a, b, res, g)


def _mm_norm_bwd(pairs, dh_first, x, g, dres, name, behind=None):
    t = x.shape[0]
    operands, in_specs = [], []
    for a, b in pairs:
        if b.ndim == 3:
            for j in range(b.shape[0]):
                operands += [a, b]
                in_specs += [pl.BlockSpec((ROW_TILE, b.shape[2]), lambda i, j=j: (i, j)),
                             pl.BlockSpec((None, D_MODEL, b.shape[2]), lambda i, j=j: (j, 0, 0))]
        else:
            operands += [a, b]
            in_specs += [pl.BlockSpec((ROW_TILE, a.shape[1]), lambda i: (i, 0)), pl.BlockSpec(b.shape, lambda i: (0, 0))]
    n_mm = len(operands)
    operands += [x, g, dres] + ([] if dh_first is None else [dh_first])
    in_specs += [_row_spec(), _vec_spec(), _row_spec()] + ([] if dh_first is None else [_row_spec()])

    def body(*refs):
        x_ref, g_ref, dres_ref = refs[n_mm:n_mm + 3]
        dx_ref, dxb_ref, dg_ref = refs[-3:]
        dh = 0.0 if dh_first is None else refs[n_mm + 3][...]
        for k in range(0, n_mm, 2):
            dh = dh + lax.dot_general(refs[k][...].astype(BF16), refs[k + 1][...].astype(BF16), NT_DIMS,
                                      preferred_element_type=F32)
        dx, dg = _rms_bwd_rows(dh, x_ref[...], g_ref[...])
        dx = dx + dres_ref[...]
        dx_ref[...] = dx
        dxb_ref[...] = dx.astype(BF16)

        @pl.when(pl.program_id(0) == 0)
        def _():
            dg_ref[...] = jnp.zeros_like(dg_ref)

        dg_ref[...] += dg

    call = dict(name=name, grid=(t // ROW_TILE,), in_specs=in_specs, out_specs=[_row_spec(), _row_spec(), _vec_spec()],
                out_shape=[jax.ShapeDtypeStruct((t, D_MODEL), F32), jax.ShapeDtypeStruct((t, D_MODEL), BF16),
                           jax.ShapeDtypeStruct((1, D_MODEL), F32)])
    if behind is not None:
        return _call_behind(body, behind, scratch_shapes=[], operands=operands, **call)
    return pl.pallas_call(body, compiler_params=_cparams(("arbitrary",)), **call)(*operands)


def _shards_spec(w):
    return pl.BlockSpec(w.shape, lambda i: (0, 0, 0))


def _gate_fwd(gl, b_gate, o_fox, o_sb, w_fox, w_sb):
    t = o_fox.shape[0]

    def body(gla_ref, glb_ref, b_ref, ofox_ref, osb_ref, wf_ref, ws_ref, m_ref, of_ref, os_ref):
        of = jnp.concatenate([_dot(ofox_ref[...], wf_ref[j]) for j in range(N_CHIPS)], axis=1)
        os_ = jnp.concatenate([_dot(osb_ref[...], ws_ref[j]) for j in range(N_CHIPS)], axis=1)
        ga = jax.nn.sigmoid(gla_ref[...] + b_ref[0:1, :])
        gb = jax.nn.sigmoid(glb_ref[...] + b_ref[1:2, :])
        of_ref[...] = of
        os_ref[...] = os_
        m_ref[...] = (ga * of + gb * os_).astype(BF16)

    return pl.pallas_call(
        body, name="gate_fwd", grid=(t // ROW_TILE,),
        in_specs=[pl.BlockSpec((ROW_TILE, D_MODEL), lambda i: (i, 0)), pl.BlockSpec((ROW_TILE, D_MODEL), lambda i: (i, 1)),
                  _vec_spec(2), _row_spec(D_ATT), _row_spec(D_ATT), _shards_spec(w_fox), _shards_spec(w_sb)],
        out_specs=[_row_spec(), _row_spec(), _row_spec()],
        out_shape=[jax.ShapeDtypeStruct((t, D_MODEL), BF16)] + [jax.ShapeDtypeStruct((t, D_MODEL), F32)] * 2,
        compiler_params=_cparams(("parallel",)),
    )(gl, gl, b_gate, o_fox, o_sb, w_fox, w_sb)


def _gate_bwd(gl, b_gate, of, os_, dx, w_out, w_fox, w_sb):
    t = of.shape[0]
    shard = D_MODEL // N_CHIPS

    def back(d, w_ref):
        return sum(_dot(d[:, j * shard:(j + 1) * shard], w_ref[j], NT_DIMS) for j in range(N_CHIPS)).astype(BF16)

    def body(gla_ref, glb_ref, b_ref, of_ref, os_ref, dx_ref, w_ref, wf_ref, ws_ref,
             dof_ref, dos_ref, dgl_ref, db_ref, dofox_ref, dosb_ref):
        dm = _dot(dx_ref[...], w_ref[...], NT_DIMS)
        ga = jax.nn.sigmoid(gla_ref[...] + b_ref[0:1, :])
        gb = jax.nn.sigmoid(glb_ref[...] + b_ref[1:2, :])
        dof = (dm * ga).astype(BF16)
        dos = (dm * gb).astype(BF16)
        dof_ref[...] = dof
        dos_ref[...] = dos
        dofox_ref[...] = back(dof, wf_ref)
        dosb_ref[...] = back(dos, ws_ref)
        dgla = dm * of_ref[...] * ga * (1.0 - ga)
        dglb = dm * os_ref[...] * gb * (1.0 - gb)
        dgl_ref[:, 0:D_MODEL] = dgla.astype(BF16)
        dgl_ref[:, D_MODEL:2 * D_MODEL] = dglb.astype(BF16)

        @pl.when(pl.program_id(0) == 0)
        def _():
            db_ref[...] = jnp.zeros_like(db_ref)

        db_ref[0:1, :] += jnp.sum(dgla, axis=0, keepdims=True)
        db_ref[1:2, :] += jnp.sum(dglb, axis=0, keepdims=True)

    outs = pl.pallas_call(
        body, name="gate_bwd", grid=(t // ROW_TILE,),
        in_specs=[pl.BlockSpec((ROW_TILE, D_MODEL), lambda i: (i, 0)), pl.BlockSpec((ROW_TILE, D_MODEL), lambda i: (i, 1)),
                  _vec_spec(2), _row_spec(), _row_spec(), _row_spec(), pl.BlockSpec(w_out.shape, lambda i: (0, 0)),
                  _shards_spec(w_fox), _shards_spec(w_sb)],
        out_specs=[_row_spec(), _row_spec(), _row_spec(2 * D_MODEL), _vec_spec(2), _row_spec(D_ATT), _row_spec(D_ATT)],
        out_shape=[jax.ShapeDtypeStruct((t, D_MODEL), BF16)] * 2 + [jax.ShapeDtypeStruct((t, 2 * D_MODEL), BF16),
                                                                      jax.ShapeDtypeStruct((2, D_MODEL), F32)]
        + [jax.ShapeDtypeStruct((t, D_ATT), BF16)] * 2,
        compiler_params=_cparams(("arbitrary",)),
    )(gl, gl, b_gate, of, os_, dx, w_out, w_fox, w_sb)
    return outs


def _head_and_loss(x2, h3, p, w_gate, w_ple, g_final, target):
    t = x2.shape[0]

    def body(x2_ref, h3_ref, p_ref, wg_ref, wp_ref, g_ref, tgt_ref, dx3_ref, dpre_ref, dpe_ref, dg_ref, loss_ref):
        gp = jax.nn.sigmoid(_dot(h3_ref[...], wg_ref[...]))
        p_t = p_ref[...].astype(BF16)
        pe_t = jnp.concatenate([_dot(p_t, wp_ref[j]) for j in range(N_CHIPS)], axis=1)
        x3 = x2_ref[...] + gp * pe_t
        g = g_ref[...]
        xh, _ = _xhat(x3)
        err = xh * g - tgt_ref[...]
        dy = err * (1.0 / D_MODEL)
        dx3, dg = _rms_bwd_rows(dy, x3, g)
        dx3_ref[...] = dx3
        dpre_ref[...] = (dx3 * pe_t * gp * (1.0 - gp)).astype(BF16)
        dpe_ref[...] = (dx3 * gp).astype(BF16)

        @pl.when(pl.program_id(0) == 0)
        def _():
            dg_ref[...] = jnp.zeros_like(dg_ref)
            loss_ref[...] = jnp.zeros_like(loss_ref)

        dg_ref[...] += dg
        loss_ref[...] += 0.5 * jnp.sum(jnp.mean(err * err, axis=-1, keepdims=True), axis=0, keepdims=True)

    return pl.pallas_call(
        body, name="head_and_loss", grid=(t // ROW_TILE,),
        in_specs=[_row_spec(), _row_spec(), _row_spec(D_PLE), pl.BlockSpec(w_gate.shape, lambda i: (0, 0)),
                  pl.BlockSpec(w_ple.shape, lambda i: (0, 0, 0)), _vec_spec(), _row_spec()],
        out_specs=[_row_spec(), _row_spec(), _row_spec(), _vec_spec(), _vec_spec(1, LANES)],
        out_shape=[jax.ShapeDtypeStruct((t, D_MODEL), F32), jax.ShapeDtypeStruct((t, D_MODEL), BF16),
                   jax.ShapeDtypeStruct((t, D_MODEL), BF16), jax.ShapeDtypeStruct((1, D_MODEL), F32),
                   jax.ShapeDtypeStruct((1, LANES), F32)],
        compiler_params=_cparams(("arbitrary",)),
    )(x2, h3, p, w_gate, w_ple, g_final, target)


def _split3(v):
    hi = v.astype(BF16)
    r1 = v - hi.astype(F32)
    mid = r1.astype(BF16)
    lo = (r1 - mid.astype(F32)).astype(BF16)
    return hi, mid, lo


def _split2(v):
    hi = v.astype(BF16)
    return jnp.concatenate([hi, (v - hi.astype(F32)).astype(BF16)], axis=1)


def _dot(a, b, dims=_DIMS["nn"]):
    return lax.dot_general(a, b, dims, preferred_element_type=F32)


def _tri(n, rel):
    row = lax.broadcasted_iota(jnp.int32, (n, n), 0)
    col = lax.broadcasted_iota(jnp.int32, (n, n), 1)
    return rel(row, col).astype(BF16)


def _tri2(n, rel):
    t = _tri(n, rel)
    return jnp.concatenate([t, t], axis=0)


def _log_sigmoid(v):
    return -(jnp.maximum(-v, 0.0) + jnp.log(1.0 + jnp.exp(-jnp.abs(v))))


def _fox_prep(fl, b_forget, batch, seq):
    nb = seq // ATT_BLOCK

    def body(fl_ref, b_ref, cw_ref, cr_ref):
        col = lax.broadcasted_iota(jnp.int32, (ATT_BLOCK, F_PAD), 1)
        lower = _tri(ATT_BLOCK, lambda r, c: c <= r)
        upper = _tri(ATT_BLOCK, lambda r, c: r <= c)
        expand = (lax.broadcasted_iota(jnp.int32, (F_PAD, D_ATT), 1) // HEAD_DIM
                  == lax.broadcasted_iota(jnp.int32, (F_PAD, D_ATT), 0)).astype(BF16)
        carry_w = jnp.zeros((1, D_ATT), F32)
        carry_r = jnp.zeros((F_PAD, 1), F32)
        for i in range(nb):
            blk = slice(i * ATT_BLOCK, (i + 1) * ATT_BLOCK)
            logf = jnp.where(col < N_HEADS, _log_sigmoid(fl_ref[blk, :] + b_ref[...]), 0.0)
            cw = jnp.zeros((ATT_BLOCK, D_ATT), F32) + carry_w
            cr = jnp.zeros((F_PAD, ATT_BLOCK), F32) + carry_r
            for part in _split3(logf):
                cw += _dot(lower, _dot(part, expand).astype(BF16))
                cr += _dot(part, upper, TN_DIMS)
            cw_ref[blk, :] = cw
            cr_ref[:, blk] = cr[0:N_HEADS, :]
            carry_w = cw[ATT_BLOCK - 1:ATT_BLOCK, :]
            carry_r = cr[:, ATT_BLOCK - 1:ATT_BLOCK]

    return pl.pallas_call(
        body, name="fox_prep", grid=(batch,),
        in_specs=[pl.BlockSpec((seq, F_PAD), lambda b: (b, 0)), pl.BlockSpec((1, F_PAD), lambda b: (0, 0))],
        out_specs=[pl.BlockSpec((seq, D_ATT), lambda b: (b, 0)), pl.BlockSpec((N_HEADS, seq), lambda b: (b, 0))],
        out_shape=[jax.ShapeDtypeStruct((batch * seq, D_ATT), F32), jax.ShapeDtypeStruct((batch * N_HEADS, seq), F32)],
        compiler_params=_cparams(("parallel",)),
    )(fl, b_forget)


def _fox_post(dcs_wide, drs_wide, fl, b_forget, batch, seq):
    nb = seq // ATT_BLOCK

    def body(dcs_ref, drs_ref, fl_ref, b_ref, dfl_ref, db_ref):
        pick = (lax.broadcasted_iota(jnp.int32, (D_ATT, F_PAD), 0)
                == lax.broadcasted_iota(jnp.int32, (D_ATT, F_PAD), 1) * HEAD_DIM).astype(BF16)
        upper = _tri(ATT_BLOCK, lambda r, c: r <= c)
        col = lax.broadcasted_iota(jnp.int32, (ATT_BLOCK, F_PAD), 1)

        @pl.when(pl.program_id(0) == 0)
        def _():
            db_ref[...] = jnp.zeros_like(db_ref)

        carry = jnp.zeros((1, F_PAD), F32)
        for i in reversed(range(nb)):
            blk = slice(i * ATT_BLOCK, (i + 1) * ATT_BLOCK)
            narrow = jnp.zeros((ATT_BLOCK, F_PAD), F32)
            for part in _split3(drs_ref[blk, :] - dcs_ref[blk, :]):
                narrow += _dot(part, pick)
            after = jnp.zeros((ATT_BLOCK, F_PAD), F32) + carry
            for part in _split3(narrow):
                after += _dot(upper, part)
            carry = after[0:1, :]
            pre = fl_ref[blk, :] + b_ref[...]
            dfl = jnp.where(col < N_HEADS, after * jax.nn.sigmoid(-pre), 0.0)
            dfl_ref[blk, :] = dfl.astype(BF16)
            db_ref[...] += jnp.sum(dfl, axis=0, keepdims=True)

    return pl.pallas_call(
        body, name="fox_post", grid=(batch,),
        in_specs=[pl.BlockSpec((seq, D_ATT), lambda b: (b, 0)), pl.BlockSpec((seq, D_ATT), lambda b: (b, 0)),
                  pl.BlockSpec((seq, F_PAD), lambda b: (b, 0)), pl.BlockSpec((1, F_PAD), lambda b: (0, 0))],
        out_specs=[pl.BlockSpec((seq, F_PAD), lambda b: (b, 0)), pl.BlockSpec((1, F_PAD), lambda b: (0, 0))],
        out_shape=[jax.ShapeDtypeStruct((batch * seq, F_PAD), BF16), jax.ShapeDtypeStruct((1, F_PAD), F32)],
        compiler_params=_cparams(("arbitrary",)),
    )(dcs_wide, drs_wide, fl, b_forget)


N_PAIRS = N_HEADS // 2


def _att_specs(seq, col0, tq):
    nq = seq // tq
    q = pl.BlockSpec((tq, LANES), lambda b, hp, qi: (b * nq + qi, col0 + hp))
    k = pl.BlockSpec((seq, LANES), lambda b, hp, qi: (b, col0 + N_PAIRS + hp))
    v = pl.BlockSpec((seq, LANES), lambda b, hp, qi: (b, col0 + 2 * N_PAIRS + hp))
    return q, k, v


def _qblock_spec(seq, tq):
    nq = seq // tq
    return pl.BlockSpec((tq, LANES), lambda b, hp, qi: (b * nq + qi, hp))


def _kv_out_spec(seq):
    return pl.BlockSpec((seq, LANES), lambda b, hp, qi: (b, hp))


def _head_masks():
    lane = lax.broadcasted_iota(jnp.int32, (1, LANES), 1)
    return [(lane >= HEAD_DIM * j) & (lane < HEAD_DIM * (j + 1)) for j in range(2)]


def _stack_heads(t, masks):
    zero = jnp.zeros_like(t)
    return jnp.concatenate([jnp.where(masks[0], t, zero), jnp.where(masks[1], t, zero)], axis=0)


def _stack_cols(t):
    return jnp.concatenate([t[:, 0:1], t[:, HEAD_DIM:HEAD_DIM + 1]], axis=0)


def _unstack(t2, masks):
    tq = t2.shape[0] // 2
    return jnp.where(masks[0], t2[:tq], t2[tq:])


def _stacked_ids(tq, tk):
    row = lax.broadcasted_iota(jnp.int32, (2 * tq, tk), 0)
    col = lax.broadcasted_iota(jnp.int32, (2 * tq, tk), 1)
    first = lax.broadcasted_iota(jnp.int32, (2 * tq, 1), 0) < tq
    return col - jnp.where(row < tq, row, row - tq), first


def _sweep(qi, tq, tk, step, init):
    per = tq // tk
    carry = lax.fori_loop(0, per * qi, lambda kb, c: step(kb, c, None, 0), init)
    for j in range(per):
        carry = step(per * qi + j, carry, -j * tk, j * tk)
    return carry


def _below(t2, top):
    tq = t2.shape[0] // 2
    return t2 if top == 0 else jnp.concatenate([t2[top:tq], t2[tq + top:]], axis=0)


def _put_below(old, new, top):
    if top == 0:
        return new
    tq = old.shape[0] // 2
    return jnp.concatenate([old[:top], new[:tq - top], old[tq:tq + top], new[tq - top:]], axis=0)


def _att_fwd(qkv, c_wide, c_row, batch, seq, behind):
    tq, tkf = FOX_TILES
    tqs, tks = SB_TILES
    assert tq == tqs and tkf == 2 * tks
    nq = seq // tq

    def body(qa_ref, ka_ref, va_ref, cw_ref, cr_ref, qb_ref, kb_ref, vb_ref, of_ref, lse_ref, os_ref, rt_ref):
        hp, qi = pl.program_id(1), pl.program_id(2)
        masks = _head_masks()
        ahead_f, first = _stacked_ids(tq, tkf)
        ahead_s, _ = _stacked_ids(tq, tks)
        later = _tri2(tks, lambda r, c: r > c)
        q2f = _stack_heads(qa_ref[...], masks) * SCALE
        q2s = _stack_heads(qb_ref[...], masks) * SCALE
        ct = _stack_cols(cw_ref[...])

        def fox(kb, carry, lead):
            m, l, acc = carry
            k0 = pl.multiple_of(kb * tkf, tkf)
            cs = jnp.where(first, cr_ref[pl.ds(2 * hp, 1), pl.ds(k0, tkf)], cr_ref[pl.ds(2 * hp + 1, 1), pl.ds(k0, tkf)])
            s = _dot(q2f, ka_ref[pl.ds(k0, tkf), :], NT_DIMS) + ct - cs
            if lead is not None:
                s = jnp.where(ahead_f <= lead, s, NEG)
            m_new = jnp.maximum(m, jnp.max(s, axis=1, keepdims=True))
            p = jnp.exp(s - m_new)
            alpha = jnp.exp(m - m_new)
            l = alpha * l + jnp.sum(p, axis=1, keepdims=True)
            acc = alpha * acc + _dot(p.astype(BF16), va_ref[pl.ds(k0, tkf), :])
            return m_new, l, acc

        def sb(kb, carry, lead):
            run, acc = carry
            k0 = pl.multiple_of(kb * tks, tks)
            ls, lsn = _sb_logits(q2s, kb_ref[pl.ds(k0, tks), :])
            if lead is not None:
                lsn = jnp.where(ahead_s < lead, lsn, 0.0)
            w = jnp.exp(ls + _dot(_split2(lsn), later) + run)
            if lead is not None:
                w = jnp.where(ahead_s < lead, w, 0.0)
            return run + jnp.sum(lsn, axis=1, keepdims=True), acc + _dot(w.astype(BF16), vb_ref[pl.ds(k0, tks), :])

        fox_c = (jnp.full((2 * tq, 1), NEG, F32), jnp.zeros((2 * tq, 1), F32), jnp.zeros((2 * tq, LANES), F32))
        sb_c = (jnp.zeros((2 * tq, 1), F32), jnp.zeros((2 * tq, LANES), F32))
        sb_c = sb(2 * qi, sb(2 * qi + 1, sb_c, -tks), 0)

        def both(i, carries):
            fox_c, sb_c = carries
            return fox(i, fox_c, None), sb(2 * qi - 2 - 2 * i, sb(2 * qi - 1 - 2 * i, sb_c, None), None)

        fox_c, (run, acc_s) = lax.fori_loop(0, qi, both, (fox_c, sb_c))
        m, l, acc = fox(qi, fox_c, 0)
        of_ref[...] = _unstack(acc / l, masks).astype(BF16)
        lse_ref[...] = _unstack(m + jnp.log(l), masks)
        os_ref[...] = _unstack(acc_s, masks).astype(BF16)
        rt_ref[...] = _unstack(run, masks)

    qa, ka, va = _att_specs(seq, 0, tq)
    qb_, kb_, vb_ = _att_specs(seq, 3 * N_PAIRS, tq)
    qb = _qblock_spec(seq, tq)
    half, wide = jax.ShapeDtypeStruct((batch * seq, D_ATT), BF16), jax.ShapeDtypeStruct((batch * seq, D_ATT), F32)
    return _call_behind(
        body, behind, name="att_fwd", grid=(batch, N_PAIRS, nq),
        in_specs=[qa, ka, va, qb, pl.BlockSpec((N_HEADS, seq), lambda b, hp, qi: (b, 0)), qb_, kb_, vb_],
        out_specs=[qb, qb, qb, qb], out_shape=[half, wide, half, wide], scratch_shapes=[],
        operands=(qkv, qkv, qkv, c_wide, c_row, qkv, qkv, qkv))


def _fox_bwd(qkv, c_wide, c_row, o, do, lse_wide, batch, seq, behind):
    tq, tk = FOX_TILES
    nq = seq // tq

    def body(q_ref, k_ref, v_ref, cw_ref, cr_ref, o_ref, do_ref, lse_ref,
             dq_ref, dk_ref, dv_ref, dcs_ref, drs_ref, dkc_acc, dv_acc):
        hp, qi = pl.program_id(1), pl.program_id(2)

        @pl.when(qi == 0)
        def _():
            dkc_acc[...] = jnp.zeros_like(dkc_acc)
            dv_acc[...] = jnp.zeros_like(dv_acc)

        masks = _head_masks()
        ahead, first = _stacked_ids(tq, tk)
        q_t, do_t = q_ref[...], do_ref[...]
        q2 = _stack_heads(q_t, masks) * SCALE
        do2 = _stack_heads(do_t, masks)
        q_and_ones = jnp.concatenate([q2, _stack_heads(jnp.ones_like(q_t), masks)], axis=1)
        ct = _stack_cols(cw_ref[...])
        lse = _stack_cols(lse_ref[...])
        prod = do_t.astype(F32) * o_ref[...].astype(F32)
        delta = jnp.concatenate([jnp.sum(jnp.where(mk, prod, 0.0), axis=1, keepdims=True) for mk in masks], axis=0)

        def step(kb, carry, lead, top):
            dq_acc, rs = carry
            k0 = pl.multiple_of(kb * tk, tk)
            kblk = k_ref[pl.ds(k0, tk), :]
            cs = jnp.where(first, cr_ref[pl.ds(2 * hp, 1), pl.ds(k0, tk)], cr_ref[pl.ds(2 * hp + 1, 1), pl.ds(k0, tk)])
            p = jnp.exp(_dot(q2, kblk, NT_DIMS) + ct - cs - lse)
            if lead is not None:
                p = jnp.where(ahead <= lead, p, 0.0)
            dp = _dot(do2, v_ref[pl.ds(k0, tk), :], NT_DIMS)
            ds = (p * (dp - delta)).astype(BF16)
            dkc_acc[pl.ds(k0, tk), :] += _dot(ds, q_and_ones, TN_DIMS)
            dv_acc[pl.ds(k0, tk), :] += _dot(p.astype(BF16), do2, TN_DIMS)
            return dq_acc + _dot(ds, kblk), rs + jnp.sum(ds.astype(F32), axis=1, keepdims=True)

        init = (jnp.zeros((2 * tq, LANES), F32), jnp.zeros((2 * tq, 1), F32))
        dq_acc, rs = _sweep(qi, tq, tk, step, init)
        dq_ref[...] = (_unstack(dq_acc, masks) * SCALE).astype(BF16)
        drs_ref[...] = _unstack(rs, masks)

        @pl.when(qi == nq - 1)
        def _():
            dk_ref[...] = dkc_acc[:, 0:LANES].astype(BF16)
            dcs_ref[...] = dkc_acc[:, LANES:2 * LANES]
            dv_ref[...] = dv_acc[...].astype(BF16)

    q_spec, k_spec, v_spec = _att_specs(seq, 0, tq)
    qb = _qblock_spec(seq, tq)
    return _call_behind(
        body, behind, name="fox_bwd", grid=(batch, N_PAIRS, nq),
        in_specs=[q_spec, k_spec, v_spec, qb, pl.BlockSpec((N_HEADS, seq), lambda b, hp, qi: (b, 0)), qb, qb, qb],
        out_specs=[qb, _kv_out_spec(seq), _kv_out_spec(seq), _kv_out_spec(seq), qb],
        out_shape=[jax.ShapeDtypeStruct((batch * seq, D_ATT), BF16)] * 3 + [jax.ShapeDtypeStruct((batch * seq, D_ATT), F32)] * 2,
        scratch_shapes=[pltpu.VMEM((seq, 2 * LANES), F32), pltpu.VMEM((seq, LANES), F32)],
        operands=(qkv, qkv, qkv, c_wide, c_row, o, do, lse_wide))


def _sb_logits(q2, kblk):
    z = _dot(q2, kblk, NT_DIMS)
    lsn = jnp.minimum(-z, 0.0) - jnp.log(1.0 + jnp.exp(-jnp.abs(z)))
    return lsn + z, lsn


def _sb_bwd(qkv, do, rt_wide, batch, seq, behind):
    tq, tk = SB_TILES
    nq = seq // tq

    def body(q_ref, k_ref, v_ref, do_ref, rt_ref, dq_ref, dk_ref, dv_ref, dk_acc, dv_acc):
        qi = pl.program_id(2)

        @pl.when(qi == 0)
        def _():
            dk_acc[...] = jnp.zeros_like(dk_acc)
            dv_acc[...] = jnp.zeros_like(dv_acc)

        masks = _head_masks()
        ahead, _ = _stacked_ids(tq, tk)
        later = _tri2(tk, lambda r, c: r > c)
        earlier = _tri(tk, lambda r, c: r < c)
        q2 = _stack_heads(q_ref[...], masks) * SCALE
        do2 = _stack_heads(do_ref[...], masks)
        total = _stack_cols(rt_ref[...])

        def step(kb, carry, lead, top):
            pref, epre, dq_acc = (_below(t, top) for t in carry)
            q_s, do_s = _below(q2, top), _below(do2, top)
            seen = None if lead is None else _below(ahead, top) < lead
            k0 = pl.multiple_of(kb * tk, tk)
            kblk = k_ref[pl.ds(k0, tk), :]
            ls, lsn_all = _sb_logits(q_s, kblk)
            lsn = lsn_all if lead is None else jnp.where(seen, lsn_all, 0.0)
            rs = jnp.sum(lsn, axis=1, keepdims=True)
            w = jnp.exp(ls + _dot(_split2(lsn), later) + (_below(total, top) - pref - rs))
            if lead is not None:
                w = jnp.where(seen, w, 0.0)
            e = w * _dot(do_s, v_ref[pl.ds(k0, tk), :], NT_DIMS)
            before = _dot(e.astype(BF16), earlier) + epre
            dz = e * jnp.exp(lsn_all) - jnp.exp(ls) * before
            if lead is not None:
                dz = jnp.where(seen, dz, 0.0)
            dz = dz.astype(BF16)
            dk_acc[pl.ds(k0, tk), :] += _dot(dz, q_s, TN_DIMS)
            dv_acc[pl.ds(k0, tk), :] += _dot(w.astype(BF16), do_s, TN_DIMS)
            new = (pref + rs, epre + jnp.sum(e, axis=1, keepdims=True), dq_acc + _dot(dz, kblk))
            return tuple(_put_below(o, n, top) for o, n in zip(carry, new))

        init = (jnp.zeros((2 * tq, 1), F32), jnp.zeros((2 * tq, 1), F32), jnp.zeros((2 * tq, LANES), F32))
        dq_acc = _sweep(qi, tq, tk, step, init)[2]
        dq_ref[...] = (_unstack(dq_acc, masks) * SCALE).astype(BF16)

        @pl.when(qi == nq - 1)
        def _():
            dk_ref[...] = dk_acc[...].astype(BF16)
            dv_ref[...] = dv_acc[...].astype(BF16)

    q_spec, k_spec, v_spec = _att_specs(seq, 3 * N_PAIRS, tq)
    qb = _qblock_spec(seq, tq)
    return _call_behind(
        body, behind, name="sb_bwd", grid=(batch, N_PAIRS, nq), in_specs=[q_spec, k_spec, v_spec, qb, qb],
        out_specs=[qb, _kv_out_spec(seq), _kv_out_spec(seq)], out_shape=[jax.ShapeDtypeStruct((batch * seq, D_ATT), BF16)] * 3,
        scratch_shapes=[pltpu.VMEM((seq, LANES), F32), pltpu.VMEM((seq, LANES), F32)], operands=(qkv, qkv, qkv, do, rt_wide))


def _local_step(x, p, target, first, rest, vec, place):
    batch, seq, _ = x.shape
    t = batch * seq
    x = x.reshape(t, D_MODEL)
    target = target.reshape(t, D_MODEL)
    p = p.reshape(t, D_PLE)
    big = dict(tm=1024, tn=1024, tk=1024)

    (h1,), (w_in_slots,) = _norm_fwd(x, vec["g_mix"], "norm_mix", first)
    w = _first_weights(w_in_slots)
    qkv = _mm(h1, w["qkv"], mode="nn", name="proj_qkv", out_dtype=BF16, **big)
    gl = _mm(h1, w["gate"], mode="nn", name="proj_gate", **big)
    fl = _mm(h1, w["forget"], mode="nn", name="proj_forget", **big)
    c_wide, c_row = _fox_prep(fl, vec["b_forget"], batch, seq)
    (o_fox, lse_wide, o_sb, rt_wide), gathered = _att_fwd(qkv, c_wide, c_row, batch, seq, rest)
    w = dict(w, **_rest_weights(dict(zip(EARLY + ("b_gate",), gathered))))
    merged, of, os_ = _gate_fwd(gl, w["b_gate"], o_fox, o_sb, w["branch_fox"], w["branch_sb"])
    x1, h2 = _mm_res_norm(merged, w["out"], x, vec["g_mlp"], "proj_out_norm")
    ar = _mm(h2, w["up"], mode="nn", name="mlp_up", out_dtype=BF16, epi=lambda acc, _: jnp.maximum(acc, 0.0),
             col_shards=True, **big)
    x2, h3 = _mm_res_norm(ar, w["down"], x1, vec["g_ple"], "mlp_down_norm", a_fn=_relu2)

    dx3, dpre, dpe, dg_final, loss = _head_and_loss(x2, h3, p, w["ple_gate"], w["ple"], vec["g_final"], target)
    gw = {}
    gw["ple"] = _mm(p, dpe, mode="tn", name="d_w_ple", col_shards=True, **big)
    gw["ple_gate"] = _mm(h3, dpre, mode="tn", name="d_w_ple_gate", **big)
    dx2, dx2b, dg_ple = _mm_norm_bwd([(dpre, w["ple_gate"])], None, x2, vec["g_ple"], dx3, "d_h_ple_norm_bwd")
    gw["down"] = _mm(ar, dx2b, mode="tn", name="d_w_down", a_fn=_relu2, **big)
    da = _mm(dx2b, w["down"], mode="nt", name="d_act", out_dtype=BF16,
             epi=lambda acc, r: acc * (2.0 * r.astype(F32)), extra=ar, **big)
    gw["up"] = _mm(h2, da, mode="tn", name="d_w_up", col_shards=True, **big)
    dx1, dx1b, dg_mlp = _mm_norm_bwd([(da, w["up"])], None, x1, vec["g_mlp"], dx2, "d_h_mlp_norm_bwd")
    gw["out"] = _mm(merged, dx1b, mode="tn", name="d_w_out", **big)
    dof, dos, dgl, gw["b_gate"], do_fox, do_sb = _gate_bwd(gl, w["b_gate"], of, os_, dx1b, w["out"], w["branch_fox"],
                                                                  w["branch_sb"])
    gw["branch_fox"] = _mm(o_fox, dof, mode="tn", name="d_w_branch_fox", col_shards=True, **big)
    gw["branch_sb"] = _mm(o_sb, dos, mode="tn", name="d_w_branch_sb", col_shards=True, **big)
    early = _early_slots(gw)
    early = [early[n] for n in EARLY]
    (dq_a, dk_a, dv_a, dcs_wide, drs_wide), received = _fox_bwd(qkv, c_wide, c_row, o_fox, do_fox, lse_wide, batch, seq,
                                                                _swap_halves(early))
    sums = _sum_sibling(place, early, received, "sum_sibling_early")
    (dq_b, dk_b, dv_b), others = _sb_bwd(qkv, do_sb, rt_wide, batch, seq, _exchange_chips(sums))
    mine = [None] * len(EARLY)
    for group, tag in ((BIG, "big"), (SMALL, "small")):
        for t, res in zip(group, _sum_chips(place, *[[a[t] for t in group] for a in (early, received, others)], "sum_chips_" + tag)):
            mine[t] = res
    dfl, db_forget = _fox_post(dcs_wide, drs_wide, fl, vec["b_forget"], batch, seq)
    dqkv = jnp.concatenate([dq_a, dk_a, dv_a, dq_b, dk_b, dv_b], axis=1)
    gw["qkv"], theirs = _mm(dqkv, h1, mode="tn", name="d_w_qkv", behind=_share_halves(mine), flat_out=True, **big)
    reduced = dict(zip(EARLY, zip(mine, theirs)))
    gw["gate"] = _mm(dgl, h1, mode="tn", name="d_w_gate", flat_out=True, **big)
    gw["forget"] = _mm(dfl, h1, mode="tn", name="d_w_forget", flat_out=True, **big)
    late = [_w_in_slots(gw)]
    dh1, received = _mm(dqkv, w["qkv"], mode="nt", name="d_h_qkv", behind=_swap_halves(late), **big)
    sums = _sum_sibling(place, late, received, "sum_sibling_w_in")
    (grad_x, _, dg_mix), others = _mm_norm_bwd([(dgl, w["gate"]), (dfl, w["forget"])], dh1, x, vec["g_mix"], dx1,
                                               "d_h_gate_norm_bwd", behind=_exchange_chips(sums))
    mine = _sum_chips(place, late, received, others, "sum_chips_w_in")
    reduced["w_in"] = (mine[0], _run_exchange(_share_halves(mine), "reduce_share_w_in")[0])
    gvec = {"g_mix": dg_mix, "b_forget": db_forget[:, 0:N_HEADS], "g_mlp": dg_mlp, "g_ple": dg_ple,
            "g_final": dg_final, "b_gate": gw["b_gate"]}
    return loss, grad_x.reshape(batch, seq, D_MODEL), reduced, gvec


ANY = pl.BlockSpec(memory_space=pl.ANY)
SHARDED = ("w_in", "w_branch_fox", "w_branch_sb", "w_out", "w_up", "w_down", "w_ple_gate", "w_ple")
ROW_ALIGN = 16
F32_ROWS = 8


def _place():
    return lax.axis_index("x"), lax.axis_index("y"), lax.axis_index("c")


def _other_chips(x, y):
    return [(1 - x, y), (x, 1 - y), (1 - x, 1 - y)]


def _half(ref, h):
    r = ref.shape[0] // 2
    assert r % ROW_ALIGN == 0
    return ref.at[pl.ds(pl.multiple_of(h * r, ROW_ALIGN), r)]


def _remote(src, dst, sems, idx, to):
    send_sems, recv_sems = sems
    return pltpu.make_async_remote_copy(src_ref=src, dst_ref=dst, send_sem=send_sems.at[idx], recv_sem=recv_sems.at[idx],
                                        device_id=to, device_id_type=MESH)


class _Exchange:
    def __init__(self, operands, out_shapes, sem_shape, start, finish):
        self.operands, self.out_shapes, self.sem_shape, self.start, self.finish = operands, out_shapes, sem_shape, start, finish

    def scratch(self):
        return [pltpu.SemaphoreType.DMA(self.sem_shape), pltpu.SemaphoreType.DMA(self.sem_shape)]


def _run_exchange(ex, name):
    n = len(ex.operands)

    def body(*refs):
        ex.start(refs[:n], refs[n:2 * n], refs[2 * n:])
        ex.finish(refs[:n], refs[n:2 * n], refs[2 * n:])

    return pl.pallas_call(body, name=name, in_specs=[ANY] * n, out_specs=[ANY] * n, out_shape=ex.out_shapes,
                          scratch_shapes=ex.scratch())(*ex.operands)


def _call_behind(body, ex, *, name, grid, in_specs, out_specs, out_shape, scratch_shapes, operands):
    n_in, n_out, nx = len(in_specs), len(out_specs), len(ex.operands)

    def wrapped(*refs):
        ins, x_in = refs[:n_in], refs[n_in:n_in + nx]
        outs, x_out = refs[n_in + nx:n_in + nx + n_out], refs[n_in + nx + n_out:n_in + 2 * nx + n_out]
        scratch, sems = refs[n_in + 2 * nx + n_out:-2], refs[-2:]
        first, last = None, None
        for d, steps in enumerate(grid):
            at_start, at_end = pl.program_id(d) == 0, pl.program_id(d) == steps - 1
            first = at_start if first is None else first & at_start
            last = at_end if last is None else last & at_end

        @pl.when(first)
        def _():
            ex.start(x_in, x_out, sems)

        body(*ins, *outs, *scratch)

        @pl.when(last)
        def _():
            ex.finish(x_in, x_out, sems)

    res = pl.pallas_call(
        wrapped, name=name, grid=grid, in_specs=list(in_specs) + [ANY] * nx, out_specs=list(out_specs) + [ANY] * nx,
        out_shape=list(out_shape) + list(ex.out_shapes), scratch_shapes=list(scratch_shapes) + ex.scratch(),
        compiler_params=_cparams(("arbitrary",) * len(grid)),
    )(*operands, *ex.operands)
    return res[:n_out], res[n_out:]


def _gather_weights(shards):
    n = len(shards)

    def first_copies(src, out, sems):
        x, y, c = _place()
        me = 2 * x + y
        copies = [_remote(_half(src[t], c), _half(out[t].at[me], c), sems, (t, k), (px, py, c))
                  for t in range(n) for k, (px, py) in enumerate(_other_chips(x, y))]
        return copies + [_remote(src[t], out[t].at[me], sems, (t, 3), (x, y, 1 - c)) for t in range(n)]

    def start(src, out, sems):
        for cp in first_copies(src, out, sems):
            cp.start()

    def finish(src, out, sems):
        x, y, c = _place()
        me = 2 * x + y
        sibling = (x, y, 1 - c)
        chips = _other_chips(x, y)
        passes = []
        for t in range(n):
            for k, (px, py) in enumerate(chips):
                landed = _half(out[t].at[2 * px + py], c)
                _remote(landed, landed, sems, (t, k), (px, py, c)).wait_recv()
                passes.append(_remote(landed, landed, sems, (t, 4 + k), sibling))
                passes[-1].start()
        for t in range(n):
            _remote(src[t], out[t].at[me], sems, (t, 3), sibling).wait_recv()
            for k, (px, py) in enumerate(chips):
                passed = _half(out[t].at[2 * px + py], 1 - c)
                _remote(passed, passed, sems, (t, 4 + k), sibling).wait_recv()
        for cp in first_copies(src, out, sems) + passes:
            cp.wait_send()

    return _Exchange(shards, [jax.ShapeDtypeStruct((N_CHIPS,) + s.shape, s.dtype) for s in shards], (n, 7), start, finish)


def _simple_exchange(operands, out_shapes, copies):
    def start(src, out, sems):
        for cp in copies(src, out, sems):
            cp.start()

    def finish(src, out, sems):
        for cp in copies(src, out, sems):
            cp.wait_recv()
        for cp in copies(src, out, sems):
            cp.wait_send()

    return _Exchange(operands, out_shapes, (len(operands),), start, finish)


def _swap_halves(slots):
    def copies(src, out, sems):
        x, y, c = _place()
        res = []
        for t in range(len(slots)):
            r = src[t].shape[1] // 2
            rows = pl.ds(pl.multiple_of((1 - c) * r, F32_ROWS), r)
            res.append(_remote(src[t].at[:, rows], out[t], sems, t, (x, y, 1 - c)))
        return res

    return _simple_exchange(slots, [jax.ShapeDtypeStruct((N_CHIPS, s.shape[1] // 2, s.shape[2]), s.dtype) for s in slots], copies)


def _exchange_chips(sums):
    n = len(sums)

    def copies(src, out, sems):
        x, y, c = _place()
        return [_remote(src[t].at[2 * px + py], out[t].at[k], sems, (t, k), (px, py, c))
                for t in range(n) for k, (px, py) in enumerate(_other_chips(x, y))]

    def start(src, out, sems):
        for cp in copies(src, out, sems):
            cp.start()

    def finish(src, out, sems):
        for cp in copies(src, out, sems):
            cp.wait_recv()
        for cp in copies(src, out, sems):
            cp.wait_send()

    return _Exchange(sums, [jax.ShapeDtypeStruct((3,) + s.shape[1:], s.dtype) for s in sums], (n, 3), start, finish)


def _share_halves(mine):
    def copies(src, out, sems):
        x, y, c = _place()
        return [_remote(src[t], out[t], sems, t, (x, y, 1 - c)) for t in range(len(mine))]

    return _simple_exchange(mine, [jax.ShapeDtypeStruct(s.shape, s.dtype) for s in mine], copies)


def _walk(name, place, parts):
    starts = [sum(p[0] for p in parts[:t]) for t in range(len(parts))]
    held = lambda index, start, steps: (lambda s, pr: index(jnp.clip(s - start, 0, steps - 1), pr))
    in_specs, out_specs, out_shapes, operands = [], [], [], []
    for (steps, ins, outs, shapes, ops, _), start in zip(parts, starts):
        in_specs += [pl.BlockSpec(blk, held(index, start, steps)) for blk, index in ins]
        out_specs += [pl.BlockSpec(blk, held(index, start, steps)) for blk, index in outs]
        out_shapes += list(shapes)
        operands += list(ops)

    def body(place_ref, *refs):
        s = pl.program_id(0)
        i, o = 0, len(in_specs)
        for (steps, ins, outs, _, _, fn), start in zip(parts, starts):
            mine_in, mine_out = refs[i:i + len(ins)], refs[o:o + len(outs)]
            i, o = i + len(ins), o + len(outs)

            @pl.when((s >= start) & (s < start + steps))
            def _(mine_in=mine_in, mine_out=mine_out, start=start, fn=fn):
                fn(s - start, mine_in, mine_out)

    res = pl.pallas_call(
        body, name=name, out_shape=out_shapes,
        grid_spec=pltpu.PrefetchScalarGridSpec(num_scalar_prefetch=1, grid=(sum(p[0] for p in parts),), in_specs=in_specs,
                                               out_specs=out_specs),
        compiler_params=_cparams(("arbitrary",)),
    )(place, *operands)
    counts = [len(p[2]) for p in parts]
    return [res[sum(counts[:t]):sum(counts[:t + 1])] for t in range(len(parts))]


WALK_BLOCK = D_MODEL * D_MODEL // 2


def _sum_sibling(place, slots, received, name):
    def part(slot, got):
        n, rows, cols = got.shape
        whole = n * rows * cols <= WALK_BLOCK
        block = (n if whole else None, rows, cols)

        def fn(j, ins, outs):
            outs[0][...] = (ins[0][...] + ins[1][...]).astype(BF16)

        return (1 if whole else n, [(block, lambda j, pr: (j, pr[1], 0)), (block, lambda j, pr: (j, 0, 0))], [(block, lambda j, pr: (j, 0, 0))],
                [jax.ShapeDtypeStruct(got.shape, BF16)], [slot, got], fn)

    return [r[0] for r in _walk(name, place, [part(s, g) for s, g in zip(slots, received)])]


def _sum_chips(place, slots, received, others, name):
    def part(slot, got, other):
        _, rows, cols = got.shape
        block = (None, rows, cols)

        def fn(_, ins, outs):
            own = ins[0][...] + ins[1][...]
            outs[0][...] = ((own + ins[2][0].astype(F32)) + ins[2][1].astype(F32)) + ins[2][2].astype(F32)

        return (1, [(block, lambda _, pr: (pr[0], pr[1], 0)), (block, lambda _, pr: (pr[0], 0, 0)),
                    ((3, rows, cols), lambda _, pr: (0, 0, 0))], [((rows, cols), lambda _, pr: (0, 0))],
                [jax.ShapeDtypeStruct((rows, cols), F32)], [slot, got, other], fn)

    return [r[0] for r in _walk(name, place, [part(*t) for t in zip(slots, received, others)])]


N_DEVICES = 8


def _sum_devices(block, name):
    def body(v_ref, o_ref, land_ref, send_sems, recv_sems):
        x, y, c = _place()
        me = 4 * x + 2 * y + c
        copies = []
        for mask in range(1, N_DEVICES):
            peer = (x ^ (mask >> 2), y ^ ((mask >> 1) & 1), c ^ (mask & 1))
            copies.append(pltpu.make_async_remote_copy(src_ref=v_ref, dst_ref=land_ref.at[me], send_sem=send_sems.at[mask - 1],
                                                       recv_sem=recv_sems.at[mask - 1], device_id=peer, device_id_type=MESH))
        for cp in copies:
            cp.start()
        land_ref[me] = v_ref[...]
        for cp in copies:
            cp.wait_recv()
        total = land_ref[0]
        for d in range(1, N_DEVICES):
            total = total + land_ref[d]
        o_ref[...] = total
        for cp in copies:
            cp.wait_send()

    vmem = pl.BlockSpec(memory_space=pltpu.VMEM)
    return pl.pallas_call(
        body, name=name, in_specs=[vmem], out_specs=vmem, out_shape=jax.ShapeDtypeStruct(block.shape, F32),
        scratch_shapes=[pltpu.VMEM((N_DEVICES,) + block.shape, F32), pltpu.SemaphoreType.DMA((N_DEVICES - 1,)),
                        pltpu.SemaphoreType.DMA((N_DEVICES - 1,))],
    )(block)


def _vec_block(g_mix, g_mlp, g_ple, g_final, b_forget, b_gate_rows, last=None):
    pad = lambda a: jnp.concatenate([a, jnp.zeros((a.shape[0], D_MODEL - a.shape[1]), F32)], axis=1)
    last = jnp.zeros((1, 0), F32) if last is None else last
    return jnp.concatenate([g_mix, g_mlp, g_ple, g_final.reshape(1, D_MODEL), pad(b_forget), pad(b_gate_rows), pad(last)],
                           axis=0)


def _adam_math(w, g, m, v):
    m_new = ADAM_B1 * m + (1.0 - ADAM_B1) * g
    v_new = ADAM_B2 * v + (1.0 - ADAM_B2) * (g * g)
    m_hat = m_new / (1.0 - ADAM_B1 ** ADAM_STEP)
    v_hat = v_new / (1.0 - ADAM_B2 ** ADAM_STEP)
    return -ADAM_LR * (m_hat / (jnp.sqrt(v_hat) + ADAM_EPS) + ADAM_WD * w), m_new, v_new


def _adamw_halves(place, weights, name):
    def part(w, m, v, g_mine, g_theirs):
        rows, cols = g_mine.shape
        whole = ((rows, cols), lambda s, pr: (pr[1] + s - 2 * pr[1] * s, 0))
        half = ((rows, cols), lambda s, pr: (0, 0))

        def fn(s, ins, outs):
            g = jnp.where(s == 0, ins[3][...], ins[4][...])
            outs[0][...] = g
            outs[1][...], outs[2][...], outs[3][...] = _adam_math(ins[0][...], g, ins[1][...], ins[2][...])

        return (2, [whole] * 3 + [half] * 2, [whole] * 4, [jax.ShapeDtypeStruct(w.shape, F32)] * 4, [w, m, v, g_mine, g_theirs], fn)

    return _walk(name, place, [part(*t) for t in weights])


def _adamw_vec(w, g, m, v):
    def body(w_ref, g_ref, m_ref, v_ref, d_ref, nm_ref, nv_ref):
        d_ref[...], nm_ref[...], nv_ref[...] = _adam_math(w_ref[...], g_ref[...], m_ref[...], v_ref[...])

    return pl.pallas_call(body, name="adamw_vectors", out_shape=[jax.ShapeDtypeStruct(w.shape, F32)] * 3)(w, g, m, v)


WEIGHT_NAMES = ("g_mix", "w_in", "b_forget", "b_gate", "w_branch_fox", "w_branch_sb", "w_out", "g_mlp", "w_up", "w_down",
                "g_ple", "w_ple_gate", "w_ple", "g_final")
W_IN_SHARD = D_IN // N_CHIPS
Q_END, F_END, B_END = 3 * D_ATT, 3 * D_ATT + N_HEADS, 6 * D_ATT + N_HEADS
GATE_SHARD = D_MODEL // N_CHIPS


LATE = SHARDED[:1]
EARLY = SHARDED[1:]
BIG = tuple(t for t, n in enumerate(EARLY) if n in ("w_up", "w_down"))
SMALL = tuple(t for t in range(len(EARLY)) if t not in BIG)


def _first_weights(w_in_slots):
    def cols(*ranges):
        parts = []
        for lo, hi in ranges:
            for j in range(N_CHIPS):
                a, b = max(lo, j * W_IN_SHARD), min(hi, (j + 1) * W_IN_SHARD)
                if a < b:
                    parts.append(w_in_slots[j, :, a - j * W_IN_SHARD:b - j * W_IN_SHARD])
        return parts

    forget = jnp.concatenate(cols((Q_END, F_END)) + [jnp.zeros((D_MODEL, F_PAD - N_HEADS), BF16)], axis=1)
    return {"qkv": jnp.concatenate(cols((0, Q_END), (F_END, B_END)), axis=1), "gate": jnp.concatenate(cols((B_END, D_IN)), axis=1),
            "forget": forget}


GATE_ROWS = 2 * ROW_ALIGN


def _gate_bits(b_gate):
    bits = lax.bitcast_convert_type(b_gate, BF16).reshape(2, 2 * GATE_SHARD)
    return jnp.concatenate([bits, jnp.zeros((GATE_ROWS - 2, 2 * GATE_SHARD), BF16)], axis=0)


def _rest_weights(gathered):
    rows = lambda a: a.reshape(N_CHIPS * a.shape[1], a.shape[2])
    bits = gathered["b_gate"][:, :2].reshape(N_CHIPS, 2, GATE_SHARD, 2)
    b_gate = jnp.transpose(lax.bitcast_convert_type(bits, F32), (1, 0, 2)).reshape(2, D_MODEL)
    return {"branch_fox": gathered["w_branch_fox"], "branch_sb": gathered["w_branch_sb"], "out": rows(gathered["w_out"]),
            "up": gathered["w_up"], "down": rows(gathered["w_down"]), "ple_gate": rows(gathered["w_ple_gate"]),
            "ple": gathered["w_ple"], "b_gate": b_gate}


def _early_slots(gw):
    rows = lambda a: a.reshape(N_CHIPS, a.shape[0] // N_CHIPS, a.shape[1])
    return {"w_branch_fox": gw["branch_fox"], "w_branch_sb": gw["branch_sb"], "w_out": rows(gw["out"]), "w_up": gw["up"],
            "w_down": rows(gw["down"]), "w_ple_gate": rows(gw["ple_gate"]), "w_ple": gw["ple"]}


W_IN_FLAT = (W_IN_SHARD * D_MODEL // LANES, LANES)


def _w_in_slots(gw):
    c = D_MODEL // LANES
    g_t = jnp.concatenate([gw["qkv"][:Q_END * c], gw["forget"][:N_HEADS * c], gw["qkv"][Q_END * c:], gw["gate"]], axis=0)
    return g_t.reshape((N_CHIPS,) + W_IN_FLAT)


def _flat(a):
    return jnp.transpose(a, (2, 0, 1)).reshape(W_IN_FLAT)


def _unflat(a):
    return jnp.transpose(a.reshape(W_IN_SHARD, D_MODEL // LANES, LANES), (1, 2, 0)).reshape(1, D_MODEL, W_IN_SHARD)


def kernel(x, p, g_mix, w_in, b_forget, b_gate, w_branch_fox, w_branch_sb, w_out, g_mlp, w_up, w_down, g_ple, w_ple_gate, w_ple, g_final, loss_target, m_g_mix, m_w_in, m_b_forget, m_b_gate, m_w_branch_fox, m_w_branch_sb, m_w_out, m_g_mlp, m_w_up, m_w_down, m_g_ple, m_w_ple_gate, m_w_ple, m_g_final, v_g_mix, v_w_in, v_b_forget, v_b_gate, v_w_branch_fox, v_w_branch_sb, v_w_out, v_g_mlp, v_w_up, v_w_down, v_g_ple, v_w_ple_gate, v_w_ple, v_g_final):
    weights = dict(g_mix=g_mix, w_in=w_in, b_forget=b_forget, b_gate=b_gate, w_branch_fox=w_branch_fox,
                   w_branch_sb=w_branch_sb, w_out=w_out, g_mlp=g_mlp, w_up=w_up, w_down=w_down, g_ple=g_ple,
                   w_ple_gate=w_ple_gate, w_ple=w_ple, g_final=g_final)
    first = dict(g_mix=m_g_mix, w_in=m_w_in, b_forget=m_b_forget, b_gate=m_b_gate, w_branch_fox=m_w_branch_fox,
                 w_branch_sb=m_w_branch_sb, w_out=m_w_out, g_mlp=m_g_mlp, w_up=m_w_up, w_down=m_w_down, g_ple=m_g_ple,
                 w_ple_gate=m_w_ple_gate, w_ple=m_w_ple, g_final=m_g_final)
    second = dict(g_mix=v_g_mix, w_in=v_w_in, b_forget=v_b_forget, b_gate=v_b_gate, w_branch_fox=v_w_branch_fox,
                  w_branch_sb=v_w_branch_sb, w_out=v_w_out, g_mlp=v_g_mlp, w_up=v_w_up, w_down=v_w_down, g_ple=v_g_ple,
                  w_ple_gate=v_w_ple_gate, w_ple=v_w_ple, g_final=v_g_final)
    cx, cy, cc = _place()
    chip = 2 * cx + cy
    place = jnp.stack([chip, cc]).astype(jnp.int32)
    col0 = chip * GATE_SHARD

    first_gather = _gather_weights([weights[n][0].astype(BF16) for n in LATE])
    rest = _gather_weights([weights[n][0].astype(BF16) for n in EARLY] + [_gate_bits(b_gate[0])])
    vec = {"g_mix": g_mix, "b_forget": jnp.concatenate([b_forget, jnp.zeros((1, F_PAD - N_HEADS), F32)], axis=1),
           "g_mlp": g_mlp, "g_ple": g_ple, "g_final": g_final.reshape(1, D_MODEL)}

    loss, grad_x, reduced, gvec = _local_step(x, p[0], loss_target, first_gather, rest, vec, place)

    out = {}
    args = lambda n: (weights[n][0], first[n][0], second[n][0]) + tuple(reduced[n])
    for names, tag in [([EARLY[t] for t in SMALL], "small")] + [([EARLY[t]], EARLY[t]) for t in BIG]:
        for n, res in zip(names, _adamw_halves(place, [args(n) for n in names], "adamw_" + tag)):
            out[n] = [r[None] for r in res]
    (res,) = _adamw_halves(place, [(_flat(w_in), _flat(m_w_in), _flat(v_w_in)) + tuple(reduced["w_in"])], "adamw_w_in")
    out["w_in"] = [_unflat(r) for r in res]

    g_block = _sum_devices(_vec_block(gvec["g_mix"], gvec["g_mlp"], gvec["g_ple"], gvec["g_final"][0], gvec["b_forget"],
                                      gvec["b_gate"], loss), "reduce_vectors")
    loss = g_block[7, 0]
    g_gate = lax.dynamic_slice(g_block[5:7], (0, col0), (2, GATE_SHARD))
    blocks = [_vec_block(d["g_mix"], d["g_mlp"], d["g_ple"], d["g_final"], d["b_forget"], d["b_gate"][0])
              for d in (weights, first, second)]
    g_rows = jnp.concatenate([g_block[0:5], jnp.concatenate([g_gate, jnp.zeros((2, D_MODEL - GATE_SHARD), F32)], axis=1),
                              jnp.zeros((1, D_MODEL), F32)], axis=0)
    res = (g_rows,) + tuple(_adamw_vec(blocks[0], g_rows, blocks[1], blocks[2]))
    out["g_mix"] = [r[0:1] for r in res]
    out["g_mlp"] = [r[1:2] for r in res]
    out["g_ple"] = [r[2:3] for r in res]
    out["g_final"] = [r[3] for r in res]
    out["b_forget"] = [r[4:5, :N_HEADS] for r in res]
    out["b_gate"] = [r[5:7, :GATE_SHARD][None] for r in res]
    return (loss, grad_x, *[out[n][0] for n in WEIGHT_NAMES], *[out[n][1] for n in WEIGHT_NAMES],
            *[out[n][2] for n in WEIGHT_NAMES], *[out[n][3] for n in WEIGHT_NAMES])
```

```python
import jax
import jax.numpy as jnp
from jax import lax
from jax.experimental import pallas as pl
from jax.experimental.pallas import tpu as pltpu

F32 = jnp.float32
BF16 = jnp.bfloat16

D_MODEL = 1024
HEAD_DIM = 64
N_HEADS = 8
D_ATT = N_HEADS * HEAD_DIM
D_PLE = 256
D_IN = 6 * D_ATT + N_HEADS + 2 * D_MODEL
F_PAD = 128
EPS = 1e-6
SCALE = HEAD_DIM ** -0.5
N_CHIPS = 4
LANES = 128
ATT_BLOCK = 256
FOX_TILES = (512, 512)
SB_TILES = (512, 256)
NEG = -1e30

ADAM_LR = 0.001
ADAM_B1 = 0.9
ADAM_B2 = 0.999
ADAM_EPS = 1e-08
ADAM_WD = 0.01
ADAM_STEP = 10

VMEM_LIMIT = 56 * 1024 * 1024

MESH = pl.DeviceIdType.MESH


def _cparams(sem=None):
    return pltpu.CompilerParams(dimension_semantics=sem, vmem_limit_bytes=VMEM_LIMIT)


def _relu2(t):
    t = t.astype(F32)
    return t * t


_DIMS = {"nn": (((1,), (0,)), ((), ())), "nt": (((1,), (1,)), ((), ())), "tn": (((0,), (0,)), ((), ()))}
NT_DIMS = _DIMS["nt"]
TN_DIMS = _DIMS["tn"]


def _mm(a, b, *, mode, name, out_dtype=F32, tm=512, tn=512, tk=512, add=None, a_fn=None, epi=None, extra=None,
        col_shards=False, behind=None, flat_out=False):
    if mode == "nn":
        (m, k), n = a.shape, b.shape[-1]
    elif mode == "nt":
        (m, k), n = a.shape, b.shape[-2]
    else:
        (k, m), n = a.shape, b.shape[1]
    shard = None
    if col_shards:
        if mode == "nn":
            shard, n = n, N_CHIPS * n
            tn = min(tn, shard)
        elif mode == "nt":
            shard = b.shape[-1]
            tk = min(tk, shard)
        else:
            shard = n // N_CHIPS
            if tn < n:
                tn = min(tn, shard)
    tm, tn, tk = min(tm, m), min(tn, n), min(tk, k)
    assert m % tm == 0 and n % tn == 0 and k % tk == 0, (name, m, n, k)
    nk = k // tk
    all_shards = col_shards and mode == "tn" and tn == n
    a_spec = {"nn": pl.BlockSpec((tm, tk), lambda i, j, kk: (i, kk)),
              "nt": pl.BlockSpec((tm, tk), lambda i, j, kk: (i, kk)),
              "tn": pl.BlockSpec((tk, tm), lambda i, j, kk: (kk, i))}[mode]
    b_spec = {"nn": pl.BlockSpec((tk, tn), lambda i, j, kk: (kk, j)),
              "nt": pl.BlockSpec((tn, tk), lambda i, j, kk: (j, kk)),
              "tn": pl.BlockSpec((tk, tn), lambda i, j, kk: (kk, j))}[mode]
    o_spec = pl.BlockSpec((tm, tn), lambda i, j, kk: (i, j))
    out_shape = (m, n)
    if col_shards and mode == "nn":
        per = shard // tn
        b_spec = pl.BlockSpec((None, tk, tn), lambda i, j, kk: (j // per, kk, j % per))
    elif col_shards and mode == "nt":
        per = shard // tk
        b_spec = pl.BlockSpec((None, tn, tk), lambda i, j, kk: (kk // per, j, kk % per))
    elif col_shards:
        assert add is None and extra is None
        if all_shards:
            o_spec = pl.BlockSpec((N_CHIPS, tm, shard), lambda i, j, kk: (0, i, 0))
        else:
            per = shard // tn
            o_spec = pl.BlockSpec((None, tm, tn), lambda i, j, kk: (j // per, i, j % per))
        out_shape = (N_CHIPS, m, shard)
    if flat_out:
        assert mode == "tn" and tn == n == D_MODEL and not col_shards and add is None and extra is None
        chunks = D_MODEL // LANES
        o_spec = pl.BlockSpec((tm * chunks, LANES), lambda i, j, kk: (i, 0))
        out_shape = (m * chunks, LANES)
    operands, in_specs = [a, b], [a_spec, b_spec]
    third = add if add is not None else extra
    if third is not None:
        operands.append(third)
        in_specs.append(o_spec)

    def body(*refs):
        a_ref, b_ref = refs[0], refs[1]
        t_ref = refs[2] if third is not None else None
        o_ref = refs[3] if third is not None else refs[2]
        acc_ref = refs[-1] if nk > 1 else None
        at = a_ref[...]
        if a_fn is not None:
            at = a_fn(at)
        part = lax.dot_general(at.astype(BF16), b_ref[...].astype(BF16), _DIMS[mode], preferred_element_type=F32)

        def finish(acc):
            if epi is not None:
                acc = epi(acc, None if t_ref is None else t_ref[...])
            elif add is not None:
                acc = acc + t_ref[...].astype(F32)
            if flat_out:
                for q in range(D_MODEL // LANES):
                    o_ref[pl.ds(q, tm, stride=D_MODEL // LANES), :] = acc[:, q * LANES:(q + 1) * LANES].astype(o_ref.dtype)
                return
            if all_shards:
                for slot in range(N_CHIPS):
                    o_ref[slot] = acc[:, slot * shard:(slot + 1) * shard].astype(o_ref.dtype)
                return
            o_ref[...] = acc.astype(o_ref.dtype)

        if nk == 1:
            finish(part)
        else:
            kk = pl.program_id(2)

            @pl.when(kk == 0)
            def _():
                acc_ref[...] = part

            @pl.when(kk > 0)
            def _():
                acc_ref[...] += part

            @pl.when(kk == nk - 1)
            def _():
                finish(acc_ref[...])

    call = dict(name=name, grid=(m // tm, n // tn, nk), in_specs=in_specs,
                scratch_shapes=[pltpu.VMEM((tm, tn), F32)] if nk > 1 else [])
    if behind is not None:
        (res,), exchanged = _call_behind(body, behind, out_specs=[o_spec], out_shape=[jax.ShapeDtypeStruct(out_shape, out_dtype)],
                                         operands=operands, **call)
        return res, exchanged
    return pl.pallas_call(body, out_specs=o_spec, out_shape=jax.ShapeDtypeStruct(out_shape, out_dtype),
                          compiler_params=_cparams(("parallel", "parallel", "arbitrary")), **call)(*operands)


ROW_TILE = 512


def _row_spec(width=D_MODEL, rows=ROW_TILE):
    return pl.BlockSpec((rows, width), lambda i: (i, 0))


def _vec_spec(rows=1, width=D_MODEL):
    return pl.BlockSpec((rows, width), lambda i: (0, 0))


def _xhat(x):
    r = lax.rsqrt(jnp.mean(x * x, axis=-1, keepdims=True) + EPS)
    return x * r, r


def _rms_bwd_rows(dh, x, g):
    xh, r = _xhat(x)
    dxh = dh * g
    dx = r * (dxh - xh * jnp.mean(dxh * xh, axis=-1, keepdims=True))
    return dx, jnp.sum(dh * xh, axis=0, keepdims=True)


def _norm_fwd(x, g, name, behind):
    t = x.shape[0]

    def body(x_ref, g_ref, h_ref):
        xh, _ = _xhat(x_ref[...])
        h_ref[...] = (xh * g_ref[...]).astype(BF16)

    return _call_behind(body, behind, name=name, grid=(t // ROW_TILE,), in_specs=[_row_spec(), _vec_spec()],
                        out_specs=[_row_spec()], out_shape=[jax.ShapeDtypeStruct((t, D_MODEL), BF16)], scratch_shapes=[],
                        operands=(x, g))


def _mm_res_norm(a, b, res, g, name, a_fn=None):
    t, k = a.shape

    def body(a_ref, b_ref, res_ref, g_ref, x_ref, h_ref):
        at = a_ref[...] if a_fn is None else a_fn(a_ref[...])
        x_new = res_ref[...] + _dot(at.astype(BF16), b_ref[...])
        x_ref[...] = x_new
        h_ref[...] = (_xhat(x_new)[0] * g_ref[...]).astype(BF16)

    return pl.pallas_call(
        body, name=name, grid=(t // ROW_TILE,),
        in_specs=[pl.BlockSpec((ROW_TILE, k), lambda i: (i, 0)), pl.BlockSpec(b.shape, lambda i: (0, 0)), _row_spec(), _vec_spec()],
        out_specs=[_row_spec(), _row_spec()],
        out_shape=[jax.ShapeDtypeStruct((t, D_MODEL), F32), jax.ShapeDtypeStruct((t, D_MODEL), BF16)],
        compiler_params=_cparams(("parallel",)),
    )(a, b, res, g)


def _mm_norm_bwd(pairs, dh_first, x, g, dres, name, behind=None):
    t = x.shape[0]
    operands, in_specs = [], []
    for a, b in pairs:
        if b.ndim == 3:
            for j in range(b.shape[0]):
                operands += [a, b]
                in_specs += [pl.BlockSpec((ROW_TILE, b.shape[2]), lambda i, j=j: (i, j)),
                             pl.BlockSpec((None, D_MODEL, b.shape[2]), lambda i, j=j: (j, 0, 0))]
        else:
            operands += [a, b]
            in_specs += [pl.BlockSpec((ROW_TILE, a.shape[1]), lambda i: (i, 0)), pl.BlockSpec(b.shape, lambda i: (0, 0))]
    n_mm = len(operands)
    operands += [x, g, dres] + ([] if dh_first is None else [dh_first])
    in_specs += [_row_spec(), _vec_spec(), _row_spec()] + ([] if dh_first is None else [_row_spec()])

    def body(*refs):
        x_ref, g_ref, dres_ref = refs[n_mm:n_mm + 3]
        dx_ref, dxb_ref, dg_ref = refs[-3:]
        dh = 0.0 if dh_first is None else refs[n_mm + 3][...]
        for k in range(0, n_mm, 2):
            dh = dh + lax.dot_general(refs[k][...].astype(BF16), refs[k + 1][...].astype(BF16), NT_DIMS,
                                      preferred_element_type=F32)
        dx, dg = _rms_bwd_rows(dh, x_ref[...], g_ref[...])
        dx = dx + dres_ref[...]
        dx_ref[...] = dx
        dxb_ref[...] = dx.astype(BF16)

        @pl.when(pl.program_id(0) == 0)
        def _():
            dg_ref[...] = jnp.zeros_like(dg_ref)

        dg_ref[...] += dg

    call = dict(name=name, grid=(t // ROW_TILE,), in_specs=in_specs, out_specs=[_row_spec(), _row_spec(), _vec_spec()],
                out_shape=[jax.ShapeDtypeStruct((t, D_MODEL), F32), jax.ShapeDtypeStruct((t, D_MODEL), BF16),
                           jax.ShapeDtypeStruct((1, D_MODEL), F32)])
    if behind is not None:
        return _call_behind(body, behind, scratch_shapes=[], operands=operands, **call)
    return pl.pallas_call(body, compiler_params=_cparams(("arbitrary",)), **call)(*operands)


def _shards_spec(w):
    return pl.BlockSpec(w.shape, lambda i: (0, 0, 0))


def _gate_fwd(gl, b_gate, o_fox, o_sb, w_fox, w_sb):
    t = o_fox.shape[0]

    def body(gla_ref, glb_ref, b_ref, ofox_ref, osb_ref, wf_ref, ws_ref, m_ref, of_ref, os_ref):
        of = jnp.concatenate([_dot(ofox_ref[...], wf_ref[j]) for j in range(N_CHIPS)], axis=1)
        os_ = jnp.concatenate([_dot(osb_ref[...], ws_ref[j]) for j in range(N_CHIPS)], axis=1)
        ga = jax.nn.sigmoid(gla_ref[...] + b_ref[0:1, :])
        gb = jax.nn.sigmoid(glb_ref[...] + b_ref[1:2, :])
        of_ref[...] = of
        os_ref[...] = os_
        m_ref[...] = (ga * of + gb * os_).astype(BF16)

    return pl.pallas_call(
        body, name="gate_fwd", grid=(t // ROW_TILE,),
        in_specs=[pl.BlockSpec((ROW_TILE, D_MODEL), lambda i: (i, 0)), pl.BlockSpec((ROW_TILE, D_MODEL), lambda i: (i, 1)),
                  _vec_spec(2), _row_spec(D_ATT), _row_spec(D_ATT), _shards_spec(w_fox), _shards_spec(w_sb)],
        out_specs=[_row_spec(), _row_spec(), _row_spec()],
        out_shape=[jax.ShapeDtypeStruct((t, D_MODEL), BF16)] + [jax.ShapeDtypeStruct((t, D_MODEL), F32)] * 2,
        compiler_params=_cparams(("parallel",)),
    )(gl, gl, b_gate, o_fox, o_sb, w_fox, w_sb)


def _gate_bwd(gl, b_gate, of, os_, dx, w_out, w_fox, w_sb):
    t = of.shape[0]
    shard = D_MODEL // N_CHIPS

    def back(d, w_ref):
        return sum(_dot(d[:, j * shard:(j + 1) * shard], w_ref[j], NT_DIMS) for j in range(N_CHIPS)).astype(BF16)

    def body(gla_ref, glb_ref, b_ref, of_ref, os_ref, dx_ref, w_ref, wf_ref, ws_ref,
             dof_ref, dos_ref, dgl_ref, db_ref, dofox_ref, dosb_ref):
        dm = _dot(dx_ref[...], w_ref[...], NT_DIMS)
        ga = jax.nn.sigmoid(gla_ref[...] + b_ref[0:1, :])
        gb = jax.nn.sigmoid(glb_ref[...] + b_ref[1:2, :])
        dof = (dm * ga).astype(BF16)
        dos = (dm * gb).astype(BF16)
        dof_ref[...] = dof
        dos_ref[...] = dos
        dofox_ref[...] = back(dof, wf_ref)
        dosb_ref[...] = back(dos, ws_ref)
        dgla = dm * of_ref[...] * ga * (1.0 - ga)
        dglb = dm * os_ref[...] * gb * (1.0 - gb)
        dgl_ref[:, 0:D_MODEL] = dgla.astype(BF16)
        dgl_ref[:, D_MODEL:2 * D_MODEL] = dglb.astype(BF16)

        @pl.when(pl.program_id(0) == 0)
        def _():
            db_ref[...] = jnp.zeros_like(db_ref)

        db_ref[0:1, :] += jnp.sum(dgla, axis=0, keepdims=True)
        db_ref[1:2, :] += jnp.sum(dglb, axis=0, keepdims=True)

    outs = pl.pallas_call(
        body, name="gate_bwd", grid=(t // ROW_TILE,),
        in_specs=[pl.BlockSpec((ROW_TILE, D_MODEL), lambda i: (i, 0)), pl.BlockSpec((ROW_TILE, D_MODEL), lambda i: (i, 1)),
                  _vec_spec(2), _row_spec(), _row_spec(), _row_spec(), pl.BlockSpec(w_out.shape, lambda i: (0, 0)),
                  _shards_spec(w_fox), _shards_spec(w_sb)],
        out_specs=[_row_spec(), _row_spec(), _row_spec(2 * D_MODEL), _vec_spec(2), _row_spec(D_ATT), _row_spec(D_ATT)],
        out_shape=[jax.ShapeDtypeStruct((t, D_MODEL), BF16)] * 2 + [jax.ShapeDtypeStruct((t, 2 * D_MODEL), BF16),
                                                                      jax.ShapeDtypeStruct((2, D_MODEL), F32)]
        + [jax.ShapeDtypeStruct((t, D_ATT), BF16)] * 2,
        compiler_params=_cparams(("arbitrary",)),
    )(gl, gl, b_gate, of, os_, dx, w_out, w_fox, w_sb)
    return outs


def _head_and_loss(x2, h3, p, w_gate, w_ple, g_final, target):
    t = x2.shape[0]

    def body(x2_ref, h3_ref, p_ref, wg_ref, wp_ref, g_ref, tgt_ref, dx3_ref, dpre_ref, dpe_ref, dg_ref, loss_ref):
        gp = jax.nn.sigmoid(_dot(h3_ref[...], wg_ref[...]))
        p_t = p_ref[...].astype(BF16)
        pe_t = jnp.concatenate([_dot(p_t, wp_ref[j]) for j in range(N_CHIPS)], axis=1)
        x3 = x2_ref[...] + gp * pe_t
        g = g_ref[...]
        xh, _ = _xhat(x3)
        err = xh * g - tgt_ref[...]
        dy = err * (1.0 / D_MODEL)
        dx3, dg = _rms_bwd_rows(dy, x3, g)
        dx3_ref[...] = dx3
        dpre_ref[...] = (dx3 * pe_t * gp * (1.0 - gp)).astype(BF16)
        dpe_ref[...] = (dx3 * gp).astype(BF16)

        @pl.when(pl.program_id(0) == 0)
        def _():
            dg_ref[...] = jnp.zeros_like(dg_ref)
            loss_ref[...] = jnp.zeros_like(loss_ref)

        dg_ref[...] += dg
        loss_ref[...] += 0.5 * jnp.sum(jnp.mean(err * err, axis=-1, keepdims=True), axis=0, keepdims=True)

    return pl.pallas_call(
        body, name="head_and_loss", grid=(t // ROW_TILE,),
        in_specs=[_row_spec(), _row_spec(), _row_spec(D_PLE), pl.BlockSpec(w_gate.shape, lambda i: (0, 0)),
                  pl.BlockSpec(w_ple.shape, lambda i: (0, 0, 0)), _vec_spec(), _row_spec()],
        out_specs=[_row_spec(), _row_spec(), _row_spec(), _vec_spec(), _vec_spec(1, LANES)],
        out_shape=[jax.ShapeDtypeStruct((t, D_MODEL), F32), jax.ShapeDtypeStruct((t, D_MODEL), BF16),
                   jax.ShapeDtypeStruct((t, D_MODEL), BF16), jax.ShapeDtypeStruct((1, D_MODEL), F32),
                   jax.ShapeDtypeStruct((1, LANES), F32)],
        compiler_params=_cparams(("arbitrary",)),
    )(x2, h3, p, w_gate, w_ple, g_final, target)


def _split3(v):
    hi = v.astype(BF16)
    r1 = v - hi.astype(F32)
    mid = r1.astype(BF16)
    lo = (r1 - mid.astype(F32)).astype(BF16)
    return hi, mid, lo


def _split2(v):
    hi = v.astype(BF16)
    return jnp.concatenate([hi, (v - hi.astype(F32)).astype(BF16)], axis=1)


def _dot(a, b, dims=_DIMS["nn"]):
    return lax.dot_general(a, b, dims, preferred_element_type=F32)


def _tri(n, rel):
    row = lax.broadcasted_iota(jnp.int32, (n, n), 0)
    col = lax.broadcasted_iota(jnp.int32, (n, n), 1)
    return rel(row, col).astype(BF16)


def _tri2(n, rel):
    t = _tri(n, rel)
    return jnp.concatenate([t, t], axis=0)


def _log_sigmoid(v):
    return -(jnp.maximum(-v, 0.0) + jnp.log(1.0 + jnp.exp(-jnp.abs(v))))


def _fox_prep(fl, b_forget, batch, seq):
    nb = seq // ATT_BLOCK

    def body(fl_ref, b_ref, cw_ref, cr_ref):
        col = lax.broadcasted_iota(jnp.int32, (ATT_BLOCK, F_PAD), 1)
        lower = _tri(ATT_BLOCK, lambda r, c: c <= r)
        upper = _tri(ATT_BLOCK, lambda r, c: r <= c)
        expand = (lax.broadcasted_iota(jnp.int32, (F_PAD, D_ATT), 1) // HEAD_DIM
                  == lax.broadcasted_iota(jnp.int32, (F_PAD, D_ATT), 0)).astype(BF16)
        carry_w = jnp.zeros((1, D_ATT), F32)
        carry_r = jnp.zeros((F_PAD, 1), F32)
        for i in range(nb):
            blk = slice(i * ATT_BLOCK, (i + 1) * ATT_BLOCK)
            logf = jnp.where(col < N_HEADS, _log_sigmoid(fl_ref[blk, :] + b_ref[...]), 0.0)
            cw = jnp.zeros((ATT_BLOCK, D_ATT), F32) + carry_w
            cr = jnp.zeros((F_PAD, ATT_BLOCK), F32) + carry_r
            for part in _split3(logf):
                cw += _dot(lower, _dot(part, expand).astype(BF16))
                cr += _dot(part, upper, TN_DIMS)
            cw_ref[blk, :] = cw
            cr_ref[:, blk] = cr[0:N_HEADS, :]
            carry_w = cw[ATT_BLOCK - 1:ATT_BLOCK, :]
            carry_r = cr[:, ATT_BLOCK - 1:ATT_BLOCK]

    return pl.pallas_call(
        body, name="fox_prep", grid=(batch,),
        in_specs=[pl.BlockSpec((seq, F_PAD), lambda b: (b, 0)), pl.BlockSpec((1, F_PAD), lambda b: (0, 0))],
        out_specs=[pl.BlockSpec((seq, D_ATT), lambda b: (b, 0)), pl.BlockSpec((N_HEADS, seq), lambda b: (b, 0))],
        out_shape=[jax.ShapeDtypeStruct((batch * seq, D_ATT), F32), jax.ShapeDtypeStruct((batch * N_HEADS, seq), F32)],
        compiler_params=_cparams(("parallel",)),
    )(fl, b_forget)


def _fox_post(dcs_wide, drs_wide, fl, b_forget, batch, seq):
    nb = seq // ATT_BLOCK

    def body(dcs_ref, drs_ref, fl_ref, b_ref, dfl_ref, db_ref):
        pick = (lax.broadcasted_iota(jnp.int32, (D_ATT, F_PAD), 0)
                == lax.broadcasted_iota(jnp.int32, (D_ATT, F_PAD), 1) * HEAD_DIM).astype(BF16)
        upper = _tri(ATT_BLOCK, lambda r, c: r <= c)
        col = lax.broadcasted_iota(jnp.int32, (ATT_BLOCK, F_PAD), 1)

        @pl.when(pl.program_id(0) == 0)
        def _():
            db_ref[...] = jnp.zeros_like(db_ref)

        carry = jnp.zeros((1, F_PAD), F32)
        for i in reversed(range(nb)):
            blk = slice(i * ATT_BLOCK, (i + 1) * ATT_BLOCK)
            narrow = jnp.zeros((ATT_BLOCK, F_PAD), F32)
            for part in _split3(drs_ref[blk, :] - dcs_ref[blk, :]):
                narrow += _dot(part, pick)
            after = jnp.zeros((ATT_BLOCK, F_PAD), F32) + carry
            for part in _split3(narrow):
                after += _dot(upper, part)
            carry = after[0:1, :]
            pre = fl_ref[blk, :] + b_ref[...]
            dfl = jnp.where(col < N_HEADS, after * jax.nn.sigmoid(-pre), 0.0)
            dfl_ref[blk, :] = dfl.astype(BF16)
            db_ref[...] += jnp.sum(dfl, axis=0, keepdims=True)

    return pl.pallas_call(
        body, name="fox_post", grid=(batch,),
        in_specs=[pl.BlockSpec((seq, D_ATT), lambda b: (b, 0)), pl.BlockSpec((seq, D_ATT), lambda b: (b, 0)),
                  pl.BlockSpec((seq, F_PAD), lambda b: (b, 0)), pl.BlockSpec((1, F_PAD), lambda b: (0, 0))],
        out_specs=[pl.BlockSpec((seq, F_PAD), lambda b: (b, 0)), pl.BlockSpec((1, F_PAD), lambda b: (0, 0))],
        out_shape=[jax.ShapeDtypeStruct((batch * seq, F_PAD), BF16), jax.ShapeDtypeStruct((1, F_PAD), F32)],
        compiler_params=_cparams(("arbitrary",)),
    )(dcs_wide, drs_wide, fl, b_forget)


N_PAIRS = N_HEADS // 2


def _att_specs(seq, col0, tq):
    nq = seq // tq
    q = pl.BlockSpec((tq, LANES), lambda b, hp, qi: (b * nq + qi, col0 + hp))
    k = pl.BlockSpec((seq, LANES), lambda b, hp, qi: (b, col0 + N_PAIRS + hp))
    v = pl.BlockSpec((seq, LANES), lambda b, hp, qi: (b, col0 + 2 * N_PAIRS + hp))
    return q, k, v


def _qblock_spec(seq, tq):
    nq = seq // tq
    return pl.BlockSpec((tq, LANES), lambda b, hp, qi: (b * nq + qi, hp))


def _kv_out_spec(seq):
    return pl.BlockSpec((seq, LANES), lambda b, hp, qi: (b, hp))


def _head_masks():
    lane = lax.broadcasted_iota(jnp.int32, (1, LANES), 1)
    return [(lane >= HEAD_DIM * j) & (lane < HEAD_DIM * (j + 1)) for j in range(2)]


def _stack_heads(t, masks):
    zero = jnp.zeros_like(t)
    return jnp.concatenate([jnp.where(masks[0], t, zero), jnp.where(masks[1], t, zero)], axis=0)


def _stack_cols(t):
    return jnp.concatenate([t[:, 0:1], t[:, HEAD_DIM:HEAD_DIM + 1]], axis=0)


def _unstack(t2, masks):
    tq = t2.shape[0] // 2
    return jnp.where(masks[0], t2[:tq], t2[tq:])


def _stacked_ids(tq, tk):
    row = lax.broadcasted_iota(jnp.int32, (2 * tq, tk), 0)
    col = lax.broadcasted_iota(jnp.int32, (2 * tq, tk), 1)
    first = lax.broadcasted_iota(jnp.int32, (2 * tq, 1), 0) < tq
    return col - jnp.where(row < tq, row, row - tq), first


def _sweep(qi, tq, tk, step, init):
    per = tq // tk
    carry = lax.fori_loop(0, per * qi, lambda kb, c: step(kb, c, None, 0), init)
    for j in range(per):
        carry = step(per * qi + j, carry, -j * tk, j * tk)
    return carry


def _below(t2, top):
    tq = t2.shape[0] // 2
    return t2 if top == 0 else jnp.concatenate([t2[top:tq], t2[tq + top:]], axis=0)


def _put_below(old, new, top):
    if top == 0:
        return new
    tq = old.shape[0] // 2
    return jnp.concatenate([old[:top], new[:tq - top], old[tq:tq + top], new[tq - top:]], axis=0)


def _att_fwd(qkv, c_wide, c_row, batch, seq, behind):
    tq, tkf = FOX_TILES
    tqs, tks = SB_TILES
    assert tq == tqs and tkf == 2 * tks
    nq = seq // tq

    def body(qa_ref, ka_ref, va_ref, cw_ref, cr_ref, qb_ref, kb_ref, vb_ref, of_ref, lse_ref, os_ref, rt_ref):
        hp, qi = pl.program_id(1), pl.program_id(2)
        masks = _head_masks()
        ahead_f, first = _stacked_ids(tq, tkf)
        ahead_s, _ = _stacked_ids(tq, tks)
        later = _tri2(tks, lambda r, c: r > c)
        q2f = _stack_heads(qa_ref[...], masks) * SCALE
        q2s = _stack_heads(qb_ref[...], masks) * SCALE
        ct = _stack_cols(cw_ref[...])

        def fox(kb, carry, lead):
            m, l, acc = carry
            k0 = pl.multiple_of(kb * tkf, tkf)
            cs = jnp.where(first, cr_ref[pl.ds(2 * hp, 1), pl.ds(k0, tkf)], cr_ref[pl.ds(2 * hp + 1, 1), pl.ds(k0, tkf)])
            s = _dot(q2f, ka_ref[pl.ds(k0, tkf), :], NT_DIMS) + ct - cs
            if lead is not None:
                s = jnp.where(ahead_f <= lead, s, NEG)
            m_new = jnp.maximum(m, jnp.max(s, axis=1, keepdims=True))
            p = jnp.exp(s - m_new)
            alpha = jnp.exp(m - m_new)
            l = alpha * l + jnp.sum(p, axis=1, keepdims=True)
            acc = alpha * acc + _dot(p.astype(BF16), va_ref[pl.ds(k0, tkf), :])
            return m_new, l, acc

        def sb(kb, carry, lead):
            run, acc = carry
            k0 = pl.multiple_of(kb * tks, tks)
            ls, lsn = _sb_logits(q2s, kb_ref[pl.ds(k0, tks), :])
            if lead is not None:
                lsn = jnp.where(ahead_s < lead, lsn, 0.0)
            w = jnp.exp(ls + _dot(_split2(lsn), later) + run)
            if lead is not None:
                w = jnp.where(ahead_s < lead, w, 0.0)
            return run + jnp.sum(lsn, axis=1, keepdims=True), acc + _dot(w.astype(BF16), vb_ref[pl.ds(k0, tks), :])

        fox_c = (jnp.full((2 * tq, 1), NEG, F32), jnp.zeros((2 * tq, 1), F32), jnp.zeros((2 * tq, LANES), F32))
        sb_c = (jnp.zeros((2 * tq, 1), F32), jnp.zeros((2 * tq, LANES), F32))
        sb_c = sb(2 * qi, sb(2 * qi + 1, sb_c, -tks), 0)

        def both(i, carries):
            fox_c, sb_c = carries
            return fox(i, fox_c, None), sb(2 * qi - 2 - 2 * i, sb(2 * qi - 1 - 2 * i, sb_c, None), None)

        fox_c, (run, acc_s) = lax.fori_loop(0, qi, both, (fox_c, sb_c))
        m, l, acc = fox(qi, fox_c, 0)
        of_ref[...] = _unstack(acc / l, masks).astype(BF16)
        lse_ref[...] = _unstack(m + jnp.log(l), masks)
        os_ref[...] = _unstack(acc_s, masks).astype(BF16)
        rt_ref[...] = _unstack(run, masks)

    qa, ka, va = _att_specs(seq, 0, tq)
    qb_, kb_, vb_ = _att_specs(seq, 3 * N_PAIRS, tq)
    qb = _qblock_spec(seq, tq)
    half, wide = jax.ShapeDtypeStruct((batch * seq, D_ATT), BF16), jax.ShapeDtypeStruct((batch * seq, D_ATT), F32)
    return _call_behind(
        body, behind, name="att_fwd", grid=(batch, N_PAIRS, nq),
        in_specs=[qa, ka, va, qb, pl.BlockSpec((N_HEADS, seq), lambda b, hp, qi: (b, 0)), qb_, kb_, vb_],
        out_specs=[qb, qb, qb, qb], out_shape=[half, wide, half, wide], scratch_shapes=[],
        operands=(qkv, qkv, qkv, c_wide, c_row, qkv, qkv, qkv))


def _fox_bwd(qkv, c_wide, c_row, o, do, lse_wide, batch, seq, behind):
    tq, tk = FOX_TILES
    nq = seq // tq

    def body(q_ref, k_ref, v_ref, cw_ref, cr_ref, o_ref, do_ref, lse_ref,
             dq_ref, dk_ref, dv_ref, dcs_ref, drs_ref, dkc_acc, dv_acc):
        hp, qi = pl.program_id(1), pl.program_id(2)

        @pl.when(qi == 0)
        def _():
            dkc_acc[...] = jnp.zeros_like(dkc_acc)
            dv_acc[...] = jnp.zeros_like(dv_acc)

        masks = _head_masks()
        ahead, first = _stacked_ids(tq, tk)
        q_t, do_t = q_ref[...], do_ref[...]
        q2 = _stack_heads(q_t, masks) * SCALE
        do2 = _stack_heads(do_t, masks)
        q_and_ones = jnp.concatenate([q2, _stack_heads(jnp.ones_like(q_t), masks)], axis=1)
        ct = _stack_cols(cw_ref[...])
        lse = _stack_cols(lse_ref[...])
        prod = do_t.astype(F32) * o_ref[...].astype(F32)
        delta = jnp.concatenate([jnp.sum(jnp.where(mk, prod, 0.0), axis=1, keepdims=True) for mk in masks], axis=0)

        def step(kb, carry, lead, top):
            dq_acc, rs = carry
            k0 = pl.multiple_of(kb * tk, tk)
            kblk = k_ref[pl.ds(k0, tk), :]
            cs = jnp.where(first, cr_ref[pl.ds(2 * hp, 1), pl.ds(k0, tk)], cr_ref[pl.ds(2 * hp + 1, 1), pl.ds(k0, tk)])
            p = jnp.exp(_dot(q2, kblk, NT_DIMS) + ct - cs - lse)
            if lead is not None:
                p = jnp.where(ahead <= lead, p, 0.0)
            dp = _dot(do2, v_ref[pl.ds(k0, tk), :], NT_DIMS)
            ds = (p * (dp - delta)).astype(BF16)
            dkc_acc[pl.ds(k0, tk), :] += _dot(ds, q_and_ones, TN_DIMS)
            dv_acc[pl.ds(k0, tk), :] += _dot(p.astype(BF16), do2, TN_DIMS)
            return dq_acc + _dot(ds, kblk), rs + jnp.sum(ds.astype(F32), axis=1, keepdims=True)

        init = (jnp.zeros((2 * tq, LANES), F32), jnp.zeros((2 * tq, 1), F32))
        dq_acc, rs = _sweep(qi, tq, tk, step, init)
        dq_ref[...] = (_unstack(dq_acc, masks) * SCALE).astype(BF16)
        drs_ref[...] = _unstack(rs, masks)

        @pl.when(qi == nq - 1)
        def _():
            dk_ref[...] = dkc_acc[:, 0:LANES].astype(BF16)
            dcs_ref[...] = dkc_acc[:, LANES:2 * LANES]
            dv_ref[...] = dv_acc[...].astype(BF16)

    q_spec, k_spec, v_spec = _att_specs(seq, 0, tq)
    qb = _qblock_spec(seq, tq)
    return _call_behind(
        body, behind, name="fox_bwd", grid=(batch, N_PAIRS, nq),
        in_specs=[q_spec, k_spec, v_spec, qb, pl.BlockSpec((N_HEADS, seq), lambda b, hp, qi: (b, 0)), qb, qb, qb],
        out_specs=[qb, _kv_out_spec(seq), _kv_out_spec(seq), _kv_out_spec(seq), qb],
        out_shape=[jax.ShapeDtypeStruct((batch * seq, D_ATT), BF16)] * 3 + [jax.ShapeDtypeStruct((batch * seq, D_ATT), F32)] * 2,
        scratch_shapes=[pltpu.VMEM((seq, 2 * LANES), F32), pltpu.VMEM((seq, LANES), F32)],
        operands=(qkv, qkv, qkv, c_wide, c_row, o, do, lse_wide))


def _sb_logits(q2, kblk):
    z = _dot(q2, kblk, NT_DIMS)
    lsn = jnp.minimum(-z, 0.0) - jnp.log(1.0 + jnp.exp(-jnp.abs(z)))
    return lsn + z, lsn


def _sb_bwd(qkv, do, rt_wide, batch, seq, behind):
    tq, tk = SB_TILES
    nq = seq // tq

    def body(q_ref, k_ref, v_ref, do_ref, rt_ref, dq_ref, dk_ref, dv_ref, dk_acc, dv_acc):
        qi = pl.program_id(2)

        @pl.when(qi == 0)
        def _():
            dk_acc[...] = jnp.zeros_like(dk_acc)
            dv_acc[...] = jnp.zeros_like(dv_acc)

        masks = _head_masks()
        ahead, _ = _stacked_ids(tq, tk)
        later = _tri2(tk, lambda r, c: r > c)
        earlier = _tri(tk, lambda r, c: r < c)
        q2 = _stack_heads(q_ref[...], masks) * SCALE
        do2 = _stack_heads(do_ref[...], masks)
        total = _stack_cols(rt_ref[...])

        def step(kb, carry, lead, top):
            pref, epre, dq_acc = (_below(t, top) for t in carry)
            q_s, do_s = _below(q2, top), _below(do2, top)
            seen = None if lead is None else _below(ahead, top) < lead
            k0 = pl.multiple_of(kb * tk, tk)
            kblk = k_ref[pl.ds(k0, tk), :]
            ls, lsn_all = _sb_logits(q_s, kblk)
            lsn = lsn_all if lead is None else jnp.where(seen, lsn_all, 0.0)
            rs = jnp.sum(lsn, axis=1, keepdims=True)
            w = jnp.exp(ls + _dot(_split2(lsn), later) + (_below(total, top) - pref - rs))
            if lead is not None:
                w = jnp.where(seen, w, 0.0)
            e = w * _dot(do_s, v_ref[pl.ds(k0, tk), :], NT_DIMS)
            before = _dot(e.astype(BF16), earlier) + epre
            dz = e * jnp.exp(lsn_all) - jnp.exp(ls) * before
            if lead is not None:
                dz = jnp.where(seen, dz, 0.0)
            dz = dz.astype(BF16)
            dk_acc[pl.ds(k0, tk), :] += _dot(dz, q_s, TN_DIMS)
            dv_acc[pl.ds(k0, tk), :] += _dot(w.astype(BF16), do_s, TN_DIMS)
            new = (pref + rs, epre + jnp.sum(e, axis=1, keepdims=True), dq_acc + _dot(dz, kblk))
            return tuple(_put_below(o, n, top) for o, n in zip(carry, new))

        init = (jnp.zeros((2 * tq, 1), F32), jnp.zeros((2 * tq, 1), F32), jnp.zeros((2 * tq, LANES), F32))
        dq_acc = _sweep(qi, tq, tk, step, init)[2]
        dq_ref[...] = (_unstack(dq_acc, masks) * SCALE).astype(BF16)

        @pl.when(qi == nq - 1)
        def _():
            dk_ref[...] = dk_acc[...].astype(BF16)
            dv_ref[...] = dv_acc[...].astype(BF16)

    q_spec, k_spec, v_spec = _att_specs(seq, 3 * N_PAIRS, tq)
    qb = _qblock_spec(seq, tq)
    return _call_behind(
        body, behind, name="sb_bwd", grid=(batch, N_PAIRS, nq), in_specs=[q_spec, k_spec, v_spec, qb, qb],
        out_specs=[qb, _kv_out_spec(seq), _kv_out_spec(seq)], out_shape=[jax.ShapeDtypeStruct((batch * seq, D_ATT), BF16)] * 3,
        scratch_shapes=[pltpu.VMEM((seq, LANES), F32), pltpu.VMEM((seq, LANES), F32)], operands=(qkv, qkv, qkv, do, rt_wide))


def _local_step(x, p, target, first, rest, vec, place):
    batch, seq, _ = x.shape
    t = batch * seq
    x = x.reshape(t, D_MODEL)
    target = target.reshape(t, D_MODEL)
    p = p.reshape(t, D_PLE)
    big = dict(tm=1024, tn=1024, tk=1024)

    (h1,), (w_in_slots,) = _norm_fwd(x, vec["g_mix"], "norm_mix", first)
    w = _first_weights(w_in_slots)
    qkv = _mm(h1, w["qkv"], mode="nn", name="proj_qkv", out_dtype=BF16, **big)
    gl = _mm(h1, w["gate"], mode="nn", name="proj_gate", **big)
    fl = _mm(h1, w["forget"], mode="nn", name="proj_forget", **big)
    c_wide, c_row = _fox_prep(fl, vec["b_forget"], batch, seq)
    (o_fox, lse_wide, o_sb, rt_wide), gathered = _att_fwd(qkv, c_wide, c_row, batch, seq, rest)
    w = dict(w, **_rest_weights(dict(zip(EARLY + ("b_gate",), gathered))))
    merged, of, os_ = _gate_fwd(gl, w["b_gate"], o_fox, o_sb, w["branch_fox"], w["branch_sb"])
    x1, h2 = _mm_res_norm(merged, w["out"], x, vec["g_mlp"], "proj_out_norm")
    ar = _mm(h2, w["up"], mode="nn", name="mlp_up", out_dtype=BF16, epi=lambda acc, _: jnp.maximum(acc, 0.0),
             col_shards=True, **big)
    x2, h3 = _mm_res_norm(ar, w["down"], x1, vec["g_ple"], "mlp_down_norm", a_fn=_relu2)

    dx3, dpre, dpe, dg_final, loss = _head_and_loss(x2, h3, p, w["ple_gate"], w["ple"], vec["g_final"], target)
    gw = {}
    gw["ple"] = _mm(p, dpe, mode="tn", name="d_w_ple", col_shards=True, **big)
    gw["ple_gate"] = _mm(h3, dpre, mode="tn", name="d_w_ple_gate", **big)
    dx2, dx2b, dg_ple = _mm_norm_bwd([(dpre, w["ple_gate"])], None, x2, vec["g_ple"], dx3, "d_h_ple_norm_bwd")
    gw["down"] = _mm(ar, dx2b, mode="tn", name="d_w_down", a_fn=_relu2, **big)
    da = _mm(dx2b, w["down"], mode="nt", name="d_act", out_dtype=BF16,
             epi=lambda acc, r: acc * (2.0 * r.astype(F32)), extra=ar, **big)
    gw["up"] = _mm(h2, da, mode="tn", name="d_w_up", col_shards=True, **big)
    dx1, dx1b, dg_mlp = _mm_norm_bwd([(da, w["up"])], None, x1, vec["g_mlp"], dx2, "d_h_mlp_norm_bwd")
    gw["out"] = _mm(merged, dx1b, mode="tn", name="d_w_out", **big)
    dof, dos, dgl, gw["b_gate"], do_fox, do_sb = _gate_bwd(gl, w["b_gate"], of, os_, dx1b, w["out"], w["branch_fox"],
                                                                  w["branch_sb"])
    gw["branch_fox"] = _mm(o_fox, dof, mode="tn", name="d_w_branch_fox", col_shards=True, **big)
    gw["branch_sb"] = _mm(o_sb, dos, mode="tn", name="d_w_branch_sb", col_shards=True, **big)
    early = _early_slots(gw)
    early = [early[n] for n in EARLY]
    (dq_a, dk_a, dv_a, dcs_wide, drs_wide), received = _fox_bwd(qkv, c_wide, c_row, o_fox, do_fox, lse_wide, batch, seq,
                                                                _swap_halves(early))
    sums = _sum_sibling(place, early, received, "sum_sibling_early")
    (dq_b, dk_b, dv_b), others = _sb_bwd(qkv, do_sb, rt_wide, batch, seq, _exchange_chips(sums))
    mine = [None] * len(EARLY)
    for group, tag in ((BIG, "big"), (SMALL, "small")):
        for t, res in zip(group, _sum_chips(place, *[[a[t] for t in group] for a in (early, received, others)], "sum_chips_" + tag)):
            mine[t] = res
    dfl, db_forget = _fox_post(dcs_wide, drs_wide, fl, vec["b_forget"], batch, seq)
    dqkv = jnp.concatenate([dq_a, dk_a, dv_a, dq_b, dk_b, dv_b], axis=1)
    gw["qkv"], theirs = _mm(dqkv, h1, mode="tn", name="d_w_qkv", behind=_share_halves(mine), flat_out=True, **big)
    reduced = dict(zip(EARLY, zip(mine, theirs)))
    gw["gate"] = _mm(dgl, h1, mode="tn", name="d_w_gate", flat_out=True, **big)
    gw["forget"] = _mm(dfl, h1, mode="tn", name="d_w_forget", flat_out=True, **big)
    late = [_w_in_slots(gw)]
    dh1, received = _mm(dqkv, w["qkv"], mode="nt", name="d_h_qkv", behind=_swap_halves(late), **big)
    sums = _sum_sibling(place, late, received, "sum_sibling_w_in")
    (grad_x, _, dg_mix), others = _mm_norm_bwd([(dgl, w["gate"]), (dfl, w["forget"])], dh1, x, vec["g_mix"], dx1,
                                               "d_h_gate_norm_bwd", behind=_exchange_chips(sums))
    mine = _sum_chips(place, late, received, others, "sum_chips_w_in")
    reduced["w_in"] = (mine[0], _run_exchange(_share_halves(mine), "reduce_share_w_in")[0])
    gvec = {"g_mix": dg_mix, "b_forget": db_forget[:, 0:N_HEADS], "g_mlp": dg_mlp, "g_ple": dg_ple,
            "g_final": dg_final, "b_gate": gw["b_gate"]}
    return loss, grad_x.reshape(batch, seq, D_MODEL), reduced, gvec


ANY = pl.BlockSpec(memory_space=pl.ANY)
SHARDED = ("w_in", "w_branch_fox", "w_branch_sb", "w_out", "w_up", "w_down", "w_ple_gate", "w_ple")
ROW_ALIGN = 16
F32_ROWS = 8


def _place():
    return lax.axis_index("x"), lax.axis_index("y"), lax.axis_index("c")


def _other_chips(x, y):
    return [(1 - x, y), (x, 1 - y), (1 - x, 1 - y)]


def _half(ref, h):
    r = ref.shape[0] // 2
    assert r % ROW_ALIGN == 0
    return ref.at[pl.ds(pl.multiple_of(h * r, ROW_ALIGN), r)]


def _remote(src, dst, sems, idx, to):
    send_sems, recv_sems = sems
    return pltpu.make_async_remote_copy(src_ref=src, dst_ref=dst, send_sem=send_sems.at[idx], recv_sem=recv_sems.at[idx],
                                        device_id=to, device_id_type=MESH)


class _Exchange:
    def __init__(self, operands, out_shapes, sem_shape, start, finish):
        self.operands, self.out_shapes, self.sem_shape, self.start, self.finish = operands, out_shapes, sem_shape, start, finish

    def scratch(self):
        return [pltpu.SemaphoreType.DMA(self.sem_shape), pltpu.SemaphoreType.DMA(self.sem_shape)]


def _run_exchange(ex, name):
    n = len(ex.operands)

    def body(*refs):
        ex.start(refs[:n], refs[n:2 * n], refs[2 * n:])
        ex.finish(refs[:n], refs[n:2 * n], refs[2 * n:])

    return pl.pallas_call(body, name=name, in_specs=[ANY] * n, out_specs=[ANY] * n, out_shape=ex.out_shapes,
                          scratch_shapes=ex.scratch())(*ex.operands)


def _call_behind(body, ex, *, name, grid, in_specs, out_specs, out_shape, scratch_shapes, operands):
    n_in, n_out, nx = len(in_specs), len(out_specs), len(ex.operands)

    def wrapped(*refs):
        ins, x_in = refs[:n_in], refs[n_in:n_in + nx]
        outs, x_out = refs[n_in + nx:n_in + nx + n_out], refs[n_in + nx + n_out:n_in + 2 * nx + n_out]
        scratch, sems = refs[n_in + 2 * nx + n_out:-2], refs[-2:]
        first, last = None, None
        for d, steps in enumerate(grid):
            at_start, at_end = pl.program_id(d) == 0, pl.program_id(d) == steps - 1
            first = at_start if first is None else first & at_start
            last = at_end if last is None else last & at_end

        @pl.when(first)
        def _():
            ex.start(x_in, x_out, sems)

        body(*ins, *outs, *scratch)

        @pl.when(last)
        def _():
            ex.finish(x_in, x_out, sems)

    res = pl.pallas_call(
        wrapped, name=name, grid=grid, in_specs=list(in_specs) + [ANY] * nx, out_specs=list(out_specs) + [ANY] * nx,
        out_shape=list(out_shape) + list(ex.out_shapes), scratch_shapes=list(scratch_shapes) + ex.scratch(),
        compiler_params=_cparams(("arbitrary",) * len(grid)),
    )(*operands, *ex.operands)
    return res[:n_out], res[n_out:]


def _gather_weights(shards):
    n = len(shards)

    def first_copies(src, out, sems):
        x, y, c = _place()
        me = 2 * x + y
        copies = [_remote(_half(src[t], c), _half(out[t].at[me], c), sems, (t, k), (px, py, c))
                  for t in range(n) for k, (px, py) in enumerate(_other_chips(x, y))]
        return copies + [_remote(src[t], out[t].at[me], sems, (t, 3), (x, y, 1 - c)) for t in range(n)]

    def start(src, out, sems):
        for cp in first_copies(src, out, sems):
            cp.start()

    def finish(src, out, sems):
        x, y, c = _place()
        me = 2 * x + y
        sibling = (x, y, 1 - c)
        chips = _other_chips(x, y)
        passes = []
        for t in range(n):
            for k, (px, py) in enumerate(chips):
                landed = _half(out[t].at[2 * px + py], c)
                _remote(landed, landed, sems, (t, k), (px, py, c)).wait_recv()
                passes.append(_remote(landed, landed, sems, (t, 4 + k), sibling))
                passes[-1].start()
        for t in range(n):
            _remote(src[t], out[t].at[me], sems, (t, 3), sibling).wait_recv()
            for k, (px, py) in enumerate(chips):
                passed = _half(out[t].at[2 * px + py], 1 - c)
                _remote(passed, passed, sems, (t, 4 + k), sibling).wait_recv()
        for cp in first_copies(src, out, sems) + passes:
            cp.wait_send()

    return _Exchange(shards, [jax.ShapeDtypeStruct((N_CHIPS,) + s.shape, s.dtype) for s in shards], (n, 7), start, finish)


def _simple_exchange(operands, out_shapes, copies):
    def start(src, out, sems):
        for cp in copies(src, out, sems):
            cp.start()

    def finish(src, out, sems):
        for cp in copies(src, out, sems):
            cp.wait_recv()
        for cp in copies(src, out, sems):
            cp.wait_send()

    return _Exchange(operands, out_shapes, (len(operands),), start, finish)


def _swap_halves(slots):
    def copies(src, out, sems):
        x, y, c = _place()
        res = []
        for t in range(len(slots)):
            r = src[t].shape[1] // 2
            rows = pl.ds(pl.multiple_of((1 - c) * r, F32_ROWS), r)
            res.append(_remote(src[t].at[:, rows], out[t], sems, t, (x, y, 1 - c)))
        return res

    return _simple_exchange(slots, [jax.ShapeDtypeStruct((N_CHIPS, s.shape[1] // 2, s.shape[2]), s.dtype) for s in slots], copies)


def _exchange_chips(sums):
    n = len(sums)

    def copies(src, out, sems):
        x, y, c = _place()
        return [_remote(src[t].at[2 * px + py], out[t].at[k], sems, (t, k), (px, py, c))
                for t in range(n) for k, (px, py) in enumerate(_other_chips(x, y))]

    def start(src, out, sems):
        for cp in copies(src, out, sems):
            cp.start()

    def finish(src, out, sems):
        for cp in copies(src, out, sems):
            cp.wait_recv()
        for cp in copies(src, out, sems):
            cp.wait_send()

    return _Exchange(sums, [jax.ShapeDtypeStruct((3,) + s.shape[1:], s.dtype) for s in sums], (n, 3), start, finish)


def _share_halves(mine):
    def copies(src, out, sems):
        x, y, c = _place()
        return [_remote(src[t], out[t], sems, t, (x, y, 1 - c)) for t in range(len(mine))]

    return _simple_exchange(mine, [jax.ShapeDtypeStruct(s.shape, s.dtype) for s in mine], copies)


def _walk(name, place, parts):
    starts = [sum(p[0] for p in parts[:t]) for t in range(len(parts))]
    held = lambda index, start, steps: (lambda s, pr: index(jnp.clip(s - start, 0, steps - 1), pr))
    in_specs, out_specs, out_shapes, operands = [], [], [], []
    for (steps, ins, outs, shapes, ops, _), start in zip(parts, starts):
        in_specs += [pl.BlockSpec(blk, held(index, start, steps)) for blk, index in ins]
        out_specs += [pl.BlockSpec(blk, held(index, start, steps)) for blk, index in outs]
        out_shapes += list(shapes)
        operands += list(ops)

    def body(place_ref, *refs):
        s = pl.program_id(0)
        i, o = 0, len(in_specs)
        for (steps, ins, outs, _, _, fn), start in zip(parts, starts):
            mine_in, mine_out = refs[i:i + len(ins)], refs[o:o + len(outs)]
            i, o = i + len(ins), o + len(outs)

            @pl.when((s >= start) & (s < start + steps))
            def _(mine_in=mine_in, mine_out=mine_out, start=start, fn=fn):
                fn(s - start, mine_in, mine_out, place_ref)

    res = pl.pallas_call(
        body, name=name, out_shape=out_shapes,
        grid_spec=pltpu.PrefetchScalarGridSpec(num_scalar_prefetch=1, grid=(sum(p[0] for p in parts),), in_specs=in_specs,
                                               out_specs=out_specs),
        compiler_params=_cparams(("arbitrary",)),
    )(place, *operands)
    counts = [len(p[2]) for p in parts]
    return [res[sum(counts[:t]):sum(counts[:t + 1])] for t in range(len(parts))]


WALK_BLOCK = D_MODEL * D_MODEL // 2


def _sum_sibling(place, slots, received, name):
    def part(slot, got):
        n, rows, cols = got.shape
        whole = n * rows * cols <= WALK_BLOCK
        block = (n if whole else None, rows, cols)

        def fn(j, ins, outs, pr):
            outs[0][...] = (ins[0][...] + ins[1][...]).astype(BF16)

        return (1 if whole else n, [(block, lambda j, pr: (j, pr[1], 0)), (block, lambda j, pr: (j, 0, 0))], [(block, lambda j, pr: (j, 0, 0))],
                [jax.ShapeDtypeStruct(got.shape, BF16)], [slot, got], fn)

    return [r[0] for r in _walk(name, place, [part(s, g) for s, g in zip(slots, received)])]


def _sum_chips(place, slots, received, others, name):
    def part(slot, got, other):
        _, rows, cols = got.shape
        block = (None, rows, cols)

        def fn(_, ins, outs, pr):
            own = ins[0][...] + ins[1][...]
            outs[0][...] = ((own + ins[2][0].astype(F32)) + ins[2][1].astype(F32)) + ins[2][2].astype(F32)

        return (1, [(block, lambda _, pr: (pr[0], pr[1], 0)), (block, lambda _, pr: (pr[0], 0, 0)),
                    ((3, rows, cols), lambda _, pr: (0, 0, 0))], [((rows, cols), lambda _, pr: (0, 0))],
                [jax.ShapeDtypeStruct((rows, cols), F32)], [slot, got, other], fn)

    return [r[0] for r in _walk(name, place, [part(*t) for t in zip(slots, received, others)])]


N_DEVICES = 8


def _sum_devices(block, name):
    def body(v_ref, o_ref, land_ref, send_sems, recv_sems):
        x, y, c = _place()
        me = 4 * x + 2 * y + c
        copies = []
        for mask in range(1, N_DEVICES):
            peer = (x ^ (mask >> 2), y ^ ((mask >> 1) & 1), c ^ (mask & 1))
            copies.append(pltpu.make_async_remote_copy(src_ref=v_ref, dst_ref=land_ref.at[me], send_sem=send_sems.at[mask - 1],
                                                       recv_sem=recv_sems.at[mask - 1], device_id=peer, device_id_type=MESH))
        for cp in copies:
            cp.start()
        land_ref[me] = v_ref[...]
        for cp in copies:
            cp.wait_recv()
        total = land_ref[0]
        for d in range(1, N_DEVICES):
            total = total + land_ref[d]
        o_ref[...] = total
        for cp in copies:
            cp.wait_send()

    vmem = pl.BlockSpec(memory_space=pltpu.VMEM)
    return pl.pallas_call(
        body, name=name, in_specs=[vmem], out_specs=vmem, out_shape=jax.ShapeDtypeStruct(block.shape, F32),
        scratch_shapes=[pltpu.VMEM((N_DEVICES,) + block.shape, F32), pltpu.SemaphoreType.DMA((N_DEVICES - 1,)),
                        pltpu.SemaphoreType.DMA((N_DEVICES - 1,))],
    )(block)


def _vec_block(g_mix, g_mlp, g_ple, g_final, b_forget, b_gate_rows, last=None):
    pad = lambda a: jnp.concatenate([a, jnp.zeros((a.shape[0], D_MODEL - a.shape[1]), F32)], axis=1)
    last = jnp.zeros((1, 0), F32) if last is None else last
    return jnp.concatenate([g_mix, g_mlp, g_ple, g_final.reshape(1, D_MODEL), pad(b_forget), pad(b_gate_rows), pad(last)],
                           axis=0)


def _adam_math(w, g, m, v):
    m_new = ADAM_B1 * m + (1.0 - ADAM_B1) * g
    v_new = ADAM_B2 * v + (1.0 - ADAM_B2) * (g * g)
    m_hat = m_new / (1.0 - ADAM_B1 ** ADAM_STEP)
    v_hat = v_new / (1.0 - ADAM_B2 ** ADAM_STEP)
    return -ADAM_LR * (m_hat / (jnp.sqrt(v_hat) + ADAM_EPS) + ADAM_WD * w), m_new, v_new


def _adamw_halves(place, weights, name):
    def part(w, m, v, g_mine, g_theirs):
        rows, cols = g_mine.shape
        whole = ((rows, cols), lambda s, pr: (pr[1] + s - 2 * pr[1] * s, 0))
        half = ((rows, cols), lambda s, pr: (0, 0))

        def fn(s, ins, outs, pr):
            g = jnp.where(s == 0, ins[3][...], ins[4][...])
            outs[0][...] = g
            outs[1][...], outs[2][...], outs[3][...] = _adam_math(ins[0][...], g, ins[1][...], ins[2][...])

        def both(s, ins, outs, pr):
            mine, theirs = ins[3][...], ins[4][...]
            g = jnp.where(pr[1] == 0, jnp.concatenate([mine, theirs], axis=0), jnp.concatenate([theirs, mine], axis=0))
            outs[0][...] = g
            outs[1][...], outs[2][...], outs[3][...] = _adam_math(ins[0][...], g, ins[1][...], ins[2][...])

        shapes, operands = [jax.ShapeDtypeStruct(w.shape, F32)] * 4, [w, m, v, g_mine, g_theirs]
        if 2 * rows * cols <= WALK_BLOCK:
            shard = ((2 * rows, cols), lambda s, pr: (0, 0))
            return (1, [shard] * 3 + [half] * 2, [shard] * 4, shapes, operands, both)
        return (2, [whole] * 3 + [half] * 2, [whole] * 4, shapes, operands, fn)

    return _walk(name, place, [part(*t) for t in weights])


def _adamw_vec(w, g, m, v):
    def body(w_ref, g_ref, m_ref, v_ref, d_ref, nm_ref, nv_ref):
        d_ref[...], nm_ref[...], nv_ref[...] = _adam_math(w_ref[...], g_ref[...], m_ref[...], v_ref[...])

    return pl.pallas_call(body, name="adamw_vectors", out_shape=[jax.ShapeDtypeStruct(w.shape, F32)] * 3)(w, g, m, v)


WEIGHT_NAMES = ("g_mix", "w_in", "b_forget", "b_gate", "w_branch_fox", "w_branch_sb", "w_out", "g_mlp", "w_up", "w_down",
                "g_ple", "w_ple_gate", "w_ple", "g_final")
W_IN_SHARD = D_IN // N_CHIPS
Q_END, F_END, B_END = 3 * D_ATT, 3 * D_ATT + N_HEADS, 6 * D_ATT + N_HEADS
GATE_SHARD = D_MODEL // N_CHIPS


LATE = SHARDED[:1]
EARLY = SHARDED[1:]
BIG = tuple(t for t, n in enumerate(EARLY) if n in ("w_up", "w_down"))
SMALL = tuple(t for t in range(len(EARLY)) if t not in BIG)


def _first_weights(w_in_slots):
    def cols(*ranges):
        parts = []
        for lo, hi in ranges:
            for j in range(N_CHIPS):
                a, b = max(lo, j * W_IN_SHARD), min(hi, (j + 1) * W_IN_SHARD)
                if a < b:
                    parts.append(w_in_slots[j, :, a - j * W_IN_SHARD:b - j * W_IN_SHARD])
        return parts

    forget = jnp.concatenate(cols((Q_END, F_END)) + [jnp.zeros((D_MODEL, F_PAD - N_HEADS), BF16)], axis=1)
    return {"qkv": jnp.concatenate(cols((0, Q_END), (F_END, B_END)), axis=1), "gate": jnp.concatenate(cols((B_END, D_IN)), axis=1),
            "forget": forget}


GATE_ROWS = 2 * ROW_ALIGN


def _gate_bits(b_gate):
    bits = lax.bitcast_convert_type(b_gate, BF16).reshape(2, 2 * GATE_SHARD)
    return jnp.concatenate([bits, jnp.zeros((GATE_ROWS - 2, 2 * GATE_SHARD), BF16)], axis=0)


def _rest_weights(gathered):
    rows = lambda a: a.reshape(N_CHIPS * a.shape[1], a.shape[2])
    bits = gathered["b_gate"][:, :2].reshape(N_CHIPS, 2, GATE_SHARD, 2)
    b_gate = jnp.transpose(lax.bitcast_convert_type(bits, F32), (1, 0, 2)).reshape(2, D_MODEL)
    return {"branch_fox": gathered["w_branch_fox"], "branch_sb": gathered["w_branch_sb"], "out": rows(gathered["w_out"]),
            "up": gathered["w_up"], "down": rows(gathered["w_down"]), "ple_gate": rows(gathered["w_ple_gate"]),
            "ple": gathered["w_ple"], "b_gate": b_gate}


def _early_slots(gw):
    rows = lambda a: a.reshape(N_CHIPS, a.shape[0] // N_CHIPS, a.shape[1])
    return {"w_branch_fox": gw["branch_fox"], "w_branch_sb": gw["branch_sb"], "w_out": rows(gw["out"]), "w_up": gw["up"],
            "w_down": rows(gw["down"]), "w_ple_gate": rows(gw["ple_gate"]), "w_ple": gw["ple"]}


W_IN_FLAT = (W_IN_SHARD * D_MODEL // LANES, LANES)


def _w_in_slots(gw):
    c = D_MODEL // LANES
    g_t = jnp.concatenate([gw["qkv"][:Q_END * c], gw["forget"][:N_HEADS * c], gw["qkv"][Q_END * c:], gw["gate"]], axis=0)
    return g_t.reshape((N_CHIPS,) + W_IN_FLAT)


def _flat(a):
    return jnp.transpose(a, (2, 0, 1)).reshape(W_IN_FLAT)


def _unflat(a):
    return jnp.transpose(a.reshape(W_IN_SHARD, D_MODEL // LANES, LANES), (1, 2, 0)).reshape(1, D_MODEL, W_IN_SHARD)


def kernel(x, p, g_mix, w_in, b_forget, b_gate, w_branch_fox, w_branch_sb, w_out, g_mlp, w_up, w_down, g_ple, w_ple_gate, w_ple, g_final, loss_target, m_g_mix, m_w_in, m_b_forget, m_b_gate, m_w_branch_fox, m_w_branch_sb, m_w_out, m_g_mlp, m_w_up, m_w_down, m_g_ple, m_w_ple_gate, m_w_ple, m_g_final, v_g_mix, v_w_in, v_b_forget, v_b_gate, v_w_branch_fox, v_w_branch_sb, v_w_out, v_g_mlp, v_w_up, v_w_down, v_g_ple, v_w_ple_gate, v_w_ple, v_g_final):
    weights = dict(g_mix=g_mix, w_in=w_in, b_forget=b_forget, b_gate=b_gate, w_branch_fox=w_branch_fox,
                   w_branch_sb=w_branch_sb, w_out=w_out, g_mlp=g_mlp, w_up=w_up, w_down=w_down, g_ple=g_ple,
                   w_ple_gate=w_ple_gate, w_ple=w_ple, g_final=g_final)
    first = dict(g_mix=m_g_mix, w_in=m_w_in, b_forget=m_b_forget, b_gate=m_b_gate, w_branch_fox=m_w_branch_fox,
                 w_branch_sb=m_w_branch_sb, w_out=m_w_out, g_mlp=m_g_mlp, w_up=m_w_up, w_down=m_w_down, g_ple=m_g_ple,
                 w_ple_gate=m_w_ple_gate, w_ple=m_w_ple, g_final=m_g_final)
    second = dict(g_mix=v_g_mix, w_in=v_w_in, b_forget=v_b_forget, b_gate=v_b_gate, w_branch_fox=v_w_branch_fox,
                  w_branch_sb=v_w_branch_sb, w_out=v_w_out, g_mlp=v_g_mlp, w_up=v_w_up, w_down=v_w_down, g_ple=v_g_ple,
                  w_ple_gate=v_w_ple_gate, w_ple=v_w_ple, g_final=v_g_final)
    cx, cy, cc = _place()
    chip = 2 * cx + cy
    place = jnp.stack([chip, cc]).astype(jnp.int32)
    col0 = chip * GATE_SHARD

    first_gather = _gather_weights([weights[n][0].astype(BF16) for n in LATE])
    rest = _gather_weights([weights[n][0].astype(BF16) for n in EARLY] + [_gate_bits(b_gate[0])])
    vec = {"g_mix": g_mix, "b_forget": jnp.concatenate([b_forget, jnp.zeros((1, F_PAD - N_HEADS), F32)], axis=1),
           "g_mlp": g_mlp, "g_ple": g_ple, "g_final": g_final.reshape(1, D_MODEL)}

    loss, grad_x, reduced, gvec = _local_step(x, p[0], loss_target, first_gather, rest, vec, place)

    out = {}
    args = lambda n: (weights[n][0], first[n][0], second[n][0]) + tuple(reduced[n])
    for names, tag in [([EARLY[t] for t in SMALL], "small")] + [([EARLY[t]], EARLY[t]) for t in BIG]:
        for n, res in zip(names, _adamw_halves(place, [args(n) for n in names], "adamw_" + tag)):
            out[n] = [r[None] for r in res]
    (res,) = _adamw_halves(place, [(_flat(w_in), _flat(m_w_in), _flat(v_w_in)) + tuple(reduced["w_in"])], "adamw_w_in")
    out["w_in"] = [_unflat(r) for r in res]

    g_block = _sum_devices(_vec_block(gvec["g_mix"], gvec["g_mlp"], gvec["g_ple"], gvec["g_final"][0], gvec["b_forget"],
                                      gvec["b_gate"], loss), "reduce_vectors")
    loss = g_block[7, 0]
    g_gate = lax.dynamic_slice(g_block[5:7], (0, col0), (2, GATE_SHARD))
    blocks = [_vec_block(d["g_mix"], d["g_mlp"], d["g_ple"], d["g_final"], d["b_forget"], d["b_gate"][0])
              for d in (weights, first, second)]
    g_rows = jnp.concatenate([g_block[0:5], jnp.concatenate([g_gate, jnp.zeros((2, D_MODEL - GATE_SHARD), F32)], axis=1),
                              jnp.zeros((1, D_MODEL), F32)], axis=0)
    res = (g_rows,) + tuple(_adamw_vec(blocks[0], g_rows, blocks[1], blocks[2]))
    out["g_mix"] = [r[0:1] for r in res]
    out["g_mlp"] = [r[1:2] for r in res]
    out["g_ple"] = [r[2:3] for r in res]
    out["g_final"] = [r[3] for r in res]
    out["b_forget"] = [r[4:5, :N_HEADS] for r in res]
    out["b_gate"] = [r[5:7, :GATE_SHARD][None] for r in res]
    return (loss, grad_x, *[out[n][0] for n in WEIGHT_NAMES], *[out[n][1] for n in WEIGHT_NAMES],
            *[out[n][2] for n in WEIGHT_NAMES], *[out[n][3] for n in WEIGHT_NAMES])
```
